```python
import jax, jax.numpy as jnp
from jax import lax
import numpy as np

D_MODEL = 1024
BATCH = 2
SEQ = 8192
DEPTH = 2
DEC_BATCH = 16
DEC_SEQ = 64
PAST_LEN = 4096

CHUNK = 64
HEAD_DIM = 64
A_HEADS = 4
A_KV_HEADS = 2
A_GROUP = A_HEADS // A_KV_HEADS
A_WINDOW = 128
A_PREV_CHUNKS = A_WINDOW // CHUNK
A_Q_W = A_HEADS * HEAD_DIM
A_KV_W = A_KV_HEADS * HEAD_DIM
A_PROJ = A_Q_W + 2 * A_KV_W
B_HEADS = 4
B_WIDTH = B_HEADS * HEAD_DIM
B_DECAY_LORA = 64
B_AAA_LORA = 64
B_GATE_LORA = 128
B_PROJ = 3 * B_WIDTH + B_DECAY_LORA + B_AAA_LORA + B_GATE_LORA
B_GN_EPS = 64e-5
C_HEADS = 4
C_WIDTH = C_HEADS * HEAD_DIM
C_PROJ = 4 * C_WIDTH
C_GN_EPS = 1e-6
ROPE_BASE = 10000.0
D_HEADS = 4
D_WIDTH = D_HEADS * HEAD_DIM
D_PROJ = 3 * D_WIDTH
D_PREV_CHUNKS = 8
D_REL_CLIP = 128
MIX_WIDTH = A_Q_W + B_WIDTH + C_WIDTH + D_WIDTH
IN_PROJ = A_PROJ + B_PROJ + C_PROJ + D_PROJ
D_FF = 2816
N_EXPERTS = 8
TOP_K = 2
E_FF = 3584
N_DENSE = (DEPTH + 1) // 2
N_MOE = DEPTH // 2
NORM_EPS = 1e-6
ATTN_SCALE = HEAD_DIM ** -0.5

kernel_name = 'hybrid_streaming_encoder_step'


def rms_norm(x, g, eps=NORM_EPS):
    xf = x.astype(jnp.float32)
    y = xf * lax.rsqrt(jnp.mean(xf * xf, axis=-1, keepdims=True) + eps)
    return (y * g.astype(jnp.float32)).astype(x.dtype)


def head_layer_norm(x, w, b, eps):
    xf = x.astype(jnp.float32)
    mu = jnp.mean(xf, axis=-1, keepdims=True)
    var = jnp.mean(jnp.square(xf - mu), axis=-1, keepdims=True)
    y = ((xf - mu) * lax.rsqrt(var + eps)).reshape(*x.shape[:-2], -1)
    return y * w.astype(jnp.float32) + b.astype(jnp.float32)


def band_keys(t, n_prev):
    nc = t.shape[1]
    pad = [(0, 0), (n_prev, 0)] + [(0, 0)] * (t.ndim - 2)
    tp = jnp.pad(t, pad)
    band = jnp.stack([tp[:, i:i + nc] for i in range(n_prev + 1)], axis=2)
    return band.reshape(t.shape[0], nc, (n_prev + 1) * t.shape[2], *t.shape[3:])


def sink_softmax(s, sink):
    m = jnp.maximum(jnp.max(s, axis=-1, keepdims=True), sink)
    e = jnp.exp(s - m)
    return e / (jnp.sum(e, axis=-1, keepdims=True) + jnp.exp(sink - m))


def swa_sink_prompt(q, k, v, sinks):
    b, t = q.shape[:2]
    nc = t // CHUNK
    qc = q.reshape(b, nc, CHUNK, A_KV_HEADS, A_GROUP, HEAD_DIM)
    kb = band_keys(k.reshape(b, nc, CHUNK, A_KV_HEADS, HEAD_DIM), A_PREV_CHUNKS)
    vb = band_keys(v.reshape(b, nc, CHUNK, A_KV_HEADS, HEAD_DIM), A_PREV_CHUNKS)
    s = jnp.einsum('bnqhgd,bnkhd->bnhgqk', qc, kb).astype(jnp.float32) * ATTN_SCALE
    kpos = (jnp.arange(nc)[:, None] - A_PREV_CHUNKS) * CHUNK + jnp.arange(kb.shape[2])[None, :]
    s = jnp.where((kpos >= 0)[None, :, None, None, None, :], s, -jnp.inf)
    p = sink_softmax(s, sinks.astype(jnp.float32)[None, None, :, :, None, None])
    o = jnp.einsum('bnhgqk,bnkhd->bnqhgd', p.astype(vb.dtype), vb)
    return o.reshape(b, t, A_Q_W)


def swa_sink_step(q, k, v, k_cache, v_cache, sinks):
    b, l = q.shape[:2]
    w = k_cache.shape[1]
    kc = jnp.concatenate([k_cache.astype(k.dtype), k], axis=1)
    vc = jnp.concatenate([v_cache.astype(v.dtype), v], axis=1)
    s = jnp.einsum('blhgd,bkhd->bhglk', q, kc).astype(jnp.float32) * ATTN_SCALE
    p = sink_softmax(s, sinks.astype(jnp.float32)[None, :, :, None, None])
    o = jnp.einsum('bhglk,bkhd->blhgd', p.astype(vc.dtype), vc)
    return o.reshape(b, l, A_Q_W), kc[:, -w:], vc[:, -w:]


def chunk_relpos_prompt(q, k, v, rel_bias):
    b, t = q.shape[:2]
    nc = t // CHUNK
    nk = (D_PREV_CHUNKS + 1) * CHUNK
    qc = q.reshape(b, nc, CHUNK, D_HEADS, HEAD_DIM)
    kb = band_keys(k.reshape(b, nc, CHUNK, D_HEADS, HEAD_DIM), D_PREV_CHUNKS)
    vb = band_keys(v.reshape(b, nc, CHUNK, D_HEADS, HEAD_DIM), D_PREV_CHUNKS)
    rel = jnp.arange(CHUNK)[:, None] - (jnp.arange(nk)[None, :] - D_PREV_CHUNKS * CHUNK)
    bias = rel_bias.astype(jnp.float32)[:, jnp.clip(rel, -D_REL_CLIP, D_REL_CLIP) + D_REL_CLIP]
    s = jnp.einsum('bnqhd,bnkhd->bnhqk', qc, kb).astype(jnp.float32) * ATTN_SCALE + bias[None, None]
    kpos = (jnp.arange(nc)[:, None] - D_PREV_CHUNKS) * CHUNK + jnp.arange(nk)[None, :]
    s = jnp.where((kpos >= 0)[None, :, None, None, :], s, -jnp.inf)
    p = jax.nn.softmax(s, axis=-1)
    o = jnp.einsum('bnhqk,bnkhd->bnqhd', p.astype(vb.dtype), vb)
    return o.reshape(b, t, D_WIDTH)


def chunk_relpos_step(q, k, v, k_cache, v_cache, rel_bias):
    b, l = q.shape[:2]
    w = k_cache.shape[1]
    kc = jnp.concatenate([k_cache.astype(k.dtype), k], axis=1)
    vc = jnp.concatenate([v_cache.astype(v.dtype), v], axis=1)
    kp = jnp.concatenate([jnp.arange(-w, 0), jnp.arange(l)])
    rel = jnp.arange(l)[:, None] - kp[None, :]
    bias = rel_bias.astype(jnp.float32)[:, jnp.clip(rel, -D_REL_CLIP, D_REL_CLIP) + D_REL_CLIP]
    s = jnp.einsum('blhd,bkhd->bhlk', q, kc).astype(jnp.float32) * ATTN_SCALE + bias[None]
    p = jax.nn.softmax(s, axis=-1)
    o = jnp.einsum('bhlk,bkhd->blhd', p.astype(vc.dtype), vc)
    return o.reshape(b, l, D_WIDTH), kc[:, -w:], vc[:, -w:]


def rwkv7_scan(r, w, k, v, kk, a, s0):
    tm = lambda t: jnp.moveaxis(t, 1, 0)

    def step(s, inp):
        r_t, w_t, k_t, v_t, kk_t, a_t = inp
        sa = jnp.einsum('bhvk,bhk->bhv', s, -kk_t)
        s = s * w_t[:, :, None, :] + sa[..., None] * (kk_t * a_t)[:, :, None, :] + v_t[..., None] * k_t[:, :, None, :]
        return s, jnp.einsum('bhvk,bhk->bhv', s, r_t)

    s, y = lax.scan(step, s0, (tm(r), tm(w), tm(k), tm(v), tm(kk), tm(a)))
    return jnp.moveaxis(y, 0, 1), s


def rwkv7_mix(xb, shift_prev, wkv_prev, mu, w0, w2, a0, a2, g2, k_k, k_a, r_k, ln_w, ln_b):
    b, t, _ = xb.shape
    prev = jnp.concatenate([shift_prev[:, None, :].astype(xb.dtype), xb[:, :-1]], axis=1)
    xs = xb + mu * (prev - xb)
    r = xs[..., :B_WIDTH]
    k = xs[..., B_WIDTH:2 * B_WIDTH]
    v = xs[..., 2 * B_WIDTH:3 * B_WIDTH]
    o = 3 * B_WIDTH
    xw = xs[..., o:o + B_DECAY_LORA]
    xa = xs[..., o + B_DECAY_LORA:o + B_DECAY_LORA + B_AAA_LORA]
    xg = xs[..., o + B_DECAY_LORA + B_AAA_LORA:]
    w_log = -jax.nn.softplus(-(w0 + jnp.tanh(xw) @ w2).astype(jnp.float32)) - 0.5
    decay = jnp.exp(-jnp.exp(w_log))
    a = jax.nn.sigmoid((a0 + xa @ a2).astype(jnp.float32))
    g = (jax.nn.sigmoid(xg) @ g2).astype(jnp.float32)
    heads = lambda z: z.reshape(b, t, B_HEADS, HEAD_DIM)
    kf = k.astype(jnp.float32)
    kk = heads(kf * k_k.astype(jnp.float32))
    kk = kk / jnp.maximum(jnp.sqrt(jnp.sum(kk * kk, axis=-1, keepdims=True)), 1e-12)
    kf = kf * (1.0 + (a - 1.0) * k_a.astype(jnp.float32))
    rh, wh, kh, vh, ah = heads(r.astype(jnp.float32)), heads(decay), heads(kf), heads(v.astype(jnp.float32)), heads(a)
    y, s = rwkv7_scan(rh, wh, kh, vh, kk, ah, wkv_prev.astype(jnp.float32))
    y = head_layer_norm(y, ln_w, ln_b, B_GN_EPS)
    y = y + (jnp.sum(rh * kh * r_k.astype(jnp.float32), axis=-1, keepdims=True) * vh).reshape(b, t, B_WIDTH)
    return (y * g).astype(xb.dtype), xb[:, -1], s.astype(xb.dtype)


def rotate(x, pos):
    half = HEAD_DIM // 2
    theta = 1.0 / (ROPE_BASE ** jnp.linspace(0.0, 1.0, half, dtype=jnp.float32))
    ang = pos.astype(jnp.float32)[:, None] * theta[None, :]
    cos, sin = jnp.cos(ang)[None, :, None, :], jnp.sin(ang)[None, :, None, :]
    x1, x2 = x[..., :half], x[..., half:]
    return jnp.concatenate([x1 * cos - x2 * sin, x1 * sin + x2 * cos], axis=-1)


def retention_chunk(q, k, v, s, log_gamma):
    l = q.shape[1]
    i = jnp.arange(l)
    diff = i[:, None] - i[None, :]
    dmat = jnp.where(diff >= 0, jnp.exp(log_gamma[:, None, None] * jnp.maximum(diff, 0)), 0.0)
    intra = jnp.einsum('bhlm,bmhe->blhe', jnp.einsum('blhd,bmhd->bhlm', q, k) * dmat[None], v)
    inter = jnp.einsum('blhd,bhde->blhe', q, s) * jnp.exp((i + 1)[:, None] * log_gamma[None, :])[None, :, :, None]
    k_dec = k * jnp.exp((l - 1 - i)[:, None] * log_gamma[None, :])[None, :, :, None]
    s_new = jnp.exp(log_gamma * l)[None, :, None, None] * s + jnp.einsum('blhd,blhe->bhde', k_dec, v)
    return intra + inter, s_new


def retention_mix(hc, s_prev, pos, ln_w, ln_b):
    b, t, _ = hc.shape
    heads = lambda z: z.reshape(b, t, C_HEADS, HEAD_DIM).astype(jnp.float32)
    q = rotate(heads(hc[..., :C_WIDTH]), pos)
    k = rotate(heads(hc[..., C_WIDTH:2 * C_WIDTH]), pos) * ATTN_SCALE
    v = heads(hc[..., 2 * C_WIDTH:3 * C_WIDTH])
    g = hc[..., 3 * C_WIDTH:].astype(jnp.float32)
    log_gamma = jnp.log(1.0 - 2.0 ** (-5.0 - jnp.arange(C_HEADS, dtype=jnp.float32)))
    s0 = s_prev.astype(jnp.float32)
    if t <= CHUNK:
        o, s = retention_chunk(q, k, v, s0, log_gamma)
    else:
        nc = t // CHUNK
        to_chunks = lambda z: jnp.moveaxis(z.reshape(b, nc, CHUNK, C_HEADS, HEAD_DIM), 1, 0)

        def step(s_c, inp):
            qq, kq, vq = inp
            o_c, s_c = retention_chunk(qq, kq, vq, s_c, log_gamma)
            return s_c, o_c

        s, o = lax.scan(step, s0, (to_chunks(q), to_chunks(k), to_chunks(v)))
        o = jnp.moveaxis(o, 0, 1).reshape(b, t, C_HEADS, HEAD_DIM)
    o = head_layer_norm(o, ln_w, ln_b, C_GN_EPS) * jax.nn.silu(g)
    return o.astype(hc.dtype), s.astype(hc.dtype)


def swiglu(x, w1, w3, w2):
    return (jax.nn.silu(x @ w1) * (x @ w3)) @ w2


def moe_swiglu(x, router, w1, w3, w2):
    b, t, d = x.shape
    xt = x.reshape(b * t, d)
    logits = (xt @ router).astype(jnp.float32)
    top_v, top_i = lax.top_k(logits, TOP_K)
    gates = jax.nn.softmax(top_v, axis=-1)
    dense_gate = jnp.sum(jax.nn.one_hot(top_i, N_EXPERTS, dtype=jnp.float32) * gates[..., None], axis=1).astype(x.dtype)
    y = jnp.zeros_like(xt)
    for e in range(N_EXPERTS):
        y = y + dense_gate[:, e:e + 1] * swiglu(xt, w1[e], w3[e], w2[e])
    return y.reshape(b, t, d)


def mixer_sublayer(x, lp, state):
    b, t, _ = x.shape
    h = rms_norm(x, lp['norm1_g']) @ lp['w_in']
    ha = h[..., :A_PROJ]
    hb = h[..., A_PROJ:A_PROJ + B_PROJ]
    hc = h[..., A_PROJ + B_PROJ:A_PROJ + B_PROJ + C_PROJ]
    hd = h[..., A_PROJ + B_PROJ + C_PROJ:]
    aq = rms_norm(ha[..., :A_Q_W].reshape(b, t, A_KV_HEADS, A_GROUP, HEAD_DIM), lp['a_q_norm'])
    ak = rms_norm(ha[..., A_Q_W:A_Q_W + A_KV_W].reshape(b, t, A_KV_HEADS, HEAD_DIM), lp['a_k_norm'])
    av = ha[..., A_Q_W + A_KV_W:].reshape(b, t, A_KV_HEADS, HEAD_DIM)
    sinks = lp['a_sinks'].reshape(A_KV_HEADS, A_GROUP)
    dq = rms_norm(hd[..., :D_WIDTH].reshape(b, t, D_HEADS, HEAD_DIM), lp['d_q_norm'])
    dk = rms_norm(hd[..., D_WIDTH:2 * D_WIDTH].reshape(b, t, D_HEADS, HEAD_DIM), lp['d_k_norm'])
    dv = hd[..., 2 * D_WIDTH:].reshape(b, t, D_HEADS, HEAD_DIM)
    rw = (lp['b_mu'], lp['b_w0'], lp['b_w2'], lp['b_a0'], lp['b_a2'], lp['b_g2'], lp['b_k_k'], lp['b_k_a'], lp['b_r_k'], lp['b_ln_w'], lp['b_ln_b'])
    if state is None:
        oa = swa_sink_prompt(aq, ak, av, sinks)
        wa = min(A_WINDOW, t)
        na_k, na_v = ak[:, -wa:], av[:, -wa:]
        ob, nb_shift, nb_wkv = rwkv7_mix(hb, jnp.zeros((b, B_PROJ), x.dtype), jnp.zeros((b, B_HEADS, HEAD_DIM, HEAD_DIM), jnp.float32), *rw)
        oc, nc_state = retention_mix(hc, jnp.zeros((b, C_HEADS, HEAD_DIM, HEAD_DIM), jnp.float32), jnp.arange(t), lp['c_ln_w'], lp['c_ln_b'])
        od = chunk_relpos_prompt(dq, dk, dv, lp['d_rel_bias'])
        wd = min(D_PREV_CHUNKS * CHUNK, t)
        nd_k, nd_v = dk[:, -wd:], dv[:, -wd:]
    else:
        ca_k, ca_v, sb_shift, sb_wkv, sc, cd_k, cd_v = state
        oa, na_k, na_v = swa_sink_step(aq, ak, av, ca_k, ca_v, sinks)
        ob, nb_shift, nb_wkv = rwkv7_mix(hb, sb_shift, sb_wkv, *rw)
        oc, nc_state = retention_mix(hc, sc, PAST_LEN + jnp.arange(t), lp['c_ln_w'], lp['c_ln_b'])
        od, nd_k, nd_v = chunk_relpos_step(dq, dk, dv, cd_k, cd_v, lp['d_rel_bias'])
    y = jnp.concatenate([oa, ob, oc, od], axis=-1) @ lp['w_out']
    return x + y, (na_k, na_v, nb_shift, nb_wkv, nc_state, nd_k, nd_v)


def setup_inputs(seed: int = 0) -> dict:
    key = jax.random.key(seed)
    ks = iter(jax.random.split(key, 48))
    nrm = lambda shape, scale: jax.random.normal(next(ks), shape, jnp.float32) * scale
    gain = lambda shape: 1.0 + nrm(shape, 0.05)
    a_cache = min(A_WINDOW, PAST_LEN)
    d_cache = min(D_PREV_CHUNKS * CHUNK, PAST_LEN)
    return {
        'x_prompt': nrm((BATCH, SEQ, D_MODEL), 1.0),
        'x_sample': nrm((DEC_BATCH, DEC_SEQ, D_MODEL), 1.0),
        'cache_a_k': nrm((DEPTH, DEC_BATCH, a_cache, A_KV_HEADS, HEAD_DIM), 1.0),
        'cache_a_v': nrm((DEPTH, DEC_BATCH, a_cache, A_KV_HEADS, HEAD_DIM), 1.0),
        'state_b_shift': nrm((DEPTH, DEC_BATCH, B_PROJ), 1.0),
        'state_b_wkv': nrm((DEPTH, DEC_BATCH, B_HEADS, HEAD_DIM, HEAD_DIM), 0.3),
        'state_c': nrm((DEPTH, DEC_BATCH, C_HEADS, HEAD_DIM, HEAD_DIM), 0.3),
        'cache_d_k': nrm((DEPTH, DEC_BATCH, d_cache, D_HEADS, HEAD_DIM), 1.0),
        'cache_d_v': nrm((DEPTH, DEC_BATCH, d_cache, D_HEADS, HEAD_DIM), 1.0),
        'norm1_g': gain((DEPTH, D_MODEL)),
        'norm2_g': gain((DEPTH, D_MODEL)),
        'w_in': nrm((DEPTH, D_MODEL, IN_PROJ), D_MODEL ** -0.5),
        'w_out': nrm((DEPTH, MIX_WIDTH, D_MODEL), MIX_WIDTH ** -0.5),
        'a_q_norm': gain((DEPTH, HEAD_DIM)),
        'a_k_norm': gain((DEPTH, HEAD_DIM)),
        'a_sinks': nrm((DEPTH, A_HEADS), 0.5),
        'b_mu': jax.random.uniform(next(ks), (DEPTH, B_PROJ), jnp.float32, 0.0, 1.0),
        'b_w0': jnp.linspace(-6.0, -1.0, B_WIDTH, dtype=jnp.float32)[None, :] + nrm((DEPTH, B_WIDTH), 0.1),
        'b_w2': nrm((DEPTH, B_DECAY_LORA, B_WIDTH), 0.1),
        'b_a0': nrm((DEPTH, B_WIDTH), 0.1),
        'b_a2': nrm((DEPTH, B_AAA_LORA, B_WIDTH), B_AAA_LORA ** -0.5),
        'b_g2': nrm((DEPTH, B_GATE_LORA, B_WIDTH), B_GATE_LORA ** -0.5),
        'b_k_k': 0.85 + nrm((DEPTH, B_WIDTH), 0.05),
        'b_k_a': 1.0 + nrm((DEPTH, B_WIDTH), 0.05),
        'b_r_k': nrm((DEPTH, B_HEADS, HEAD_DIM), 0.1),
        'b_ln_w': gain((DEPTH, B_WIDTH)),
        'b_ln_b': nrm((DEPTH, B_WIDTH), 0.02),
        'c_ln_w': gain((DEPTH, C_WIDTH)),
        'c_ln_b': nrm((DEPTH, C_WIDTH), 0.02),
        'd_q_norm': gain((DEPTH, HEAD_DIM)),
        'd_k_norm': gain((DEPTH, HEAD_DIM)),
        'd_rel_bias': nrm((DEPTH, D_HEADS, 2 * D_REL_CLIP + 1), 0.2),
        'ffn_w1': nrm((N_DENSE, D_MODEL, D_FF), D_MODEL ** -0.5),
        'ffn_w3': nrm((N_DENSE, D_MODEL, D_FF), D_MODEL ** -0.5),
        'ffn_w2': nrm((N_DENSE, D_FF, D_MODEL), D_FF ** -0.5),
        'moe_router': nrm((N_MOE, D_MODEL, N_EXPERTS), D_MODEL ** -0.5),
        'moe_w1': nrm((N_MOE, N_EXPERTS, D_MODEL, E_FF), D_MODEL ** -0.5),
        'moe_w3': nrm((N_MOE, N_EXPERTS, D_MODEL, E_FF), D_MODEL ** -0.5),
        'moe_w2': nrm((N_MOE, N_EXPERTS, E_FF, D_MODEL), E_FF ** -0.5),
    }


def reference(x_prompt, x_sample, cache_a_k, cache_a_v, state_b_shift, state_b_wkv, state_c, cache_d_k, cache_d_v,
              norm1_g, norm2_g, w_in, w_out, a_q_norm, a_k_norm, a_sinks, b_mu, b_w0, b_w2, b_a0, b_a2, b_g2,
              b_k_k, b_k_a, b_r_k, b_ln_w, b_ln_b, c_ln_w, c_ln_b, d_q_norm, d_k_norm, d_rel_bias,
              ffn_w1, ffn_w3, ffn_w2, moe_router, moe_w1, moe_w3, moe_w2):
    yp, ys = x_prompt, x_sample
    p_states, s_states = [], []
    for l in range(DEPTH):
        lp = {'norm1_g': norm1_g[l], 'w_in': w_in[l], 'w_out': w_out[l], 'a_q_norm': a_q_norm[l], 'a_k_norm': a_k_norm[l],
              'a_sinks': a_sinks[l], 'b_mu': b_mu[l], 'b_w0': b_w0[l], 'b_w2': b_w2[l], 'b_a0': b_a0[l], 'b_a2': b_a2[l],
              'b_g2': b_g2[l], 'b_k_k': b_k_k[l], 'b_k_a': b_k_a[l], 'b_r_k': b_r_k[l], 'b_ln_w': b_ln_w[l], 'b_ln_b': b_ln_b[l],
              'c_ln_w': c_ln_w[l], 'c_ln_b': c_ln_b[l], 'd_q_norm': d_q_norm[l], 'd_k_norm': d_k_norm[l], 'd_rel_bias': d_rel_bias[l]}
        yp, ps = mixer_sublayer(yp, lp, None)
        ys, ss = mixer_sublayer(ys, lp, (cache_a_k[l], cache_a_v[l], state_b_shift[l], state_b_wkv[l], state_c[l], cache_d_k[l], cache_d_v[l]))
        p_states.append(ps)
        s_states.append(ss)
        j = l // 2
        if l % 2 == 0:
            yp = yp + swiglu(rms_norm(yp, norm2_g[l]), ffn_w1[j], ffn_w3[j], ffn_w2[j])
            ys = ys + swiglu(rms_norm(ys, norm2_g[l]), ffn_w1[j], ffn_w3[j], ffn_w2[j])
        else:
            yp = yp + moe_swiglu(rms_norm(yp, norm2_g[l]), moe_router[j], moe_w1[j], moe_w3[j], moe_w2[j])
            ys = ys + moe_swiglu(rms_norm(ys, norm2_g[l]), moe_router[j], moe_w1[j], moe_w3[j], moe_w2[j])
    st = lambda group, i: jnp.stack([g[i] for g in group], axis=0)
    return (yp, ys,
            st(p_states, 0), st(p_states, 1), st(p_states, 2), st(p_states, 3), st(p_states, 4), st(p_states, 5), st(p_states, 6),
            st(s_states, 0), st(s_states, 1), st(s_states, 2), st(s_states, 3), st(s_states, 4), st(s_states, 5), st(s_states, 6))
```

```python
import functools

import jax
import jax.numpy as jnp
from jax import lax
from jax.experimental import pallas as pl
from jax.experimental.pallas import tpu as pltpu

F32 = jnp.float32
BF16 = jnp.bfloat16

D_MODEL = 1024
DEPTH = 2
CHUNK = 64
HEAD_DIM = 64
N_HEADS = 4
MIX_W = N_HEADS * HEAD_DIM
A_KV_W = 128
A_PREV_CHUNKS = 2
D_PREV_CHUNKS = 8
D_REL_CLIP = 128
B_PROJ = 1024
C_PROJ = 1024
IN_PROJ = 3328
B_GN_EPS = 64e-5
C_GN_EPS = 1e-6
NORM_EPS = 1e-6
ATTN_SCALE = 0.125
ROPE_BASE = 10000.0
D_FF = 2816
N_EXPERTS = 8
E_FF = 3584
NEG_BIG = -1e30

VMEM_LIMIT = 48 * 1024 * 1024

NN = ((1,), (0,))
NT = ((1,), (1,))
TN = ((0,), (0,))


def _dg(a, b, dims=NN):
    return lax.dot_general(a, b, (dims, ((), ())), preferred_element_type=F32)


def _parts(x, n):
    out = []
    r = x
    for i in range(n):
        p = r.astype(BF16)
        out.append(p)
        if i + 1 < n:
            r = r - p.astype(F32)
    return out


def _mm(a, b, dims=NN, passes=1):
    if passes == 1:
        return _dg(a.astype(BF16), b.astype(BF16), dims)
    ah, al = _parts(a, 2)
    bh, bl = _parts(b, 2)
    return _dg(ah, bh, dims) + (_dg(ah, bl, dims) + _dg(al, bh, dims))


def _mm_exact_rhs(a, b_bf, dims=NN, n=3):
    acc = None
    for p in _parts(a, n):
        t = _dg(p, b_bf, dims)
        acc = t if acc is None else acc + t
    return acc


def _iota2(shape, dim):
    return lax.broadcasted_iota(jnp.int32, shape, dim)


def _head_mask(rows, cols=MIX_W):
    return (_iota2((rows, cols), 0) >> 6) == (_iota2((rows, cols), 1) >> 6)


def _seg_matrix(width, value):
    m = _head_mask(width, width)
    return jnp.where(m, value, 0.0).astype(BF16)


def _tile4(z):
    return jnp.concatenate([z, z, z, z], axis=0)


def _fold4(z):
    return (z[0:64] + z[64:128]) + (z[128:192] + z[192:256])


def _sigmoid(x):
    return 1.0 / (1.0 + jnp.exp(-x))


def _cparams(sem):
    return pltpu.CompilerParams(dimension_semantics=sem, vmem_limit_bytes=VMEM_LIMIT)


IN_TM = 512


def _inproj_kernel(x_ref, g_ref, w_ref, aqg_ref, akg_ref, dqg_ref, dkg_ref,
                   aq_ref, ak_ref, av_ref, hb_ref, hc_ref, dq_ref, dk_ref, dv_ref):
    x = x_ref[...]
    ms = jnp.mean(x * x, axis=-1, keepdims=True)
    xn = ((x * lax.rsqrt(ms + NORM_EPS)) * g_ref[...]).astype(BF16)
    seg = _seg_matrix(MIX_W, 1.0 / HEAD_DIM)

    def proj(lo, hi):
        return jnp.dot(xn, w_ref[:, lo:hi], preferred_element_type=F32)

    def head_rms(h, gain_ref):
        w = h.shape[-1]
        msq = _mm_exact_rhs(h * h, seg[:w, :w], n=2)
        return (h * lax.rsqrt(msq + NORM_EPS)) * gain_ref[...]

    aq_ref[...] = head_rms(proj(0, 256), aqg_ref)
    ak_ref[...] = head_rms(proj(256, 384), akg_ref)
    av_ref[...] = proj(384, 512)
    hb_ref[...] = proj(512, 1536)
    hc_ref[...] = proj(1536, 2560)
    dq_ref[...] = head_rms(proj(2560, 2816), dqg_ref)
    dk_ref[...] = head_rms(proj(2816, 3072), dkg_ref)
    dv_ref[...] = proj(3072, 3328)


def _inproj(x, g, w_bf, aqg, akg, dqg, dkg):
    n = x.shape[0]
    widths = (256, 128, 128, B_PROJ, C_PROJ, 256, 256, 256)
    row = lambda w: pl.BlockSpec((IN_TM, w), lambda i: (i, 0))
    full = lambda a: pl.BlockSpec(a.shape, lambda i: (0,) * a.ndim)
    return pl.pallas_call(
        _inproj_kernel,
        grid=(n // IN_TM,),
        in_specs=[row(D_MODEL), full(g), full(w_bf), full(aqg), full(akg), full(dqg), full(dkg)],
        out_specs=[row(w) for w in widths],
        out_shape=[jax.ShapeDtypeStruct((n, w), F32) for w in widths],
        compiler_params=_cparams(("parallel",)),
        name="inproj",
    )(x, g, w_bf, aqg, akg, dqg, dkg)


def _outproj_kernel(x_ref, oa_ref, ob_ref, oc_ref, od_ref, w_ref, g_ref, x1_ref, xn_ref):
    acc = x_ref[...]
    for m, o_ref in enumerate((oa_ref, ob_ref, oc_ref, od_ref)):
        acc = acc + jnp.dot(o_ref[...].astype(BF16), w_ref[m * MIX_W:(m + 1) * MIX_W, :],
                            preferred_element_type=F32)
    x1_ref[...] = acc
    ms = jnp.mean(acc * acc, axis=-1, keepdims=True)
    xn_ref[...] = ((acc * lax.rsqrt(ms + NORM_EPS)) * g_ref[...]).astype(BF16)


def _outproj(x, oa, ob, oc, od, w_bf, g2):
    n = x.shape[0]
    row = lambda w: pl.BlockSpec((IN_TM, w), lambda i: (i, 0))
    full = lambda a: pl.BlockSpec(a.shape, lambda i: (0,) * a.ndim)
    return pl.pallas_call(
        _outproj_kernel,
        grid=(n // IN_TM,),
        in_specs=[row(D_MODEL), row(MIX_W), row(MIX_W), row(MIX_W), row(MIX_W), full(w_bf), full(g2)],
        out_specs=[row(D_MODEL), row(D_MODEL)],
        out_shape=[jax.ShapeDtypeStruct((n, D_MODEL), F32), jax.ShapeDtypeStruct((n, D_MODEL), BF16)],
        compiler_params=_cparams(("parallel",)),
        name="outproj",
    )(x, oa, ob, oc, od, w_bf, g2)


FFN_TM = 512
FFN_TF = 1408


def _ffn_kernel(xn_ref, x1_ref, w1_ref, w3_ref, w2_ref, o_ref):
    f = pl.program_id(1)
    xn = xn_ref[...]
    a = jnp.dot(xn, w1_ref[...], preferred_element_type=F32)
    b = jnp.dot(xn, w3_ref[...], preferred_element_type=F32)
    h = ((a * _sigmoid(a)) * b).astype(BF16)
    y = jnp.dot(h, w2_ref[...], preferred_element_type=F32)

    @pl.when(f == 0)
    def _():
        o_ref[...] = x1_ref[...] + y

    @pl.when(f != 0)
    def _():
        o_ref[...] += y


def _ffn(xn, x1, w1_bf, w3_bf, w2_bf):
    n = xn.shape[0]
    return pl.pallas_call(
        _ffn_kernel,
        grid=(n // FFN_TM, D_FF // FFN_TF),
        in_specs=[
            pl.BlockSpec((FFN_TM, D_MODEL), lambda i, f: (i, 0)),
            pl.BlockSpec((FFN_TM, D_MODEL), lambda i, f: (i, 0)),
            pl.BlockSpec((D_MODEL, FFN_TF), lambda i, f: (0, f)),
            pl.BlockSpec((D_MODEL, FFN_TF), lambda i, f: (0, f)),
            pl.BlockSpec((FFN_TF, D_MODEL), lambda i, f: (f, 0)),
        ],
        out_specs=pl.BlockSpec((FFN_TM, D_MODEL), lambda i, f: (i, 0)),
        out_shape=jax.ShapeDtypeStruct((n, D_MODEL), F32),
        compiler_params=_cparams(("parallel", "arbitrary")),
        name="ffn",
    )(xn, x1, w1_bf, w3_bf, w2_bf)


ROUTER_LANES = 128


def _router_kernel(xn_ref, r_ref, gate_ref):
    xn = xn_ref[...].astype(F32)
    logits = _mm(xn, r_ref[...], passes=3)
    lane = _iota2(logits.shape, 1)
    logits = jnp.where(lane < N_EXPERTS, logits, NEG_BIG)
    m1 = jnp.max(logits, axis=-1, keepdims=True)
    i1 = jnp.min(jnp.where(logits == m1, lane, ROUTER_LANES), axis=-1, keepdims=True)
    rest = jnp.where(lane == i1, NEG_BIG, logits)
    m2 = jnp.max(rest, axis=-1, keepdims=True)
    i2 = jnp.min(jnp.where(rest == m2, lane, ROUTER_LANES), axis=-1, keepdims=True)
    e2 = jnp.exp(m2 - m1)
    den = 1.0 + e2
    gate_ref[...] = jnp.where(lane == i1, 1.0 / den, 0.0) + jnp.where(lane == i2, e2 / den, 0.0)


def _router(xn, router_pad):
    n = xn.shape[0]
    return pl.pallas_call(
        _router_kernel,
        grid=(n // IN_TM,),
        in_specs=[pl.BlockSpec((IN_TM, D_MODEL), lambda i: (i, 0)),
                  pl.BlockSpec(router_pad.shape, lambda i: (0, 0))],
        out_specs=pl.BlockSpec((IN_TM, ROUTER_LANES), lambda i: (i, 0)),
        out_shape=jax.ShapeDtypeStruct((n, ROUTER_LANES), F32),
        compiler_params=_cparams(("parallel",)),
        name="router",
    )(xn, router_pad)


MOE_TM = 1024
MOE_TF = 896


def _moe_kernel(xn_ref, x1_ref, gate_ref, w1_ref, w3_ref, w2_ref, o_ref):
    e = pl.program_id(1)
    f = pl.program_id(2)
    xn = xn_ref[...]
    a = jnp.dot(xn, w1_ref[...], preferred_element_type=F32)
    b = jnp.dot(xn, w3_ref[...], preferred_element_type=F32)
    h = ((a * _sigmoid(a)) * b).astype(BF16)
    y = gate_ref[...] * jnp.dot(h, w2_ref[...], preferred_element_type=F32)
    first = jnp.logical_and(e == 0, f == 0)

    @pl.when(first)
    def _():
        o_ref[...] = x1_ref[...] + y

    @pl.when(jnp.logical_not(first))
    def _():
        o_ref[...] += y


def _moe(xn, x1, gate_cols, w1_bf, w3_bf, w2_bf):
    n = xn.shape[0]
    return pl.pallas_call(
        _moe_kernel,
        grid=(n // MOE_TM, N_EXPERTS, E_FF // MOE_TF),
        in_specs=[
            pl.BlockSpec((MOE_TM, D_MODEL), lambda i, e, f: (i, 0)),
            pl.BlockSpec((MOE_TM, D_MODEL), lambda i, e, f: (i, 0)),
            pl.BlockSpec((None, MOE_TM, 1), lambda i, e, f: (e, i, 0)),
            pl.BlockSpec((None, D_MODEL, MOE_TF), lambda i, e, f: (e, 0, f)),
            pl.BlockSpec((None, D_MODEL, MOE_TF), lambda i, e, f: (e, 0, f)),
            pl.BlockSpec((None, MOE_TF, D_MODEL), lambda i, e, f: (e, f, 0)),
        ],
        out_specs=pl.BlockSpec((MOE_TM, D_MODEL), lambda i, e, f: (i, 0)),
        out_shape=jax.ShapeDtypeStruct((n, D_MODEL), F32),
        compiler_params=_cparams(("parallel", "arbitrary", "arbitrary")),
        name="moe",
    )(xn, x1, gate_cols, w1_bf, w3_bf, w2_bf)


def _relbias_kernel(rb_ref, o_ref, *, nk):
    h = pl.program_id(0)
    q = _iota2((CHUNK, nk), 0)
    r = _iota2((CHUNK, nk), 1)
    idx = jnp.clip(q - (r - (nk - CHUNK)), -D_REL_CLIP, D_REL_CLIP) + D_REL_CLIP

    def body(j, acc):
        return jnp.where(idx == j, rb_ref[h, j], acc)

    o_ref[...] = lax.fori_loop(0, 2 * D_REL_CLIP + 1, body, jnp.zeros((CHUNK, nk), F32))


def _relbias_table(rel_bias, nk):
    return pl.pallas_call(
        functools.partial(_relbias_kernel, nk=nk),
        grid=(N_HEADS,),
        in_specs=[pl.BlockSpec(memory_space=pltpu.SMEM)],
        out_specs=pl.BlockSpec((CHUNK, nk), lambda h: (h, 0)),
        out_shape=jax.ShapeDtypeStruct((N_HEADS * CHUNK, nk), F32),
        name="relbias",
    )(rel_bias)


def _attn_kernel(q_ref, kp_ref, kc_ref, vp_ref, vc_ref, x_ref, o_ref, kbuf, vbuf,
                 *, qb, n_prev, use_sink, mask_first):
    i = pl.program_id(1)
    p_rows = kp_ref.shape[0]
    nk = (n_prev + 1) * CHUNK
    wk = kp_ref.shape[1]

    if wk == MIX_W:
        stage = lambda ref: ref[...].astype(BF16)
    else:
        src = _iota2((wk, MIX_W), 0)
        dst = _iota2((wk, MIX_W), 1)
        expand = jnp.where(src == (dst >> 7) * HEAD_DIM + (dst & (HEAD_DIM - 1)), 1.0, 0.0).astype(BF16)
        stage = lambda ref: jnp.dot(ref[...].astype(BF16), expand, preferred_element_type=F32).astype(BF16)

    kbuf[0:p_rows, :] = stage(kp_ref)
    kbuf[p_rows:, :] = stage(kc_ref)
    vbuf[0:p_rows, :] = stage(vp_ref)
    vbuf[p_rows:, :] = stage(vc_ref)

    hmask = _head_mask(N_HEADS * CHUNK)
    extra = x_ref[...]

    def body(j, carry):
        r0 = pl.multiple_of(j * CHUNK, CHUNK)
        base = pl.multiple_of(p_rows + (j - n_prev) * CHUNK, CHUNK)
        qj = q_ref[pl.ds(r0, CHUNK), :] * ATTN_SCALE
        qs = jnp.where(hmask, _tile4(qj), 0.0).astype(BF16)
        s = _dg(qs, kbuf[pl.ds(base, nk), :], NT)
        if not use_sink:
            s = s + extra
        if mask_first:
            krow = base + _iota2(s.shape, 1)
            s = jnp.where(jnp.logical_and(i == 0, krow < p_rows), NEG_BIG, s)
        m = jnp.max(s, axis=-1, keepdims=True)
        if use_sink:
            m = jnp.maximum(m, extra)
        e = jnp.exp(s - m)
        den = jnp.sum(e, axis=-1, keepdims=True)
        if use_sink:
            den = den + jnp.exp(extra - m)
        o_all = _dg(e.astype(BF16), vbuf[pl.ds(base, nk), :]) * (1.0 / den)
        o_ref[pl.ds(r0, CHUNK), :] = _fold4(jnp.where(hmask, o_all, 0.0))
        return carry

    lax.fori_loop(0, qb, body, 0)


def _attention(q, k, v, k_prev, v_prev, extra, *, n_prev, use_sink):
    s_n, t, _ = q.shape
    wk = k.shape[-1]
    if k_prev is None:
        qb = 8
        rows = qb * CHUNK
        nblk = t // rows
        prev_spec = pl.BlockSpec((None, rows, wk), lambda s, i: (s, jnp.maximum(i - 1, 0), 0))
        k_prev, v_prev, p_rows, mask_first = k, v, rows, True
    else:
        qb, rows, nblk = t // CHUNK, t, 1
        p_rows = k_prev.shape[1]
        prev_spec = pl.BlockSpec((None, p_rows, wk), lambda s, i: (s, 0, 0))
        mask_first = False
    cur = lambda w: pl.BlockSpec((None, rows, w), lambda s, i: (s, i, 0))
    kern = functools.partial(_attn_kernel, qb=qb, n_prev=n_prev, use_sink=use_sink, mask_first=mask_first)
    return pl.pallas_call(
        kern,
        grid=(s_n, nblk),
        in_specs=[cur(MIX_W), prev_spec, cur(wk), prev_spec, cur(wk),
                  pl.BlockSpec(extra.shape, lambda s, i: (0, 0))],
        out_specs=cur(MIX_W),
        out_shape=jax.ShapeDtypeStruct((s_n, t, MIX_W), F32),
        scratch_shapes=[pltpu.VMEM((p_rows + rows, MIX_W), BF16), pltpu.VMEM((p_rows + rows, MIX_W), BF16)],
        compiler_params=_cparams(("parallel", "arbitrary")),
        name="attn_sink" if use_sink else "attn_bias",
    )(q, k_prev, k, v_prev, v, extra)


def _head_layer_norm(o, seg_mean_bf, w, b, eps):
    mu = _mm_exact_rhs(o, seg_mean_bf)
    d = o - mu
    var = _mm_exact_rhs(d * d, seg_mean_bf)
    return (d * lax.rsqrt(var + eps)) * w + b


def _ret_kernel(hc_ref, cos_ref, sin_ref, s0_ref, dstack_ref, qsc_ref, ksc_ref, gam_ref, lnw_ref, lnb_ref,
                o_ref, sout_ref, s_scr, *, qb):
    i = pl.program_id(1)

    @pl.when(i == 0)
    def _():
        s_scr[...] = s0_ref[...]

    hmask = _head_mask(N_HEADS * CHUNK)
    seg_mean = _seg_matrix(MIX_W, 1.0 / HEAD_DIM)
    first_half = (_iota2((CHUNK, MIX_W), 1) & (HEAD_DIM - 1)) < (HEAD_DIM // 2)

    def rope(x, cos, sin):
        partner = jnp.where(first_half, pltpu.roll(x, MIX_W - HEAD_DIM // 2, 1), pltpu.roll(x, HEAD_DIM // 2, 1))
        return x * cos + partner * sin

    def body(j, carry):
        r0 = pl.multiple_of(j * CHUNK, CHUNK)
        cos = cos_ref[pl.ds(r0, CHUNK), :]
        sin = sin_ref[pl.ds(r0, CHUNK), :]
        q = rope(hc_ref[pl.ds(r0, CHUNK), 0:256], cos, sin)
        k = rope(hc_ref[pl.ds(r0, CHUNK), 256:512], cos, sin) * ATTN_SCALE
        v = hc_ref[pl.ds(r0, CHUNK), 512:768]
        g = hc_ref[pl.ds(r0, CHUNK), 768:1024]
        k_bf = k.astype(BF16)
        v_bf = v.astype(BF16)
        state = s_scr[...]
        qs = jnp.where(hmask, _tile4(q), 0.0).astype(BF16)
        sc = _dg(qs, k_bf, NT) * dstack_ref[...]
        intra = _fold4(jnp.where(hmask, _dg(sc.astype(BF16), v_bf), 0.0))
        inter = _dg((q * qsc_ref[...]).astype(BF16), state.astype(BF16))
        kv = _dg((k * ksc_ref[...]).astype(BF16), v_bf, TN)
        s_scr[...] = gam_ref[...] * state + jnp.where(hmask, kv, 0.0)
        y = _head_layer_norm(intra + inter, seg_mean, lnw_ref[...], lnb_ref[...], C_GN_EPS)
        o_ref[pl.ds(r0, CHUNK), :] = y * (g * _sigmoid(g))
        return carry

    lax.fori_loop(0, qb, body, 0)
    sout_ref[...] = s_scr[...]


def _retention(hc, cos, sin, s0_bd, tabs, lnw, lnb):
    s_n, t, _ = hc.shape
    qb = min(8, t // CHUNK)
    rows = qb * CHUNK
    dstack, qsc, ksc, gam = tabs
    full = lambda a: pl.BlockSpec(a.shape, lambda s, i: (0,) * a.ndim)
    return pl.pallas_call(
        functools.partial(_ret_kernel, qb=qb),
        grid=(s_n, t // rows),
        in_specs=[pl.BlockSpec((None, rows, C_PROJ), lambda s, i: (s, i, 0)),
                  pl.BlockSpec((rows, MIX_W), lambda s, i: (i, 0)),
                  pl.BlockSpec((rows, MIX_W), lambda s, i: (i, 0)),
                  pl.BlockSpec((None, MIX_W, MIX_W), lambda s, i: (s, 0, 0)),
                  full(dstack), full(qsc), full(ksc), full(gam), full(lnw), full(lnb)],
        out_specs=[pl.BlockSpec((None, rows, MIX_W), lambda s, i: (s, i, 0)),
                   pl.BlockSpec((None, MIX_W, MIX_W), lambda s, i: (s, 0, 0))],
        out_shape=[jax.ShapeDtypeStruct((s_n, t, MIX_W), F32),
                   jax.ShapeDtypeStruct((s_n, MIX_W, MIX_W), F32)],
        scratch_shapes=[pltpu.VMEM((MIX_W, MIX_W), F32)],
        compiler_params=_cparams(("parallel", "arbitrary")),
        name="retention",
    )(hc, cos, sin, s0_bd, dstack, qsc, ksc, gam, lnw, lnb)


def _retention_tables():
    hh = jnp.arange(N_HEADS, dtype=F32)
    log_gamma = jnp.log(1.0 - 2.0 ** (-5.0 - hh))
    t = jnp.arange(CHUNK)
    diff = t[:, None] - t[None, :]
    dmat = jnp.where(diff >= 0, jnp.exp(log_gamma[:, None, None] * jnp.maximum(diff, 0)), 0.0)
    dstack = dmat.reshape(N_HEADS * CHUNK, CHUNK)
    lanes = lambda per_head: jnp.repeat(per_head, HEAD_DIM, axis=-1)
    qsc = lanes(jnp.exp((t + 1)[:, None] * log_gamma[None, :]))
    ksc = lanes(jnp.exp((CHUNK - 1 - t)[:, None] * log_gamma[None, :]))
    gam = jnp.broadcast_to(lanes(jnp.exp(log_gamma * CHUNK))[:, None], (MIX_W, MIX_W))
    return dstack.astype(F32), qsc.astype(F32), ksc.astype(F32), gam.astype(F32)


def _rope_tables(pos):
    half = HEAD_DIM // 2
    theta = 1.0 / (ROPE_BASE ** jnp.linspace(0.0, 1.0, half, dtype=F32))
    ang = pos.astype(F32)[:, None] * theta[None, :]
    cos, sin = jnp.cos(ang), jnp.sin(ang)
    cos_t = jnp.tile(jnp.concatenate([cos, cos], axis=-1), (1, N_HEADS))
    sin_t = jnp.tile(jnp.concatenate([-sin, sin], axis=-1), (1, N_HEADS))
    return cos_t, sin_t


DECAY_SCALE = 0.6065306597126334
INV_PASSES = 3
STATE_PASSES = 3
OUT_PASSES = 1


def _rwkv_kernel(hb_ref, shift0_ref, h0_ref, mu_ref, w0_ref, w2_ref, a0_ref, a2_ref, g2_ref,
                 kk_ref, ka_ref, rk_ref, lnw_ref, lnb_ref, o_ref, hout_ref, h_scr, shift_scr):
    c = pl.program_id(1)

    @pl.when(c == 0)
    def _():
        h_scr[...] = h0_ref[...]
        shift_scr[...] = shift0_ref[...]

    xb = hb_ref[...]
    row = _iota2(xb.shape, 0)
    prev = jnp.where(row == 0, shift_scr[...], pltpu.roll(xb, 1, 0))
    shift_scr[...] = xb[CHUNK - 1:CHUNK, :]
    xs = xb + mu_ref[...] * (prev - xb)
    r = xs[:, 0:256]
    k = xs[:, 256:512]
    v = xs[:, 512:768]
    xw = xs[:, 768:832]
    xa = xs[:, 832:896]
    xg = xs[:, 896:1024]

    z = w0_ref[...] + _mm(jnp.tanh(xw), w2_ref[...], passes=3)
    lw = -DECAY_SCALE * _sigmoid(z)
    a_gate = _sigmoid(a0_ref[...] + _mm(xa, a2_ref[...], passes=3))
    gate = _mm(_sigmoid(xg), g2_ref[...], passes=1)

    seg_sum = _seg_matrix(MIX_W, 1.0)
    seg_mean = _seg_matrix(MIX_W, 1.0 / HEAD_DIM)
    kkn = k * kk_ref[...]
    norm = jnp.sqrt(_mm_exact_rhs(kkn * kkn, seg_sum))
    kk = kkn / jnp.maximum(norm, 1e-12)
    kf = k * (1.0 + (a_gate - 1.0) * ka_ref[...])

    tt = _iota2((CHUNK, CHUNK), 0)
    ss = _iota2((CHUNK, CHUNK), 1)
    tril = jnp.where(ss <= tt, 1.0, 0.0).astype(BF16)
    lw_parts = _parts(lw, 3)
    cum = _dg(tril, lw_parts[0]) + (_dg(tril, lw_parts[1]) + _dg(tril, lw_parts[2]))
    ones = jnp.ones((CHUNK, MIX_W), BF16)
    ctot = _dg(lw_parts[0], ones, TN) + (_dg(lw_parts[1], ones, TN) + _dg(lw_parts[2], ones, TN))
    w_inv = jnp.exp(-cum)
    rho = r * jnp.exp(cum)
    alpha = -kk * jnp.exp(cum - lw)
    beta = (kk * a_gate) * w_inv
    kappa = kf * w_inv

    n4 = N_HEADS * CHUNK
    hmask = _head_mask(n4)
    bd = lambda zz: jnp.where(hmask, _tile4(zz), 0.0)
    al_bd, be_bd, ka_bd, rh_bd, v_bd = bd(alpha), bd(beta), bd(kappa), bd(rho), bd(v)

    ri = _iota2((n4, n4), 0)
    ci = _iota2((n4, n4), 1)
    strict = jnp.logical_and(hmask, (ci & 63) < (ri & 63))
    incl = jnp.logical_and(hmask, (ci & 63) <= (ri & 63))

    a_mat = jnp.where(strict, _mm(al_bd, be_bd, NT, INV_PASSES), 0.0)
    a_ak = jnp.where(strict, _mm(al_bd, ka_bd, NT, STATE_PASSES), 0.0)
    b_rb = jnp.where(incl, _mm(rh_bd, be_bd, NT, OUT_PASSES), 0.0)
    b_rk = jnp.where(incl, _mm(rh_bd, ka_bd, NT, OUT_PASSES), 0.0)

    def level_mask(log_m):
        same = (ri >> (log_m + 1)) == (ci >> (log_m + 1))
        lower = jnp.logical_and(((ri >> log_m) & 1) == 1, ((ci >> log_m) & 1) == 0)
        return jnp.logical_and(same, lower)

    t_inv = jnp.where(ri == ci, 1.0, 0.0) + jnp.where(level_mask(0), a_mat, 0.0)
    for log_m in range(1, 6):
        e_mat = _mm(jnp.where(level_mask(log_m), a_mat, 0.0), t_inv, NN, INV_PASSES)
        t_inv = t_inv + _mm(t_inv, e_mat, NN, INV_PASSES)

    h0 = h_scr[...]
    x_mat = _mm(al_bd, h0, NN, STATE_PASSES) + _mm(a_ak, v_bd, NN, STATE_PASSES)
    u_mat = _mm(t_inv, x_mat, NN, STATE_PASSES)
    y_bd = _mm(rh_bd, h0, NN, OUT_PASSES) + _mm(b_rb, u_mat, NN, OUT_PASSES) + _mm(b_rk, v_bd, NN, OUT_PASSES)
    h_new = jnp.exp(ctot) * (h0 + _mm(be_bd, u_mat, TN, STATE_PASSES) + _mm(ka_bd, v_bd, TN, STATE_PASSES))
    h_scr[...] = h_new
    hout_ref[...] = h_new

    y = _head_layer_norm(_fold4(y_bd), seg_mean, lnw_ref[...], lnb_ref[...], B_GN_EPS)
    bonus = _mm_exact_rhs(r * kf * rk_ref[...], seg_sum) * v
    o_ref[...] = (y + bonus) * gate


def _rwkv(hb, shift0, h0_bd, params):
    s_n, t, _ = hb.shape
    full = lambda a: pl.BlockSpec(a.shape, lambda s, c: (0,) * a.ndim)
    return pl.pallas_call(
        _rwkv_kernel,
        grid=(s_n, t // CHUNK),
        in_specs=[pl.BlockSpec((None, CHUNK, B_PROJ), lambda s, c: (s, c, 0)),
                  pl.BlockSpec((None, 1, B_PROJ), lambda s, c: (s, 0, 0)),
                  pl.BlockSpec((None, MIX_W, MIX_W), lambda s, c: (s, 0, 0))] + [full(p) for p in params],
        out_specs=[pl.BlockSpec((None, CHUNK, MIX_W), lambda s, c: (s, c, 0)),
                   pl.BlockSpec((None, MIX_W, MIX_W), lambda s, c: (s, 0, 0))],
        out_shape=[jax.ShapeDtypeStruct((s_n, t, MIX_W), F32),
                   jax.ShapeDtypeStruct((s_n, MIX_W, MIX_W), F32)],
        scratch_shapes=[pltpu.VMEM((MIX_W, MIX_W), F32), pltpu.VMEM((1, B_PROJ), F32)],
        compiler_params=_cparams(("parallel", "arbitrary")),
        name="rwkv7",
    )(hb, shift0, h0_bd, *params)


def _to_block_diag(s):
    eye = jnp.eye(N_HEADS, dtype=s.dtype)
    out = s[:, :, :, None, :] * eye[None, :, None, :, None]
    return out.reshape(s.shape[0], MIX_W, MIX_W)


def _from_block_diag(m):
    b = m.reshape(m.shape[0], N_HEADS, HEAD_DIM, N_HEADS, HEAD_DIM)
    return jnp.stack([b[:, h, :, h, :] for h in range(N_HEADS)], axis=1)


def _row(p):
    return p.reshape(1, -1).astype(F32)


def _mixers(group, caches, lp, tabs):
    s_n, t, _ = group["hb"].shape
    if caches is None:
        ka_prev = va_prev = kd_prev = vd_prev = None
        shift0 = jnp.zeros((s_n, 1, B_PROJ), F32)
        h0 = jnp.zeros((s_n, MIX_W, MIX_W), F32)
        s0 = jnp.zeros((s_n, MIX_W, MIX_W), F32)
        cos, sin = tabs["rope_prompt"]
    else:
        ca_k, ca_v, sb_shift, sb_wkv, sc, cd_k, cd_v = caches
        ka_prev = ca_k.reshape(s_n, -1, A_KV_W)
        va_prev = ca_v.reshape(s_n, -1, A_KV_W)
        kd_prev = cd_k.reshape(s_n, -1, MIX_W)
        vd_prev = cd_v.reshape(s_n, -1, MIX_W)
        shift0 = sb_shift.reshape(s_n, 1, B_PROJ)
        h0 = _to_block_diag(jnp.swapaxes(sb_wkv, -1, -2))
        s0 = _to_block_diag(sc)
        cos, sin = tabs["rope_sample"]
    oa = _attention(group["aq"], group["ak"], group["av"], ka_prev, va_prev, lp["sink_col"],
                    n_prev=A_PREV_CHUNKS, use_sink=True)
    ob, h_out = _rwkv(group["hb"], shift0, h0, lp["rwkv"])
    oc, s_out = _retention(group["hc"], cos, sin, s0, tabs["ret"], lp["c_ln_w"], lp["c_ln_b"])
    od = _attention(group["dq"], group["dk"], group["dv"], kd_prev, vd_prev, lp["bias_table"],
                    n_prev=D_PREV_CHUNKS, use_sink=False)
    wkv = jnp.swapaxes(_from_block_diag(h_out), -1, -2)
    ret = _from_block_diag(s_out)
    return (oa, ob, oc, od), (wkv, ret)


def kernel(x_prompt, x_sample, cache_a_k, cache_a_v, state_b_shift, state_b_wkv, state_c, cache_d_k, cache_d_v,
           norm1_g, norm2_g, w_in, w_out, a_q_norm, a_k_norm, a_sinks, b_mu, b_w0, b_w2, b_a0, b_a2, b_g2,
           b_k_k, b_k_a, b_r_k, b_ln_w, b_ln_b, c_ln_w, c_ln_b, d_q_norm, d_k_norm, d_rel_bias,
           ffn_w1, ffn_w3, ffn_w2, moe_router, moe_w1, moe_w3, moe_w2):
    bp, tp, _ = x_prompt.shape
    bs, ts, _ = x_sample.shape
    n_p, n_s = bp * tp, bs * ts
    past_len = 4096
    x = jnp.concatenate([x_prompt.reshape(n_p, D_MODEL), x_sample.reshape(n_s, D_MODEL)], axis=0)

    tabs = {
        "ret": _retention_tables(),
        "rope_prompt": _rope_tables(jnp.arange(tp)),
        "rope_sample": _rope_tables(past_len + jnp.arange(ts)),
    }
    tile = lambda g: _row(jnp.tile(g, MIX_W // HEAD_DIM))

    p_states, s_states = [], []
    for l in range(DEPTH):
        lp = {
            "sink_col": jnp.repeat(a_sinks[l].astype(F32), CHUNK).reshape(N_HEADS * CHUNK, 1),
            "bias_table": _relbias_table(d_rel_bias[l].astype(F32), (D_PREV_CHUNKS + 1) * CHUNK),
            "rwkv": (_row(b_mu[l]), _row(b_w0[l]), b_w2[l], _row(b_a0[l]), b_a2[l], b_g2[l], _row(b_k_k[l]),
                     _row(b_k_a[l]), _row(b_r_k[l]), _row(b_ln_w[l]), _row(b_ln_b[l])),
            "c_ln_w": _row(c_ln_w[l]), "c_ln_b": _row(c_ln_b[l]),
        }
        aq, ak, av, hb, hc, dq, dk, dv = _inproj(
            x, _row(norm1_g[l]), w_in[l].astype(BF16), tile(a_q_norm[l]),
            _row(jnp.tile(a_k_norm[l], A_KV_W // HEAD_DIM)), tile(d_q_norm[l]), tile(d_k_norm[l]))
        names = ("aq", "ak", "av", "hb", "hc", "dq", "dk", "dv")
        arrs = (aq, ak, av, hb, hc, dq, dk, dv)
        gp = {nm: a[:n_p].reshape(bp, tp, -1) for nm, a in zip(names, arrs)}
        gs = {nm: a[n_p:].reshape(bs, ts, -1) for nm, a in zip(names, arrs)}
        caches = (cache_a_k[l], cache_a_v[l], state_b_shift[l], state_b_wkv[l], state_c[l], cache_d_k[l], cache_d_v[l])
        outs_p, (wkv_p, ret_p) = _mixers(gp, None, lp, tabs)
        outs_s, (wkv_s, ret_s) = _mixers(gs, caches, lp, tabs)
        mix = [jnp.concatenate([op.reshape(n_p, MIX_W), os.reshape(n_s, MIX_W)], axis=0)
               for op, os in zip(outs_p, outs_s)]
        x1, xn2 = _outproj(x, *mix, w_out[l].astype(BF16), _row(norm2_g[l]))
        j = l // 2
        if l % 2 == 0:
            x = _ffn(xn2, x1, ffn_w1[j].astype(BF16), ffn_w3[j].astype(BF16), ffn_w2[j].astype(BF16))
        else:
            router_pad = jnp.pad(moe_router[j].astype(F32), ((0, 0), (0, ROUTER_LANES - N_EXPERTS)))
            gates = _router(xn2, router_pad)
            gate_cols = jnp.transpose(gates[:, :N_EXPERTS])[:, :, None]
            x = _moe(xn2, x1, gate_cols, moe_w1[j].astype(BF16), moe_w3[j].astype(BF16), moe_w2[j].astype(BF16))

        wa = min(A_PREV_CHUNKS * CHUNK, tp)
        wd = min(D_PREV_CHUNKS * CHUNK, tp)
        p_states.append((
            gp["ak"][:, -wa:].reshape(bp, wa, 2, HEAD_DIM), gp["av"][:, -wa:].reshape(bp, wa, 2, HEAD_DIM),
            gp["hb"][:, -1], wkv_p, ret_p,
            gp["dk"][:, -wd:].reshape(bp, wd, N_HEADS, HEAD_DIM), gp["dv"][:, -wd:].reshape(bp, wd, N_HEADS, HEAD_DIM)))
        ca_w, cd_w = cache_a_k.shape[2], cache_d_k.shape[2]
        roll_in = lambda cache, new, heads: jnp.concatenate(
            [cache.astype(F32), new.reshape(bs, ts, heads, HEAD_DIM)], axis=1)[:, -cache.shape[1]:]
        s_states.append((
            roll_in(cache_a_k[l], gs["ak"], 2), roll_in(cache_a_v[l], gs["av"], 2),
            gs["hb"][:, -1], wkv_s, ret_s,
            roll_in(cache_d_k[l], gs["dk"], N_HEADS), roll_in(cache_d_v[l], gs["dv"], N_HEADS)))

    yp = x[:n_p].reshape(bp, tp, D_MODEL)
    ys = x[n_p:].reshape(bs, ts, D_MODEL)
    st = lambda group, i: jnp.stack([g[i] for g in group], axis=0)
    return (yp, ys,
            st(p_states, 0), st(p_states, 1), st(p_states, 2), st(p_states, 3), st(p_states, 4), st(p_states, 5), st(p_states, 6),
            st(s_states, 0), st(s_states, 1), st(s_states, 2), st(s_states, 3), st(s_states, 4), st(s_states, 5), st(s_states, 6))
```

```python
import functools

import jax
import jax.numpy as jnp
from jax import lax
from jax.experimental import pallas as pl
from jax.experimental.pallas import tpu as pltpu

F32 = jnp.float32
BF16 = jnp.bfloat16

D_MODEL = 1024
DEPTH = 2
PAST_LEN = 4096
CHUNK = 64
HEAD_DIM = 64
N_HEADS = 4
MIX_W = N_HEADS * HEAD_DIM
A_KV_W = 128
A_PREV_CHUNKS = 2
D_PREV_CHUNKS = 8
D_REL_CLIP = 128
B_PROJ = 1024
C_PROJ = 1024
IN_PROJ = 3328
B_GN_EPS = 64e-5
C_GN_EPS = 1e-6
NORM_EPS = 1e-6
ATTN_SCALE = 0.125
ROPE_BASE = 10000.0
D_FF = 2816
N_EXPERTS = 8
E_FF = 3584
NEG_BIG = -1e30

VMEM_LIMIT = 48 * 1024 * 1024

NN = ((1,), (0,))
NT = ((1,), (1,))
TN = ((0,), (0,))


def _dg(a, b, dims=NN):
    return lax.dot_general(a, b, (dims, ((), ())), preferred_element_type=F32)


def _parts(x, n):
    out = []
    r = x
    for i in range(n):
        p = r.astype(BF16)
        out.append(p)
        if i + 1 < n:
            r = r - p.astype(F32)
    return out


def _mm(a, b, dims=NN, passes=1):
    if passes == 1:
        return _dg(a.astype(BF16), b.astype(BF16), dims)
    ah, al = _parts(a, 2)
    bh, bl = _parts(b, 2)
    return _dg(ah, bh, dims) + (_dg(ah, bl, dims) + _dg(al, bh, dims))


def _mm_exact_rhs(a, b_bf, dims=NN, n=3):
    acc = None
    for p in _parts(a, n):
        t = _dg(p, b_bf, dims)
        acc = t if acc is None else acc + t
    return acc


def _iota2(shape, dim):
    return lax.broadcasted_iota(jnp.int32, shape, dim)


def _head_mask(rows, cols=MIX_W):
    return (_iota2((rows, cols), 0) >> 6) == (_iota2((rows, cols), 1) >> 6)


def _seg_matrix(width, value):
    m = _head_mask(width, width)
    return jnp.where(m, value, 0.0).astype(BF16)


def _tile4(z):
    return jnp.concatenate([z, z, z, z], axis=0)


def _fold4(z):
    return (z[0:64] + z[64:128]) + (z[128:192] + z[192:256])


def _sigmoid(x):
    return 1.0 / (1.0 + jnp.exp(-x))


def _cparams(sem):
    return pltpu.CompilerParams(dimension_semantics=sem, vmem_limit_bytes=VMEM_LIMIT)


def _alias_kwargs(alias, n_inputs):
    if alias is None:
        return [], [], {}
    return [alias], [pl.BlockSpec(memory_space=pl.ANY)], {"input_output_aliases": {n_inputs: 0}}


IN_TM = 512


def _two_source_specs(xa, xb, n):
    na, nb = xa.shape[0] // IN_TM, xb.shape[0] // IN_TM
    spec_a = pl.BlockSpec((IN_TM, D_MODEL), lambda i: (jnp.minimum(i, na - 1), 0))
    spec_b = pl.BlockSpec((IN_TM, D_MODEL), lambda i: (jnp.clip(i - na, 0, nb - 1), 0))
    return na, n // IN_TM, spec_a, spec_b


def _inproj_kernel(xa_ref, xb_ref, g_ref, w_ref, aqg_ref, akg_ref, dqg_ref, dkg_ref,
                   aq_ref, ak_ref, av_ref, hb_ref, hc_ref, dq_ref, dk_ref, dv_ref, *, n_first):
    x = jnp.where(pl.program_id(0) < n_first, xa_ref[...], xb_ref[...])
    ms = jnp.mean(x * x, axis=-1, keepdims=True)
    xn = ((x * lax.rsqrt(ms + NORM_EPS)) * g_ref[...]).astype(BF16)
    seg = _seg_matrix(MIX_W, 1.0 / HEAD_DIM)

    def proj(lo, hi):
        return jnp.dot(xn, w_ref[:, lo:hi], preferred_element_type=F32)

    def head_rms(h, gain_ref):
        w = h.shape[-1]
        msq = _mm_exact_rhs(h * h, seg[:w, :w], n=2)
        return (h * lax.rsqrt(msq + NORM_EPS)) * gain_ref[...]

    aq_ref[...] = head_rms(proj(0, 256), aqg_ref)
    ak_ref[...] = head_rms(proj(256, 384), akg_ref)
    av_ref[...] = proj(384, 512)
    hb_ref[...] = proj(512, 1536)
    hc_ref[...] = proj(1536, 2560)
    dq_ref[...] = head_rms(proj(2560, 2816), dqg_ref)
    dk_ref[...] = head_rms(proj(2816, 3072), dkg_ref)
    dv_ref[...] = proj(3072, 3328)


def _inproj(xa, xb, n, g, w_bf, aqg, akg, dqg, dkg):
    na, nblk, spec_a, spec_b = _two_source_specs(xa, xb, n)
    widths = (256, 128, 128, B_PROJ, C_PROJ, 256, 256, 256)
    row = lambda w: pl.BlockSpec((IN_TM, w), lambda i: (i, 0))
    full = lambda a: pl.BlockSpec(a.shape, lambda i: (0,) * a.ndim)
    return pl.pallas_call(
        functools.partial(_inproj_kernel, n_first=na),
        grid=(nblk,),
        in_specs=[spec_a, spec_b, full(g), full(w_bf), full(aqg), full(akg), full(dqg), full(dkg)],
        out_specs=[row(w) for w in widths],
        out_shape=[jax.ShapeDtypeStruct((n, w), F32) for w in widths],
        compiler_params=_cparams(("parallel",)),
        name="inproj",
    )(xa, xb, g, w_bf, aqg, akg, dqg, dkg)


def _outproj_kernel(xa_ref, xb_ref, oa_ref, ob_ref, oc_ref, od_ref, w_ref, g_ref, x1_ref, xn_ref, *, n_first):
    acc = jnp.where(pl.program_id(0) < n_first, xa_ref[...], xb_ref[...])
    for m, o_ref in enumerate((oa_ref, ob_ref, oc_ref, od_ref)):
        acc = acc + jnp.dot(o_ref[...].astype(BF16), w_ref[m * MIX_W:(m + 1) * MIX_W, :],
                            preferred_element_type=F32)
    x1_ref[...] = acc
    ms = jnp.mean(acc * acc, axis=-1, keepdims=True)
    xn_ref[...] = ((acc * lax.rsqrt(ms + NORM_EPS)) * g_ref[...]).astype(BF16)


def _outproj(xa, xb, n, oa, ob, oc, od, w_bf, g2):
    na, nblk, spec_a, spec_b = _two_source_specs(xa, xb, n)
    row = lambda w: pl.BlockSpec((IN_TM, w), lambda i: (i, 0))
    full = lambda a: pl.BlockSpec(a.shape, lambda i: (0,) * a.ndim)
    return pl.pallas_call(
        functools.partial(_outproj_kernel, n_first=na),
        grid=(nblk,),
        in_specs=[spec_a, spec_b, row(MIX_W), row(MIX_W), row(MIX_W), row(MIX_W), full(w_bf), full(g2)],
        out_specs=[row(D_MODEL), row(D_MODEL)],
        out_shape=[jax.ShapeDtypeStruct((n, D_MODEL), F32), jax.ShapeDtypeStruct((n, D_MODEL), BF16)],
        compiler_params=_cparams(("parallel",)),
        name="outproj",
    )(xa, xb, oa, ob, oc, od, w_bf, g2)


FFN_TM = 512
FFN_TF = 1408


def _ffn_kernel(xn_ref, x1_ref, w1_ref, w3_ref, w2_ref, o_ref):
    f = pl.program_id(1)
    xn = xn_ref[...]
    a = jnp.dot(xn, w1_ref[...], preferred_element_type=F32)
    b = jnp.dot(xn, w3_ref[...], preferred_element_type=F32)
    h = ((a * _sigmoid(a)) * b).astype(BF16)
    y = jnp.dot(h, w2_ref[...], preferred_element_type=F32)

    @pl.when(f == 0)
    def _():
        o_ref[...] = x1_ref[...] + y

    @pl.when(f != 0)
    def _():
        o_ref[...] += y


def _ffn(xn, x1, w1_bf, w3_bf, w2_bf):
    n = xn.shape[0]
    return pl.pallas_call(
        _ffn_kernel,
        grid=(n // FFN_TM, D_FF // FFN_TF),
        in_specs=[
            pl.BlockSpec((FFN_TM, D_MODEL), lambda i, f: (i, 0)),
            pl.BlockSpec((FFN_TM, D_MODEL), lambda i, f: (i, 0)),
            pl.BlockSpec((D_MODEL, FFN_TF), lambda i, f: (0, f)),
            pl.BlockSpec((D_MODEL, FFN_TF), lambda i, f: (0, f)),
            pl.BlockSpec((FFN_TF, D_MODEL), lambda i, f: (f, 0)),
        ],
        out_specs=pl.BlockSpec((FFN_TM, D_MODEL), lambda i, f: (i, 0)),
        out_shape=jax.ShapeDtypeStruct((n, D_MODEL), F32),
        compiler_params=_cparams(("parallel", "arbitrary")),
        name="ffn",
    )(xn, x1, w1_bf, w3_bf, w2_bf)


ROUTER_LANES = 128


def _router_kernel(xn_ref, r_ref, gate_ref):
    xn = xn_ref[...].astype(F32)
    logits = _mm(xn, r_ref[...], passes=3)
    lane = _iota2(logits.shape, 1)
    logits = jnp.where(lane < N_EXPERTS, logits, NEG_BIG)
    m1 = jnp.max(logits, axis=-1, keepdims=True)
    i1 = jnp.min(jnp.where(logits == m1, lane, ROUTER_LANES), axis=-1, keepdims=True)
    rest = jnp.where(lane == i1, NEG_BIG, logits)
    m2 = jnp.max(rest, axis=-1, keepdims=True)
    i2 = jnp.min(jnp.where(rest == m2, lane, ROUTER_LANES), axis=-1, keepdims=True)
    e2 = jnp.exp(m2 - m1)
    den = 1.0 + e2
    gate_ref[...] = jnp.where(lane == i1, 1.0 / den, 0.0) + jnp.where(lane == i2, e2 / den, 0.0)


def _router(xn, router_pad):
    n = xn.shape[0]
    return pl.pallas_call(
        _router_kernel,
        grid=(n // IN_TM,),
        in_specs=[pl.BlockSpec((IN_TM, D_MODEL), lambda i: (i, 0)),
                  pl.BlockSpec(router_pad.shape, lambda i: (0, 0))],
        out_specs=pl.BlockSpec((IN_TM, ROUTER_LANES), lambda i: (i, 0)),
        out_shape=jax.ShapeDtypeStruct((n, ROUTER_LANES), F32),
        compiler_params=_cparams(("parallel",)),
        name="router",
    )(xn, router_pad)


MOE_TM = 1024
MOE_TF = 896


def _moe_kernel(xn_ref, x1_ref, gate_ref, w1_ref, w3_ref, w2_ref, o_ref):
    e = pl.program_id(1)
    f = pl.program_id(2)
    xn = xn_ref[...]
    a = jnp.dot(xn, w1_ref[...], preferred_element_type=F32)
    b = jnp.dot(xn, w3_ref[...], preferred_element_type=F32)
    h = ((a * _sigmoid(a)) * b).astype(BF16)
    y = gate_ref[...] * jnp.dot(h, w2_ref[...], preferred_element_type=F32)
    first = jnp.logical_and(e == 0, f == 0)

    @pl.when(first)
    def _():
        o_ref[...] = x1_ref[...] + y

    @pl.when(jnp.logical_not(first))
    def _():
        o_ref[...] += y


def _moe(xn, x1, gate_cols, w1_bf, w3_bf, w2_bf):
    n = xn.shape[0]
    return pl.pallas_call(
        _moe_kernel,
        grid=(n // MOE_TM, N_EXPERTS, E_FF // MOE_TF),
        in_specs=[
            pl.BlockSpec((MOE_TM, D_MODEL), lambda i, e, f: (i, 0)),
            pl.BlockSpec((MOE_TM, D_MODEL), lambda i, e, f: (i, 0)),
            pl.BlockSpec((None, MOE_TM, 1), lambda i, e, f: (e, i, 0)),
            pl.BlockSpec((None, D_MODEL, MOE_TF), lambda i, e, f: (e, 0, f)),
            pl.BlockSpec((None, D_MODEL, MOE_TF), lambda i, e, f: (e, 0, f)),
            pl.BlockSpec((None, MOE_TF, D_MODEL), lambda i, e, f: (e, f, 0)),
        ],
        out_specs=pl.BlockSpec((MOE_TM, D_MODEL), lambda i, e, f: (i, 0)),
        out_shape=jax.ShapeDtypeStruct((n, D_MODEL), F32),
        compiler_params=_cparams(("parallel", "arbitrary", "arbitrary")),
        name="moe",
    )(xn, x1, gate_cols, w1_bf, w3_bf, w2_bf)


def _relbias_kernel(rb_ref, o_ref, *, nk):
    h = pl.program_id(0)
    q = _iota2((CHUNK, nk), 0)
    r = _iota2((CHUNK, nk), 1)
    idx = jnp.clip(q - (r - (nk - CHUNK)), -D_REL_CLIP, D_REL_CLIP) + D_REL_CLIP

    def body(j, acc):
        return jnp.where(idx == j, rb_ref[h, j], acc)

    o_ref[...] = lax.fori_loop(0, 2 * D_REL_CLIP + 1, body, jnp.zeros((CHUNK, nk), F32))


def _relbias_table(rel_bias, nk):
    return pl.pallas_call(
        functools.partial(_relbias_kernel, nk=nk),
        grid=(N_HEADS,),
        in_specs=[pl.BlockSpec(memory_space=pltpu.SMEM)],
        out_specs=pl.BlockSpec((CHUNK, nk), lambda h: (h, 0)),
        out_shape=jax.ShapeDtypeStruct((N_HEADS * CHUNK, nk), F32),
        name="relbias",
    )(rel_bias)


ATTN_QB = 8


def _attn_kernel(q_ref, kp_ref, kc_ref, vp_ref, vc_ref, x_ref, *rest, qb, n_prev, use_sink, mask_first):
    o_ref, kbuf, vbuf = rest[-3:]
    i = pl.program_id(1)
    p_rows = kp_ref.shape[0]
    nk = (n_prev + 1) * CHUNK
    wk = kp_ref.shape[1]

    if wk == MIX_W:
        stage = lambda ref: ref[...].astype(BF16)
    else:
        src = _iota2((wk, MIX_W), 0)
        dst = _iota2((wk, MIX_W), 1)
        expand = jnp.where(src == (dst >> 7) * HEAD_DIM + (dst & (HEAD_DIM - 1)), 1.0, 0.0).astype(BF16)
        stage = lambda ref: jnp.dot(ref[...].astype(BF16), expand, preferred_element_type=F32).astype(BF16)

    kbuf[0:p_rows, :] = stage(kp_ref)
    kbuf[p_rows:, :] = stage(kc_ref)
    vbuf[0:p_rows, :] = stage(vp_ref)
    vbuf[p_rows:, :] = stage(vc_ref)

    hmask = _head_mask(N_HEADS * CHUNK)
    extra = x_ref[...]

    def body(j, carry):
        r0 = pl.multiple_of(j * CHUNK, CHUNK)
        base = pl.multiple_of(p_rows + (j - n_prev) * CHUNK, CHUNK)
        qj = q_ref[pl.ds(r0, CHUNK), :] * ATTN_SCALE
        qs = jnp.where(hmask, _tile4(qj), 0.0).astype(BF16)
        s = _dg(qs, kbuf[pl.ds(base, nk), :], NT)
        if not use_sink:
            s = s + extra
        if mask_first:
            krow = base + _iota2(s.shape, 1)
            s = jnp.where(jnp.logical_and(i == 0, krow < p_rows), NEG_BIG, s)
        m = jnp.max(s, axis=-1, keepdims=True)
        if use_sink:
            m = jnp.maximum(m, extra)
        e = jnp.exp(s - m)
        den = jnp.sum(e, axis=-1, keepdims=True)
        if use_sink:
            den = den + jnp.exp(extra - m)
        o_all = _dg(e.astype(BF16), vbuf[pl.ds(base, nk), :]) * (1.0 / den)
        o_ref[pl.ds(r0, CHUNK), :] = _fold4(jnp.where(hmask, o_all, 0.0))
        return carry

    lax.fori_loop(0, qb, body, 0)


def _attention(q, k, v, prev, extra, *, n_prev, use_sink, n_streams, t, base_row, alias=None):
    n = q.shape[0]
    wk = k.shape[-1]
    if prev is None:
        qb = ATTN_QB
        rows = qb * CHUNK
        nblk = t // rows
        base = base_row // rows
        cur_map = lambda s, i: (base + s * nblk + i, 0)
        prev_spec = pl.BlockSpec((rows, wk), lambda s, i: (base + s * nblk + jnp.maximum(i - 1, 0), 0))
        k_prev, v_prev, p_rows, mask_first = k, v, rows, True
    else:
        qb, rows, nblk = t // CHUNK, t, 1
        base = base_row // rows
        cur_map = lambda s, i: (base + s, 0)
        k_prev, v_prev = prev
        p_rows = k_prev.shape[1]
        prev_spec = pl.BlockSpec((None, p_rows, wk), lambda s, i: (s, 0, 0))
        mask_first = False
    cur = lambda w: pl.BlockSpec((rows, w), cur_map)
    kern = functools.partial(_attn_kernel, qb=qb, n_prev=n_prev, use_sink=use_sink, mask_first=mask_first)
    a_in, a_spec, a_kw = _alias_kwargs(alias, 6)
    return pl.pallas_call(
        kern,
        grid=(n_streams, nblk),
        in_specs=[cur(MIX_W), prev_spec, cur(wk), prev_spec, cur(wk),
                  pl.BlockSpec(extra.shape, lambda s, i: (0, 0))] + a_spec,
        out_specs=cur(MIX_W),
        out_shape=jax.ShapeDtypeStruct((n, MIX_W), F32),
        scratch_shapes=[pltpu.VMEM((p_rows + rows, MIX_W), BF16), pltpu.VMEM((p_rows + rows, MIX_W), BF16)],
        compiler_params=_cparams(("parallel", "arbitrary")),
        name="attn_sink" if use_sink else "attn_bias",
        **a_kw,
    )(q, k_prev, k, v_prev, v, extra, *a_in)


def _head_layer_norm(o, seg_mean_bf, w, b, eps):
    mu = _mm_exact_rhs(o, seg_mean_bf)
    d = o - mu
    var = _mm_exact_rhs(d * d, seg_mean_bf)
    return (d * lax.rsqrt(var + eps)) * w + b


def _ret_kernel(hc_ref, cos_ref, sin_ref, s0_ref, dstack_ref, qsc_ref, ksc_ref, gam_ref, lnw_ref, lnb_ref,
                *rest, qb):
    o_ref, sout_ref, s_scr = rest[-3:]
    i = pl.program_id(1)

    @pl.when(i == 0)
    def _():
        s_scr[...] = s0_ref[...]

    hmask = _head_mask(N_HEADS * CHUNK)
    seg_mean = _seg_matrix(MIX_W, 1.0 / HEAD_DIM)
    first_half = (_iota2((CHUNK, MIX_W), 1) & (HEAD_DIM - 1)) < (HEAD_DIM // 2)

    def rope(x, cos, sin):
        partner = jnp.where(first_half, pltpu.roll(x, MIX_W - HEAD_DIM // 2, 1), pltpu.roll(x, HEAD_DIM // 2, 1))
        return x * cos + partner * sin

    def body(j, carry):
        r0 = pl.multiple_of(j * CHUNK, CHUNK)
        cos = cos_ref[pl.ds(r0, CHUNK), :]
        sin = sin_ref[pl.ds(r0, CHUNK), :]
        q = rope(hc_ref[pl.ds(r0, CHUNK), 0:256], cos, sin)
        k = rope(hc_ref[pl.ds(r0, CHUNK), 256:512], cos, sin) * ATTN_SCALE
        v = hc_ref[pl.ds(r0, CHUNK), 512:768]
        g = hc_ref[pl.ds(r0, CHUNK), 768:1024]
        k_bf = k.astype(BF16)
        v_bf = v.astype(BF16)
        state = s_scr[...]
        qs = jnp.where(hmask, _tile4(q), 0.0).astype(BF16)
        sc = _dg(qs, k_bf, NT) * dstack_ref[...]
        intra = _fold4(jnp.where(hmask, _dg(sc.astype(BF16), v_bf), 0.0))
        inter = _dg((q * qsc_ref[...]).astype(BF16), state.astype(BF16))
        kv = _dg((k * ksc_ref[...]).astype(BF16), v_bf, TN)
        s_scr[...] = gam_ref[...] * state + jnp.where(hmask, kv, 0.0)
        y = _head_layer_norm(intra + inter, seg_mean, lnw_ref[...], lnb_ref[...], C_GN_EPS)
        o_ref[pl.ds(r0, CHUNK), :] = y * (g * _sigmoid(g))
        return carry

    lax.fori_loop(0, qb, body, 0)
    sout_ref[...] = s_scr[...]


def _retention(hc, cos, sin, s0_bd, tabs, lnw, lnb, *, n_streams, t, base_row, alias=None):
    n = hc.shape[0]
    qb = min(8, t // CHUNK)
    rows = qb * CHUNK
    nblk = t // rows
    base = base_row // rows
    dstack, qsc, ksc, gam = tabs
    full = lambda a: pl.BlockSpec(a.shape, lambda s, i: (0,) * a.ndim)
    cur = lambda w: pl.BlockSpec((rows, w), lambda s, i: (base + s * nblk + i, 0))
    state = pl.BlockSpec((None, MIX_W, MIX_W), lambda s, i: (s, 0, 0))
    a_in, a_spec, a_kw = _alias_kwargs(alias, 10)
    return pl.pallas_call(
        functools.partial(_ret_kernel, qb=qb),
        grid=(n_streams, nblk),
        in_specs=[cur(C_PROJ),
                  pl.BlockSpec((rows, MIX_W), lambda s, i: (i, 0)),
                  pl.BlockSpec((rows, MIX_W), lambda s, i: (i, 0)),
                  state, full(dstack), full(qsc), full(ksc), full(gam), full(lnw), full(lnb)] + a_spec,
        out_specs=[cur(MIX_W), state],
        out_shape=[jax.ShapeDtypeStruct((n, MIX_W), F32),
                   jax.ShapeDtypeStruct((n_streams, MIX_W, MIX_W), F32)],
        scratch_shapes=[pltpu.VMEM((MIX_W, MIX_W), F32)],
        compiler_params=_cparams(("parallel", "arbitrary")),
        name="retention",
        **a_kw,
    )(hc, cos, sin, s0_bd, dstack, qsc, ksc, gam, lnw, lnb, *a_in)


def _retention_tables():
    hh = jnp.arange(N_HEADS, dtype=F32)
    log_gamma = jnp.log(1.0 - 2.0 ** (-5.0 - hh))
    t = jnp.arange(CHUNK)
    diff = t[:, None] - t[None, :]
    dmat = jnp.where(diff >= 0, jnp.exp(log_gamma[:, None, None] * jnp.maximum(diff, 0)), 0.0)
    dstack = dmat.reshape(N_HEADS * CHUNK, CHUNK)
    lanes = lambda per_head: jnp.repeat(per_head, HEAD_DIM, axis=-1)
    qsc = lanes(jnp.exp((t + 1)[:, None] * log_gamma[None, :]))
    ksc = lanes(jnp.exp((CHUNK - 1 - t)[:, None] * log_gamma[None, :]))
    gam = jnp.broadcast_to(lanes(jnp.exp(log_gamma * CHUNK))[:, None], (MIX_W, MIX_W))
    return dstack.astype(F32), qsc.astype(F32), ksc.astype(F32), gam.astype(F32)


def _rope_tables(pos):
    half = HEAD_DIM // 2
    theta = 1.0 / (ROPE_BASE ** jnp.linspace(0.0, 1.0, half, dtype=F32))
    ang = pos.astype(F32)[:, None] * theta[None, :]
    cos, sin = jnp.cos(ang), jnp.sin(ang)
    cos_t = jnp.tile(jnp.concatenate([cos, cos], axis=-1), (1, N_HEADS))
    sin_t = jnp.tile(jnp.concatenate([-sin, sin], axis=-1), (1, N_HEADS))
    return cos_t, sin_t


DECAY_SCALE = 0.6065306597126334
RWKV_CB = 4
N_LEVELS = 6
MASK_HEAD, MASK_STRICT, MASK_INCL, MASK_LEVEL0 = 0, 1, 2, 3


def _rwkv_masks():
    n4 = N_HEADS * CHUNK
    ri = jnp.arange(n4)[:, None]
    ci = jnp.arange(n4)[None, :]
    head = (ri >> 6) == (ci >> 6)
    tabs = [head, head & ((ci & 63) < (ri & 63)), head & ((ci & 63) <= (ri & 63))]
    for log_m in range(N_LEVELS):
        same = (ri >> (log_m + 1)) == (ci >> (log_m + 1))
        tabs.append(same & (((ri >> log_m) & 1) == 1) & (((ci >> log_m) & 1) == 0))
    return jnp.stack(tabs).astype(BF16)


def _rwkv_kernel(hb_ref, shift0_ref, h0_ref, masks_ref, mu_ref, w0_ref, w2_ref, a0_ref, a2_ref, g2_ref,
                 kk_ref, ka_ref, rk_ref, lnw_ref, lnb_ref, *rest, cb, independent):
    o_ref, hout_ref, h_scr, shift_scr = rest[-4:]
    c = pl.program_id(1)
    rows = cb * CHUNK
    xb = hb_ref[...]
    row = _iota2(xb.shape, 0)
    prev = pltpu.roll(xb, 1, 0)
    if independent:
        for j in range(cb):
            prev = jnp.where(row == j * CHUNK, shift0_ref[j], prev)
    else:
        @pl.when(c == 0)
        def _():
            h_scr[...] = h0_ref[0]
            shift_scr[...] = shift0_ref[0]

        prev = jnp.where(row == 0, shift_scr[...], prev)
        shift_scr[...] = xb[rows - 1:rows, :]
    xs = xb + mu_ref[...] * (prev - xb)
    r = xs[:, 0:256]
    k = xs[:, 256:512]
    v = xs[:, 512:768]
    xw = xs[:, 768:832]
    xa = xs[:, 832:896]
    xg = xs[:, 896:1024]

    z = w0_ref[...] + _mm(jnp.tanh(xw), w2_ref[...], passes=3)
    lw = -DECAY_SCALE * _sigmoid(z)
    a_gate = _sigmoid(a0_ref[...] + _mm(xa, a2_ref[...], passes=3))
    gate = _mm(_sigmoid(xg), g2_ref[...], passes=1)

    seg_sum = _seg_matrix(MIX_W, 1.0)
    seg_mean = _seg_matrix(MIX_W, 1.0 / HEAD_DIM)
    kkn = k * kk_ref[...]
    norm = jnp.sqrt(_mm_exact_rhs(kkn * kkn, seg_sum))
    kk = kkn / jnp.maximum(norm, 1e-12)
    kf = k * (1.0 + (a_gate - 1.0) * ka_ref[...])

    tt = _iota2((rows, rows), 0)
    ss = _iota2((rows, rows), 1)
    tril = jnp.where(jnp.logical_and(ss <= tt, (ss >> 6) == (tt >> 6)), 1.0, 0.0).astype(BF16)
    lw_parts = _parts(lw, 3)
    cum = _dg(tril, lw_parts[0]) + (_dg(tril, lw_parts[1]) + _dg(tril, lw_parts[2]))
    w_inv = jnp.exp(-cum)
    rho = (r * jnp.exp(cum)).astype(BF16)
    alpha = (-kk * jnp.exp(cum - lw)).astype(BF16)
    beta = ((kk * a_gate) * w_inv).astype(BF16)
    kappa = (kf * w_inv).astype(BF16)
    v_bf = v.astype(BF16)
    ones = jnp.ones((CHUNK, MIX_W), BF16)

    hmask = masks_ref[MASK_HEAD]
    n4 = N_HEADS * CHUNK
    eye = jnp.where(_iota2((n4, n4), 0) == _iota2((n4, n4), 1), 1.0, 0.0)

    pre = []
    for j in range(cb):
        sl = slice(j * CHUNK, (j + 1) * CHUNK)
        bd = lambda zz: _tile4(zz[sl]) * hmask
        al_bd, be_bd, ka_bd, rh_bd, v_bd = bd(alpha), bd(beta), bd(kappa), bd(rho), bd(v_bf)
        a_bf = _dg(al_bd, be_bd, NT).astype(BF16) * masks_ref[MASK_STRICT]
        a_ak = _dg(al_bd, ka_bd, NT).astype(BF16) * masks_ref[MASK_STRICT]
        b_rb = _dg(rh_bd, be_bd, NT).astype(BF16) * masks_ref[MASK_INCL]
        b_rk = _dg(rh_bd, ka_bd, NT).astype(BF16) * masks_ref[MASK_INCL]
        t_inv = eye + (a_bf * masks_ref[MASK_LEVEL0]).astype(F32)
        for lvl in range(1, N_LEVELS):
            t_bf = t_inv.astype(BF16)
            e_mat = _dg(a_bf * masks_ref[MASK_LEVEL0 + lvl], t_bf)
            t_inv = t_inv + _dg(t_bf, e_mat.astype(BF16))
        x0 = _dg(a_ak, v_bd)
        y0 = _dg(b_rk, v_bd)
        hn0 = _dg(ka_bd, v_bd, TN)
        ctot = (_dg(lw_parts[0][sl], ones, TN)
                + (_dg(lw_parts[1][sl], ones, TN) + _dg(lw_parts[2][sl], ones, TN)))
        pre.append((al_bd, be_bd, rh_bd, b_rb, t_inv.astype(BF16), x0, y0, hn0, jnp.exp(ctot)))

    ys = []
    h = None if independent else h_scr[...]
    for j in range(cb):
        al_bd, be_bd, rh_bd, b_rb, t_bf, x0, y0, hn0, wc = pre[j]
        h0 = h0_ref[j] if independent else h
        h0_bf = h0.astype(BF16)
        x_mat = _dg(al_bd, h0_bf) + x0
        u_bf = _dg(t_bf, x_mat.astype(BF16)).astype(BF16)
        y_bd = _dg(rh_bd, h0_bf) + _dg(b_rb, u_bf) + y0
        h_new = wc * (h0 + _dg(be_bd, u_bf, TN) + hn0)
        ys.append(_fold4(y_bd))
        if independent:
            hout_ref[j] = h_new
        else:
            h = h_new
    if not independent:
        h_scr[...] = h
        hout_ref[0] = h

    y = _head_layer_norm(jnp.concatenate(ys, axis=0), seg_mean, lnw_ref[...], lnb_ref[...], B_GN_EPS)
    bonus = _mm_exact_rhs(r * kf * rk_ref[...], seg_sum) * v
    o_ref[...] = (y + bonus) * gate


def _rwkv(hb, shift0, h0_bd, params, *, independent, n_streams, t, base_row, alias=None):
    n = hb.shape[0]
    cb = RWKV_CB
    rows = cb * CHUNK
    nblk = t // rows
    base = base_row // rows
    masks = _rwkv_masks()
    full = lambda a: pl.BlockSpec(a.shape, lambda s, c: (0,) * a.ndim)
    if independent:
        assert n_streams == 1
        st_map = lambda s, c: (c, 0, 0)
        n_state, st_blk = t // CHUNK, cb
    else:
        st_map = lambda s, c: (s, 0, 0)
        n_state, st_blk = n_streams, 1
    cur = lambda w: pl.BlockSpec((rows, w), lambda s, c: (base + s * nblk + c, 0))
    a_in, a_spec, a_kw = _alias_kwargs(alias, 4 + len(params))
    return pl.pallas_call(
        functools.partial(_rwkv_kernel, cb=cb, independent=independent),
        grid=(n_streams, nblk),
        in_specs=[cur(B_PROJ),
                  pl.BlockSpec((st_blk, 1, B_PROJ), st_map),
                  pl.BlockSpec((st_blk, MIX_W, MIX_W), st_map),
                  full(masks)] + [full(p) for p in params] + a_spec,
        out_specs=[cur(MIX_W), pl.BlockSpec((st_blk, MIX_W, MIX_W), st_map)],
        out_shape=[jax.ShapeDtypeStruct((n, MIX_W), F32),
                   jax.ShapeDtypeStruct((n_state, MIX_W, MIX_W), F32)],
        scratch_shapes=[pltpu.VMEM((MIX_W, MIX_W), F32), pltpu.VMEM((1, B_PROJ), F32)],
        compiler_params=_cparams(("parallel", "arbitrary")),
        name="rwkv7",
        **a_kw,
    )(hb, shift0, h0_bd, masks, *params, *a_in)


def _to_block_diag(s):
    eye = jnp.eye(N_HEADS, dtype=s.dtype)
    out = s[:, :, :, None, :] * eye[None, :, None, :, None]
    return out.reshape(s.shape[0], MIX_W, MIX_W)


def _from_block_diag(m):
    b = m.reshape(m.shape[0], N_HEADS, HEAD_DIM, N_HEADS, HEAD_DIM)
    return jnp.stack([b[:, h, :, h, :] for h in range(N_HEADS)], axis=1)


def _row(p):
    return p.reshape(1, -1).astype(F32)


def _mixers(proj, caches, lp, tabs, geom):
    aq, ak, av, hb, hc, dq, dk, dv = proj
    bp, tp, bs, ts = geom
    n_p = bp * tp
    ca_k, ca_v, sb_shift, sb_wkv, sc, cd_k, cd_v = caches
    pr = dict(n_streams=bp, t=tp, base_row=0)
    sm = dict(n_streams=bs, t=ts, base_row=n_p)
    zeros_state = jnp.zeros((bp, MIX_W, MIX_W), F32)

    oa = _attention(aq, ak, av, None, lp["sink_col"], n_prev=A_PREV_CHUNKS, use_sink=True, **pr)
    oa = _attention(aq, ak, av, (ca_k.reshape(bs, -1, A_KV_W), ca_v.reshape(bs, -1, A_KV_W)), lp["sink_col"],
                    n_prev=A_PREV_CHUNKS, use_sink=True, alias=oa, **sm)

    ob, h_p = _rwkv(hb, jnp.zeros((bp, 1, B_PROJ), F32), zeros_state, lp["rwkv"], independent=False, **pr)
    h0_s = _to_block_diag(jnp.swapaxes(sb_wkv, -1, -2))
    ob, h_s = _rwkv(hb, sb_shift.reshape(bs, 1, B_PROJ), h0_s, lp["rwkv"], independent=True,
                    n_streams=1, t=bs * ts, base_row=n_p, alias=ob)

    oc, s_p = _retention(hc, *tabs["rope_prompt"], zeros_state, tabs["ret"], lp["c_ln_w"], lp["c_ln_b"], **pr)
    oc, s_s = _retention(hc, *tabs["rope_sample"], _to_block_diag(sc), tabs["ret"], lp["c_ln_w"], lp["c_ln_b"],
                         alias=oc, **sm)

    od = _attention(dq, dk, dv, None, lp["bias_table"], n_prev=D_PREV_CHUNKS, use_sink=False, **pr)
    od = _attention(dq, dk, dv, (cd_k.reshape(bs, -1, MIX_W), cd_v.reshape(bs, -1, MIX_W)), lp["bias_table"],
                    n_prev=D_PREV_CHUNKS, use_sink=False, alias=od, **sm)

    wkv = lambda h: jnp.swapaxes(_from_block_diag(h), -1, -2)
    return (oa, ob, oc, od), (wkv(h_p), _from_block_diag(s_p)), (wkv(h_s), _from_block_diag(s_s))


def kernel(x_prompt, x_sample, cache_a_k, cache_a_v, state_b_shift, state_b_wkv, state_c, cache_d_k, cache_d_v,
           norm1_g, norm2_g, w_in, w_out, a_q_norm, a_k_norm, a_sinks, b_mu, b_w0, b_w2, b_a0, b_a2, b_g2,
           b_k_k, b_k_a, b_r_k, b_ln_w, b_ln_b, c_ln_w, c_ln_b, d_q_norm, d_k_norm, d_rel_bias,
           ffn_w1, ffn_w3, ffn_w2, moe_router, moe_w1, moe_w3, moe_w2):
    bp, tp, _ = x_prompt.shape
    bs, ts, _ = x_sample.shape
    assert ts == CHUNK
    n_p, n_s = bp * tp, bs * ts
    geom = (bp, tp, bs, ts)
    xa = x_prompt.reshape(n_p, D_MODEL)
    xb = x_sample.reshape(n_s, D_MODEL)

    tabs = {
        "ret": _retention_tables(),
        "rope_prompt": _rope_tables(jnp.arange(tp)),
        "rope_sample": _rope_tables(PAST_LEN + jnp.arange(ts)),
    }
    tile = lambda g: _row(jnp.tile(g, MIX_W // HEAD_DIM))

    p_states, s_states = [], []
    for l in range(DEPTH):
        lp = {
            "sink_col": jnp.repeat(a_sinks[l].astype(F32), CHUNK).reshape(N_HEADS * CHUNK, 1),
            "bias_table": _relbias_table(d_rel_bias[l].astype(F32), (D_PREV_CHUNKS + 1) * CHUNK),
            "rwkv": (_row(b_mu[l]), _row(b_w0[l]), b_w2[l], _row(b_a0[l]), b_a2[l], b_g2[l], _row(b_k_k[l]),
                     _row(b_k_a[l]), _row(b_r_k[l]), _row(b_ln_w[l]), _row(b_ln_b[l])),
            "c_ln_w": _row(c_ln_w[l]), "c_ln_b": _row(c_ln_b[l]),
        }
        proj = _inproj(xa, xb, n_p + n_s, _row(norm1_g[l]), w_in[l].astype(BF16), tile(a_q_norm[l]),
                       _row(jnp.tile(a_k_norm[l], A_KV_W // HEAD_DIM)), tile(d_q_norm[l]), tile(d_k_norm[l]))
        _, ak, av, hb, _, _, dk, dv = proj
        caches = (cache_a_k[l], cache_a_v[l], state_b_shift[l], state_b_wkv[l], state_c[l], cache_d_k[l], cache_d_v[l])
        mix, (wkv_p, ret_p), (wkv_s, ret_s) = _mixers(proj, caches, lp, tabs, geom)
        x1, xn2 = _outproj(xa, xb, n_p + n_s, *mix, w_out[l].astype(BF16), _row(norm2_g[l]))
        j = l // 2
        if l % 2 == 0:
            x = _ffn(xn2, x1, ffn_w1[j].astype(BF16), ffn_w3[j].astype(BF16), ffn_w2[j].astype(BF16))
        else:
            router_pad = jnp.pad(moe_router[j].astype(F32), ((0, 0), (0, ROUTER_LANES - N_EXPERTS)))
            gates = _router(xn2, router_pad)
            gate_cols = jnp.transpose(gates[:, :N_EXPERTS])[:, :, None]
            x = _moe(xn2, x1, gate_cols, moe_w1[j].astype(BF16), moe_w3[j].astype(BF16), moe_w2[j].astype(BF16))
        xa = xb = x

        wa = min(A_PREV_CHUNKS * CHUNK, tp)
        wd = min(D_PREV_CHUNKS * CHUNK, tp)
        tail = lambda a, w, heads: a[:n_p].reshape(bp, tp, heads, HEAD_DIM)[:, tp - w:]
        p_states.append((tail(ak, wa, 2), tail(av, wa, 2), hb[:n_p].reshape(bp, tp, B_PROJ)[:, -1], wkv_p, ret_p,
                         tail(dk, wd, N_HEADS), tail(dv, wd, N_HEADS)))
        roll_in = lambda cache, new, heads: jnp.concatenate(
            [cache.astype(F32), new[n_p:].reshape(bs, ts, heads, HEAD_DIM)], axis=1)[:, -cache.shape[1]:]
        s_states.append((roll_in(cache_a_k[l], ak, 2), roll_in(cache_a_v[l], av, 2),
                         hb[n_p:].reshape(bs, ts, B_PROJ)[:, -1], wkv_s, ret_s,
                         roll_in(cache_d_k[l], dk, N_HEADS), roll_in(cache_d_v[l], dv, N_HEADS)))

    yp = x[:n_p].reshape(bp, tp, D_MODEL)
    ys = x[n_p:].reshape(bs, ts, D_MODEL)
    st = lambda group, i: jnp.stack([g[i] for g in group], axis=0)
    return (yp, ys,
            st(p_states, 0), st(p_states, 1), st(p_states, 2), st(p_states, 3), st(p_states, 4), st(p_states, 5), st(p_states, 6),
            st(s_states, 0), st(s_states, 1), st(s_states, 2), st(s_states, 3), st(s_states, 4), st(s_states, 5), st(s_states, 6))
```

```python
import functools

import jax
import jax.numpy as jnp
from jax import lax
from jax.experimental import pallas as pl
from jax.experimental.pallas import tpu as pltpu

F32 = jnp.float32
BF16 = jnp.bfloat16

D_MODEL = 1024
DEPTH = 2
PAST_LEN = 4096
CHUNK = 64
HEAD_DIM = 64
N_HEADS = 4
MIX_W = N_HEADS * HEAD_DIM
A_KV_W = 128
A_PREV_CHUNKS = 2
D_PREV_CHUNKS = 8
D_REL_CLIP = 128
B_PROJ = 1024
C_PROJ = 1024
IN_PROJ = 3328
B_GN_EPS = 64e-5
C_GN_EPS = 1e-6
NORM_EPS = 1e-6
ATTN_SCALE = 0.125
ROPE_BASE = 10000.0
D_FF = 2816
N_EXPERTS = 8
E_FF = 3584
NEG_BIG = -1e30

VMEM_LIMIT = 48 * 1024 * 1024

NN = ((1,), (0,))
NT = ((1,), (1,))
TN = ((0,), (0,))


def _dg(a, b, dims=NN):
    return lax.dot_general(a, b, (dims, ((), ())), preferred_element_type=F32)


def _parts(x, n):
    out = []
    r = x
    for i in range(n):
        p = r.astype(BF16)
        out.append(p)
        if i + 1 < n:
            r = r - p.astype(F32)
    return out


def _mm(a, b, dims=NN, passes=1):
    if passes == 1:
        return _dg(a.astype(BF16), b.astype(BF16), dims)
    ah, al = _parts(a, 2)
    bh, bl = _parts(b, 2)
    return _dg(ah, bh, dims) + (_dg(ah, bl, dims) + _dg(al, bh, dims))


def _mm_exact_rhs(a, b_bf, dims=NN, n=3):
    acc = None
    for p in _parts(a, n):
        t = _dg(p, b_bf, dims)
        acc = t if acc is None else acc + t
    return acc


def _iota2(shape, dim):
    return lax.broadcasted_iota(jnp.int32, shape, dim)


def _head_mask(rows, cols=MIX_W):
    return (_iota2((rows, cols), 0) >> 6) == (_iota2((rows, cols), 1) >> 6)


def _seg_matrix(width, value):
    m = _head_mask(width, width)
    return jnp.where(m, value, 0.0).astype(BF16)


def _tile4(z):
    return jnp.concatenate([z, z, z, z], axis=0)


def _fold4(z):
    return (z[0:64] + z[64:128]) + (z[128:192] + z[192:256])


def _sigmoid(x):
    return 1.0 / (1.0 + jnp.exp(-x))


def _cparams(sem):
    return pltpu.CompilerParams(dimension_semantics=sem, vmem_limit_bytes=VMEM_LIMIT)


def _alias_kwargs(alias, n_inputs):
    if alias is None:
        return [], [], {}
    return [alias], [pl.BlockSpec(memory_space=pl.ANY)], {"input_output_aliases": {n_inputs: 0}}


IN_TM = 512


def _two_source_specs(xa, xb, n):
    na, nb = xa.shape[0] // IN_TM, xb.shape[0] // IN_TM
    spec_a = pl.BlockSpec((IN_TM, D_MODEL), lambda i: (jnp.minimum(i, na - 1), 0))
    spec_b = pl.BlockSpec((IN_TM, D_MODEL), lambda i: (jnp.clip(i - na, 0, nb - 1), 0))
    return na, n // IN_TM, spec_a, spec_b


def _inproj_kernel(xa_ref, xb_ref, g_ref, w_ref, aqg_ref, akg_ref, dqg_ref, dkg_ref,
                   aq_ref, ak_ref, av_ref, hb_ref, hc_ref, dq_ref, dk_ref, dv_ref, *, n_first):
    x = jnp.where(pl.program_id(0) < n_first, xa_ref[...], xb_ref[...])
    ms = jnp.mean(x * x, axis=-1, keepdims=True)
    xn = ((x * lax.rsqrt(ms + NORM_EPS)) * g_ref[...]).astype(BF16)
    seg = _seg_matrix(MIX_W, 1.0 / HEAD_DIM)

    def proj(lo, hi):
        return jnp.dot(xn, w_ref[:, lo:hi], preferred_element_type=F32)

    def head_rms(h, gain_ref):
        w = h.shape[-1]
        msq = _mm_exact_rhs(h * h, seg[:w, :w], n=2)
        return (h * lax.rsqrt(msq + NORM_EPS)) * gain_ref[...]

    aq_ref[...] = head_rms(proj(0, 256), aqg_ref)
    ak_ref[...] = head_rms(proj(256, 384), akg_ref)
    av_ref[...] = proj(384, 512)
    hb_ref[...] = proj(512, 1536)
    hc_ref[...] = proj(1536, 2560)
    dq_ref[...] = head_rms(proj(2560, 2816), dqg_ref)
    dk_ref[...] = head_rms(proj(2816, 3072), dkg_ref)
    dv_ref[...] = proj(3072, 3328)


def _inproj(xa, xb, n, g, w_bf, aqg, akg, dqg, dkg):
    na, nblk, spec_a, spec_b = _two_source_specs(xa, xb, n)
    widths = (256, 128, 128, B_PROJ, C_PROJ, 256, 256, 256)
    row = lambda w: pl.BlockSpec((IN_TM, w), lambda i: (i, 0))
    full = lambda a: pl.BlockSpec(a.shape, lambda i: (0,) * a.ndim)
    return pl.pallas_call(
        functools.partial(_inproj_kernel, n_first=na),
        grid=(nblk,),
        in_specs=[spec_a, spec_b, full(g), full(w_bf), full(aqg), full(akg), full(dqg), full(dkg)],
        out_specs=[row(w) for w in widths],
        out_shape=[jax.ShapeDtypeStruct((n, w), F32) for w in widths],
        compiler_params=_cparams(("parallel",)),
        name="inproj",
    )(xa, xb, g, w_bf, aqg, akg, dqg, dkg)


def _outproj_kernel(xa_ref, xb_ref, oa_ref, ob_ref, oc_ref, od_ref, w_ref, g_ref, x1_ref, xn_ref, *, n_first):
    acc = jnp.where(pl.program_id(0) < n_first, xa_ref[...], xb_ref[...])
    for m, o_ref in enumerate((oa_ref, ob_ref, oc_ref, od_ref)):
        acc = acc + jnp.dot(o_ref[...].astype(BF16), w_ref[m * MIX_W:(m + 1) * MIX_W, :],
                            preferred_element_type=F32)
    x1_ref[...] = acc
    ms = jnp.mean(acc * acc, axis=-1, keepdims=True)
    xn_ref[...] = ((acc * lax.rsqrt(ms + NORM_EPS)) * g_ref[...]).astype(BF16)


def _outproj(xa, xb, n, oa, ob, oc, od, w_bf, g2):
    na, nblk, spec_a, spec_b = _two_source_specs(xa, xb, n)
    row = lambda w: pl.BlockSpec((IN_TM, w), lambda i: (i, 0))
    full = lambda a: pl.BlockSpec(a.shape, lambda i: (0,) * a.ndim)
    return pl.pallas_call(
        functools.partial(_outproj_kernel, n_first=na),
        grid=(nblk,),
        in_specs=[spec_a, spec_b, row(MIX_W), row(MIX_W), row(MIX_W), row(MIX_W), full(w_bf), full(g2)],
        out_specs=[row(D_MODEL), row(D_MODEL)],
        out_shape=[jax.ShapeDtypeStruct((n, D_MODEL), F32), jax.ShapeDtypeStruct((n, D_MODEL), BF16)],
        compiler_params=_cparams(("parallel",)),
        name="outproj",
    )(xa, xb, oa, ob, oc, od, w_bf, g2)


FFN_TM = 512
FFN_TF = 1408


def _ffn_kernel(xn_ref, x1_ref, w1_ref, w3_ref, w2_ref, o_ref):
    f = pl.program_id(1)
    xn = xn_ref[...]
    a = jnp.dot(xn, w1_ref[...], preferred_element_type=F32)
    b = jnp.dot(xn, w3_ref[...], preferred_element_type=F32)
    h = ((a * _sigmoid(a)) * b).astype(BF16)
    y = jnp.dot(h, w2_ref[...], preferred_element_type=F32)

    @pl.when(f == 0)
    def _():
        o_ref[...] = x1_ref[...] + y

    @pl.when(f != 0)
    def _():
        o_ref[...] += y


def _ffn(xn, x1, w1_bf, w3_bf, w2_bf):
    n = xn.shape[0]
    return pl.pallas_call(
        _ffn_kernel,
        grid=(n // FFN_TM, D_FF // FFN_TF),
        in_specs=[
            pl.BlockSpec((FFN_TM, D_MODEL), lambda i, f: (i, 0)),
            pl.BlockSpec((FFN_TM, D_MODEL), lambda i, f: (i, 0)),
            pl.BlockSpec((D_MODEL, FFN_TF), lambda i, f: (0, f)),
            pl.BlockSpec((D_MODEL, FFN_TF), lambda i, f: (0, f)),
            pl.BlockSpec((FFN_TF, D_MODEL), lambda i, f: (f, 0)),
        ],
        out_specs=pl.BlockSpec((FFN_TM, D_MODEL), lambda i, f: (i, 0)),
        out_shape=jax.ShapeDtypeStruct((n, D_MODEL), F32),
        compiler_params=_cparams(("parallel", "arbitrary")),
        name="ffn",
    )(xn, x1, w1_bf, w3_bf, w2_bf)


ROUTER_LANES = 128


def _router_kernel(xn_ref, r_ref, gate_ref, rank_ref, cnt_ref, cnt_scr):
    @pl.when(pl.program_id(0) == 0)
    def _():
        cnt_scr[...] = jnp.zeros_like(cnt_scr)

    xn = xn_ref[...].astype(F32)
    logits = _mm(xn, r_ref[...], passes=3)
    lane = _iota2(logits.shape, 1)
    logits = jnp.where(lane < N_EXPERTS, logits, NEG_BIG)
    m1 = jnp.max(logits, axis=-1, keepdims=True)
    i1 = jnp.min(jnp.where(logits == m1, lane, ROUTER_LANES), axis=-1, keepdims=True)
    rest = jnp.where(lane == i1, NEG_BIG, logits)
    m2 = jnp.max(rest, axis=-1, keepdims=True)
    i2 = jnp.min(jnp.where(rest == m2, lane, ROUTER_LANES), axis=-1, keepdims=True)
    e2 = jnp.exp(m2 - m1)
    den = 1.0 + e2
    gates = jnp.where(lane == i1, 1.0 / den, 0.0) + jnp.where(lane == i2, e2 / den, 0.0)
    gate_ref[...] = gates
    sel = jnp.where(gates > 0.0, 1.0, 0.0)
    tm = sel.shape[0]
    before = jnp.where(_iota2((tm, tm), 1) < _iota2((tm, tm), 0), 1.0, 0.0).astype(BF16)
    rank_ref[...] = (_dg(before, sel.astype(BF16)) + cnt_scr[...]).astype(jnp.int32)
    cnt_scr[...] = cnt_scr[...] + jnp.sum(sel, axis=0, keepdims=True)
    cnt_ref[...] = cnt_scr[...].astype(jnp.int32)


def _router(xn, router_pad):
    n = xn.shape[0]
    row = pl.BlockSpec((IN_TM, ROUTER_LANES), lambda i: (i, 0))
    return pl.pallas_call(
        _router_kernel,
        grid=(n // IN_TM,),
        in_specs=[pl.BlockSpec((IN_TM, D_MODEL), lambda i: (i, 0)),
                  pl.BlockSpec(router_pad.shape, lambda i: (0, 0))],
        out_specs=[row, row, pl.BlockSpec((1, ROUTER_LANES), lambda i: (0, 0))],
        out_shape=[jax.ShapeDtypeStruct((n, ROUTER_LANES), F32),
                   jax.ShapeDtypeStruct((n, ROUTER_LANES), jnp.int32),
                   jax.ShapeDtypeStruct((1, ROUTER_LANES), jnp.int32)],
        scratch_shapes=[pltpu.VMEM((1, ROUTER_LANES), F32)],
        compiler_params=_cparams(("arbitrary",)),
        name="router",
    )(xn, router_pad)


MOE_R = 512
MOE_TT = 512
MOE_TF = 1792


def _moe_plan(gates, rank, counts, n):
    n_tiles = n // MOE_TT
    n_blocks = (2 * n) // MOE_R + N_EXPERTS
    sel = gates[:, :N_EXPERTS] > 0.0
    rank = rank[:, :N_EXPERTS]
    counts = counts[0, :N_EXPERTS]
    padded = ((counts + MOE_R - 1) // MOE_R) * MOE_R
    pad_end = jnp.cumsum(padded)
    pad_start = pad_end - padded
    pos = jnp.where(sel, pad_start[None, :] + rank, -1)
    tile_rank = rank[::MOE_TT]
    tile_cnt = jnp.concatenate([tile_rank[1:], counts[None, :]], axis=0) - tile_rank
    start = pad_start[None, :] + tile_rank
    b0 = jnp.minimum(start // MOE_R, n_blocks - 1)
    b1 = jnp.minimum((start + jnp.maximum(tile_cnt, 1) - 1) // MOE_R, n_blocks - 1)
    v0 = tile_cnt > 0
    v1 = jnp.logical_and(v0, b1 != b0)
    blocks = jnp.stack([b0, b1], axis=-1).astype(jnp.int32)
    valid = jnp.stack([v0, v1], axis=-1).astype(jnp.int32)
    block_expert = jnp.minimum(
        jnp.sum(pad_end[None, :] <= (jnp.arange(n_blocks) * MOE_R)[:, None], axis=1), N_EXPERTS - 1)
    return dict(
        n_tiles=n_tiles, n_blocks=n_blocks,
        pos_rows=jnp.transpose(pos).astype(jnp.int32),
        pos_cols=jnp.pad(pos.astype(F32), ((0, 0), (0, ROUTER_LANES - N_EXPERTS)), constant_values=-1.0),
        g_blocks=jnp.transpose(blocks, (1, 0, 2)).reshape(-1), g_valid=jnp.transpose(valid, (1, 0, 2)).reshape(-1),
        c_blocks=blocks.reshape(-1), c_valid=valid.reshape(-1),
        block_expert=block_expert.astype(jnp.int32), n_used=(pad_end[-1:] // MOE_R).astype(jnp.int32))


def _moe_gather_kernel(blk_ref, val_ref, xn_ref, gate_ref, pos_ref, xs_ref, gs_ref, *, n_tiles):
    e, i, s = pl.program_id(0), pl.program_id(1), pl.program_id(2)
    q = (e * n_tiles + i) * 2 + s
    blk = blk_ref[q]
    first = jnp.logical_or(q == 0, blk != blk_ref[jnp.maximum(q - 1, 0)])

    @pl.when(first)
    def _():
        xs_ref[...] = jnp.zeros_like(xs_ref)
        gs_ref[...] = jnp.zeros_like(gs_ref)

    @pl.when(val_ref[q] > 0)
    def _():
        row_id = blk * MOE_R + _iota2((MOE_R, MOE_TT), 0)
        pick = jnp.where(row_id == pos_ref[pl.ds(e, 1), :], 1.0, 0.0).astype(BF16)
        xs_ref[...] += _dg(pick, xn_ref[...]).astype(BF16)
        g = gate_ref[...]
        gs_ref[...] += sum(_dg(pick, p) for p in _parts(g, 3))


def _moe_gather(plan, xn, gates):
    n_tiles, n_blocks = plan["n_tiles"], plan["n_blocks"]
    out_map = lambda e, i, s, blk, val: (blk[(e * n_tiles + i) * 2 + s], 0)
    grid_spec = pltpu.PrefetchScalarGridSpec(
        num_scalar_prefetch=2,
        grid=(N_EXPERTS, n_tiles, 2),
        in_specs=[pl.BlockSpec((MOE_TT, D_MODEL), lambda e, i, s, blk, val: (i, 0)),
                  pl.BlockSpec((MOE_TT, ROUTER_LANES), lambda e, i, s, blk, val: (i, 0)),
                  pl.BlockSpec((N_EXPERTS, MOE_TT), lambda e, i, s, blk, val: (0, i))],
        out_specs=[pl.BlockSpec((MOE_R, D_MODEL), out_map), pl.BlockSpec((MOE_R, ROUTER_LANES), out_map)])
    return pl.pallas_call(
        functools.partial(_moe_gather_kernel, n_tiles=n_tiles),
        grid_spec=grid_spec,
        out_shape=[jax.ShapeDtypeStruct((n_blocks * MOE_R, D_MODEL), BF16),
                   jax.ShapeDtypeStruct((n_blocks * MOE_R, ROUTER_LANES), F32)],
        compiler_params=_cparams(("arbitrary", "arbitrary", "arbitrary")),
        name="moe_gather",
    )(plan["g_blocks"], plan["g_valid"], xn, gates, plan["pos_rows"])


def _moe_expert_kernel(be_ref, nu_ref, xs_ref, gs_ref, w1_ref, w3_ref, w2_ref, os_ref, acc_ref):
    j, f = pl.program_id(0), pl.program_id(1)
    used = j < nu_ref[0]

    @pl.when(used)
    def _():
        x = xs_ref[...]
        a = jnp.dot(x, w1_ref[...], preferred_element_type=F32)
        b = jnp.dot(x, w3_ref[...], preferred_element_type=F32)
        h = ((a * _sigmoid(a)) * b).astype(BF16)
        y = jnp.dot(h, w2_ref[...], preferred_element_type=F32)

        @pl.when(f == 0)
        def _():
            acc_ref[...] = y

        @pl.when(f != 0)
        def _():
            acc_ref[...] += y

    @pl.when(f == pl.num_programs(1) - 1)
    def _():
        lane = _iota2(gs_ref.shape, 1)
        g = jnp.sum(jnp.where(lane == be_ref[j], gs_ref[...], 0.0), axis=1, keepdims=True)
        os_ref[...] = jnp.where(used, acc_ref[...] * g, 0.0).astype(BF16)


def _moe_experts(plan, xs, gs, w1_bf, w3_bf, w2_bf):
    n_blocks = plan["n_blocks"]
    grid_spec = pltpu.PrefetchScalarGridSpec(
        num_scalar_prefetch=2,
        grid=(n_blocks, E_FF // MOE_TF),
        in_specs=[pl.BlockSpec((MOE_R, D_MODEL), lambda j, f, be, nu: (j, 0)),
                  pl.BlockSpec((MOE_R, ROUTER_LANES), lambda j, f, be, nu: (j, 0)),
                  pl.BlockSpec((None, D_MODEL, MOE_TF), lambda j, f, be, nu: (be[j], 0, f)),
                  pl.BlockSpec((None, D_MODEL, MOE_TF), lambda j, f, be, nu: (be[j], 0, f)),
                  pl.BlockSpec((None, MOE_TF, D_MODEL), lambda j, f, be, nu: (be[j], f, 0))],
        out_specs=pl.BlockSpec((MOE_R, D_MODEL), lambda j, f, be, nu: (j, 0)),
        scratch_shapes=[pltpu.VMEM((MOE_R, D_MODEL), F32)])
    return pl.pallas_call(
        _moe_expert_kernel,
        grid_spec=grid_spec,
        out_shape=jax.ShapeDtypeStruct((n_blocks * MOE_R, D_MODEL), BF16),
        compiler_params=_cparams(("arbitrary", "arbitrary")),
        name="moe_experts",
    )(plan["block_expert"], plan["n_used"], xs, gs, w1_bf, w3_bf, w2_bf)


def _moe_combine_kernel(blk_ref, val_ref, x1_ref, pos_ref, os_ref, o_ref):
    i, e, s = pl.program_id(0), pl.program_id(1), pl.program_id(2)
    q = (i * N_EXPERTS + e) * 2 + s

    @pl.when(jnp.logical_and(e == 0, s == 0))
    def _():
        o_ref[...] = x1_ref[...]

    @pl.when(val_ref[q] > 0)
    def _():
        lane = _iota2(pos_ref.shape, 1)
        pos_col = jnp.sum(jnp.where(lane == e, pos_ref[...], 0.0), axis=1, keepdims=True)
        row_id = (blk_ref[q] * MOE_R + _iota2((MOE_TT, MOE_R), 1)).astype(F32)
        pick = jnp.where(row_id == pos_col, 1.0, 0.0).astype(BF16)
        o_ref[...] += _dg(pick, os_ref[...])


def _moe_combine(plan, x1, os_rows):
    n_tiles = plan["n_tiles"]
    q_of = lambda i, e, s: (i * N_EXPERTS + e) * 2 + s
    grid_spec = pltpu.PrefetchScalarGridSpec(
        num_scalar_prefetch=2,
        grid=(n_tiles, N_EXPERTS, 2),
        in_specs=[pl.BlockSpec((MOE_TT, D_MODEL), lambda i, e, s, blk, val: (i, 0)),
                  pl.BlockSpec((MOE_TT, ROUTER_LANES), lambda i, e, s, blk, val: (i, 0)),
                  pl.BlockSpec((MOE_R, D_MODEL), lambda i, e, s, blk, val: (blk[q_of(i, e, s)], 0))],
        out_specs=pl.BlockSpec((MOE_TT, D_MODEL), lambda i, e, s, blk, val: (i, 0)))
    return pl.pallas_call(
        _moe_combine_kernel,
        grid_spec=grid_spec,
        out_shape=jax.ShapeDtypeStruct(x1.shape, F32),
        compiler_params=_cparams(("arbitrary", "arbitrary", "arbitrary")),
        name="moe_combine",
    )(plan["c_blocks"], plan["c_valid"], x1, plan["pos_cols"], os_rows)


def _moe(xn, x1, router_pad, w1_bf, w3_bf, w2_bf):
    gates, rank, counts = _router(xn, router_pad)
    plan = _moe_plan(gates, rank, counts, xn.shape[0])
    xs, gs = _moe_gather(plan, xn, gates)
    os_rows = _moe_experts(plan, xs, gs, w1_bf, w3_bf, w2_bf)
    return _moe_combine(plan, x1, os_rows)


def _relbias_kernel(rb_ref, o_ref, *, nk):
    h = pl.program_id(0)
    q = _iota2((CHUNK, nk), 0)
    r = _iota2((CHUNK, nk), 1)
    idx = jnp.clip(q - (r - (nk - CHUNK)), -D_REL_CLIP, D_REL_CLIP) + D_REL_CLIP

    def body(j, acc):
        return jnp.where(idx == j, rb_ref[h, j], acc)

    o_ref[...] = lax.fori_loop(0, 2 * D_REL_CLIP + 1, body, jnp.zeros((CHUNK, nk), F32))


def _relbias_table(rel_bias, nk):
    return pl.pallas_call(
        functools.partial(_relbias_kernel, nk=nk),
        grid=(N_HEADS,),
        in_specs=[pl.BlockSpec(memory_space=pltpu.SMEM)],
        out_specs=pl.BlockSpec((CHUNK, nk), lambda h: (h, 0)),
        out_shape=jax.ShapeDtypeStruct((N_HEADS * CHUNK, nk), F32),
        name="relbias",
    )(rel_bias)


ATTN_QB = 8


def _attn_kernel(q_ref, kp_ref, kc_ref, vp_ref, vc_ref, x_ref, *rest, qb, n_prev, use_sink, mask_first):
    o_ref, kbuf, vbuf = rest[-3:]
    i = pl.program_id(1)
    p_rows = kp_ref.shape[0]
    nk = (n_prev + 1) * CHUNK
    wk = kp_ref.shape[1]

    if wk == MIX_W:
        stage = lambda ref: ref[...].astype(BF16)
    else:
        src = _iota2((wk, MIX_W), 0)
        dst = _iota2((wk, MIX_W), 1)
        expand = jnp.where(src == (dst >> 7) * HEAD_DIM + (dst & (HEAD_DIM - 1)), 1.0, 0.0).astype(BF16)
        stage = lambda ref: jnp.dot(ref[...].astype(BF16), expand, preferred_element_type=F32).astype(BF16)

    kbuf[0:p_rows, :] = stage(kp_ref)
    kbuf[p_rows:, :] = stage(kc_ref)
    vbuf[0:p_rows, :] = stage(vp_ref)
    vbuf[p_rows:, :] = stage(vc_ref)

    hmask = _head_mask(N_HEADS * CHUNK)
    extra = x_ref[...]

    def body(j, carry):
        r0 = pl.multiple_of(j * CHUNK, CHUNK)
        base = pl.multiple_of(p_rows + (j - n_prev) * CHUNK, CHUNK)
        qj = q_ref[pl.ds(r0, CHUNK), :] * ATTN_SCALE
        qs = jnp.where(hmask, _tile4(qj), 0.0).astype(BF16)
        s = _dg(qs, kbuf[pl.ds(base, nk), :], NT)
        if not use_sink:
            s = s + extra
        if mask_first:
            krow = base + _iota2(s.shape, 1)
            s = jnp.where(jnp.logical_and(i == 0, krow < p_rows), NEG_BIG, s)
        m = jnp.max(s, axis=-1, keepdims=True)
        if use_sink:
            m = jnp.maximum(m, extra)
        e = jnp.exp(s - m)
        den = jnp.sum(e, axis=-1, keepdims=True)
        if use_sink:
            den = den + jnp.exp(extra - m)
        o_all = _dg(e.astype(BF16), vbuf[pl.ds(base, nk), :]) * (1.0 / den)
        o_ref[pl.ds(r0, CHUNK), :] = _fold4(jnp.where(hmask, o_all, 0.0))
        return carry

    lax.fori_loop(0, qb, body, 0)


def _attention(q, k, v, prev, extra, *, n_prev, use_sink, n_streams, t, base_row, alias=None):
    n = q.shape[0]
    wk = k.shape[-1]
    if prev is None:
        qb = ATTN_QB
        rows = qb * CHUNK
        nblk = t // rows
        base = base_row // rows
        cur_map = lambda s, i: (base + s * nblk + i, 0)
        prev_spec = pl.BlockSpec((rows, wk), lambda s, i: (base + s * nblk + jnp.maximum(i - 1, 0), 0))
        k_prev, v_prev, p_rows, mask_first = k, v, rows, True
    else:
        qb, rows, nblk = t // CHUNK, t, 1
        base = base_row // rows
        cur_map = lambda s, i: (base + s, 0)
        k_prev, v_prev = prev
        p_rows = k_prev.shape[1]
        prev_spec = pl.BlockSpec((None, p_rows, wk), lambda s, i: (s, 0, 0))
        mask_first = False
    cur = lambda w: pl.BlockSpec((rows, w), cur_map)
    kern = functools.partial(_attn_kernel, qb=qb, n_prev=n_prev, use_sink=use_sink, mask_first=mask_first)
    a_in, a_spec, a_kw = _alias_kwargs(alias, 6)
    return pl.pallas_call(
        kern,
        grid=(n_streams, nblk),
        in_specs=[cur(MIX_W), prev_spec, cur(wk), prev_spec, cur(wk),
                  pl.BlockSpec(extra.shape, lambda s, i: (0, 0))] + a_spec,
        out_specs=cur(MIX_W),
        out_shape=jax.ShapeDtypeStruct((n, MIX_W), F32),
        scratch_shapes=[pltpu.VMEM((p_rows + rows, MIX_W), BF16), pltpu.VMEM((p_rows + rows, MIX_W), BF16)],
        compiler_params=_cparams(("parallel", "arbitrary")),
        name="attn_sink" if use_sink else "attn_bias",
        **a_kw,
    )(q, k_prev, k, v_prev, v, extra, *a_in)


def _head_layer_norm(o, seg_mean_bf, w, b, eps):
    mu = _mm_exact_rhs(o, seg_mean_bf)
    d = o - mu
    var = _mm_exact_rhs(d * d, seg_mean_bf)
    return (d * lax.rsqrt(var + eps)) * w + b


def _ret_kernel(hc_ref, cos_ref, sin_ref, s0_ref, dstack_ref, qsc_ref, ksc_ref, gam_ref, lnw_ref, lnb_ref,
                *rest, qb):
    o_ref, sout_ref, s_scr = rest[-3:]
    i = pl.program_id(1)

    @pl.when(i == 0)
    def _():
        s_scr[...] = s0_ref[...]

    hmask = _head_mask(N_HEADS * CHUNK)
    seg_mean = _seg_matrix(MIX_W, 1.0 / HEAD_DIM)
    first_half = (_iota2((CHUNK, MIX_W), 1) & (HEAD_DIM - 1)) < (HEAD_DIM // 2)

    def rope(x, cos, sin):
        partner = jnp.where(first_half, pltpu.roll(x, MIX_W - HEAD_DIM // 2, 1), pltpu.roll(x, HEAD_DIM // 2, 1))
        return x * cos + partner * sin

    def body(j, carry):
        r0 = pl.multiple_of(j * CHUNK, CHUNK)
        cos = cos_ref[pl.ds(r0, CHUNK), :]
        sin = sin_ref[pl.ds(r0, CHUNK), :]
        q = rope(hc_ref[pl.ds(r0, CHUNK), 0:256], cos, sin)
        k = rope(hc_ref[pl.ds(r0, CHUNK), 256:512], cos, sin) * ATTN_SCALE
        v = hc_ref[pl.ds(r0, CHUNK), 512:768]
        g = hc_ref[pl.ds(r0, CHUNK), 768:1024]
        k_bf = k.astype(BF16)
        v_bf = v.astype(BF16)
        state = s_scr[...]
        qs = jnp.where(hmask, _tile4(q), 0.0).astype(BF16)
        sc = _dg(qs, k_bf, NT) * dstack_ref[...]
        intra = _fold4(jnp.where(hmask, _dg(sc.astype(BF16), v_bf), 0.0))
        inter = _dg((q * qsc_ref[...]).astype(BF16), state.astype(BF16))
        kv = _dg((k * ksc_ref[...]).astype(BF16), v_bf, TN)
        s_scr[...] = gam_ref[...] * state + jnp.where(hmask, kv, 0.0)
        y = _head_layer_norm(intra + inter, seg_mean, lnw_ref[...], lnb_ref[...], C_GN_EPS)
        o_ref[pl.ds(r0, CHUNK), :] = y * (g * _sigmoid(g))
        return carry

    lax.fori_loop(0, qb, body, 0)
    sout_ref[...] = s_scr[...]


def _retention(hc, cos, sin, s0_bd, tabs, lnw, lnb, *, n_streams, t, base_row, alias=None):
    n = hc.shape[0]
    qb = min(8, t // CHUNK)
    rows = qb * CHUNK
    nblk = t // rows
    base = base_row // rows
    dstack, qsc, ksc, gam = tabs
    full = lambda a: pl.BlockSpec(a.shape, lambda s, i: (0,) * a.ndim)
    cur = lambda w: pl.BlockSpec((rows, w), lambda s, i: (base + s * nblk + i, 0))
    state = pl.BlockSpec((None, MIX_W, MIX_W), lambda s, i: (s, 0, 0))
    a_in, a_spec, a_kw = _alias_kwargs(alias, 10)
    return pl.pallas_call(
        functools.partial(_ret_kernel, qb=qb),
        grid=(n_streams, nblk),
        in_specs=[cur(C_PROJ),
                  pl.BlockSpec((rows, MIX_W), lambda s, i: (i, 0)),
                  pl.BlockSpec((rows, MIX_W), lambda s, i: (i, 0)),
                  state, full(dstack), full(qsc), full(ksc), full(gam), full(lnw), full(lnb)] + a_spec,
        out_specs=[cur(MIX_W), state],
        out_shape=[jax.ShapeDtypeStruct((n, MIX_W), F32),
                   jax.ShapeDtypeStruct((n_streams, MIX_W, MIX_W), F32)],
        scratch_shapes=[pltpu.VMEM((MIX_W, MIX_W), F32)],
        compiler_params=_cparams(("parallel", "arbitrary")),
        name="retention",
        **a_kw,
    )(hc, cos, sin, s0_bd, dstack, qsc, ksc, gam, lnw, lnb, *a_in)


def _retention_tables():
    hh = jnp.arange(N_HEADS, dtype=F32)
    log_gamma = jnp.log(1.0 - 2.0 ** (-5.0 - hh))
    t = jnp.arange(CHUNK)
    diff = t[:, None] - t[None, :]
    dmat = jnp.where(diff >= 0, jnp.exp(log_gamma[:, None, None] * jnp.maximum(diff, 0)), 0.0)
    dstack = dmat.reshape(N_HEADS * CHUNK, CHUNK)
    lanes = lambda per_head: jnp.repeat(per_head, HEAD_DIM, axis=-1)
    qsc = lanes(jnp.exp((t + 1)[:, None] * log_gamma[None, :]))
    ksc = lanes(jnp.exp((CHUNK - 1 - t)[:, None] * log_gamma[None, :]))
    gam = jnp.broadcast_to(lanes(jnp.exp(log_gamma * CHUNK))[:, None], (MIX_W, MIX_W))
    return dstack.astype(F32), qsc.astype(F32), ksc.astype(F32), gam.astype(F32)


def _rope_tables(pos):
    half = HEAD_DIM // 2
    theta = 1.0 / (ROPE_BASE ** jnp.linspace(0.0, 1.0, half, dtype=F32))
    ang = pos.astype(F32)[:, None] * theta[None, :]
    cos, sin = jnp.cos(ang), jnp.sin(ang)
    cos_t = jnp.tile(jnp.concatenate([cos, cos], axis=-1), (1, N_HEADS))
    sin_t = jnp.tile(jnp.concatenate([-sin, sin], axis=-1), (1, N_HEADS))
    return cos_t, sin_t


DECAY_SCALE = 0.6065306597126334
RWKV_CB = 4
N_LEVELS = 6
MASK_HEAD, MASK_STRICT, MASK_INCL, MASK_LEVEL0 = 0, 1, 2, 3


def _rwkv_masks():
    n4 = N_HEADS * CHUNK
    ri = jnp.arange(n4)[:, None]
    ci = jnp.arange(n4)[None, :]
    head = (ri >> 6) == (ci >> 6)
    tabs = [head, head & ((ci & 63) < (ri & 63)), head & ((ci & 63) <= (ri & 63))]
    for log_m in range(N_LEVELS):
        same = (ri >> (log_m + 1)) == (ci >> (log_m + 1))
        tabs.append(same & (((ri >> log_m) & 1) == 1) & (((ci >> log_m) & 1) == 0))
    return jnp.stack(tabs).astype(BF16)


def _rwkv_kernel(hb_ref, shift0_ref, h0_ref, masks_ref, mu_ref, w0_ref, w2_ref, a0_ref, a2_ref, g2_ref,
                 kk_ref, ka_ref, rk_ref, lnw_ref, lnb_ref, *rest, cb, independent):
    o_ref, hout_ref, h_scr, shift_scr = rest[-4:]
    c = pl.program_id(1)
    rows = cb * CHUNK
    xb = hb_ref[...]
    row = _iota2(xb.shape, 0)
    prev = pltpu.roll(xb, 1, 0)
    if independent:
        for j in range(cb):
            prev = jnp.where(row == j * CHUNK, shift0_ref[j], prev)
    else:
        @pl.when(c == 0)
        def _():
            h_scr[...] = h0_ref[0]
            shift_scr[...] = shift0_ref[0]

        prev = jnp.where(row == 0, shift_scr[...], prev)
        shift_scr[...] = xb[rows - 1:rows, :]
    xs = xb + mu_ref[...] * (prev - xb)
    r = xs[:, 0:256]
    k = xs[:, 256:512]
    v = xs[:, 512:768]
    xw = xs[:, 768:832]
    xa = xs[:, 832:896]
    xg = xs[:, 896:1024]

    z = w0_ref[...] + _mm(jnp.tanh(xw), w2_ref[...], passes=3)
    lw = -DECAY_SCALE * _sigmoid(z)
    a_gate = _sigmoid(a0_ref[...] + _mm(xa, a2_ref[...], passes=3))
    gate = _mm(_sigmoid(xg), g2_ref[...], passes=1)

    seg_sum = _seg_matrix(MIX_W, 1.0)
    seg_mean = _seg_matrix(MIX_W, 1.0 / HEAD_DIM)
    kkn = k * kk_ref[...]
    norm = jnp.sqrt(_mm_exact_rhs(kkn * kkn, seg_sum))
    kk = kkn / jnp.maximum(norm, 1e-12)
    kf = k * (1.0 + (a_gate - 1.0) * ka_ref[...])

    tt = _iota2((rows, rows), 0)
    ss = _iota2((rows, rows), 1)
    tril = jnp.where(jnp.logical_and(ss <= tt, (ss >> 6) == (tt >> 6)), 1.0, 0.0).astype(BF16)
    lw_parts = _parts(lw, 3)
    cum = _dg(tril, lw_parts[0]) + (_dg(tril, lw_parts[1]) + _dg(tril, lw_parts[2]))
    w_inv = jnp.exp(-cum)
    rho = (r * jnp.exp(cum)).astype(BF16)
    alpha = (-kk * jnp.exp(cum - lw)).astype(BF16)
    beta = ((kk * a_gate) * w_inv).astype(BF16)
    kappa = (kf * w_inv).astype(BF16)
    v_bf = v.astype(BF16)
    ones = jnp.ones((CHUNK, MIX_W), BF16)

    hmask = masks_ref[MASK_HEAD]
    n4 = N_HEADS * CHUNK
    eye = jnp.where(_iota2((n4, n4), 0) == _iota2((n4, n4), 1), 1.0, 0.0)

    pre = []
    for j in range(cb):
        sl = slice(j * CHUNK, (j + 1) * CHUNK)
        bd = lambda zz: _tile4(zz[sl]) * hmask
        al_bd, be_bd, ka_bd, rh_bd, v_bd = bd(alpha), bd(beta), bd(kappa), bd(rho), bd(v_bf)
        a_bf = _dg(al_bd, be_bd, NT).astype(BF16) * masks_ref[MASK_STRICT]
        a_ak = _dg(al_bd, ka_bd, NT).astype(BF16) * masks_ref[MASK_STRICT]
        b_rb = _dg(rh_bd, be_bd, NT).astype(BF16) * masks_ref[MASK_INCL]
        b_rk = _dg(rh_bd, ka_bd, NT).astype(BF16) * masks_ref[MASK_INCL]
        t_inv = eye + (a_bf * masks_ref[MASK_LEVEL0]).astype(F32)
        for lvl in range(1, N_LEVELS):
            t_bf = t_inv.astype(BF16)
            e_mat = _dg(a_bf * masks_ref[MASK_LEVEL0 + lvl], t_bf)
            t_inv = t_inv + _dg(t_bf, e_mat.astype(BF16))
        x0 = _dg(a_ak, v_bd)
        y0 = _dg(b_rk, v_bd)
        hn0 = _dg(ka_bd, v_bd, TN)
        ctot = (_dg(lw_parts[0][sl], ones, TN)
                + (_dg(lw_parts[1][sl], ones, TN) + _dg(lw_parts[2][sl], ones, TN)))
        pre.append((al_bd, be_bd, rh_bd, b_rb, t_inv.astype(BF16), x0, y0, hn0, jnp.exp(ctot)))

    ys = []
    h = None if independent else h_scr[...]
    for j in range(cb):
        al_bd, be_bd, rh_bd, b_rb, t_bf, x0, y0, hn0, wc = pre[j]
        h0 = h0_ref[j] if independent else h
        h0_bf = h0.astype(BF16)
        x_mat = _dg(al_bd, h0_bf) + x0
        u_bf = _dg(t_bf, x_mat.astype(BF16)).astype(BF16)
        y_bd = _dg(rh_bd, h0_bf) + _dg(b_rb, u_bf) + y0
        h_new = wc * (h0 + _dg(be_bd, u_bf, TN) + hn0)
        ys.append(_fold4(y_bd))
        if independent:
            hout_ref[j] = h_new
        else:
            h = h_new
    if not independent:
        h_scr[...] = h
        hout_ref[0] = h

    y = _head_layer_norm(jnp.concatenate(ys, axis=0), seg_mean, lnw_ref[...], lnb_ref[...], B_GN_EPS)
    bonus = _mm_exact_rhs(r * kf * rk_ref[...], seg_sum) * v
    o_ref[...] = (y + bonus) * gate


def _rwkv(hb, shift0, h0_bd, params, *, independent, n_streams, t, base_row, alias=None):
    n = hb.shape[0]
    cb = RWKV_CB
    rows = cb * CHUNK
    nblk = t // rows
    base = base_row // rows
    masks = _rwkv_masks()
    full = lambda a: pl.BlockSpec(a.shape, lambda s, c: (0,) * a.ndim)
    if independent:
        assert n_streams == 1
        st_map = lambda s, c: (c, 0, 0)
        n_state, st_blk = t // CHUNK, cb
    else:
        st_map = lambda s, c: (s, 0, 0)
        n_state, st_blk = n_streams, 1
    cur = lambda w: pl.BlockSpec((rows, w), lambda s, c: (base + s * nblk + c, 0))
    a_in, a_spec, a_kw = _alias_kwargs(alias, 4 + len(params))
    return pl.pallas_call(
        functools.partial(_rwkv_kernel, cb=cb, independent=independent),
        grid=(n_streams, nblk),
        in_specs=[cur(B_PROJ),
                  pl.BlockSpec((st_blk, 1, B_PROJ), st_map),
                  pl.BlockSpec((st_blk, MIX_W, MIX_W), st_map),
                  full(masks)] + [full(p) for p in params] + a_spec,
        out_specs=[cur(MIX_W), pl.BlockSpec((st_blk, MIX_W, MIX_W), st_map)],
        out_shape=[jax.ShapeDtypeStruct((n, MIX_W), F32),
                   jax.ShapeDtypeStruct((n_state, MIX_W, MIX_W), F32)],
        scratch_shapes=[pltpu.VMEM((MIX_W, MIX_W), F32), pltpu.VMEM((1, B_PROJ), F32)],
        compiler_params=_cparams(("parallel", "arbitrary")),
        name="rwkv7",
        **a_kw,
    )(hb, shift0, h0_bd, masks, *params, *a_in)


def _to_block_diag(s):
    eye = jnp.eye(N_HEADS, dtype=s.dtype)
    out = s[:, :, :, None, :] * eye[None, :, None, :, None]
    return out.reshape(s.shape[0], MIX_W, MIX_W)


def _from_block_diag(m):
    b = m.reshape(m.shape[0], N_HEADS, HEAD_DIM, N_HEADS, HEAD_DIM)
    return jnp.stack([b[:, h, :, h, :] for h in range(N_HEADS)], axis=1)


def _row(p):
    return p.reshape(1, -1).astype(F32)


def _mixers(proj, caches, lp, tabs, geom):
    aq, ak, av, hb, hc, dq, dk, dv = proj
    bp, tp, bs, ts = geom
    n_p = bp * tp
    ca_k, ca_v, sb_shift, sb_wkv, sc, cd_k, cd_v = caches
    pr = dict(n_streams=bp, t=tp, base_row=0)
    sm = dict(n_streams=bs, t=ts, base_row=n_p)
    zeros_state = jnp.zeros((bp, MIX_W, MIX_W), F32)

    oa = _attention(aq, ak, av, None, lp["sink_col"], n_prev=A_PREV_CHUNKS, use_sink=True, **pr)
    oa = _attention(aq, ak, av, (ca_k.reshape(bs, -1, A_KV_W), ca_v.reshape(bs, -1, A_KV_W)), lp["sink_col"],
                    n_prev=A_PREV_CHUNKS, use_sink=True, alias=oa, **sm)

    ob, h_p = _rwkv(hb, jnp.zeros((bp, 1, B_PROJ), F32), zeros_state, lp["rwkv"], independent=False, **pr)
    h0_s = _to_block_diag(jnp.swapaxes(sb_wkv, -1, -2))
    ob, h_s = _rwkv(hb, sb_shift.reshape(bs, 1, B_PROJ), h0_s, lp["rwkv"], independent=True,
                    n_streams=1, t=bs * ts, base_row=n_p, alias=ob)

    oc, s_p = _retention(hc, *tabs["rope_prompt"], zeros_state, tabs["ret"], lp["c_ln_w"], lp["c_ln_b"], **pr)
    oc, s_s = _retention(hc, *tabs["rope_sample"], _to_block_diag(sc), tabs["ret"], lp["c_ln_w"], lp["c_ln_b"],
                         alias=oc, **sm)

    od = _attention(dq, dk, dv, None, lp["bias_table"], n_prev=D_PREV_CHUNKS, use_sink=False, **pr)
    od = _attention(dq, dk, dv, (cd_k.reshape(bs, -1, MIX_W), cd_v.reshape(bs, -1, MIX_W)), lp["bias_table"],
                    n_prev=D_PREV_CHUNKS, use_sink=False, alias=od, **sm)

    wkv = lambda h: jnp.swapaxes(_from_block_diag(h), -1, -2)
    return (oa, ob, oc, od), (wkv(h_p), _from_block_diag(s_p)), (wkv(h_s), _from_block_diag(s_s))


def kernel(x_prompt, x_sample, cache_a_k, cache_a_v, state_b_shift, state_b_wkv, state_c, cache_d_k, cache_d_v,
           norm1_g, norm2_g, w_in, w_out, a_q_norm, a_k_norm, a_sinks, b_mu, b_w0, b_w2, b_a0, b_a2, b_g2,
           b_k_k, b_k_a, b_r_k, b_ln_w, b_ln_b, c_ln_w, c_ln_b, d_q_norm, d_k_norm, d_rel_bias,
           ffn_w1, ffn_w3, ffn_w2, moe_router, moe_w1, moe_w3, moe_w2):
    bp, tp, _ = x_prompt.shape
    bs, ts, _ = x_sample.shape
    assert ts == CHUNK
    n_p, n_s = bp * tp, bs * ts
    geom = (bp, tp, bs, ts)
    xa = x_prompt.reshape(n_p, D_MODEL)
    xb = x_sample.reshape(n_s, D_MODEL)

    tabs = {
        "ret": _retention_tables(),
        "rope_prompt": _rope_tables(jnp.arange(tp)),
        "rope_sample": _rope_tables(PAST_LEN + jnp.arange(ts)),
    }
    tile = lambda g: _row(jnp.tile(g, MIX_W // HEAD_DIM))

    p_states, s_states = [], []
    for l in range(DEPTH):
        lp = {
            "sink_col": jnp.repeat(a_sinks[l].astype(F32), CHUNK).reshape(N_HEADS * CHUNK, 1),
            "bias_table": _relbias_table(d_rel_bias[l].astype(F32), (D_PREV_CHUNKS + 1) * CHUNK),
            "rwkv": (_row(b_mu[l]), _row(b_w0[l]), b_w2[l], _row(b_a0[l]), b_a2[l], b_g2[l], _row(b_k_k[l]),
                     _row(b_k_a[l]), _row(b_r_k[l]), _row(b_ln_w[l]), _row(b_ln_b[l])),
            "c_ln_w": _row(c_ln_w[l]), "c_ln_b": _row(c_ln_b[l]),
        }
        proj = _inproj(xa, xb, n_p + n_s, _row(norm1_g[l]), w_in[l].astype(BF16), tile(a_q_norm[l]),
                       _row(jnp.tile(a_k_norm[l], A_KV_W // HEAD_DIM)), tile(d_q_norm[l]), tile(d_k_norm[l]))
        _, ak, av, hb, _, _, dk, dv = proj
        caches = (cache_a_k[l], cache_a_v[l], state_b_shift[l], state_b_wkv[l], state_c[l], cache_d_k[l], cache_d_v[l])
        mix, (wkv_p, ret_p), (wkv_s, ret_s) = _mixers(proj, caches, lp, tabs, geom)
        x1, xn2 = _outproj(xa, xb, n_p + n_s, *mix, w_out[l].astype(BF16), _row(norm2_g[l]))
        j = l // 2
        if l % 2 == 0:
            x = _ffn(xn2, x1, ffn_w1[j].astype(BF16), ffn_w3[j].astype(BF16), ffn_w2[j].astype(BF16))
        else:
            router_pad = jnp.pad(moe_router[j].astype(F32), ((0, 0), (0, ROUTER_LANES - N_EXPERTS)))
            x = _moe(xn2, x1, router_pad, moe_w1[j].astype(BF16), moe_w3[j].astype(BF16), moe_w2[j].astype(BF16))
        xa = xb = x

        wa = min(A_PREV_CHUNKS * CHUNK, tp)
        wd = min(D_PREV_CHUNKS * CHUNK, tp)
        tail = lambda a, w, heads: jnp.stack(
            [a[(b + 1) * tp - w:(b + 1) * tp] for b in range(bp)]).reshape(bp, w, heads, HEAD_DIM)
        p_states.append((tail(ak, wa, 2), tail(av, wa, 2), hb[tp - 1:n_p:tp], wkv_p, ret_p,
                         tail(dk, wd, N_HEADS), tail(dv, wd, N_HEADS)))
        roll_in = lambda cache, new, heads: jnp.concatenate(
            [cache.astype(F32), new[n_p:].reshape(bs, ts, heads, HEAD_DIM)], axis=1)[:, -cache.shape[1]:]
        s_states.append((roll_in(cache_a_k[l], ak, 2), roll_in(cache_a_v[l], av, 2),
                         hb[n_p + ts - 1::ts], wkv_s, ret_s,
                         roll_in(cache_d_k[l], dk, N_HEADS), roll_in(cache_d_v[l], dv, N_HEADS)))

    yp = x[:n_p].reshape(bp, tp, D_MODEL)
    ys = x[n_p:].reshape(bs, ts, D_MODEL)
    st = lambda group, i: jnp.stack([g[i] for g in group], axis=0)
    return (yp, ys,
            st(p_states, 0), st(p_states, 1), st(p_states, 2), st(p_states, 3), st(p_states, 4), st(p_states, 5), st(p_states, 6),
            st(s_states, 0), st(s_states, 1), st(s_states, 2), st(s_states, 3), st(s_states, 4), st(s_states, 5), st(s_states, 6))
```

```python
import functools

import jax
import jax.numpy as jnp
from jax import lax
from jax.experimental import pallas as pl
from jax.experimental.pallas import tpu as pltpu
from jax.experimental.pallas import tpu_sc as plsc

F32 = jnp.float32
BF16 = jnp.bfloat16

D_MODEL = 1024
DEPTH = 2
PAST_LEN = 4096
CHUNK = 64
HEAD_DIM = 64
N_HEADS = 4
MIX_W = N_HEADS * HEAD_DIM
A_KV_W = 128
A_PREV_CHUNKS = 2
D_PREV_CHUNKS = 8
D_REL_CLIP = 128
B_PROJ = 1024
C_PROJ = 1024
IN_PROJ = 3328
B_GN_EPS = 64e-5
C_GN_EPS = 1e-6
NORM_EPS = 1e-6
ATTN_SCALE = 0.125
ROPE_BASE = 10000.0
D_FF = 2816
N_EXPERTS = 8
E_FF = 3584
NEG_BIG = -1e30

VMEM_LIMIT = 48 * 1024 * 1024

NN = ((1,), (0,))
NT = ((1,), (1,))
TN = ((0,), (0,))


def _dg(a, b, dims=NN):
    return lax.dot_general(a, b, (dims, ((), ())), preferred_element_type=F32)


def _parts(x, n):
    out = []
    r = x
    for i in range(n):
        p = r.astype(BF16)
        out.append(p)
        if i + 1 < n:
            r = r - p.astype(F32)
    return out


def _mm(a, b, dims=NN, passes=1):
    if passes == 1:
        return _dg(a.astype(BF16), b.astype(BF16), dims)
    ah, al = _parts(a, 2)
    bh, bl = _parts(b, 2)
    return _dg(ah, bh, dims) + (_dg(ah, bl, dims) + _dg(al, bh, dims))


def _mm_exact_rhs(a, b_bf, dims=NN, n=3):
    acc = None
    for p in _parts(a, n):
        t = _dg(p, b_bf, dims)
        acc = t if acc is None else acc + t
    return acc


def _iota2(shape, dim):
    return lax.broadcasted_iota(jnp.int32, shape, dim)


def _head_mask(rows, cols=MIX_W):
    return (_iota2((rows, cols), 0) >> 6) == (_iota2((rows, cols), 1) >> 6)


def _seg_matrix(width, value):
    m = _head_mask(width, width)
    return jnp.where(m, value, 0.0).astype(BF16)


def _tile4(z):
    return jnp.concatenate([z, z, z, z], axis=0)


def _fold4(z):
    return (z[0:64] + z[64:128]) + (z[128:192] + z[192:256])


def _sigmoid(x):
    return 1.0 / (1.0 + jnp.exp(-x))


def _cparams(sem):
    return pltpu.CompilerParams(dimension_semantics=sem, vmem_limit_bytes=VMEM_LIMIT)


IN_TM = 512


def _two_source_specs(xa, xb, n):
    na, nb = xa.shape[0] // IN_TM, xb.shape[0] // IN_TM
    spec_a = pl.BlockSpec((IN_TM, D_MODEL), lambda i: (jnp.minimum(i, na - 1), 0))
    spec_b = pl.BlockSpec((IN_TM, D_MODEL), lambda i: (jnp.clip(i - na, 0, nb - 1), 0))
    return na, n // IN_TM, spec_a, spec_b


def _inproj_kernel(xa_ref, xb_ref, g_ref, w_ref, aqg_ref, akg_ref, dqg_ref, dkg_ref,
                   aq_ref, ak_ref, av_ref, hb_ref, hc_ref, dq_ref, dk_ref, dv_ref, *, n_first):
    x = jnp.where(pl.program_id(0) < n_first, xa_ref[...], xb_ref[...])
    ms = jnp.mean(x * x, axis=-1, keepdims=True)
    xn = ((x * lax.rsqrt(ms + NORM_EPS)) * g_ref[...]).astype(BF16)
    seg = _seg_matrix(MIX_W, 1.0 / HEAD_DIM)

    def proj(lo, hi):
        return jnp.dot(xn, w_ref[:, lo:hi], preferred_element_type=F32)

    def head_rms(h, gain_ref):
        w = h.shape[-1]
        msq = _mm_exact_rhs(h * h, seg[:w, :w], n=2)
        return (h * lax.rsqrt(msq + NORM_EPS)) * gain_ref[...]

    aq_ref[...] = head_rms(proj(0, 256), aqg_ref)
    ak_ref[...] = head_rms(proj(256, 384), akg_ref)
    av_ref[...] = proj(384, 512)
    hb_ref[...] = proj(512, 1536)
    hc_ref[...] = proj(1536, 2560)
    dq_ref[...] = head_rms(proj(2560, 2816), dqg_ref)
    dk_ref[...] = head_rms(proj(2816, 3072), dkg_ref)
    dv_ref[...] = proj(3072, 3328)


def _inproj(xa, xb, n, g, w_bf, aqg, akg, dqg, dkg):
    na, nblk, spec_a, spec_b = _two_source_specs(xa, xb, n)
    widths = (256, 128, 128, B_PROJ, C_PROJ, 256, 256, 256)
    row = lambda w: pl.BlockSpec((IN_TM, w), lambda i: (i, 0))
    full = lambda a: pl.BlockSpec(a.shape, lambda i: (0,) * a.ndim)
    return pl.pallas_call(
        functools.partial(_inproj_kernel, n_first=na),
        grid=(nblk,),
        in_specs=[spec_a, spec_b, full(g), full(w_bf), full(aqg), full(akg), full(dqg), full(dkg)],
        out_specs=[row(w) for w in widths],
        out_shape=[jax.ShapeDtypeStruct((n, w), F32) for w in widths],
        compiler_params=_cparams(("parallel",)),
        name="inproj",
    )(xa, xb, g, w_bf, aqg, akg, dqg, dkg)


PACK_W = 256


def _pack_bf16_pairs(hi, lo):
    bits = lambda z: pltpu.bitcast(z.astype(BF16).astype(F32), jnp.int32)
    return bits(hi) | lax.shift_right_logical(bits(lo), jnp.full(lo.shape, 16, jnp.int32))


def _unpack_bf16_pairs(w):
    hi = pltpu.bitcast(w & jnp.int32(-65536), F32)
    lo = pltpu.bitcast(lax.shift_left(w, jnp.full(w.shape, 16, jnp.int32)), F32)
    return hi, lo


def _pack_rows(x):
    return (_pack_bf16_pairs(x[:, 0:PACK_W], x[:, PACK_W:2 * PACK_W]),
            _pack_bf16_pairs(x[:, 2 * PACK_W:3 * PACK_W], x[:, 3 * PACK_W:4 * PACK_W]))


def _unpack_rows(wa, wb):
    return jnp.concatenate(_unpack_bf16_pairs(wa) + _unpack_bf16_pairs(wb), axis=1)


def _outproj_kernel(xa_ref, xb_ref, *refs, n_first_x, n_first_mix, pack):
    mix_refs, (w_ref, g_ref), outs = refs[:8], refs[8:10], refs[10:]
    i = pl.program_id(0)
    acc = jnp.where(i < n_first_x, xa_ref[...], xb_ref[...])
    for m in range(4):
        o = jnp.where(i < n_first_mix, mix_refs[2 * m][...], mix_refs[2 * m + 1][...])
        acc = acc + jnp.dot(o.astype(BF16), w_ref[m * MIX_W:(m + 1) * MIX_W, :], preferred_element_type=F32)
    outs[0][...] = acc
    ms = jnp.mean(acc * acc, axis=-1, keepdims=True)
    xn = (acc * lax.rsqrt(ms + NORM_EPS)) * g_ref[...]
    outs[1][...] = xn.astype(BF16)
    if pack:
        outs[2][...], outs[3][...] = _pack_rows(xn)


def _outproj(xa, xb, n, mix, w_bf, g2, *, pack):
    na, nblk, spec_a, spec_b = _two_source_specs(xa, xb, n)
    nm_p, nm_s = mix[0][0].shape[0] // IN_TM, mix[0][1].shape[0] // IN_TM
    row = lambda w: pl.BlockSpec((IN_TM, w), lambda i: (i, 0))
    full = lambda a: pl.BlockSpec(a.shape, lambda i: (0,) * a.ndim)
    mix_specs = [pl.BlockSpec((IN_TM, MIX_W), lambda i: (jnp.minimum(i, nm_p - 1), 0)),
                 pl.BlockSpec((IN_TM, MIX_W), lambda i: (jnp.clip(i - nm_p, 0, nm_s - 1), 0))] * 4
    out_specs = [row(D_MODEL), row(D_MODEL)]
    out_shape = [jax.ShapeDtypeStruct((n, D_MODEL), F32), jax.ShapeDtypeStruct((n, D_MODEL), BF16)]
    if pack:
        out_specs += [row(PACK_W), row(PACK_W)]
        out_shape += [jax.ShapeDtypeStruct((n, PACK_W), jnp.int32)] * 2
    return pl.pallas_call(
        functools.partial(_outproj_kernel, n_first_x=na, n_first_mix=nm_p, pack=pack),
        grid=(nblk,),
        in_specs=[spec_a, spec_b] + mix_specs + [full(w_bf), full(g2)],
        out_specs=out_specs,
        out_shape=out_shape,
        compiler_params=_cparams(("parallel",)),
        name="outproj",
    )(xa, xb, *[a for pair in mix for a in pair], w_bf, g2)


FFN_TM = 512
FFN_TF = 1408


def _ffn_kernel(xn_ref, x1_ref, w1_ref, w3_ref, w2_ref, o_ref):
    f = pl.program_id(1)
    xn = xn_ref[...]
    a = jnp.dot(xn, w1_ref[...], preferred_element_type=F32)
    b = jnp.dot(xn, w3_ref[...], preferred_element_type=F32)
    h = ((a * _sigmoid(a)) * b).astype(BF16)
    y = jnp.dot(h, w2_ref[...], preferred_element_type=F32)

    @pl.when(f == 0)
    def _():
        o_ref[...] = x1_ref[...] + y

    @pl.when(f != 0)
    def _():
        o_ref[...] += y


def _ffn(xn, x1, w1_bf, w3_bf, w2_bf):
    n = xn.shape[0]
    return pl.pallas_call(
        _ffn_kernel,
        grid=(n // FFN_TM, D_FF // FFN_TF),
        in_specs=[
            pl.BlockSpec((FFN_TM, D_MODEL), lambda i, f: (i, 0)),
            pl.BlockSpec((FFN_TM, D_MODEL), lambda i, f: (i, 0)),
            pl.BlockSpec((D_MODEL, FFN_TF), lambda i, f: (0, f)),
            pl.BlockSpec((D_MODEL, FFN_TF), lambda i, f: (0, f)),
            pl.BlockSpec((FFN_TF, D_MODEL), lambda i, f: (f, 0)),
        ],
        out_specs=pl.BlockSpec((FFN_TM, D_MODEL), lambda i, f: (i, 0)),
        out_shape=jax.ShapeDtypeStruct((n, D_MODEL), F32),
        compiler_params=_cparams(("parallel", "arbitrary")),
        name="ffn",
    )(xn, x1, w1_bf, w3_bf, w2_bf)


ROUTER_LANES = 128


def _router_kernel(xn_ref, r_ref, gate_ref, rank_ref, cnt_ref, cnt_scr):
    @pl.when(pl.program_id(0) == 0)
    def _():
        cnt_scr[...] = jnp.zeros_like(cnt_scr)

    xn = xn_ref[...].astype(F32)
    logits = _mm(xn, r_ref[...], passes=3)
    lane = _iota2(logits.shape, 1)
    logits = jnp.where(lane < N_EXPERTS, logits, NEG_BIG)
    m1 = jnp.max(logits, axis=-1, keepdims=True)
    i1 = jnp.min(jnp.where(logits == m1, lane, ROUTER_LANES), axis=-1, keepdims=True)
    rest = jnp.where(lane == i1, NEG_BIG, logits)
    m2 = jnp.max(rest, axis=-1, keepdims=True)
    i2 = jnp.min(jnp.where(rest == m2, lane, ROUTER_LANES), axis=-1, keepdims=True)
    e2 = jnp.exp(m2 - m1)
    den = 1.0 + e2
    gates = jnp.where(lane == i1, 1.0 / den, 0.0) + jnp.where(lane == i2, e2 / den, 0.0)
    gate_ref[...] = gates
    sel = jnp.where(gates > 0.0, 1.0, 0.0)
    tm = sel.shape[0]
    before = jnp.where(_iota2((tm, tm), 1) < _iota2((tm, tm), 0), 1.0, 0.0).astype(BF16)
    rank_ref[...] = (_dg(before, sel.astype(BF16)) + cnt_scr[...]).astype(jnp.int32)
    cnt_scr[...] = cnt_scr[...] + jnp.sum(sel, axis=0, keepdims=True)
    cnt_ref[...] = cnt_scr[...].astype(jnp.int32)


def _router(xn, router_pad):
    n = xn.shape[0]
    row = pl.BlockSpec((IN_TM, ROUTER_LANES), lambda i: (i, 0))
    return pl.pallas_call(
        _router_kernel,
        grid=(n // IN_TM,),
        in_specs=[pl.BlockSpec((IN_TM, D_MODEL), lambda i: (i, 0)),
                  pl.BlockSpec(router_pad.shape, lambda i: (0, 0))],
        out_specs=[row, row, pl.BlockSpec((1, ROUTER_LANES), lambda i: (0, 0))],
        out_shape=[jax.ShapeDtypeStruct((n, ROUTER_LANES), F32),
                   jax.ShapeDtypeStruct((n, ROUTER_LANES), jnp.int32),
                   jax.ShapeDtypeStruct((1, ROUTER_LANES), jnp.int32)],
        scratch_shapes=[pltpu.VMEM((1, ROUTER_LANES), F32)],
        compiler_params=_cparams(("arbitrary",)),
        name="router",
    )(xn, router_pad)


MOE_R = 512
MOE_TF = 1792
SC_WINDOW = 128


def _moe_plan(gates, rank, counts, n):
    n_blocks = (2 * n) // MOE_R + N_EXPERTS + 1
    spare_row = (n_blocks - 1) * MOE_R
    sel = gates[:, :N_EXPERTS] > 0.0
    rank = rank[:, :N_EXPERTS]
    counts = counts[0, :N_EXPERTS]
    padded = ((counts + MOE_R - 1) // MOE_R) * MOE_R
    pad_end = jnp.cumsum(padded)
    pad_start = pad_end - padded
    pos = jnp.where(sel, pad_start[None, :] + rank, -1)
    order = jnp.cumsum(sel.astype(jnp.int32), axis=1)
    pick = lambda j: jnp.max(jnp.where(jnp.logical_and(sel, order == j), pos, -1), axis=1)
    to_row = lambda p: jnp.where(p >= 0, p, spare_row).astype(jnp.int32).reshape(1, n)
    block_expert = jnp.minimum(
        jnp.sum(pad_end[None, :] <= (jnp.arange(n_blocks) * MOE_R)[:, None], axis=1), N_EXPERTS - 1)
    return dict(n_blocks=n_blocks, pos0=to_row(pick(1)), pos1=to_row(pick(2)),
                block_expert=block_expert.astype(jnp.int32), n_used=(pad_end[-1:] // MOE_R).astype(jnp.int32))


def _sc_mesh():
    return plsc.VectorSubcoreMesh(core_axis_name="core", subcore_axis_name="subcore")


def _sc_scatter_rows(table, idx_lists, n_rows):
    n, cols = table.shape
    k = len(idx_lists)

    @functools.partial(pl.kernel, out_type=jax.ShapeDtypeStruct((n_rows, cols), table.dtype), mesh=_sc_mesh())
    def scatter(x_hbm, *rest):
        i_hbms, o_hbm = rest[:k], rest[k]

        def body(x_vmem, *i_vmems):
            for i_vmem in i_vmems:
                pltpu.sync_copy(x_vmem, o_hbm.at[i_vmem.at[0]])

        pltpu.emit_pipeline(
            body,
            grid=(n // SC_WINDOW,),
            in_specs=[pl.BlockSpec((SC_WINDOW, cols), lambda i: (i, 0))]
            + [pl.BlockSpec((1, SC_WINDOW), lambda i: (0, i))] * k,
            out_specs=[],
            core_axis_name=("core", "subcore"),
            dimension_semantics=(pltpu.PARALLEL,),
        )(x_hbm, *i_hbms)

    return scatter(table, *idx_lists)


def _sc_gather_rows(table, idx):
    n = idx.shape[1]
    cols = table.shape[1]

    @functools.partial(pl.kernel, out_type=jax.ShapeDtypeStruct((n, cols), table.dtype), mesh=_sc_mesh())
    def gather(x_hbm, i_hbm, o_hbm):
        def body(i_vmem, o_vmem):
            pltpu.sync_copy(x_hbm.at[i_vmem.at[0]], o_vmem)

        pltpu.emit_pipeline(
            body,
            grid=(n // SC_WINDOW,),
            in_specs=[pl.BlockSpec((1, SC_WINDOW), lambda i: (0, i))],
            out_specs=[pl.BlockSpec((SC_WINDOW, cols), lambda i: (i, 0))],
            core_axis_name=("core", "subcore"),
            dimension_semantics=(pltpu.PARALLEL,),
        )(i_hbm, o_hbm)

    return gather(table, idx)


def _moe_expert_kernel(be_ref, nu_ref, xa_ref, xb_ref, gs_ref, w1_ref, w3_ref, w2_ref, oa_ref, ob_ref, acc_ref):
    j, f = pl.program_id(0), pl.program_id(1)
    used = j < nu_ref[0]

    @pl.when(used)
    def _():
        x = _unpack_rows(xa_ref[...], xb_ref[...]).astype(BF16)
        a = jnp.dot(x, w1_ref[...], preferred_element_type=F32)
        b = jnp.dot(x, w3_ref[...], preferred_element_type=F32)
        h = ((a * _sigmoid(a)) * b).astype(BF16)
        y = jnp.dot(h, w2_ref[...], preferred_element_type=F32)

        @pl.when(f == 0)
        def _():
            acc_ref[...] = y

        @pl.when(f != 0)
        def _():
            acc_ref[...] += y

    @pl.when(f == pl.num_programs(1) - 1)
    def _():
        lane = _iota2(gs_ref.shape, 1)
        g = jnp.sum(jnp.where(lane == be_ref[j], gs_ref[...], 0.0), axis=1, keepdims=True)
        oa_ref[...], ob_ref[...] = _pack_rows(jnp.where(used, acc_ref[...] * g, 0.0))


def _moe_experts(plan, xs_a, xs_b, gs, w1_bf, w3_bf, w2_bf):
    n_blocks = plan["n_blocks"]
    half = pl.BlockSpec((MOE_R, PACK_W), lambda j, f, be, nu: (j, 0))
    grid_spec = pltpu.PrefetchScalarGridSpec(
        num_scalar_prefetch=2,
        grid=(n_blocks, E_FF // MOE_TF),
        in_specs=[half, half,
                  pl.BlockSpec((MOE_R, ROUTER_LANES), lambda j, f, be, nu: (j, 0)),
                  pl.BlockSpec((None, D_MODEL, MOE_TF), lambda j, f, be, nu: (be[j], 0, f)),
                  pl.BlockSpec((None, D_MODEL, MOE_TF), lambda j, f, be, nu: (be[j], 0, f)),
                  pl.BlockSpec((None, MOE_TF, D_MODEL), lambda j, f, be, nu: (be[j], f, 0))],
        out_specs=[half, half],
        scratch_shapes=[pltpu.VMEM((MOE_R, D_MODEL), F32)])
    return pl.pallas_call(
        _moe_expert_kernel,
        grid_spec=grid_spec,
        out_shape=[jax.ShapeDtypeStruct((n_blocks * MOE_R, PACK_W), jnp.int32)] * 2,
        compiler_params=_cparams(("arbitrary", "arbitrary")),
        name="moe_experts",
    )(plan["block_expert"], plan["n_used"], xs_a, xs_b, gs, w1_bf, w3_bf, w2_bf)


def _moe_combine_kernel(x1_ref, a0_ref, b0_ref, a1_ref, b1_ref, o_ref):
    o_ref[...] = (x1_ref[...] + _unpack_rows(a0_ref[...], b0_ref[...])) + _unpack_rows(a1_ref[...], b1_ref[...])


def _moe_combine(x1, picked):
    n = x1.shape[0]
    row = lambda w: pl.BlockSpec((IN_TM, w), lambda i: (i, 0))
    return pl.pallas_call(
        _moe_combine_kernel,
        grid=(n // IN_TM,),
        in_specs=[row(D_MODEL)] + [row(PACK_W)] * 4,
        out_specs=row(D_MODEL),
        out_shape=jax.ShapeDtypeStruct((n, D_MODEL), F32),
        compiler_params=_cparams(("parallel",)),
        name="moe_combine",
    )(x1, *picked)


def _moe(xn, xn_a, xn_b, x1, router_pad, w1_bf, w3_bf, w2_bf):
    gates, rank, counts = _router(xn, router_pad)
    plan = _moe_plan(gates, rank, counts, xn.shape[0])
    n_rows = plan["n_blocks"] * MOE_R
    idx = (plan["pos0"], plan["pos1"])
    xs_a = _sc_scatter_rows(xn_a, idx, n_rows)
    xs_b = _sc_scatter_rows(xn_b, idx, n_rows)
    gs = _sc_scatter_rows(gates, idx, n_rows)
    os_a, os_b = _moe_experts(plan, xs_a, xs_b, gs, w1_bf, w3_bf, w2_bf)
    picked = [_sc_gather_rows(t, p) for p in idx for t in (os_a, os_b)]
    return _moe_combine(x1, picked)


def _relbias_kernel(rb_ref, o_ref, *, nk):
    h = pl.program_id(0)
    q = _iota2((CHUNK, nk), 0)
    r = _iota2((CHUNK, nk), 1)
    idx = jnp.clip(q - (r - (nk - CHUNK)), -D_REL_CLIP, D_REL_CLIP) + D_REL_CLIP

    def body(j, acc):
        return jnp.where(idx == j, rb_ref[h, j], acc)

    o_ref[...] = lax.fori_loop(0, 2 * D_REL_CLIP + 1, body, jnp.zeros((CHUNK, nk), F32))


def _relbias_table(rel_bias, nk):
    return pl.pallas_call(
        functools.partial(_relbias_kernel, nk=nk),
        grid=(N_HEADS,),
        in_specs=[pl.BlockSpec(memory_space=pltpu.SMEM)],
        out_specs=pl.BlockSpec((CHUNK, nk), lambda h: (h, 0)),
        out_shape=jax.ShapeDtypeStruct((N_HEADS * CHUNK, nk), F32),
        name="relbias",
    )(rel_bias)


ATTN_QB = 8


def _attn_kernel(q_ref, kp_ref, kc_ref, vp_ref, vc_ref, x_ref, *rest, qb, n_prev, use_sink, mask_first):
    o_ref, kbuf, vbuf = rest[-3:]
    i = pl.program_id(1)
    p_rows = kp_ref.shape[0]
    nk = (n_prev + 1) * CHUNK
    wk = kp_ref.shape[1]

    if wk == MIX_W:
        stage = lambda ref: ref[...].astype(BF16)
    else:
        src = _iota2((wk, MIX_W), 0)
        dst = _iota2((wk, MIX_W), 1)
        expand = jnp.where(src == (dst >> 7) * HEAD_DIM + (dst & (HEAD_DIM - 1)), 1.0, 0.0).astype(BF16)
        stage = lambda ref: jnp.dot(ref[...].astype(BF16), expand, preferred_element_type=F32).astype(BF16)

    kbuf[0:p_rows, :] = stage(kp_ref)
    kbuf[p_rows:, :] = stage(kc_ref)
    vbuf[0:p_rows, :] = stage(vp_ref)
    vbuf[p_rows:, :] = stage(vc_ref)

    hmask = _head_mask(N_HEADS * CHUNK)
    extra = x_ref[...]

    def body(j, carry):
        r0 = pl.multiple_of(j * CHUNK, CHUNK)
        base = pl.multiple_of(p_rows + (j - n_prev) * CHUNK, CHUNK)
        qj = q_ref[pl.ds(r0, CHUNK), :] * ATTN_SCALE
        qs = jnp.where(hmask, _tile4(qj), 0.0).astype(BF16)
        s = _dg(qs, kbuf[pl.ds(base, nk), :], NT)
        if not use_sink:
            s = s + extra
        if mask_first:
            krow = base + _iota2(s.shape, 1)
            s = jnp.where(jnp.logical_and(i == 0, krow < p_rows), NEG_BIG, s)
        m = jnp.max(s, axis=-1, keepdims=True)
        if use_sink:
            m = jnp.maximum(m, extra)
        e = jnp.exp(s - m)
        den = jnp.sum(e, axis=-1, keepdims=True)
        if use_sink:
            den = den + jnp.exp(extra - m)
        o_all = _dg(e.astype(BF16), vbuf[pl.ds(base, nk), :]) * (1.0 / den)
        o_ref[pl.ds(r0, CHUNK), :] = _fold4(jnp.where(hmask, o_all, 0.0))
        return carry

    lax.fori_loop(0, qb, body, 0)


def _attention(q, k, v, prev, extra, *, n_prev, use_sink, n_streams, t, base_row):
    wk = k.shape[-1]
    if prev is None:
        qb = ATTN_QB
        rows = qb * CHUNK
        nblk = t // rows
        base = base_row // rows
        prev_spec = pl.BlockSpec((rows, wk), lambda s, i: (base + s * nblk + jnp.maximum(i - 1, 0), 0))
        k_prev, v_prev, p_rows, mask_first = k, v, rows, True
    else:
        qb, rows, nblk = t // CHUNK, t, 1
        base = base_row // rows
        k_prev, v_prev = prev
        p_rows = k_prev.shape[1]
        prev_spec = pl.BlockSpec((None, p_rows, wk), lambda s, i: (s, 0, 0))
        mask_first = False
    cur = lambda w: pl.BlockSpec((rows, w), lambda s, i: (base + s * nblk + i, 0))
    kern = functools.partial(_attn_kernel, qb=qb, n_prev=n_prev, use_sink=use_sink, mask_first=mask_first)
    return pl.pallas_call(
        kern,
        grid=(n_streams, nblk),
        in_specs=[cur(MIX_W), prev_spec, cur(wk), prev_spec, cur(wk),
                  pl.BlockSpec(extra.shape, lambda s, i: (0, 0))],
        out_specs=pl.BlockSpec((rows, MIX_W), lambda s, i: (s * nblk + i, 0)),
        out_shape=jax.ShapeDtypeStruct((n_streams * t, MIX_W), F32),
        scratch_shapes=[pltpu.VMEM((p_rows + rows, MIX_W), BF16), pltpu.VMEM((p_rows + rows, MIX_W), BF16)],
        compiler_params=_cparams(("parallel", "arbitrary")),
        name="attn_sink" if use_sink else "attn_bias",
    )(q, k_prev, k, v_prev, v, extra)


def _head_layer_norm(o, seg_mean_bf, w, b, eps):
    mu = _mm_exact_rhs(o, seg_mean_bf)
    d = o - mu
    var = _mm_exact_rhs(d * d, seg_mean_bf)
    return (d * lax.rsqrt(var + eps)) * w + b


def _ret_kernel(hc_ref, cos_ref, sin_ref, s0_ref, dstack_ref, qsc_ref, ksc_ref, gam_ref, lnw_ref, lnb_ref,
                *rest, qb):
    o_ref, sout_ref, s_scr = rest[-3:]
    i = pl.program_id(1)

    @pl.when(i == 0)
    def _():
        s_scr[...] = s0_ref[...]

    hmask = _head_mask(N_HEADS * CHUNK)
    seg_mean = _seg_matrix(MIX_W, 1.0 / HEAD_DIM)
    first_half = (_iota2((CHUNK, MIX_W), 1) & (HEAD_DIM - 1)) < (HEAD_DIM // 2)

    def rope(x, cos, sin):
        partner = jnp.where(first_half, pltpu.roll(x, MIX_W - HEAD_DIM // 2, 1), pltpu.roll(x, HEAD_DIM // 2, 1))
        return x * cos + partner * sin

    def body(j, carry):
        r0 = pl.multiple_of(j * CHUNK, CHUNK)
        cos = cos_ref[pl.ds(r0, CHUNK), :]
        sin = sin_ref[pl.ds(r0, CHUNK), :]
        q = rope(hc_ref[pl.ds(r0, CHUNK), 0:256], cos, sin)
        k = rope(hc_ref[pl.ds(r0, CHUNK), 256:512], cos, sin) * ATTN_SCALE
        v = hc_ref[pl.ds(r0, CHUNK), 512:768]
        g = hc_ref[pl.ds(r0, CHUNK), 768:1024]
        k_bf = k.astype(BF16)
        v_bf = v.astype(BF16)
        state = s_scr[...]
        qs = jnp.where(hmask, _tile4(q), 0.0).astype(BF16)
        sc = _dg(qs, k_bf, NT) * dstack_ref[...]
        intra = _fold4(jnp.where(hmask, _dg(sc.astype(BF16), v_bf), 0.0))
        inter = _dg((q * qsc_ref[...]).astype(BF16), state.astype(BF16))
        kv = _dg((k * ksc_ref[...]).astype(BF16), v_bf, TN)
        s_scr[...] = gam_ref[...] * state + jnp.where(hmask, kv, 0.0)
        y = _head_layer_norm(intra + inter, seg_mean, lnw_ref[...], lnb_ref[...], C_GN_EPS)
        o_ref[pl.ds(r0, CHUNK), :] = y * (g * _sigmoid(g))
        return carry

    lax.fori_loop(0, qb, body, 0)
    sout_ref[...] = s_scr[...]


def _retention(hc, cos, sin, s0_bd, tabs, lnw, lnb, *, n_streams, t, base_row):
    qb = min(8, t // CHUNK)
    rows = qb * CHUNK
    nblk = t // rows
    base = base_row // rows
    dstack, qsc, ksc, gam = tabs
    full = lambda a: pl.BlockSpec(a.shape, lambda s, i: (0,) * a.ndim)
    cur = lambda w: pl.BlockSpec((rows, w), lambda s, i: (base + s * nblk + i, 0))
    state = pl.BlockSpec((None, MIX_W, MIX_W), lambda s, i: (s, 0, 0))
    return pl.pallas_call(
        functools.partial(_ret_kernel, qb=qb),
        grid=(n_streams, nblk),
        in_specs=[cur(C_PROJ),
                  pl.BlockSpec((rows, MIX_W), lambda s, i: (i, 0)),
                  pl.BlockSpec((rows, MIX_W), lambda s, i: (i, 0)),
                  state, full(dstack), full(qsc), full(ksc), full(gam), full(lnw), full(lnb)],
        out_specs=[pl.BlockSpec((rows, MIX_W), lambda s, i: (s * nblk + i, 0)), state],
        out_shape=[jax.ShapeDtypeStruct((n_streams * t, MIX_W), F32),
                   jax.ShapeDtypeStruct((n_streams, MIX_W, MIX_W), F32)],
        scratch_shapes=[pltpu.VMEM((MIX_W, MIX_W), F32)],
        compiler_params=_cparams(("parallel", "arbitrary")),
        name="retention",
    )(hc, cos, sin, s0_bd, dstack, qsc, ksc, gam, lnw, lnb)


def _retention_tables():
    hh = jnp.arange(N_HEADS, dtype=F32)
    log_gamma = jnp.log(1.0 - 2.0 ** (-5.0 - hh))
    t = jnp.arange(CHUNK)
    diff = t[:, None] - t[None, :]
    dmat = jnp.where(diff >= 0, jnp.exp(log_gamma[:, None, None] * jnp.maximum(diff, 0)), 0.0)
    dstack = dmat.reshape(N_HEADS * CHUNK, CHUNK)
    lanes = lambda per_head: jnp.repeat(per_head, HEAD_DIM, axis=-1)
    qsc = lanes(jnp.exp((t + 1)[:, None] * log_gamma[None, :]))
    ksc = lanes(jnp.exp((CHUNK - 1 - t)[:, None] * log_gamma[None, :]))
    gam = jnp.broadcast_to(lanes(jnp.exp(log_gamma * CHUNK))[:, None], (MIX_W, MIX_W))
    return dstack.astype(F32), qsc.astype(F32), ksc.astype(F32), gam.astype(F32)


def _rope_tables(pos):
    half = HEAD_DIM // 2
    theta = 1.0 / (ROPE_BASE ** jnp.linspace(0.0, 1.0, half, dtype=F32))
    ang = pos.astype(F32)[:, None] * theta[None, :]
    cos, sin = jnp.cos(ang), jnp.sin(ang)
    cos_t = jnp.tile(jnp.concatenate([cos, cos], axis=-1), (1, N_HEADS))
    sin_t = jnp.tile(jnp.concatenate([-sin, sin], axis=-1), (1, N_HEADS))
    return cos_t, sin_t


DECAY_SCALE = 0.6065306597126334
RWKV_CB = 4
N_LEVELS = 6
MASK_HEAD, MASK_STRICT, MASK_INCL, MASK_LEVEL0 = 0, 1, 2, 3


def _rwkv_masks():
    n4 = N_HEADS * CHUNK
    ri = jnp.arange(n4)[:, None]
    ci = jnp.arange(n4)[None, :]
    head = (ri >> 6) == (ci >> 6)
    tabs = [head, head & ((ci & 63) < (ri & 63)), head & ((ci & 63) <= (ri & 63))]
    for log_m in range(N_LEVELS):
        same = (ri >> (log_m + 1)) == (ci >> (log_m + 1))
        tabs.append(same & (((ri >> log_m) & 1) == 1) & (((ci >> log_m) & 1) == 0))
    return jnp.stack(tabs).astype(BF16)


def _rwkv_kernel(hb_ref, shift0_ref, h0_ref, masks_ref, mu_ref, w0_ref, w2_ref, a0_ref, a2_ref, g2_ref,
                 kk_ref, ka_ref, rk_ref, lnw_ref, lnb_ref, *rest, cb, independent):
    o_ref, hout_ref, h_scr, shift_scr = rest[-4:]
    c = pl.program_id(1)
    rows = cb * CHUNK
    xb = hb_ref[...]
    row = _iota2(xb.shape, 0)
    prev = pltpu.roll(xb, 1, 0)
    if independent:
        for j in range(cb):
            prev = jnp.where(row == j * CHUNK, shift0_ref[j], prev)
    else:
        @pl.when(c == 0)
        def _():
            h_scr[...] = h0_ref[0]
            shift_scr[...] = shift0_ref[0]

        prev = jnp.where(row == 0, shift_scr[...], prev)
        shift_scr[...] = xb[rows - 1:rows, :]
    xs = xb + mu_ref[...] * (prev - xb)
    r = xs[:, 0:256]
    k = xs[:, 256:512]
    v = xs[:, 512:768]
    xw = xs[:, 768:832]
    xa = xs[:, 832:896]
    xg = xs[:, 896:1024]

    z = w0_ref[...] + _mm(jnp.tanh(xw), w2_ref[...], passes=3)
    lw = -DECAY_SCALE * _sigmoid(z)
    a_gate = _sigmoid(a0_ref[...] + _mm(xa, a2_ref[...], passes=3))
    gate = _mm(_sigmoid(xg), g2_ref[...], passes=1)

    seg_sum = _seg_matrix(MIX_W, 1.0)
    seg_mean = _seg_matrix(MIX_W, 1.0 / HEAD_DIM)
    kkn = k * kk_ref[...]
    norm = jnp.sqrt(_mm_exact_rhs(kkn * kkn, seg_sum))
    kk = kkn / jnp.maximum(norm, 1e-12)
    kf = k * (1.0 + (a_gate - 1.0) * ka_ref[...])

    tt = _iota2((rows, rows), 0)
    ss = _iota2((rows, rows), 1)
    tril = jnp.where(jnp.logical_and(ss <= tt, (ss >> 6) == (tt >> 6)), 1.0, 0.0).astype(BF16)
    lw_parts = _parts(lw, 3)
    cum = _dg(tril, lw_parts[0]) + (_dg(tril, lw_parts[1]) + _dg(tril, lw_parts[2]))
    w_inv = jnp.exp(-cum)
    rho = (r * jnp.exp(cum)).astype(BF16)
    alpha = (-kk * jnp.exp(cum - lw)).astype(BF16)
    beta = ((kk * a_gate) * w_inv).astype(BF16)
    kappa = (kf * w_inv).astype(BF16)
    v_bf = v.astype(BF16)
    ones = jnp.ones((CHUNK, MIX_W), BF16)

    hmask = masks_ref[MASK_HEAD]
    n4 = N_HEADS * CHUNK
    eye = jnp.where(_iota2((n4, n4), 0) == _iota2((n4, n4), 1), 1.0, 0.0)

    pre = []
    for j in range(cb):
        sl = slice(j * CHUNK, (j + 1) * CHUNK)
        bd = lambda zz: _tile4(zz[sl]) * hmask
        al_bd, be_bd, ka_bd, rh_bd, v_bd = bd(alpha), bd(beta), bd(kappa), bd(rho), bd(v_bf)
        a_bf = _dg(al_bd, be_bd, NT).astype(BF16) * masks_ref[MASK_STRICT]
        a_ak = _dg(al_bd, ka_bd, NT).astype(BF16) * masks_ref[MASK_STRICT]
        b_rb = _dg(rh_bd, be_bd, NT).astype(BF16) * masks_ref[MASK_INCL]
        b_rk = _dg(rh_bd, ka_bd, NT).astype(BF16) * masks_ref[MASK_INCL]
        t_inv = eye + (a_bf * masks_ref[MASK_LEVEL0]).astype(F32)
        for lvl in range(1, N_LEVELS):
            t_bf = t_inv.astype(BF16)
            e_mat = _dg(a_bf * masks_ref[MASK_LEVEL0 + lvl], t_bf)
            t_inv = t_inv + _dg(t_bf, e_mat.astype(BF16))
        x0 = _dg(a_ak, v_bd)
        y0 = _dg(b_rk, v_bd)
        hn0 = _dg(ka_bd, v_bd, TN)
        ctot = (_dg(lw_parts[0][sl], ones, TN)
                + (_dg(lw_parts[1][sl], ones, TN) + _dg(lw_parts[2][sl], ones, TN)))
        pre.append((al_bd, be_bd, rh_bd, b_rb, t_inv.astype(BF16), x0, y0, hn0, jnp.exp(ctot)))

    ys = []
    h = None if independent else h_scr[...]
    for j in range(cb):
        al_bd, be_bd, rh_bd, b_rb, t_bf, x0, y0, hn0, wc = pre[j]
        h0 = h0_ref[j] if independent else h
        h0_bf = h0.astype(BF16)
        x_mat = _dg(al_bd, h0_bf) + x0
        u_bf = _dg(t_bf, x_mat.astype(BF16)).astype(BF16)
        y_bd = _dg(rh_bd, h0_bf) + _dg(b_rb, u_bf) + y0
        h_new = wc * (h0 + _dg(be_bd, u_bf, TN) + hn0)
        ys.append(_fold4(y_bd))
        if independent:
            hout_ref[j] = h_new
        else:
            h = h_new
    if not independent:
        h_scr[...] = h
        hout_ref[0] = h

    y = _head_layer_norm(jnp.concatenate(ys, axis=0), seg_mean, lnw_ref[...], lnb_ref[...], B_GN_EPS)
    bonus = _mm_exact_rhs(r * kf * rk_ref[...], seg_sum) * v
    o_ref[...] = (y + bonus) * gate


def _rwkv(hb, shift0, h0_bd, params, *, independent, n_streams, t, base_row):
    cb = RWKV_CB
    rows = cb * CHUNK
    nblk = t // rows
    base = base_row // rows
    masks = _rwkv_masks()
    full = lambda a: pl.BlockSpec(a.shape, lambda s, c: (0,) * a.ndim)
    if independent:
        assert n_streams == 1
        st_map = lambda s, c: (c, 0, 0)
        n_state, st_blk = t // CHUNK, cb
    else:
        st_map = lambda s, c: (s, 0, 0)
        n_state, st_blk = n_streams, 1
    cur = lambda w: pl.BlockSpec((rows, w), lambda s, c: (base + s * nblk + c, 0))
    return pl.pallas_call(
        functools.partial(_rwkv_kernel, cb=cb, independent=independent),
        grid=(n_streams, nblk),
        in_specs=[cur(B_PROJ),
                  pl.BlockSpec((st_blk, 1, B_PROJ), st_map),
                  pl.BlockSpec((st_blk, MIX_W, MIX_W), st_map),
                  full(masks)] + [full(p) for p in params],
        out_specs=[pl.BlockSpec((rows, MIX_W), lambda s, c: (s * nblk + c, 0)),
                   pl.BlockSpec((st_blk, MIX_W, MIX_W), st_map)],
        out_shape=[jax.ShapeDtypeStruct((n_streams * t, MIX_W), F32),
                   jax.ShapeDtypeStruct((n_state, MIX_W, MIX_W), F32)],
        scratch_shapes=[pltpu.VMEM((MIX_W, MIX_W), F32), pltpu.VMEM((1, B_PROJ), F32)],
        compiler_params=_cparams(("parallel", "arbitrary")),
        name="rwkv7",
    )(hb, shift0, h0_bd, masks, *params)


def _to_block_diag(s):
    eye = jnp.eye(N_HEADS, dtype=s.dtype)
    out = s[:, :, :, None, :] * eye[None, :, None, :, None]
    return out.reshape(s.shape[0], MIX_W, MIX_W)


def _from_block_diag(m):
    b = m.reshape(m.shape[0], N_HEADS, HEAD_DIM, N_HEADS, HEAD_DIM)
    return jnp.stack([b[:, h, :, h, :] for h in range(N_HEADS)], axis=1)


def _row(p):
    return p.reshape(1, -1).astype(F32)


def _mixers(proj, caches, lp, tabs, geom):
    aq, ak, av, hb, hc, dq, dk, dv = proj
    bp, tp, bs, ts = geom
    n_p = bp * tp
    ca_k, ca_v, sb_shift, sb_wkv, sc, cd_k, cd_v = caches
    pr = dict(n_streams=bp, t=tp, base_row=0)
    sm = dict(n_streams=bs, t=ts, base_row=n_p)
    zeros_state = jnp.zeros((bp, MIX_W, MIX_W), F32)

    oa_p = _attention(aq, ak, av, None, lp["sink_col"], n_prev=A_PREV_CHUNKS, use_sink=True, **pr)
    oa_s = _attention(aq, ak, av, (ca_k.reshape(bs, -1, A_KV_W), ca_v.reshape(bs, -1, A_KV_W)), lp["sink_col"],
                      n_prev=A_PREV_CHUNKS, use_sink=True, **sm)

    ob_p, h_p = _rwkv(hb, jnp.zeros((bp, 1, B_PROJ), F32), zeros_state, lp["rwkv"], independent=False, **pr)
    h0_s = _to_block_diag(jnp.swapaxes(sb_wkv, -1, -2))
    ob_s, h_s = _rwkv(hb, sb_shift.reshape(bs, 1, B_PROJ), h0_s, lp["rwkv"], independent=True,
                      n_streams=1, t=bs * ts, base_row=n_p)

    oc_p, s_p = _retention(hc, *tabs["rope_prompt"], zeros_state, tabs["ret"], lp["c_ln_w"], lp["c_ln_b"], **pr)
    oc_s, s_s = _retention(hc, *tabs["rope_sample"], _to_block_diag(sc), tabs["ret"], lp["c_ln_w"], lp["c_ln_b"], **sm)

    od_p = _attention(dq, dk, dv, None, lp["bias_table"], n_prev=D_PREV_CHUNKS, use_sink=False, **pr)
    od_s = _attention(dq, dk, dv, (cd_k.reshape(bs, -1, MIX_W), cd_v.reshape(bs, -1, MIX_W)), lp["bias_table"],
                      n_prev=D_PREV_CHUNKS, use_sink=False, **sm)

    wkv = lambda h: jnp.swapaxes(_from_block_diag(h), -1, -2)
    mix = ((oa_p, oa_s), (ob_p, ob_s), (oc_p, oc_s), (od_p, od_s))
    return mix, (wkv(h_p), _from_block_diag(s_p)), (wkv(h_s), _from_block_diag(s_s))


def kernel(x_prompt, x_sample, cache_a_k, cache_a_v, state_b_shift, state_b_wkv, state_c, cache_d_k, cache_d_v,
           norm1_g, norm2_g, w_in, w_out, a_q_norm, a_k_norm, a_sinks, b_mu, b_w0, b_w2, b_a0, b_a2, b_g2,
           b_k_k, b_k_a, b_r_k, b_ln_w, b_ln_b, c_ln_w, c_ln_b, d_q_norm, d_k_norm, d_rel_bias,
           ffn_w1, ffn_w3, ffn_w2, moe_router, moe_w1, moe_w3, moe_w2):
    bp, tp, _ = x_prompt.shape
    bs, ts, _ = x_sample.shape
    assert ts == CHUNK
    n_p, n_s = bp * tp, bs * ts
    geom = (bp, tp, bs, ts)
    xa = x_prompt.reshape(n_p, D_MODEL)
    xb = x_sample.reshape(n_s, D_MODEL)

    tabs = {
        "ret": _retention_tables(),
        "rope_prompt": _rope_tables(jnp.arange(tp)),
        "rope_sample": _rope_tables(PAST_LEN + jnp.arange(ts)),
    }
    tile = lambda g: _row(jnp.tile(g, MIX_W // HEAD_DIM))

    p_states, s_states = [], []
    for l in range(DEPTH):
        lp = {
            "sink_col": jnp.repeat(a_sinks[l].astype(F32), CHUNK).reshape(N_HEADS * CHUNK, 1),
            "bias_table": _relbias_table(d_rel_bias[l].astype(F32), (D_PREV_CHUNKS + 1) * CHUNK),
            "rwkv": (_row(b_mu[l]), _row(b_w0[l]), b_w2[l], _row(b_a0[l]), b_a2[l], b_g2[l], _row(b_k_k[l]),
                     _row(b_k_a[l]), _row(b_r_k[l]), _row(b_ln_w[l]), _row(b_ln_b[l])),
            "c_ln_w": _row(c_ln_w[l]), "c_ln_b": _row(c_ln_b[l]),
        }
        proj = _inproj(xa, xb, n_p + n_s, _row(norm1_g[l]), w_in[l].astype(BF16), tile(a_q_norm[l]),
                       _row(jnp.tile(a_k_norm[l], A_KV_W // HEAD_DIM)), tile(d_q_norm[l]), tile(d_k_norm[l]))
        _, ak, av, hb, _, _, dk, dv = proj
        caches = (cache_a_k[l], cache_a_v[l], state_b_shift[l], state_b_wkv[l], state_c[l], cache_d_k[l], cache_d_v[l])
        mix, (wkv_p, ret_p), (wkv_s, ret_s) = _mixers(proj, caches, lp, tabs, geom)
        j = l // 2
        dense = l % 2 == 0
        x1, xn2, *packed = _outproj(xa, xb, n_p + n_s, mix, w_out[l].astype(BF16), _row(norm2_g[l]), pack=not dense)
        if dense:
            x = _ffn(xn2, x1, ffn_w1[j].astype(BF16), ffn_w3[j].astype(BF16), ffn_w2[j].astype(BF16))
        else:
            router_pad = jnp.pad(moe_router[j].astype(F32), ((0, 0), (0, ROUTER_LANES - N_EXPERTS)))
            x = _moe(xn2, *packed, x1, router_pad,
                     moe_w1[j].astype(BF16), moe_w3[j].astype(BF16), moe_w2[j].astype(BF16))
        xa = xb = x

        wa = min(A_PREV_CHUNKS * CHUNK, tp)
        wd = min(D_PREV_CHUNKS * CHUNK, tp)
        tail = lambda a, w, heads: jnp.stack(
            [a[(b + 1) * tp - w:(b + 1) * tp] for b in range(bp)]).reshape(bp, w, heads, HEAD_DIM)
        p_states.append((tail(ak, wa, 2), tail(av, wa, 2), hb[tp - 1:n_p:tp], wkv_p, ret_p,
                         tail(dk, wd, N_HEADS), tail(dv, wd, N_HEADS)))
        roll_in = lambda cache, new, heads: jnp.concatenate(
            [cache.astype(F32), new[n_p:].reshape(bs, ts, heads, HEAD_DIM)], axis=1)[:, -cache.shape[1]:]
        s_states.append((roll_in(cache_a_k[l], ak, 2), roll_in(cache_a_v[l], av, 2),
                         hb[n_p + ts - 1::ts], wkv_s, ret_s,
                         roll_in(cache_d_k[l], dk, N_HEADS), roll_in(cache_d_v[l], dv, N_HEADS)))

    yp = x[:n_p].reshape(bp, tp, D_MODEL)
    ys = x[n_p:].reshape(bs, ts, D_MODEL)
    st = lambda group, i: jnp.stack([g[i] for g in group], axis=0)
    return (yp, ys,
            st(p_states, 0), st(p_states, 1), st(p_states, 2), st(p_states, 3), st(p_states, 4), st(p_states, 5), st(p_states, 6),
            st(s_states, 0), st(s_states, 1), st(s_states, 2), st(s_states, 3), st(s_states, 4), st(s_states, 5), st(s_states, 6))
```

```python
import functools

import jax
import jax.numpy as jnp
from jax import lax
from jax.experimental import pallas as pl
from jax.experimental.pallas import tpu as pltpu
from jax.experimental.pallas import tpu_sc as plsc

F32 = jnp.float32
BF16 = jnp.bfloat16

D_MODEL = 1024
DEPTH = 2
PAST_LEN = 4096
CHUNK = 64
HEAD_DIM = 64
N_HEADS = 4
MIX_W = N_HEADS * HEAD_DIM
A_KV_W = 128
A_PREV_CHUNKS = 2
D_PREV_CHUNKS = 8
D_REL_CLIP = 128
B_PROJ = 1024
C_PROJ = 1024
IN_PROJ = 3328
B_GN_EPS = 64e-5
C_GN_EPS = 1e-6
NORM_EPS = 1e-6
ATTN_SCALE = 0.125
ROPE_BASE = 10000.0
D_FF = 2816
N_EXPERTS = 8
E_FF = 3584
NEG_BIG = -1e30

VMEM_LIMIT = 48 * 1024 * 1024

NN = ((1,), (0,))
NT = ((1,), (1,))
TN = ((0,), (0,))


def _dg(a, b, dims=NN):
    return lax.dot_general(a, b, (dims, ((), ())), preferred_element_type=F32)


def _parts(x, n):
    out = []
    r = x
    for i in range(n):
        p = r.astype(BF16)
        out.append(p)
        if i + 1 < n:
            r = r - p.astype(F32)
    return out


def _mm(a, b, dims=NN, passes=1):
    if passes == 1:
        return _dg(a.astype(BF16), b.astype(BF16), dims)
    ah, al = _parts(a, 2)
    bh, bl = _parts(b, 2)
    return _dg(ah, bh, dims) + (_dg(ah, bl, dims) + _dg(al, bh, dims))


def _mm_exact_rhs(a, b_bf, dims=NN, n=3):
    acc = None
    for p in _parts(a, n):
        t = _dg(p, b_bf, dims)
        acc = t if acc is None else acc + t
    return acc


def _iota2(shape, dim):
    return lax.broadcasted_iota(jnp.int32, shape, dim)


def _head_mask(rows, cols=MIX_W):
    return (_iota2((rows, cols), 0) >> 6) == (_iota2((rows, cols), 1) >> 6)


def _seg_matrix(width, value):
    m = _head_mask(width, width)
    return jnp.where(m, value, 0.0).astype(BF16)


def _tile4(z):
    return jnp.concatenate([z, z, z, z], axis=0)


def _fold4(z):
    return (z[0:64] + z[64:128]) + (z[128:192] + z[192:256])


def _sigmoid(x):
    return 1.0 / (1.0 + jnp.exp(-x))


def _cparams(sem):
    return pltpu.CompilerParams(dimension_semantics=sem, vmem_limit_bytes=VMEM_LIMIT)


IN_TM = 512


def _two_source_specs(xa, xb, n):
    na, nb = xa.shape[0] // IN_TM, xb.shape[0] // IN_TM
    spec_a = pl.BlockSpec((IN_TM, D_MODEL), lambda i: (jnp.minimum(i, na - 1), 0))
    spec_b = pl.BlockSpec((IN_TM, D_MODEL), lambda i: (jnp.clip(i - na, 0, nb - 1), 0))
    return na, n // IN_TM, spec_a, spec_b


def _inproj_kernel(xa_ref, xb_ref, g_ref, w_ref, aqg_ref, akg_ref, dqg_ref, dkg_ref,
                   aq_ref, ak_ref, av_ref, hb_ref, hc_ref, dq_ref, dk_ref, dv_ref, *, n_first):
    x = jnp.where(pl.program_id(0) < n_first, xa_ref[...], xb_ref[...])
    ms = jnp.mean(x * x, axis=-1, keepdims=True)
    xn = ((x * lax.rsqrt(ms + NORM_EPS)) * g_ref[...]).astype(BF16)
    seg = _seg_matrix(MIX_W, 1.0 / HEAD_DIM)

    def proj(lo, hi):
        return jnp.dot(xn, w_ref[:, lo:hi], preferred_element_type=F32)

    def head_rms(h, gain_ref):
        w = h.shape[-1]
        msq = _mm_exact_rhs(h * h, seg[:w, :w], n=2)
        return (h * lax.rsqrt(msq + NORM_EPS)) * gain_ref[...]

    aq_ref[...] = head_rms(proj(0, 256), aqg_ref)
    ak_ref[...] = head_rms(proj(256, 384), akg_ref)
    av_ref[...] = proj(384, 512)
    hb_ref[...] = proj(512, 1536)
    hc_ref[...] = proj(1536, 2560)
    dq_ref[...] = head_rms(proj(2560, 2816), dqg_ref)
    dk_ref[...] = head_rms(proj(2816, 3072), dkg_ref)
    dv_ref[...] = proj(3072, 3328)


def _inproj(xa, xb, n, g, w_bf, aqg, akg, dqg, dkg):
    na, nblk, spec_a, spec_b = _two_source_specs(xa, xb, n)
    widths = (256, 128, 128, B_PROJ, C_PROJ, 256, 256, 256)
    row = lambda w: pl.BlockSpec((IN_TM, w), lambda i: (i, 0))
    full = lambda a: pl.BlockSpec(a.shape, lambda i: (0,) * a.ndim)
    return pl.pallas_call(
        functools.partial(_inproj_kernel, n_first=na),
        grid=(nblk,),
        in_specs=[spec_a, spec_b, full(g), full(w_bf), full(aqg), full(akg), full(dqg), full(dkg)],
        out_specs=[row(w) for w in widths],
        out_shape=[jax.ShapeDtypeStruct((n, w), F32) for w in widths],
        compiler_params=_cparams(("parallel",)),
        name="inproj",
    )(xa, xb, g, w_bf, aqg, akg, dqg, dkg)


PACK_W = 256


def _pack_bf16_pairs(hi, lo):
    bits = lambda z: pltpu.bitcast(z.astype(BF16).astype(F32), jnp.int32)
    return bits(hi) | lax.shift_right_logical(bits(lo), jnp.full(lo.shape, 16, jnp.int32))


def _unpack_bf16_pairs(w):
    hi = pltpu.bitcast(w & jnp.int32(-65536), F32)
    lo = pltpu.bitcast(lax.shift_left(w, jnp.full(w.shape, 16, jnp.int32)), F32)
    return hi, lo


def _pack_rows(x):
    return (_pack_bf16_pairs(x[:, 0:PACK_W], x[:, PACK_W:2 * PACK_W]),
            _pack_bf16_pairs(x[:, 2 * PACK_W:3 * PACK_W], x[:, 3 * PACK_W:4 * PACK_W]))


def _unpack_rows(wa, wb):
    return jnp.concatenate(_unpack_bf16_pairs(wa) + _unpack_bf16_pairs(wb), axis=1)


def _outproj_kernel(xa_ref, xb_ref, *refs, n_first_x, n_first_mix, pack):
    mix_refs, (w_ref, g_ref), outs = refs[:8], refs[8:10], refs[10:]
    i = pl.program_id(0)
    acc = jnp.where(i < n_first_x, xa_ref[...], xb_ref[...])
    for m in range(4):
        o = jnp.where(i < n_first_mix, mix_refs[2 * m][...], mix_refs[2 * m + 1][...])
        acc = acc + jnp.dot(o.astype(BF16), w_ref[m * MIX_W:(m + 1) * MIX_W, :], preferred_element_type=F32)
    outs[0][...] = acc
    ms = jnp.mean(acc * acc, axis=-1, keepdims=True)
    xn = (acc * lax.rsqrt(ms + NORM_EPS)) * g_ref[...]
    outs[1][...] = xn.astype(BF16)
    if pack:
        outs[2][...], outs[3][...] = _pack_rows(xn)


def _outproj(xa, xb, n, mix, w_bf, g2, *, pack):
    na, nblk, spec_a, spec_b = _two_source_specs(xa, xb, n)
    nm_p, nm_s = mix[0][0].shape[0] // IN_TM, mix[0][1].shape[0] // IN_TM
    row = lambda w: pl.BlockSpec((IN_TM, w), lambda i: (i, 0))
    full = lambda a: pl.BlockSpec(a.shape, lambda i: (0,) * a.ndim)
    mix_specs = [pl.BlockSpec((IN_TM, MIX_W), lambda i: (jnp.minimum(i, nm_p - 1), 0)),
                 pl.BlockSpec((IN_TM, MIX_W), lambda i: (jnp.clip(i - nm_p, 0, nm_s - 1), 0))] * 4
    out_specs = [row(D_MODEL), row(D_MODEL)]
    out_shape = [jax.ShapeDtypeStruct((n, D_MODEL), F32), jax.ShapeDtypeStruct((n, D_MODEL), BF16)]
    if pack:
        out_specs += [row(PACK_W), row(PACK_W)]
        out_shape += [jax.ShapeDtypeStruct((n, PACK_W), jnp.int32)] * 2
    return pl.pallas_call(
        functools.partial(_outproj_kernel, n_first_x=na, n_first_mix=nm_p, pack=pack),
        grid=(nblk,),
        in_specs=[spec_a, spec_b] + mix_specs + [full(w_bf), full(g2)],
        out_specs=out_specs,
        out_shape=out_shape,
        compiler_params=_cparams(("parallel",)),
        name="outproj",
    )(xa, xb, *[a for pair in mix for a in pair], w_bf, g2)


FFN_TM = 512
FFN_TF = 1408


def _ffn_kernel(xn_ref, x1_ref, w1_ref, w3_ref, w2_ref, o_ref):
    f = pl.program_id(1)
    xn = xn_ref[...]
    a = jnp.dot(xn, w1_ref[...], preferred_element_type=F32)
    b = jnp.dot(xn, w3_ref[...], preferred_element_type=F32)
    h = ((a * _sigmoid(a)) * b).astype(BF16)
    y = jnp.dot(h, w2_ref[...], preferred_element_type=F32)

    @pl.when(f == 0)
    def _():
        o_ref[...] = x1_ref[...] + y

    @pl.when(f != 0)
    def _():
        o_ref[...] += y


def _ffn(xn, x1, w1_bf, w3_bf, w2_bf):
    n = xn.shape[0]
    return pl.pallas_call(
        _ffn_kernel,
        grid=(n // FFN_TM, D_FF // FFN_TF),
        in_specs=[
            pl.BlockSpec((FFN_TM, D_MODEL), lambda i, f: (i, 0)),
            pl.BlockSpec((FFN_TM, D_MODEL), lambda i, f: (i, 0)),
            pl.BlockSpec((D_MODEL, FFN_TF), lambda i, f: (0, f)),
            pl.BlockSpec((D_MODEL, FFN_TF), lambda i, f: (0, f)),
            pl.BlockSpec((FFN_TF, D_MODEL), lambda i, f: (f, 0)),
        ],
        out_specs=pl.BlockSpec((FFN_TM, D_MODEL), lambda i, f: (i, 0)),
        out_shape=jax.ShapeDtypeStruct((n, D_MODEL), F32),
        compiler_params=_cparams(("parallel", "arbitrary")),
        name="ffn",
    )(xn, x1, w1_bf, w3_bf, w2_bf)


ROUTER_LANES = 128


def _router_kernel(xn_ref, r_ref, gate_ref, rank_ref, cnt_ref, cnt_scr):
    @pl.when(pl.program_id(0) == 0)
    def _():
        cnt_scr[...] = jnp.zeros_like(cnt_scr)

    xn = xn_ref[...].astype(F32)
    logits = _mm(xn, r_ref[...], passes=3)
    lane = _iota2(logits.shape, 1)
    logits = jnp.where(lane < N_EXPERTS, logits, NEG_BIG)
    m1 = jnp.max(logits, axis=-1, keepdims=True)
    i1 = jnp.min(jnp.where(logits == m1, lane, ROUTER_LANES), axis=-1, keepdims=True)
    rest = jnp.where(lane == i1, NEG_BIG, logits)
    m2 = jnp.max(rest, axis=-1, keepdims=True)
    i2 = jnp.min(jnp.where(rest == m2, lane, ROUTER_LANES), axis=-1, keepdims=True)
    e2 = jnp.exp(m2 - m1)
    den = 1.0 + e2
    gates = jnp.where(lane == i1, 1.0 / den, 0.0) + jnp.where(lane == i2, e2 / den, 0.0)
    gate_ref[...] = gates
    sel = jnp.where(gates > 0.0, 1.0, 0.0)
    tm = sel.shape[0]
    before = jnp.where(_iota2((tm, tm), 1) < _iota2((tm, tm), 0), 1.0, 0.0).astype(BF16)
    rank_ref[...] = (_dg(before, sel.astype(BF16)) + cnt_scr[...]).astype(jnp.int32)
    cnt_scr[...] = cnt_scr[...] + jnp.sum(sel, axis=0, keepdims=True)
    cnt_ref[...] = cnt_scr[...].astype(jnp.int32)


def _router(xn, router_pad):
    n = xn.shape[0]
    row = pl.BlockSpec((IN_TM, ROUTER_LANES), lambda i: (i, 0))
    return pl.pallas_call(
        _router_kernel,
        grid=(n // IN_TM,),
        in_specs=[pl.BlockSpec((IN_TM, D_MODEL), lambda i: (i, 0)),
                  pl.BlockSpec(router_pad.shape, lambda i: (0, 0))],
        out_specs=[row, row, pl.BlockSpec((1, ROUTER_LANES), lambda i: (0, 0))],
        out_shape=[jax.ShapeDtypeStruct((n, ROUTER_LANES), F32),
                   jax.ShapeDtypeStruct((n, ROUTER_LANES), jnp.int32),
                   jax.ShapeDtypeStruct((1, ROUTER_LANES), jnp.int32)],
        scratch_shapes=[pltpu.VMEM((1, ROUTER_LANES), F32)],
        compiler_params=_cparams(("arbitrary",)),
        name="router",
    )(xn, router_pad)


MOE_R = 512
MOE_TF = 1792
SC_WINDOW = 128


def _moe_plan(gates, rank, counts, n):
    n_blocks = (2 * n) // MOE_R + N_EXPERTS + 1
    spare_row = (n_blocks - 1) * MOE_R
    sel = gates[:, :N_EXPERTS] > 0.0
    rank = rank[:, :N_EXPERTS]
    counts = counts[0, :N_EXPERTS]
    padded = ((counts + MOE_R - 1) // MOE_R) * MOE_R
    pad_end = jnp.cumsum(padded)
    pad_start = pad_end - padded
    pos = jnp.where(sel, pad_start[None, :] + rank, -1)
    order = jnp.cumsum(sel.astype(jnp.int32), axis=1)
    pick = lambda j: jnp.max(jnp.where(jnp.logical_and(sel, order == j), pos, -1), axis=1)
    to_row = lambda p: jnp.where(p >= 0, p, spare_row).astype(jnp.int32).reshape(1, n)
    block_expert = jnp.minimum(
        jnp.sum(pad_end[None, :] <= (jnp.arange(n_blocks) * MOE_R)[:, None], axis=1), N_EXPERTS - 1)
    return dict(n_blocks=n_blocks, pos0=to_row(pick(1)), pos1=to_row(pick(2)),
                block_expert=block_expert.astype(jnp.int32), n_used=(pad_end[-1:] // MOE_R).astype(jnp.int32))


def _sc_mesh():
    return plsc.VectorSubcoreMesh(core_axis_name="core", subcore_axis_name="subcore")


def _sc_scatter_rows(table, idx_lists, n_rows):
    n, cols = table.shape
    k = len(idx_lists)

    @functools.partial(pl.kernel, out_type=jax.ShapeDtypeStruct((n_rows, cols), table.dtype), mesh=_sc_mesh())
    def scatter(x_hbm, *rest):
        i_hbms, o_hbm = rest[:k], rest[k]

        def body(x_vmem, *i_vmems):
            for i_vmem in i_vmems:
                pltpu.sync_copy(x_vmem, o_hbm.at[i_vmem.at[0]])

        pltpu.emit_pipeline(
            body,
            grid=(n // SC_WINDOW,),
            in_specs=[pl.BlockSpec((SC_WINDOW, cols), lambda i: (i, 0))]
            + [pl.BlockSpec((1, SC_WINDOW), lambda i: (0, i))] * k,
            out_specs=[],
            core_axis_name=("core", "subcore"),
            dimension_semantics=(pltpu.PARALLEL,),
        )(x_hbm, *i_hbms)

    return scatter(table, *idx_lists)


def _sc_gather_rows(table, idx):
    n = idx.shape[1]
    cols = table.shape[1]

    @functools.partial(pl.kernel, out_type=jax.ShapeDtypeStruct((n, cols), table.dtype), mesh=_sc_mesh())
    def gather(x_hbm, i_hbm, o_hbm):
        def body(i_vmem, o_vmem):
            pltpu.sync_copy(x_hbm.at[i_vmem.at[0]], o_vmem)

        pltpu.emit_pipeline(
            body,
            grid=(n // SC_WINDOW,),
            in_specs=[pl.BlockSpec((1, SC_WINDOW), lambda i: (0, i))],
            out_specs=[pl.BlockSpec((SC_WINDOW, cols), lambda i: (i, 0))],
            core_axis_name=("core", "subcore"),
            dimension_semantics=(pltpu.PARALLEL,),
        )(i_hbm, o_hbm)

    return gather(table, idx)


def _moe_expert_kernel(be_ref, nu_ref, xa_ref, xb_ref, gs_ref, w1_ref, w3_ref, w2_ref, oa_ref, ob_ref, acc_ref):
    j, f = pl.program_id(0), pl.program_id(1)
    used = j < nu_ref[0]

    @pl.when(used)
    def _():
        x = _unpack_rows(xa_ref[...], xb_ref[...]).astype(BF16)
        a = jnp.dot(x, w1_ref[...], preferred_element_type=F32)
        b = jnp.dot(x, w3_ref[...], preferred_element_type=F32)
        h = ((a * _sigmoid(a)) * b).astype(BF16)
        y = jnp.dot(h, w2_ref[...], preferred_element_type=F32)

        @pl.when(f == 0)
        def _():
            acc_ref[...] = y

        @pl.when(f != 0)
        def _():
            acc_ref[...] += y

    @pl.when(f == pl.num_programs(1) - 1)
    def _():
        lane = _iota2(gs_ref.shape, 1)
        g = jnp.sum(jnp.where(lane == be_ref[j], gs_ref[...], 0.0), axis=1, keepdims=True)
        oa_ref[...], ob_ref[...] = _pack_rows(jnp.where(used, acc_ref[...] * g, 0.0))


def _moe_experts(plan, xs_a, xs_b, gs, w1_bf, w3_bf, w2_bf):
    n_blocks = plan["n_blocks"]
    half = pl.BlockSpec((MOE_R, PACK_W), lambda j, f, be, nu: (j, 0))
    grid_spec = pltpu.PrefetchScalarGridSpec(
        num_scalar_prefetch=2,
        grid=(n_blocks, E_FF // MOE_TF),
        in_specs=[half, half,
                  pl.BlockSpec((MOE_R, ROUTER_LANES), lambda j, f, be, nu: (j, 0)),
                  pl.BlockSpec((None, D_MODEL, MOE_TF), lambda j, f, be, nu: (be[j], 0, f)),
                  pl.BlockSpec((None, D_MODEL, MOE_TF), lambda j, f, be, nu: (be[j], 0, f)),
                  pl.BlockSpec((None, MOE_TF, D_MODEL), lambda j, f, be, nu: (be[j], f, 0))],
        out_specs=[half, half],
        scratch_shapes=[pltpu.VMEM((MOE_R, D_MODEL), F32)])
    return pl.pallas_call(
        _moe_expert_kernel,
        grid_spec=grid_spec,
        out_shape=[jax.ShapeDtypeStruct((n_blocks * MOE_R, PACK_W), jnp.int32)] * 2,
        compiler_params=_cparams(("arbitrary", "arbitrary")),
        name="moe_experts",
    )(plan["block_expert"], plan["n_used"], xs_a, xs_b, gs, w1_bf, w3_bf, w2_bf)


def _moe_combine_kernel(x1_ref, a0_ref, b0_ref, a1_ref, b1_ref, o_ref):
    o_ref[...] = (x1_ref[...] + _unpack_rows(a0_ref[...], b0_ref[...])) + _unpack_rows(a1_ref[...], b1_ref[...])


def _moe_combine(x1, picked):
    n = x1.shape[0]
    row = lambda w: pl.BlockSpec((IN_TM, w), lambda i: (i, 0))
    return pl.pallas_call(
        _moe_combine_kernel,
        grid=(n // IN_TM,),
        in_specs=[row(D_MODEL)] + [row(PACK_W)] * 4,
        out_specs=row(D_MODEL),
        out_shape=jax.ShapeDtypeStruct((n, D_MODEL), F32),
        compiler_params=_cparams(("parallel",)),
        name="moe_combine",
    )(x1, *picked)


def _moe(xn, xn_a, xn_b, x1, router_pad, w1_bf, w3_bf, w2_bf):
    gates, rank, counts = _router(xn, router_pad)
    plan = _moe_plan(gates, rank, counts, xn.shape[0])
    n_rows = plan["n_blocks"] * MOE_R
    idx = (plan["pos0"], plan["pos1"])
    xs_a = _sc_scatter_rows(xn_a, idx, n_rows)
    xs_b = _sc_scatter_rows(xn_b, idx, n_rows)
    gs = _sc_scatter_rows(gates, idx, n_rows)
    os_a, os_b = _moe_experts(plan, xs_a, xs_b, gs, w1_bf, w3_bf, w2_bf)
    picked = [_sc_gather_rows(t, p) for p in idx for t in (os_a, os_b)]
    return _moe_combine(x1, picked)


def _relbias_kernel(rb_ref, o_ref, *, nk):
    h = pl.program_id(0)
    q = _iota2((CHUNK, nk), 0)
    r = _iota2((CHUNK, nk), 1)
    idx = jnp.clip(q - (r - (nk - CHUNK)), -D_REL_CLIP, D_REL_CLIP) + D_REL_CLIP

    def body(j, acc):
        return jnp.where(idx == j, rb_ref[h, j], acc)

    o_ref[...] = lax.fori_loop(0, 2 * D_REL_CLIP + 1, body, jnp.zeros((CHUNK, nk), F32))


def _relbias_table(rel_bias, nk):
    return pl.pallas_call(
        functools.partial(_relbias_kernel, nk=nk),
        grid=(N_HEADS,),
        in_specs=[pl.BlockSpec(memory_space=pltpu.SMEM)],
        out_specs=pl.BlockSpec((CHUNK, nk), lambda h: (h, 0)),
        out_shape=jax.ShapeDtypeStruct((N_HEADS * CHUNK, nk), F32),
        name="relbias",
    )(rel_bias)


ATTN_QB = 8
ATTN_GROUP = 4


def _attn_kernel(q_ref, kp_ref, kc_ref, vp_ref, vc_ref, x_ref, *rest, qb, n_prev, use_sink, mask_first):
    o_ref, kbuf, vbuf = rest[-3:]
    i = pl.program_id(1)
    p_rows = kp_ref.shape[0]
    nk = (n_prev + 1) * CHUNK
    wk = kp_ref.shape[1]

    kbuf[0:p_rows, :] = kp_ref[...].astype(BF16)
    kbuf[p_rows:, :] = kc_ref[...].astype(BF16)
    vbuf[0:p_rows, :] = vp_ref[...].astype(BF16)
    vbuf[p_rows:, :] = vc_ref[...].astype(BF16)

    hmask = _head_mask(N_HEADS * CHUNK)
    extra = x_ref[...]
    grouped = wk != MIX_W
    low = _iota2((CHUNK, A_KV_W), 1) < HEAD_DIM

    def stack_queries(qj):
        if not grouped:
            return jnp.where(hmask, _tile4(qj), 0.0)
        shifted = pltpu.roll(qj, MIX_W - HEAD_DIM, 1)[:, :A_KV_W]
        return jnp.concatenate([jnp.where(low, qj[:, :A_KV_W], 0.0), jnp.where(low, shifted, 0.0),
                                jnp.where(low, 0.0, shifted), jnp.where(low, 0.0, qj[:, A_KV_W:])], axis=0)

    def unstack_outputs(o_all):
        if not grouped:
            return _fold4(jnp.where(hmask, o_all, 0.0))
        b0, b1, b2, b3 = (o_all[h * CHUNK:(h + 1) * CHUNK] for h in range(N_HEADS))
        left = jnp.where(low, b0, 0.0) + pltpu.roll(jnp.where(low, b1, 0.0), HEAD_DIM, 1)
        right = pltpu.roll(jnp.where(low, 0.0, b2), HEAD_DIM, 1) + jnp.where(low, 0.0, b3)
        return jnp.concatenate([left, right], axis=1)

    def scores(j):
        base = p_rows + (j - n_prev) * CHUNK
        qs = stack_queries(q_ref[pl.ds(j * CHUNK, CHUNK), :] * ATTN_SCALE).astype(BF16)
        s = _dg(qs, kbuf[pl.ds(base, nk), :], NT)
        if not use_sink:
            s = s + extra
        if mask_first and base < p_rows:
            krow = base + _iota2(s.shape, 1)
            s = jnp.where(jnp.logical_and(i == 0, krow < p_rows), NEG_BIG, s)
        return s

    def weights(s):
        m = jnp.max(s, axis=-1, keepdims=True)
        if use_sink:
            m = jnp.maximum(m, extra)
        e = jnp.exp(s - m)
        den = jnp.sum(e, axis=-1, keepdims=True)
        if use_sink:
            den = den + jnp.exp(extra - m)
        return e.astype(BF16), 1.0 / den

    def output(j, e, inv_den):
        base = p_rows + (j - n_prev) * CHUNK
        o_all = _dg(e, vbuf[pl.ds(base, nk), :]) * inv_den
        o_ref[pl.ds(j * CHUNK, CHUNK), :] = unstack_outputs(o_all)

    for j0 in range(0, qb, ATTN_GROUP):
        group = range(j0, min(j0 + ATTN_GROUP, qb))
        ss = [scores(j) for j in group]
        ws = [weights(s) for s in ss]
        for j, (e, inv_den) in zip(group, ws):
            output(j, e, inv_den)


def _attention(q, k, v, prev, extra, *, n_prev, use_sink, n_streams, t, base_row):
    wk = k.shape[-1]
    if prev is None:
        qb = ATTN_QB
        rows = qb * CHUNK
        nblk = t // rows
        base = base_row // rows
        prev_spec = pl.BlockSpec((rows, wk), lambda s, i: (base + s * nblk + jnp.maximum(i - 1, 0), 0))
        k_prev, v_prev, p_rows, mask_first = k, v, rows, True
    else:
        qb, rows, nblk = t // CHUNK, t, 1
        base = base_row // rows
        k_prev, v_prev = prev
        p_rows = k_prev.shape[1]
        prev_spec = pl.BlockSpec((None, p_rows, wk), lambda s, i: (s, 0, 0))
        mask_first = False
    cur = lambda w: pl.BlockSpec((rows, w), lambda s, i: (base + s * nblk + i, 0))
    kern = functools.partial(_attn_kernel, qb=qb, n_prev=n_prev, use_sink=use_sink, mask_first=mask_first)
    return pl.pallas_call(
        kern,
        grid=(n_streams, nblk),
        in_specs=[cur(MIX_W), prev_spec, cur(wk), prev_spec, cur(wk),
                  pl.BlockSpec(extra.shape, lambda s, i: (0, 0))],
        out_specs=pl.BlockSpec((rows, MIX_W), lambda s, i: (s * nblk + i, 0)),
        out_shape=jax.ShapeDtypeStruct((n_streams * t, MIX_W), F32),
        scratch_shapes=[pltpu.VMEM((p_rows + rows, wk), BF16), pltpu.VMEM((p_rows + rows, wk), BF16)],
        compiler_params=_cparams(("parallel", "arbitrary")),
        name="attn_sink" if use_sink else "attn_bias",
    )(q, k_prev, k, v_prev, v, extra)


def _head_layer_norm(o, seg_mean_bf, w, b, eps):
    mu = _mm_exact_rhs(o, seg_mean_bf)
    d = o - mu
    var = _mm_exact_rhs(d * d, seg_mean_bf)
    return (d * lax.rsqrt(var + eps)) * w + b


def _ret_kernel(hc_ref, cos_ref, sin_ref, s0_ref, dstack_ref, qsc_ref, ksc_ref, gam_ref, lnw_ref, lnb_ref,
                *rest, qb):
    o_ref, sout_ref, s_scr = rest[-3:]
    i = pl.program_id(1)

    @pl.when(i == 0)
    def _():
        s_scr[...] = s0_ref[...]

    hmask = _head_mask(N_HEADS * CHUNK)
    seg_mean = _seg_matrix(MIX_W, 1.0 / HEAD_DIM)
    rows = qb * CHUNK
    first_half = (_iota2((rows, MIX_W), 1) & (HEAD_DIM - 1)) < (HEAD_DIM // 2)
    cos = cos_ref[...]
    sin = sin_ref[...]

    def rope(x):
        partner = jnp.where(first_half, pltpu.roll(x, MIX_W - HEAD_DIM // 2, 1), pltpu.roll(x, HEAD_DIM // 2, 1))
        return x * cos + partner * sin

    q = rope(hc_ref[:, 0:256])
    k = rope(hc_ref[:, 256:512]) * ATTN_SCALE
    v_bf = hc_ref[:, 512:768].astype(BF16)
    state = s_scr[...]
    outs = []
    for j in range(qb):
        sl = slice(j * CHUNK, (j + 1) * CHUNK)
        qj, kj, vj = q[sl], k[sl], v_bf[sl]
        qs = jnp.where(hmask, _tile4(qj), 0.0).astype(BF16)
        sc = _dg(qs, kj.astype(BF16), NT) * dstack_ref[...]
        intra = _fold4(jnp.where(hmask, _dg(sc.astype(BF16), vj), 0.0))
        inter = _dg((qj * qsc_ref[...]).astype(BF16), state.astype(BF16))
        kv = _dg((kj * ksc_ref[...]).astype(BF16), vj, TN)
        state = gam_ref[...] * state + jnp.where(hmask, kv, 0.0)
        outs.append(intra + inter)
    s_scr[...] = state
    sout_ref[...] = state
    y = _head_layer_norm(jnp.concatenate(outs, axis=0), seg_mean, lnw_ref[...], lnb_ref[...], C_GN_EPS)
    g = hc_ref[:, 768:1024]
    o_ref[...] = y * (g * _sigmoid(g))


def _retention(hc, cos, sin, s0_bd, tabs, lnw, lnb, *, n_streams, t, base_row):
    qb = min(8, t // CHUNK)
    rows = qb * CHUNK
    nblk = t // rows
    base = base_row // rows
    dstack, qsc, ksc, gam = tabs
    full = lambda a: pl.BlockSpec(a.shape, lambda s, i: (0,) * a.ndim)
    cur = lambda w: pl.BlockSpec((rows, w), lambda s, i: (base + s * nblk + i, 0))
    state = pl.BlockSpec((None, MIX_W, MIX_W), lambda s, i: (s, 0, 0))
    return pl.pallas_call(
        functools.partial(_ret_kernel, qb=qb),
        grid=(n_streams, nblk),
        in_specs=[cur(C_PROJ),
                  pl.BlockSpec((rows, MIX_W), lambda s, i: (i, 0)),
                  pl.BlockSpec((rows, MIX_W), lambda s, i: (i, 0)),
                  state, full(dstack), full(qsc), full(ksc), full(gam), full(lnw), full(lnb)],
        out_specs=[pl.BlockSpec((rows, MIX_W), lambda s, i: (s * nblk + i, 0)), state],
        out_shape=[jax.ShapeDtypeStruct((n_streams * t, MIX_W), F32),
                   jax.ShapeDtypeStruct((n_streams, MIX_W, MIX_W), F32)],
        scratch_shapes=[pltpu.VMEM((MIX_W, MIX_W), F32)],
        compiler_params=_cparams(("parallel", "arbitrary")),
        name="retention",
    )(hc, cos, sin, s0_bd, dstack, qsc, ksc, gam, lnw, lnb)


def _retention_tables():
    hh = jnp.arange(N_HEADS, dtype=F32)
    log_gamma = jnp.log(1.0 - 2.0 ** (-5.0 - hh))
    t = jnp.arange(CHUNK)
    diff = t[:, None] - t[None, :]
    dmat = jnp.where(diff >= 0, jnp.exp(log_gamma[:, None, None] * jnp.maximum(diff, 0)), 0.0)
    dstack = dmat.reshape(N_HEADS * CHUNK, CHUNK)
    lanes = lambda per_head: jnp.repeat(per_head, HEAD_DIM, axis=-1)
    qsc = lanes(jnp.exp((t + 1)[:, None] * log_gamma[None, :]))
    ksc = lanes(jnp.exp((CHUNK - 1 - t)[:, None] * log_gamma[None, :]))
    gam = jnp.broadcast_to(lanes(jnp.exp(log_gamma * CHUNK))[:, None], (MIX_W, MIX_W))
    return dstack.astype(F32), qsc.astype(F32), ksc.astype(F32), gam.astype(F32)


def _rope_tables(pos):
    half = HEAD_DIM // 2
    theta = 1.0 / (ROPE_BASE ** jnp.linspace(0.0, 1.0, half, dtype=F32))
    ang = pos.astype(F32)[:, None] * theta[None, :]
    cos, sin = jnp.cos(ang), jnp.sin(ang)
    cos_t = jnp.tile(jnp.concatenate([cos, cos], axis=-1), (1, N_HEADS))
    sin_t = jnp.tile(jnp.concatenate([-sin, sin], axis=-1), (1, N_HEADS))
    return cos_t, sin_t


DECAY_SCALE = 0.6065306597126334
RWKV_CB = 4
N_LEVELS = 6
MASK_HEAD, MASK_STRICT, MASK_INCL, MASK_LEVEL0 = 0, 1, 2, 3


def _rwkv_masks():
    n4 = N_HEADS * CHUNK
    ri = jnp.arange(n4)[:, None]
    ci = jnp.arange(n4)[None, :]
    head = (ri >> 6) == (ci >> 6)
    tabs = [head, head & ((ci & 63) < (ri & 63)), head & ((ci & 63) <= (ri & 63))]
    for log_m in range(N_LEVELS):
        same = (ri >> (log_m + 1)) == (ci >> (log_m + 1))
        tabs.append(same & (((ri >> log_m) & 1) == 1) & (((ci >> log_m) & 1) == 0))
    return jnp.stack(tabs).astype(BF16)


def _rwkv_kernel(hb_ref, shift0_ref, h0_ref, masks_ref, mu_ref, w0_ref, w2_ref, a0_ref, a2_ref, g2_ref,
                 kk_ref, ka_ref, rk_ref, lnw_ref, lnb_ref, *rest, cb, independent):
    o_ref, hout_ref, h_scr, shift_scr = rest[-4:]
    c = pl.program_id(1)
    rows = cb * CHUNK
    xb = hb_ref[...]
    row = _iota2(xb.shape, 0)
    prev = pltpu.roll(xb, 1, 0)
    if independent:
        for j in range(cb):
            prev = jnp.where(row == j * CHUNK, shift0_ref[j], prev)
    else:
        @pl.when(c == 0)
        def _():
            h_scr[...] = h0_ref[0]
            shift_scr[...] = shift0_ref[0]

        prev = jnp.where(row == 0, shift_scr[...], prev)
        shift_scr[...] = xb[rows - 1:rows, :]
    xs = xb + mu_ref[...] * (prev - xb)
    r = xs[:, 0:256]
    k = xs[:, 256:512]
    v = xs[:, 512:768]
    xw = xs[:, 768:832]
    xa = xs[:, 832:896]
    xg = xs[:, 896:1024]

    z = w0_ref[...] + _mm(jnp.tanh(xw), w2_ref[...], passes=3)
    lw = -DECAY_SCALE * _sigmoid(z)
    a_gate = _sigmoid(a0_ref[...] + _mm(xa, a2_ref[...], passes=3))
    gate = _mm(_sigmoid(xg), g2_ref[...], passes=1)

    seg_sum = _seg_matrix(MIX_W, 1.0)
    seg_mean = _seg_matrix(MIX_W, 1.0 / HEAD_DIM)
    kkn = k * kk_ref[...]
    norm = jnp.sqrt(_mm_exact_rhs(kkn * kkn, seg_sum))
    kk = kkn / jnp.maximum(norm, 1e-12)
    kf = k * (1.0 + (a_gate - 1.0) * ka_ref[...])

    tt = _iota2((rows, rows), 0)
    ss = _iota2((rows, rows), 1)
    tril = jnp.where(jnp.logical_and(ss <= tt, (ss >> 6) == (tt >> 6)), 1.0, 0.0).astype(BF16)
    lw_parts = _parts(lw, 3)
    cum = _dg(tril, lw_parts[0]) + (_dg(tril, lw_parts[1]) + _dg(tril, lw_parts[2]))
    w_inv = jnp.exp(-cum)
    rho = (r * jnp.exp(cum)).astype(BF16)
    alpha = (-kk * jnp.exp(cum - lw)).astype(BF16)
    beta = ((kk * a_gate) * w_inv).astype(BF16)
    kappa = (kf * w_inv).astype(BF16)
    v_bf = v.astype(BF16)
    ones = jnp.ones((CHUNK, MIX_W), BF16)

    hmask = masks_ref[MASK_HEAD]
    n4 = N_HEADS * CHUNK
    eye = jnp.where(_iota2((n4, n4), 0) == _iota2((n4, n4), 1), 1.0, 0.0)

    pre, a_bfs, t_invs = [], [], []
    for j in range(cb):
        sl = slice(j * CHUNK, (j + 1) * CHUNK)
        bd = lambda zz: _tile4(zz[sl]) * hmask
        al_bd, be_bd, ka_bd, rh_bd, v_bd = bd(alpha), bd(beta), bd(kappa), bd(rho), bd(v_bf)
        a_bf = _dg(al_bd, be_bd, NT).astype(BF16) * masks_ref[MASK_STRICT]
        a_ak = _dg(al_bd, ka_bd, NT).astype(BF16) * masks_ref[MASK_STRICT]
        b_rb = _dg(rh_bd, be_bd, NT).astype(BF16) * masks_ref[MASK_INCL]
        b_rk = _dg(rh_bd, ka_bd, NT).astype(BF16) * masks_ref[MASK_INCL]
        x0 = _dg(a_ak, v_bd)
        y0 = _dg(b_rk, v_bd)
        hn0 = _dg(ka_bd, v_bd, TN)
        ctot = (_dg(lw_parts[0][sl], ones, TN)
                + (_dg(lw_parts[1][sl], ones, TN) + _dg(lw_parts[2][sl], ones, TN)))
        a_bfs.append(a_bf)
        t_invs.append(eye + (a_bf * masks_ref[MASK_LEVEL0]).astype(F32))
        pre.append([al_bd, be_bd, rh_bd, b_rb, None, x0, y0, hn0, jnp.exp(ctot)])
    for lvl in range(1, N_LEVELS):
        t_bfs = [t.astype(BF16) for t in t_invs]
        e_mats = [_dg(a_bfs[j] * masks_ref[MASK_LEVEL0 + lvl], t_bfs[j]) for j in range(cb)]
        t_invs = [t_invs[j] + _dg(t_bfs[j], e_mats[j].astype(BF16)) for j in range(cb)]
    for j in range(cb):
        pre[j][4] = t_invs[j].astype(BF16)

    ys = []
    h = None if independent else h_scr[...]
    for j in range(cb):
        al_bd, be_bd, rh_bd, b_rb, t_bf, x0, y0, hn0, wc = pre[j]
        h0 = h0_ref[j] if independent else h
        h0_bf = h0.astype(BF16)
        x_mat = _dg(al_bd, h0_bf) + x0
        u_bf = _dg(t_bf, x_mat.astype(BF16)).astype(BF16)
        y_bd = _dg(rh_bd, h0_bf) + _dg(b_rb, u_bf) + y0
        h_new = wc * (h0 + _dg(be_bd, u_bf, TN) + hn0)
        ys.append(_fold4(y_bd))
        if independent:
            hout_ref[j] = h_new
        else:
            h = h_new
    if not independent:
        h_scr[...] = h
        hout_ref[0] = h

    y = _head_layer_norm(jnp.concatenate(ys, axis=0), seg_mean, lnw_ref[...], lnb_ref[...], B_GN_EPS)
    bonus = _mm_exact_rhs(r * kf * rk_ref[...], seg_sum) * v
    o_ref[...] = (y + bonus) * gate


def _rwkv(hb, shift0, h0_bd, params, *, independent, n_streams, t, base_row):
    cb = RWKV_CB
    rows = cb * CHUNK
    nblk = t // rows
    base = base_row // rows
    masks = _rwkv_masks()
    full = lambda a: pl.BlockSpec(a.shape, lambda s, c: (0,) * a.ndim)
    if independent:
        assert n_streams == 1
        st_map = lambda s, c: (c, 0, 0)
        n_state, st_blk = t // CHUNK, cb
    else:
        st_map = lambda s, c: (s, 0, 0)
        n_state, st_blk = n_streams, 1
    cur = lambda w: pl.BlockSpec((rows, w), lambda s, c: (base + s * nblk + c, 0))
    return pl.pallas_call(
        functools.partial(_rwkv_kernel, cb=cb, independent=independent),
        grid=(n_streams, nblk),
        in_specs=[cur(B_PROJ),
                  pl.BlockSpec((st_blk, 1, B_PROJ), st_map),
                  pl.BlockSpec((st_blk, MIX_W, MIX_W), st_map),
                  full(masks)] + [full(p) for p in params],
        out_specs=[pl.BlockSpec((rows, MIX_W), lambda s, c: (s * nblk + c, 0)),
                   pl.BlockSpec((st_blk, MIX_W, MIX_W), st_map)],
        out_shape=[jax.ShapeDtypeStruct((n_streams * t, MIX_W), F32),
                   jax.ShapeDtypeStruct((n_state, MIX_W, MIX_W), F32)],
        scratch_shapes=[pltpu.VMEM((MIX_W, MIX_W), F32), pltpu.VMEM((1, B_PROJ), F32)],
        compiler_params=_cparams(("parallel", "arbitrary")),
        name="rwkv7",
    )(hb, shift0, h0_bd, masks, *params)


def _to_block_diag(s):
    eye = jnp.eye(N_HEADS, dtype=s.dtype)
    out = s[:, :, :, None, :] * eye[None, :, None, :, None]
    return out.reshape(s.shape[0], MIX_W, MIX_W)


def _from_block_diag(m):
    b = m.reshape(m.shape[0], N_HEADS, HEAD_DIM, N_HEADS, HEAD_DIM)
    return jnp.stack([b[:, h, :, h, :] for h in range(N_HEADS)], axis=1)


def _row(p):
    return p.reshape(1, -1).astype(F32)


def _mixers(proj, caches, lp, tabs, geom):
    aq, ak, av, hb, hc, dq, dk, dv = proj
    bp, tp, bs, ts = geom
    n_p = bp * tp
    ca_k, ca_v, sb_shift, sb_wkv, sc, cd_k, cd_v = caches
    pr = dict(n_streams=bp, t=tp, base_row=0)
    sm = dict(n_streams=bs, t=ts, base_row=n_p)
    zeros_state = jnp.zeros((bp, MIX_W, MIX_W), F32)

    oa_p = _attention(aq, ak, av, None, lp["sink_col"], n_prev=A_PREV_CHUNKS, use_sink=True, **pr)
    oa_s = _attention(aq, ak, av, (ca_k.reshape(bs, -1, A_KV_W), ca_v.reshape(bs, -1, A_KV_W)), lp["sink_col"],
                      n_prev=A_PREV_CHUNKS, use_sink=True, **sm)

    ob_p, h_p = _rwkv(hb, jnp.zeros((bp, 1, B_PROJ), F32), zeros_state, lp["rwkv"], independent=False, **pr)
    h0_s = _to_block_diag(jnp.swapaxes(sb_wkv, -1, -2))
    ob_s, h_s = _rwkv(hb, sb_shift.reshape(bs, 1, B_PROJ), h0_s, lp["rwkv"], independent=True,
                      n_streams=1, t=bs * ts, base_row=n_p)

    oc_p, s_p = _retention(hc, *tabs["rope_prompt"], zeros_state, tabs["ret"], lp["c_ln_w"], lp["c_ln_b"], **pr)
    oc_s, s_s = _retention(hc, *tabs["rope_sample"], _to_block_diag(sc), tabs["ret"], lp["c_ln_w"], lp["c_ln_b"], **sm)

    od_p = _attention(dq, dk, dv, None, lp["bias_table"], n_prev=D_PREV_CHUNKS, use_sink=False, **pr)
    od_s = _attention(dq, dk, dv, (cd_k.reshape(bs, -1, MIX_W), cd_v.reshape(bs, -1, MIX_W)), lp["bias_table"],
                      n_prev=D_PREV_CHUNKS, use_sink=False, **sm)

    wkv = lambda h: jnp.swapaxes(_from_block_diag(h), -1, -2)
    mix = ((oa_p, oa_s), (ob_p, ob_s), (oc_p, oc_s), (od_p, od_s))
    return mix, (wkv(h_p), _from_block_diag(s_p)), (wkv(h_s), _from_block_diag(s_s))


def kernel(x_prompt, x_sample, cache_a_k, cache_a_v, state_b_shift, state_b_wkv, state_c, cache_d_k, cache_d_v,
           norm1_g, norm2_g, w_in, w_out, a_q_norm, a_k_norm, a_sinks, b_mu, b_w0, b_w2, b_a0, b_a2, b_g2,
           b_k_k, b_k_a, b_r_k, b_ln_w, b_ln_b, c_ln_w, c_ln_b, d_q_norm, d_k_norm, d_rel_bias,
           ffn_w1, ffn_w3, ffn_w2, moe_router, moe_w1, moe_w3, moe_w2):
    bp, tp, _ = x_prompt.shape
    bs, ts, _ = x_sample.shape
    assert ts == CHUNK
    n_p, n_s = bp * tp, bs * ts
    geom = (bp, tp, bs, ts)
    xa = x_prompt.reshape(n_p, D_MODEL)
    xb = x_sample.reshape(n_s, D_MODEL)

    tabs = {
        "ret": _retention_tables(),
        "rope_prompt": _rope_tables(jnp.arange(tp)),
        "rope_sample": _rope_tables(PAST_LEN + jnp.arange(ts)),
    }
    tile = lambda g: _row(jnp.tile(g, MIX_W // HEAD_DIM))

    p_states, s_states = [], []
    for l in range(DEPTH):
        lp = {
            "sink_col": jnp.repeat(a_sinks[l].astype(F32), CHUNK).reshape(N_HEADS * CHUNK, 1),
            "bias_table": _relbias_table(d_rel_bias[l].astype(F32), (D_PREV_CHUNKS + 1) * CHUNK),
            "rwkv": (_row(b_mu[l]), _row(b_w0[l]), b_w2[l], _row(b_a0[l]), b_a2[l], b_g2[l], _row(b_k_k[l]),
                     _row(b_k_a[l]), _row(b_r_k[l]), _row(b_ln_w[l]), _row(b_ln_b[l])),
            "c_ln_w": _row(c_ln_w[l]), "c_ln_b": _row(c_ln_b[l]),
        }
        proj = _inproj(xa, xb, n_p + n_s, _row(norm1_g[l]), w_in[l].astype(BF16), tile(a_q_norm[l]),
                       _row(jnp.tile(a_k_norm[l], A_KV_W // HEAD_DIM)), tile(d_q_norm[l]), tile(d_k_norm[l]))
        _, ak, av, hb, _, _, dk, dv = proj
        caches = (cache_a_k[l], cache_a_v[l], state_b_shift[l], state_b_wkv[l], state_c[l], cache_d_k[l], cache_d_v[l])
        mix, (wkv_p, ret_p), (wkv_s, ret_s) = _mixers(proj, caches, lp, tabs, geom)
        j = l // 2
        dense = l % 2 == 0
        x1, xn2, *packed = _outproj(xa, xb, n_p + n_s, mix, w_out[l].astype(BF16), _row(norm2_g[l]), pack=not dense)
        if dense:
            x = _ffn(xn2, x1, ffn_w1[j].astype(BF16), ffn_w3[j].astype(BF16), ffn_w2[j].astype(BF16))
        else:
            router_pad = jnp.pad(moe_router[j].astype(F32), ((0, 0), (0, ROUTER_LANES - N_EXPERTS)))
            x = _moe(xn2, *packed, x1, router_pad,
                     moe_w1[j].astype(BF16), moe_w3[j].astype(BF16), moe_w2[j].astype(BF16))
        xa = xb = x

        wa = min(A_PREV_CHUNKS * CHUNK, tp)
        wd = min(D_PREV_CHUNKS * CHUNK, tp)
        tail = lambda a, w, heads: jnp.stack(
            [a[(b + 1) * tp - w:(b + 1) * tp] for b in range(bp)]).reshape(bp, w, heads, HEAD_DIM)
        p_states.append((tail(ak, wa, 2), tail(av, wa, 2), hb[tp - 1:n_p:tp], wkv_p, ret_p,
                         tail(dk, wd, N_HEADS), tail(dv, wd, N_HEADS)))
        roll_in = lambda cache, new, heads: jnp.concatenate(
            [cache.astype(F32), new[n_p:].reshape(bs, ts, heads, HEAD_DIM)], axis=1)[:, -cache.shape[1]:]
        s_states.append((roll_in(cache_a_k[l], ak, 2), roll_in(cache_a_v[l], av, 2),
                         hb[n_p + ts - 1::ts], wkv_s, ret_s,
                         roll_in(cache_d_k[l], dk, N_HEADS), roll_in(cache_d_v[l], dv, N_HEADS)))

    yp = x[:n_p].reshape(bp, tp, D_MODEL)
    ys = x[n_p:].reshape(bs, ts, D_MODEL)
    st = lambda group, i: jnp.stack([g[i] for g in group], axis=0)
    return (yp, ys,
            st(p_states, 0), st(p_states, 1), st(p_states, 2), st(p_states, 3), st(p_states, 4), st(p_states, 5), st(p_states, 6),
            st(s_states, 0), st(s_states, 1), st(s_states, 2), st(s_states, 3), st(s_states, 4), st(s_states, 5), st(s_states, 6))
```

```python
import functools

import jax
import jax.numpy as jnp
import numpy as np
from jax import lax
from jax.experimental import pallas as pl
from jax.experimental.pallas import tpu as pltpu
from jax.experimental.pallas import tpu_sc as plsc

F32 = jnp.float32
BF16 = jnp.bfloat16

D_MODEL = 1024
DEPTH = 2
PAST_LEN = 4096
CHUNK = 64
HEAD_DIM = 64
N_HEADS = 4
MIX_W = N_HEADS * HEAD_DIM
A_KV_W = 128
A_PREV_CHUNKS = 2
D_PREV_CHUNKS = 8
D_REL_CLIP = 128
B_PROJ = 1024
C_PROJ = 1024
IN_PROJ = 3328
B_GN_EPS = 64e-5
C_GN_EPS = 1e-6
NORM_EPS = 1e-6
ATTN_SCALE = 0.125
ROPE_BASE = 10000.0
D_FF = 2816
N_EXPERTS = 8
E_FF = 3584
NEG_BIG = -1e30

VMEM_LIMIT = 48 * 1024 * 1024

NN = ((1,), (0,))
NT = ((1,), (1,))
TN = ((0,), (0,))


def _dg(a, b, dims=NN):
    return lax.dot_general(a, b, (dims, ((), ())), preferred_element_type=F32)


def _parts(x, n):
    out = []
    r = x
    for i in range(n):
        p = r.astype(BF16)
        out.append(p)
        if i + 1 < n:
            r = r - p.astype(F32)
    return out


def _mm(a, b, dims=NN, passes=1):
    if passes == 1:
        return _dg(a.astype(BF16), b.astype(BF16), dims)
    ah, al = _parts(a, 2)
    bh, bl = _parts(b, 2)
    return _dg(ah, bh, dims) + (_dg(ah, bl, dims) + _dg(al, bh, dims))


def _mm_exact_rhs(a, b_bf, dims=NN, n=2):
    acc = None
    for p in _parts(a, n):
        t = _dg(p, b_bf, dims)
        acc = t if acc is None else acc + t
    return acc


def _iota2(shape, dim):
    return lax.broadcasted_iota(jnp.int32, shape, dim)


def _head_mask(rows, cols=MIX_W):
    return (_iota2((rows, cols), 0) >> 6) == (_iota2((rows, cols), 1) >> 6)


def _seg_matrix(width, value):
    m = _head_mask(width, width)
    return jnp.where(m, value, 0.0).astype(BF16)


def _tile4(z):
    return jnp.concatenate([z, z, z, z], axis=0)


def _fold4(z):
    return (z[0:64] + z[64:128]) + (z[128:192] + z[192:256])


def _sigmoid(x):
    return 1.0 / (1.0 + jnp.exp(-x))


def _cparams(sem):
    return pltpu.CompilerParams(dimension_semantics=sem, vmem_limit_bytes=VMEM_LIMIT)


IN_TM = 512


def _two_source_specs(xa, xb, n):
    na, nb = xa.shape[0] // IN_TM, xb.shape[0] // IN_TM
    spec_a = pl.BlockSpec((IN_TM, D_MODEL), lambda i: (jnp.minimum(i, na - 1), 0))
    spec_b = pl.BlockSpec((IN_TM, D_MODEL), lambda i: (jnp.clip(i - na, 0, nb - 1), 0))
    return na, n // IN_TM, spec_a, spec_b


def _inproj_kernel(xa_ref, xb_ref, g_ref, w_ref, aqg_ref, akg_ref, dqg_ref, dkg_ref,
                   aq_ref, ak_ref, av_ref, hb_ref, hc_ref, dq_ref, dk_ref, dv_ref, *, n_first):
    x = jnp.where(pl.program_id(0) < n_first, xa_ref[...], xb_ref[...])
    ms = jnp.mean(x * x, axis=-1, keepdims=True)
    xn = ((x * lax.rsqrt(ms + NORM_EPS)) * g_ref[...]).astype(BF16)
    seg = _seg_matrix(MIX_W, 1.0 / HEAD_DIM)

    def proj(lo, hi):
        return jnp.dot(xn, w_ref[:, lo:hi], preferred_element_type=F32)

    def head_rms(h, gain_ref):
        w = h.shape[-1]
        msq = _mm_exact_rhs(h * h, seg[:w, :w], n=2)
        return (h * lax.rsqrt(msq + NORM_EPS)) * gain_ref[...]

    aq_ref[...] = head_rms(proj(0, 256), aqg_ref)
    ak_ref[...] = head_rms(proj(256, 384), akg_ref)
    av_ref[...] = proj(384, 512)
    hb_ref[...] = proj(512, 1536)
    hc_ref[...] = proj(1536, 2560)
    dq_ref[...] = head_rms(proj(2560, 2816), dqg_ref)
    dk_ref[...] = head_rms(proj(2816, 3072), dkg_ref)
    dv_ref[...] = proj(3072, 3328)


def _inproj(xa, xb, n, g, w_bf, aqg, akg, dqg, dkg):
    na, nblk, spec_a, spec_b = _two_source_specs(xa, xb, n)
    widths = (256, 128, 128, B_PROJ, C_PROJ, 256, 256, 256)
    row = lambda w: pl.BlockSpec((IN_TM, w), lambda i: (i, 0))
    full = lambda a: pl.BlockSpec(a.shape, lambda i: (0,) * a.ndim)
    return pl.pallas_call(
        functools.partial(_inproj_kernel, n_first=na),
        grid=(nblk,),
        in_specs=[spec_a, spec_b, full(g), full(w_bf), full(aqg), full(akg), full(dqg), full(dkg)],
        out_specs=[row(w) for w in widths],
        out_shape=[jax.ShapeDtypeStruct((n, w), F32) for w in widths],
        compiler_params=_cparams(("parallel",)),
        name="inproj",
    )(xa, xb, g, w_bf, aqg, akg, dqg, dkg)


PACK_W = 256


def _pack_bf16_pairs(hi, lo):
    bits = lambda z: pltpu.bitcast(z.astype(BF16).astype(F32), jnp.int32)
    return bits(hi) | lax.shift_right_logical(bits(lo), jnp.full(lo.shape, 16, jnp.int32))


def _unpack_bf16_pairs(w):
    hi = pltpu.bitcast(w & jnp.int32(-65536), F32)
    lo = pltpu.bitcast(lax.shift_left(w, jnp.full(w.shape, 16, jnp.int32)), F32)
    return hi, lo


def _pack_rows(x):
    return (_pack_bf16_pairs(x[:, 0:PACK_W], x[:, PACK_W:2 * PACK_W]),
            _pack_bf16_pairs(x[:, 2 * PACK_W:3 * PACK_W], x[:, 3 * PACK_W:4 * PACK_W]))


def _unpack_rows(wa, wb):
    return jnp.concatenate(_unpack_bf16_pairs(wa) + _unpack_bf16_pairs(wb), axis=1)


def _outproj_kernel(xa_ref, xb_ref, *refs, n_first_x, n_first_mix, pack):
    mix_refs, (w_ref, g_ref), outs = refs[:8], refs[8:10], refs[10:]
    i = pl.program_id(0)
    acc = jnp.where(i < n_first_x, xa_ref[...], xb_ref[...])
    for m in range(4):
        o = jnp.where(i < n_first_mix, mix_refs[2 * m][...], mix_refs[2 * m + 1][...])
        acc = acc + jnp.dot(o.astype(BF16), w_ref[m * MIX_W:(m + 1) * MIX_W, :], preferred_element_type=F32)
    outs[0][...] = acc
    ms = jnp.mean(acc * acc, axis=-1, keepdims=True)
    xn = (acc * lax.rsqrt(ms + NORM_EPS)) * g_ref[...]
    outs[1][...] = xn.astype(BF16)
    if pack:
        outs[2][...], outs[3][...] = _pack_rows(xn)


def _outproj(xa, xb, n, mix, w_bf, g2, *, pack):
    na, nblk, spec_a, spec_b = _two_source_specs(xa, xb, n)
    nm_p, nm_s = mix[0][0].shape[0] // IN_TM, mix[0][1].shape[0] // IN_TM
    row = lambda w: pl.BlockSpec((IN_TM, w), lambda i: (i, 0))
    full = lambda a: pl.BlockSpec(a.shape, lambda i: (0,) * a.ndim)
    mix_specs = [pl.BlockSpec((IN_TM, MIX_W), lambda i: (jnp.minimum(i, nm_p - 1), 0)),
                 pl.BlockSpec((IN_TM, MIX_W), lambda i: (jnp.clip(i - nm_p, 0, nm_s - 1), 0))] * 4
    out_specs = [row(D_MODEL), row(D_MODEL)]
    out_shape = [jax.ShapeDtypeStruct((n, D_MODEL), F32), jax.ShapeDtypeStruct((n, D_MODEL), BF16)]
    if pack:
        out_specs += [row(PACK_W), row(PACK_W)]
        out_shape += [jax.ShapeDtypeStruct((n, PACK_W), jnp.int32)] * 2
    return pl.pallas_call(
        functools.partial(_outproj_kernel, n_first_x=na, n_first_mix=nm_p, pack=pack),
        grid=(nblk,),
        in_specs=[spec_a, spec_b] + mix_specs + [full(w_bf), full(g2)],
        out_specs=out_specs,
        out_shape=out_shape,
        compiler_params=_cparams(("parallel",)),
        name="outproj",
    )(xa, xb, *[a for pair in mix for a in pair], w_bf, g2)


FFN_TM = 512
FFN_TF = 1408


def _ffn_kernel(xn_ref, x1_ref, w1_ref, w3_ref, w2_ref, o_ref):
    f = pl.program_id(1)
    xn = xn_ref[...]
    a = jnp.dot(xn, w1_ref[...], preferred_element_type=F32)
    b = jnp.dot(xn, w3_ref[...], preferred_element_type=F32)
    h = ((a * _sigmoid(a)) * b).astype(BF16)
    y = jnp.dot(h, w2_ref[...], preferred_element_type=F32)

    @pl.when(f == 0)
    def _():
        o_ref[...] = x1_ref[...] + y

    @pl.when(f != 0)
    def _():
        o_ref[...] += y


def _ffn(xn, x1, w1_bf, w3_bf, w2_bf):
    n = xn.shape[0]
    return pl.pallas_call(
        _ffn_kernel,
        grid=(n // FFN_TM, D_FF // FFN_TF),
        in_specs=[
            pl.BlockSpec((FFN_TM, D_MODEL), lambda i, f: (i, 0)),
            pl.BlockSpec((FFN_TM, D_MODEL), lambda i, f: (i, 0)),
            pl.BlockSpec((D_MODEL, FFN_TF), lambda i, f: (0, f)),
            pl.BlockSpec((D_MODEL, FFN_TF), lambda i, f: (0, f)),
            pl.BlockSpec((FFN_TF, D_MODEL), lambda i, f: (f, 0)),
        ],
        out_specs=pl.BlockSpec((FFN_TM, D_MODEL), lambda i, f: (i, 0)),
        out_shape=jax.ShapeDtypeStruct((n, D_MODEL), F32),
        compiler_params=_cparams(("parallel", "arbitrary")),
        name="ffn",
    )(xn, x1, w1_bf, w3_bf, w2_bf)


ROUTER_LANES = 128


def _router_kernel(xn_ref, r_ref, gate_ref, rank_ref, cnt_ref, cnt_scr):
    @pl.when(pl.program_id(0) == 0)
    def _():
        cnt_scr[...] = jnp.zeros_like(cnt_scr)

    xn = xn_ref[...].astype(F32)
    logits = _mm(xn, r_ref[...], passes=3)
    lane = _iota2(logits.shape, 1)
    logits = jnp.where(lane < N_EXPERTS, logits, NEG_BIG)
    m1 = jnp.max(logits, axis=-1, keepdims=True)
    i1 = jnp.min(jnp.where(logits == m1, lane, ROUTER_LANES), axis=-1, keepdims=True)
    rest = jnp.where(lane == i1, NEG_BIG, logits)
    m2 = jnp.max(rest, axis=-1, keepdims=True)
    i2 = jnp.min(jnp.where(rest == m2, lane, ROUTER_LANES), axis=-1, keepdims=True)
    e2 = jnp.exp(m2 - m1)
    den = 1.0 + e2
    gates = jnp.where(lane == i1, 1.0 / den, 0.0) + jnp.where(lane == i2, e2 / den, 0.0)
    gate_ref[...] = gates
    sel = jnp.where(gates > 0.0, 1.0, 0.0)
    tm = sel.shape[0]
    before = jnp.where(_iota2((tm, tm), 1) < _iota2((tm, tm), 0), 1.0, 0.0).astype(BF16)
    rank_ref[...] = (_dg(before, sel.astype(BF16)) + cnt_scr[...]).astype(jnp.int32)
    cnt_scr[...] = cnt_scr[...] + jnp.sum(sel, axis=0, keepdims=True)
    cnt_ref[...] = cnt_scr[...].astype(jnp.int32)


def _router(xn, router_pad):
    n = xn.shape[0]
    row = pl.BlockSpec((IN_TM, ROUTER_LANES), lambda i: (i, 0))
    return pl.pallas_call(
        _router_kernel,
        grid=(n // IN_TM,),
        in_specs=[pl.BlockSpec((IN_TM, D_MODEL), lambda i: (i, 0)),
                  pl.BlockSpec(router_pad.shape, lambda i: (0, 0))],
        out_specs=[row, row, pl.BlockSpec((1, ROUTER_LANES), lambda i: (0, 0))],
        out_shape=[jax.ShapeDtypeStruct((n, ROUTER_LANES), F32),
                   jax.ShapeDtypeStruct((n, ROUTER_LANES), jnp.int32),
                   jax.ShapeDtypeStruct((1, ROUTER_LANES), jnp.int32)],
        scratch_shapes=[pltpu.VMEM((1, ROUTER_LANES), F32)],
        compiler_params=_cparams(("arbitrary",)),
        name="router",
    )(xn, router_pad)


MOE_R = 512
MOE_TF = 1792
SC_WINDOW = 128


def _moe_plan(gates, rank, counts, n):
    n_blocks = (2 * n) // MOE_R + N_EXPERTS + 1
    spare_row = (n_blocks - 1) * MOE_R
    sel = gates[:, :N_EXPERTS] > 0.0
    rank = rank[:, :N_EXPERTS]
    counts = counts[0, :N_EXPERTS]
    padded = ((counts + MOE_R - 1) // MOE_R) * MOE_R
    pad_end = jnp.cumsum(padded)
    pad_start = pad_end - padded
    pos = jnp.where(sel, pad_start[None, :] + rank, -1)
    order = jnp.cumsum(sel.astype(jnp.int32), axis=1)
    pick = lambda j: jnp.max(jnp.where(jnp.logical_and(sel, order == j), pos, -1), axis=1)
    to_row = lambda p: jnp.where(p >= 0, p, spare_row).astype(jnp.int32).reshape(1, n)
    block_expert = jnp.minimum(
        jnp.sum(pad_end[None, :] <= (jnp.arange(n_blocks) * MOE_R)[:, None], axis=1), N_EXPERTS - 1)
    return dict(n_blocks=n_blocks, pos0=to_row(pick(1)), pos1=to_row(pick(2)),
                block_expert=block_expert.astype(jnp.int32), n_used=(pad_end[-1:] // MOE_R).astype(jnp.int32))


def _sc_mesh():
    return plsc.VectorSubcoreMesh(core_axis_name="core", subcore_axis_name="subcore")


def _sc_scatter_rows(table, idx_lists, n_rows):
    n, cols = table.shape
    k = len(idx_lists)

    @functools.partial(pl.kernel, out_type=jax.ShapeDtypeStruct((n_rows, cols), table.dtype), mesh=_sc_mesh())
    def scatter(x_hbm, *rest):
        i_hbms, o_hbm = rest[:k], rest[k]

        def body(x_vmem, *i_vmems):
            for i_vmem in i_vmems:
                pltpu.sync_copy(x_vmem, o_hbm.at[i_vmem.at[0]])

        pltpu.emit_pipeline(
            body,
            grid=(n // SC_WINDOW,),
            in_specs=[pl.BlockSpec((SC_WINDOW, cols), lambda i: (i, 0))]
            + [pl.BlockSpec((1, SC_WINDOW), lambda i: (0, i))] * k,
            out_specs=[],
            core_axis_name=("core", "subcore"),
            dimension_semantics=(pltpu.PARALLEL,),
        )(x_hbm, *i_hbms)

    return scatter(table, *idx_lists)


def _sc_gather_rows(table, idx):
    n = idx.shape[1]
    cols = table.shape[1]

    @functools.partial(pl.kernel, out_type=jax.ShapeDtypeStruct((n, cols), table.dtype), mesh=_sc_mesh())
    def gather(x_hbm, i_hbm, o_hbm):
        def body(i_vmem, o_vmem):
            pltpu.sync_copy(x_hbm.at[i_vmem.at[0]], o_vmem)

        pltpu.emit_pipeline(
            body,
            grid=(n // SC_WINDOW,),
            in_specs=[pl.BlockSpec((1, SC_WINDOW), lambda i: (0, i))],
            out_specs=[pl.BlockSpec((SC_WINDOW, cols), lambda i: (i, 0))],
            core_axis_name=("core", "subcore"),
            dimension_semantics=(pltpu.PARALLEL,),
        )(i_hbm, o_hbm)

    return gather(table, idx)


def _moe_expert_kernel(be_ref, nu_ref, xa_ref, xb_ref, gs_ref, w1_ref, w3_ref, w2_ref, oa_ref, ob_ref, acc_ref):
    j, f = pl.program_id(0), pl.program_id(1)
    used = j < nu_ref[0]

    @pl.when(used)
    def _():
        x = _unpack_rows(xa_ref[...], xb_ref[...]).astype(BF16)
        a = jnp.dot(x, w1_ref[...], preferred_element_type=F32)
        b = jnp.dot(x, w3_ref[...], preferred_element_type=F32)
        h = ((a * _sigmoid(a)) * b).astype(BF16)
        y = jnp.dot(h, w2_ref[...], preferred_element_type=F32)

        @pl.when(f == 0)
        def _():
            acc_ref[...] = y

        @pl.when(f != 0)
        def _():
            acc_ref[...] += y

    @pl.when(f == pl.num_programs(1) - 1)
    def _():
        lane = _iota2(gs_ref.shape, 1)
        g = jnp.sum(jnp.where(lane == be_ref[j], gs_ref[...], 0.0), axis=1, keepdims=True)
        oa_ref[...], ob_ref[...] = _pack_rows(jnp.where(used, acc_ref[...] * g, 0.0))


def _moe_experts(plan, xs_a, xs_b, gs, w1_bf, w3_bf, w2_bf):
    n_blocks = plan["n_blocks"]
    half = pl.BlockSpec((MOE_R, PACK_W), lambda j, f, be, nu: (j, 0))
    grid_spec = pltpu.PrefetchScalarGridSpec(
        num_scalar_prefetch=2,
        grid=(n_blocks, E_FF // MOE_TF),
        in_specs=[half, half,
                  pl.BlockSpec((MOE_R, ROUTER_LANES), lambda j, f, be, nu: (j, 0)),
                  pl.BlockSpec((None, D_MODEL, MOE_TF), lambda j, f, be, nu: (be[j], 0, f)),
                  pl.BlockSpec((None, D_MODEL, MOE_TF), lambda j, f, be, nu: (be[j], 0, f)),
                  pl.BlockSpec((None, MOE_TF, D_MODEL), lambda j, f, be, nu: (be[j], f, 0))],
        out_specs=[half, half],
        scratch_shapes=[pltpu.VMEM((MOE_R, D_MODEL), F32)])
    return pl.pallas_call(
        _moe_expert_kernel,
        grid_spec=grid_spec,
        out_shape=[jax.ShapeDtypeStruct((n_blocks * MOE_R, PACK_W), jnp.int32)] * 2,
        compiler_params=_cparams(("arbitrary", "arbitrary")),
        name="moe_experts",
    )(plan["block_expert"], plan["n_used"], xs_a, xs_b, gs, w1_bf, w3_bf, w2_bf)


def _moe_combine_kernel(x1_ref, a0_ref, b0_ref, a1_ref, b1_ref, o_ref):
    o_ref[...] = (x1_ref[...] + _unpack_rows(a0_ref[...], b0_ref[...])) + _unpack_rows(a1_ref[...], b1_ref[...])


def _moe_combine(x1, picked, row0, n_rows):
    base = row0 // IN_TM
    row = lambda w: pl.BlockSpec((IN_TM, w), lambda i: (base + i, 0))
    return pl.pallas_call(
        _moe_combine_kernel,
        grid=(n_rows // IN_TM,),
        in_specs=[row(D_MODEL)] + [row(PACK_W)] * 4,
        out_specs=pl.BlockSpec((IN_TM, D_MODEL), lambda i: (i, 0)),
        out_shape=jax.ShapeDtypeStruct((n_rows, D_MODEL), F32),
        compiler_params=_cparams(("parallel",)),
        name="moe_combine",
    )(x1, *picked)


def _moe(xn, xn_a, xn_b, x1, router_pad, w1_bf, w3_bf, w2_bf, row_groups):
    gates, rank, counts = _router(xn, router_pad)
    plan = _moe_plan(gates, rank, counts, xn.shape[0])
    n_rows = plan["n_blocks"] * MOE_R
    idx = (plan["pos0"], plan["pos1"])
    xs_a = _sc_scatter_rows(xn_a, idx, n_rows)
    xs_b = _sc_scatter_rows(xn_b, idx, n_rows)
    gs = _sc_scatter_rows(gates, idx, n_rows)
    os_a, os_b = _moe_experts(plan, xs_a, xs_b, gs, w1_bf, w3_bf, w2_bf)
    picked = [_sc_gather_rows(t, p) for p in idx for t in (os_a, os_b)]
    return [_moe_combine(x1, picked, row0, n_rows) for row0, n_rows in row_groups]


def _relbias_kernel(rb_ref, o_ref, *, nk):
    h = pl.program_id(0)
    q = _iota2((CHUNK, nk), 0)
    r = _iota2((CHUNK, nk), 1)
    idx = jnp.clip(q - (r - (nk - CHUNK)), -D_REL_CLIP, D_REL_CLIP) + D_REL_CLIP

    def body(j, acc):
        return jnp.where(idx == j, rb_ref[h, j], acc)

    o_ref[...] = lax.fori_loop(0, 2 * D_REL_CLIP + 1, body, jnp.zeros((CHUNK, nk), F32))


def _relbias_table(rel_bias, nk):
    return pl.pallas_call(
        functools.partial(_relbias_kernel, nk=nk),
        grid=(N_HEADS,),
        in_specs=[pl.BlockSpec(memory_space=pltpu.SMEM)],
        out_specs=pl.BlockSpec((CHUNK, nk), lambda h: (h, 0)),
        out_shape=jax.ShapeDtypeStruct((N_HEADS * CHUNK, nk), F32),
        name="relbias",
    )(rel_bias)


ATTN_QB = 8
ATTN_GROUP = 8


def _attn_kernel(q_ref, kp_ref, kc_ref, vp_ref, vc_ref, x_ref, *rest, qb, n_prev, use_sink, mask_first):
    o_ref, kbuf, vbuf = rest[-3:]
    i = pl.program_id(1)
    p_rows = kp_ref.shape[0]
    nk = (n_prev + 1) * CHUNK
    wk = kp_ref.shape[1]

    kbuf[0:p_rows, :] = kp_ref[...].astype(BF16)
    kbuf[p_rows:, :] = kc_ref[...].astype(BF16)
    vbuf[0:p_rows, :] = vp_ref[...].astype(BF16)
    vbuf[p_rows:, :] = vc_ref[...].astype(BF16)

    hmask = _head_mask(N_HEADS * CHUNK)
    extra = x_ref[...]
    grouped = wk != MIX_W
    low = _iota2((CHUNK, A_KV_W), 1) < HEAD_DIM

    def stack_queries(qj):
        if not grouped:
            return jnp.where(hmask, _tile4(qj), 0.0)
        shifted = pltpu.roll(qj, MIX_W - HEAD_DIM, 1)[:, :A_KV_W]
        return jnp.concatenate([jnp.where(low, qj[:, :A_KV_W], 0.0), jnp.where(low, shifted, 0.0),
                                jnp.where(low, 0.0, shifted), jnp.where(low, 0.0, qj[:, A_KV_W:])], axis=0)

    def unstack_outputs(o_all):
        if not grouped:
            return _fold4(jnp.where(hmask, o_all, 0.0))
        b0, b1, b2, b3 = (o_all[h * CHUNK:(h + 1) * CHUNK] for h in range(N_HEADS))
        left = jnp.where(low, b0, 0.0) + pltpu.roll(jnp.where(low, b1, 0.0), HEAD_DIM, 1)
        right = pltpu.roll(jnp.where(low, 0.0, b2), HEAD_DIM, 1) + jnp.where(low, 0.0, b3)
        return jnp.concatenate([left, right], axis=1)

    def scores(j):
        base = p_rows + (j - n_prev) * CHUNK
        qs = stack_queries(q_ref[pl.ds(j * CHUNK, CHUNK), :] * ATTN_SCALE).astype(BF16)
        s = _dg(qs, kbuf[pl.ds(base, nk), :], NT)
        if not use_sink:
            s = s + extra
        if mask_first and base < p_rows:
            krow = base + _iota2(s.shape, 1)
            s = jnp.where(jnp.logical_and(i == 0, krow < p_rows), NEG_BIG, s)
        return s

    def weights(s):
        m = jnp.max(s, axis=-1, keepdims=True)
        if use_sink:
            m = jnp.maximum(m, extra)
        e = jnp.exp(s - m)
        den = jnp.sum(e, axis=-1, keepdims=True)
        if use_sink:
            den = den + jnp.exp(extra - m)
        return e.astype(BF16), 1.0 / den

    def output(j, e, inv_den):
        base = p_rows + (j - n_prev) * CHUNK
        o_all = _dg(e, vbuf[pl.ds(base, nk), :]) * inv_den
        o_ref[pl.ds(j * CHUNK, CHUNK), :] = unstack_outputs(o_all)

    for j0 in range(0, qb, ATTN_GROUP):
        group = range(j0, min(j0 + ATTN_GROUP, qb))
        ss = [scores(j) for j in group]
        ws = [weights(s) for s in ss]
        for j, (e, inv_den) in zip(group, ws):
            output(j, e, inv_den)


def _attention(q, k, v, prev, extra, *, n_prev, use_sink, n_streams, t, base_row):
    wk = k.shape[-1]
    if prev is None:
        qb = ATTN_QB
        rows = qb * CHUNK
        nblk = t // rows
        base = base_row // rows
        prev_spec = pl.BlockSpec((rows, wk), lambda s, i: (base + s * nblk + jnp.maximum(i - 1, 0), 0))
        k_prev, v_prev, p_rows, mask_first = k, v, rows, True
    else:
        qb, rows, nblk = t // CHUNK, t, 1
        base = base_row // rows
        k_prev, v_prev = prev
        p_rows = k_prev.shape[1]
        prev_spec = pl.BlockSpec((None, p_rows, wk), lambda s, i: (s, 0, 0))
        mask_first = False
    cur = lambda w: pl.BlockSpec((rows, w), lambda s, i: (base + s * nblk + i, 0))
    kern = functools.partial(_attn_kernel, qb=qb, n_prev=n_prev, use_sink=use_sink, mask_first=mask_first)
    return pl.pallas_call(
        kern,
        grid=(n_streams, nblk),
        in_specs=[cur(MIX_W), prev_spec, cur(wk), prev_spec, cur(wk),
                  pl.BlockSpec(extra.shape, lambda s, i: (0, 0))],
        out_specs=pl.BlockSpec((rows, MIX_W), lambda s, i: (s * nblk + i, 0)),
        out_shape=jax.ShapeDtypeStruct((n_streams * t, MIX_W), F32),
        scratch_shapes=[pltpu.VMEM((p_rows + rows, wk), BF16), pltpu.VMEM((p_rows + rows, wk), BF16)],
        compiler_params=_cparams(("parallel", "arbitrary")),
        name="attn_sink" if use_sink else "attn_bias",
    )(q, k_prev, k, v_prev, v, extra)


def _head_layer_norm(o, seg_mean_bf, w, b, eps):
    mu = _mm_exact_rhs(o, seg_mean_bf)
    d = o - mu
    var = _mm_exact_rhs(d * d, seg_mean_bf)
    return (d * lax.rsqrt(var + eps)) * w + b


def _ret_kernel(hc_ref, cos_ref, sin_ref, s0_ref, dstack_ref, qsc_ref, ksc_ref, gam_ref, lnw_ref, lnb_ref,
                *rest, qb):
    o_ref, sout_ref, s_scr = rest[-3:]
    i = pl.program_id(1)

    @pl.when(i == 0)
    def _():
        s_scr[...] = s0_ref[...]

    hmask = _head_mask(N_HEADS * CHUNK)
    seg_mean = _seg_matrix(MIX_W, 1.0 / HEAD_DIM)
    rows = qb * CHUNK
    first_half = (_iota2((rows, MIX_W), 1) & (HEAD_DIM - 1)) < (HEAD_DIM // 2)
    cos = jnp.concatenate([cos_ref[...]] * (MIX_W // ROPE_W), axis=1)
    sin = jnp.concatenate([sin_ref[...]] * (MIX_W // ROPE_W), axis=1)

    def rope(x):
        partner = jnp.where(first_half, pltpu.roll(x, MIX_W - HEAD_DIM // 2, 1), pltpu.roll(x, HEAD_DIM // 2, 1))
        return x * cos + partner * sin

    q = rope(hc_ref[:, 0:256])
    k = rope(hc_ref[:, 256:512]) * ATTN_SCALE
    v_bf = hc_ref[:, 512:768].astype(BF16)
    state = s_scr[...]
    outs = []
    for j in range(qb):
        sl = slice(j * CHUNK, (j + 1) * CHUNK)
        qj, kj, vj = q[sl], k[sl], v_bf[sl]
        qs = jnp.where(hmask, _tile4(qj), 0.0).astype(BF16)
        sc = _dg(qs, kj.astype(BF16), NT) * dstack_ref[...]
        intra = _fold4(jnp.where(hmask, _dg(sc.astype(BF16), vj), 0.0))
        inter = _dg((qj * qsc_ref[...]).astype(BF16), state.astype(BF16))
        kv = _dg((kj * ksc_ref[...]).astype(BF16), vj, TN)
        state = gam_ref[...] * state + jnp.where(hmask, kv, 0.0)
        outs.append(intra + inter)
    s_scr[...] = state
    sout_ref[...] = state
    y = _head_layer_norm(jnp.concatenate(outs, axis=0), seg_mean, lnw_ref[...], lnb_ref[...], C_GN_EPS)
    g = hc_ref[:, 768:1024]
    o_ref[...] = y * (g * _sigmoid(g))


def _retention(hc, cos, sin, s0_bd, tabs, lnw, lnb, *, n_streams, t, base_row):
    qb = min(8, t // CHUNK)
    rows = qb * CHUNK
    nblk = t // rows
    base = base_row // rows
    dstack, qsc, ksc, gam = tabs
    full = lambda a: pl.BlockSpec(a.shape, lambda s, i: (0,) * a.ndim)
    cur = lambda w: pl.BlockSpec((rows, w), lambda s, i: (base + s * nblk + i, 0))
    state = pl.BlockSpec((None, MIX_W, MIX_W), lambda s, i: (s, 0, 0))
    return pl.pallas_call(
        functools.partial(_ret_kernel, qb=qb),
        grid=(n_streams, nblk),
        in_specs=[cur(C_PROJ),
                  pl.BlockSpec((rows, ROPE_W), lambda s, i: (i, 0)),
                  pl.BlockSpec((rows, ROPE_W), lambda s, i: (i, 0)),
                  state, full(dstack), full(qsc), full(ksc), full(gam), full(lnw), full(lnb)],
        out_specs=[pl.BlockSpec((rows, MIX_W), lambda s, i: (s * nblk + i, 0)), state],
        out_shape=[jax.ShapeDtypeStruct((n_streams * t, MIX_W), F32),
                   jax.ShapeDtypeStruct((n_streams, MIX_W, MIX_W), F32)],
        scratch_shapes=[pltpu.VMEM((MIX_W, MIX_W), F32)],
        compiler_params=_cparams(("parallel", "arbitrary")),
        name="retention",
    )(hc, cos, sin, s0_bd, dstack, qsc, ksc, gam, lnw, lnb)


def _retention_tables():
    gamma = 1.0 - 2.0 ** (-5.0 - np.arange(N_HEADS, dtype=np.float64))
    t = np.arange(CHUNK)
    diff = t[:, None] - t[None, :]
    dmat = np.where(diff >= 0, gamma[:, None, None] ** np.maximum(diff, 0), 0.0)
    dstack = dmat.reshape(N_HEADS * CHUNK, CHUNK)
    lanes = lambda per_head: np.repeat(per_head, HEAD_DIM, axis=-1)
    qsc = lanes(gamma[None, :] ** (t + 1)[:, None])
    ksc = lanes(gamma[None, :] ** (CHUNK - 1 - t)[:, None])
    gam = np.broadcast_to(lanes(gamma ** CHUNK)[:, None], (MIX_W, MIX_W))
    return tuple(jnp.asarray(a, F32) for a in (dstack, qsc, ksc, gam))


ROPE_W = 2 * HEAD_DIM


def _rope_tables(pos):
    half = HEAD_DIM // 2
    theta = np.float32(1.0) / (np.float32(ROPE_BASE) ** np.linspace(0.0, 1.0, half, dtype=np.float32))
    ang = np.asarray(pos, np.float32)[:, None] * theta[None, :]
    cos, sin = np.cos(ang), np.sin(ang)
    reps = ROPE_W // HEAD_DIM
    cos_t = np.tile(np.concatenate([cos, cos], axis=-1), (1, reps))
    sin_t = np.tile(np.concatenate([-sin, sin], axis=-1), (1, reps))
    return jnp.asarray(cos_t, F32), jnp.asarray(sin_t, F32)


DECAY_SCALE = 0.6065306597126334
RWKV_CB = 4
N_LEVELS = 6
MASK_HEAD, MASK_STRICT, MASK_INCL, MASK_LEVEL0 = 0, 1, 2, 3


def _rwkv_masks():
    n4 = N_HEADS * CHUNK
    ri = np.arange(n4)[:, None]
    ci = np.arange(n4)[None, :]
    head = (ri >> 6) == (ci >> 6)
    tabs = [head, head & ((ci & 63) < (ri & 63)), head & ((ci & 63) <= (ri & 63))]
    for log_m in range(N_LEVELS):
        same = (ri >> (log_m + 1)) == (ci >> (log_m + 1))
        tabs.append(same & (((ri >> log_m) & 1) == 1) & (((ci >> log_m) & 1) == 0))
    return jnp.asarray(np.stack(tabs), BF16)


def _rwkv_kernel(hb_ref, shift0_ref, h0_ref, masks_ref, mu_ref, w0_ref, w2_ref, a0_ref, a2_ref, g2_ref,
                 kk_ref, ka_ref, rk_ref, lnw_ref, lnb_ref, *rest, cb, independent):
    o_ref, hout_ref, h_scr, shift_scr = rest[-4:]
    c = pl.program_id(1)
    rows = cb * CHUNK
    xb = hb_ref[...]
    row = _iota2(xb.shape, 0)
    prev = pltpu.roll(xb, 1, 0)
    if independent:
        for j in range(cb):
            prev = jnp.where(row == j * CHUNK, shift0_ref[j], prev)
    else:
        @pl.when(c == 0)
        def _():
            h_scr[...] = h0_ref[0]
            shift_scr[...] = shift0_ref[0]

        prev = jnp.where(row == 0, shift_scr[...], prev)
        shift_scr[...] = xb[rows - 1:rows, :]
    xs = xb + mu_ref[...] * (prev - xb)
    r = xs[:, 0:256]
    k = xs[:, 256:512]
    v = xs[:, 512:768]
    xw = xs[:, 768:832]
    xa = xs[:, 832:896]
    xg = xs[:, 896:1024]

    z = w0_ref[...] + _mm(jnp.tanh(xw), w2_ref[...], passes=3)
    lw = -DECAY_SCALE * _sigmoid(z)
    a_gate = _sigmoid(a0_ref[...] + _mm(xa, a2_ref[...], passes=3))
    gate = _mm(_sigmoid(xg), g2_ref[...], passes=1)

    seg_sum = _seg_matrix(MIX_W, 1.0)
    seg_mean = _seg_matrix(MIX_W, 1.0 / HEAD_DIM)
    kkn = k * kk_ref[...]
    norm = jnp.sqrt(_mm_exact_rhs(kkn * kkn, seg_sum))
    kk = kkn / jnp.maximum(norm, 1e-12)
    kf = k * (1.0 + (a_gate - 1.0) * ka_ref[...])

    tt = _iota2((rows, rows), 0)
    ss = _iota2((rows, rows), 1)
    tril = jnp.where(jnp.logical_and(ss <= tt, (ss >> 6) == (tt >> 6)), 1.0, 0.0).astype(BF16)
    lw_parts = _parts(lw, 3)
    cum = _dg(tril, lw_parts[0]) + (_dg(tril, lw_parts[1]) + _dg(tril, lw_parts[2]))
    w_inv = jnp.exp(-cum)
    rho = (r * jnp.exp(cum)).astype(BF16)
    alpha = (-kk * jnp.exp(cum - lw)).astype(BF16)
    beta = ((kk * a_gate) * w_inv).astype(BF16)
    kappa = (kf * w_inv).astype(BF16)
    v_bf = v.astype(BF16)

    hmask = masks_ref[MASK_HEAD]
    n4 = N_HEADS * CHUNK
    eye = jnp.where(_iota2((n4, n4), 0) == _iota2((n4, n4), 1), 1.0, 0.0)

    pre, a_bfs, t_invs = [], [], []
    for j in range(cb):
        sl = slice(j * CHUNK, (j + 1) * CHUNK)
        bd = lambda zz: _tile4(zz[sl]) * hmask
        al_bd, be_bd, ka_bd, rh_bd, v_bd = bd(alpha), bd(beta), bd(kappa), bd(rho), bd(v_bf)
        a_bf = _dg(al_bd, be_bd, NT).astype(BF16) * masks_ref[MASK_STRICT]
        a_ak = _dg(al_bd, ka_bd, NT).astype(BF16) * masks_ref[MASK_STRICT]
        b_rb = _dg(rh_bd, be_bd, NT).astype(BF16) * masks_ref[MASK_INCL]
        b_rk = _dg(rh_bd, ka_bd, NT).astype(BF16) * masks_ref[MASK_INCL]
        x0 = _dg(a_ak, v_bd)
        y0 = _dg(b_rk, v_bd)
        sn0 = _dg(v_bd, ka_bd, TN)
        w_chunk = jnp.exp(cum[(j + 1) * CHUNK - 1:(j + 1) * CHUNK, :])
        a_bfs.append(a_bf)
        t_invs.append(eye + (a_bf * masks_ref[MASK_LEVEL0]).astype(F32))
        pre.append([al_bd, be_bd, rh_bd, b_rb, None, x0, y0, sn0, w_chunk])
    for lvl in range(1, N_LEVELS):
        t_bfs = [t.astype(BF16) for t in t_invs]
        e_mats = [_dg(a_bfs[j] * masks_ref[MASK_LEVEL0 + lvl], t_bfs[j]) for j in range(cb)]
        t_invs = [t_invs[j] + _dg(t_bfs[j], e_mats[j].astype(BF16)) for j in range(cb)]
    for j in range(cb):
        pre[j][4] = t_invs[j].astype(BF16)

    ys = []
    h = None if independent else h_scr[...]
    for j in range(cb):
        al_bd, be_bd, rh_bd, b_rb, t_bf, x0, y0, sn0, w_chunk = pre[j]
        h0 = h0_ref[j] if independent else h
        h0_bf = h0.astype(BF16)
        x_mat = _dg(al_bd, h0_bf, NT) + x0
        u_bf = _dg(t_bf, x_mat.astype(BF16)).astype(BF16)
        y_bd = _dg(rh_bd, h0_bf, NT) + _dg(b_rb, u_bf) + y0
        h_new = (h0 + _dg(u_bf, be_bd, TN) + sn0) * w_chunk
        ys.append(_fold4(y_bd))
        if independent:
            hout_ref[j] = h_new
        else:
            h = h_new
    if not independent:
        h_scr[...] = h
        hout_ref[0] = h

    y = _head_layer_norm(jnp.concatenate(ys, axis=0), seg_mean, lnw_ref[...], lnb_ref[...], B_GN_EPS)
    bonus = _mm_exact_rhs(r * kf * rk_ref[...], seg_sum) * v
    o_ref[...] = (y + bonus) * gate


def _rwkv(hb, shift0, h0_bd, params, *, independent, n_streams, t, base_row):
    cb = RWKV_CB
    rows = cb * CHUNK
    nblk = t // rows
    base = base_row // rows
    masks = _rwkv_masks()
    full = lambda a: pl.BlockSpec(a.shape, lambda s, c: (0,) * a.ndim)
    if independent:
        assert n_streams == 1
        st_map = lambda s, c: (c, 0, 0)
        n_state, st_blk = t // CHUNK, cb
    else:
        st_map = lambda s, c: (s, 0, 0)
        n_state, st_blk = n_streams, 1
    cur = lambda w: pl.BlockSpec((rows, w), lambda s, c: (base + s * nblk + c, 0))
    return pl.pallas_call(
        functools.partial(_rwkv_kernel, cb=cb, independent=independent),
        grid=(n_streams, nblk),
        in_specs=[cur(B_PROJ),
                  pl.BlockSpec((st_blk, 1, B_PROJ), st_map),
                  pl.BlockSpec((st_blk, MIX_W, MIX_W), st_map),
                  full(masks)] + [full(p) for p in params],
        out_specs=[pl.BlockSpec((rows, MIX_W), lambda s, c: (s * nblk + c, 0)),
                   pl.BlockSpec((st_blk, MIX_W, MIX_W), st_map)],
        out_shape=[jax.ShapeDtypeStruct((n_streams * t, MIX_W), F32),
                   jax.ShapeDtypeStruct((n_state, MIX_W, MIX_W), F32)],
        scratch_shapes=[pltpu.VMEM((MIX_W, MIX_W), F32), pltpu.VMEM((1, B_PROJ), F32)],
        compiler_params=_cparams(("parallel", "arbitrary")),
        name="rwkv7",
    )(hb, shift0, h0_bd, masks, *params)


def _to_block_diag(s):
    eye = jnp.eye(N_HEADS, dtype=s.dtype)
    out = s[:, :, :, None, :] * eye[None, :, None, :, None]
    return out.reshape(s.shape[0], MIX_W, MIX_W)


def _from_block_diag(m):
    b = m.reshape(m.shape[0], N_HEADS, HEAD_DIM, N_HEADS, HEAD_DIM)
    return jnp.stack([b[:, h, :, h, :] for h in range(N_HEADS)], axis=1)


def _row(p):
    return p.reshape(1, -1).astype(F32)


def _mixers(proj, caches, lp, tabs, geom):
    aq, ak, av, hb, hc, dq, dk, dv = proj
    bp, tp, bs, ts = geom
    n_p = bp * tp
    ca_k, ca_v, sb_shift, sb_wkv, sc, cd_k, cd_v = caches
    pr = dict(n_streams=bp, t=tp, base_row=0)
    sm = dict(n_streams=bs, t=ts, base_row=n_p)
    zeros_state = jnp.zeros((bp, MIX_W, MIX_W), F32)

    oa_p = _attention(aq, ak, av, None, lp["sink_col"], n_prev=A_PREV_CHUNKS, use_sink=True, **pr)
    oa_s = _attention(aq, ak, av, (ca_k.reshape(bs, -1, A_KV_W), ca_v.reshape(bs, -1, A_KV_W)), lp["sink_col"],
                      n_prev=A_PREV_CHUNKS, use_sink=True, **sm)

    ob_p, h_p = _rwkv(hb, jnp.zeros((bp, 1, B_PROJ), F32), zeros_state, lp["rwkv"], independent=False, **pr)
    ob_s, h_s = _rwkv(hb, sb_shift.reshape(bs, 1, B_PROJ), _to_block_diag(sb_wkv), lp["rwkv"], independent=True,
                      n_streams=1, t=bs * ts, base_row=n_p)

    oc_p, s_p = _retention(hc, *tabs["rope_prompt"], zeros_state, tabs["ret"], lp["c_ln_w"], lp["c_ln_b"], **pr)
    oc_s, s_s = _retention(hc, *tabs["rope_sample"], _to_block_diag(sc), tabs["ret"], lp["c_ln_w"], lp["c_ln_b"], **sm)

    od_p = _attention(dq, dk, dv, None, lp["bias_table"], n_prev=D_PREV_CHUNKS, use_sink=False, **pr)
    od_s = _attention(dq, dk, dv, (cd_k.reshape(bs, -1, MIX_W), cd_v.reshape(bs, -1, MIX_W)), lp["bias_table"],
                      n_prev=D_PREV_CHUNKS, use_sink=False, **sm)

    mix = ((oa_p, oa_s), (ob_p, ob_s), (oc_p, oc_s), (od_p, od_s))
    return mix, (_from_block_diag(h_p), _from_block_diag(s_p)), (_from_block_diag(h_s), _from_block_diag(s_s))


def kernel(x_prompt, x_sample, cache_a_k, cache_a_v, state_b_shift, state_b_wkv, state_c, cache_d_k, cache_d_v,
           norm1_g, norm2_g, w_in, w_out, a_q_norm, a_k_norm, a_sinks, b_mu, b_w0, b_w2, b_a0, b_a2, b_g2,
           b_k_k, b_k_a, b_r_k, b_ln_w, b_ln_b, c_ln_w, c_ln_b, d_q_norm, d_k_norm, d_rel_bias,
           ffn_w1, ffn_w3, ffn_w2, moe_router, moe_w1, moe_w3, moe_w2):
    bp, tp, _ = x_prompt.shape
    bs, ts, _ = x_sample.shape
    assert ts == CHUNK
    n_p, n_s = bp * tp, bs * ts
    geom = (bp, tp, bs, ts)
    xa = x_prompt.reshape(n_p, D_MODEL)
    xb = x_sample.reshape(n_s, D_MODEL)

    tabs = {
        "ret": _retention_tables(),
        "rope_prompt": _rope_tables(np.arange(tp)),
        "rope_sample": _rope_tables(PAST_LEN + np.arange(ts)),
    }
    tile = lambda g: _row(jnp.tile(g, MIX_W // HEAD_DIM))

    p_states, s_states = [], []
    for l in range(DEPTH):
        lp = {
            "sink_col": jnp.repeat(a_sinks[l].astype(F32), CHUNK).reshape(N_HEADS * CHUNK, 1),
            "bias_table": _relbias_table(d_rel_bias[l].astype(F32), (D_PREV_CHUNKS + 1) * CHUNK),
            "rwkv": (_row(b_mu[l]), _row(b_w0[l]), b_w2[l], _row(b_a0[l]), b_a2[l], b_g2[l], _row(b_k_k[l]),
                     _row(b_k_a[l]), _row(b_r_k[l]), _row(b_ln_w[l]), _row(b_ln_b[l])),
            "c_ln_w": _row(c_ln_w[l]), "c_ln_b": _row(c_ln_b[l]),
        }
        proj = _inproj(xa, xb, n_p + n_s, _row(norm1_g[l]), w_in[l].astype(BF16), tile(a_q_norm[l]),
                       _row(jnp.tile(a_k_norm[l], A_KV_W // HEAD_DIM)), tile(d_q_norm[l]), tile(d_k_norm[l]))
        _, ak, av, hb, _, _, dk, dv = proj
        caches = (cache_a_k[l], cache_a_v[l], state_b_shift[l], state_b_wkv[l], state_c[l], cache_d_k[l], cache_d_v[l])
        mix, (wkv_p, ret_p), (wkv_s, ret_s) = _mixers(proj, caches, lp, tabs, geom)
        j = l // 2
        dense = l % 2 == 0
        x1, xn2, *packed = _outproj(xa, xb, n_p + n_s, mix, w_out[l].astype(BF16), _row(norm2_g[l]), pack=not dense)
        if dense:
            xa = xb = _ffn(xn2, x1, ffn_w1[j].astype(BF16), ffn_w3[j].astype(BF16), ffn_w2[j].astype(BF16))
        else:
            router_pad = jnp.pad(moe_router[j].astype(F32), ((0, 0), (0, ROUTER_LANES - N_EXPERTS)))
            groups = ((0, n_p), (n_p, n_s)) if l == DEPTH - 1 else ((0, n_p + n_s),)
            outs = _moe(xn2, *packed, x1, router_pad,
                        moe_w1[j].astype(BF16), moe_w3[j].astype(BF16), moe_w2[j].astype(BF16), groups)
            xa, xb = (outs[0], outs[-1])

        wa = min(A_PREV_CHUNKS * CHUNK, tp)
        wd = min(D_PREV_CHUNKS * CHUNK, tp)
        tail = lambda a, w, heads: jnp.stack(
            [a[(b + 1) * tp - w:(b + 1) * tp] for b in range(bp)]).reshape(bp, w, heads, HEAD_DIM)
        p_states.append((tail(ak, wa, 2), tail(av, wa, 2), hb[tp - 1:n_p:tp], wkv_p, ret_p,
                         tail(dk, wd, N_HEADS), tail(dv, wd, N_HEADS)))
        roll_in = lambda cache, new, heads: jnp.concatenate(
            [cache.astype(F32), new[n_p:].reshape(bs, ts, heads, HEAD_DIM)], axis=1)[:, -cache.shape[1]:]
        s_states.append((roll_in(cache_a_k[l], ak, 2), roll_in(cache_a_v[l], av, 2),
                         hb[n_p + ts - 1::ts], wkv_s, ret_s,
                         roll_in(cache_d_k[l], dk, N_HEADS), roll_in(cache_d_v[l], dv, N_HEADS)))

    if xa is xb:
        xa, xb = xa[:n_p], xa[n_p:]
    yp = xa.reshape(bp, tp, D_MODEL)
    ys = xb.reshape(bs, ts, D_MODEL)
    st = lambda group, i: jnp.stack([g[i] for g in group], axis=0)
    return (yp, ys,
            st(p_states, 0), st(p_states, 1), st(p_states, 2), st(p_states, 3), st(p_states, 4), st(p_states, 5), st(p_states, 6),
            st(s_states, 0), st(s_states, 1), st(s_states, 2), st(s_states, 3), st(s_states, 4), st(s_states, 5), st(s_states, 6))
```

```python
import functools

import jax
import jax.numpy as jnp
import numpy as np
from jax import lax
from jax.experimental import pallas as pl
from jax.experimental.pallas import tpu as pltpu
from jax.experimental.pallas import tpu_sc as plsc

F32 = jnp.float32
BF16 = jnp.bfloat16

D_MODEL = 1024
DEPTH = 2
PAST_LEN = 4096
CHUNK = 64
HEAD_DIM = 64
N_HEADS = 4
MIX_W = N_HEADS * HEAD_DIM
A_KV_W = 128
A_PREV_CHUNKS = 2
D_PREV_CHUNKS = 8
D_REL_CLIP = 128
B_PROJ = 1024
C_PROJ = 1024
IN_PROJ = 3328
B_GN_EPS = 64e-5
C_GN_EPS = 1e-6
NORM_EPS = 1e-6
ATTN_SCALE = 0.125
ROPE_BASE = 10000.0
D_FF = 2816
N_EXPERTS = 8
E_FF = 3584
NEG_BIG = -1e30

VMEM_LIMIT = 48 * 1024 * 1024

NN = ((1,), (0,))
NT = ((1,), (1,))
TN = ((0,), (0,))


def _dg(a, b, dims=NN):
    return lax.dot_general(a, b, (dims, ((), ())), preferred_element_type=F32)


def _parts(x, n):
    out = []
    r = x
    for i in range(n):
        p = r.astype(BF16)
        out.append(p)
        if i + 1 < n:
            r = r - p.astype(F32)
    return out


def _mm(a, b, dims=NN, passes=1):
    if passes == 1:
        return _dg(a.astype(BF16), b.astype(BF16), dims)
    ah, al = _parts(a, 2)
    bh, bl = _parts(b, 2)
    return _dg(ah, bh, dims) + (_dg(ah, bl, dims) + _dg(al, bh, dims))


def _mm_exact_rhs(a, b_bf, dims=NN, n=2):
    acc = None
    for p in _parts(a, n):
        t = _dg(p, b_bf, dims)
        acc = t if acc is None else acc + t
    return acc


def _iota2(shape, dim):
    return lax.broadcasted_iota(jnp.int32, shape, dim)


def _head_mask(rows, cols=MIX_W):
    return (_iota2((rows, cols), 0) >> 6) == (_iota2((rows, cols), 1) >> 6)


def _seg_matrix(width, value):
    m = _head_mask(width, width)
    return jnp.where(m, value, 0.0).astype(BF16)


def _tile4(z):
    return jnp.concatenate([z, z, z, z], axis=0)


def _fold4(z):
    return (z[0:64] + z[64:128]) + (z[128:192] + z[192:256])


def _sigmoid(x):
    return 1.0 / (1.0 + jnp.exp(-x))


def _cparams(sem):
    return pltpu.CompilerParams(dimension_semantics=sem, vmem_limit_bytes=VMEM_LIMIT)


IN_TM = 512


def _two_source_specs(xa, xb, n):
    na, nb = xa.shape[0] // IN_TM, xb.shape[0] // IN_TM
    spec_a = pl.BlockSpec((IN_TM, D_MODEL), lambda i: (jnp.minimum(i, na - 1), 0))
    spec_b = pl.BlockSpec((IN_TM, D_MODEL), lambda i: (jnp.clip(i - na, 0, nb - 1), 0))
    return na, n // IN_TM, spec_a, spec_b


def _inproj_kernel(xa_ref, xb_ref, g_ref, w_ref, aqg_ref, akg_ref, dqg_ref, dkg_ref,
                   aq_ref, ak_ref, av_ref, hb_ref, hc_ref, dq_ref, dk_ref, dv_ref, *, n_first):
    x = jnp.where(pl.program_id(0) < n_first, xa_ref[...], xb_ref[...])
    ms = jnp.mean(x * x, axis=-1, keepdims=True)
    xn = ((x * lax.rsqrt(ms + NORM_EPS)) * g_ref[...]).astype(BF16)
    seg = _seg_matrix(MIX_W, 1.0 / HEAD_DIM)

    def proj(lo, hi):
        return jnp.dot(xn, w_ref[:, lo:hi], preferred_element_type=F32)

    def head_rms(h, gain_ref):
        w = h.shape[-1]
        msq = _mm_exact_rhs(h * h, seg[:w, :w], n=1)
        return (h * lax.rsqrt(msq + NORM_EPS)) * gain_ref[...]

    aq_ref[...] = head_rms(proj(0, 256), aqg_ref)
    ak_ref[...] = head_rms(proj(256, 384), akg_ref)
    av_ref[...] = proj(384, 512)
    hb_ref[...] = proj(512, 1536)
    hc_ref[...] = proj(1536, 2560)
    dq_ref[...] = head_rms(proj(2560, 2816), dqg_ref)
    dk_ref[...] = head_rms(proj(2816, 3072), dkg_ref)
    dv_ref[...] = proj(3072, 3328)


def _inproj(xa, xb, n, g, w_bf, aqg, akg, dqg, dkg):
    na, nblk, spec_a, spec_b = _two_source_specs(xa, xb, n)
    widths = (256, 128, 128, B_PROJ, C_PROJ, 256, 256, 256)
    row = lambda w: pl.BlockSpec((IN_TM, w), lambda i: (i, 0))
    full = lambda a: pl.BlockSpec(a.shape, lambda i: (0,) * a.ndim)
    return pl.pallas_call(
        functools.partial(_inproj_kernel, n_first=na),
        grid=(nblk,),
        in_specs=[spec_a, spec_b, full(g), full(w_bf), full(aqg), full(akg), full(dqg), full(dkg)],
        out_specs=[row(w) for w in widths],
        out_shape=[jax.ShapeDtypeStruct((n, w), F32) for w in widths],
        compiler_params=_cparams(("parallel",)),
        name="inproj",
    )(xa, xb, g, w_bf, aqg, akg, dqg, dkg)


PACK_W = 256


def _pack_bf16_pairs(hi, lo):
    bits = lambda z: pltpu.bitcast(z.astype(BF16).astype(F32), jnp.int32)
    return bits(hi) | lax.shift_right_logical(bits(lo), jnp.full(lo.shape, 16, jnp.int32))


def _unpack_bf16_pairs(w):
    hi = pltpu.bitcast(w & jnp.int32(-65536), F32)
    lo = pltpu.bitcast(lax.shift_left(w, jnp.full(w.shape, 16, jnp.int32)), F32)
    return hi, lo


def _pack_rows(x):
    return (_pack_bf16_pairs(x[:, 0:PACK_W], x[:, PACK_W:2 * PACK_W]),
            _pack_bf16_pairs(x[:, 2 * PACK_W:3 * PACK_W], x[:, 3 * PACK_W:4 * PACK_W]))


def _unpack_rows(wa, wb):
    return jnp.concatenate(_unpack_bf16_pairs(wa) + _unpack_bf16_pairs(wb), axis=1)


ROUTER_LANES = 128


def _route(xn, router, counts):
    logits = _mm(xn, router, passes=3)
    lane = _iota2(logits.shape, 1)
    logits = jnp.where(lane < N_EXPERTS, logits, NEG_BIG)
    m1 = jnp.max(logits, axis=-1, keepdims=True)
    i1 = jnp.min(jnp.where(logits == m1, lane, ROUTER_LANES), axis=-1, keepdims=True)
    rest = jnp.where(lane == i1, NEG_BIG, logits)
    m2 = jnp.max(rest, axis=-1, keepdims=True)
    i2 = jnp.min(jnp.where(rest == m2, lane, ROUTER_LANES), axis=-1, keepdims=True)
    e2 = jnp.exp(m2 - m1)
    den = 1.0 + e2
    gates = jnp.where(lane == i1, 1.0 / den, 0.0) + jnp.where(lane == i2, e2 / den, 0.0)
    sel = jnp.where(gates > 0.0, 1.0, 0.0)
    tm = sel.shape[0]
    before = jnp.where(_iota2((tm, tm), 1) < _iota2((tm, tm), 0), 1.0, 0.0).astype(BF16)
    rank = (_dg(before, sel.astype(BF16)) + counts).astype(jnp.int32)
    return gates, rank, counts + jnp.sum(sel, axis=0, keepdims=True)


def _outproj_kernel(xa_ref, xb_ref, *refs, n_first_x, n_first_mix, route):
    mix_refs, (w_ref, g_ref), rest = refs[:8], refs[8:10], refs[10:]
    i = pl.program_id(0)
    acc = jnp.where(i < n_first_x, xa_ref[...], xb_ref[...])
    for m in range(4):
        o = jnp.where(i < n_first_mix, mix_refs[2 * m][...], mix_refs[2 * m + 1][...])
        acc = acc + jnp.dot(o.astype(BF16), w_ref[m * MIX_W:(m + 1) * MIX_W, :], preferred_element_type=F32)
    ms = jnp.mean(acc * acc, axis=-1, keepdims=True)
    xn = (acc * lax.rsqrt(ms + NORM_EPS)) * g_ref[...]
    if not route:
        x1_ref, xn_ref = rest
        x1_ref[...] = acc
        xn_ref[...] = xn.astype(BF16)
        return
    r_ref, x1_ref, pa_ref, pb_ref, gate_ref, rank_ref, cnt_ref, cnt_scr = rest

    @pl.when(i == 0)
    def _():
        cnt_scr[...] = jnp.zeros_like(cnt_scr)

    x1_ref[...] = acc
    pa_ref[...], pb_ref[...] = _pack_rows(xn)
    gates, rank, counts = _route(xn.astype(BF16).astype(F32), r_ref[...], cnt_scr[...])
    gate_ref[...] = gates
    rank_ref[...] = rank
    cnt_scr[...] = counts
    cnt_ref[...] = counts.astype(jnp.int32)


def _outproj(xa, xb, n, mix, w_bf, g2, router_pad=None):
    na, nblk, spec_a, spec_b = _two_source_specs(xa, xb, n)
    nm_p, nm_s = mix[0][0].shape[0] // IN_TM, mix[0][1].shape[0] // IN_TM
    row = lambda w: pl.BlockSpec((IN_TM, w), lambda i: (i, 0))
    full = lambda a: pl.BlockSpec(a.shape, lambda i: (0,) * a.ndim)
    mix_specs = [pl.BlockSpec((IN_TM, MIX_W), lambda i: (jnp.minimum(i, nm_p - 1), 0)),
                 pl.BlockSpec((IN_TM, MIX_W), lambda i: (jnp.clip(i - nm_p, 0, nm_s - 1), 0))] * 4
    route = router_pad is not None
    inputs = [xa, xb, *[a for pair in mix for a in pair], w_bf, g2]
    in_specs = [spec_a, spec_b] + mix_specs + [full(w_bf), full(g2)]
    if route:
        inputs.append(router_pad)
        in_specs.append(full(router_pad))
        out_specs = [row(D_MODEL), row(PACK_W), row(PACK_W), row(ROUTER_LANES), row(ROUTER_LANES),
                     pl.BlockSpec((1, ROUTER_LANES), lambda i: (0, 0))]
        out_shape = [jax.ShapeDtypeStruct((n, D_MODEL), F32),
                     jax.ShapeDtypeStruct((n, PACK_W), jnp.int32), jax.ShapeDtypeStruct((n, PACK_W), jnp.int32),
                     jax.ShapeDtypeStruct((n, ROUTER_LANES), F32), jax.ShapeDtypeStruct((n, ROUTER_LANES), jnp.int32),
                     jax.ShapeDtypeStruct((1, ROUTER_LANES), jnp.int32)]
        scratch = [pltpu.VMEM((1, ROUTER_LANES), F32)]
    else:
        out_specs = [row(D_MODEL), row(D_MODEL)]
        out_shape = [jax.ShapeDtypeStruct((n, D_MODEL), F32), jax.ShapeDtypeStruct((n, D_MODEL), BF16)]
        scratch = []
    return pl.pallas_call(
        functools.partial(_outproj_kernel, n_first_x=na, n_first_mix=nm_p, route=route),
        grid=(nblk,),
        in_specs=in_specs,
        out_specs=out_specs,
        out_shape=out_shape,
        scratch_shapes=scratch,
        compiler_params=_cparams(("arbitrary",)),
        name="outproj",
    )(*inputs)


FFN_TM = 512
FFN_TF = 1408


def _ffn_kernel(xn_ref, x1_ref, w1_ref, w3_ref, w2_ref, o_ref):
    f = pl.program_id(1)
    xn = xn_ref[...]
    a = jnp.dot(xn, w1_ref[...], preferred_element_type=F32)
    b = jnp.dot(xn, w3_ref[...], preferred_element_type=F32)
    h = ((a * _sigmoid(a)) * b).astype(BF16)
    y = jnp.dot(h, w2_ref[...], preferred_element_type=F32)

    @pl.when(f == 0)
    def _():
        o_ref[...] = x1_ref[...] + y

    @pl.when(f != 0)
    def _():
        o_ref[...] += y


def _ffn(xn, x1, w1_bf, w3_bf, w2_bf):
    n = xn.shape[0]
    return pl.pallas_call(
        _ffn_kernel,
        grid=(n // FFN_TM, D_FF // FFN_TF),
        in_specs=[
            pl.BlockSpec((FFN_TM, D_MODEL), lambda i, f: (i, 0)),
            pl.BlockSpec((FFN_TM, D_MODEL), lambda i, f: (i, 0)),
            pl.BlockSpec((D_MODEL, FFN_TF), lambda i, f: (0, f)),
            pl.BlockSpec((D_MODEL, FFN_TF), lambda i, f: (0, f)),
            pl.BlockSpec((FFN_TF, D_MODEL), lambda i, f: (f, 0)),
        ],
        out_specs=pl.BlockSpec((FFN_TM, D_MODEL), lambda i, f: (i, 0)),
        out_shape=jax.ShapeDtypeStruct((n, D_MODEL), F32),
        compiler_params=_cparams(("parallel", "arbitrary")),
        name="ffn",
    )(xn, x1, w1_bf, w3_bf, w2_bf)


MOE_R = 512
MOE_TF = 1792
SC_WINDOW = 128


def _moe_plan(gates, rank, counts, n):
    n_blocks = (2 * n) // MOE_R + N_EXPERTS + 1
    spare_row = (n_blocks - 1) * MOE_R
    sel = gates[:, :N_EXPERTS] > 0.0
    rank = rank[:, :N_EXPERTS]
    counts = counts[0, :N_EXPERTS]
    padded = ((counts + MOE_R - 1) // MOE_R) * MOE_R
    pad_end = jnp.cumsum(padded)
    pad_start = pad_end - padded
    pos = jnp.where(sel, pad_start[None, :] + rank, -1)
    order = jnp.cumsum(sel.astype(jnp.int32), axis=1)
    pick = lambda j: jnp.max(jnp.where(jnp.logical_and(sel, order == j), pos, -1), axis=1)
    to_row = lambda p: jnp.where(p >= 0, p, spare_row).astype(jnp.int32).reshape(1, n)
    block_expert = jnp.minimum(
        jnp.sum(pad_end[None, :] <= (jnp.arange(n_blocks) * MOE_R)[:, None], axis=1), N_EXPERTS - 1)
    return dict(n_blocks=n_blocks, pos0=to_row(pick(1)), pos1=to_row(pick(2)),
                block_expert=block_expert.astype(jnp.int32), n_used=(pad_end[-1:] // MOE_R).astype(jnp.int32))


def _sc_mesh():
    return plsc.VectorSubcoreMesh(core_axis_name="core", subcore_axis_name="subcore")


def _sc_scatter_rows(table, idx_lists, n_rows):
    n, cols = table.shape
    k = len(idx_lists)

    @functools.partial(pl.kernel, out_type=jax.ShapeDtypeStruct((n_rows, cols), table.dtype), mesh=_sc_mesh())
    def scatter(x_hbm, *rest):
        i_hbms, o_hbm = rest[:k], rest[k]

        def body(x_vmem, *i_vmems):
            for i_vmem in i_vmems:
                pltpu.sync_copy(x_vmem, o_hbm.at[i_vmem.at[0]])

        pltpu.emit_pipeline(
            body,
            grid=(n // SC_WINDOW,),
            in_specs=[pl.BlockSpec((SC_WINDOW, cols), lambda i: (i, 0))]
            + [pl.BlockSpec((1, SC_WINDOW), lambda i: (0, i))] * k,
            out_specs=[],
            core_axis_name=("core", "subcore"),
            dimension_semantics=(pltpu.PARALLEL,),
        )(x_hbm, *i_hbms)

    return scatter(table, *idx_lists)


def _sc_gather_rows(table, idx):
    n = idx.shape[1]
    cols = table.shape[1]

    @functools.partial(pl.kernel, out_type=jax.ShapeDtypeStruct((n, cols), table.dtype), mesh=_sc_mesh())
    def gather(x_hbm, i_hbm, o_hbm):
        def body(i_vmem, o_vmem):
            pltpu.sync_copy(x_hbm.at[i_vmem.at[0]], o_vmem)

        pltpu.emit_pipeline(
            body,
            grid=(n // SC_WINDOW,),
            in_specs=[pl.BlockSpec((1, SC_WINDOW), lambda i: (0, i))],
            out_specs=[pl.BlockSpec((SC_WINDOW, cols), lambda i: (i, 0))],
            core_axis_name=("core", "subcore"),
            dimension_semantics=(pltpu.PARALLEL,),
        )(i_hbm, o_hbm)

    return gather(table, idx)


def _moe_expert_kernel(be_ref, nu_ref, xa_ref, xb_ref, gs_ref, w1_ref, w3_ref, w2_ref, oa_ref, ob_ref, acc_ref):
    j, f = pl.program_id(0), pl.program_id(1)
    used = j < nu_ref[0]

    @pl.when(used)
    def _():
        x = _unpack_rows(xa_ref[...], xb_ref[...]).astype(BF16)
        a = jnp.dot(x, w1_ref[...], preferred_element_type=F32)
        b = jnp.dot(x, w3_ref[...], preferred_element_type=F32)
        h = ((a * _sigmoid(a)) * b).astype(BF16)
        y = jnp.dot(h, w2_ref[...], preferred_element_type=F32)

        @pl.when(f == 0)
        def _():
            acc_ref[...] = y

        @pl.when(f != 0)
        def _():
            acc_ref[...] += y

    @pl.when(f == pl.num_programs(1) - 1)
    def _():
        lane = _iota2(gs_ref.shape, 1)
        g = jnp.sum(jnp.where(lane == be_ref[j], gs_ref[...], 0.0), axis=1, keepdims=True)
        oa_ref[...], ob_ref[...] = _pack_rows(jnp.where(used, acc_ref[...] * g, 0.0))


def _moe_experts(plan, xs_a, xs_b, gs, w1_bf, w3_bf, w2_bf):
    n_blocks = plan["n_blocks"]
    half = pl.BlockSpec((MOE_R, PACK_W), lambda j, f, be, nu: (j, 0))
    grid_spec = pltpu.PrefetchScalarGridSpec(
        num_scalar_prefetch=2,
        grid=(n_blocks, E_FF // MOE_TF),
        in_specs=[half, half,
                  pl.BlockSpec((MOE_R, ROUTER_LANES), lambda j, f, be, nu: (j, 0)),
                  pl.BlockSpec((None, D_MODEL, MOE_TF), lambda j, f, be, nu: (be[j], 0, f)),
                  pl.BlockSpec((None, D_MODEL, MOE_TF), lambda j, f, be, nu: (be[j], 0, f)),
                  pl.BlockSpec((None, MOE_TF, D_MODEL), lambda j, f, be, nu: (be[j], f, 0))],
        out_specs=[half, half],
        scratch_shapes=[pltpu.VMEM((MOE_R, D_MODEL), F32)])
    return pl.pallas_call(
        _moe_expert_kernel,
        grid_spec=grid_spec,
        out_shape=[jax.ShapeDtypeStruct((n_blocks * MOE_R, PACK_W), jnp.int32)] * 2,
        compiler_params=_cparams(("arbitrary", "arbitrary")),
        name="moe_experts",
    )(plan["block_expert"], plan["n_used"], xs_a, xs_b, gs, w1_bf, w3_bf, w2_bf)


def _moe_combine_kernel(x1_ref, a0_ref, b0_ref, a1_ref, b1_ref, o_ref):
    o_ref[...] = (x1_ref[...] + _unpack_rows(a0_ref[...], b0_ref[...])) + _unpack_rows(a1_ref[...], b1_ref[...])


def _moe_combine(x1, picked, row0, n_rows):
    base = row0 // IN_TM
    row = lambda w: pl.BlockSpec((IN_TM, w), lambda i: (base + i, 0))
    return pl.pallas_call(
        _moe_combine_kernel,
        grid=(n_rows // IN_TM,),
        in_specs=[row(D_MODEL)] + [row(PACK_W)] * 4,
        out_specs=pl.BlockSpec((IN_TM, D_MODEL), lambda i: (i, 0)),
        out_shape=jax.ShapeDtypeStruct((n_rows, D_MODEL), F32),
        compiler_params=_cparams(("parallel",)),
        name="moe_combine",
    )(x1, *picked)


def _moe(xn_a, xn_b, x1, routing, w1_bf, w3_bf, w2_bf, row_groups):
    gates, rank, counts = routing
    plan = _moe_plan(gates, rank, counts, x1.shape[0])
    n_rows = plan["n_blocks"] * MOE_R
    idx = (plan["pos0"], plan["pos1"])
    xs_a = _sc_scatter_rows(xn_a, idx, n_rows)
    xs_b = _sc_scatter_rows(xn_b, idx, n_rows)
    gs = _sc_scatter_rows(gates, idx, n_rows)
    os_a, os_b = _moe_experts(plan, xs_a, xs_b, gs, w1_bf, w3_bf, w2_bf)
    picked = [_sc_gather_rows(t, p) for p in idx for t in (os_a, os_b)]
    return [_moe_combine(x1, picked, row0, rows) for row0, rows in row_groups]


def _relbias_kernel(rb_ref, o_ref, *, nk):
    h = pl.program_id(0)
    q = _iota2((CHUNK, nk), 0)
    r = _iota2((CHUNK, nk), 1)
    idx = jnp.clip(q - (r - (nk - CHUNK)), -D_REL_CLIP, D_REL_CLIP) + D_REL_CLIP

    def body(j, acc):
        return jnp.where(idx == j, rb_ref[h, j], acc)

    o_ref[...] = lax.fori_loop(0, 2 * D_REL_CLIP + 1, body, jnp.zeros((CHUNK, nk), F32))


def _relbias_table(rel_bias, nk):
    return pl.pallas_call(
        functools.partial(_relbias_kernel, nk=nk),
        grid=(N_HEADS,),
        in_specs=[pl.BlockSpec(memory_space=pltpu.SMEM)],
        out_specs=pl.BlockSpec((CHUNK, nk), lambda h: (h, 0)),
        out_shape=jax.ShapeDtypeStruct((N_HEADS * CHUNK, nk), F32),
        name="relbias",
    )(rel_bias)


ATTN_QB = 8
ATTN_GROUP = 8


def _attn_kernel(q_ref, kp_ref, kc_ref, vp_ref, vc_ref, x_ref, *rest, qb, n_prev, use_sink, mask_first):
    o_ref, kbuf, vbuf = rest[-3:]
    i = pl.program_id(1)
    p_rows = kp_ref.shape[0]
    nk = (n_prev + 1) * CHUNK
    wk = kp_ref.shape[1]

    kbuf[0:p_rows, :] = kp_ref[...].astype(BF16)
    kbuf[p_rows:, :] = kc_ref[...].astype(BF16)
    vbuf[0:p_rows, :] = vp_ref[...].astype(BF16)
    vbuf[p_rows:, :] = vc_ref[...].astype(BF16)

    hmask = _head_mask(N_HEADS * CHUNK)
    extra = x_ref[...]
    grouped = wk != MIX_W
    low = _iota2((CHUNK, A_KV_W), 1) < HEAD_DIM

    def stack_queries(qj):
        if not grouped:
            return jnp.where(hmask, _tile4(qj), 0.0)
        shifted = pltpu.roll(qj, MIX_W - HEAD_DIM, 1)[:, :A_KV_W]
        return jnp.concatenate([jnp.where(low, qj[:, :A_KV_W], 0.0), jnp.where(low, shifted, 0.0),
                                jnp.where(low, 0.0, shifted), jnp.where(low, 0.0, qj[:, A_KV_W:])], axis=0)

    def unstack_outputs(o_all):
        if not grouped:
            return _fold4(jnp.where(hmask, o_all, 0.0))
        b0, b1, b2, b3 = (o_all[h * CHUNK:(h + 1) * CHUNK] for h in range(N_HEADS))
        left = jnp.where(low, b0, 0.0) + pltpu.roll(jnp.where(low, b1, 0.0), HEAD_DIM, 1)
        right = pltpu.roll(jnp.where(low, 0.0, b2), HEAD_DIM, 1) + jnp.where(low, 0.0, b3)
        return jnp.concatenate([left, right], axis=1)

    def scores(j):
        base = p_rows + (j - n_prev) * CHUNK
        qs = stack_queries(q_ref[pl.ds(j * CHUNK, CHUNK), :] * ATTN_SCALE).astype(BF16)
        s = _dg(qs, kbuf[pl.ds(base, nk), :], NT)
        if not use_sink:
            s = s + extra
        if mask_first and base < p_rows:
            krow = base + _iota2(s.shape, 1)
            s = jnp.where(jnp.logical_and(i == 0, krow < p_rows), NEG_BIG, s)
        return s

    def weights(s):
        m = jnp.max(s, axis=-1, keepdims=True)
        if use_sink:
            m = jnp.maximum(m, extra)
        e = jnp.exp(s - m)
        den = jnp.sum(e, axis=-1, keepdims=True)
        if use_sink:
            den = den + jnp.exp(extra - m)
        return e.astype(BF16), 1.0 / den

    def output(j, e, inv_den):
        base = p_rows + (j - n_prev) * CHUNK
        o_all = _dg(e, vbuf[pl.ds(base, nk), :]) * inv_den
        o_ref[pl.ds(j * CHUNK, CHUNK), :] = unstack_outputs(o_all)

    for j0 in range(0, qb, ATTN_GROUP):
        group = range(j0, min(j0 + ATTN_GROUP, qb))
        ss = [scores(j) for j in group]
        ws = [weights(s) for s in ss]
        for j, (e, inv_den) in zip(group, ws):
            output(j, e, inv_den)


def _attention(q, k, v, prev, extra, *, n_prev, use_sink, n_streams, t, base_row):
    wk = k.shape[-1]
    if prev is None:
        qb = ATTN_QB
        rows = qb * CHUNK
        nblk = t // rows
        base = base_row // rows
        prev_spec = pl.BlockSpec((rows, wk), lambda s, i: (base + s * nblk + jnp.maximum(i - 1, 0), 0))
        k_prev, v_prev, p_rows, mask_first = k, v, rows, True
    else:
        qb, rows, nblk = t // CHUNK, t, 1
        base = base_row // rows
        k_prev, v_prev = prev
        p_rows = k_prev.shape[1]
        prev_spec = pl.BlockSpec((None, p_rows, wk), lambda s, i: (s, 0, 0))
        mask_first = False
    cur = lambda w: pl.BlockSpec((rows, w), lambda s, i: (base + s * nblk + i, 0))
    kern = functools.partial(_attn_kernel, qb=qb, n_prev=n_prev, use_sink=use_sink, mask_first=mask_first)
    return pl.pallas_call(
        kern,
        grid=(n_streams, nblk),
        in_specs=[cur(MIX_W), prev_spec, cur(wk), prev_spec, cur(wk),
                  pl.BlockSpec(extra.shape, lambda s, i: (0, 0))],
        out_specs=pl.BlockSpec((rows, MIX_W), lambda s, i: (s * nblk + i, 0)),
        out_shape=jax.ShapeDtypeStruct((n_streams * t, MIX_W), F32),
        scratch_shapes=[pltpu.VMEM((p_rows + rows, wk), BF16), pltpu.VMEM((p_rows + rows, wk), BF16)],
        compiler_params=_cparams(("parallel", "arbitrary")),
        name="attn_sink" if use_sink else "attn_bias",
    )(q, k_prev, k, v_prev, v, extra)


def _head_layer_norm(o, seg_mean_bf, w, b, eps):
    mu = _mm_exact_rhs(o, seg_mean_bf)
    d = o - mu
    var = _mm_exact_rhs(d * d, seg_mean_bf)
    return (d * lax.rsqrt(var + eps)) * w + b


def _ret_kernel(hc_ref, cos_ref, sin_ref, s0_ref, dstack_ref, qsc_ref, ksc_ref, gam_ref, lnw_ref, lnb_ref,
                *rest, qb):
    o_ref, sout_ref, s_scr = rest[-3:]
    i = pl.program_id(1)

    @pl.when(i == 0)
    def _():
        s_scr[...] = s0_ref[...]

    hmask = _head_mask(N_HEADS * CHUNK)
    seg_mean = _seg_matrix(MIX_W, 1.0 / HEAD_DIM)
    rows = qb * CHUNK
    first_half = (_iota2((rows, MIX_W), 1) & (HEAD_DIM - 1)) < (HEAD_DIM // 2)
    cos = jnp.concatenate([cos_ref[...]] * (MIX_W // ROPE_W), axis=1)
    sin = jnp.concatenate([sin_ref[...]] * (MIX_W // ROPE_W), axis=1)

    def rope(x):
        partner = jnp.where(first_half, pltpu.roll(x, MIX_W - HEAD_DIM // 2, 1), pltpu.roll(x, HEAD_DIM // 2, 1))
        return x * cos + partner * sin

    q = rope(hc_ref[:, 0:256])
    k = rope(hc_ref[:, 256:512]) * ATTN_SCALE
    v_bf = hc_ref[:, 512:768].astype(BF16)
    state = s_scr[...]
    outs = []
    for j in range(qb):
        sl = slice(j * CHUNK, (j + 1) * CHUNK)
        qj, kj, vj = q[sl], k[sl], v_bf[sl]
        qs = jnp.where(hmask, _tile4(qj), 0.0).astype(BF16)
        sc = _dg(qs, kj.astype(BF16), NT) * dstack_ref[...]
        intra = _fold4(jnp.where(hmask, _dg(sc.astype(BF16), vj), 0.0))
        inter = _dg((qj * qsc_ref[...]).astype(BF16), state.astype(BF16))
        kv = _dg((kj * ksc_ref[...]).astype(BF16), vj, TN)
        state = gam_ref[...] * state + jnp.where(hmask, kv, 0.0)
        outs.append(intra + inter)
    s_scr[...] = state
    sout_ref[...] = state
    y = _head_layer_norm(jnp.concatenate(outs, axis=0), seg_mean, lnw_ref[...], lnb_ref[...], C_GN_EPS)
    g = hc_ref[:, 768:1024]
    o_ref[...] = y * (g * _sigmoid(g))


def _retention(hc, cos, sin, s0_bd, tabs, lnw, lnb, *, n_streams, t, base_row):
    qb = min(8, t // CHUNK)
    rows = qb * CHUNK
    nblk = t // rows
    base = base_row // rows
    dstack, qsc, ksc, gam = tabs
    full = lambda a: pl.BlockSpec(a.shape, lambda s, i: (0,) * a.ndim)
    cur = lambda w: pl.BlockSpec((rows, w), lambda s, i: (base + s * nblk + i, 0))
    state = pl.BlockSpec((None, MIX_W, MIX_W), lambda s, i: (s, 0, 0))
    return pl.pallas_call(
        functools.partial(_ret_kernel, qb=qb),
        grid=(n_streams, nblk),
        in_specs=[cur(C_PROJ),
                  pl.BlockSpec((rows, ROPE_W), lambda s, i: (i, 0)),
                  pl.BlockSpec((rows, ROPE_W), lambda s, i: (i, 0)),
                  state, full(dstack), full(qsc), full(ksc), full(gam), full(lnw), full(lnb)],
        out_specs=[pl.BlockSpec((rows, MIX_W), lambda s, i: (s * nblk + i, 0)), state],
        out_shape=[jax.ShapeDtypeStruct((n_streams * t, MIX_W), F32),
                   jax.ShapeDtypeStruct((n_streams, MIX_W, MIX_W), F32)],
        scratch_shapes=[pltpu.VMEM((MIX_W, MIX_W), F32)],
        compiler_params=_cparams(("parallel", "arbitrary")),
        name="retention",
    )(hc, cos, sin, s0_bd, dstack, qsc, ksc, gam, lnw, lnb)


def _retention_tables():
    gamma = 1.0 - 2.0 ** (-5.0 - np.arange(N_HEADS, dtype=np.float64))
    t = np.arange(CHUNK)
    diff = t[:, None] - t[None, :]
    dmat = np.where(diff >= 0, gamma[:, None, None] ** np.maximum(diff, 0), 0.0)
    dstack = dmat.reshape(N_HEADS * CHUNK, CHUNK)
    lanes = lambda per_head: np.repeat(per_head, HEAD_DIM, axis=-1)
    qsc = lanes(gamma[None, :] ** (t + 1)[:, None])
    ksc = lanes(gamma[None, :] ** (CHUNK - 1 - t)[:, None])
    gam = np.broadcast_to(lanes(gamma ** CHUNK)[:, None], (MIX_W, MIX_W))
    return tuple(jnp.asarray(a, F32) for a in (dstack, qsc, ksc, gam))


ROPE_W = 2 * HEAD_DIM


def _rope_tables(pos):
    half = HEAD_DIM // 2
    theta = np.float32(1.0) / (np.float32(ROPE_BASE) ** np.linspace(0.0, 1.0, half, dtype=np.float32))
    ang = np.asarray(pos, np.float32)[:, None] * theta[None, :]
    cos, sin = np.cos(ang), np.sin(ang)
    reps = ROPE_W // HEAD_DIM
    cos_t = np.tile(np.concatenate([cos, cos], axis=-1), (1, reps))
    sin_t = np.tile(np.concatenate([-sin, sin], axis=-1), (1, reps))
    return jnp.asarray(cos_t, F32), jnp.asarray(sin_t, F32)


DECAY_SCALE = 0.6065306597126334
RWKV_CB = 4
N_LEVELS = 6
MASK_HEAD, MASK_STRICT, MASK_INCL, MASK_LEVEL0 = 0, 1, 2, 3


def _rwkv_masks():
    n4 = N_HEADS * CHUNK
    ri = np.arange(n4)[:, None]
    ci = np.arange(n4)[None, :]
    head = (ri >> 6) == (ci >> 6)
    tabs = [head, head & ((ci & 63) < (ri & 63)), head & ((ci & 63) <= (ri & 63))]
    for log_m in range(N_LEVELS):
        same = (ri >> (log_m + 1)) == (ci >> (log_m + 1))
        tabs.append(same & (((ri >> log_m) & 1) == 1) & (((ci >> log_m) & 1) == 0))
    return jnp.asarray(np.stack(tabs), BF16)


def _rwkv_kernel(hb_ref, shift0_ref, h0_ref, masks_ref, mu_ref, w0_ref, w2_ref, a0_ref, a2_ref, g2_ref,
                 kk_ref, ka_ref, rk_ref, lnw_ref, lnb_ref, *rest, cb, independent):
    o_ref, hout_ref, h_scr, shift_scr = rest[-4:]
    c = pl.program_id(1)
    rows = cb * CHUNK
    xb = hb_ref[...]
    row = _iota2(xb.shape, 0)
    prev = pltpu.roll(xb, 1, 0)
    if independent:
        for j in range(cb):
            prev = jnp.where(row == j * CHUNK, shift0_ref[j], prev)
    else:
        @pl.when(c == 0)
        def _():
            h_scr[...] = h0_ref[0]
            shift_scr[...] = shift0_ref[0]

        prev = jnp.where(row == 0, shift_scr[...], prev)
        shift_scr[...] = xb[rows - 1:rows, :]
    xs = xb + mu_ref[...] * (prev - xb)
    r = xs[:, 0:256]
    k = xs[:, 256:512]
    v = xs[:, 512:768]
    xw = xs[:, 768:832]
    xa = xs[:, 832:896]
    xg = xs[:, 896:1024]

    z = w0_ref[...] + _mm(jnp.tanh(xw), w2_ref[...], passes=3)
    lw = -DECAY_SCALE * _sigmoid(z)
    a_gate = _sigmoid(a0_ref[...] + _mm(xa, a2_ref[...], passes=3))
    gate = _mm(_sigmoid(xg), g2_ref[...], passes=1)

    seg_sum = _seg_matrix(MIX_W, 1.0)
    seg_mean = _seg_matrix(MIX_W, 1.0 / HEAD_DIM)
    kkn = k * kk_ref[...]
    norm = jnp.sqrt(_mm_exact_rhs(kkn * kkn, seg_sum))
    kk = kkn / jnp.maximum(norm, 1e-12)
    kf = k * (1.0 + (a_gate - 1.0) * ka_ref[...])

    tt = _iota2((rows, rows), 0)
    ss = _iota2((rows, rows), 1)
    tril = jnp.where(jnp.logical_and(ss <= tt, (ss >> 6) == (tt >> 6)), 1.0, 0.0).astype(BF16)
    lw_parts = _parts(lw, 3)
    cum = _dg(tril, lw_parts[0]) + (_dg(tril, lw_parts[1]) + _dg(tril, lw_parts[2]))
    w_inv = jnp.exp(-cum)
    rho = (r * jnp.exp(cum)).astype(BF16)
    alpha = (-kk * jnp.exp(cum - lw)).astype(BF16)
    beta = ((kk * a_gate) * w_inv).astype(BF16)
    kappa = (kf * w_inv).astype(BF16)
    v_bf = v.astype(BF16)

    hmask = masks_ref[MASK_HEAD]
    n4 = N_HEADS * CHUNK
    eye = jnp.where(_iota2((n4, n4), 0) == _iota2((n4, n4), 1), 1.0, 0.0)

    pre, a_bfs, t_invs = [], [], []
    for j in range(cb):
        sl = slice(j * CHUNK, (j + 1) * CHUNK)
        bd = lambda zz: _tile4(zz[sl]) * hmask
        al_bd, be_bd, ka_bd, rh_bd, v_bd = bd(alpha), bd(beta), bd(kappa), bd(rho), bd(v_bf)
        a_bf = _dg(al_bd, be_bd, NT).astype(BF16) * masks_ref[MASK_STRICT]
        a_ak = _dg(al_bd, ka_bd, NT).astype(BF16) * masks_ref[MASK_STRICT]
        b_rb = _dg(rh_bd, be_bd, NT).astype(BF16) * masks_ref[MASK_INCL]
        b_rk = _dg(rh_bd, ka_bd, NT).astype(BF16) * masks_ref[MASK_INCL]
        x0 = _dg(a_ak, v_bd)
        y0 = _dg(b_rk, v_bd)
        sn0 = _dg(v_bd, ka_bd, TN)
        w_chunk = jnp.exp(cum[(j + 1) * CHUNK - 1:(j + 1) * CHUNK, :])
        a_bfs.append(a_bf)
        t_invs.append(eye + (a_bf * masks_ref[MASK_LEVEL0]).astype(F32))
        pre.append([al_bd, be_bd, rh_bd, b_rb, None, x0, y0, sn0, w_chunk])
    for lvl in range(1, N_LEVELS):
        t_bfs = [t.astype(BF16) for t in t_invs]
        e_mats = [_dg(a_bfs[j] * masks_ref[MASK_LEVEL0 + lvl], t_bfs[j]) for j in range(cb)]
        t_invs = [t_invs[j] + _dg(t_bfs[j], e_mats[j].astype(BF16)) for j in range(cb)]
    for j in range(cb):
        pre[j][4] = t_invs[j].astype(BF16)

    ys = []
    h = None if independent else h_scr[...]
    for j in range(cb):
        al_bd, be_bd, rh_bd, b_rb, t_bf, x0, y0, sn0, w_chunk = pre[j]
        h0 = h0_ref[j] if independent else h
        h0_bf = h0.astype(BF16)
        x_mat = _dg(al_bd, h0_bf, NT) + x0
        u_bf = _dg(t_bf, x_mat.astype(BF16)).astype(BF16)
        y_bd = _dg(rh_bd, h0_bf, NT) + _dg(b_rb, u_bf) + y0
        h_new = (h0 + _dg(u_bf, be_bd, TN) + sn0) * w_chunk
        ys.append(_fold4(y_bd))
        if independent:
            hout_ref[j] = h_new
        else:
            h = h_new
    if not independent:
        h_scr[...] = h
        hout_ref[0] = h

    y = _head_layer_norm(jnp.concatenate(ys, axis=0), seg_mean, lnw_ref[...], lnb_ref[...], B_GN_EPS)
    bonus = _mm_exact_rhs(r * kf * rk_ref[...], seg_sum) * v
    o_ref[...] = (y + bonus) * gate


def _rwkv(hb, shift0, h0_bd, params, *, independent, n_streams, t, base_row):
    cb = RWKV_CB
    rows = cb * CHUNK
    nblk = t // rows
    base = base_row // rows
    masks = _rwkv_masks()
    full = lambda a: pl.BlockSpec(a.shape, lambda s, c: (0,) * a.ndim)
    if independent:
        assert n_streams == 1
        st_map = lambda s, c: (c, 0, 0)
        n_state, st_blk = t // CHUNK, cb
    else:
        st_map = lambda s, c: (s, 0, 0)
        n_state, st_blk = n_streams, 1
    cur = lambda w: pl.BlockSpec((rows, w), lambda s, c: (base + s * nblk + c, 0))
    return pl.pallas_call(
        functools.partial(_rwkv_kernel, cb=cb, independent=independent),
        grid=(n_streams, nblk),
        in_specs=[cur(B_PROJ),
                  pl.BlockSpec((st_blk, 1, B_PROJ), st_map),
                  pl.BlockSpec((st_blk, MIX_W, MIX_W), st_map),
                  full(masks)] + [full(p) for p in params],
        out_specs=[pl.BlockSpec((rows, MIX_W), lambda s, c: (s * nblk + c, 0)),
                   pl.BlockSpec((st_blk, MIX_W, MIX_W), st_map)],
        out_shape=[jax.ShapeDtypeStruct((n_streams * t, MIX_W), F32),
                   jax.ShapeDtypeStruct((n_state, MIX_W, MIX_W), F32)],
        scratch_shapes=[pltpu.VMEM((MIX_W, MIX_W), F32), pltpu.VMEM((1, B_PROJ), F32)],
        compiler_params=_cparams(("parallel", "arbitrary")),
        name="rwkv7",
    )(hb, shift0, h0_bd, masks, *params)


def _to_block_diag(s):
    eye = jnp.eye(N_HEADS, dtype=s.dtype)
    out = s[:, :, :, None, :] * eye[None, :, None, :, None]
    return out.reshape(s.shape[0], MIX_W, MIX_W)


def _from_block_diag(m):
    b = m.reshape(m.shape[0], N_HEADS, HEAD_DIM, N_HEADS, HEAD_DIM)
    return jnp.stack([b[:, h, :, h, :] for h in range(N_HEADS)], axis=1)


def _row(p):
    return p.reshape(1, -1).astype(F32)


def _mixers(proj, caches, lp, tabs, geom):
    aq, ak, av, hb, hc, dq, dk, dv = proj
    bp, tp, bs, ts = geom
    n_p = bp * tp
    ca_k, ca_v, sb_shift, sb_wkv, sc, cd_k, cd_v = caches
    pr = dict(n_streams=bp, t=tp, base_row=0)
    sm = dict(n_streams=bs, t=ts, base_row=n_p)
    zeros_state = jnp.zeros((bp, MIX_W, MIX_W), F32)

    oa_p = _attention(aq, ak, av, None, lp["sink_col"], n_prev=A_PREV_CHUNKS, use_sink=True, **pr)
    oa_s = _attention(aq, ak, av, (ca_k.reshape(bs, -1, A_KV_W), ca_v.reshape(bs, -1, A_KV_W)), lp["sink_col"],
                      n_prev=A_PREV_CHUNKS, use_sink=True, **sm)

    ob_p, h_p = _rwkv(hb, jnp.zeros((bp, 1, B_PROJ), F32), zeros_state, lp["rwkv"], independent=False, **pr)
    ob_s, h_s = _rwkv(hb, sb_shift.reshape(bs, 1, B_PROJ), _to_block_diag(sb_wkv), lp["rwkv"], independent=True,
                      n_streams=1, t=bs * ts, base_row=n_p)

    oc_p, s_p = _retention(hc, *tabs["rope_prompt"], zeros_state, tabs["ret"], lp["c_ln_w"], lp["c_ln_b"], **pr)
    oc_s, s_s = _retention(hc, *tabs["rope_sample"], _to_block_diag(sc), tabs["ret"], lp["c_ln_w"], lp["c_ln_b"], **sm)

    od_p = _attention(dq, dk, dv, None, lp["bias_table"], n_prev=D_PREV_CHUNKS, use_sink=False, **pr)
    od_s = _attention(dq, dk, dv, (cd_k.reshape(bs, -1, MIX_W), cd_v.reshape(bs, -1, MIX_W)), lp["bias_table"],
                      n_prev=D_PREV_CHUNKS, use_sink=False, **sm)

    mix = ((oa_p, oa_s), (ob_p, ob_s), (oc_p, oc_s), (od_p, od_s))
    return mix, (_from_block_diag(h_p), _from_block_diag(s_p)), (_from_block_diag(h_s), _from_block_diag(s_s))


def kernel(x_prompt, x_sample, cache_a_k, cache_a_v, state_b_shift, state_b_wkv, state_c, cache_d_k, cache_d_v,
           norm1_g, norm2_g, w_in, w_out, a_q_norm, a_k_norm, a_sinks, b_mu, b_w0, b_w2, b_a0, b_a2, b_g2,
           b_k_k, b_k_a, b_r_k, b_ln_w, b_ln_b, c_ln_w, c_ln_b, d_q_norm, d_k_norm, d_rel_bias,
           ffn_w1, ffn_w3, ffn_w2, moe_router, moe_w1, moe_w3, moe_w2):
    bp, tp, _ = x_prompt.shape
    bs, ts, _ = x_sample.shape
    assert ts == CHUNK
    n_p, n_s = bp * tp, bs * ts
    geom = (bp, tp, bs, ts)
    xa = x_prompt.reshape(n_p, D_MODEL)
    xb = x_sample.reshape(n_s, D_MODEL)

    tabs = {
        "ret": _retention_tables(),
        "rope_prompt": _rope_tables(np.arange(tp)),
        "rope_sample": _rope_tables(PAST_LEN + np.arange(ts)),
    }
    tile = lambda g: _row(jnp.tile(g, MIX_W // HEAD_DIM))

    p_states, s_states = [], []
    for l in range(DEPTH):
        lp = {
            "sink_col": jnp.repeat(a_sinks[l].astype(F32), CHUNK).reshape(N_HEADS * CHUNK, 1),
            "bias_table": _relbias_table(d_rel_bias[l].astype(F32), (D_PREV_CHUNKS + 1) * CHUNK),
            "rwkv": (_row(b_mu[l]), _row(b_w0[l]), b_w2[l], _row(b_a0[l]), b_a2[l], b_g2[l], _row(b_k_k[l]),
                     _row(b_k_a[l]), _row(b_r_k[l]), _row(b_ln_w[l]), _row(b_ln_b[l])),
            "c_ln_w": _row(c_ln_w[l]), "c_ln_b": _row(c_ln_b[l]),
        }
        proj = _inproj(xa, xb, n_p + n_s, _row(norm1_g[l]), w_in[l].astype(BF16), tile(a_q_norm[l]),
                       _row(jnp.tile(a_k_norm[l], A_KV_W // HEAD_DIM)), tile(d_q_norm[l]), tile(d_k_norm[l]))
        _, ak, av, hb, _, _, dk, dv = proj
        caches = (cache_a_k[l], cache_a_v[l], state_b_shift[l], state_b_wkv[l], state_c[l], cache_d_k[l], cache_d_v[l])
        mix, (wkv_p, ret_p), (wkv_s, ret_s) = _mixers(proj, caches, lp, tabs, geom)
        j = l // 2
        if l % 2 == 0:
            x1, xn2 = _outproj(xa, xb, n_p + n_s, mix, w_out[l].astype(BF16), _row(norm2_g[l]))
            xa = xb = _ffn(xn2, x1, ffn_w1[j].astype(BF16), ffn_w3[j].astype(BF16), ffn_w2[j].astype(BF16))
        else:
            router_pad = jnp.pad(moe_router[j].astype(F32), ((0, 0), (0, ROUTER_LANES - N_EXPERTS)))
            x1, xn_a, xn_b, *routing = _outproj(xa, xb, n_p + n_s, mix, w_out[l].astype(BF16), _row(norm2_g[l]),
                                                router_pad)
            groups = ((0, n_p), (n_p, n_s)) if l == DEPTH - 1 else ((0, n_p + n_s),)
            outs = _moe(xn_a, xn_b, x1, routing,
                        moe_w1[j].astype(BF16), moe_w3[j].astype(BF16), moe_w2[j].astype(BF16), groups)
            xa, xb = (outs[0], outs[-1])

        wa = min(A_PREV_CHUNKS * CHUNK, tp)
        wd = min(D_PREV_CHUNKS * CHUNK, tp)
        tail = lambda a, w, heads: jnp.stack(
            [a[(b + 1) * tp - w:(b + 1) * tp] for b in range(bp)]).reshape(bp, w, heads, HEAD_DIM)
        p_states.append((tail(ak, wa, 2), tail(av, wa, 2), hb[tp - 1:n_p:tp], wkv_p, ret_p,
                         tail(dk, wd, N_HEADS), tail(dv, wd, N_HEADS)))
        roll_in = lambda cache, new, heads: jnp.concatenate(
            [cache.astype(F32), new[n_p:].reshape(bs, ts, heads, HEAD_DIM)], axis=1)[:, -cache.shape[1]:]
        s_states.append((roll_in(cache_a_k[l], ak, 2), roll_in(cache_a_v[l], av, 2),
                         hb[n_p + ts - 1::ts], wkv_s, ret_s,
                         roll_in(cache_d_k[l], dk, N_HEADS), roll_in(cache_d_v[l], dv, N_HEADS)))

    if xa is xb:
        xa, xb = xa[:n_p], xa[n_p:]
    yp = xa.reshape(bp, tp, D_MODEL)
    ys = xb.reshape(bs, ts, D_MODEL)
    st = lambda group, i: jnp.stack([g[i] for g in group], axis=0)
    return (yp, ys,
            st(p_states, 0), st(p_states, 1), st(p_states, 2), st(p_states, 3), st(p_states, 4), st(p_states, 5), st(p_states, 6),
            st(s_states, 0), st(s_states, 1), st(s_states, 2), st(s_states, 3), st(s_states, 4), st(s_states, 5), st(s_states, 6))
```

```python
import functools

import jax
import jax.numpy as jnp
import numpy as np
from jax import lax
from jax.experimental import pallas as pl
from jax.experimental.pallas import tpu as pltpu
from jax.experimental.pallas import tpu_sc as plsc

F32 = jnp.float32
BF16 = jnp.bfloat16

D_MODEL = 1024
DEPTH = 2
PAST_LEN = 4096
CHUNK = 64
HEAD_DIM = 64
N_HEADS = 4
MIX_W = N_HEADS * HEAD_DIM
A_KV_W = 128
A_PREV_CHUNKS = 2
D_PREV_CHUNKS = 8
D_REL_CLIP = 128
B_PROJ = 1024
C_PROJ = 1024
IN_PROJ = 3328
B_GN_EPS = 64e-5
C_GN_EPS = 1e-6
NORM_EPS = 1e-6
ATTN_SCALE = 0.125
ROPE_BASE = 10000.0
D_FF = 2816
N_EXPERTS = 8
E_FF = 3584
NEG_BIG = -1e30

VMEM_LIMIT = 56 * 1024 * 1024

NN = ((1,), (0,))
NT = ((1,), (1,))
TN = ((0,), (0,))


def _dg(a, b, dims=NN):
    return lax.dot_general(a, b, (dims, ((), ())), preferred_element_type=F32)


def _parts(x, n):
    out = []
    r = x
    for i in range(n):
        p = r.astype(BF16)
        out.append(p)
        if i + 1 < n:
            r = r - p.astype(F32)
    return out


def _mm(a, b, dims=NN, passes=1):
    if passes == 1:
        return _dg(a.astype(BF16), b.astype(BF16), dims)
    ah, al = _parts(a, 2)
    bh, bl = _parts(b, 2)
    return _dg(ah, bh, dims) + (_dg(ah, bl, dims) + _dg(al, bh, dims))


def _mm_exact_rhs(a, b_bf, dims=NN, n=2):
    acc = None
    for p in _parts(a, n):
        t = _dg(p, b_bf, dims)
        acc = t if acc is None else acc + t
    return acc


def _iota2(shape, dim):
    return lax.broadcasted_iota(jnp.int32, shape, dim)


def _head_mask(rows, cols=MIX_W):
    return (_iota2((rows, cols), 0) >> 6) == (_iota2((rows, cols), 1) >> 6)


def _seg_matrix(width, value):
    m = _head_mask(width, width)
    return jnp.where(m, value, 0.0).astype(BF16)


def _tile4(z):
    return jnp.concatenate([z, z, z, z], axis=0)


def _fold4(z):
    return (z[0:64] + z[64:128]) + (z[128:192] + z[192:256])


def _sigmoid(x):
    return 1.0 / (1.0 + jnp.exp(-x))


def _cparams(sem):
    return pltpu.CompilerParams(dimension_semantics=sem, vmem_limit_bytes=VMEM_LIMIT)


IN_TM = 512


def _two_source_specs(xa, xb, n):
    na, nb = xa.shape[0] // IN_TM, xb.shape[0] // IN_TM
    spec_a = pl.BlockSpec((IN_TM, D_MODEL), lambda i: (jnp.minimum(i, na - 1), 0))
    spec_b = pl.BlockSpec((IN_TM, D_MODEL), lambda i: (jnp.clip(i - na, 0, nb - 1), 0))
    return na, n // IN_TM, spec_a, spec_b


def _inproj_kernel(xa_ref, xb_ref, g_ref, w_ref, aqg_ref, akg_ref, dqg_ref, dkg_ref,
                   aq_ref, ak_ref, av_ref, hb_ref, hc_ref, dq_ref, dk_ref, dv_ref, *, n_first):
    x = jnp.where(pl.program_id(0) < n_first, xa_ref[...], xb_ref[...])
    ms = jnp.mean(x * x, axis=-1, keepdims=True)
    xn = ((x * lax.rsqrt(ms + NORM_EPS)) * g_ref[...]).astype(BF16)
    seg = _seg_matrix(MIX_W, 1.0 / HEAD_DIM)

    def proj(lo, hi):
        return jnp.dot(xn, w_ref[:, lo:hi], preferred_element_type=F32)

    def head_rms(h, gain_ref):
        w = h.shape[-1]
        msq = _mm_exact_rhs(h * h, seg[:w, :w], n=1)
        return (h * lax.rsqrt(msq + NORM_EPS)) * gain_ref[...]

    aq_ref[...] = head_rms(proj(0, 256), aqg_ref)
    ak_ref[...] = head_rms(proj(256, 384), akg_ref)
    av_ref[...] = proj(384, 512)
    hb_ref[...] = proj(512, 1536)
    hc_ref[...] = proj(1536, 2560)
    dq_ref[...] = head_rms(proj(2560, 2816), dqg_ref)
    dk_ref[...] = head_rms(proj(2816, 3072), dkg_ref)
    dv_ref[...] = proj(3072, 3328)


def _inproj(xa, xb, n, g, w_bf, aqg, akg, dqg, dkg):
    na, nblk, spec_a, spec_b = _two_source_specs(xa, xb, n)
    widths = (256, 128, 128, B_PROJ, C_PROJ, 256, 256, 256)
    row = lambda w: pl.BlockSpec((IN_TM, w), lambda i: (i, 0))
    full = lambda a: pl.BlockSpec(a.shape, lambda i: (0,) * a.ndim)
    return pl.pallas_call(
        functools.partial(_inproj_kernel, n_first=na),
        grid=(nblk,),
        in_specs=[spec_a, spec_b, full(g), full(w_bf), full(aqg), full(akg), full(dqg), full(dkg)],
        out_specs=[row(w) for w in widths],
        out_shape=[jax.ShapeDtypeStruct((n, w), F32) for w in widths],
        compiler_params=_cparams(("parallel",)),
        name="inproj",
    )(xa, xb, g, w_bf, aqg, akg, dqg, dkg)


PACK_W = 256


def _pack_bf16_pairs(hi, lo):
    bits = lambda z: pltpu.bitcast(z.astype(BF16).astype(F32), jnp.int32)
    return bits(hi) | lax.shift_right_logical(bits(lo), jnp.full(lo.shape, 16, jnp.int32))


def _unpack_bf16_pairs(w):
    hi = pltpu.bitcast(w & jnp.int32(-65536), F32)
    lo = pltpu.bitcast(lax.shift_left(w, jnp.full(w.shape, 16, jnp.int32)), F32)
    return hi, lo


def _pack_rows(x):
    return (_pack_bf16_pairs(x[:, 0:PACK_W], x[:, PACK_W:2 * PACK_W]),
            _pack_bf16_pairs(x[:, 2 * PACK_W:3 * PACK_W], x[:, 3 * PACK_W:4 * PACK_W]))


def _unpack_rows(wa, wb):
    return jnp.concatenate(_unpack_bf16_pairs(wa) + _unpack_bf16_pairs(wb), axis=1)


ROUTER_LANES = 128


ROUTER_TERMS = 3


def _router_lanes(router):
    terms = _parts(router.astype(F32), ROUTER_TERMS)
    return jnp.pad(jnp.concatenate(terms, axis=1), ((0, 0), (0, ROUTER_LANES - ROUTER_TERMS * N_EXPERTS)))


def _route(xn_bf, router_bf, counts):
    split = _dg(xn_bf, router_bf)
    logits = split
    for k in range(1, ROUTER_TERMS):
        logits = logits + pltpu.roll(split, ROUTER_LANES - k * N_EXPERTS, 1)
    lane = _iota2(logits.shape, 1)
    logits = jnp.where(lane < N_EXPERTS, logits, NEG_BIG)
    m1 = jnp.max(logits, axis=-1, keepdims=True)
    i1 = jnp.min(jnp.where(logits == m1, lane, ROUTER_LANES), axis=-1, keepdims=True)
    rest = jnp.where(lane == i1, NEG_BIG, logits)
    m2 = jnp.max(rest, axis=-1, keepdims=True)
    i2 = jnp.min(jnp.where(rest == m2, lane, ROUTER_LANES), axis=-1, keepdims=True)
    e2 = jnp.exp(m2 - m1)
    den = 1.0 + e2
    gates = jnp.where(lane == i1, 1.0 / den, 0.0) + jnp.where(lane == i2, e2 / den, 0.0)
    sel = jnp.where(gates > 0.0, 1.0, 0.0)
    tm = sel.shape[0]
    before = jnp.where(_iota2((tm, tm), 1) < _iota2((tm, tm), 0), 1.0, 0.0).astype(BF16)
    rank = (_dg(before, sel.astype(BF16)) + counts).astype(jnp.int32)
    return gates, rank, counts + jnp.sum(sel, axis=0, keepdims=True)


def _mixed_residual(i, xa_ref, xb_ref, mix_refs, w_ref, g_ref, n_first_x, n_first_mix):
    x1 = jnp.where(i < n_first_x, xa_ref[...], xb_ref[...])
    for m in range(4):
        o = jnp.where(i < n_first_mix, mix_refs[2 * m][...], mix_refs[2 * m + 1][...])
        x1 = x1 + jnp.dot(o.astype(BF16), w_ref[m * MIX_W:(m + 1) * MIX_W, :], preferred_element_type=F32)
    ms = jnp.mean(x1 * x1, axis=-1, keepdims=True)
    return x1, (x1 * lax.rsqrt(ms + NORM_EPS)) * g_ref[...]


def _outproj_route_kernel(xa_ref, xb_ref, *refs, n_first_x, n_first_mix):
    mix_refs, (w_ref, g_ref, r_ref), outs = refs[:8], refs[8:11], refs[11:]
    x1_ref, pa_ref, pb_ref, gate_ref, rank_ref, cnt_ref, cnt_scr = outs
    i = pl.program_id(0)
    acc, xn = _mixed_residual(i, xa_ref, xb_ref, mix_refs, w_ref, g_ref, n_first_x, n_first_mix)

    @pl.when(i == 0)
    def _():
        cnt_scr[...] = jnp.zeros_like(cnt_scr)

    x1_ref[...] = acc
    pa_ref[...], pb_ref[...] = _pack_rows(xn)
    gates, rank, counts = _route(xn.astype(BF16), r_ref[...], cnt_scr[...])
    gate_ref[...] = gates
    rank_ref[...] = rank
    cnt_scr[...] = counts
    cnt_ref[...] = counts.astype(jnp.int32)


def _residual_specs(xa, xb, n, mix, index):
    na, nb = xa.shape[0] // IN_TM, xb.shape[0] // IN_TM
    nm_p, nm_s = mix[0][0].shape[0] // IN_TM, mix[0][1].shape[0] // IN_TM
    first = lambda w, cnt: pl.BlockSpec((IN_TM, w), index(lambda i: jnp.minimum(i, cnt - 1)))
    second = lambda w, skip, cnt: pl.BlockSpec((IN_TM, w), index(lambda i: jnp.clip(i - skip, 0, cnt - 1)))
    specs = [first(D_MODEL, na), second(D_MODEL, na, nb)] + [first(MIX_W, nm_p), second(MIX_W, nm_p, nm_s)] * 4
    return specs, [xa, xb, *[a for pair in mix for a in pair]], dict(n_first_x=na, n_first_mix=nm_p)


def _outproj_route(xa, xb, n, mix, w_bf, g2, router_bf):
    specs, operands, statics = _residual_specs(xa, xb, n, mix, lambda blk: (lambda i: (blk(i), 0)))
    row = lambda w: pl.BlockSpec((IN_TM, w), lambda i: (i, 0))
    full = lambda a: pl.BlockSpec(a.shape, lambda i: (0,) * a.ndim)
    return pl.pallas_call(
        functools.partial(_outproj_route_kernel, **statics),
        grid=(n // IN_TM,),
        in_specs=specs + [full(w_bf), full(g2), full(router_bf)],
        out_specs=[row(D_MODEL), row(PACK_W), row(PACK_W), row(ROUTER_LANES), row(ROUTER_LANES),
                   pl.BlockSpec((1, ROUTER_LANES), lambda i: (0, 0))],
        out_shape=[jax.ShapeDtypeStruct((n, D_MODEL), F32),
                   jax.ShapeDtypeStruct((n, PACK_W), jnp.int32), jax.ShapeDtypeStruct((n, PACK_W), jnp.int32),
                   jax.ShapeDtypeStruct((n, ROUTER_LANES), F32), jax.ShapeDtypeStruct((n, ROUTER_LANES), jnp.int32),
                   jax.ShapeDtypeStruct((1, ROUTER_LANES), jnp.int32)],
        scratch_shapes=[pltpu.VMEM((1, ROUTER_LANES), F32)],
        compiler_params=_cparams(("arbitrary",)),
        name="outproj_route",
    )(*operands, w_bf, g2, router_bf)


FFN_TF = 1408


def _outproj_ffn_kernel(xa_ref, xb_ref, *refs, n_first_x, n_first_mix):
    mix_refs, rest = refs[:8], refs[8:]
    w_ref, g_ref, w1_ref, w3_ref, w2_ref, o_ref, x1_scr, xn_scr = rest
    i, f = pl.program_id(0), pl.program_id(1)

    @pl.when(f == 0)
    def _():
        x1, xn = _mixed_residual(i, xa_ref, xb_ref, mix_refs, w_ref, g_ref, n_first_x, n_first_mix)
        x1_scr[...] = x1
        xn_scr[...] = xn.astype(BF16)

    xn = xn_scr[...]
    a = jnp.dot(xn, w1_ref[...], preferred_element_type=F32)
    b = jnp.dot(xn, w3_ref[...], preferred_element_type=F32)
    h = ((a * _sigmoid(a)) * b).astype(BF16)
    y = jnp.dot(h, w2_ref[...], preferred_element_type=F32)

    @pl.when(f == 0)
    def _():
        o_ref[...] = x1_scr[...] + y

    @pl.when(f != 0)
    def _():
        o_ref[...] += y


def _outproj_ffn(xa, xb, n, mix, w_bf, g2, w1_bf, w3_bf, w2_bf):
    specs, operands, statics = _residual_specs(xa, xb, n, mix, lambda blk: (lambda i, f: (blk(i), 0)))
    full = lambda a: pl.BlockSpec(a.shape, lambda i, f: (0,) * a.ndim)
    return pl.pallas_call(
        functools.partial(_outproj_ffn_kernel, **statics),
        grid=(n // IN_TM, D_FF // FFN_TF),
        in_specs=specs + [full(w_bf), full(g2),
                          pl.BlockSpec((D_MODEL, FFN_TF), lambda i, f: (0, f)),
                          pl.BlockSpec((D_MODEL, FFN_TF), lambda i, f: (0, f)),
                          pl.BlockSpec((FFN_TF, D_MODEL), lambda i, f: (f, 0))],
        out_specs=pl.BlockSpec((IN_TM, D_MODEL), lambda i, f: (i, 0)),
        out_shape=jax.ShapeDtypeStruct((n, D_MODEL), F32),
        scratch_shapes=[pltpu.VMEM((IN_TM, D_MODEL), F32), pltpu.VMEM((IN_TM, D_MODEL), BF16)],
        compiler_params=_cparams(("parallel", "arbitrary")),
        name="outproj_ffn",
    )(*operands, w_bf, g2, w1_bf, w3_bf, w2_bf)


MOE_R = 512
MOE_TF = 1792
SC_WINDOW = 128


def _moe_plan(gates, rank, counts, n):
    n_blocks = (2 * n) // MOE_R + N_EXPERTS + 1
    spare_row = (n_blocks - 1) * MOE_R
    sel = gates[:, :N_EXPERTS] > 0.0
    rank = rank[:, :N_EXPERTS]
    counts = counts[0, :N_EXPERTS]
    padded = ((counts + MOE_R - 1) // MOE_R) * MOE_R
    pad_end = jnp.cumsum(padded)
    pad_start = pad_end - padded
    pos = jnp.where(sel, pad_start[None, :] + rank, -1)
    order = jnp.cumsum(sel.astype(jnp.int32), axis=1)
    pick = lambda j: jnp.max(jnp.where(jnp.logical_and(sel, order == j), pos, -1), axis=1)
    to_row = lambda p: jnp.where(p >= 0, p, spare_row).astype(jnp.int32).reshape(1, n)
    block_expert = jnp.minimum(
        jnp.sum(pad_end[None, :] <= (jnp.arange(n_blocks) * MOE_R)[:, None], axis=1), N_EXPERTS - 1)
    return dict(n_blocks=n_blocks, pos0=to_row(pick(1)), pos1=to_row(pick(2)),
                block_expert=block_expert.astype(jnp.int32), n_used=(pad_end[-1:] // MOE_R).astype(jnp.int32))


def _sc_mesh():
    return plsc.VectorSubcoreMesh(core_axis_name="core", subcore_axis_name="subcore")


def _sc_scatter_rows(table, idx_lists, n_rows):
    n, cols = table.shape
    k = len(idx_lists)

    @functools.partial(pl.kernel, out_type=jax.ShapeDtypeStruct((n_rows, cols), table.dtype), mesh=_sc_mesh())
    def scatter(x_hbm, *rest):
        i_hbms, o_hbm = rest[:k], rest[k]

        def body(x_vmem, *i_vmems):
            for i_vmem in i_vmems:
                pltpu.sync_copy(x_vmem, o_hbm.at[i_vmem.at[0]])

        pltpu.emit_pipeline(
            body,
            grid=(n // SC_WINDOW,),
            in_specs=[pl.BlockSpec((SC_WINDOW, cols), lambda i: (i, 0))]
            + [pl.BlockSpec((1, SC_WINDOW), lambda i: (0, i))] * k,
            out_specs=[],
            core_axis_name=("core", "subcore"),
            dimension_semantics=(pltpu.PARALLEL,),
        )(x_hbm, *i_hbms)

    return scatter(table, *idx_lists)


def _sc_gather_rows(table, idx):
    n = idx.shape[1]
    cols = table.shape[1]

    @functools.partial(pl.kernel, out_type=jax.ShapeDtypeStruct((n, cols), table.dtype), mesh=_sc_mesh())
    def gather(x_hbm, i_hbm, o_hbm):
        def body(i_vmem, o_vmem):
            pltpu.sync_copy(x_hbm.at[i_vmem.at[0]], o_vmem)

        pltpu.emit_pipeline(
            body,
            grid=(n // SC_WINDOW,),
            in_specs=[pl.BlockSpec((1, SC_WINDOW), lambda i: (0, i))],
            out_specs=[pl.BlockSpec((SC_WINDOW, cols), lambda i: (i, 0))],
            core_axis_name=("core", "subcore"),
            dimension_semantics=(pltpu.PARALLEL,),
        )(i_hbm, o_hbm)

    return gather(table, idx)


def _moe_expert_kernel(be_ref, nu_ref, xa_ref, xb_ref, gs_ref, w1_ref, w3_ref, w2_ref, oa_ref, ob_ref, acc_ref):
    j, f = pl.program_id(0), pl.program_id(1)
    used = j < nu_ref[0]

    @pl.when(used)
    def _():
        x = _unpack_rows(xa_ref[...], xb_ref[...]).astype(BF16)
        a = jnp.dot(x, w1_ref[...], preferred_element_type=F32)
        b = jnp.dot(x, w3_ref[...], preferred_element_type=F32)
        h = ((a * _sigmoid(a)) * b).astype(BF16)
        y = jnp.dot(h, w2_ref[...], preferred_element_type=F32)

        @pl.when(f == 0)
        def _():
            acc_ref[...] = y

        @pl.when(f != 0)
        def _():
            acc_ref[...] += y

    @pl.when(f == pl.num_programs(1) - 1)
    def _():
        lane = _iota2(gs_ref.shape, 1)
        g = jnp.sum(jnp.where(lane == be_ref[j], gs_ref[...], 0.0), axis=1, keepdims=True)
        oa_ref[...], ob_ref[...] = _pack_rows(jnp.where(used, acc_ref[...] * g, 0.0))


def _moe_experts(plan, xs_a, xs_b, gs, w1_bf, w3_bf, w2_bf):
    n_blocks = plan["n_blocks"]
    half = pl.BlockSpec((MOE_R, PACK_W), lambda j, f, be, nu: (j, 0))
    grid_spec = pltpu.PrefetchScalarGridSpec(
        num_scalar_prefetch=2,
        grid=(n_blocks, E_FF // MOE_TF),
        in_specs=[half, half,
                  pl.BlockSpec((MOE_R, ROUTER_LANES), lambda j, f, be, nu: (j, 0)),
                  pl.BlockSpec((None, D_MODEL, MOE_TF), lambda j, f, be, nu: (be[j], 0, f)),
                  pl.BlockSpec((None, D_MODEL, MOE_TF), lambda j, f, be, nu: (be[j], 0, f)),
                  pl.BlockSpec((None, MOE_TF, D_MODEL), lambda j, f, be, nu: (be[j], f, 0))],
        out_specs=[half, half],
        scratch_shapes=[pltpu.VMEM((MOE_R, D_MODEL), F32)])
    return pl.pallas_call(
        _moe_expert_kernel,
        grid_spec=grid_spec,
        out_shape=[jax.ShapeDtypeStruct((n_blocks * MOE_R, PACK_W), jnp.int32)] * 2,
        compiler_params=_cparams(("arbitrary", "arbitrary")),
        name="moe_experts",
    )(plan["block_expert"], plan["n_used"], xs_a, xs_b, gs, w1_bf, w3_bf, w2_bf)


def _moe_combine_kernel(x1_ref, a0_ref, b0_ref, a1_ref, b1_ref, o_ref):
    o_ref[...] = (x1_ref[...] + _unpack_rows(a0_ref[...], b0_ref[...])) + _unpack_rows(a1_ref[...], b1_ref[...])


def _moe_combine(x1, picked, row0, n_rows):
    base = row0 // IN_TM
    row = lambda w: pl.BlockSpec((IN_TM, w), lambda i: (base + i, 0))
    return pl.pallas_call(
        _moe_combine_kernel,
        grid=(n_rows // IN_TM,),
        in_specs=[row(D_MODEL)] + [row(PACK_W)] * 4,
        out_specs=pl.BlockSpec((IN_TM, D_MODEL), lambda i: (i, 0)),
        out_shape=jax.ShapeDtypeStruct((n_rows, D_MODEL), F32),
        compiler_params=_cparams(("parallel",)),
        name="moe_combine",
    )(x1, *picked)


def _moe(xn_a, xn_b, x1, routing, w1_bf, w3_bf, w2_bf, row_groups):
    gates, rank, counts = routing
    plan = _moe_plan(gates, rank, counts, x1.shape[0])
    n_rows = plan["n_blocks"] * MOE_R
    idx = (plan["pos0"], plan["pos1"])
    xs_a = _sc_scatter_rows(xn_a, idx, n_rows)
    xs_b = _sc_scatter_rows(xn_b, idx, n_rows)
    gs = _sc_scatter_rows(gates, idx, n_rows)
    os_a, os_b = _moe_experts(plan, xs_a, xs_b, gs, w1_bf, w3_bf, w2_bf)
    picked = [_sc_gather_rows(t, p) for p in idx for t in (os_a, os_b)]
    return [_moe_combine(x1, picked, row0, rows) for row0, rows in row_groups]


def _relbias_kernel(rb_ref, o_ref, *, nk):
    h = pl.program_id(0)
    q = _iota2((CHUNK, nk), 0)
    r = _iota2((CHUNK, nk), 1)
    idx = jnp.clip(q - (r - (nk - CHUNK)), -D_REL_CLIP, D_REL_CLIP) + D_REL_CLIP

    def body(j, acc):
        return jnp.where(idx == j, rb_ref[h, j], acc)

    o_ref[...] = lax.fori_loop(0, 2 * D_REL_CLIP + 1, body, jnp.zeros((CHUNK, nk), F32))


def _relbias_table(rel_bias, nk):
    return pl.pallas_call(
        functools.partial(_relbias_kernel, nk=nk),
        grid=(N_HEADS,),
        in_specs=[pl.BlockSpec(memory_space=pltpu.SMEM)],
        out_specs=pl.BlockSpec((CHUNK, nk), lambda h: (h, 0)),
        out_shape=jax.ShapeDtypeStruct((N_HEADS * CHUNK, nk), F32),
        name="relbias",
    )(rel_bias)


ATTN_QB = 8
ATTN_GROUP = 8


def _attn_kernel(q_ref, kp_ref, kc_ref, vp_ref, vc_ref, x_ref, *rest, qb, n_prev, use_sink, mask_first):
    o_ref, kbuf, vbuf = rest[-3:]
    i = pl.program_id(1)
    p_rows = kp_ref.shape[0]
    nk = (n_prev + 1) * CHUNK
    wk = kp_ref.shape[1]

    kbuf[0:p_rows, :] = kp_ref[...].astype(BF16)
    kbuf[p_rows:, :] = kc_ref[...].astype(BF16)
    vbuf[0:p_rows, :] = vp_ref[...].astype(BF16)
    vbuf[p_rows:, :] = vc_ref[...].astype(BF16)

    hmask = _head_mask(N_HEADS * CHUNK)
    extra = x_ref[...]
    grouped = wk != MIX_W
    low = _iota2((CHUNK, A_KV_W), 1) < HEAD_DIM

    def stack_queries(qj):
        if not grouped:
            return jnp.where(hmask, _tile4(qj), 0.0)
        shifted = pltpu.roll(qj, MIX_W - HEAD_DIM, 1)[:, :A_KV_W]
        return jnp.concatenate([jnp.where(low, qj[:, :A_KV_W], 0.0), jnp.where(low, shifted, 0.0),
                                jnp.where(low, 0.0, shifted), jnp.where(low, 0.0, qj[:, A_KV_W:])], axis=0)

    def unstack_outputs(o_all):
        if not grouped:
            return _fold4(jnp.where(hmask, o_all, 0.0))
        b0, b1, b2, b3 = (o_all[h * CHUNK:(h + 1) * CHUNK] for h in range(N_HEADS))
        left = jnp.where(low, b0, 0.0) + pltpu.roll(jnp.where(low, b1, 0.0), HEAD_DIM, 1)
        right = pltpu.roll(jnp.where(low, 0.0, b2), HEAD_DIM, 1) + jnp.where(low, 0.0, b3)
        return jnp.concatenate([left, right], axis=1)

    def scores(j):
        base = p_rows + (j - n_prev) * CHUNK
        qs = stack_queries(q_ref[pl.ds(j * CHUNK, CHUNK), :] * ATTN_SCALE).astype(BF16)
        s = _dg(qs, kbuf[pl.ds(base, nk), :], NT)
        if not use_sink:
            s = s + extra
        if mask_first and base < p_rows:
            krow = base + _iota2(s.shape, 1)
            s = jnp.where(jnp.logical_and(i == 0, krow < p_rows), NEG_BIG, s)
        return s

    def weights(s):
        m = jnp.max(s, axis=-1, keepdims=True)
        if use_sink:
            m = jnp.maximum(m, extra)
        e = jnp.exp(s - m)
        den = jnp.sum(e, axis=-1, keepdims=True)
        if use_sink:
            den = den + jnp.exp(extra - m)
        return e.astype(BF16), 1.0 / den

    def output(j, e, inv_den):
        base = p_rows + (j - n_prev) * CHUNK
        o_all = _dg(e, vbuf[pl.ds(base, nk), :]) * inv_den
        o_ref[pl.ds(j * CHUNK, CHUNK), :] = unstack_outputs(o_all)

    for j0 in range(0, qb, ATTN_GROUP):
        group = range(j0, min(j0 + ATTN_GROUP, qb))
        ss = [scores(j) for j in group]
        ws = [weights(s) for s in ss]
        for j, (e, inv_den) in zip(group, ws):
            output(j, e, inv_den)


def _attention(q, k, v, prev, extra, *, n_prev, use_sink, n_streams, t, base_row):
    wk = k.shape[-1]
    if prev is None:
        qb = ATTN_QB
        rows = qb * CHUNK
        nblk = t // rows
        base = base_row // rows
        prev_spec = pl.BlockSpec((rows, wk), lambda s, i: (base + s * nblk + jnp.maximum(i - 1, 0), 0))
        k_prev, v_prev, p_rows, mask_first = k, v, rows, True
    else:
        qb, rows, nblk = t // CHUNK, t, 1
        base = base_row // rows
        k_prev, v_prev = prev
        p_rows = k_prev.shape[1]
        prev_spec = pl.BlockSpec((None, p_rows, wk), lambda s, i: (s, 0, 0))
        mask_first = False
    cur = lambda w: pl.BlockSpec((rows, w), lambda s, i: (base + s * nblk + i, 0))
    kern = functools.partial(_attn_kernel, qb=qb, n_prev=n_prev, use_sink=use_sink, mask_first=mask_first)
    return pl.pallas_call(
        kern,
        grid=(n_streams, nblk),
        in_specs=[cur(MIX_W), prev_spec, cur(wk), prev_spec, cur(wk),
                  pl.BlockSpec(extra.shape, lambda s, i: (0, 0))],
        out_specs=pl.BlockSpec((rows, MIX_W), lambda s, i: (s * nblk + i, 0)),
        out_shape=jax.ShapeDtypeStruct((n_streams * t, MIX_W), F32),
        scratch_shapes=[pltpu.VMEM((p_rows + rows, wk), BF16), pltpu.VMEM((p_rows + rows, wk), BF16)],
        compiler_params=_cparams(("parallel", "arbitrary")),
        name="attn_sink" if use_sink else "attn_bias",
    )(q, k_prev, k, v_prev, v, extra)


def _head_layer_norm(o, seg_mean_bf, w, b, eps):
    mu = _mm_exact_rhs(o, seg_mean_bf)
    d = o - mu
    var = _mm_exact_rhs(d * d, seg_mean_bf)
    return (d * lax.rsqrt(var + eps)) * w + b


def _ret_kernel(hc_ref, cos_ref, sin_ref, s0_ref, dstack_ref, qsc_ref, ksc_ref, gam_ref, lnw_ref, lnb_ref,
                *rest, qb):
    o_ref, sout_ref, s_scr = rest[-3:]
    i = pl.program_id(1)

    @pl.when(i == 0)
    def _():
        s_scr[...] = s0_ref[...]

    hmask = _head_mask(N_HEADS * CHUNK)
    seg_mean = _seg_matrix(MIX_W, 1.0 / HEAD_DIM)
    rows = qb * CHUNK
    first_half = (_iota2((rows, MIX_W), 1) & (HEAD_DIM - 1)) < (HEAD_DIM // 2)
    cos = jnp.concatenate([cos_ref[...]] * (MIX_W // ROPE_W), axis=1)
    sin = jnp.concatenate([sin_ref[...]] * (MIX_W // ROPE_W), axis=1)

    def rope(x):
        partner = jnp.where(first_half, pltpu.roll(x, MIX_W - HEAD_DIM // 2, 1), pltpu.roll(x, HEAD_DIM // 2, 1))
        return x * cos + partner * sin

    q = rope(hc_ref[:, 0:256])
    k = rope(hc_ref[:, 256:512]) * ATTN_SCALE
    v_bf = hc_ref[:, 512:768].astype(BF16)
    state = s_scr[...]
    outs = []
    for j in range(qb):
        sl = slice(j * CHUNK, (j + 1) * CHUNK)
        qj, kj, vj = q[sl], k[sl], v_bf[sl]
        qs = jnp.where(hmask, _tile4(qj), 0.0).astype(BF16)
        sc = _dg(qs, kj.astype(BF16), NT) * dstack_ref[...]
        intra = _fold4(jnp.where(hmask, _dg(sc.astype(BF16), vj), 0.0))
        inter = _dg((qj * qsc_ref[...]).astype(BF16), state.astype(BF16))
        kv = _dg((kj * ksc_ref[...]).astype(BF16), vj, TN)
        state = gam_ref[...] * state + jnp.where(hmask, kv, 0.0)
        outs.append(intra + inter)
    s_scr[...] = state
    sout_ref[...] = state
    y = _head_layer_norm(jnp.concatenate(outs, axis=0), seg_mean, lnw_ref[...], lnb_ref[...], C_GN_EPS)
    g = hc_ref[:, 768:1024]
    o_ref[...] = y * (g * _sigmoid(g))


def _retention(hc, cos, sin, s0_bd, tabs, lnw, lnb, *, n_streams, t, base_row):
    qb = min(8, t // CHUNK)
    rows = qb * CHUNK
    nblk = t // rows
    base = base_row // rows
    dstack, qsc, ksc, gam = tabs
    full = lambda a: pl.BlockSpec(a.shape, lambda s, i: (0,) * a.ndim)
    cur = lambda w: pl.BlockSpec((rows, w), lambda s, i: (base + s * nblk + i, 0))
    state = pl.BlockSpec((None, MIX_W, MIX_W), lambda s, i: (s, 0, 0))
    return pl.pallas_call(
        functools.partial(_ret_kernel, qb=qb),
        grid=(n_streams, nblk),
        in_specs=[cur(C_PROJ),
                  pl.BlockSpec((rows, ROPE_W), lambda s, i: (i, 0)),
                  pl.BlockSpec((rows, ROPE_W), lambda s, i: (i, 0)),
                  state, full(dstack), full(qsc), full(ksc), full(gam), full(lnw), full(lnb)],
        out_specs=[pl.BlockSpec((rows, MIX_W), lambda s, i: (s * nblk + i, 0)), state],
        out_shape=[jax.ShapeDtypeStruct((n_streams * t, MIX_W), F32),
                   jax.ShapeDtypeStruct((n_streams, MIX_W, MIX_W), F32)],
        scratch_shapes=[pltpu.VMEM((MIX_W, MIX_W), F32)],
        compiler_params=_cparams(("parallel", "arbitrary")),
        name="retention",
    )(hc, cos, sin, s0_bd, dstack, qsc, ksc, gam, lnw, lnb)


def _retention_tables():
    gamma = 1.0 - 2.0 ** (-5.0 - np.arange(N_HEADS, dtype=np.float64))
    t = np.arange(CHUNK)
    diff = t[:, None] - t[None, :]
    dmat = np.where(diff >= 0, gamma[:, None, None] ** np.maximum(diff, 0), 0.0)
    dstack = dmat.reshape(N_HEADS * CHUNK, CHUNK)
    lanes = lambda per_head: np.repeat(per_head, HEAD_DIM, axis=-1)
    qsc = lanes(gamma[None, :] ** (t + 1)[:, None])
    ksc = lanes(gamma[None, :] ** (CHUNK - 1 - t)[:, None])
    gam = np.broadcast_to(lanes(gamma ** CHUNK)[:, None], (MIX_W, MIX_W))
    return tuple(jnp.asarray(a, F32) for a in (dstack, qsc, ksc, gam))


ROPE_W = 2 * HEAD_DIM


def _rope_tables(pos):
    half = HEAD_DIM // 2
    theta = np.float32(1.0) / (np.float32(ROPE_BASE) ** np.linspace(0.0, 1.0, half, dtype=np.float32))
    ang = np.asarray(pos, np.float32)[:, None] * theta[None, :]
    cos, sin = np.cos(ang), np.sin(ang)
    reps = ROPE_W // HEAD_DIM
    cos_t = np.tile(np.concatenate([cos, cos], axis=-1), (1, reps))
    sin_t = np.tile(np.concatenate([-sin, sin], axis=-1), (1, reps))
    return jnp.asarray(cos_t, F32), jnp.asarray(sin_t, F32)


DECAY_SCALE = 0.6065306597126334
RWKV_CB = 4
N_LEVELS = 6
MASK_HEAD, MASK_STRICT, MASK_INCL, MASK_LEVEL0 = 0, 1, 2, 3


def _rwkv_masks():
    n4 = N_HEADS * CHUNK
    ri = np.arange(n4)[:, None]
    ci = np.arange(n4)[None, :]
    head = (ri >> 6) == (ci >> 6)
    tabs = [head, head & ((ci & 63) < (ri & 63)), head & ((ci & 63) <= (ri & 63))]
    for log_m in range(N_LEVELS):
        same = (ri >> (log_m + 1)) == (ci >> (log_m + 1))
        tabs.append(same & (((ri >> log_m) & 1) == 1) & (((ci >> log_m) & 1) == 0))
    return jnp.asarray(np.stack(tabs), BF16)


def _rwkv_kernel(hb_ref, shift0_ref, h0_ref, masks_ref, mu_ref, w0_ref, w2_ref, a0_ref, a2_ref, g2_ref,
                 kk_ref, ka_ref, rk_ref, lnw_ref, lnb_ref, *rest, cb, independent):
    o_ref, hout_ref, h_scr, shift_scr = rest[-4:]
    c = pl.program_id(1)
    rows = cb * CHUNK
    xb = hb_ref[...]
    row = _iota2(xb.shape, 0)
    prev = pltpu.roll(xb, 1, 0)
    if independent:
        for j in range(cb):
            prev = jnp.where(row == j * CHUNK, shift0_ref[j], prev)
    else:
        @pl.when(c == 0)
        def _():
            h_scr[...] = h0_ref[0]
            shift_scr[...] = shift0_ref[0]

        prev = jnp.where(row == 0, shift_scr[...], prev)
        shift_scr[...] = xb[rows - 1:rows, :]
    xs = xb + mu_ref[...] * (prev - xb)
    r = xs[:, 0:256]
    k = xs[:, 256:512]
    v = xs[:, 512:768]
    xw = xs[:, 768:832]
    xa = xs[:, 832:896]
    xg = xs[:, 896:1024]

    z = w0_ref[...] + _mm(jnp.tanh(xw), w2_ref[...], passes=3)
    lw = -DECAY_SCALE * _sigmoid(z)
    a_gate = _sigmoid(a0_ref[...] + _mm(xa, a2_ref[...], passes=3))
    gate = _mm(_sigmoid(xg), g2_ref[...], passes=1)

    seg_sum = _seg_matrix(MIX_W, 1.0)
    seg_mean = _seg_matrix(MIX_W, 1.0 / HEAD_DIM)
    kkn = k * kk_ref[...]
    norm = jnp.sqrt(_mm_exact_rhs(kkn * kkn, seg_sum))
    kk = kkn / jnp.maximum(norm, 1e-12)
    kf = k * (1.0 + (a_gate - 1.0) * ka_ref[...])

    tt = _iota2((rows, rows), 0)
    ss = _iota2((rows, rows), 1)
    tril = jnp.where(jnp.logical_and(ss <= tt, (ss >> 6) == (tt >> 6)), 1.0, 0.0).astype(BF16)
    lw_parts = _parts(lw, 3)
    cum = _dg(tril, lw_parts[0]) + (_dg(tril, lw_parts[1]) + _dg(tril, lw_parts[2]))
    w_inv = jnp.exp(-cum)
    rho = (r * jnp.exp(cum)).astype(BF16)
    alpha = (-kk * jnp.exp(cum - lw)).astype(BF16)
    beta = ((kk * a_gate) * w_inv).astype(BF16)
    kappa = (kf * w_inv).astype(BF16)
    v_bf = v.astype(BF16)

    hmask = masks_ref[MASK_HEAD]
    n4 = N_HEADS * CHUNK
    eye = jnp.where(_iota2((n4, n4), 0) == _iota2((n4, n4), 1), 1.0, 0.0)

    pre, a_bfs, t_invs = [], [], []
    for j in range(cb):
        sl = slice(j * CHUNK, (j + 1) * CHUNK)
        bd = lambda zz: _tile4(zz[sl]) * hmask
        al_bd, be_bd, ka_bd, rh_bd, v_bd = bd(alpha), bd(beta), bd(kappa), bd(rho), bd(v_bf)
        a_bf = _dg(al_bd, be_bd, NT).astype(BF16) * masks_ref[MASK_STRICT]
        a_ak = _dg(al_bd, ka_bd, NT).astype(BF16) * masks_ref[MASK_STRICT]
        b_rb = _dg(rh_bd, be_bd, NT).astype(BF16) * masks_ref[MASK_INCL]
        b_rk = _dg(rh_bd, ka_bd, NT).astype(BF16) * masks_ref[MASK_INCL]
        x0 = _dg(a_ak, v_bd)
        y0 = _dg(b_rk, v_bd)
        sn0 = _dg(v_bd, ka_bd, TN)
        w_chunk = jnp.exp(cum[(j + 1) * CHUNK - 1:(j + 1) * CHUNK, :])
        a_bfs.append(a_bf)
        t_invs.append(eye + (a_bf * masks_ref[MASK_LEVEL0]).astype(F32))
        pre.append([al_bd, be_bd, rh_bd, b_rb, None, x0, y0, sn0, w_chunk])
    for lvl in range(1, N_LEVELS):
        t_bfs = [t.astype(BF16) for t in t_invs]
        e_mats = [_dg(a_bfs[j] * masks_ref[MASK_LEVEL0 + lvl], t_bfs[j]) for j in range(cb)]
        t_invs = [t_invs[j] + _dg(t_bfs[j], e_mats[j].astype(BF16)) for j in range(cb)]
    for j in range(cb):
        pre[j][4] = t_invs[j].astype(BF16)

    ys = []
    h = None if independent else h_scr[...]
    for j in range(cb):
        al_bd, be_bd, rh_bd, b_rb, t_bf, x0, y0, sn0, w_chunk = pre[j]
        h0 = h0_ref[j] if independent else h
        h0_bf = h0.astype(BF16)
        x_mat = _dg(al_bd, h0_bf, NT) + x0
        u_bf = _dg(t_bf, x_mat.astype(BF16)).astype(BF16)
        y_bd = _dg(rh_bd, h0_bf, NT) + _dg(b_rb, u_bf) + y0
        h_new = (h0 + _dg(u_bf, be_bd, TN) + sn0) * w_chunk
        ys.append(_fold4(y_bd))
        if independent:
            hout_ref[j] = h_new
        else:
            h = h_new
    if not independent:
        h_scr[...] = h
        hout_ref[0] = h

    y = _head_layer_norm(jnp.concatenate(ys, axis=0), seg_mean, lnw_ref[...], lnb_ref[...], B_GN_EPS)
    bonus = _mm_exact_rhs(r * kf * rk_ref[...], seg_sum) * v
    o_ref[...] = (y + bonus) * gate


def _rwkv(hb, shift0, h0_bd, params, *, independent, n_streams, t, base_row):
    cb = RWKV_CB
    rows = cb * CHUNK
    nblk = t // rows
    base = base_row // rows
    masks = _rwkv_masks()
    full = lambda a: pl.BlockSpec(a.shape, lambda s, c: (0,) * a.ndim)
    if independent:
        assert n_streams == 1
        st_map = lambda s, c: (c, 0, 0)
        n_state, st_blk = t // CHUNK, cb
    else:
        st_map = lambda s, c: (s, 0, 0)
        n_state, st_blk = n_streams, 1
    cur = lambda w: pl.BlockSpec((rows, w), lambda s, c: (base + s * nblk + c, 0))
    return pl.pallas_call(
        functools.partial(_rwkv_kernel, cb=cb, independent=independent),
        grid=(n_streams, nblk),
        in_specs=[cur(B_PROJ),
                  pl.BlockSpec((st_blk, 1, B_PROJ), st_map),
                  pl.BlockSpec((st_blk, MIX_W, MIX_W), st_map),
                  full(masks)] + [full(p) for p in params],
        out_specs=[pl.BlockSpec((rows, MIX_W), lambda s, c: (s * nblk + c, 0)),
                   pl.BlockSpec((st_blk, MIX_W, MIX_W), st_map)],
        out_shape=[jax.ShapeDtypeStruct((n_streams * t, MIX_W), F32),
                   jax.ShapeDtypeStruct((n_state, MIX_W, MIX_W), F32)],
        scratch_shapes=[pltpu.VMEM((MIX_W, MIX_W), F32), pltpu.VMEM((1, B_PROJ), F32)],
        compiler_params=_cparams(("parallel", "arbitrary")),
        name="rwkv7",
    )(hb, shift0, h0_bd, masks, *params)


def _to_block_diag(s):
    eye = jnp.eye(N_HEADS, dtype=s.dtype)
    out = s[:, :, :, None, :] * eye[None, :, None, :, None]
    return out.reshape(s.shape[0], MIX_W, MIX_W)


def _from_block_diag(m):
    b = m.reshape(m.shape[0], N_HEADS, HEAD_DIM, N_HEADS, HEAD_DIM)
    return jnp.stack([b[:, h, :, h, :] for h in range(N_HEADS)], axis=1)


def _row(p):
    return p.reshape(1, -1).astype(F32)


def _mixers(proj, caches, lp, tabs, geom):
    aq, ak, av, hb, hc, dq, dk, dv = proj
    bp, tp, bs, ts = geom
    n_p = bp * tp
    ca_k, ca_v, sb_shift, sb_wkv, sc, cd_k, cd_v = caches
    pr = dict(n_streams=bp, t=tp, base_row=0)
    sm = dict(n_streams=bs, t=ts, base_row=n_p)
    zeros_state = jnp.zeros((bp, MIX_W, MIX_W), F32)

    oa_p = _attention(aq, ak, av, None, lp["sink_col"], n_prev=A_PREV_CHUNKS, use_sink=True, **pr)
    oa_s = _attention(aq, ak, av, (ca_k.reshape(bs, -1, A_KV_W), ca_v.reshape(bs, -1, A_KV_W)), lp["sink_col"],
                      n_prev=A_PREV_CHUNKS, use_sink=True, **sm)

    ob_p, h_p = _rwkv(hb, jnp.zeros((bp, 1, B_PROJ), F32), zeros_state, lp["rwkv"], independent=False, **pr)
    ob_s, h_s = _rwkv(hb, sb_shift.reshape(bs, 1, B_PROJ), _to_block_diag(sb_wkv), lp["rwkv"], independent=True,
                      n_streams=1, t=bs * ts, base_row=n_p)

    oc_p, s_p = _retention(hc, *tabs["rope_prompt"], zeros_state, tabs["ret"], lp["c_ln_w"], lp["c_ln_b"], **pr)
    oc_s, s_s = _retention(hc, *tabs["rope_sample"], _to_block_diag(sc), tabs["ret"], lp["c_ln_w"], lp["c_ln_b"], **sm)

    od_p = _attention(dq, dk, dv, None, lp["bias_table"], n_prev=D_PREV_CHUNKS, use_sink=False, **pr)
    od_s = _attention(dq, dk, dv, (cd_k.reshape(bs, -1, MIX_W), cd_v.reshape(bs, -1, MIX_W)), lp["bias_table"],
                      n_prev=D_PREV_CHUNKS, use_sink=False, **sm)

    mix = ((oa_p, oa_s), (ob_p, ob_s), (oc_p, oc_s), (od_p, od_s))
    return mix, (_from_block_diag(h_p), _from_block_diag(s_p)), (_from_block_diag(h_s), _from_block_diag(s_s))


def kernel(x_prompt, x_sample, cache_a_k, cache_a_v, state_b_shift, state_b_wkv, state_c, cache_d_k, cache_d_v,
           norm1_g, norm2_g, w_in, w_out, a_q_norm, a_k_norm, a_sinks, b_mu, b_w0, b_w2, b_a0, b_a2, b_g2,
           b_k_k, b_k_a, b_r_k, b_ln_w, b_ln_b, c_ln_w, c_ln_b, d_q_norm, d_k_norm, d_rel_bias,
           ffn_w1, ffn_w3, ffn_w2, moe_router, moe_w1, moe_w3, moe_w2):
    bp, tp, _ = x_prompt.shape
    bs, ts, _ = x_sample.shape
    assert ts == CHUNK
    n_p, n_s = bp * tp, bs * ts
    geom = (bp, tp, bs, ts)
    xa = x_prompt.reshape(n_p, D_MODEL)
    xb = x_sample.reshape(n_s, D_MODEL)

    tabs = {
        "ret": _retention_tables(),
        "rope_prompt": _rope_tables(np.arange(tp)),
        "rope_sample": _rope_tables(PAST_LEN + np.arange(ts)),
    }
    tile = lambda g: _row(jnp.tile(g, MIX_W // HEAD_DIM))

    p_states, s_states = [], []
    for l in range(DEPTH):
        lp = {
            "sink_col": jnp.repeat(a_sinks[l].astype(F32), CHUNK).reshape(N_HEADS * CHUNK, 1),
            "bias_table": _relbias_table(d_rel_bias[l].astype(F32), (D_PREV_CHUNKS + 1) * CHUNK),
            "rwkv": (_row(b_mu[l]), _row(b_w0[l]), b_w2[l], _row(b_a0[l]), b_a2[l], b_g2[l], _row(b_k_k[l]),
                     _row(b_k_a[l]), _row(b_r_k[l]), _row(b_ln_w[l]), _row(b_ln_b[l])),
            "c_ln_w": _row(c_ln_w[l]), "c_ln_b": _row(c_ln_b[l]),
        }
        proj = _inproj(xa, xb, n_p + n_s, _row(norm1_g[l]), w_in[l].astype(BF16), tile(a_q_norm[l]),
                       _row(jnp.tile(a_k_norm[l], A_KV_W // HEAD_DIM)), tile(d_q_norm[l]), tile(d_k_norm[l]))
        _, ak, av, hb, _, _, dk, dv = proj
        caches = (cache_a_k[l], cache_a_v[l], state_b_shift[l], state_b_wkv[l], state_c[l], cache_d_k[l], cache_d_v[l])
        mix, (wkv_p, ret_p), (wkv_s, ret_s) = _mixers(proj, caches, lp, tabs, geom)
        j = l // 2
        if l % 2 == 0:
            xa = xb = _outproj_ffn(xa, xb, n_p + n_s, mix, w_out[l].astype(BF16), _row(norm2_g[l]),
                                   ffn_w1[j].astype(BF16), ffn_w3[j].astype(BF16), ffn_w2[j].astype(BF16))
        else:
            x1, xn_a, xn_b, *routing = _outproj_route(xa, xb, n_p + n_s, mix, w_out[l].astype(BF16),
                                                      _row(norm2_g[l]), _router_lanes(moe_router[j]))
            groups = ((0, n_p), (n_p, n_s)) if l == DEPTH - 1 else ((0, n_p + n_s),)
            outs = _moe(xn_a, xn_b, x1, routing,
                        moe_w1[j].astype(BF16), moe_w3[j].astype(BF16), moe_w2[j].astype(BF16), groups)
            xa, xb = (outs[0], outs[-1])

        wa = min(A_PREV_CHUNKS * CHUNK, tp)
        wd = min(D_PREV_CHUNKS * CHUNK, tp)
        tail = lambda a, w, heads: jnp.stack(
            [a[(b + 1) * tp - w:(b + 1) * tp] for b in range(bp)]).reshape(bp, w, heads, HEAD_DIM)
        p_states.append((tail(ak, wa, 2), tail(av, wa, 2), hb[tp - 1:n_p:tp], wkv_p, ret_p,
                         tail(dk, wd, N_HEADS), tail(dv, wd, N_HEADS)))
        roll_in = lambda cache, new, heads: jnp.concatenate(
            [cache.astype(F32), new[n_p:].reshape(bs, ts, heads, HEAD_DIM)], axis=1)[:, -cache.shape[1]:]
        s_states.append((roll_in(cache_a_k[l], ak, 2), roll_in(cache_a_v[l], av, 2),
                         hb[n_p + ts - 1::ts], wkv_s, ret_s,
                         roll_in(cache_d_k[l], dk, N_HEADS), roll_in(cache_d_v[l], dv, N_HEADS)))

    if xa is xb:
        xa, xb = xa[:n_p], xa[n_p:]
    yp = xa.reshape(bp, tp, D_MODEL)
    ys = xb.reshape(bs, ts, D_MODEL)
    st = lambda group, i: jnp.stack([g[i] for g in group], axis=0)
    return (yp, ys,
            st(p_states, 0), st(p_states, 1), st(p_states, 2), st(p_states, 3), st(p_states, 4), st(p_states, 5), st(p_states, 6),
            st(s_states, 0), st(s_states, 1), st(s_states, 2), st(s_states, 3), st(s_states, 4), st(s_states, 5), st(s_states, 6))
```

```python
import functools

import jax
import jax.numpy as jnp
import numpy as np
from jax import lax
from jax.experimental import pallas as pl
from jax.experimental.pallas import tpu as pltpu
from jax.experimental.pallas import tpu_sc as plsc

F32 = jnp.float32
BF16 = jnp.bfloat16

D_MODEL = 1024
DEPTH = 2
PAST_LEN = 4096
CHUNK = 64
HEAD_DIM = 64
N_HEADS = 4
MIX_W = N_HEADS * HEAD_DIM
A_KV_W = 128
A_PREV_CHUNKS = 2
D_PREV_CHUNKS = 8
D_REL_CLIP = 128
B_PROJ = 1024
C_PROJ = 1024
IN_PROJ = 3328
B_GN_EPS = 64e-5
C_GN_EPS = 1e-6
NORM_EPS = 1e-6
ATTN_SCALE = 0.125
ROPE_BASE = 10000.0
D_FF = 2816
N_EXPERTS = 8
E_FF = 3584
NEG_BIG = -1e30

VMEM_LIMIT = 56 * 1024 * 1024

NN = ((1,), (0,))
NT = ((1,), (1,))
TN = ((0,), (0,))


def _dg(a, b, dims=NN):
    return lax.dot_general(a, b, (dims, ((), ())), preferred_element_type=F32)


def _parts(x, n):
    out = []
    r = x
    for i in range(n):
        p = r.astype(BF16)
        out.append(p)
        if i + 1 < n:
            r = r - p.astype(F32)
    return out


def _mm(a, b, dims=NN, passes=1):
    if passes == 1:
        return _dg(a.astype(BF16), b.astype(BF16), dims)
    ah, al = _parts(a, 2)
    bh, bl = _parts(b, 2)
    return _dg(ah, bh, dims) + (_dg(ah, bl, dims) + _dg(al, bh, dims))


def _mm_exact_rhs(a, b_bf, dims=NN, n=2):
    acc = None
    for p in _parts(a, n):
        t = _dg(p, b_bf, dims)
        acc = t if acc is None else acc + t
    return acc


def _iota2(shape, dim):
    return lax.broadcasted_iota(jnp.int32, shape, dim)


def _head_mask(rows, cols=MIX_W):
    return (_iota2((rows, cols), 0) >> 6) == (_iota2((rows, cols), 1) >> 6)


def _seg_matrix(width, value):
    m = _head_mask(width, width)
    return jnp.where(m, value, 0.0).astype(BF16)


def _tile4(z):
    return jnp.concatenate([z, z, z, z], axis=0)


def _fold4(z):
    return (z[0:64] + z[64:128]) + (z[128:192] + z[192:256])


def _sigmoid(x):
    return 1.0 / (1.0 + jnp.exp(-x))


def _cparams(sem):
    return pltpu.CompilerParams(dimension_semantics=sem, vmem_limit_bytes=VMEM_LIMIT)


IN_TM = 512


def _two_source_specs(xa, xb, n):
    na, nb = xa.shape[0] // IN_TM, xb.shape[0] // IN_TM
    spec_a = pl.BlockSpec((IN_TM, D_MODEL), lambda i: (jnp.minimum(i, na - 1), 0))
    spec_b = pl.BlockSpec((IN_TM, D_MODEL), lambda i: (jnp.clip(i - na, 0, nb - 1), 0))
    return na, n // IN_TM, spec_a, spec_b


def _inproj_kernel(xa_ref, xb_ref, g_ref, w_ref, aqg_ref, akg_ref, dqg_ref, dkg_ref,
                   aq_ref, ak_ref, av_ref, hb_ref, hc_ref, dq_ref, dk_ref, dv_ref, *, n_first):
    x = jnp.where(pl.program_id(0) < n_first, xa_ref[...], xb_ref[...])
    ms = jnp.mean(x * x, axis=-1, keepdims=True)
    xn = ((x * lax.rsqrt(ms + NORM_EPS)) * g_ref[...]).astype(BF16)
    seg = _seg_matrix(MIX_W, 1.0 / HEAD_DIM)

    def proj(lo, hi):
        return jnp.dot(xn, w_ref[:, lo:hi], preferred_element_type=F32)

    def head_rms(h, gain_ref):
        w = h.shape[-1]
        msq = _mm_exact_rhs(h * h, seg[:w, :w], n=1)
        return (h * lax.rsqrt(msq + NORM_EPS)) * gain_ref[...]

    aq_ref[...] = head_rms(proj(0, 256), aqg_ref)
    ak_ref[...] = head_rms(proj(256, 384), akg_ref)
    av_ref[...] = proj(384, 512)
    hb_ref[...] = proj(512, 1536)
    hc_ref[...] = proj(1536, 2560)
    dq_ref[...] = head_rms(proj(2560, 2816), dqg_ref)
    dk_ref[...] = head_rms(proj(2816, 3072), dkg_ref)
    dv_ref[...] = proj(3072, 3328)


def _inproj(xa, xb, n, g, w_bf, aqg, akg, dqg, dkg):
    na, nblk, spec_a, spec_b = _two_source_specs(xa, xb, n)
    widths = (256, 128, 128, B_PROJ, C_PROJ, 256, 256, 256)
    row = lambda w: pl.BlockSpec((IN_TM, w), lambda i: (i, 0))
    full = lambda a: pl.BlockSpec(a.shape, lambda i: (0,) * a.ndim)
    return pl.pallas_call(
        functools.partial(_inproj_kernel, n_first=na),
        grid=(nblk,),
        in_specs=[spec_a, spec_b, full(g), full(w_bf), full(aqg), full(akg), full(dqg), full(dkg)],
        out_specs=[row(w) for w in widths],
        out_shape=[jax.ShapeDtypeStruct((n, w), F32) for w in widths],
        compiler_params=_cparams(("parallel",)),
        name="inproj",
    )(xa, xb, g, w_bf, aqg, akg, dqg, dkg)


PACK_W = 256


def _pack_bf16_pairs(hi, lo):
    bits = lambda z: pltpu.bitcast(z.astype(BF16).astype(F32), jnp.int32)
    return bits(hi) | lax.shift_right_logical(bits(lo), jnp.full(lo.shape, 16, jnp.int32))


def _unpack_bf16_pairs(w):
    hi = pltpu.bitcast(w & jnp.int32(-65536), F32)
    lo = pltpu.bitcast(lax.shift_left(w, jnp.full(w.shape, 16, jnp.int32)), F32)
    return hi, lo


def _pack_rows(x):
    return (_pack_bf16_pairs(x[:, 0:PACK_W], x[:, PACK_W:2 * PACK_W]),
            _pack_bf16_pairs(x[:, 2 * PACK_W:3 * PACK_W], x[:, 3 * PACK_W:4 * PACK_W]))


def _unpack_rows(wa, wb):
    return jnp.concatenate(_unpack_bf16_pairs(wa) + _unpack_bf16_pairs(wb), axis=1)


ROUTER_LANES = 128


ROUTER_TERMS = 3


def _router_lanes(router):
    terms = _parts(router.astype(F32), ROUTER_TERMS)
    return jnp.pad(jnp.concatenate(terms, axis=1), ((0, 0), (0, ROUTER_LANES - ROUTER_TERMS * N_EXPERTS)))


def _route(xn_bf, router_bf, counts):
    split = _dg(xn_bf, router_bf)
    logits = split
    for k in range(1, ROUTER_TERMS):
        logits = logits + pltpu.roll(split, ROUTER_LANES - k * N_EXPERTS, 1)
    lane = _iota2(logits.shape, 1)
    logits = jnp.where(lane < N_EXPERTS, logits, NEG_BIG)
    m1 = jnp.max(logits, axis=-1, keepdims=True)
    i1 = jnp.min(jnp.where(logits == m1, lane, ROUTER_LANES), axis=-1, keepdims=True)
    rest = jnp.where(lane == i1, NEG_BIG, logits)
    m2 = jnp.max(rest, axis=-1, keepdims=True)
    i2 = jnp.min(jnp.where(rest == m2, lane, ROUTER_LANES), axis=-1, keepdims=True)
    e2 = jnp.exp(m2 - m1)
    den = 1.0 + e2
    gates = jnp.where(lane == i1, 1.0 / den, 0.0) + jnp.where(lane == i2, e2 / den, 0.0)
    sel = jnp.where(gates > 0.0, 1.0, 0.0)
    tm = sel.shape[0]
    before = jnp.where(_iota2((tm, tm), 1) < _iota2((tm, tm), 0), 1.0, 0.0).astype(BF16)
    rank = (_dg(before, sel.astype(BF16)) + counts).astype(jnp.int32)
    return gates, rank, counts + jnp.sum(sel, axis=0, keepdims=True)


def _mixed_residual(i, xa_ref, xb_ref, mix_refs, w_ref, g_ref, n_first_x, n_first_mix):
    x1 = jnp.where(i < n_first_x, xa_ref[...], xb_ref[...])
    for m in range(4):
        o = jnp.where(i < n_first_mix, mix_refs[2 * m][...], mix_refs[2 * m + 1][...])
        x1 = x1 + jnp.dot(o.astype(BF16), w_ref[m * MIX_W:(m + 1) * MIX_W, :], preferred_element_type=F32)
    ms = jnp.mean(x1 * x1, axis=-1, keepdims=True)
    return x1, (x1 * lax.rsqrt(ms + NORM_EPS)) * g_ref[...]


def _outproj_route_kernel(xa_ref, xb_ref, *refs, n_first_x, n_first_mix):
    mix_refs, (w_ref, g_ref, r_ref), outs = refs[:8], refs[8:11], refs[11:]
    x1_ref, pa_ref, pb_ref, gate_ref, rank_ref, cnt_ref, cnt_scr = outs
    i = pl.program_id(0)
    acc, xn = _mixed_residual(i, xa_ref, xb_ref, mix_refs, w_ref, g_ref, n_first_x, n_first_mix)

    @pl.when(i == 0)
    def _():
        cnt_scr[...] = jnp.zeros_like(cnt_scr)

    x1_ref[...] = acc
    pa_ref[...], pb_ref[...] = _pack_rows(xn)
    gates, rank, counts = _route(xn.astype(BF16), r_ref[...], cnt_scr[...])
    gate_ref[...] = gates
    rank_ref[...] = rank
    cnt_scr[...] = counts
    cnt_ref[...] = counts.astype(jnp.int32)


def _residual_specs(xa, xb, n, mix, index):
    na, nb = xa.shape[0] // IN_TM, xb.shape[0] // IN_TM
    nm_p, nm_s = mix[0][0].shape[0] // IN_TM, mix[0][1].shape[0] // IN_TM
    first = lambda w, cnt: pl.BlockSpec((IN_TM, w), index(lambda i: jnp.minimum(i, cnt - 1)))
    second = lambda w, skip, cnt: pl.BlockSpec((IN_TM, w), index(lambda i: jnp.clip(i - skip, 0, cnt - 1)))
    specs = [first(D_MODEL, na), second(D_MODEL, na, nb)] + [first(MIX_W, nm_p), second(MIX_W, nm_p, nm_s)] * 4
    return specs, [xa, xb, *[a for pair in mix for a in pair]], dict(n_first_x=na, n_first_mix=nm_p)


def _outproj_route(xa, xb, n, mix, w_bf, g2, router_bf):
    specs, operands, statics = _residual_specs(xa, xb, n, mix, lambda blk: (lambda i: (blk(i), 0)))
    row = lambda w: pl.BlockSpec((IN_TM, w), lambda i: (i, 0))
    full = lambda a: pl.BlockSpec(a.shape, lambda i: (0,) * a.ndim)
    return pl.pallas_call(
        functools.partial(_outproj_route_kernel, **statics),
        grid=(n // IN_TM,),
        in_specs=specs + [full(w_bf), full(g2), full(router_bf)],
        out_specs=[row(D_MODEL), row(PACK_W), row(PACK_W), row(ROUTER_LANES), row(ROUTER_LANES),
                   pl.BlockSpec((1, ROUTER_LANES), lambda i: (0, 0))],
        out_shape=[jax.ShapeDtypeStruct((n, D_MODEL), F32),
                   jax.ShapeDtypeStruct((n, PACK_W), jnp.int32), jax.ShapeDtypeStruct((n, PACK_W), jnp.int32),
                   jax.ShapeDtypeStruct((n, ROUTER_LANES), F32), jax.ShapeDtypeStruct((n, ROUTER_LANES), jnp.int32),
                   jax.ShapeDtypeStruct((1, ROUTER_LANES), jnp.int32)],
        scratch_shapes=[pltpu.VMEM((1, ROUTER_LANES), F32)],
        compiler_params=_cparams(("arbitrary",)),
        name="outproj_route",
    )(*operands, w_bf, g2, router_bf)


FFN_TF = 1408


def _outproj_ffn_kernel(xa_ref, xb_ref, *refs, n_first_x, n_first_mix):
    mix_refs, rest = refs[:8], refs[8:]
    w_ref, g_ref, w1_ref, w3_ref, w2_ref, o_ref, x1_scr, xn_scr = rest
    i, f = pl.program_id(0), pl.program_id(1)

    @pl.when(f == 0)
    def _():
        x1, xn = _mixed_residual(i, xa_ref, xb_ref, mix_refs, w_ref, g_ref, n_first_x, n_first_mix)
        x1_scr[...] = x1
        xn_scr[...] = xn.astype(BF16)

    xn = xn_scr[...]
    a = jnp.dot(xn, w1_ref[...], preferred_element_type=F32)
    b = jnp.dot(xn, w3_ref[...], preferred_element_type=F32)
    h = ((a * _sigmoid(a)) * b).astype(BF16)
    y = jnp.dot(h, w2_ref[...], preferred_element_type=F32)

    @pl.when(f == 0)
    def _():
        o_ref[...] = x1_scr[...] + y

    @pl.when(f != 0)
    def _():
        o_ref[...] += y


def _outproj_ffn(xa, xb, n, mix, w_bf, g2, w1_bf, w3_bf, w2_bf):
    specs, operands, statics = _residual_specs(xa, xb, n, mix, lambda blk: (lambda i, f: (blk(i), 0)))
    full = lambda a: pl.BlockSpec(a.shape, lambda i, f: (0,) * a.ndim)
    return pl.pallas_call(
        functools.partial(_outproj_ffn_kernel, **statics),
        grid=(n // IN_TM, D_FF // FFN_TF),
        in_specs=specs + [full(w_bf), full(g2),
                          pl.BlockSpec((D_MODEL, FFN_TF), lambda i, f: (0, f)),
                          pl.BlockSpec((D_MODEL, FFN_TF), lambda i, f: (0, f)),
                          pl.BlockSpec((FFN_TF, D_MODEL), lambda i, f: (f, 0))],
        out_specs=pl.BlockSpec((IN_TM, D_MODEL), lambda i, f: (i, 0)),
        out_shape=jax.ShapeDtypeStruct((n, D_MODEL), F32),
        scratch_shapes=[pltpu.VMEM((IN_TM, D_MODEL), F32), pltpu.VMEM((IN_TM, D_MODEL), BF16)],
        compiler_params=_cparams(("parallel", "arbitrary")),
        name="outproj_ffn",
    )(*operands, w_bf, g2, w1_bf, w3_bf, w2_bf)


MOE_R = 512
MOE_TF = 1792
SC_WINDOW = 128


def _moe_plan(gates, rank, counts, n):
    n_blocks = (2 * n) // MOE_R + N_EXPERTS + 1
    spare_row = (n_blocks - 1) * MOE_R
    sel = gates[:, :N_EXPERTS] > 0.0
    rank = rank[:, :N_EXPERTS]
    counts = counts[0, :N_EXPERTS]
    padded = ((counts + MOE_R - 1) // MOE_R) * MOE_R
    pad_end = jnp.cumsum(padded)
    pad_start = pad_end - padded
    pos = jnp.where(sel, pad_start[None, :] + rank, -1)
    order = jnp.cumsum(sel.astype(jnp.int32), axis=1)
    pick = lambda j: jnp.max(jnp.where(jnp.logical_and(sel, order == j), pos, -1), axis=1)
    to_row = lambda p: jnp.where(p >= 0, p, spare_row).astype(jnp.int32).reshape(1, n)
    block_expert = jnp.minimum(
        jnp.sum(pad_end[None, :] <= (jnp.arange(n_blocks) * MOE_R)[:, None], axis=1), N_EXPERTS - 1)
    return dict(n_blocks=n_blocks, pos0=to_row(pick(1)), pos1=to_row(pick(2)),
                block_expert=block_expert.astype(jnp.int32), n_used=(pad_end[-1:] // MOE_R).astype(jnp.int32))


def _sc_mesh():
    return plsc.VectorSubcoreMesh(core_axis_name="core", subcore_axis_name="subcore")


def _sc_scatter_rows(table, idx_lists, n_rows):
    n, cols = table.shape
    k = len(idx_lists)

    @functools.partial(pl.kernel, out_type=jax.ShapeDtypeStruct((n_rows, cols), table.dtype), mesh=_sc_mesh())
    def scatter(x_hbm, *rest):
        i_hbms, o_hbm = rest[:k], rest[k]

        def body(x_vmem, *i_vmems):
            for i_vmem in i_vmems:
                pltpu.sync_copy(x_vmem, o_hbm.at[i_vmem.at[0]])

        pltpu.emit_pipeline(
            body,
            grid=(n // SC_WINDOW,),
            in_specs=[pl.BlockSpec((SC_WINDOW, cols), lambda i: (i, 0))]
            + [pl.BlockSpec((1, SC_WINDOW), lambda i: (0, i))] * k,
            out_specs=[],
            core_axis_name=("core", "subcore"),
            dimension_semantics=(pltpu.PARALLEL,),
        )(x_hbm, *i_hbms)

    return scatter(table, *idx_lists)


def _sc_gather_rows(table, idx):
    n = idx.shape[1]
    cols = table.shape[1]

    @functools.partial(pl.kernel, out_type=jax.ShapeDtypeStruct((n, cols), table.dtype), mesh=_sc_mesh())
    def gather(x_hbm, i_hbm, o_hbm):
        def body(i_vmem, o_vmem):
            pltpu.sync_copy(x_hbm.at[i_vmem.at[0]], o_vmem)

        pltpu.emit_pipeline(
            body,
            grid=(n // SC_WINDOW,),
            in_specs=[pl.BlockSpec((1, SC_WINDOW), lambda i: (0, i))],
            out_specs=[pl.BlockSpec((SC_WINDOW, cols), lambda i: (i, 0))],
            core_axis_name=("core", "subcore"),
            dimension_semantics=(pltpu.PARALLEL,),
        )(i_hbm, o_hbm)

    return gather(table, idx)


def _moe_expert_kernel(be_ref, nu_ref, xa_ref, xb_ref, gs_ref, w1_ref, w3_ref, w2_ref, oa_ref, ob_ref, acc_ref):
    j, f = pl.program_id(0), pl.program_id(1)
    used = j < nu_ref[0]

    @pl.when(used)
    def _():
        x = _unpack_rows(xa_ref[...], xb_ref[...]).astype(BF16)
        a = jnp.dot(x, w1_ref[...], preferred_element_type=F32)
        b = jnp.dot(x, w3_ref[...], preferred_element_type=F32)
        h = ((a * _sigmoid(a)) * b).astype(BF16)
        y = jnp.dot(h, w2_ref[...], preferred_element_type=F32)

        @pl.when(f == 0)
        def _():
            acc_ref[...] = y

        @pl.when(f != 0)
        def _():
            acc_ref[...] += y

    @pl.when(f == pl.num_programs(1) - 1)
    def _():
        lane = _iota2(gs_ref.shape, 1)
        g = jnp.sum(jnp.where(lane == be_ref[j], gs_ref[...], 0.0), axis=1, keepdims=True)
        oa_ref[...], ob_ref[...] = _pack_rows(jnp.where(used, acc_ref[...] * g, 0.0))


def _moe_experts(plan, xs_a, xs_b, gs, w1_bf, w3_bf, w2_bf):
    n_blocks = plan["n_blocks"]
    half = pl.BlockSpec((MOE_R, PACK_W), lambda j, f, be, nu: (j, 0))
    grid_spec = pltpu.PrefetchScalarGridSpec(
        num_scalar_prefetch=2,
        grid=(n_blocks, E_FF // MOE_TF),
        in_specs=[half, half,
                  pl.BlockSpec((MOE_R, ROUTER_LANES), lambda j, f, be, nu: (j, 0)),
                  pl.BlockSpec((None, D_MODEL, MOE_TF), lambda j, f, be, nu: (be[j], 0, f)),
                  pl.BlockSpec((None, D_MODEL, MOE_TF), lambda j, f, be, nu: (be[j], 0, f)),
                  pl.BlockSpec((None, MOE_TF, D_MODEL), lambda j, f, be, nu: (be[j], f, 0))],
        out_specs=[half, half],
        scratch_shapes=[pltpu.VMEM((MOE_R, D_MODEL), F32)])
    return pl.pallas_call(
        _moe_expert_kernel,
        grid_spec=grid_spec,
        out_shape=[jax.ShapeDtypeStruct((n_blocks * MOE_R, PACK_W), jnp.int32)] * 2,
        compiler_params=_cparams(("arbitrary", "arbitrary")),
        name="moe_experts",
    )(plan["block_expert"], plan["n_used"], xs_a, xs_b, gs, w1_bf, w3_bf, w2_bf)


def _moe_combine_kernel(x1_ref, a0_ref, b0_ref, a1_ref, b1_ref, o_ref):
    o_ref[...] = (x1_ref[...] + _unpack_rows(a0_ref[...], b0_ref[...])) + _unpack_rows(a1_ref[...], b1_ref[...])


def _moe_combine(x1, picked, row0, n_rows):
    base = row0 // IN_TM
    row = lambda w: pl.BlockSpec((IN_TM, w), lambda i: (base + i, 0))
    return pl.pallas_call(
        _moe_combine_kernel,
        grid=(n_rows // IN_TM,),
        in_specs=[row(D_MODEL)] + [row(PACK_W)] * 4,
        out_specs=pl.BlockSpec((IN_TM, D_MODEL), lambda i: (i, 0)),
        out_shape=jax.ShapeDtypeStruct((n_rows, D_MODEL), F32),
        compiler_params=_cparams(("parallel",)),
        name="moe_combine",
    )(x1, *picked)


def _moe(xn_a, xn_b, x1, routing, w1_bf, w3_bf, w2_bf, row_groups):
    gates, rank, counts = routing
    plan = _moe_plan(gates, rank, counts, x1.shape[0])
    n_rows = plan["n_blocks"] * MOE_R
    idx = (plan["pos0"], plan["pos1"])
    xs_a = _sc_scatter_rows(xn_a, idx, n_rows)
    xs_b = _sc_scatter_rows(xn_b, idx, n_rows)
    gs = _sc_scatter_rows(gates, idx, n_rows)
    os_a, os_b = _moe_experts(plan, xs_a, xs_b, gs, w1_bf, w3_bf, w2_bf)
    picked = [_sc_gather_rows(t, p) for p in idx for t in (os_a, os_b)]
    return [_moe_combine(x1, picked, row0, rows) for row0, rows in row_groups]


def _relbias_kernel(rb_ref, o_ref, *, nk):
    h = pl.program_id(0)
    q = _iota2((CHUNK, nk), 0)
    r = _iota2((CHUNK, nk), 1)
    idx = jnp.clip(q - (r - (nk - CHUNK)), -D_REL_CLIP, D_REL_CLIP) + D_REL_CLIP

    def body(j, acc):
        return jnp.where(idx == j, rb_ref[h, j], acc)

    o_ref[...] = lax.fori_loop(0, 2 * D_REL_CLIP + 1, body, jnp.zeros((CHUNK, nk), F32))


def _relbias_table(rel_bias, nk):
    return pl.pallas_call(
        functools.partial(_relbias_kernel, nk=nk),
        grid=(N_HEADS,),
        in_specs=[pl.BlockSpec(memory_space=pltpu.SMEM)],
        out_specs=pl.BlockSpec((CHUNK, nk), lambda h: (h, 0)),
        out_shape=jax.ShapeDtypeStruct((N_HEADS * CHUNK, nk), F32),
        name="relbias",
    )(rel_bias)


ATTN_QB = 8
ATTN_GROUP = 8


def _attn_kernel(q_ref, kp_ref, kc_ref, vp_ref, vc_ref, x_ref, *rest, qb, n_prev, use_sink, mask_first):
    o_ref, kbuf, vbuf = rest[-3:]
    i = pl.program_id(1)
    p_rows = kp_ref.shape[0]
    nk = (n_prev + 1) * CHUNK
    wk = kp_ref.shape[1]

    kbuf[0:p_rows, :] = kp_ref[...].astype(BF16)
    kbuf[p_rows:, :] = kc_ref[...].astype(BF16)
    vbuf[0:p_rows, :] = vp_ref[...].astype(BF16)
    vbuf[p_rows:, :] = vc_ref[...].astype(BF16)

    hmask = _head_mask(N_HEADS * CHUNK)
    extra = x_ref[...]
    grouped = wk != MIX_W
    low = _iota2((CHUNK, A_KV_W), 1) < HEAD_DIM

    def stack_queries(qj):
        if not grouped:
            return jnp.where(hmask, _tile4(qj), 0.0)
        shifted = pltpu.roll(qj, MIX_W - HEAD_DIM, 1)[:, :A_KV_W]
        return jnp.concatenate([jnp.where(low, qj[:, :A_KV_W], 0.0), jnp.where(low, shifted, 0.0),
                                jnp.where(low, 0.0, shifted), jnp.where(low, 0.0, qj[:, A_KV_W:])], axis=0)

    def unstack_outputs(o_all):
        if not grouped:
            return _fold4(jnp.where(hmask, o_all, 0.0))
        b0, b1, b2, b3 = (o_all[h * CHUNK:(h + 1) * CHUNK] for h in range(N_HEADS))
        left = jnp.where(low, b0, 0.0) + pltpu.roll(jnp.where(low, b1, 0.0), HEAD_DIM, 1)
        right = pltpu.roll(jnp.where(low, 0.0, b2), HEAD_DIM, 1) + jnp.where(low, 0.0, b3)
        return jnp.concatenate([left, right], axis=1)

    def scores(j):
        base = p_rows + (j - n_prev) * CHUNK
        qs = stack_queries(q_ref[pl.ds(j * CHUNK, CHUNK), :] * ATTN_SCALE).astype(BF16)
        s = _dg(qs, kbuf[pl.ds(base, nk), :], NT)
        if not use_sink:
            s = s + extra
        if mask_first and base < p_rows:
            krow = base + _iota2(s.shape, 1)
            s = jnp.where(jnp.logical_and(i == 0, krow < p_rows), NEG_BIG, s)
        return s

    def weights(s):
        m = jnp.max(s, axis=-1, keepdims=True)
        if use_sink:
            m = jnp.maximum(m, extra)
        e = jnp.exp(s - m)
        den = jnp.sum(e, axis=-1, keepdims=True)
        if use_sink:
            den = den + jnp.exp(extra - m)
        return e.astype(BF16), 1.0 / den

    def output(j, e, inv_den):
        base = p_rows + (j - n_prev) * CHUNK
        o_all = _dg(e, vbuf[pl.ds(base, nk), :]) * inv_den
        o_ref[pl.ds(j * CHUNK, CHUNK), :] = unstack_outputs(o_all)

    for j0 in range(0, qb, ATTN_GROUP):
        group = range(j0, min(j0 + ATTN_GROUP, qb))
        ss = [scores(j) for j in group]
        ws = [weights(s) for s in ss]
        for j, (e, inv_den) in zip(group, ws):
            output(j, e, inv_den)


def _attention(q, k, v, prev, extra, *, n_prev, use_sink, n_streams, t, base_row):
    wk = k.shape[-1]
    if prev is None:
        qb = ATTN_QB
        rows = qb * CHUNK
        nblk = t // rows
        base = base_row // rows
        prev_spec = pl.BlockSpec((rows, wk), lambda s, i: (base + s * nblk + jnp.maximum(i - 1, 0), 0))
        k_prev, v_prev, p_rows, mask_first = k, v, rows, True
    else:
        qb, rows, nblk = t // CHUNK, t, 1
        base = base_row // rows
        k_prev, v_prev = prev
        p_rows = k_prev.shape[1]
        prev_spec = pl.BlockSpec((None, p_rows, wk), lambda s, i: (s, 0, 0))
        mask_first = False
    cur = lambda w: pl.BlockSpec((rows, w), lambda s, i: (base + s * nblk + i, 0))
    kern = functools.partial(_attn_kernel, qb=qb, n_prev=n_prev, use_sink=use_sink, mask_first=mask_first)
    return pl.pallas_call(
        kern,
        grid=(n_streams, nblk),
        in_specs=[cur(MIX_W), prev_spec, cur(wk), prev_spec, cur(wk),
                  pl.BlockSpec(extra.shape, lambda s, i: (0, 0))],
        out_specs=pl.BlockSpec((rows, MIX_W), lambda s, i: (s * nblk + i, 0)),
        out_shape=jax.ShapeDtypeStruct((n_streams * t, MIX_W), F32),
        scratch_shapes=[pltpu.VMEM((p_rows + rows, wk), BF16), pltpu.VMEM((p_rows + rows, wk), BF16)],
        compiler_params=_cparams(("parallel", "arbitrary")),
        name="attn_sink" if use_sink else "attn_bias",
    )(q, k_prev, k, v_prev, v, extra)


def _head_layer_norm(o, seg_mean_bf, w, b, eps):
    mu = _mm_exact_rhs(o, seg_mean_bf)
    d = o - mu
    var = _mm_exact_rhs(d * d, seg_mean_bf)
    return (d * lax.rsqrt(var + eps)) * w + b


def _ret_kernel(hc_ref, cos_ref, sin_ref, s0_ref, dstack_ref, qsc_ref, ksc_ref, gam_ref, lnw_ref, lnb_ref,
                *rest, qb):
    o_ref, sout_ref, s_scr = rest[-3:]
    i = pl.program_id(1)

    @pl.when(i == 0)
    def _():
        s_scr[...] = s0_ref[...]

    hmask = _head_mask(N_HEADS * CHUNK)
    seg_mean = _seg_matrix(MIX_W, 1.0 / HEAD_DIM)
    rows = qb * CHUNK
    first_half = (_iota2((rows, MIX_W), 1) & (HEAD_DIM - 1)) < (HEAD_DIM // 2)
    cos = jnp.concatenate([cos_ref[...]] * (MIX_W // ROPE_W), axis=1)
    sin = jnp.concatenate([sin_ref[...]] * (MIX_W // ROPE_W), axis=1)

    def rope(x):
        partner = jnp.where(first_half, pltpu.roll(x, MIX_W - HEAD_DIM // 2, 1), pltpu.roll(x, HEAD_DIM // 2, 1))
        return x * cos + partner * sin

    q = rope(hc_ref[:, 0:256])
    k = rope(hc_ref[:, 256:512]) * ATTN_SCALE
    v_bf = hc_ref[:, 512:768].astype(BF16)
    state = s_scr[...]
    outs = []
    for j in range(qb):
        sl = slice(j * CHUNK, (j + 1) * CHUNK)
        qj, kj, vj = q[sl], k[sl], v_bf[sl]
        qs = jnp.where(hmask, _tile4(qj), 0.0).astype(BF16)
        sc = _dg(qs, kj.astype(BF16), NT) * dstack_ref[...]
        intra = _fold4(jnp.where(hmask, _dg(sc.astype(BF16), vj), 0.0))
        inter = _dg((qj * qsc_ref[...]).astype(BF16), state.astype(BF16))
        kv = _dg((kj * ksc_ref[...]).astype(BF16), vj, TN)
        state = gam_ref[...] * state + jnp.where(hmask, kv, 0.0)
        outs.append(intra + inter)
    s_scr[...] = state
    sout_ref[...] = state
    y = _head_layer_norm(jnp.concatenate(outs, axis=0), seg_mean, lnw_ref[...], lnb_ref[...], C_GN_EPS)
    g = hc_ref[:, 768:1024]
    o_ref[...] = y * (g * _sigmoid(g))


def _retention(hc, cos, sin, s0_bd, tabs, lnw, lnb, *, n_streams, t, base_row):
    qb = min(8, t // CHUNK)
    rows = qb * CHUNK
    nblk = t // rows
    base = base_row // rows
    dstack, qsc, ksc, gam = tabs
    full = lambda a: pl.BlockSpec(a.shape, lambda s, i: (0,) * a.ndim)
    cur = lambda w: pl.BlockSpec((rows, w), lambda s, i: (base + s * nblk + i, 0))
    state = pl.BlockSpec((None, MIX_W, MIX_W), lambda s, i: (s, 0, 0))
    return pl.pallas_call(
        functools.partial(_ret_kernel, qb=qb),
        grid=(n_streams, nblk),
        in_specs=[cur(C_PROJ),
                  pl.BlockSpec((rows, ROPE_W), lambda s, i: (i, 0)),
                  pl.BlockSpec((rows, ROPE_W), lambda s, i: (i, 0)),
                  state, full(dstack), full(qsc), full(ksc), full(gam), full(lnw), full(lnb)],
        out_specs=[pl.BlockSpec((rows, MIX_W), lambda s, i: (s * nblk + i, 0)), state],
        out_shape=[jax.ShapeDtypeStruct((n_streams * t, MIX_W), F32),
                   jax.ShapeDtypeStruct((n_streams, MIX_W, MIX_W), F32)],
        scratch_shapes=[pltpu.VMEM((MIX_W, MIX_W), F32)],
        compiler_params=_cparams(("parallel", "arbitrary")),
        name="retention",
    )(hc, cos, sin, s0_bd, dstack, qsc, ksc, gam, lnw, lnb)


def _retention_tables():
    gamma = 1.0 - 2.0 ** (-5.0 - np.arange(N_HEADS, dtype=np.float64))
    t = np.arange(CHUNK)
    diff = t[:, None] - t[None, :]
    dmat = np.where(diff >= 0, gamma[:, None, None] ** np.maximum(diff, 0), 0.0)
    dstack = dmat.reshape(N_HEADS * CHUNK, CHUNK)
    lanes = lambda per_head: np.repeat(per_head, HEAD_DIM, axis=-1)
    qsc = lanes(gamma[None, :] ** (t + 1)[:, None])
    ksc = lanes(gamma[None, :] ** (CHUNK - 1 - t)[:, None])
    gam = np.broadcast_to(lanes(gamma ** CHUNK)[:, None], (MIX_W, MIX_W))
    return tuple(jnp.asarray(a, F32) for a in (dstack, qsc, ksc, gam))


ROPE_W = 2 * HEAD_DIM


def _rope_tables(pos):
    half = HEAD_DIM // 2
    theta = np.float32(1.0) / (np.float32(ROPE_BASE) ** np.linspace(0.0, 1.0, half, dtype=np.float32))
    ang = np.asarray(pos, np.float32)[:, None] * theta[None, :]
    cos, sin = np.cos(ang), np.sin(ang)
    reps = ROPE_W // HEAD_DIM
    cos_t = np.tile(np.concatenate([cos, cos], axis=-1), (1, reps))
    sin_t = np.tile(np.concatenate([-sin, sin], axis=-1), (1, reps))
    return jnp.asarray(cos_t, F32), jnp.asarray(sin_t, F32)


DECAY_SCALE = 0.6065306597126334
RWKV_CB = 4
N_LEVELS = 6
MASK_HEAD, MASK_STRICT, MASK_INCL, MASK_LEVEL0 = 0, 1, 2, 3


def _rwkv_masks():
    n4 = N_HEADS * CHUNK
    ri = np.arange(n4)[:, None]
    ci = np.arange(n4)[None, :]
    head = (ri >> 6) == (ci >> 6)
    tabs = [head, head & ((ci & 63) < (ri & 63)), head & ((ci & 63) <= (ri & 63))]
    for log_m in range(N_LEVELS):
        same = (ri >> (log_m + 1)) == (ci >> (log_m + 1))
        tabs.append(same & (((ri >> log_m) & 1) == 1) & (((ci >> log_m) & 1) == 0))
    return jnp.asarray(np.stack(tabs), BF16)


def _rwkv_kernel(hb_ref, shift0_ref, h0_ref, masks_ref, mu_ref, w0_ref, w2_ref, a0_ref, a2_ref, g2_ref,
                 kk_ref, ka_ref, rk_ref, lnw_ref, lnb_ref, *rest, cb, independent):
    o_ref, hout_ref, h_scr, shift_scr = rest[-4:]
    c = pl.program_id(1)
    rows = cb * CHUNK
    xb = hb_ref[...]
    row = _iota2(xb.shape, 0)
    prev = pltpu.roll(xb, 1, 0)
    if independent:
        for j in range(cb):
            prev = jnp.where(row == j * CHUNK, shift0_ref[j], prev)
    else:
        @pl.when(c == 0)
        def _():
            h_scr[...] = h0_ref[0]
            shift_scr[...] = shift0_ref[0]

        prev = jnp.where(row == 0, shift_scr[...], prev)
        shift_scr[...] = xb[rows - 1:rows, :]
    xs = xb + mu_ref[...] * (prev - xb)
    r = xs[:, 0:256]
    k = xs[:, 256:512]
    v = xs[:, 512:768]
    xw = xs[:, 768:832]
    xa = xs[:, 832:896]
    xg = xs[:, 896:1024]

    z = w0_ref[...] + _mm(jnp.tanh(xw), w2_ref[...], passes=3)
    lw = -DECAY_SCALE * _sigmoid(z)
    a_gate = _sigmoid(a0_ref[...] + _mm(xa, a2_ref[...], passes=3))
    gate = _mm(_sigmoid(xg), g2_ref[...], passes=1)

    seg_sum = _seg_matrix(MIX_W, 1.0)
    seg_mean = _seg_matrix(MIX_W, 1.0 / HEAD_DIM)
    kkn = k * kk_ref[...]
    norm = jnp.sqrt(_mm_exact_rhs(kkn * kkn, seg_sum))
    kk = kkn / jnp.maximum(norm, 1e-12)
    kf = k * (1.0 + (a_gate - 1.0) * ka_ref[...])

    tt = _iota2((rows, rows), 0)
    ss = _iota2((rows, rows), 1)
    tril = jnp.where(jnp.logical_and(ss <= tt, (ss >> 6) == (tt >> 6)), 1.0, 0.0).astype(BF16)
    lw_parts = _parts(lw, 3)
    cum = _dg(tril, lw_parts[0]) + (_dg(tril, lw_parts[1]) + _dg(tril, lw_parts[2]))
    w_inv = jnp.exp(-cum)
    rho = (r * jnp.exp(cum)).astype(BF16)
    alpha = (-kk * jnp.exp(cum - lw)).astype(BF16)
    beta = ((kk * a_gate) * w_inv).astype(BF16)
    kappa = (kf * w_inv).astype(BF16)
    v_bf = v.astype(BF16)

    hmask = masks_ref[MASK_HEAD]
    n4 = N_HEADS * CHUNK
    eye = jnp.where(_iota2((n4, n4), 0) == _iota2((n4, n4), 1), 1.0, 0.0)

    pre, a_bfs, t_invs = [], [], []
    for j in range(cb):
        sl = slice(j * CHUNK, (j + 1) * CHUNK)
        bd = lambda zz: _tile4(zz[sl]) * hmask
        al_bd, be_bd, ka_bd, rh_bd, v_bd = bd(alpha), bd(beta), bd(kappa), bd(rho), bd(v_bf)
        a_bf = _dg(al_bd, be_bd, NT).astype(BF16) * masks_ref[MASK_STRICT]
        a_ak = _dg(al_bd, ka_bd, NT).astype(BF16) * masks_ref[MASK_STRICT]
        b_rb = _dg(rh_bd, be_bd, NT).astype(BF16) * masks_ref[MASK_INCL]
        b_rk = _dg(rh_bd, ka_bd, NT).astype(BF16) * masks_ref[MASK_INCL]
        x0 = _dg(a_ak, v_bd)
        y0 = _dg(b_rk, v_bd)
        sn0 = _dg(v_bd, ka_bd, TN)
        w_chunk = jnp.exp(cum[(j + 1) * CHUNK - 1:(j + 1) * CHUNK, :])
        a_bfs.append(a_bf)
        t_invs.append(eye + (a_bf * masks_ref[MASK_LEVEL0]).astype(F32))
        pre.append([al_bd, be_bd, rh_bd, b_rb, None, x0, y0, sn0, w_chunk])
    for lvl in range(1, N_LEVELS):
        t_bfs = [t.astype(BF16) for t in t_invs]
        e_mats = [_dg(a_bfs[j] * masks_ref[MASK_LEVEL0 + lvl], t_bfs[j]) for j in range(cb)]
        t_invs = [t_invs[j] + _dg(t_bfs[j], e_mats[j].astype(BF16)) for j in range(cb)]
    for j in range(cb):
        pre[j][4] = t_invs[j].astype(BF16)

    ys = []
    h = None if independent else h_scr[...]
    for j in range(cb):
        al_bd, be_bd, rh_bd, b_rb, t_bf, x0, y0, sn0, w_chunk = pre[j]
        h0 = h0_ref[j] if independent else h
        h0_bf = h0.astype(BF16)
        x_mat = _dg(al_bd, h0_bf, NT) + x0
        u_bf = _dg(t_bf, x_mat.astype(BF16)).astype(BF16)
        y_bd = _dg(rh_bd, h0_bf, NT) + _dg(b_rb, u_bf) + y0
        h_new = (h0 + _dg(u_bf, be_bd, TN) + sn0) * w_chunk
        ys.append(_fold4(y_bd))
        if independent:
            hout_ref[j] = h_new
        else:
            h = h_new
    if not independent:
        h_scr[...] = h
        hout_ref[0] = h

    y = _head_layer_norm(jnp.concatenate(ys, axis=0), seg_mean, lnw_ref[...], lnb_ref[...], B_GN_EPS)
    bonus = _mm_exact_rhs(r * kf * rk_ref[...], seg_sum) * v
    o_ref[...] = (y + bonus) * gate


def _rwkv(hb, shift0, h0_bd, params, *, independent, n_streams, t, base_row):
    cb = RWKV_CB
    rows = cb * CHUNK
    nblk = t // rows
    base = base_row // rows
    masks = _rwkv_masks()
    full = lambda a: pl.BlockSpec(a.shape, lambda s, c: (0,) * a.ndim)
    if independent:
        assert n_streams == 1
        st_map = lambda s, c: (c, 0, 0)
        n_state, st_blk = t // CHUNK, cb
    else:
        st_map = lambda s, c: (s, 0, 0)
        n_state, st_blk = n_streams, 1
    cur = lambda w: pl.BlockSpec((rows, w), lambda s, c: (base + s * nblk + c, 0))
    return pl.pallas_call(
        functools.partial(_rwkv_kernel, cb=cb, independent=independent),
        grid=(n_streams, nblk),
        in_specs=[cur(B_PROJ),
                  pl.BlockSpec((st_blk, 1, B_PROJ), st_map),
                  pl.BlockSpec((st_blk, MIX_W, MIX_W), st_map),
                  full(masks)] + [full(p) for p in params],
        out_specs=[pl.BlockSpec((rows, MIX_W), lambda s, c: (s * nblk + c, 0)),
                   pl.BlockSpec((st_blk, MIX_W, MIX_W), st_map)],
        out_shape=[jax.ShapeDtypeStruct((n_streams * t, MIX_W), F32),
                   jax.ShapeDtypeStruct((n_state, MIX_W, MIX_W), F32)],
        scratch_shapes=[pltpu.VMEM((MIX_W, MIX_W), F32), pltpu.VMEM((1, B_PROJ), F32)],
        compiler_params=_cparams(("parallel", "arbitrary")),
        name="rwkv7",
    )(hb, shift0, h0_bd, masks, *params)


def _to_block_diag(s):
    eye = jnp.eye(N_HEADS, dtype=s.dtype)
    out = s[:, :, :, None, :] * eye[None, :, None, :, None]
    return out.reshape(s.shape[0], MIX_W, MIX_W)


def _from_block_diag(m):
    b = m.reshape(m.shape[0], N_HEADS, HEAD_DIM, N_HEADS, HEAD_DIM)
    return jnp.stack([b[:, h, :, h, :] for h in range(N_HEADS)], axis=1)


def _row(p):
    return p.reshape(1, -1).astype(F32)


def _mixers(proj, caches, lp, tabs, geom):
    aq, ak, av, hb, hc, dq, dk, dv = proj
    bp, tp, bs, ts = geom
    n_p = bp * tp
    ca_k, ca_v, sb_shift, sb_wkv, sc, cd_k, cd_v = caches
    pr = dict(n_streams=bp, t=tp, base_row=0)
    sm = dict(n_streams=bs, t=ts, base_row=n_p)
    zeros_state = jnp.zeros((bp, MIX_W, MIX_W), F32)

    oa_p = _attention(aq, ak, av, None, lp["sink_col"], n_prev=A_PREV_CHUNKS, use_sink=True, **pr)
    oa_s = _attention(aq, ak, av, (ca_k.reshape(bs, -1, A_KV_W), ca_v.reshape(bs, -1, A_KV_W)), lp["sink_col"],
                      n_prev=A_PREV_CHUNKS, use_sink=True, **sm)

    ob_p, h_p = _rwkv(hb, jnp.zeros((bp, 1, B_PROJ), F32), zeros_state, lp["rwkv"], independent=False, **pr)
    ob_s, h_s = _rwkv(hb, sb_shift.reshape(bs, 1, B_PROJ), _to_block_diag(sb_wkv), lp["rwkv"], independent=True,
                      n_streams=1, t=bs * ts, base_row=n_p)

    oc_p, s_p = _retention(hc, *tabs["rope_prompt"], zeros_state, tabs["ret"], lp["c_ln_w"], lp["c_ln_b"], **pr)
    oc_s, s_s = _retention(hc, *tabs["rope_sample"], _to_block_diag(sc), tabs["ret"], lp["c_ln_w"], lp["c_ln_b"], **sm)

    od_p = _attention(dq, dk, dv, None, lp["bias_table"], n_prev=D_PREV_CHUNKS, use_sink=False, **pr)
    od_s = _attention(dq, dk, dv, (cd_k.reshape(bs, -1, MIX_W), cd_v.reshape(bs, -1, MIX_W)), lp["bias_table"],
                      n_prev=D_PREV_CHUNKS, use_sink=False, **sm)

    mix = ((oa_p, oa_s), (ob_p, ob_s), (oc_p, oc_s), (od_p, od_s))
    return mix, (_from_block_diag(h_p), _from_block_diag(s_p)), (_from_block_diag(h_s), _from_block_diag(s_s))


def kernel(x_prompt, x_sample, cache_a_k, cache_a_v, state_b_shift, state_b_wkv, state_c, cache_d_k, cache_d_v,
           norm1_g, norm2_g, w_in, w_out, a_q_norm, a_k_norm, a_sinks, b_mu, b_w0, b_w2, b_a0, b_a2, b_g2,
           b_k_k, b_k_a, b_r_k, b_ln_w, b_ln_b, c_ln_w, c_ln_b, d_q_norm, d_k_norm, d_rel_bias,
           ffn_w1, ffn_w3, ffn_w2, moe_router, moe_w1, moe_w3, moe_w2):
    bp, tp, _ = x_prompt.shape
    bs, ts, _ = x_sample.shape
    assert ts == CHUNK
    n_p, n_s = bp * tp, bs * ts
    geom = (bp, tp, bs, ts)
    xa = x_prompt.reshape(n_p, D_MODEL)
    xb = x_sample.reshape(n_s, D_MODEL)

    tabs = {
        "ret": _retention_tables(),
        "rope_prompt": _rope_tables(np.arange(tp)),
        "rope_sample": _rope_tables(PAST_LEN + np.arange(ts)),
    }
    tile = lambda g: _row(jnp.tile(g, MIX_W // HEAD_DIM))

    p_states, s_states = [], []
    for l in range(DEPTH):
        lp = {
            "sink_col": jnp.repeat(a_sinks[l].astype(F32), CHUNK).reshape(N_HEADS * CHUNK, 1),
            "bias_table": _relbias_table(d_rel_bias[l].astype(F32), (D_PREV_CHUNKS + 1) * CHUNK),
            "rwkv": (_row(b_mu[l]), _row(b_w0[l]), b_w2[l], _row(b_a0[l]), b_a2[l], b_g2[l], _row(b_k_k[l]),
                     _row(b_k_a[l]), _row(b_r_k[l]), _row(b_ln_w[l]), _row(b_ln_b[l])),
            "c_ln_w": _row(c_ln_w[l]), "c_ln_b": _row(c_ln_b[l]),
        }
        proj = _inproj(xa, xb, n_p + n_s, _row(norm1_g[l]), w_in[l].astype(BF16), tile(a_q_norm[l]),
                       _row(jnp.tile(a_k_norm[l], A_KV_W // HEAD_DIM)), tile(d_q_norm[l]), tile(d_k_norm[l]))
        _, ak, av, hb, _, _, dk, dv = proj
        caches = (cache_a_k[l], cache_a_v[l], state_b_shift[l], state_b_wkv[l], state_c[l], cache_d_k[l], cache_d_v[l])
        mix, (wkv_p, ret_p), (wkv_s, ret_s) = _mixers(proj, caches, lp, tabs, geom)
        j = l // 2
        if l % 2 == 0:
            xa = xb = _outproj_ffn(xa, xb, n_p + n_s, mix, w_out[l].astype(BF16), _row(norm2_g[l]),
                                   ffn_w1[j].astype(BF16), ffn_w3[j].astype(BF16), ffn_w2[j].astype(BF16))
        else:
            x1, xn_a, xn_b, *routing = _outproj_route(xa, xb, n_p + n_s, mix, w_out[l].astype(BF16),
                                                      _row(norm2_g[l]), _router_lanes(moe_router[j]))
            groups = ((0, n_p), (n_p, n_s)) if l == DEPTH - 1 else ((0, n_p + n_s),)
            outs = _moe(xn_a, xn_b, x1, routing,
                        moe_w1[j].astype(BF16), moe_w3[j].astype(BF16), moe_w2[j].astype(BF16), groups)
            xa, xb = (outs[0], outs[-1])

        wa = min(A_PREV_CHUNKS * CHUNK, tp)
        wd = min(D_PREV_CHUNKS * CHUNK, tp)
        tail = lambda a, w, heads: jnp.stack(
            [a[(b + 1) * tp - w:(b + 1) * tp] for b in range(bp)]).reshape(bp, w, heads, HEAD_DIM)
        last_rows = lambda a, t, first, count: jnp.concatenate(
            [a[first + (s + 1) * t - 1:first + (s + 1) * t] for s in range(count)], axis=0)
        p_states.append((tail(ak, wa, 2), tail(av, wa, 2), last_rows(hb, tp, 0, bp), wkv_p, ret_p,
                         tail(dk, wd, N_HEADS), tail(dv, wd, N_HEADS)))
        roll_in = lambda cache, new, heads: jnp.concatenate(
            [cache.astype(F32), new[n_p:].reshape(bs, ts, heads, HEAD_DIM)], axis=1)[:, -cache.shape[1]:]
        s_states.append((roll_in(cache_a_k[l], ak, 2), roll_in(cache_a_v[l], av, 2),
                         last_rows(hb, ts, n_p, bs), wkv_s, ret_s,
                         roll_in(cache_d_k[l], dk, N_HEADS), roll_in(cache_d_v[l], dv, N_HEADS)))

    if xa is xb:
        xa, xb = xa[:n_p], xa[n_p:]
    yp = xa.reshape(bp, tp, D_MODEL)
    ys = xb.reshape(bs, ts, D_MODEL)
    st = lambda group, i: jnp.stack([g[i] for g in group], axis=0)
    return (yp, ys,
            st(p_states, 0), st(p_states, 1), st(p_states, 2), st(p_states, 3), st(p_states, 4), st(p_states, 5), st(p_states, 6),
            st(s_states, 0), st(s_states, 1), st(s_states, 2), st(s_states, 3), st(s_states, 4), st(s_states, 5), st(s_states, 6))
```

```python
import functools

import jax
import jax.numpy as jnp
import numpy as np
from jax import lax
from jax.experimental import pallas as pl
from jax.experimental.pallas import tpu as pltpu
from jax.experimental.pallas import tpu_sc as plsc

F32 = jnp.float32
BF16 = jnp.bfloat16

D_MODEL = 1024
DEPTH = 2
PAST_LEN = 4096
CHUNK = 64
HEAD_DIM = 64
N_HEADS = 4
MIX_W = N_HEADS * HEAD_DIM
A_KV_W = 128
A_PREV_CHUNKS = 2
D_PREV_CHUNKS = 8
D_REL_CLIP = 128
B_PROJ = 1024
C_PROJ = 1024
IN_PROJ = 3328
B_GN_EPS = 64e-5
C_GN_EPS = 1e-6
NORM_EPS = 1e-6
ATTN_SCALE = 0.125
ROPE_BASE = 10000.0
D_FF = 2816
N_EXPERTS = 8
E_FF = 3584
NEG_BIG = -1e30

VMEM_LIMIT = 56 * 1024 * 1024

NN = ((1,), (0,))
NT = ((1,), (1,))
TN = ((0,), (0,))


def _dg(a, b, dims=NN):
    return lax.dot_general(a, b, (dims, ((), ())), preferred_element_type=F32)


def _parts(x, n):
    out = []
    r = x
    for i in range(n):
        p = r.astype(BF16)
        out.append(p)
        if i + 1 < n:
            r = r - p.astype(F32)
    return out


def _mm(a, b, dims=NN, passes=1):
    if passes == 1:
        return _dg(a.astype(BF16), b.astype(BF16), dims)
    ah, al = _parts(a, 2)
    bh, bl = _parts(b, 2)
    return _dg(ah, bh, dims) + (_dg(ah, bl, dims) + _dg(al, bh, dims))


def _mm_exact_rhs(a, b_bf, dims=NN, n=2):
    acc = None
    for p in _parts(a, n):
        t = _dg(p, b_bf, dims)
        acc = t if acc is None else acc + t
    return acc


def _iota2(shape, dim):
    return lax.broadcasted_iota(jnp.int32, shape, dim)


def _head_mask(rows, cols=MIX_W):
    return (_iota2((rows, cols), 0) >> 6) == (_iota2((rows, cols), 1) >> 6)


def _seg_matrix(width, value):
    m = _head_mask(width, width)
    return jnp.where(m, value, 0.0).astype(BF16)


def _tile4(z):
    return jnp.concatenate([z, z, z, z], axis=0)


def _fold4(z):
    return (z[0:64] + z[64:128]) + (z[128:192] + z[192:256])


def _sigmoid(x):
    return 1.0 / (1.0 + jnp.exp(-x))


def _cparams(sem):
    return pltpu.CompilerParams(dimension_semantics=sem, vmem_limit_bytes=VMEM_LIMIT)


IN_TM = 512


def _two_source_specs(xa, xb, n):
    na, nb = xa.shape[0] // IN_TM, xb.shape[0] // IN_TM
    spec_a = pl.BlockSpec((IN_TM, D_MODEL), lambda i: (jnp.minimum(i, na - 1), 0))
    spec_b = pl.BlockSpec((IN_TM, D_MODEL), lambda i: (jnp.clip(i - na, 0, nb - 1), 0))
    return na, n // IN_TM, spec_a, spec_b


def _inproj_kernel(xa_ref, xb_ref, g_ref, w_ref, aqg_ref, akg_ref, dqg_ref, dkg_ref,
                   aq_ref, ak_ref, av_ref, hb_ref, hc_ref, dq_ref, dk_ref, dv_ref, *, n_first):
    x = jnp.where(pl.program_id(0) < n_first, xa_ref[...], xb_ref[...])
    ms = jnp.mean(x * x, axis=-1, keepdims=True)
    xn = ((x * lax.rsqrt(ms + NORM_EPS)) * g_ref[...]).astype(BF16)
    seg = _seg_matrix(MIX_W, 1.0 / HEAD_DIM)

    def proj(lo, hi):
        return jnp.dot(xn, w_ref[:, lo:hi], preferred_element_type=F32)

    def head_rms(h, gain_ref):
        w = h.shape[-1]
        msq = _mm_exact_rhs(h * h, seg[:w, :w], n=1)
        return (h * lax.rsqrt(msq + NORM_EPS)) * gain_ref[...]

    aq_ref[...] = head_rms(proj(0, 256), aqg_ref)
    ak_ref[...] = head_rms(proj(256, 384), akg_ref)
    av_ref[...] = proj(384, 512)
    hb_ref[...] = proj(512, 1536)
    hc_ref[...] = proj(1536, 2560)
    dq_ref[...] = head_rms(proj(2560, 2816), dqg_ref)
    dk_ref[...] = head_rms(proj(2816, 3072), dkg_ref)
    dv_ref[...] = proj(3072, 3328)


def _inproj(xa, xb, n, g, w_bf, aqg, akg, dqg, dkg):
    na, nblk, spec_a, spec_b = _two_source_specs(xa, xb, n)
    widths = (256, 128, 128, B_PROJ, C_PROJ, 256, 256, 256)
    row = lambda w: pl.BlockSpec((IN_TM, w), lambda i: (i, 0))
    full = lambda a: pl.BlockSpec(a.shape, lambda i: (0,) * a.ndim)
    return pl.pallas_call(
        functools.partial(_inproj_kernel, n_first=na),
        grid=(nblk,),
        in_specs=[spec_a, spec_b, full(g), full(w_bf), full(aqg), full(akg), full(dqg), full(dkg)],
        out_specs=[row(w) for w in widths],
        out_shape=[jax.ShapeDtypeStruct((n, w), F32) for w in widths],
        compiler_params=_cparams(("parallel",)),
        name="inproj",
    )(xa, xb, g, w_bf, aqg, akg, dqg, dkg)


PACK_W = 256


def _pack_bf16_pairs(hi, lo):
    bits = lambda z: pltpu.bitcast(z.astype(BF16).astype(F32), jnp.int32)
    return bits(hi) | lax.shift_right_logical(bits(lo), jnp.full(lo.shape, 16, jnp.int32))


def _unpack_bf16_pairs(w):
    hi = pltpu.bitcast(w & jnp.int32(-65536), F32)
    lo = pltpu.bitcast(lax.shift_left(w, jnp.full(w.shape, 16, jnp.int32)), F32)
    return hi, lo


def _pack_rows(x):
    return (_pack_bf16_pairs(x[:, 0:PACK_W], x[:, PACK_W:2 * PACK_W]),
            _pack_bf16_pairs(x[:, 2 * PACK_W:3 * PACK_W], x[:, 3 * PACK_W:4 * PACK_W]))


def _unpack_rows(wa, wb):
    return jnp.concatenate(_unpack_bf16_pairs(wa) + _unpack_bf16_pairs(wb), axis=1)


ROUTER_LANES = 128


ROUTER_TERMS = 3


def _router_lanes(router):
    terms = _parts(router.astype(F32), ROUTER_TERMS)
    return jnp.pad(jnp.concatenate(terms, axis=1), ((0, 0), (0, ROUTER_LANES - ROUTER_TERMS * N_EXPERTS)))


def _route(xn_bf, router_bf, counts):
    split = _dg(xn_bf, router_bf)
    logits = split
    for k in range(1, ROUTER_TERMS):
        logits = logits + pltpu.roll(split, ROUTER_LANES - k * N_EXPERTS, 1)
    lane = _iota2(logits.shape, 1)
    logits = jnp.where(lane < N_EXPERTS, logits, NEG_BIG)
    m1 = jnp.max(logits, axis=-1, keepdims=True)
    i1 = jnp.min(jnp.where(logits == m1, lane, ROUTER_LANES), axis=-1, keepdims=True)
    rest = jnp.where(lane == i1, NEG_BIG, logits)
    m2 = jnp.max(rest, axis=-1, keepdims=True)
    i2 = jnp.min(jnp.where(rest == m2, lane, ROUTER_LANES), axis=-1, keepdims=True)
    e2 = jnp.exp(m2 - m1)
    den = 1.0 + e2
    gates = jnp.where(lane == i1, 1.0 / den, 0.0) + jnp.where(lane == i2, e2 / den, 0.0)
    sel = jnp.where(gates > 0.0, 1.0, 0.0)
    tm = sel.shape[0]
    before = jnp.where(_iota2((tm, tm), 1) < _iota2((tm, tm), 0), 1.0, 0.0).astype(BF16)
    rank = (_dg(before, sel.astype(BF16)) + counts).astype(jnp.int32)
    return gates, rank, counts + jnp.sum(sel, axis=0, keepdims=True)


def _mixed_residual(i, xa_ref, xb_ref, mix_refs, w_ref, g_ref, n_first_x, n_first_mix):
    x1 = jnp.where(i < n_first_x, xa_ref[...], xb_ref[...])
    for m in range(4):
        o = jnp.where(i < n_first_mix, mix_refs[2 * m][...], mix_refs[2 * m + 1][...])
        x1 = x1 + jnp.dot(o.astype(BF16), w_ref[m * MIX_W:(m + 1) * MIX_W, :], preferred_element_type=F32)
    ms = jnp.mean(x1 * x1, axis=-1, keepdims=True)
    return x1, (x1 * lax.rsqrt(ms + NORM_EPS)) * g_ref[...]


def _outproj_route_kernel(xa_ref, xb_ref, *refs, n_first_x, n_first_mix):
    mix_refs, (w_ref, g_ref, r_ref), outs = refs[:8], refs[8:11], refs[11:]
    x1_ref, pa_ref, pb_ref, gate_ref, rank_ref, cnt_ref, cnt_scr = outs
    i = pl.program_id(0)
    acc, xn = _mixed_residual(i, xa_ref, xb_ref, mix_refs, w_ref, g_ref, n_first_x, n_first_mix)

    @pl.when(i == 0)
    def _():
        cnt_scr[...] = jnp.zeros_like(cnt_scr)

    x1_ref[...] = acc
    pa_ref[...], pb_ref[...] = _pack_rows(xn)
    gates, rank, counts = _route(xn.astype(BF16), r_ref[...], cnt_scr[...])
    gate_ref[...] = gates
    rank_ref[...] = rank
    cnt_scr[...] = counts
    cnt_ref[...] = counts.astype(jnp.int32)


def _residual_specs(xa, xb, n, mix, index):
    na, nb = xa.shape[0] // IN_TM, xb.shape[0] // IN_TM
    nm_p, nm_s = mix[0][0].shape[0] // IN_TM, mix[0][1].shape[0] // IN_TM
    first = lambda w, cnt: pl.BlockSpec((IN_TM, w), index(lambda i: jnp.minimum(i, cnt - 1)))
    second = lambda w, skip, cnt: pl.BlockSpec((IN_TM, w), index(lambda i: jnp.clip(i - skip, 0, cnt - 1)))
    specs = [first(D_MODEL, na), second(D_MODEL, na, nb)] + [first(MIX_W, nm_p), second(MIX_W, nm_p, nm_s)] * 4
    return specs, [xa, xb, *[a for pair in mix for a in pair]], dict(n_first_x=na, n_first_mix=nm_p)


def _outproj_route(xa, xb, n, mix, w_bf, g2, router_bf):
    specs, operands, statics = _residual_specs(xa, xb, n, mix, lambda blk: (lambda i: (blk(i), 0)))
    row = lambda w: pl.BlockSpec((IN_TM, w), lambda i: (i, 0))
    full = lambda a: pl.BlockSpec(a.shape, lambda i: (0,) * a.ndim)
    return pl.pallas_call(
        functools.partial(_outproj_route_kernel, **statics),
        grid=(n // IN_TM,),
        in_specs=specs + [full(w_bf), full(g2), full(router_bf)],
        out_specs=[row(D_MODEL), row(PACK_W), row(PACK_W), row(ROUTER_LANES), row(ROUTER_LANES),
                   pl.BlockSpec((1, ROUTER_LANES), lambda i: (0, 0))],
        out_shape=[jax.ShapeDtypeStruct((n, D_MODEL), F32),
                   jax.ShapeDtypeStruct((n, PACK_W), jnp.int32), jax.ShapeDtypeStruct((n, PACK_W), jnp.int32),
                   jax.ShapeDtypeStruct((n, ROUTER_LANES), F32), jax.ShapeDtypeStruct((n, ROUTER_LANES), jnp.int32),
                   jax.ShapeDtypeStruct((1, ROUTER_LANES), jnp.int32)],
        scratch_shapes=[pltpu.VMEM((1, ROUTER_LANES), F32)],
        compiler_params=_cparams(("arbitrary",)),
        name="outproj_route",
    )(*operands, w_bf, g2, router_bf)


FFN_TF = 1408


def _outproj_ffn_kernel(xa_ref, xb_ref, *refs, n_first_x, n_first_mix):
    mix_refs, rest = refs[:8], refs[8:]
    w_ref, g_ref, w1_ref, w3_ref, w2_ref, o_ref, x1_scr, xn_scr = rest
    i, f = pl.program_id(0), pl.program_id(1)

    @pl.when(f == 0)
    def _():
        x1, xn = _mixed_residual(i, xa_ref, xb_ref, mix_refs, w_ref, g_ref, n_first_x, n_first_mix)
        x1_scr[...] = x1
        xn_scr[...] = xn.astype(BF16)

    xn = xn_scr[...]
    a = jnp.dot(xn, w1_ref[...], preferred_element_type=F32)
    b = jnp.dot(xn, w3_ref[...], preferred_element_type=F32)
    h = ((a * _sigmoid(a)) * b).astype(BF16)
    y = jnp.dot(h, w2_ref[...], preferred_element_type=F32)

    @pl.when(f == 0)
    def _():
        o_ref[...] = x1_scr[...] + y

    @pl.when(f != 0)
    def _():
        o_ref[...] += y


def _outproj_ffn(xa, xb, n, mix, w_bf, g2, w1_bf, w3_bf, w2_bf):
    specs, operands, statics = _residual_specs(xa, xb, n, mix, lambda blk: (lambda i, f: (blk(i), 0)))
    full = lambda a: pl.BlockSpec(a.shape, lambda i, f: (0,) * a.ndim)
    return pl.pallas_call(
        functools.partial(_outproj_ffn_kernel, **statics),
        grid=(n // IN_TM, D_FF // FFN_TF),
        in_specs=specs + [full(w_bf), full(g2),
                          pl.BlockSpec((D_MODEL, FFN_TF), lambda i, f: (0, f)),
                          pl.BlockSpec((D_MODEL, FFN_TF), lambda i, f: (0, f)),
                          pl.BlockSpec((FFN_TF, D_MODEL), lambda i, f: (f, 0))],
        out_specs=pl.BlockSpec((IN_TM, D_MODEL), lambda i, f: (i, 0)),
        out_shape=jax.ShapeDtypeStruct((n, D_MODEL), F32),
        scratch_shapes=[pltpu.VMEM((IN_TM, D_MODEL), F32), pltpu.VMEM((IN_TM, D_MODEL), BF16)],
        compiler_params=_cparams(("parallel", "arbitrary")),
        name="outproj_ffn",
    )(*operands, w_bf, g2, w1_bf, w3_bf, w2_bf)


MOE_R = 512
MOE_TF = 1792
SC_WINDOW = 128


def _moe_plan(gates, rank, counts, n):
    n_blocks = (2 * n) // MOE_R + N_EXPERTS + 1
    spare_row = (n_blocks - 1) * MOE_R
    sel = gates[:, :N_EXPERTS] > 0.0
    rank = rank[:, :N_EXPERTS]
    counts = counts[0, :N_EXPERTS]
    padded = ((counts + MOE_R - 1) // MOE_R) * MOE_R
    pad_end = jnp.cumsum(padded)
    pad_start = pad_end - padded
    pos = jnp.where(sel, pad_start[None, :] + rank, -1)
    order = jnp.cumsum(sel.astype(jnp.int32), axis=1)
    pick = lambda j: jnp.max(jnp.where(jnp.logical_and(sel, order == j), pos, -1), axis=1)
    to_row = lambda p: jnp.where(p >= 0, p, spare_row).astype(jnp.int32).reshape(1, n)
    block_expert = jnp.minimum(
        jnp.sum(pad_end[None, :] <= (jnp.arange(n_blocks) * MOE_R)[:, None], axis=1), N_EXPERTS - 1)
    return dict(n_blocks=n_blocks, pos0=to_row(pick(1)), pos1=to_row(pick(2)),
                block_expert=block_expert.astype(jnp.int32), n_used=(pad_end[-1:] // MOE_R).astype(jnp.int32))


def _sc_mesh():
    return plsc.VectorSubcoreMesh(core_axis_name="core", subcore_axis_name="subcore")


SC_CAST_BLOCK = (64, 512)
SC_CAST_OP = (8, 16)


def _sc_to_bf16(w):
    rows, cols = w.size // w.shape[-1], w.shape[-1]

    @functools.partial(pl.kernel, out_type=jax.ShapeDtypeStruct((rows, cols), BF16), mesh=_sc_mesh(),
                       scratch_types=[])
    def cast(x_hbm, o_hbm):
        def body(x_vmem, o_vmem):
            @pl.loop(0, SC_CAST_BLOCK[0], step=SC_CAST_OP[0])
            def _(r0):
                @pl.loop(0, SC_CAST_BLOCK[1], step=SC_CAST_OP[1])
                def _(c0):
                    at = (pl.ds(r0, SC_CAST_OP[0]), pl.ds(c0, SC_CAST_OP[1]))
                    o_vmem.at[*at][...] = x_vmem.at[*at][...].astype(BF16)

        pltpu.emit_pipeline(
            body,
            grid=(rows // SC_CAST_BLOCK[0], cols // SC_CAST_BLOCK[1]),
            in_specs=[pl.BlockSpec(SC_CAST_BLOCK, lambda i, j: (i, j))],
            out_specs=[pl.BlockSpec(SC_CAST_BLOCK, lambda i, j: (i, j))],
            core_axis_name=("core", "subcore"),
            dimension_semantics=(pltpu.PARALLEL, pltpu.PARALLEL),
        )(x_hbm, o_hbm)

    return cast(w.astype(F32).reshape(rows, cols)).reshape(w.shape)


def _sc_scatter_rows(table, idx_lists, n_rows):
    n, cols = table.shape
    k = len(idx_lists)

    @functools.partial(pl.kernel, out_type=jax.ShapeDtypeStruct((n_rows, cols), table.dtype), mesh=_sc_mesh())
    def scatter(x_hbm, *rest):
        i_hbms, o_hbm = rest[:k], rest[k]

        def body(x_vmem, *i_vmems):
            for i_vmem in i_vmems:
                pltpu.sync_copy(x_vmem, o_hbm.at[i_vmem.at[0]])

        pltpu.emit_pipeline(
            body,
            grid=(n // SC_WINDOW,),
            in_specs=[pl.BlockSpec((SC_WINDOW, cols), lambda i: (i, 0))]
            + [pl.BlockSpec((1, SC_WINDOW), lambda i: (0, i))] * k,
            out_specs=[],
            core_axis_name=("core", "subcore"),
            dimension_semantics=(pltpu.PARALLEL,),
        )(x_hbm, *i_hbms)

    return scatter(table, *idx_lists)


def _sc_gather_rows(table, idx):
    n = idx.shape[1]
    cols = table.shape[1]

    @functools.partial(pl.kernel, out_type=jax.ShapeDtypeStruct((n, cols), table.dtype), mesh=_sc_mesh())
    def gather(x_hbm, i_hbm, o_hbm):
        def body(i_vmem, o_vmem):
            pltpu.sync_copy(x_hbm.at[i_vmem.at[0]], o_vmem)

        pltpu.emit_pipeline(
            body,
            grid=(n // SC_WINDOW,),
            in_specs=[pl.BlockSpec((1, SC_WINDOW), lambda i: (0, i))],
            out_specs=[pl.BlockSpec((SC_WINDOW, cols), lambda i: (i, 0))],
            core_axis_name=("core", "subcore"),
            dimension_semantics=(pltpu.PARALLEL,),
        )(i_hbm, o_hbm)

    return gather(table, idx)


def _moe_expert_kernel(be_ref, nu_ref, xa_ref, xb_ref, gs_ref, w1_ref, w3_ref, w2_ref, oa_ref, ob_ref, acc_ref):
    j, f = pl.program_id(0), pl.program_id(1)
    used = j < nu_ref[0]

    @pl.when(used)
    def _():
        x = _unpack_rows(xa_ref[...], xb_ref[...]).astype(BF16)
        a = jnp.dot(x, w1_ref[...], preferred_element_type=F32)
        b = jnp.dot(x, w3_ref[...], preferred_element_type=F32)
        h = ((a * _sigmoid(a)) * b).astype(BF16)
        y = jnp.dot(h, w2_ref[...], preferred_element_type=F32)

        @pl.when(f == 0)
        def _():
            acc_ref[...] = y

        @pl.when(f != 0)
        def _():
            acc_ref[...] += y

    @pl.when(f == pl.num_programs(1) - 1)
    def _():
        lane = _iota2(gs_ref.shape, 1)
        g = jnp.sum(jnp.where(lane == be_ref[j], gs_ref[...], 0.0), axis=1, keepdims=True)
        oa_ref[...], ob_ref[...] = _pack_rows(jnp.where(used, acc_ref[...] * g, 0.0))


def _moe_experts(plan, xs_a, xs_b, gs, w1_bf, w3_bf, w2_bf):
    n_blocks = plan["n_blocks"]
    half = pl.BlockSpec((MOE_R, PACK_W), lambda j, f, be, nu: (j, 0))
    grid_spec = pltpu.PrefetchScalarGridSpec(
        num_scalar_prefetch=2,
        grid=(n_blocks, E_FF // MOE_TF),
        in_specs=[half, half,
                  pl.BlockSpec((MOE_R, ROUTER_LANES), lambda j, f, be, nu: (j, 0)),
                  pl.BlockSpec((None, D_MODEL, MOE_TF), lambda j, f, be, nu: (be[j], 0, f)),
                  pl.BlockSpec((None, D_MODEL, MOE_TF), lambda j, f, be, nu: (be[j], 0, f)),
                  pl.BlockSpec((None, MOE_TF, D_MODEL), lambda j, f, be, nu: (be[j], f, 0))],
        out_specs=[half, half],
        scratch_shapes=[pltpu.VMEM((MOE_R, D_MODEL), F32)])
    return pl.pallas_call(
        _moe_expert_kernel,
        grid_spec=grid_spec,
        out_shape=[jax.ShapeDtypeStruct((n_blocks * MOE_R, PACK_W), jnp.int32)] * 2,
        compiler_params=_cparams(("arbitrary", "arbitrary")),
        name="moe_experts",
    )(plan["block_expert"], plan["n_used"], xs_a, xs_b, gs, w1_bf, w3_bf, w2_bf)


def _moe_combine_kernel(x1_ref, a0_ref, b0_ref, a1_ref, b1_ref, o_ref):
    o_ref[...] = (x1_ref[...] + _unpack_rows(a0_ref[...], b0_ref[...])) + _unpack_rows(a1_ref[...], b1_ref[...])


def _moe_combine(x1, picked, row0, n_rows):
    base = row0 // IN_TM
    row = lambda w: pl.BlockSpec((IN_TM, w), lambda i: (base + i, 0))
    return pl.pallas_call(
        _moe_combine_kernel,
        grid=(n_rows // IN_TM,),
        in_specs=[row(D_MODEL)] + [row(PACK_W)] * 4,
        out_specs=pl.BlockSpec((IN_TM, D_MODEL), lambda i: (i, 0)),
        out_shape=jax.ShapeDtypeStruct((n_rows, D_MODEL), F32),
        compiler_params=_cparams(("parallel",)),
        name="moe_combine",
    )(x1, *picked)


def _moe(xn_a, xn_b, x1, routing, w1_bf, w3_bf, w2_bf, row_groups):
    gates, rank, counts = routing
    plan = _moe_plan(gates, rank, counts, x1.shape[0])
    n_rows = plan["n_blocks"] * MOE_R
    idx = (plan["pos0"], plan["pos1"])
    xs_a = _sc_scatter_rows(xn_a, idx, n_rows)
    xs_b = _sc_scatter_rows(xn_b, idx, n_rows)
    gs = _sc_scatter_rows(gates, idx, n_rows)
    os_a, os_b = _moe_experts(plan, xs_a, xs_b, gs, w1_bf, w3_bf, w2_bf)
    picked = [_sc_gather_rows(t, p) for p in idx for t in (os_a, os_b)]
    return [_moe_combine(x1, picked, row0, rows) for row0, rows in row_groups]


def _relbias_kernel(rb_ref, o_ref, *, nk):
    h = pl.program_id(0)
    q = _iota2((CHUNK, nk), 0)
    r = _iota2((CHUNK, nk), 1)
    idx = jnp.clip(q - (r - (nk - CHUNK)), -D_REL_CLIP, D_REL_CLIP) + D_REL_CLIP

    def body(j, acc):
        return jnp.where(idx == j, rb_ref[h, j], acc)

    o_ref[...] = lax.fori_loop(0, 2 * D_REL_CLIP + 1, body, jnp.zeros((CHUNK, nk), F32))


def _relbias_table(rel_bias, nk):
    return pl.pallas_call(
        functools.partial(_relbias_kernel, nk=nk),
        grid=(N_HEADS,),
        in_specs=[pl.BlockSpec(memory_space=pltpu.SMEM)],
        out_specs=pl.BlockSpec((CHUNK, nk), lambda h: (h, 0)),
        out_shape=jax.ShapeDtypeStruct((N_HEADS * CHUNK, nk), F32),
        name="relbias",
    )(rel_bias)


ATTN_QB = 8
ATTN_GROUP = 8


def _attn_kernel(q_ref, kp_ref, kc_ref, vp_ref, vc_ref, x_ref, *rest, qb, n_prev, use_sink, mask_first):
    o_ref, kbuf, vbuf = rest[-3:]
    i = pl.program_id(1)
    p_rows = kp_ref.shape[0]
    nk = (n_prev + 1) * CHUNK
    wk = kp_ref.shape[1]

    kbuf[0:p_rows, :] = kp_ref[...].astype(BF16)
    kbuf[p_rows:, :] = kc_ref[...].astype(BF16)
    vbuf[0:p_rows, :] = vp_ref[...].astype(BF16)
    vbuf[p_rows:, :] = vc_ref[...].astype(BF16)

    hmask = _head_mask(N_HEADS * CHUNK)
    extra = x_ref[...]
    grouped = wk != MIX_W
    low = _iota2((CHUNK, A_KV_W), 1) < HEAD_DIM

    def stack_queries(qj):
        if not grouped:
            return jnp.where(hmask, _tile4(qj), 0.0)
        shifted = pltpu.roll(qj, MIX_W - HEAD_DIM, 1)[:, :A_KV_W]
        return jnp.concatenate([jnp.where(low, qj[:, :A_KV_W], 0.0), jnp.where(low, shifted, 0.0),
                                jnp.where(low, 0.0, shifted), jnp.where(low, 0.0, qj[:, A_KV_W:])], axis=0)

    def unstack_outputs(o_all):
        if not grouped:
            return _fold4(jnp.where(hmask, o_all, 0.0))
        b0, b1, b2, b3 = (o_all[h * CHUNK:(h + 1) * CHUNK] for h in range(N_HEADS))
        left = jnp.where(low, b0, 0.0) + pltpu.roll(jnp.where(low, b1, 0.0), HEAD_DIM, 1)
        right = pltpu.roll(jnp.where(low, 0.0, b2), HEAD_DIM, 1) + jnp.where(low, 0.0, b3)
        return jnp.concatenate([left, right], axis=1)

    def scores(j):
        base = p_rows + (j - n_prev) * CHUNK
        qs = stack_queries(q_ref[pl.ds(j * CHUNK, CHUNK), :] * ATTN_SCALE).astype(BF16)
        s = _dg(qs, kbuf[pl.ds(base, nk), :], NT)
        if not use_sink:
            s = s + extra
        if mask_first and base < p_rows:
            krow = base + _iota2(s.shape, 1)
            s = jnp.where(jnp.logical_and(i == 0, krow < p_rows), NEG_BIG, s)
        return s

    def weights(s):
        m = jnp.max(s, axis=-1, keepdims=True)
        if use_sink:
            m = jnp.maximum(m, extra)
        e = jnp.exp(s - m)
        den = jnp.sum(e, axis=-1, keepdims=True)
        if use_sink:
            den = den + jnp.exp(extra - m)
        return e.astype(BF16), 1.0 / den

    def output(j, e, inv_den):
        base = p_rows + (j - n_prev) * CHUNK
        o_all = _dg(e, vbuf[pl.ds(base, nk), :]) * inv_den
        o_ref[pl.ds(j * CHUNK, CHUNK), :] = unstack_outputs(o_all)

    for j0 in range(0, qb, ATTN_GROUP):
        group = range(j0, min(j0 + ATTN_GROUP, qb))
        ss = [scores(j) for j in group]
        ws = [weights(s) for s in ss]
        for j, (e, inv_den) in zip(group, ws):
            output(j, e, inv_den)


def _attention(q, k, v, prev, extra, *, n_prev, use_sink, n_streams, t, base_row):
    wk = k.shape[-1]
    if prev is None:
        qb = ATTN_QB
        rows = qb * CHUNK
        nblk = t // rows
        base = base_row // rows
        prev_spec = pl.BlockSpec((rows, wk), lambda s, i: (base + s * nblk + jnp.maximum(i - 1, 0), 0))
        k_prev, v_prev, p_rows, mask_first = k, v, rows, True
    else:
        qb, rows, nblk = t // CHUNK, t, 1
        base = base_row // rows
        k_prev, v_prev = prev
        p_rows = k_prev.shape[1]
        prev_spec = pl.BlockSpec((None, p_rows, wk), lambda s, i: (s, 0, 0))
        mask_first = False
    cur = lambda w: pl.BlockSpec((rows, w), lambda s, i: (base + s * nblk + i, 0))
    kern = functools.partial(_attn_kernel, qb=qb, n_prev=n_prev, use_sink=use_sink, mask_first=mask_first)
    return pl.pallas_call(
        kern,
        grid=(n_streams, nblk),
        in_specs=[cur(MIX_W), prev_spec, cur(wk), prev_spec, cur(wk),
                  pl.BlockSpec(extra.shape, lambda s, i: (0, 0))],
        out_specs=pl.BlockSpec((rows, MIX_W), lambda s, i: (s * nblk + i, 0)),
        out_shape=jax.ShapeDtypeStruct((n_streams * t, MIX_W), F32),
        scratch_shapes=[pltpu.VMEM((p_rows + rows, wk), BF16), pltpu.VMEM((p_rows + rows, wk), BF16)],
        compiler_params=_cparams(("parallel", "arbitrary")),
        name="attn_sink" if use_sink else "attn_bias",
    )(q, k_prev, k, v_prev, v, extra)


def _head_layer_norm(o, seg_mean_bf, w, b, eps):
    mu = _mm_exact_rhs(o, seg_mean_bf)
    d = o - mu
    var = _mm_exact_rhs(d * d, seg_mean_bf)
    return (d * lax.rsqrt(var + eps)) * w + b


def _ret_kernel(hc_ref, cos_ref, sin_ref, s0_ref, dstack_ref, qsc_ref, ksc_ref, gam_ref, lnw_ref, lnb_ref,
                *rest, qb):
    o_ref, sout_ref, s_scr = rest[-3:]
    i = pl.program_id(1)

    @pl.when(i == 0)
    def _():
        s_scr[...] = s0_ref[...]

    hmask = _head_mask(N_HEADS * CHUNK)
    seg_mean = _seg_matrix(MIX_W, 1.0 / HEAD_DIM)
    rows = qb * CHUNK
    first_half = (_iota2((rows, MIX_W), 1) & (HEAD_DIM - 1)) < (HEAD_DIM // 2)
    cos = jnp.concatenate([cos_ref[...]] * (MIX_W // ROPE_W), axis=1)
    sin = jnp.concatenate([sin_ref[...]] * (MIX_W // ROPE_W), axis=1)

    def rope(x):
        partner = jnp.where(first_half, pltpu.roll(x, MIX_W - HEAD_DIM // 2, 1), pltpu.roll(x, HEAD_DIM // 2, 1))
        return x * cos + partner * sin

    q = rope(hc_ref[:, 0:256])
    k = rope(hc_ref[:, 256:512]) * ATTN_SCALE
    v_bf = hc_ref[:, 512:768].astype(BF16)
    state = s_scr[...]
    outs = []
    for j in range(qb):
        sl = slice(j * CHUNK, (j + 1) * CHUNK)
        qj, kj, vj = q[sl], k[sl], v_bf[sl]
        qs = jnp.where(hmask, _tile4(qj), 0.0).astype(BF16)
        sc = _dg(qs, kj.astype(BF16), NT) * dstack_ref[...]
        intra = _fold4(jnp.where(hmask, _dg(sc.astype(BF16), vj), 0.0))
        inter = _dg((qj * qsc_ref[...]).astype(BF16), state.astype(BF16))
        kv = _dg((kj * ksc_ref[...]).astype(BF16), vj, TN)
        state = gam_ref[...] * state + jnp.where(hmask, kv, 0.0)
        outs.append(intra + inter)
    s_scr[...] = state
    sout_ref[...] = state
    y = _head_layer_norm(jnp.concatenate(outs, axis=0), seg_mean, lnw_ref[...], lnb_ref[...], C_GN_EPS)
    g = hc_ref[:, 768:1024]
    o_ref[...] = y * (g * _sigmoid(g))


def _retention(hc, cos, sin, s0_bd, tabs, lnw, lnb, *, n_streams, t, base_row):
    qb = min(8, t // CHUNK)
    rows = qb * CHUNK
    nblk = t // rows
    base = base_row // rows
    dstack, qsc, ksc, gam = tabs
    full = lambda a: pl.BlockSpec(a.shape, lambda s, i: (0,) * a.ndim)
    cur = lambda w: pl.BlockSpec((rows, w), lambda s, i: (base + s * nblk + i, 0))
    state = pl.BlockSpec((None, MIX_W, MIX_W), lambda s, i: (s, 0, 0))
    return pl.pallas_call(
        functools.partial(_ret_kernel, qb=qb),
        grid=(n_streams, nblk),
        in_specs=[cur(C_PROJ),
                  pl.BlockSpec((rows, ROPE_W), lambda s, i: (i, 0)),
                  pl.BlockSpec((rows, ROPE_W), lambda s, i: (i, 0)),
                  state, full(dstack), full(qsc), full(ksc), full(gam), full(lnw), full(lnb)],
        out_specs=[pl.BlockSpec((rows, MIX_W), lambda s, i: (s * nblk + i, 0)), state],
        out_shape=[jax.ShapeDtypeStruct((n_streams * t, MIX_W), F32),
                   jax.ShapeDtypeStruct((n_streams, MIX_W, MIX_W), F32)],
        scratch_shapes=[pltpu.VMEM((MIX_W, MIX_W), F32)],
        compiler_params=_cparams(("parallel", "arbitrary")),
        name="retention",
    )(hc, cos, sin, s0_bd, dstack, qsc, ksc, gam, lnw, lnb)


def _retention_tables():
    gamma = 1.0 - 2.0 ** (-5.0 - np.arange(N_HEADS, dtype=np.float64))
    t = np.arange(CHUNK)
    diff = t[:, None] - t[None, :]
    dmat = np.where(diff >= 0, gamma[:, None, None] ** np.maximum(diff, 0), 0.0)
    dstack = dmat.reshape(N_HEADS * CHUNK, CHUNK)
    lanes = lambda per_head: np.repeat(per_head, HEAD_DIM, axis=-1)
    qsc = lanes(gamma[None, :] ** (t + 1)[:, None])
    ksc = lanes(gamma[None, :] ** (CHUNK - 1 - t)[:, None])
    gam = np.broadcast_to(lanes(gamma ** CHUNK)[:, None], (MIX_W, MIX_W))
    return tuple(jnp.asarray(a, F32) for a in (dstack, qsc, ksc, gam))


ROPE_W = 2 * HEAD_DIM


def _rope_tables(pos):
    half = HEAD_DIM // 2
    theta = np.float32(1.0) / (np.float32(ROPE_BASE) ** np.linspace(0.0, 1.0, half, dtype=np.float32))
    ang = np.asarray(pos, np.float32)[:, None] * theta[None, :]
    cos, sin = np.cos(ang), np.sin(ang)
    reps = ROPE_W // HEAD_DIM
    cos_t = np.tile(np.concatenate([cos, cos], axis=-1), (1, reps))
    sin_t = np.tile(np.concatenate([-sin, sin], axis=-1), (1, reps))
    return jnp.asarray(cos_t, F32), jnp.asarray(sin_t, F32)


DECAY_SCALE = 0.6065306597126334
RWKV_CB = 4
N_LEVELS = 6
MASK_HEAD, MASK_STRICT, MASK_INCL, MASK_LEVEL0 = 0, 1, 2, 3


def _rwkv_masks():
    n4 = N_HEADS * CHUNK
    ri = np.arange(n4)[:, None]
    ci = np.arange(n4)[None, :]
    head = (ri >> 6) == (ci >> 6)
    tabs = [head, head & ((ci & 63) < (ri & 63)), head & ((ci & 63) <= (ri & 63))]
    for log_m in range(N_LEVELS):
        same = (ri >> (log_m + 1)) == (ci >> (log_m + 1))
        tabs.append(same & (((ri >> log_m) & 1) == 1) & (((ci >> log_m) & 1) == 0))
    return jnp.asarray(np.stack(tabs), BF16)


def _rwkv_kernel(hb_ref, shift0_ref, h0_ref, masks_ref, mu_ref, w0_ref, w2_ref, a0_ref, a2_ref, g2_ref,
                 kk_ref, ka_ref, rk_ref, lnw_ref, lnb_ref, *rest, cb, independent):
    o_ref, hout_ref, h_scr, shift_scr = rest[-4:]
    c = pl.program_id(1)
    rows = cb * CHUNK
    xb = hb_ref[...]
    row = _iota2(xb.shape, 0)
    prev = pltpu.roll(xb, 1, 0)
    if independent:
        for j in range(cb):
            prev = jnp.where(row == j * CHUNK, shift0_ref[j], prev)
    else:
        @pl.when(c == 0)
        def _():
            h_scr[...] = h0_ref[0]
            shift_scr[...] = shift0_ref[0]

        prev = jnp.where(row == 0, shift_scr[...], prev)
        shift_scr[...] = xb[rows - 1:rows, :]
    xs = xb + mu_ref[...] * (prev - xb)
    r = xs[:, 0:256]
    k = xs[:, 256:512]
    v = xs[:, 512:768]
    xw = xs[:, 768:832]
    xa = xs[:, 832:896]
    xg = xs[:, 896:1024]

    z = w0_ref[...] + _mm(jnp.tanh(xw), w2_ref[...], passes=3)
    lw = -DECAY_SCALE * _sigmoid(z)
    a_gate = _sigmoid(a0_ref[...] + _mm(xa, a2_ref[...], passes=3))
    gate = _mm(_sigmoid(xg), g2_ref[...], passes=1)

    seg_sum = _seg_matrix(MIX_W, 1.0)
    seg_mean = _seg_matrix(MIX_W, 1.0 / HEAD_DIM)
    kkn = k * kk_ref[...]
    norm = jnp.sqrt(_mm_exact_rhs(kkn * kkn, seg_sum))
    kk = kkn / jnp.maximum(norm, 1e-12)
    kf = k * (1.0 + (a_gate - 1.0) * ka_ref[...])

    tt = _iota2((rows, rows), 0)
    ss = _iota2((rows, rows), 1)
    tril = jnp.where(jnp.logical_and(ss <= tt, (ss >> 6) == (tt >> 6)), 1.0, 0.0).astype(BF16)
    lw_parts = _parts(lw, 3)
    cum = _dg(tril, lw_parts[0]) + (_dg(tril, lw_parts[1]) + _dg(tril, lw_parts[2]))
    w_inv = jnp.exp(-cum)
    rho = (r * jnp.exp(cum)).astype(BF16)
    alpha = (-kk * jnp.exp(cum - lw)).astype(BF16)
    beta = ((kk * a_gate) * w_inv).astype(BF16)
    kappa = (kf * w_inv).astype(BF16)
    v_bf = v.astype(BF16)

    hmask = masks_ref[MASK_HEAD]
    n4 = N_HEADS * CHUNK
    eye = jnp.where(_iota2((n4, n4), 0) == _iota2((n4, n4), 1), 1.0, 0.0)

    pre, a_bfs, t_invs = [], [], []
    for j in range(cb):
        sl = slice(j * CHUNK, (j + 1) * CHUNK)
        bd = lambda zz: _tile4(zz[sl]) * hmask
        al_bd, be_bd, ka_bd, rh_bd, v_bd = bd(alpha), bd(beta), bd(kappa), bd(rho), bd(v_bf)
        a_bf = _dg(al_bd, be_bd, NT).astype(BF16) * masks_ref[MASK_STRICT]
        a_ak = _dg(al_bd, ka_bd, NT).astype(BF16) * masks_ref[MASK_STRICT]
        b_rb = _dg(rh_bd, be_bd, NT).astype(BF16) * masks_ref[MASK_INCL]
        b_rk = _dg(rh_bd, ka_bd, NT).astype(BF16) * masks_ref[MASK_INCL]
        x0 = _dg(a_ak, v_bd)
        y0 = _dg(b_rk, v_bd)
        sn0 = _dg(v_bd, ka_bd, TN)
        w_chunk = jnp.exp(cum[(j + 1) * CHUNK - 1:(j + 1) * CHUNK, :])
        a_bfs.append(a_bf)
        t_invs.append(eye + (a_bf * masks_ref[MASK_LEVEL0]).astype(F32))
        pre.append([al_bd, be_bd, rh_bd, b_rb, None, x0, y0, sn0, w_chunk])
    for lvl in range(1, N_LEVELS):
        t_bfs = [t.astype(BF16) for t in t_invs]
        e_mats = [_dg(a_bfs[j] * masks_ref[MASK_LEVEL0 + lvl], t_bfs[j]) for j in range(cb)]
        t_invs = [t_invs[j] + _dg(t_bfs[j], e_mats[j].astype(BF16)) for j in range(cb)]
    for j in range(cb):
        pre[j][4] = t_invs[j].astype(BF16)

    ys = []
    h = None if independent else h_scr[...]
    for j in range(cb):
        al_bd, be_bd, rh_bd, b_rb, t_bf, x0, y0, sn0, w_chunk = pre[j]
        h0 = h0_ref[j] if independent else h
        h0_bf = h0.astype(BF16)
        x_mat = _dg(al_bd, h0_bf, NT) + x0
        u_bf = _dg(t_bf, x_mat.astype(BF16)).astype(BF16)
        y_bd = _dg(rh_bd, h0_bf, NT) + _dg(b_rb, u_bf) + y0
        h_new = (h0 + _dg(u_bf, be_bd, TN) + sn0) * w_chunk
        ys.append(_fold4(y_bd))
        if independent:
            hout_ref[j] = h_new
        else:
            h = h_new
    if not independent:
        h_scr[...] = h
        hout_ref[0] = h

    y = _head_layer_norm(jnp.concatenate(ys, axis=0), seg_mean, lnw_ref[...], lnb_ref[...], B_GN_EPS)
    bonus = _mm_exact_rhs(r * kf * rk_ref[...], seg_sum) * v
    o_ref[...] = (y + bonus) * gate


def _rwkv(hb, shift0, h0_bd, params, *, independent, n_streams, t, base_row):
    cb = RWKV_CB
    rows = cb * CHUNK
    nblk = t // rows
    base = base_row // rows
    masks = _rwkv_masks()
    full = lambda a: pl.BlockSpec(a.shape, lambda s, c: (0,) * a.ndim)
    if independent:
        assert n_streams == 1
        st_map = lambda s, c: (c, 0, 0)
        n_state, st_blk = t // CHUNK, cb
    else:
        st_map = lambda s, c: (s, 0, 0)
        n_state, st_blk = n_streams, 1
    cur = lambda w: pl.BlockSpec((rows, w), lambda s, c: (base + s * nblk + c, 0))
    return pl.pallas_call(
        functools.partial(_rwkv_kernel, cb=cb, independent=independent),
        grid=(n_streams, nblk),
        in_specs=[cur(B_PROJ),
                  pl.BlockSpec((st_blk, 1, B_PROJ), st_map),
                  pl.BlockSpec((st_blk, MIX_W, MIX_W), st_map),
                  full(masks)] + [full(p) for p in params],
        out_specs=[pl.BlockSpec((rows, MIX_W), lambda s, c: (s * nblk + c, 0)),
                   pl.BlockSpec((st_blk, MIX_W, MIX_W), st_map)],
        out_shape=[jax.ShapeDtypeStruct((n_streams * t, MIX_W), F32),
                   jax.ShapeDtypeStruct((n_state, MIX_W, MIX_W), F32)],
        scratch_shapes=[pltpu.VMEM((MIX_W, MIX_W), F32), pltpu.VMEM((1, B_PROJ), F32)],
        compiler_params=_cparams(("parallel", "arbitrary")),
        name="rwkv7",
    )(hb, shift0, h0_bd, masks, *params)


def _to_block_diag(s):
    eye = jnp.eye(N_HEADS, dtype=s.dtype)
    out = s[:, :, :, None, :] * eye[None, :, None, :, None]
    return out.reshape(s.shape[0], MIX_W, MIX_W)


def _from_block_diag(m):
    b = m.reshape(m.shape[0], N_HEADS, HEAD_DIM, N_HEADS, HEAD_DIM)
    return jnp.stack([b[:, h, :, h, :] for h in range(N_HEADS)], axis=1)


def _row(p):
    return p.reshape(1, -1).astype(F32)


def _mixers(proj, caches, lp, tabs, geom):
    aq, ak, av, hb, hc, dq, dk, dv = proj
    bp, tp, bs, ts = geom
    n_p = bp * tp
    ca_k, ca_v, sb_shift, sb_wkv, sc, cd_k, cd_v = caches
    pr = dict(n_streams=bp, t=tp, base_row=0)
    sm = dict(n_streams=bs, t=ts, base_row=n_p)
    zeros_state = jnp.zeros((bp, MIX_W, MIX_W), F32)

    oa_p = _attention(aq, ak, av, None, lp["sink_col"], n_prev=A_PREV_CHUNKS, use_sink=True, **pr)
    oa_s = _attention(aq, ak, av, (ca_k.reshape(bs, -1, A_KV_W), ca_v.reshape(bs, -1, A_KV_W)), lp["sink_col"],
                      n_prev=A_PREV_CHUNKS, use_sink=True, **sm)

    ob_p, h_p = _rwkv(hb, jnp.zeros((bp, 1, B_PROJ), F32), zeros_state, lp["rwkv"], independent=False, **pr)
    ob_s, h_s = _rwkv(hb, sb_shift.reshape(bs, 1, B_PROJ), _to_block_diag(sb_wkv), lp["rwkv"], independent=True,
                      n_streams=1, t=bs * ts, base_row=n_p)

    oc_p, s_p = _retention(hc, *tabs["rope_prompt"], zeros_state, tabs["ret"], lp["c_ln_w"], lp["c_ln_b"], **pr)
    oc_s, s_s = _retention(hc, *tabs["rope_sample"], _to_block_diag(sc), tabs["ret"], lp["c_ln_w"], lp["c_ln_b"], **sm)

    od_p = _attention(dq, dk, dv, None, lp["bias_table"], n_prev=D_PREV_CHUNKS, use_sink=False, **pr)
    od_s = _attention(dq, dk, dv, (cd_k.reshape(bs, -1, MIX_W), cd_v.reshape(bs, -1, MIX_W)), lp["bias_table"],
                      n_prev=D_PREV_CHUNKS, use_sink=False, **sm)

    mix = ((oa_p, oa_s), (ob_p, ob_s), (oc_p, oc_s), (od_p, od_s))
    return mix, (_from_block_diag(h_p), _from_block_diag(s_p)), (_from_block_diag(h_s), _from_block_diag(s_s))


def kernel(x_prompt, x_sample, cache_a_k, cache_a_v, state_b_shift, state_b_wkv, state_c, cache_d_k, cache_d_v,
           norm1_g, norm2_g, w_in, w_out, a_q_norm, a_k_norm, a_sinks, b_mu, b_w0, b_w2, b_a0, b_a2, b_g2,
           b_k_k, b_k_a, b_r_k, b_ln_w, b_ln_b, c_ln_w, c_ln_b, d_q_norm, d_k_norm, d_rel_bias,
           ffn_w1, ffn_w3, ffn_w2, moe_router, moe_w1, moe_w3, moe_w2):
    bp, tp, _ = x_prompt.shape
    bs, ts, _ = x_sample.shape
    assert ts == CHUNK
    n_p, n_s = bp * tp, bs * ts
    geom = (bp, tp, bs, ts)
    xa = x_prompt.reshape(n_p, D_MODEL)
    xb = x_sample.reshape(n_s, D_MODEL)

    tabs = {
        "ret": _retention_tables(),
        "rope_prompt": _rope_tables(np.arange(tp)),
        "rope_sample": _rope_tables(PAST_LEN + np.arange(ts)),
    }
    tile = lambda g: _row(jnp.tile(g, MIX_W // HEAD_DIM))

    expert_w = [tuple(_sc_to_bf16(w[j]) for w in (moe_w1, moe_w3, moe_w2)) for j in range(moe_w1.shape[0])]

    p_states, s_states = [], []
    for l in range(DEPTH):
        lp = {
            "sink_col": jnp.repeat(a_sinks[l].astype(F32), CHUNK).reshape(N_HEADS * CHUNK, 1),
            "bias_table": _relbias_table(d_rel_bias[l].astype(F32), (D_PREV_CHUNKS + 1) * CHUNK),
            "rwkv": (_row(b_mu[l]), _row(b_w0[l]), b_w2[l], _row(b_a0[l]), b_a2[l], b_g2[l], _row(b_k_k[l]),
                     _row(b_k_a[l]), _row(b_r_k[l]), _row(b_ln_w[l]), _row(b_ln_b[l])),
            "c_ln_w": _row(c_ln_w[l]), "c_ln_b": _row(c_ln_b[l]),
        }
        proj = _inproj(xa, xb, n_p + n_s, _row(norm1_g[l]), w_in[l].astype(BF16), tile(a_q_norm[l]),
                       _row(jnp.tile(a_k_norm[l], A_KV_W // HEAD_DIM)), tile(d_q_norm[l]), tile(d_k_norm[l]))
        _, ak, av, hb, _, _, dk, dv = proj
        caches = (cache_a_k[l], cache_a_v[l], state_b_shift[l], state_b_wkv[l], state_c[l], cache_d_k[l], cache_d_v[l])
        mix, (wkv_p, ret_p), (wkv_s, ret_s) = _mixers(proj, caches, lp, tabs, geom)
        j = l // 2
        if l % 2 == 0:
            xa = xb = _outproj_ffn(xa, xb, n_p + n_s, mix, w_out[l].astype(BF16), _row(norm2_g[l]),
                                   ffn_w1[j].astype(BF16), ffn_w3[j].astype(BF16), ffn_w2[j].astype(BF16))
        else:
            x1, xn_a, xn_b, *routing = _outproj_route(xa, xb, n_p + n_s, mix, w_out[l].astype(BF16),
                                                      _row(norm2_g[l]), _router_lanes(moe_router[j]))
            groups = ((0, n_p), (n_p, n_s)) if l == DEPTH - 1 else ((0, n_p + n_s),)
            outs = _moe(xn_a, xn_b, x1, routing, *expert_w[j], groups)
            xa, xb = (outs[0], outs[-1])

        wa = min(A_PREV_CHUNKS * CHUNK, tp)
        wd = min(D_PREV_CHUNKS * CHUNK, tp)
        tail = lambda a, w, heads: jnp.stack(
            [a[(b + 1) * tp - w:(b + 1) * tp] for b in range(bp)]).reshape(bp, w, heads, HEAD_DIM)
        last_rows = lambda a, t, first, count: jnp.concatenate(
            [a[first + (s + 1) * t - 1:first + (s + 1) * t] for s in range(count)], axis=0)
        p_states.append((tail(ak, wa, 2), tail(av, wa, 2), last_rows(hb, tp, 0, bp), wkv_p, ret_p,
                         tail(dk, wd, N_HEADS), tail(dv, wd, N_HEADS)))
        roll_in = lambda cache, new, heads: jnp.concatenate(
            [cache.astype(F32), new[n_p:].reshape(bs, ts, heads, HEAD_DIM)], axis=1)[:, -cache.shape[1]:]
        s_states.append((roll_in(cache_a_k[l], ak, 2), roll_in(cache_a_v[l], av, 2),
                         last_rows(hb, ts, n_p, bs), wkv_s, ret_s,
                         roll_in(cache_d_k[l], dk, N_HEADS), roll_in(cache_d_v[l], dv, N_HEADS)))

    if xa is xb:
        xa, xb = xa[:n_p], xa[n_p:]
    yp = xa.reshape(bp, tp, D_MODEL)
    ys = xb.reshape(bs, ts, D_MODEL)
    st = lambda group, i: jnp.stack([g[i] for g in group], axis=0)
    return (yp, ys,
            st(p_states, 0), st(p_states, 1), st(p_states, 2), st(p_states, 3), st(p_states, 4), st(p_states, 5), st(p_states, 6),
            st(s_states, 0), st(s_states, 1), st(s_states, 2), st(s_states, 3), st(s_states, 4), st(s_states, 5), st(s_states, 6))
```

```python
import functools

import jax
import jax.numpy as jnp
import numpy as np
from jax import lax
from jax.experimental import pallas as pl
from jax.experimental.pallas import tpu as pltpu
from jax.experimental.pallas import tpu_sc as plsc

F32 = jnp.float32
BF16 = jnp.bfloat16

D_MODEL = 1024
DEPTH = 2
PAST_LEN = 4096
CHUNK = 64
HEAD_DIM = 64
N_HEADS = 4
MIX_W = N_HEADS * HEAD_DIM
A_KV_W = 128
A_PREV_CHUNKS = 2
D_PREV_CHUNKS = 8
D_REL_CLIP = 128
B_PROJ = 1024
C_PROJ = 1024
IN_PROJ = 3328
B_GN_EPS = 64e-5
C_GN_EPS = 1e-6
NORM_EPS = 1e-6
ATTN_SCALE = 0.125
ROPE_BASE = 10000.0
D_FF = 2816
N_EXPERTS = 8
E_FF = 3584
NEG_BIG = -1e30

VMEM_LIMIT = 56 * 1024 * 1024

NN = ((1,), (0,))
NT = ((1,), (1,))
TN = ((0,), (0,))


def _dg(a, b, dims=NN):
    return lax.dot_general(a, b, (dims, ((), ())), preferred_element_type=F32)


def _parts(x, n):
    out = []
    r = x
    for i in range(n):
        p = r.astype(BF16)
        out.append(p)
        if i + 1 < n:
            r = r - p.astype(F32)
    return out


def _mm(a, b, dims=NN, passes=1):
    if passes == 1:
        return _dg(a.astype(BF16), b.astype(BF16), dims)
    ah, al = _parts(a, 2)
    bh, bl = _parts(b, 2)
    return _dg(ah, bh, dims) + (_dg(ah, bl, dims) + _dg(al, bh, dims))


def _mm_exact_rhs(a, b_bf, dims=NN, n=2):
    acc = None
    for p in _parts(a, n):
        t = _dg(p, b_bf, dims)
        acc = t if acc is None else acc + t
    return acc


def _iota2(shape, dim):
    return lax.broadcasted_iota(jnp.int32, shape, dim)


def _head_mask(rows, cols=MIX_W):
    return (_iota2((rows, cols), 0) >> 6) == (_iota2((rows, cols), 1) >> 6)


def _seg_matrix(width, value):
    m = _head_mask(width, width)
    return jnp.where(m, value, 0.0).astype(BF16)


def _tile4(z):
    return jnp.concatenate([z, z, z, z], axis=0)


def _fold4(z):
    return (z[0:64] + z[64:128]) + (z[128:192] + z[192:256])


def _sigmoid(x):
    return 1.0 / (1.0 + jnp.exp(-x))


def _cparams(sem):
    return pltpu.CompilerParams(dimension_semantics=sem, vmem_limit_bytes=VMEM_LIMIT)


IN_TM = 512


def _two_source_specs(xa, xb, n):
    na, nb = xa.shape[0] // IN_TM, xb.shape[0] // IN_TM
    spec_a = pl.BlockSpec((IN_TM, D_MODEL), lambda i: (jnp.minimum(i, na - 1), 0))
    spec_b = pl.BlockSpec((IN_TM, D_MODEL), lambda i: (jnp.clip(i - na, 0, nb - 1), 0))
    return na, n // IN_TM, spec_a, spec_b


def _inproj_kernel(xa_ref, xb_ref, g_ref, w_ref, aqg_ref, akg_ref, dqg_ref, dkg_ref,
                   aq_ref, ak_ref, av_ref, hb_ref, hc_ref, dq_ref, dk_ref, dv_ref, *, n_first):
    x = jnp.where(pl.program_id(0) < n_first, xa_ref[...], xb_ref[...])
    ms = jnp.mean(x * x, axis=-1, keepdims=True)
    xn = ((x * lax.rsqrt(ms + NORM_EPS)) * g_ref[...]).astype(BF16)
    seg = _seg_matrix(MIX_W, 1.0 / HEAD_DIM)

    def proj(lo, hi):
        return jnp.dot(xn, w_ref[:, lo:hi], preferred_element_type=F32)

    def head_rms(h, gain_ref):
        w = h.shape[-1]
        msq = _mm_exact_rhs(h * h, seg[:w, :w], n=1)
        return (h * lax.rsqrt(msq + NORM_EPS)) * gain_ref[...]

    aq_ref[...] = head_rms(proj(0, 256), aqg_ref)
    ak_ref[...] = head_rms(proj(256, 384), akg_ref)
    av_ref[...] = proj(384, 512)
    hb_ref[...] = proj(512, 1536)
    hc_ref[...] = proj(1536, 2560)
    dq_ref[...] = head_rms(proj(2560, 2816), dqg_ref)
    dk_ref[...] = head_rms(proj(2816, 3072), dkg_ref)
    dv_ref[...] = proj(3072, 3328)


def _inproj(xa, xb, n, g, w_bf, aqg, akg, dqg, dkg):
    na, nblk, spec_a, spec_b = _two_source_specs(xa, xb, n)
    widths = (256, 128, 128, B_PROJ, C_PROJ, 256, 256, 256)
    row = lambda w: pl.BlockSpec((IN_TM, w), lambda i: (i, 0))
    full = lambda a: pl.BlockSpec(a.shape, lambda i: (0,) * a.ndim)
    return pl.pallas_call(
        functools.partial(_inproj_kernel, n_first=na),
        grid=(nblk,),
        in_specs=[spec_a, spec_b, full(g), full(w_bf), full(aqg), full(akg), full(dqg), full(dkg)],
        out_specs=[row(w) for w in widths],
        out_shape=[jax.ShapeDtypeStruct((n, w), F32) for w in widths],
        compiler_params=_cparams(("parallel",)),
        name="inproj",
    )(xa, xb, g, w_bf, aqg, akg, dqg, dkg)


PACK_W = 256


def _pack_bf16_pairs(hi, lo):
    bits = lambda z: pltpu.bitcast(z.astype(BF16).astype(F32), jnp.int32)
    return bits(hi) | lax.shift_right_logical(bits(lo), jnp.full(lo.shape, 16, jnp.int32))


def _unpack_bf16_pairs(w):
    hi = pltpu.bitcast(w & jnp.int32(-65536), F32)
    lo = pltpu.bitcast(lax.shift_left(w, jnp.full(w.shape, 16, jnp.int32)), F32)
    return hi, lo


def _pack_rows(x):
    return (_pack_bf16_pairs(x[:, 0:PACK_W], x[:, PACK_W:2 * PACK_W]),
            _pack_bf16_pairs(x[:, 2 * PACK_W:3 * PACK_W], x[:, 3 * PACK_W:4 * PACK_W]))


def _unpack_rows(wa, wb):
    return jnp.concatenate(_unpack_bf16_pairs(wa) + _unpack_bf16_pairs(wb), axis=1)


ROUTER_LANES = 128


ROUTER_TERMS = 3


def _router_lanes(router):
    terms = _parts(router.astype(F32), ROUTER_TERMS)
    return jnp.pad(jnp.concatenate(terms, axis=1), ((0, 0), (0, ROUTER_LANES - ROUTER_TERMS * N_EXPERTS)))


def _route(xn_bf, router_bf, counts):
    split = _dg(xn_bf, router_bf)
    logits = split
    for k in range(1, ROUTER_TERMS):
        logits = logits + pltpu.roll(split, ROUTER_LANES - k * N_EXPERTS, 1)
    lane = _iota2(logits.shape, 1)
    logits = jnp.where(lane < N_EXPERTS, logits, NEG_BIG)
    m1 = jnp.max(logits, axis=-1, keepdims=True)
    i1 = jnp.min(jnp.where(logits == m1, lane, ROUTER_LANES), axis=-1, keepdims=True)
    rest = jnp.where(lane == i1, NEG_BIG, logits)
    m2 = jnp.max(rest, axis=-1, keepdims=True)
    i2 = jnp.min(jnp.where(rest == m2, lane, ROUTER_LANES), axis=-1, keepdims=True)
    e2 = jnp.exp(m2 - m1)
    den = 1.0 + e2
    gates = jnp.where(lane == i1, 1.0 / den, 0.0) + jnp.where(lane == i2, e2 / den, 0.0)
    sel = jnp.where(gates > 0.0, 1.0, 0.0)
    tm = sel.shape[0]
    before = jnp.where(_iota2((tm, tm), 1) < _iota2((tm, tm), 0), 1.0, 0.0).astype(BF16)
    rank = (_dg(before, sel.astype(BF16)) + counts).astype(jnp.int32)
    return gates, rank, counts + jnp.sum(sel, axis=0, keepdims=True)


def _mixed_residual(i, xa_ref, xb_ref, mix_refs, w_ref, g_ref, n_first_x, n_first_mix):
    x1 = jnp.where(i < n_first_x, xa_ref[...], xb_ref[...])
    for m in range(4):
        o = jnp.where(i < n_first_mix, mix_refs[2 * m][...], mix_refs[2 * m + 1][...])
        x1 = x1 + jnp.dot(o.astype(BF16), w_ref[m * MIX_W:(m + 1) * MIX_W, :], preferred_element_type=F32)
    ms = jnp.mean(x1 * x1, axis=-1, keepdims=True)
    return x1, (x1 * lax.rsqrt(ms + NORM_EPS)) * g_ref[...]


def _outproj_route_kernel(xa_ref, xb_ref, *refs, n_first_x, n_first_mix):
    mix_refs, (w_ref, g_ref, r_ref), outs = refs[:8], refs[8:11], refs[11:]
    x1_ref, pa_ref, pb_ref, gate_ref, rank_ref, cnt_ref, cnt_scr = outs
    i = pl.program_id(0)
    acc, xn = _mixed_residual(i, xa_ref, xb_ref, mix_refs, w_ref, g_ref, n_first_x, n_first_mix)

    @pl.when(i == 0)
    def _():
        cnt_scr[...] = jnp.zeros_like(cnt_scr)

    x1_ref[...] = acc
    pa_ref[...], pb_ref[...] = _pack_rows(xn)
    gates, rank, counts = _route(xn.astype(BF16), r_ref[...], cnt_scr[...])
    gate_ref[...] = gates
    rank_ref[...] = rank
    cnt_scr[...] = counts
    cnt_ref[...] = counts.astype(jnp.int32)


def _residual_specs(xa, xb, n, mix, index):
    na, nb = xa.shape[0] // IN_TM, xb.shape[0] // IN_TM
    nm_p, nm_s = mix[0][0].shape[0] // IN_TM, mix[0][1].shape[0] // IN_TM
    first = lambda w, cnt: pl.BlockSpec((IN_TM, w), index(lambda i: jnp.minimum(i, cnt - 1)))
    second = lambda w, skip, cnt: pl.BlockSpec((IN_TM, w), index(lambda i: jnp.clip(i - skip, 0, cnt - 1)))
    specs = [first(D_MODEL, na), second(D_MODEL, na, nb)] + [first(MIX_W, nm_p), second(MIX_W, nm_p, nm_s)] * 4
    return specs, [xa, xb, *[a for pair in mix for a in pair]], dict(n_first_x=na, n_first_mix=nm_p)


def _outproj_route(xa, xb, n, mix, w_bf, g2, router_bf):
    specs, operands, statics = _residual_specs(xa, xb, n, mix, lambda blk: (lambda i: (blk(i), 0)))
    row = lambda w: pl.BlockSpec((IN_TM, w), lambda i: (i, 0))
    full = lambda a: pl.BlockSpec(a.shape, lambda i: (0,) * a.ndim)
    return pl.pallas_call(
        functools.partial(_outproj_route_kernel, **statics),
        grid=(n // IN_TM,),
        in_specs=specs + [full(w_bf), full(g2), full(router_bf)],
        out_specs=[row(D_MODEL), row(PACK_W), row(PACK_W), row(ROUTER_LANES), row(ROUTER_LANES),
                   pl.BlockSpec((1, ROUTER_LANES), lambda i: (0, 0))],
        out_shape=[jax.ShapeDtypeStruct((n, D_MODEL), F32),
                   jax.ShapeDtypeStruct((n, PACK_W), jnp.int32), jax.ShapeDtypeStruct((n, PACK_W), jnp.int32),
                   jax.ShapeDtypeStruct((n, ROUTER_LANES), F32), jax.ShapeDtypeStruct((n, ROUTER_LANES), jnp.int32),
                   jax.ShapeDtypeStruct((1, ROUTER_LANES), jnp.int32)],
        scratch_shapes=[pltpu.VMEM((1, ROUTER_LANES), F32)],
        compiler_params=_cparams(("arbitrary",)),
        name="outproj_route",
    )(*operands, w_bf, g2, router_bf)


FFN_TF = 1408


def _outproj_ffn_kernel(xa_ref, xb_ref, *refs, n_first_x, n_first_mix):
    mix_refs, rest = refs[:8], refs[8:]
    w_ref, g_ref, w1_ref, w3_ref, w2_ref, o_ref, x1_scr, xn_scr = rest
    i, f = pl.program_id(0), pl.program_id(1)

    @pl.when(f == 0)
    def _():
        x1, xn = _mixed_residual(i, xa_ref, xb_ref, mix_refs, w_ref, g_ref, n_first_x, n_first_mix)
        x1_scr[...] = x1
        xn_scr[...] = xn.astype(BF16)

    xn = xn_scr[...]
    a = jnp.dot(xn, w1_ref[...], preferred_element_type=F32)
    b = jnp.dot(xn, w3_ref[...], preferred_element_type=F32)
    h = ((a * _sigmoid(a)) * b).astype(BF16)
    y = jnp.dot(h, w2_ref[...], preferred_element_type=F32)

    @pl.when(f == 0)
    def _():
        o_ref[...] = x1_scr[...] + y

    @pl.when(f != 0)
    def _():
        o_ref[...] += y


def _outproj_ffn(xa, xb, n, mix, w_bf, g2, w1_bf, w3_bf, w2_bf):
    specs, operands, statics = _residual_specs(xa, xb, n, mix, lambda blk: (lambda i, f: (blk(i), 0)))
    full = lambda a: pl.BlockSpec(a.shape, lambda i, f: (0,) * a.ndim)
    return pl.pallas_call(
        functools.partial(_outproj_ffn_kernel, **statics),
        grid=(n // IN_TM, D_FF // FFN_TF),
        in_specs=specs + [full(w_bf), full(g2),
                          pl.BlockSpec((D_MODEL, FFN_TF), lambda i, f: (0, f)),
                          pl.BlockSpec((D_MODEL, FFN_TF), lambda i, f: (0, f)),
                          pl.BlockSpec((FFN_TF, D_MODEL), lambda i, f: (f, 0))],
        out_specs=pl.BlockSpec((IN_TM, D_MODEL), lambda i, f: (i, 0)),
        out_shape=jax.ShapeDtypeStruct((n, D_MODEL), F32),
        scratch_shapes=[pltpu.VMEM((IN_TM, D_MODEL), F32), pltpu.VMEM((IN_TM, D_MODEL), BF16)],
        compiler_params=_cparams(("parallel", "arbitrary")),
        name="outproj_ffn",
    )(*operands, w_bf, g2, w1_bf, w3_bf, w2_bf)


MOE_R = 512
MOE_TF = 1792
SC_WINDOW = 128


def _moe_plan(gates, rank, counts, n):
    n_blocks = (2 * n) // MOE_R + N_EXPERTS + 1
    spare_row = (n_blocks - 1) * MOE_R
    sel = gates[:, :N_EXPERTS] > 0.0
    rank = rank[:, :N_EXPERTS]
    counts = counts[0, :N_EXPERTS]
    padded = ((counts + MOE_R - 1) // MOE_R) * MOE_R
    pad_end = jnp.cumsum(padded)
    pad_start = pad_end - padded
    pos = jnp.where(sel, pad_start[None, :] + rank, -1)
    order = jnp.cumsum(sel.astype(jnp.int32), axis=1)
    pick = lambda j: jnp.max(jnp.where(jnp.logical_and(sel, order == j), pos, -1), axis=1)
    to_row = lambda p: jnp.where(p >= 0, p, spare_row).astype(jnp.int32).reshape(1, n)
    block_expert = jnp.minimum(
        jnp.sum(pad_end[None, :] <= (jnp.arange(n_blocks) * MOE_R)[:, None], axis=1), N_EXPERTS - 1)
    return dict(n_blocks=n_blocks, pos0=to_row(pick(1)), pos1=to_row(pick(2)),
                block_expert=block_expert.astype(jnp.int32), n_used=(pad_end[-1:] // MOE_R).astype(jnp.int32))


def _sc_mesh():
    return plsc.VectorSubcoreMesh(core_axis_name="core", subcore_axis_name="subcore")


SC_CAST_BLOCK = (64, 512)
SC_CAST_OP = (8, 16)


def _sc_to_bf16(w):
    rows, cols = w.size // w.shape[-1], w.shape[-1]

    @functools.partial(pl.kernel, out_type=jax.ShapeDtypeStruct((rows, cols), BF16), mesh=_sc_mesh(),
                       scratch_types=[],
                       cost_estimate=pl.CostEstimate(flops=rows * cols, transcendentals=0, bytes_accessed=6 * rows * cols))
    def cast(x_hbm, o_hbm):
        def body(x_vmem, o_vmem):
            @pl.loop(0, SC_CAST_BLOCK[0], step=SC_CAST_OP[0])
            def _(r0):
                @pl.loop(0, SC_CAST_BLOCK[1], step=SC_CAST_OP[1])
                def _(c0):
                    at = (pl.ds(r0, SC_CAST_OP[0]), pl.ds(c0, SC_CAST_OP[1]))
                    o_vmem.at[*at][...] = x_vmem.at[*at][...].astype(BF16)

        pltpu.emit_pipeline(
            body,
            grid=(rows // SC_CAST_BLOCK[0], cols // SC_CAST_BLOCK[1]),
            in_specs=[pl.BlockSpec(SC_CAST_BLOCK, lambda i, j: (i, j))],
            out_specs=[pl.BlockSpec(SC_CAST_BLOCK, lambda i, j: (i, j))],
            core_axis_name=("core", "subcore"),
            dimension_semantics=(pltpu.PARALLEL, pltpu.PARALLEL),
        )(x_hbm, o_hbm)

    return cast(w.astype(F32).reshape(rows, cols)).reshape(w.shape)


def _sc_scatter_rows(table, idx_lists, n_rows):
    n, cols = table.shape
    k = len(idx_lists)

    @functools.partial(pl.kernel, out_type=jax.ShapeDtypeStruct((n_rows, cols), table.dtype), mesh=_sc_mesh())
    def scatter(x_hbm, *rest):
        i_hbms, o_hbm = rest[:k], rest[k]

        def body(x_vmem, *i_vmems):
            for i_vmem in i_vmems:
                pltpu.sync_copy(x_vmem, o_hbm.at[i_vmem.at[0]])

        pltpu.emit_pipeline(
            body,
            grid=(n // SC_WINDOW,),
            in_specs=[pl.BlockSpec((SC_WINDOW, cols), lambda i: (i, 0))]
            + [pl.BlockSpec((1, SC_WINDOW), lambda i: (0, i))] * k,
            out_specs=[],
            core_axis_name=("core", "subcore"),
            dimension_semantics=(pltpu.PARALLEL,),
        )(x_hbm, *i_hbms)

    return scatter(table, *idx_lists)


def _sc_gather_rows(table, idx):
    n = idx.shape[1]
    cols = table.shape[1]

    @functools.partial(pl.kernel, out_type=jax.ShapeDtypeStruct((n, cols), table.dtype), mesh=_sc_mesh())
    def gather(x_hbm, i_hbm, o_hbm):
        def body(i_vmem, o_vmem):
            pltpu.sync_copy(x_hbm.at[i_vmem.at[0]], o_vmem)

        pltpu.emit_pipeline(
            body,
            grid=(n // SC_WINDOW,),
            in_specs=[pl.BlockSpec((1, SC_WINDOW), lambda i: (0, i))],
            out_specs=[pl.BlockSpec((SC_WINDOW, cols), lambda i: (i, 0))],
            core_axis_name=("core", "subcore"),
            dimension_semantics=(pltpu.PARALLEL,),
        )(i_hbm, o_hbm)

    return gather(table, idx)


def _moe_expert_kernel(be_ref, nu_ref, xa_ref, xb_ref, gs_ref, w1_ref, w3_ref, w2_ref, oa_ref, ob_ref, acc_ref):
    j, f = pl.program_id(0), pl.program_id(1)
    used = j < nu_ref[0]

    @pl.when(used)
    def _():
        x = _unpack_rows(xa_ref[...], xb_ref[...]).astype(BF16)
        a = jnp.dot(x, w1_ref[...], preferred_element_type=F32)
        b = jnp.dot(x, w3_ref[...], preferred_element_type=F32)
        h = ((a * _sigmoid(a)) * b).astype(BF16)
        y = jnp.dot(h, w2_ref[...], preferred_element_type=F32)

        @pl.when(f == 0)
        def _():
            acc_ref[...] = y

        @pl.when(f != 0)
        def _():
            acc_ref[...] += y

    @pl.when(f == pl.num_programs(1) - 1)
    def _():
        lane = _iota2(gs_ref.shape, 1)
        g = jnp.sum(jnp.where(lane == be_ref[j], gs_ref[...], 0.0), axis=1, keepdims=True)
        oa_ref[...], ob_ref[...] = _pack_rows(jnp.where(used, acc_ref[...] * g, 0.0))


def _moe_experts(plan, xs_a, xs_b, gs, w1_bf, w3_bf, w2_bf):
    n_blocks = plan["n_blocks"]
    half = pl.BlockSpec((MOE_R, PACK_W), lambda j, f, be, nu: (j, 0))
    grid_spec = pltpu.PrefetchScalarGridSpec(
        num_scalar_prefetch=2,
        grid=(n_blocks, E_FF // MOE_TF),
        in_specs=[half, half,
                  pl.BlockSpec((MOE_R, ROUTER_LANES), lambda j, f, be, nu: (j, 0)),
                  pl.BlockSpec((None, D_MODEL, MOE_TF), lambda j, f, be, nu: (be[j], 0, f)),
                  pl.BlockSpec((None, D_MODEL, MOE_TF), lambda j, f, be, nu: (be[j], 0, f)),
                  pl.BlockSpec((None, MOE_TF, D_MODEL), lambda j, f, be, nu: (be[j], f, 0))],
        out_specs=[half, half],
        scratch_shapes=[pltpu.VMEM((MOE_R, D_MODEL), F32)])
    return pl.pallas_call(
        _moe_expert_kernel,
        grid_spec=grid_spec,
        out_shape=[jax.ShapeDtypeStruct((n_blocks * MOE_R, PACK_W), jnp.int32)] * 2,
        compiler_params=_cparams(("arbitrary", "arbitrary")),
        name="moe_experts",
    )(plan["block_expert"], plan["n_used"], xs_a, xs_b, gs, w1_bf, w3_bf, w2_bf)


def _moe_combine_kernel(x1_ref, a0_ref, b0_ref, a1_ref, b1_ref, o_ref):
    o_ref[...] = (x1_ref[...] + _unpack_rows(a0_ref[...], b0_ref[...])) + _unpack_rows(a1_ref[...], b1_ref[...])


def _moe_combine(x1, picked, row0, n_rows):
    base = row0 // IN_TM
    row = lambda w: pl.BlockSpec((IN_TM, w), lambda i: (base + i, 0))
    return pl.pallas_call(
        _moe_combine_kernel,
        grid=(n_rows // IN_TM,),
        in_specs=[row(D_MODEL)] + [row(PACK_W)] * 4,
        out_specs=pl.BlockSpec((IN_TM, D_MODEL), lambda i: (i, 0)),
        out_shape=jax.ShapeDtypeStruct((n_rows, D_MODEL), F32),
        compiler_params=_cparams(("parallel",)),
        name="moe_combine",
    )(x1, *picked)


def _moe(xn_a, xn_b, x1, routing, w1_bf, w3_bf, w2_bf, row_groups):
    gates, rank, counts = routing
    plan = _moe_plan(gates, rank, counts, x1.shape[0])
    n_rows = plan["n_blocks"] * MOE_R
    idx = (plan["pos0"], plan["pos1"])
    xs_a = _sc_scatter_rows(xn_a, idx, n_rows)
    xs_b = _sc_scatter_rows(xn_b, idx, n_rows)
    gs = _sc_scatter_rows(gates, idx, n_rows)
    os_a, os_b = _moe_experts(plan, xs_a, xs_b, gs, w1_bf, w3_bf, w2_bf)
    picked = [_sc_gather_rows(t, p) for p in idx for t in (os_a, os_b)]
    return [_moe_combine(x1, picked, row0, rows) for row0, rows in row_groups]


def _relbias_kernel(rb_ref, o_ref, *, nk):
    h = pl.program_id(0)
    q = _iota2((CHUNK, nk), 0)
    r = _iota2((CHUNK, nk), 1)
    idx = jnp.clip(q - (r - (nk - CHUNK)), -D_REL_CLIP, D_REL_CLIP) + D_REL_CLIP

    def body(j, acc):
        return jnp.where(idx == j, rb_ref[h, j], acc)

    o_ref[...] = lax.fori_loop(0, 2 * D_REL_CLIP + 1, body, jnp.zeros((CHUNK, nk), F32))


def _relbias_table(rel_bias, nk):
    return pl.pallas_call(
        functools.partial(_relbias_kernel, nk=nk),
        grid=(N_HEADS,),
        in_specs=[pl.BlockSpec(memory_space=pltpu.SMEM)],
        out_specs=pl.BlockSpec((CHUNK, nk), lambda h: (h, 0)),
        out_shape=jax.ShapeDtypeStruct((N_HEADS * CHUNK, nk), F32),
        name="relbias",
    )(rel_bias)


ATTN_QB = 8
ATTN_GROUP = 8


def _attn_kernel(q_ref, kp_ref, kc_ref, vp_ref, vc_ref, x_ref, *rest, qb, n_prev, use_sink, mask_first):
    o_ref, kbuf, vbuf = rest[-3:]
    i = pl.program_id(1)
    p_rows = kp_ref.shape[0]
    nk = (n_prev + 1) * CHUNK
    wk = kp_ref.shape[1]

    kbuf[0:p_rows, :] = kp_ref[...].astype(BF16)
    kbuf[p_rows:, :] = kc_ref[...].astype(BF16)
    vbuf[0:p_rows, :] = vp_ref[...].astype(BF16)
    vbuf[p_rows:, :] = vc_ref[...].astype(BF16)

    hmask = _head_mask(N_HEADS * CHUNK)
    extra = x_ref[...]
    grouped = wk != MIX_W
    low = _iota2((CHUNK, A_KV_W), 1) < HEAD_DIM

    def stack_queries(qj):
        if not grouped:
            return jnp.where(hmask, _tile4(qj), 0.0)
        shifted = pltpu.roll(qj, MIX_W - HEAD_DIM, 1)[:, :A_KV_W]
        return jnp.concatenate([jnp.where(low, qj[:, :A_KV_W], 0.0), jnp.where(low, shifted, 0.0),
                                jnp.where(low, 0.0, shifted), jnp.where(low, 0.0, qj[:, A_KV_W:])], axis=0)

    def unstack_outputs(o_all):
        if not grouped:
            return _fold4(jnp.where(hmask, o_all, 0.0))
        b0, b1, b2, b3 = (o_all[h * CHUNK:(h + 1) * CHUNK] for h in range(N_HEADS))
        left = jnp.where(low, b0, 0.0) + pltpu.roll(jnp.where(low, b1, 0.0), HEAD_DIM, 1)
        right = pltpu.roll(jnp.where(low, 0.0, b2), HEAD_DIM, 1) + jnp.where(low, 0.0, b3)
        return jnp.concatenate([left, right], axis=1)

    def scores(j):
        base = p_rows + (j - n_prev) * CHUNK
        qs = stack_queries(q_ref[pl.ds(j * CHUNK, CHUNK), :] * ATTN_SCALE).astype(BF16)
        s = _dg(qs, kbuf[pl.ds(base, nk), :], NT)
        if not use_sink:
            s = s + extra
        if mask_first and base < p_rows:
            krow = base + _iota2(s.shape, 1)
            s = jnp.where(jnp.logical_and(i == 0, krow < p_rows), NEG_BIG, s)
        return s

    def weights(s):
        m = jnp.max(s, axis=-1, keepdims=True)
        if use_sink:
            m = jnp.maximum(m, extra)
        e = jnp.exp(s - m)
        den = jnp.sum(e, axis=-1, keepdims=True)
        if use_sink:
            den = den + jnp.exp(extra - m)
        return e.astype(BF16), 1.0 / den

    def output(j, e, inv_den):
        base = p_rows + (j - n_prev) * CHUNK
        o_all = _dg(e, vbuf[pl.ds(base, nk), :]) * inv_den
        o_ref[pl.ds(j * CHUNK, CHUNK), :] = unstack_outputs(o_all)

    for j0 in range(0, qb, ATTN_GROUP):
        group = range(j0, min(j0 + ATTN_GROUP, qb))
        ss = [scores(j) for j in group]
        ws = [weights(s) for s in ss]
        for j, (e, inv_den) in zip(group, ws):
            output(j, e, inv_den)


def _attention(q, k, v, prev, extra, *, n_prev, use_sink, n_streams, t, base_row):
    wk = k.shape[-1]
    if prev is None:
        qb = ATTN_QB
        rows = qb * CHUNK
        nblk = t // rows
        base = base_row // rows
        prev_spec = pl.BlockSpec((rows, wk), lambda s, i: (base + s * nblk + jnp.maximum(i - 1, 0), 0))
        k_prev, v_prev, p_rows, mask_first = k, v, rows, True
    else:
        qb, rows, nblk = t // CHUNK, t, 1
        base = base_row // rows
        k_prev, v_prev = prev
        p_rows = k_prev.shape[1]
        prev_spec = pl.BlockSpec((None, p_rows, wk), lambda s, i: (s, 0, 0))
        mask_first = False
    cur = lambda w: pl.BlockSpec((rows, w), lambda s, i: (base + s * nblk + i, 0))
    kern = functools.partial(_attn_kernel, qb=qb, n_prev=n_prev, use_sink=use_sink, mask_first=mask_first)
    return pl.pallas_call(
        kern,
        grid=(n_streams, nblk),
        in_specs=[cur(MIX_W), prev_spec, cur(wk), prev_spec, cur(wk),
                  pl.BlockSpec(extra.shape, lambda s, i: (0, 0))],
        out_specs=pl.BlockSpec((rows, MIX_W), lambda s, i: (s * nblk + i, 0)),
        out_shape=jax.ShapeDtypeStruct((n_streams * t, MIX_W), F32),
        scratch_shapes=[pltpu.VMEM((p_rows + rows, wk), BF16), pltpu.VMEM((p_rows + rows, wk), BF16)],
        compiler_params=_cparams(("parallel", "arbitrary")),
        name="attn_sink" if use_sink else "attn_bias",
    )(q, k_prev, k, v_prev, v, extra)


def _head_layer_norm(o, seg_mean_bf, w, b, eps):
    mu = _mm_exact_rhs(o, seg_mean_bf)
    d = o - mu
    var = _mm_exact_rhs(d * d, seg_mean_bf)
    return (d * lax.rsqrt(var + eps)) * w + b


def _ret_kernel(hc_ref, cos_ref, sin_ref, s0_ref, dstack_ref, qsc_ref, ksc_ref, gam_ref, lnw_ref, lnb_ref,
                *rest, qb):
    o_ref, sout_ref, s_scr = rest[-3:]
    i = pl.program_id(1)

    @pl.when(i == 0)
    def _():
        s_scr[...] = s0_ref[...]

    hmask = _head_mask(N_HEADS * CHUNK)
    seg_mean = _seg_matrix(MIX_W, 1.0 / HEAD_DIM)
    rows = qb * CHUNK
    first_half = (_iota2((rows, MIX_W), 1) & (HEAD_DIM - 1)) < (HEAD_DIM // 2)
    cos = jnp.concatenate([cos_ref[...]] * (MIX_W // ROPE_W), axis=1)
    sin = jnp.concatenate([sin_ref[...]] * (MIX_W // ROPE_W), axis=1)

    def rope(x):
        partner = jnp.where(first_half, pltpu.roll(x, MIX_W - HEAD_DIM // 2, 1), pltpu.roll(x, HEAD_DIM // 2, 1))
        return x * cos + partner * sin

    q = rope(hc_ref[:, 0:256])
    k = rope(hc_ref[:, 256:512]) * ATTN_SCALE
    v_bf = hc_ref[:, 512:768].astype(BF16)
    state = s_scr[...]
    outs = []
    for j in range(qb):
        sl = slice(j * CHUNK, (j + 1) * CHUNK)
        qj, kj, vj = q[sl], k[sl], v_bf[sl]
        qs = jnp.where(hmask, _tile4(qj), 0.0).astype(BF16)
        sc = _dg(qs, kj.astype(BF16), NT) * dstack_ref[...]
        intra = _fold4(jnp.where(hmask, _dg(sc.astype(BF16), vj), 0.0))
        inter = _dg((qj * qsc_ref[...]).astype(BF16), state.astype(BF16))
        kv = _dg((kj * ksc_ref[...]).astype(BF16), vj, TN)
        state = gam_ref[...] * state + jnp.where(hmask, kv, 0.0)
        outs.append(intra + inter)
    s_scr[...] = state
    sout_ref[...] = state
    y = _head_layer_norm(jnp.concatenate(outs, axis=0), seg_mean, lnw_ref[...], lnb_ref[...], C_GN_EPS)
    g = hc_ref[:, 768:1024]
    o_ref[...] = y * (g * _sigmoid(g))


def _retention(hc, cos, sin, s0_bd, tabs, lnw, lnb, *, n_streams, t, base_row):
    qb = min(8, t // CHUNK)
    rows = qb * CHUNK
    nblk = t // rows
    base = base_row // rows
    dstack, qsc, ksc, gam = tabs
    full = lambda a: pl.BlockSpec(a.shape, lambda s, i: (0,) * a.ndim)
    cur = lambda w: pl.BlockSpec((rows, w), lambda s, i: (base + s * nblk + i, 0))
    state = pl.BlockSpec((None, MIX_W, MIX_W), lambda s, i: (s, 0, 0))
    return pl.pallas_call(
        functools.partial(_ret_kernel, qb=qb),
        grid=(n_streams, nblk),
        in_specs=[cur(C_PROJ),
                  pl.BlockSpec((rows, ROPE_W), lambda s, i: (i, 0)),
                  pl.BlockSpec((rows, ROPE_W), lambda s, i: (i, 0)),
                  state, full(dstack), full(qsc), full(ksc), full(gam), full(lnw), full(lnb)],
        out_specs=[pl.BlockSpec((rows, MIX_W), lambda s, i: (s * nblk + i, 0)), state],
        out_shape=[jax.ShapeDtypeStruct((n_streams * t, MIX_W), F32),
                   jax.ShapeDtypeStruct((n_streams, MIX_W, MIX_W), F32)],
        scratch_shapes=[pltpu.VMEM((MIX_W, MIX_W), F32)],
        compiler_params=_cparams(("parallel", "arbitrary")),
        name="retention",
    )(hc, cos, sin, s0_bd, dstack, qsc, ksc, gam, lnw, lnb)


def _retention_tables():
    gamma = 1.0 - 2.0 ** (-5.0 - np.arange(N_HEADS, dtype=np.float64))
    t = np.arange(CHUNK)
    diff = t[:, None] - t[None, :]
    dmat = np.where(diff >= 0, gamma[:, None, None] ** np.maximum(diff, 0), 0.0)
    dstack = dmat.reshape(N_HEADS * CHUNK, CHUNK)
    lanes = lambda per_head: np.repeat(per_head, HEAD_DIM, axis=-1)
    qsc = lanes(gamma[None, :] ** (t + 1)[:, None])
    ksc = lanes(gamma[None, :] ** (CHUNK - 1 - t)[:, None])
    gam = np.broadcast_to(lanes(gamma ** CHUNK)[:, None], (MIX_W, MIX_W))
    return tuple(jnp.asarray(a, F32) for a in (dstack, qsc, ksc, gam))


ROPE_W = 2 * HEAD_DIM


def _rope_tables(pos):
    half = HEAD_DIM // 2
    theta = np.float32(1.0) / (np.float32(ROPE_BASE) ** np.linspace(0.0, 1.0, half, dtype=np.float32))
    ang = np.asarray(pos, np.float32)[:, None] * theta[None, :]
    cos, sin = np.cos(ang), np.sin(ang)
    reps = ROPE_W // HEAD_DIM
    cos_t = np.tile(np.concatenate([cos, cos], axis=-1), (1, reps))
    sin_t = np.tile(np.concatenate([-sin, sin], axis=-1), (1, reps))
    return jnp.asarray(cos_t, F32), jnp.asarray(sin_t, F32)


DECAY_SCALE = 0.6065306597126334
RWKV_CB = 4
N_LEVELS = 6
MASK_HEAD, MASK_STRICT, MASK_INCL, MASK_LEVEL0 = 0, 1, 2, 3


def _rwkv_masks():
    n4 = N_HEADS * CHUNK
    ri = np.arange(n4)[:, None]
    ci = np.arange(n4)[None, :]
    head = (ri >> 6) == (ci >> 6)
    tabs = [head, head & ((ci & 63) < (ri & 63)), head & ((ci & 63) <= (ri & 63))]
    for log_m in range(N_LEVELS):
        same = (ri >> (log_m + 1)) == (ci >> (log_m + 1))
        tabs.append(same & (((ri >> log_m) & 1) == 1) & (((ci >> log_m) & 1) == 0))
    return jnp.asarray(np.stack(tabs), BF16)


def _rwkv_kernel(hb_ref, shift0_ref, h0_ref, masks_ref, mu_ref, w0_ref, w2_ref, a0_ref, a2_ref, g2_ref,
                 kk_ref, ka_ref, rk_ref, lnw_ref, lnb_ref, *rest, cb, independent):
    o_ref, hout_ref, h_scr, shift_scr = rest[-4:]
    c = pl.program_id(1)
    rows = cb * CHUNK
    xb = hb_ref[...]
    row = _iota2(xb.shape, 0)
    prev = pltpu.roll(xb, 1, 0)
    if independent:
        for j in range(cb):
            prev = jnp.where(row == j * CHUNK, shift0_ref[j], prev)
    else:
        @pl.when(c == 0)
        def _():
            h_scr[...] = h0_ref[0]
            shift_scr[...] = shift0_ref[0]

        prev = jnp.where(row == 0, shift_scr[...], prev)
        shift_scr[...] = xb[rows - 1:rows, :]
    xs = xb + mu_ref[...] * (prev - xb)
    r = xs[:, 0:256]
    k = xs[:, 256:512]
    v = xs[:, 512:768]
    xw = xs[:, 768:832]
    xa = xs[:, 832:896]
    xg = xs[:, 896:1024]

    z = w0_ref[...] + _mm(jnp.tanh(xw), w2_ref[...], passes=3)
    lw = -DECAY_SCALE * _sigmoid(z)
    a_gate = _sigmoid(a0_ref[...] + _mm(xa, a2_ref[...], passes=3))
    gate = _mm(_sigmoid(xg), g2_ref[...], passes=1)

    seg_sum = _seg_matrix(MIX_W, 1.0)
    seg_mean = _seg_matrix(MIX_W, 1.0 / HEAD_DIM)
    kkn = k * kk_ref[...]
    norm = jnp.sqrt(_mm_exact_rhs(kkn * kkn, seg_sum))
    kk = kkn / jnp.maximum(norm, 1e-12)
    kf = k * (1.0 + (a_gate - 1.0) * ka_ref[...])

    tt = _iota2((rows, rows), 0)
    ss = _iota2((rows, rows), 1)
    tril = jnp.where(jnp.logical_and(ss <= tt, (ss >> 6) == (tt >> 6)), 1.0, 0.0).astype(BF16)
    lw_parts = _parts(lw, 3)
    cum = _dg(tril, lw_parts[0]) + (_dg(tril, lw_parts[1]) + _dg(tril, lw_parts[2]))
    w_inv = jnp.exp(-cum)
    rho = (r * jnp.exp(cum)).astype(BF16)
    alpha = (-kk * jnp.exp(cum - lw)).astype(BF16)
    beta = ((kk * a_gate) * w_inv).astype(BF16)
    kappa = (kf * w_inv).astype(BF16)
    v_bf = v.astype(BF16)

    hmask = masks_ref[MASK_HEAD]
    n4 = N_HEADS * CHUNK
    eye = jnp.where(_iota2((n4, n4), 0) == _iota2((n4, n4), 1), 1.0, 0.0)

    pre, a_bfs, t_invs = [], [], []
    for j in range(cb):
        sl = slice(j * CHUNK, (j + 1) * CHUNK)
        bd = lambda zz: _tile4(zz[sl]) * hmask
        al_bd, be_bd, ka_bd, rh_bd, v_bd = bd(alpha), bd(beta), bd(kappa), bd(rho), bd(v_bf)
        a_bf = _dg(al_bd, be_bd, NT).astype(BF16) * masks_ref[MASK_STRICT]
        a_ak = _dg(al_bd, ka_bd, NT).astype(BF16) * masks_ref[MASK_STRICT]
        b_rb = _dg(rh_bd, be_bd, NT).astype(BF16) * masks_ref[MASK_INCL]
        b_rk = _dg(rh_bd, ka_bd, NT).astype(BF16) * masks_ref[MASK_INCL]
        x0 = _dg(a_ak, v_bd)
        y0 = _dg(b_rk, v_bd)
        sn0 = _dg(v_bd, ka_bd, TN)
        w_chunk = jnp.exp(cum[(j + 1) * CHUNK - 1:(j + 1) * CHUNK, :])
        a_bfs.append(a_bf)
        t_invs.append(eye + (a_bf * masks_ref[MASK_LEVEL0]).astype(F32))
        pre.append([al_bd, be_bd, rh_bd, b_rb, None, x0, y0, sn0, w_chunk])
    for lvl in range(1, N_LEVELS):
        t_bfs = [t.astype(BF16) for t in t_invs]
        e_mats = [_dg(a_bfs[j] * masks_ref[MASK_LEVEL0 + lvl], t_bfs[j]) for j in range(cb)]
        t_invs = [t_invs[j] + _dg(t_bfs[j], e_mats[j].astype(BF16)) for j in range(cb)]
    for j in range(cb):
        pre[j][4] = t_invs[j].astype(BF16)

    ys = []
    h = None if independent else h_scr[...]
    for j in range(cb):
        al_bd, be_bd, rh_bd, b_rb, t_bf, x0, y0, sn0, w_chunk = pre[j]
        h0 = h0_ref[j] if independent else h
        h0_bf = h0.astype(BF16)
        x_mat = _dg(al_bd, h0_bf, NT) + x0
        u_bf = _dg(t_bf, x_mat.astype(BF16)).astype(BF16)
        y_bd = _dg(rh_bd, h0_bf, NT) + _dg(b_rb, u_bf) + y0
        h_new = (h0 + _dg(u_bf, be_bd, TN) + sn0) * w_chunk
        ys.append(_fold4(y_bd))
        if independent:
            hout_ref[j] = h_new
        else:
            h = h_new
    if not independent:
        h_scr[...] = h
        hout_ref[0] = h

    y = _head_layer_norm(jnp.concatenate(ys, axis=0), seg_mean, lnw_ref[...], lnb_ref[...], B_GN_EPS)
    bonus = _mm_exact_rhs(r * kf * rk_ref[...], seg_sum) * v
    o_ref[...] = (y + bonus) * gate


def _rwkv(hb, shift0, h0_bd, params, *, independent, n_streams, t, base_row):
    cb = RWKV_CB
    rows = cb * CHUNK
    nblk = t // rows
    base = base_row // rows
    masks = _rwkv_masks()
    full = lambda a: pl.BlockSpec(a.shape, lambda s, c: (0,) * a.ndim)
    if independent:
        assert n_streams == 1
        st_map = lambda s, c: (c, 0, 0)
        n_state, st_blk = t // CHUNK, cb
    else:
        st_map = lambda s, c: (s, 0, 0)
        n_state, st_blk = n_streams, 1
    cur = lambda w: pl.BlockSpec((rows, w), lambda s, c: (base + s * nblk + c, 0))
    return pl.pallas_call(
        functools.partial(_rwkv_kernel, cb=cb, independent=independent),
        grid=(n_streams, nblk),
        in_specs=[cur(B_PROJ),
                  pl.BlockSpec((st_blk, 1, B_PROJ), st_map),
                  pl.BlockSpec((st_blk, MIX_W, MIX_W), st_map),
                  full(masks)] + [full(p) for p in params],
        out_specs=[pl.BlockSpec((rows, MIX_W), lambda s, c: (s * nblk + c, 0)),
                   pl.BlockSpec((st_blk, MIX_W, MIX_W), st_map)],
        out_shape=[jax.ShapeDtypeStruct((n_streams * t, MIX_W), F32),
                   jax.ShapeDtypeStruct((n_state, MIX_W, MIX_W), F32)],
        scratch_shapes=[pltpu.VMEM((MIX_W, MIX_W), F32), pltpu.VMEM((1, B_PROJ), F32)],
        compiler_params=_cparams(("parallel", "arbitrary")),
        name="rwkv7",
    )(hb, shift0, h0_bd, masks, *params)


def _to_block_diag(s):
    eye = jnp.eye(N_HEADS, dtype=s.dtype)
    out = s[:, :, :, None, :] * eye[None, :, None, :, None]
    return out.reshape(s.shape[0], MIX_W, MIX_W)


def _from_block_diag(m):
    b = m.reshape(m.shape[0], N_HEADS, HEAD_DIM, N_HEADS, HEAD_DIM)
    return jnp.stack([b[:, h, :, h, :] for h in range(N_HEADS)], axis=1)


def _row(p):
    return p.reshape(1, -1).astype(F32)


def _mixers(proj, caches, lp, tabs, geom):
    aq, ak, av, hb, hc, dq, dk, dv = proj
    bp, tp, bs, ts = geom
    n_p = bp * tp
    ca_k, ca_v, sb_shift, sb_wkv, sc, cd_k, cd_v = caches
    pr = dict(n_streams=bp, t=tp, base_row=0)
    sm = dict(n_streams=bs, t=ts, base_row=n_p)
    zeros_state = jnp.zeros((bp, MIX_W, MIX_W), F32)

    oa_p = _attention(aq, ak, av, None, lp["sink_col"], n_prev=A_PREV_CHUNKS, use_sink=True, **pr)
    oa_s = _attention(aq, ak, av, (ca_k.reshape(bs, -1, A_KV_W), ca_v.reshape(bs, -1, A_KV_W)), lp["sink_col"],
                      n_prev=A_PREV_CHUNKS, use_sink=True, **sm)

    ob_p, h_p = _rwkv(hb, jnp.zeros((bp, 1, B_PROJ), F32), zeros_state, lp["rwkv"], independent=False, **pr)
    ob_s, h_s = _rwkv(hb, sb_shift.reshape(bs, 1, B_PROJ), _to_block_diag(sb_wkv), lp["rwkv"], independent=True,
                      n_streams=1, t=bs * ts, base_row=n_p)

    oc_p, s_p = _retention(hc, *tabs["rope_prompt"], zeros_state, tabs["ret"], lp["c_ln_w"], lp["c_ln_b"], **pr)
    oc_s, s_s = _retention(hc, *tabs["rope_sample"], _to_block_diag(sc), tabs["ret"], lp["c_ln_w"], lp["c_ln_b"], **sm)

    od_p = _attention(dq, dk, dv, None, lp["bias_table"], n_prev=D_PREV_CHUNKS, use_sink=False, **pr)
    od_s = _attention(dq, dk, dv, (cd_k.reshape(bs, -1, MIX_W), cd_v.reshape(bs, -1, MIX_W)), lp["bias_table"],
                      n_prev=D_PREV_CHUNKS, use_sink=False, **sm)

    mix = ((oa_p, oa_s), (ob_p, ob_s), (oc_p, oc_s), (od_p, od_s))
    return mix, (_from_block_diag(h_p), _from_block_diag(s_p)), (_from_block_diag(h_s), _from_block_diag(s_s))


def kernel(x_prompt, x_sample, cache_a_k, cache_a_v, state_b_shift, state_b_wkv, state_c, cache_d_k, cache_d_v,
           norm1_g, norm2_g, w_in, w_out, a_q_norm, a_k_norm, a_sinks, b_mu, b_w0, b_w2, b_a0, b_a2, b_g2,
           b_k_k, b_k_a, b_r_k, b_ln_w, b_ln_b, c_ln_w, c_ln_b, d_q_norm, d_k_norm, d_rel_bias,
           ffn_w1, ffn_w3, ffn_w2, moe_router, moe_w1, moe_w3, moe_w2):
    bp, tp, _ = x_prompt.shape
    bs, ts, _ = x_sample.shape
    assert ts == CHUNK
    n_p, n_s = bp * tp, bs * ts
    geom = (bp, tp, bs, ts)
    xa = x_prompt.reshape(n_p, D_MODEL)
    xb = x_sample.reshape(n_s, D_MODEL)

    tabs = {
        "ret": _retention_tables(),
        "rope_prompt": _rope_tables(np.arange(tp)),
        "rope_sample": _rope_tables(PAST_LEN + np.arange(ts)),
    }
    tile = lambda g: _row(jnp.tile(g, MIX_W // HEAD_DIM))

    expert_w = [tuple(_sc_to_bf16(w[j]) for w in (moe_w1, moe_w3, moe_w2)) for j in range(moe_w1.shape[0])]

    p_states, s_states = [], []
    for l in range(DEPTH):
        lp = {
            "sink_col": jnp.repeat(a_sinks[l].astype(F32), CHUNK).reshape(N_HEADS * CHUNK, 1),
            "bias_table": _relbias_table(d_rel_bias[l].astype(F32), (D_PREV_CHUNKS + 1) * CHUNK),
            "rwkv": (_row(b_mu[l]), _row(b_w0[l]), b_w2[l], _row(b_a0[l]), b_a2[l], b_g2[l], _row(b_k_k[l]),
                     _row(b_k_a[l]), _row(b_r_k[l]), _row(b_ln_w[l]), _row(b_ln_b[l])),
            "c_ln_w": _row(c_ln_w[l]), "c_ln_b": _row(c_ln_b[l]),
        }
        proj = _inproj(xa, xb, n_p + n_s, _row(norm1_g[l]), w_in[l].astype(BF16), tile(a_q_norm[l]),
                       _row(jnp.tile(a_k_norm[l], A_KV_W // HEAD_DIM)), tile(d_q_norm[l]), tile(d_k_norm[l]))
        _, ak, av, hb, _, _, dk, dv = proj
        caches = (cache_a_k[l], cache_a_v[l], state_b_shift[l], state_b_wkv[l], state_c[l], cache_d_k[l], cache_d_v[l])
        mix, (wkv_p, ret_p), (wkv_s, ret_s) = _mixers(proj, caches, lp, tabs, geom)
        j = l // 2
        if l % 2 == 0:
            xa = xb = _outproj_ffn(xa, xb, n_p + n_s, mix, w_out[l].astype(BF16), _row(norm2_g[l]),
                                   ffn_w1[j].astype(BF16), ffn_w3[j].astype(BF16), ffn_w2[j].astype(BF16))
        else:
            x1, xn_a, xn_b, *routing = _outproj_route(xa, xb, n_p + n_s, mix, w_out[l].astype(BF16),
                                                      _row(norm2_g[l]), _router_lanes(moe_router[j]))
            groups = ((0, n_p), (n_p, n_s)) if l == DEPTH - 1 else ((0, n_p + n_s),)
            outs = _moe(xn_a, xn_b, x1, routing, *expert_w[j], groups)
            xa, xb = (outs[0], outs[-1])

        wa = min(A_PREV_CHUNKS * CHUNK, tp)
        wd = min(D_PREV_CHUNKS * CHUNK, tp)
        tail = lambda a, w, heads: jnp.stack(
            [a[(b + 1) * tp - w:(b + 1) * tp] for b in range(bp)]).reshape(bp, w, heads, HEAD_DIM)
        last_rows = lambda a, t, first, count: jnp.concatenate(
            [a[first + (s + 1) * t - 1:first + (s + 1) * t] for s in range(count)], axis=0)
        p_states.append((tail(ak, wa, 2), tail(av, wa, 2), last_rows(hb, tp, 0, bp), wkv_p, ret_p,
                         tail(dk, wd, N_HEADS), tail(dv, wd, N_HEADS)))
        roll_in = lambda cache, new, heads: jnp.concatenate(
            [cache.astype(F32), new[n_p:].reshape(bs, ts, heads, HEAD_DIM)], axis=1)[:, -cache.shape[1]:]
        s_states.append((roll_in(cache_a_k[l], ak, 2), roll_in(cache_a_v[l], av, 2),
                         last_rows(hb, ts, n_p, bs), wkv_s, ret_s,
                         roll_in(cache_d_k[l], dk, N_HEADS), roll_in(cache_d_v[l], dv, N_HEADS)))

    if xa is xb:
        xa, xb = xa[:n_p], xa[n_p:]
    yp = xa.reshape(bp, tp, D_MODEL)
    ys = xb.reshape(bs, ts, D_MODEL)
    st = lambda group, i: jnp.stack([g[i] for g in group], axis=0)
    return (yp, ys,
            st(p_states, 0), st(p_states, 1), st(p_states, 2), st(p_states, 3), st(p_states, 4), st(p_states, 5), st(p_states, 6),
            st(s_states, 0), st(s_states, 1), st(s_states, 2), st(s_states, 3), st(s_states, 4), st(s_states, 5), st(s_states, 6))
```

```python
import functools

import jax
import jax.numpy as jnp
import numpy as np
from jax import lax
from jax.experimental import pallas as pl
from jax.experimental.pallas import tpu as pltpu
from jax.experimental.pallas import tpu_sc as plsc

F32 = jnp.float32
BF16 = jnp.bfloat16

D_MODEL = 1024
DEPTH = 2
PAST_LEN = 4096
CHUNK = 64
HEAD_DIM = 64
N_HEADS = 4
MIX_W = N_HEADS * HEAD_DIM
A_KV_W = 128
A_PREV_CHUNKS = 2
D_PREV_CHUNKS = 8
D_REL_CLIP = 128
B_PROJ = 1024
C_PROJ = 1024
IN_PROJ = 3328
B_GN_EPS = 64e-5
C_GN_EPS = 1e-6
NORM_EPS = 1e-6
ATTN_SCALE = 0.125
ROPE_BASE = 10000.0
D_FF = 2816
N_EXPERTS = 8
E_FF = 3584
NEG_BIG = -1e30

VMEM_LIMIT = 56 * 1024 * 1024

NN = ((1,), (0,))
NT = ((1,), (1,))
TN = ((0,), (0,))


def _dg(a, b, dims=NN):
    return lax.dot_general(a, b, (dims, ((), ())), preferred_element_type=F32)


def _parts(x, n):
    out = []
    r = x
    for i in range(n):
        p = r.astype(BF16)
        out.append(p)
        if i + 1 < n:
            r = r - p.astype(F32)
    return out


def _mm(a, b, dims=NN, passes=1):
    if passes == 1:
        return _dg(a.astype(BF16), b.astype(BF16), dims)
    ah, al = _parts(a, 2)
    bh, bl = _parts(b, 2)
    return _dg(ah, bh, dims) + (_dg(ah, bl, dims) + _dg(al, bh, dims))


def _mm_exact_rhs(a, b_bf, dims=NN, n=2):
    acc = None
    for p in _parts(a, n):
        t = _dg(p, b_bf, dims)
        acc = t if acc is None else acc + t
    return acc


def _iota2(shape, dim):
    return lax.broadcasted_iota(jnp.int32, shape, dim)


def _head_mask(rows, cols=MIX_W):
    return (_iota2((rows, cols), 0) >> 6) == (_iota2((rows, cols), 1) >> 6)


def _seg_matrix(width, value):
    m = _head_mask(width, width)
    return jnp.where(m, value, 0.0).astype(BF16)


def _tile4(z):
    return jnp.concatenate([z, z, z, z], axis=0)


def _fold4(z):
    return (z[0:64] + z[64:128]) + (z[128:192] + z[192:256])


def _heads_to_block_diag(state_ref):
    rows = []
    for h in range(N_HEADS):
        pieces = [jnp.zeros((HEAD_DIM, HEAD_DIM), F32)] * N_HEADS
        pieces[h] = state_ref[h]
        rows.append(jnp.concatenate(pieces, axis=1))
    return jnp.concatenate(rows, axis=0)


def _block_diag_to_heads(m, state_ref):
    for h in range(N_HEADS):
        state_ref[h] = m[h * HEAD_DIM:(h + 1) * HEAD_DIM, h * HEAD_DIM:(h + 1) * HEAD_DIM]


def _sigmoid(x):
    return 1.0 / (1.0 + jnp.exp(-x))


def _cparams(sem):
    return pltpu.CompilerParams(dimension_semantics=sem, vmem_limit_bytes=VMEM_LIMIT)


IN_TM = 512


def _two_source_specs(xa, xb, n):
    na, nb = xa.shape[0] // IN_TM, xb.shape[0] // IN_TM
    spec_a = pl.BlockSpec((IN_TM, D_MODEL), lambda i: (jnp.minimum(i, na - 1), 0))
    spec_b = pl.BlockSpec((IN_TM, D_MODEL), lambda i: (jnp.clip(i - na, 0, nb - 1), 0))
    return na, n // IN_TM, spec_a, spec_b


def _inproj_kernel(xa_ref, xb_ref, g_ref, w_ref, aqg_ref, akg_ref, dqg_ref, dkg_ref,
                   aq_ref, ak_ref, av_ref, hb_ref, hc_ref, dq_ref, dk_ref, dv_ref, *, n_first):
    x = jnp.where(pl.program_id(0) < n_first, xa_ref[...], xb_ref[...])
    ms = jnp.mean(x * x, axis=-1, keepdims=True)
    xn = ((x * lax.rsqrt(ms + NORM_EPS)) * g_ref[...]).astype(BF16)
    seg = _seg_matrix(MIX_W, 1.0 / HEAD_DIM)

    def proj(lo, hi):
        return jnp.dot(xn, w_ref[:, lo:hi], preferred_element_type=F32)

    def head_rms(h, gain_ref):
        w = h.shape[-1]
        msq = _mm_exact_rhs(h * h, seg[:w, :w], n=1)
        return (h * lax.rsqrt(msq + NORM_EPS)) * gain_ref[...]

    aq_ref[...] = head_rms(proj(0, 256), aqg_ref)
    ak_ref[...] = head_rms(proj(256, 384), akg_ref)
    av_ref[...] = proj(384, 512)
    hb_ref[...] = proj(512, 1536)
    hc_ref[...] = proj(1536, 2560)
    dq_ref[...] = head_rms(proj(2560, 2816), dqg_ref)
    dk_ref[...] = head_rms(proj(2816, 3072), dkg_ref)
    dv_ref[...] = proj(3072, 3328)


def _inproj(xa, xb, n, g, w_bf, aqg, akg, dqg, dkg):
    na, nblk, spec_a, spec_b = _two_source_specs(xa, xb, n)
    widths = (256, 128, 128, B_PROJ, C_PROJ, 256, 256, 256)
    row = lambda w: pl.BlockSpec((IN_TM, w), lambda i: (i, 0))
    full = lambda a: pl.BlockSpec(a.shape, lambda i: (0,) * a.ndim)
    return pl.pallas_call(
        functools.partial(_inproj_kernel, n_first=na),
        grid=(nblk,),
        in_specs=[spec_a, spec_b, full(g), full(w_bf), full(aqg), full(akg), full(dqg), full(dkg)],
        out_specs=[row(w) for w in widths],
        out_shape=[jax.ShapeDtypeStruct((n, w), F32) for w in widths],
        compiler_params=_cparams(("parallel",)),
        name="inproj",
    )(xa, xb, g, w_bf, aqg, akg, dqg, dkg)


PACK_W = 256


def _pack_bf16_pairs(hi, lo):
    bits = lambda z: pltpu.bitcast(z.astype(BF16).astype(F32), jnp.int32)
    return bits(hi) | lax.shift_right_logical(bits(lo), jnp.full(lo.shape, 16, jnp.int32))


def _unpack_bf16_pairs(w):
    hi = pltpu.bitcast(w & jnp.int32(-65536), F32)
    lo = pltpu.bitcast(lax.shift_left(w, jnp.full(w.shape, 16, jnp.int32)), F32)
    return hi, lo


def _pack_rows(x):
    return (_pack_bf16_pairs(x[:, 0:PACK_W], x[:, PACK_W:2 * PACK_W]),
            _pack_bf16_pairs(x[:, 2 * PACK_W:3 * PACK_W], x[:, 3 * PACK_W:4 * PACK_W]))


def _unpack_rows(wa, wb):
    return jnp.concatenate(_unpack_bf16_pairs(wa) + _unpack_bf16_pairs(wb), axis=1)


ROUTER_LANES = 128


ROUTER_TERMS = 3


def _router_lanes(router):
    terms = _parts(router.astype(F32), ROUTER_TERMS)
    return jnp.pad(jnp.concatenate(terms, axis=1), ((0, 0), (0, ROUTER_LANES - ROUTER_TERMS * N_EXPERTS)))


def _route(xn_bf, router_bf, counts):
    split = _dg(xn_bf, router_bf)
    logits = split
    for k in range(1, ROUTER_TERMS):
        logits = logits + pltpu.roll(split, ROUTER_LANES - k * N_EXPERTS, 1)
    lane = _iota2(logits.shape, 1)
    logits = jnp.where(lane < N_EXPERTS, logits, NEG_BIG)
    m1 = jnp.max(logits, axis=-1, keepdims=True)
    i1 = jnp.min(jnp.where(logits == m1, lane, ROUTER_LANES), axis=-1, keepdims=True)
    rest = jnp.where(lane == i1, NEG_BIG, logits)
    m2 = jnp.max(rest, axis=-1, keepdims=True)
    i2 = jnp.min(jnp.where(rest == m2, lane, ROUTER_LANES), axis=-1, keepdims=True)
    e2 = jnp.exp(m2 - m1)
    den = 1.0 + e2
    gates = jnp.where(lane == i1, 1.0 / den, 0.0) + jnp.where(lane == i2, e2 / den, 0.0)
    sel = jnp.where(gates > 0.0, 1.0, 0.0)
    tm = sel.shape[0]
    before = jnp.where(_iota2((tm, tm), 1) < _iota2((tm, tm), 0), 1.0, 0.0).astype(BF16)
    rank = (_dg(before, sel.astype(BF16)) + counts).astype(jnp.int32)
    return gates, rank, counts + jnp.sum(sel, axis=0, keepdims=True)


def _mixed_residual(i, xa_ref, xb_ref, mix_refs, w_ref, g_ref, n_first_x, n_first_mix):
    x1 = jnp.where(i < n_first_x, xa_ref[...], xb_ref[...])
    for m in range(4):
        o = jnp.where(i < n_first_mix, mix_refs[2 * m][...], mix_refs[2 * m + 1][...])
        x1 = x1 + jnp.dot(o.astype(BF16), w_ref[m * MIX_W:(m + 1) * MIX_W, :], preferred_element_type=F32)
    ms = jnp.mean(x1 * x1, axis=-1, keepdims=True)
    return x1, (x1 * lax.rsqrt(ms + NORM_EPS)) * g_ref[...]


def _outproj_route_kernel(xa_ref, xb_ref, *refs, n_first_x, n_first_mix):
    mix_refs, (w_ref, g_ref, r_ref), outs = refs[:8], refs[8:11], refs[11:]
    x1_ref, pa_ref, pb_ref, gate_ref, rank_ref, cnt_ref, cnt_scr = outs
    i = pl.program_id(0)
    acc, xn = _mixed_residual(i, xa_ref, xb_ref, mix_refs, w_ref, g_ref, n_first_x, n_first_mix)

    @pl.when(i == 0)
    def _():
        cnt_scr[...] = jnp.zeros_like(cnt_scr)

    x1_ref[...] = acc
    pa_ref[...], pb_ref[...] = _pack_rows(xn)
    gates, rank, counts = _route(xn.astype(BF16), r_ref[...], cnt_scr[...])
    gate_ref[...] = gates
    rank_ref[...] = rank
    cnt_scr[...] = counts
    cnt_ref[...] = counts.astype(jnp.int32)


def _residual_specs(xa, xb, n, mix, index):
    na, nb = xa.shape[0] // IN_TM, xb.shape[0] // IN_TM
    nm_p, nm_s = mix[0][0].shape[0] // IN_TM, mix[0][1].shape[0] // IN_TM
    first = lambda w, cnt: pl.BlockSpec((IN_TM, w), index(lambda i: jnp.minimum(i, cnt - 1)))
    second = lambda w, skip, cnt: pl.BlockSpec((IN_TM, w), index(lambda i: jnp.clip(i - skip, 0, cnt - 1)))
    specs = [first(D_MODEL, na), second(D_MODEL, na, nb)] + [first(MIX_W, nm_p), second(MIX_W, nm_p, nm_s)] * 4
    return specs, [xa, xb, *[a for pair in mix for a in pair]], dict(n_first_x=na, n_first_mix=nm_p)


def _outproj_route(xa, xb, n, mix, w_bf, g2, router_bf):
    specs, operands, statics = _residual_specs(xa, xb, n, mix, lambda blk: (lambda i: (blk(i), 0)))
    row = lambda w: pl.BlockSpec((IN_TM, w), lambda i: (i, 0))
    full = lambda a: pl.BlockSpec(a.shape, lambda i: (0,) * a.ndim)
    return pl.pallas_call(
        functools.partial(_outproj_route_kernel, **statics),
        grid=(n // IN_TM,),
        in_specs=specs + [full(w_bf), full(g2), full(router_bf)],
        out_specs=[row(D_MODEL), row(PACK_W), row(PACK_W), row(ROUTER_LANES), row(ROUTER_LANES),
                   pl.BlockSpec((1, ROUTER_LANES), lambda i: (0, 0))],
        out_shape=[jax.ShapeDtypeStruct((n, D_MODEL), F32),
                   jax.ShapeDtypeStruct((n, PACK_W), jnp.int32), jax.ShapeDtypeStruct((n, PACK_W), jnp.int32),
                   jax.ShapeDtypeStruct((n, ROUTER_LANES), F32), jax.ShapeDtypeStruct((n, ROUTER_LANES), jnp.int32),
                   jax.ShapeDtypeStruct((1, ROUTER_LANES), jnp.int32)],
        scratch_shapes=[pltpu.VMEM((1, ROUTER_LANES), F32)],
        compiler_params=_cparams(("arbitrary",)),
        name="outproj_route",
    )(*operands, w_bf, g2, router_bf)


FFN_TF = 1408


def _outproj_ffn_kernel(xa_ref, xb_ref, *refs, n_first_x, n_first_mix):
    mix_refs, rest = refs[:8], refs[8:]
    w_ref, g_ref, w1_ref, w3_ref, w2_ref, o_ref, x1_scr, xn_scr = rest
    i, f = pl.program_id(0), pl.program_id(1)

    @pl.when(f == 0)
    def _():
        x1, xn = _mixed_residual(i, xa_ref, xb_ref, mix_refs, w_ref, g_ref, n_first_x, n_first_mix)
        x1_scr[...] = x1
        xn_scr[...] = xn.astype(BF16)

    xn = xn_scr[...]
    a = jnp.dot(xn, w1_ref[...], preferred_element_type=F32)
    b = jnp.dot(xn, w3_ref[...], preferred_element_type=F32)
    h = ((a * _sigmoid(a)) * b).astype(BF16)
    y = jnp.dot(h, w2_ref[...], preferred_element_type=F32)

    @pl.when(f == 0)
    def _():
        o_ref[...] = x1_scr[...] + y

    @pl.when(f != 0)
    def _():
        o_ref[...] += y


def _outproj_ffn(xa, xb, n, mix, w_bf, g2, w1_bf, w3_bf, w2_bf):
    specs, operands, statics = _residual_specs(xa, xb, n, mix, lambda blk: (lambda i, f: (blk(i), 0)))
    full = lambda a: pl.BlockSpec(a.shape, lambda i, f: (0,) * a.ndim)
    return pl.pallas_call(
        functools.partial(_outproj_ffn_kernel, **statics),
        grid=(n // IN_TM, D_FF // FFN_TF),
        in_specs=specs + [full(w_bf), full(g2),
                          pl.BlockSpec((D_MODEL, FFN_TF), lambda i, f: (0, f)),
                          pl.BlockSpec((D_MODEL, FFN_TF), lambda i, f: (0, f)),
                          pl.BlockSpec((FFN_TF, D_MODEL), lambda i, f: (f, 0))],
        out_specs=pl.BlockSpec((IN_TM, D_MODEL), lambda i, f: (i, 0)),
        out_shape=jax.ShapeDtypeStruct((n, D_MODEL), F32),
        scratch_shapes=[pltpu.VMEM((IN_TM, D_MODEL), F32), pltpu.VMEM((IN_TM, D_MODEL), BF16)],
        compiler_params=_cparams(("parallel", "arbitrary")),
        name="outproj_ffn",
    )(*operands, w_bf, g2, w1_bf, w3_bf, w2_bf)


MOE_R = 512
MOE_TF = 1792
SC_WINDOW = 128


def _moe_plan(gates, rank, counts, n):
    n_blocks = (2 * n) // MOE_R + N_EXPERTS + 1
    spare_row = (n_blocks - 1) * MOE_R
    sel = gates[:, :N_EXPERTS] > 0.0
    rank = rank[:, :N_EXPERTS]
    counts = counts[0, :N_EXPERTS]
    padded = ((counts + MOE_R - 1) // MOE_R) * MOE_R
    pad_end = jnp.cumsum(padded)
    pad_start = pad_end - padded
    pos = jnp.where(sel, pad_start[None, :] + rank, -1)
    order = jnp.cumsum(sel.astype(jnp.int32), axis=1)
    pick = lambda j: jnp.max(jnp.where(jnp.logical_and(sel, order == j), pos, -1), axis=1)
    to_row = lambda p: jnp.where(p >= 0, p, spare_row).astype(jnp.int32).reshape(1, n)
    block_expert = jnp.minimum(
        jnp.sum(pad_end[None, :] <= (jnp.arange(n_blocks) * MOE_R)[:, None], axis=1), N_EXPERTS - 1)
    return dict(n_blocks=n_blocks, pos0=to_row(pick(1)), pos1=to_row(pick(2)),
                block_expert=block_expert.astype(jnp.int32), n_used=(pad_end[-1:] // MOE_R).astype(jnp.int32))


def _sc_mesh():
    return plsc.VectorSubcoreMesh(core_axis_name="core", subcore_axis_name="subcore")


def _sc_scatter_rows(table, idx_lists, n_rows):
    n, cols = table.shape
    k = len(idx_lists)

    @functools.partial(pl.kernel, out_type=jax.ShapeDtypeStruct((n_rows, cols), table.dtype), mesh=_sc_mesh())
    def scatter(x_hbm, *rest):
        i_hbms, o_hbm = rest[:k], rest[k]

        def body(x_vmem, *i_vmems):
            for i_vmem in i_vmems:
                pltpu.sync_copy(x_vmem, o_hbm.at[i_vmem.at[0]])

        pltpu.emit_pipeline(
            body,
            grid=(n // SC_WINDOW,),
            in_specs=[pl.BlockSpec((SC_WINDOW, cols), lambda i: (i, 0))]
            + [pl.BlockSpec((1, SC_WINDOW), lambda i: (0, i))] * k,
            out_specs=[],
            core_axis_name=("core", "subcore"),
            dimension_semantics=(pltpu.PARALLEL,),
        )(x_hbm, *i_hbms)

    return scatter(table, *idx_lists)


def _sc_gather_rows(table, idx):
    n = idx.shape[1]
    cols = table.shape[1]

    @functools.partial(pl.kernel, out_type=jax.ShapeDtypeStruct((n, cols), table.dtype), mesh=_sc_mesh())
    def gather(x_hbm, i_hbm, o_hbm):
        def body(i_vmem, o_vmem):
            pltpu.sync_copy(x_hbm.at[i_vmem.at[0]], o_vmem)

        pltpu.emit_pipeline(
            body,
            grid=(n // SC_WINDOW,),
            in_specs=[pl.BlockSpec((1, SC_WINDOW), lambda i: (0, i))],
            out_specs=[pl.BlockSpec((SC_WINDOW, cols), lambda i: (i, 0))],
            core_axis_name=("core", "subcore"),
            dimension_semantics=(pltpu.PARALLEL,),
        )(i_hbm, o_hbm)

    return gather(table, idx)


def _moe_expert_kernel(be_ref, nu_ref, xa_ref, xb_ref, gs_ref, w1_ref, w3_ref, w2_ref, oa_ref, ob_ref, acc_ref):
    j, f = pl.program_id(0), pl.program_id(1)
    used = j < nu_ref[0]

    @pl.when(used)
    def _():
        x = _unpack_rows(xa_ref[...], xb_ref[...]).astype(BF16)
        a = jnp.dot(x, w1_ref[...], preferred_element_type=F32)
        b = jnp.dot(x, w3_ref[...], preferred_element_type=F32)
        h = ((a * _sigmoid(a)) * b).astype(BF16)
        y = jnp.dot(h, w2_ref[...], preferred_element_type=F32)

        @pl.when(f == 0)
        def _():
            acc_ref[...] = y

        @pl.when(f != 0)
        def _():
            acc_ref[...] += y

    @pl.when(f == pl.num_programs(1) - 1)
    def _():
        lane = _iota2(gs_ref.shape, 1)
        g = jnp.sum(jnp.where(lane == be_ref[j], gs_ref[...], 0.0), axis=1, keepdims=True)
        oa_ref[...], ob_ref[...] = _pack_rows(jnp.where(used, acc_ref[...] * g, 0.0))


def _moe_experts(plan, xs_a, xs_b, gs, w1_bf, w3_bf, w2_bf):
    n_blocks = plan["n_blocks"]
    half = pl.BlockSpec((MOE_R, PACK_W), lambda j, f, be, nu: (j, 0))
    grid_spec = pltpu.PrefetchScalarGridSpec(
        num_scalar_prefetch=2,
        grid=(n_blocks, E_FF // MOE_TF),
        in_specs=[half, half,
                  pl.BlockSpec((MOE_R, ROUTER_LANES), lambda j, f, be, nu: (j, 0)),
                  pl.BlockSpec((None, D_MODEL, MOE_TF), lambda j, f, be, nu: (be[j], 0, f)),
                  pl.BlockSpec((None, D_MODEL, MOE_TF), lambda j, f, be, nu: (be[j], 0, f)),
                  pl.BlockSpec((None, MOE_TF, D_MODEL), lambda j, f, be, nu: (be[j], f, 0))],
        out_specs=[half, half],
        scratch_shapes=[pltpu.VMEM((MOE_R, D_MODEL), F32)])
    return pl.pallas_call(
        _moe_expert_kernel,
        grid_spec=grid_spec,
        out_shape=[jax.ShapeDtypeStruct((n_blocks * MOE_R, PACK_W), jnp.int32)] * 2,
        compiler_params=_cparams(("arbitrary", "arbitrary")),
        name="moe_experts",
    )(plan["block_expert"], plan["n_used"], xs_a, xs_b, gs, w1_bf, w3_bf, w2_bf)


def _moe_combine_kernel(x1_ref, a0_ref, b0_ref, a1_ref, b1_ref, o_ref):
    o_ref[...] = (x1_ref[...] + _unpack_rows(a0_ref[...], b0_ref[...])) + _unpack_rows(a1_ref[...], b1_ref[...])


def _moe_combine(x1, picked, row0, n_rows):
    base = row0 // IN_TM
    row = lambda w: pl.BlockSpec((IN_TM, w), lambda i: (base + i, 0))
    return pl.pallas_call(
        _moe_combine_kernel,
        grid=(n_rows // IN_TM,),
        in_specs=[row(D_MODEL)] + [row(PACK_W)] * 4,
        out_specs=pl.BlockSpec((IN_TM, D_MODEL), lambda i: (i, 0)),
        out_shape=jax.ShapeDtypeStruct((n_rows, D_MODEL), F32),
        compiler_params=_cparams(("parallel",)),
        name="moe_combine",
    )(x1, *picked)


def _moe(xn_a, xn_b, x1, routing, w1_bf, w3_bf, w2_bf, row_groups):
    gates, rank, counts = routing
    plan = _moe_plan(gates, rank, counts, x1.shape[0])
    n_rows = plan["n_blocks"] * MOE_R
    idx = (plan["pos0"], plan["pos1"])
    xs_a = _sc_scatter_rows(xn_a, idx, n_rows)
    xs_b = _sc_scatter_rows(xn_b, idx, n_rows)
    gs = _sc_scatter_rows(gates, idx, n_rows)
    os_a, os_b = _moe_experts(plan, xs_a, xs_b, gs, w1_bf, w3_bf, w2_bf)
    picked = [_sc_gather_rows(t, p) for p in idx for t in (os_a, os_b)]
    return [_moe_combine(x1, picked, row0, rows) for row0, rows in row_groups]


def _relbias_kernel(rb_ref, o_ref, *, nk):
    h = pl.program_id(0)
    q = _iota2((CHUNK, nk), 0)
    r = _iota2((CHUNK, nk), 1)
    idx = jnp.clip(q - (r - (nk - CHUNK)), -D_REL_CLIP, D_REL_CLIP) + D_REL_CLIP

    def body(j, acc):
        return jnp.where(idx == j, rb_ref[h, j], acc)

    o_ref[...] = lax.fori_loop(0, 2 * D_REL_CLIP + 1, body, jnp.zeros((CHUNK, nk), F32))


def _relbias_table(rel_bias, nk):
    return pl.pallas_call(
        functools.partial(_relbias_kernel, nk=nk),
        grid=(N_HEADS,),
        in_specs=[pl.BlockSpec(memory_space=pltpu.SMEM)],
        out_specs=pl.BlockSpec((CHUNK, nk), lambda h: (h, 0)),
        out_shape=jax.ShapeDtypeStruct((N_HEADS * CHUNK, nk), F32),
        name="relbias",
    )(rel_bias)


ATTN_QB = 8
ATTN_GROUP = 8


def _attn_kernel(q_ref, kp_ref, kc_ref, vp_ref, vc_ref, x_ref, *rest, qb, n_prev, use_sink, mask_first):
    o_ref, kbuf, vbuf = rest[-3:]
    i = pl.program_id(1)
    p_rows = kp_ref.shape[0]
    nk = (n_prev + 1) * CHUNK
    wk = kp_ref.shape[1]

    kbuf[0:p_rows, :] = kp_ref[...].astype(BF16)
    kbuf[p_rows:, :] = kc_ref[...].astype(BF16)
    vbuf[0:p_rows, :] = vp_ref[...].astype(BF16)
    vbuf[p_rows:, :] = vc_ref[...].astype(BF16)

    hmask = _head_mask(N_HEADS * CHUNK)
    extra = x_ref[...]
    grouped = wk != MIX_W
    low = _iota2((CHUNK, A_KV_W), 1) < HEAD_DIM

    def stack_queries(qj):
        if not grouped:
            return jnp.where(hmask, _tile4(qj), 0.0)
        shifted = pltpu.roll(qj, MIX_W - HEAD_DIM, 1)[:, :A_KV_W]
        return jnp.concatenate([jnp.where(low, qj[:, :A_KV_W], 0.0), jnp.where(low, shifted, 0.0),
                                jnp.where(low, 0.0, shifted), jnp.where(low, 0.0, qj[:, A_KV_W:])], axis=0)

    def unstack_outputs(o_all):
        if not grouped:
            return _fold4(jnp.where(hmask, o_all, 0.0))
        b0, b1, b2, b3 = (o_all[h * CHUNK:(h + 1) * CHUNK] for h in range(N_HEADS))
        left = jnp.where(low, b0, 0.0) + pltpu.roll(jnp.where(low, b1, 0.0), HEAD_DIM, 1)
        right = pltpu.roll(jnp.where(low, 0.0, b2), HEAD_DIM, 1) + jnp.where(low, 0.0, b3)
        return jnp.concatenate([left, right], axis=1)

    def scores(j):
        base = p_rows + (j - n_prev) * CHUNK
        qs = stack_queries(q_ref[pl.ds(j * CHUNK, CHUNK), :] * ATTN_SCALE).astype(BF16)
        s = _dg(qs, kbuf[pl.ds(base, nk), :], NT)
        if not use_sink:
            s = s + extra
        if mask_first and base < p_rows:
            krow = base + _iota2(s.shape, 1)
            s = jnp.where(jnp.logical_and(i == 0, krow < p_rows), NEG_BIG, s)
        return s

    def weights(s):
        m = jnp.max(s, axis=-1, keepdims=True)
        if use_sink:
            m = jnp.maximum(m, extra)
        e = jnp.exp(s - m)
        den = jnp.sum(e, axis=-1, keepdims=True)
        if use_sink:
            den = den + jnp.exp(extra - m)
        return e.astype(BF16), 1.0 / den

    def output(j, e, inv_den):
        base = p_rows + (j - n_prev) * CHUNK
        o_all = _dg(e, vbuf[pl.ds(base, nk), :]) * inv_den
        o_ref[pl.ds(j * CHUNK, CHUNK), :] = unstack_outputs(o_all)

    for j0 in range(0, qb, ATTN_GROUP):
        group = range(j0, min(j0 + ATTN_GROUP, qb))
        ss = [scores(j) for j in group]
        ws = [weights(s) for s in ss]
        for j, (e, inv_den) in zip(group, ws):
            output(j, e, inv_den)


def _attention(q, k, v, prev, extra, *, n_prev, use_sink, n_streams, t, base_row):
    wk = k.shape[-1]
    if prev is None:
        qb = ATTN_QB
        rows = qb * CHUNK
        nblk = t // rows
        base = base_row // rows
        prev_spec = pl.BlockSpec((rows, wk), lambda s, i: (base + s * nblk + jnp.maximum(i - 1, 0), 0))
        k_prev, v_prev, p_rows, mask_first = k, v, rows, True
    else:
        qb, rows, nblk = t // CHUNK, t, 1
        base = base_row // rows
        k_prev, v_prev = prev
        p_rows = k_prev.shape[1]
        prev_spec = pl.BlockSpec((None, p_rows, wk), lambda s, i: (s, 0, 0))
        mask_first = False
    cur = lambda w: pl.BlockSpec((rows, w), lambda s, i: (base + s * nblk + i, 0))
    kern = functools.partial(_attn_kernel, qb=qb, n_prev=n_prev, use_sink=use_sink, mask_first=mask_first)
    return pl.pallas_call(
        kern,
        grid=(n_streams, nblk),
        in_specs=[cur(MIX_W), prev_spec, cur(wk), prev_spec, cur(wk),
                  pl.BlockSpec(extra.shape, lambda s, i: (0, 0))],
        out_specs=pl.BlockSpec((rows, MIX_W), lambda s, i: (s * nblk + i, 0)),
        out_shape=jax.ShapeDtypeStruct((n_streams * t, MIX_W), F32),
        scratch_shapes=[pltpu.VMEM((p_rows + rows, wk), BF16), pltpu.VMEM((p_rows + rows, wk), BF16)],
        compiler_params=_cparams(("parallel", "arbitrary")),
        name="attn_sink" if use_sink else "attn_bias",
    )(q, k_prev, k, v_prev, v, extra)


def _head_layer_norm(o, seg_mean_bf, w, b, eps):
    mu = _mm_exact_rhs(o, seg_mean_bf)
    d = o - mu
    var = _mm_exact_rhs(d * d, seg_mean_bf)
    return (d * lax.rsqrt(var + eps)) * w + b


def _ret_kernel(hc_ref, cos_ref, sin_ref, s0_ref, dstack_ref, qsc_ref, ksc_ref, gam_ref, lnw_ref, lnb_ref,
                *rest, qb):
    o_ref, sout_ref, s_scr = rest[-3:]
    i = pl.program_id(1)

    @pl.when(i == 0)
    def _():
        s_scr[...] = _heads_to_block_diag(s0_ref)

    hmask = _head_mask(N_HEADS * CHUNK)
    seg_mean = _seg_matrix(MIX_W, 1.0 / HEAD_DIM)
    rows = qb * CHUNK
    first_half = (_iota2((rows, MIX_W), 1) & (HEAD_DIM - 1)) < (HEAD_DIM // 2)
    cos = jnp.concatenate([cos_ref[...]] * (MIX_W // ROPE_W), axis=1)
    sin = jnp.concatenate([sin_ref[...]] * (MIX_W // ROPE_W), axis=1)

    def rope(x):
        partner = jnp.where(first_half, pltpu.roll(x, MIX_W - HEAD_DIM // 2, 1), pltpu.roll(x, HEAD_DIM // 2, 1))
        return x * cos + partner * sin

    q = rope(hc_ref[:, 0:256])
    k = rope(hc_ref[:, 256:512]) * ATTN_SCALE
    v_bf = hc_ref[:, 512:768].astype(BF16)
    state = s_scr[...]
    outs = []
    for j in range(qb):
        sl = slice(j * CHUNK, (j + 1) * CHUNK)
        qj, kj, vj = q[sl], k[sl], v_bf[sl]
        qs = jnp.where(hmask, _tile4(qj), 0.0).astype(BF16)
        sc = _dg(qs, kj.astype(BF16), NT) * dstack_ref[...]
        intra = _fold4(jnp.where(hmask, _dg(sc.astype(BF16), vj), 0.0))
        inter = _dg((qj * qsc_ref[...]).astype(BF16), state.astype(BF16))
        kv = _dg((kj * ksc_ref[...]).astype(BF16), vj, TN)
        state = gam_ref[...] * state + jnp.where(hmask, kv, 0.0)
        outs.append(intra + inter)
    s_scr[...] = state
    _block_diag_to_heads(state, sout_ref)
    y = _head_layer_norm(jnp.concatenate(outs, axis=0), seg_mean, lnw_ref[...], lnb_ref[...], C_GN_EPS)
    g = hc_ref[:, 768:1024]
    o_ref[...] = y * (g * _sigmoid(g))


def _retention(hc, cos, sin, s0, tabs, lnw, lnb, *, n_streams, t, base_row):
    qb = min(8, t // CHUNK)
    rows = qb * CHUNK
    nblk = t // rows
    base = base_row // rows
    dstack, qsc, ksc, gam = tabs
    full = lambda a: pl.BlockSpec(a.shape, lambda s, i: (0,) * a.ndim)
    cur = lambda w: pl.BlockSpec((rows, w), lambda s, i: (base + s * nblk + i, 0))
    state = pl.BlockSpec((None, N_HEADS, HEAD_DIM, HEAD_DIM), lambda s, i: (s, 0, 0, 0))
    return pl.pallas_call(
        functools.partial(_ret_kernel, qb=qb),
        grid=(n_streams, nblk),
        in_specs=[cur(C_PROJ),
                  pl.BlockSpec((rows, ROPE_W), lambda s, i: (i, 0)),
                  pl.BlockSpec((rows, ROPE_W), lambda s, i: (i, 0)),
                  state, full(dstack), full(qsc), full(ksc), full(gam), full(lnw), full(lnb)],
        out_specs=[pl.BlockSpec((rows, MIX_W), lambda s, i: (s * nblk + i, 0)), state],
        out_shape=[jax.ShapeDtypeStruct((n_streams * t, MIX_W), F32),
                   jax.ShapeDtypeStruct((n_streams, N_HEADS, HEAD_DIM, HEAD_DIM), F32)],
        scratch_shapes=[pltpu.VMEM((MIX_W, MIX_W), F32)],
        compiler_params=_cparams(("parallel", "arbitrary")),
        name="retention",
    )(hc, cos, sin, s0, dstack, qsc, ksc, gam, lnw, lnb)


def _retention_tables():
    gamma = 1.0 - 2.0 ** (-5.0 - np.arange(N_HEADS, dtype=np.float64))
    t = np.arange(CHUNK)
    diff = t[:, None] - t[None, :]
    dmat = np.where(diff >= 0, gamma[:, None, None] ** np.maximum(diff, 0), 0.0)
    dstack = dmat.reshape(N_HEADS * CHUNK, CHUNK)
    lanes = lambda per_head: np.repeat(per_head, HEAD_DIM, axis=-1)
    qsc = lanes(gamma[None, :] ** (t + 1)[:, None])
    ksc = lanes(gamma[None, :] ** (CHUNK - 1 - t)[:, None])
    gam = np.broadcast_to(lanes(gamma ** CHUNK)[:, None], (MIX_W, MIX_W))
    return tuple(jnp.asarray(a, F32) for a in (dstack, qsc, ksc, gam))


ROPE_W = 2 * HEAD_DIM


def _rope_tables(pos):
    half = HEAD_DIM // 2
    theta = np.float32(1.0) / (np.float32(ROPE_BASE) ** np.linspace(0.0, 1.0, half, dtype=np.float32))
    ang = np.asarray(pos, np.float32)[:, None] * theta[None, :]
    cos, sin = np.cos(ang), np.sin(ang)
    reps = ROPE_W // HEAD_DIM
    cos_t = np.tile(np.concatenate([cos, cos], axis=-1), (1, reps))
    sin_t = np.tile(np.concatenate([-sin, sin], axis=-1), (1, reps))
    return jnp.asarray(cos_t, F32), jnp.asarray(sin_t, F32)


DECAY_SCALE = 0.6065306597126334
RWKV_CB = 4
N_LEVELS = 6
MASK_HEAD, MASK_STRICT, MASK_INCL, MASK_LEVEL0 = 0, 1, 2, 3


def _rwkv_masks():
    n4 = N_HEADS * CHUNK
    ri = np.arange(n4)[:, None]
    ci = np.arange(n4)[None, :]
    head = (ri >> 6) == (ci >> 6)
    tabs = [head, head & ((ci & 63) < (ri & 63)), head & ((ci & 63) <= (ri & 63))]
    for log_m in range(N_LEVELS):
        same = (ri >> (log_m + 1)) == (ci >> (log_m + 1))
        tabs.append(same & (((ri >> log_m) & 1) == 1) & (((ci >> log_m) & 1) == 0))
    return jnp.asarray(np.stack(tabs), BF16)


def _rwkv_kernel(hb_ref, shift0_ref, h0_ref, masks_ref, mu_ref, w0_ref, w2_ref, a0_ref, a2_ref, g2_ref,
                 kk_ref, ka_ref, rk_ref, lnw_ref, lnb_ref, *rest, cb, independent):
    o_ref, hout_ref, h_scr, shift_scr = rest[-4:]
    c = pl.program_id(1)
    rows = cb * CHUNK
    xb = hb_ref[...]
    row = _iota2(xb.shape, 0)
    prev = pltpu.roll(xb, 1, 0)
    if independent:
        for j in range(cb):
            prev = jnp.where(row == j * CHUNK, shift0_ref[j], prev)
    else:
        @pl.when(c == 0)
        def _():
            h_scr[...] = _heads_to_block_diag(h0_ref.at[0])
            shift_scr[...] = shift0_ref[0]

        prev = jnp.where(row == 0, shift_scr[...], prev)
        shift_scr[...] = xb[rows - 1:rows, :]
    xs = xb + mu_ref[...] * (prev - xb)
    r = xs[:, 0:256]
    k = xs[:, 256:512]
    v = xs[:, 512:768]
    xw = xs[:, 768:832]
    xa = xs[:, 832:896]
    xg = xs[:, 896:1024]

    z = w0_ref[...] + _mm(jnp.tanh(xw), w2_ref[...], passes=3)
    lw = -DECAY_SCALE * _sigmoid(z)
    a_gate = _sigmoid(a0_ref[...] + _mm(xa, a2_ref[...], passes=3))
    gate = _mm(_sigmoid(xg), g2_ref[...], passes=1)

    seg_sum = _seg_matrix(MIX_W, 1.0)
    seg_mean = _seg_matrix(MIX_W, 1.0 / HEAD_DIM)
    kkn = k * kk_ref[...]
    norm = jnp.sqrt(_mm_exact_rhs(kkn * kkn, seg_sum))
    kk = kkn / jnp.maximum(norm, 1e-12)
    kf = k * (1.0 + (a_gate - 1.0) * ka_ref[...])

    tt = _iota2((rows, rows), 0)
    ss = _iota2((rows, rows), 1)
    tril = jnp.where(jnp.logical_and(ss <= tt, (ss >> 6) == (tt >> 6)), 1.0, 0.0).astype(BF16)
    lw_parts = _parts(lw, 3)
    cum = _dg(tril, lw_parts[0]) + (_dg(tril, lw_parts[1]) + _dg(tril, lw_parts[2]))
    w_inv = jnp.exp(-cum)
    rho = (r * jnp.exp(cum)).astype(BF16)
    alpha = (-kk * jnp.exp(cum - lw)).astype(BF16)
    beta = ((kk * a_gate) * w_inv).astype(BF16)
    kappa = (kf * w_inv).astype(BF16)
    v_bf = v.astype(BF16)

    hmask = masks_ref[MASK_HEAD]
    n4 = N_HEADS * CHUNK
    eye = jnp.where(_iota2((n4, n4), 0) == _iota2((n4, n4), 1), 1.0, 0.0)

    pre, a_bfs, t_invs = [], [], []
    for j in range(cb):
        sl = slice(j * CHUNK, (j + 1) * CHUNK)
        bd = lambda zz: _tile4(zz[sl]) * hmask
        al_bd, be_bd, ka_bd, rh_bd, v_bd = bd(alpha), bd(beta), bd(kappa), bd(rho), bd(v_bf)
        a_bf = _dg(al_bd, be_bd, NT).astype(BF16) * masks_ref[MASK_STRICT]
        a_ak = _dg(al_bd, ka_bd, NT).astype(BF16) * masks_ref[MASK_STRICT]
        b_rb = _dg(rh_bd, be_bd, NT).astype(BF16) * masks_ref[MASK_INCL]
        b_rk = _dg(rh_bd, ka_bd, NT).astype(BF16) * masks_ref[MASK_INCL]
        x0 = _dg(a_ak, v_bd)
        y0 = _dg(b_rk, v_bd)
        sn0 = _dg(v_bd, ka_bd, TN)
        w_chunk = jnp.exp(cum[(j + 1) * CHUNK - 1:(j + 1) * CHUNK, :])
        a_bfs.append(a_bf)
        t_invs.append(eye + (a_bf * masks_ref[MASK_LEVEL0]).astype(F32))
        pre.append([al_bd, be_bd, rh_bd, b_rb, None, x0, y0, sn0, w_chunk])
    for lvl in range(1, N_LEVELS):
        t_bfs = [t.astype(BF16) for t in t_invs]
        e_mats = [_dg(a_bfs[j] * masks_ref[MASK_LEVEL0 + lvl], t_bfs[j]) for j in range(cb)]
        t_invs = [t_invs[j] + _dg(t_bfs[j], e_mats[j].astype(BF16)) for j in range(cb)]
    for j in range(cb):
        pre[j][4] = t_invs[j].astype(BF16)

    ys = []
    h = None if independent else h_scr[...]
    for j in range(cb):
        al_bd, be_bd, rh_bd, b_rb, t_bf, x0, y0, sn0, w_chunk = pre[j]
        h0 = _heads_to_block_diag(h0_ref.at[j]) if independent else h
        h0_bf = h0.astype(BF16)
        x_mat = _dg(al_bd, h0_bf, NT) + x0
        u_bf = _dg(t_bf, x_mat.astype(BF16)).astype(BF16)
        y_bd = _dg(rh_bd, h0_bf, NT) + _dg(b_rb, u_bf) + y0
        h_new = (h0 + _dg(u_bf, be_bd, TN) + sn0) * w_chunk
        ys.append(_fold4(y_bd))
        if independent:
            _block_diag_to_heads(h_new, hout_ref.at[j])
        else:
            h = h_new
    if not independent:
        h_scr[...] = h
        _block_diag_to_heads(h, hout_ref.at[0])

    y = _head_layer_norm(jnp.concatenate(ys, axis=0), seg_mean, lnw_ref[...], lnb_ref[...], B_GN_EPS)
    bonus = _mm_exact_rhs(r * kf * rk_ref[...], seg_sum) * v
    o_ref[...] = (y + bonus) * gate


def _rwkv(hb, shift0, h0, params, *, independent, n_streams, t, base_row):
    cb = RWKV_CB
    rows = cb * CHUNK
    nblk = t // rows
    base = base_row // rows
    masks = _rwkv_masks()
    full = lambda a: pl.BlockSpec(a.shape, lambda s, c: (0,) * a.ndim)
    if independent:
        assert n_streams == 1
        st_idx = lambda s, c: c
        n_state, st_blk = t // CHUNK, cb
    else:
        st_idx = lambda s, c: s
        n_state, st_blk = n_streams, 1
    cur = lambda w: pl.BlockSpec((rows, w), lambda s, c: (base + s * nblk + c, 0))
    state = pl.BlockSpec((st_blk, N_HEADS, HEAD_DIM, HEAD_DIM), lambda s, c: (st_idx(s, c), 0, 0, 0))
    return pl.pallas_call(
        functools.partial(_rwkv_kernel, cb=cb, independent=independent),
        grid=(n_streams, nblk),
        in_specs=[cur(B_PROJ),
                  pl.BlockSpec((st_blk, 1, B_PROJ), lambda s, c: (st_idx(s, c), 0, 0)),
                  state,
                  full(masks)] + [full(p) for p in params],
        out_specs=[pl.BlockSpec((rows, MIX_W), lambda s, c: (s * nblk + c, 0)), state],
        out_shape=[jax.ShapeDtypeStruct((n_streams * t, MIX_W), F32),
                   jax.ShapeDtypeStruct((n_state, N_HEADS, HEAD_DIM, HEAD_DIM), F32)],
        scratch_shapes=[pltpu.VMEM((MIX_W, MIX_W), F32), pltpu.VMEM((1, B_PROJ), F32)],
        compiler_params=_cparams(("parallel", "arbitrary")),
        name="rwkv7",
    )(hb, shift0, h0, masks, *params)


def _row(p):
    return p.reshape(1, -1).astype(F32)


def _mixers(proj, caches, lp, tabs, geom):
    aq, ak, av, hb, hc, dq, dk, dv = proj
    bp, tp, bs, ts = geom
    n_p = bp * tp
    ca_k, ca_v, sb_shift, sb_wkv, sc, cd_k, cd_v = caches
    pr = dict(n_streams=bp, t=tp, base_row=0)
    sm = dict(n_streams=bs, t=ts, base_row=n_p)
    zeros_state = jnp.zeros((bp, N_HEADS, HEAD_DIM, HEAD_DIM), F32)

    oa_p = _attention(aq, ak, av, None, lp["sink_col"], n_prev=A_PREV_CHUNKS, use_sink=True, **pr)
    oa_s = _attention(aq, ak, av, (ca_k.reshape(bs, -1, A_KV_W), ca_v.reshape(bs, -1, A_KV_W)), lp["sink_col"],
                      n_prev=A_PREV_CHUNKS, use_sink=True, **sm)

    ob_p, h_p = _rwkv(hb, jnp.zeros((bp, 1, B_PROJ), F32), zeros_state, lp["rwkv"], independent=False, **pr)
    ob_s, h_s = _rwkv(hb, sb_shift.reshape(bs, 1, B_PROJ), sb_wkv.astype(F32), lp["rwkv"], independent=True,
                      n_streams=1, t=bs * ts, base_row=n_p)

    oc_p, s_p = _retention(hc, *tabs["rope_prompt"], zeros_state, tabs["ret"], lp["c_ln_w"], lp["c_ln_b"], **pr)
    oc_s, s_s = _retention(hc, *tabs["rope_sample"], sc.astype(F32), tabs["ret"], lp["c_ln_w"], lp["c_ln_b"], **sm)

    od_p = _attention(dq, dk, dv, None, lp["bias_table"], n_prev=D_PREV_CHUNKS, use_sink=False, **pr)
    od_s = _attention(dq, dk, dv, (cd_k.reshape(bs, -1, MIX_W), cd_v.reshape(bs, -1, MIX_W)), lp["bias_table"],
                      n_prev=D_PREV_CHUNKS, use_sink=False, **sm)

    mix = ((oa_p, oa_s), (ob_p, ob_s), (oc_p, oc_s), (od_p, od_s))
    return mix, (h_p, s_p), (h_s, s_s)


def kernel(x_prompt, x_sample, cache_a_k, cache_a_v, state_b_shift, state_b_wkv, state_c, cache_d_k, cache_d_v,
           norm1_g, norm2_g, w_in, w_out, a_q_norm, a_k_norm, a_sinks, b_mu, b_w0, b_w2, b_a0, b_a2, b_g2,
           b_k_k, b_k_a, b_r_k, b_ln_w, b_ln_b, c_ln_w, c_ln_b, d_q_norm, d_k_norm, d_rel_bias,
           ffn_w1, ffn_w3, ffn_w2, moe_router, moe_w1, moe_w3, moe_w2):
    bp, tp, _ = x_prompt.shape
    bs, ts, _ = x_sample.shape
    assert ts == CHUNK
    n_p, n_s = bp * tp, bs * ts
    geom = (bp, tp, bs, ts)
    xa = x_prompt.reshape(n_p, D_MODEL)
    xb = x_sample.reshape(n_s, D_MODEL)

    tabs = {
        "ret": _retention_tables(),
        "rope_prompt": _rope_tables(np.arange(tp)),
        "rope_sample": _rope_tables(PAST_LEN + np.arange(ts)),
    }
    tile = lambda g: _row(jnp.tile(g, MIX_W // HEAD_DIM))

    p_states, s_states = [], []
    for l in range(DEPTH):
        lp = {
            "sink_col": jnp.repeat(a_sinks[l].astype(F32), CHUNK).reshape(N_HEADS * CHUNK, 1),
            "bias_table": _relbias_table(d_rel_bias[l].astype(F32), (D_PREV_CHUNKS + 1) * CHUNK),
            "rwkv": (_row(b_mu[l]), _row(b_w0[l]), b_w2[l], _row(b_a0[l]), b_a2[l], b_g2[l], _row(b_k_k[l]),
                     _row(b_k_a[l]), _row(b_r_k[l]), _row(b_ln_w[l]), _row(b_ln_b[l])),
            "c_ln_w": _row(c_ln_w[l]), "c_ln_b": _row(c_ln_b[l]),
        }
        proj = _inproj(xa, xb, n_p + n_s, _row(norm1_g[l]), w_in[l].astype(BF16), tile(a_q_norm[l]),
                       _row(jnp.tile(a_k_norm[l], A_KV_W // HEAD_DIM)), tile(d_q_norm[l]), tile(d_k_norm[l]))
        _, ak, av, hb, _, _, dk, dv = proj
        caches = (cache_a_k[l], cache_a_v[l], state_b_shift[l], state_b_wkv[l], state_c[l], cache_d_k[l], cache_d_v[l])
        mix, (wkv_p, ret_p), (wkv_s, ret_s) = _mixers(proj, caches, lp, tabs, geom)
        j = l // 2
        if l % 2 == 0:
            xa = xb = _outproj_ffn(xa, xb, n_p + n_s, mix, w_out[l].astype(BF16), _row(norm2_g[l]),
                                   ffn_w1[j].astype(BF16), ffn_w3[j].astype(BF16), ffn_w2[j].astype(BF16))
        else:
            x1, xn_a, xn_b, *routing = _outproj_route(xa, xb, n_p + n_s, mix, w_out[l].astype(BF16),
                                                      _row(norm2_g[l]), _router_lanes(moe_router[j]))
            groups = ((0, n_p), (n_p, n_s)) if l == DEPTH - 1 else ((0, n_p + n_s),)
            outs = _moe(xn_a, xn_b, x1, routing,
                        moe_w1[j].astype(BF16), moe_w3[j].astype(BF16), moe_w2[j].astype(BF16), groups)
            xa, xb = (outs[0], outs[-1])

        wa = min(A_PREV_CHUNKS * CHUNK, tp)
        wd = min(D_PREV_CHUNKS * CHUNK, tp)
        tail = lambda a, w, heads: jnp.stack(
            [a[(b + 1) * tp - w:(b + 1) * tp] for b in range(bp)]).reshape(bp, w, heads, HEAD_DIM)
        last_rows = lambda a, t, first, count: jnp.concatenate(
            [a[first + (s + 1) * t - 1:first + (s + 1) * t] for s in range(count)], axis=0)
        p_states.append((tail(ak, wa, 2), tail(av, wa, 2), last_rows(hb, tp, 0, bp), wkv_p, ret_p,
                         tail(dk, wd, N_HEADS), tail(dv, wd, N_HEADS)))
        roll_in = lambda cache, new, heads: jnp.concatenate(
            [cache.astype(F32), new[n_p:].reshape(bs, ts, heads, HEAD_DIM)], axis=1)[:, -cache.shape[1]:]
        s_states.append((roll_in(cache_a_k[l], ak, 2), roll_in(cache_a_v[l], av, 2),
                         last_rows(hb, ts, n_p, bs), wkv_s, ret_s,
                         roll_in(cache_d_k[l], dk, N_HEADS), roll_in(cache_d_v[l], dv, N_HEADS)))

    if xa is xb:
        xa, xb = xa[:n_p], xa[n_p:]
    yp = xa.reshape(bp, tp, D_MODEL)
    ys = xb.reshape(bs, ts, D_MODEL)
    st = lambda group, i: jnp.stack([g[i] for g in group], axis=0)
    return (yp, ys,
            st(p_states, 0), st(p_states, 1), st(p_states, 2), st(p_states, 3), st(p_states, 4), st(p_states, 5), st(p_states, 6),
            st(s_states, 0), st(s_states, 1), st(s_states, 2), st(s_states, 3), st(s_states, 4), st(s_states, 5), st(s_states, 6))
```

```python
import functools

import jax
import jax.numpy as jnp
import numpy as np
from jax import lax
from jax.experimental import pallas as pl
from jax.experimental.pallas import tpu as pltpu
from jax.experimental.pallas import tpu_sc as plsc

F32 = jnp.float32
BF16 = jnp.bfloat16

D_MODEL = 1024
DEPTH = 2
PAST_LEN = 4096
CHUNK = 64
HEAD_DIM = 64
N_HEADS = 4
MIX_W = N_HEADS * HEAD_DIM
A_KV_W = 128
A_PREV_CHUNKS = 2
D_PREV_CHUNKS = 8
D_REL_CLIP = 128
B_PROJ = 1024
C_PROJ = 1024
IN_PROJ = 3328
B_GN_EPS = 64e-5
C_GN_EPS = 1e-6
NORM_EPS = 1e-6
ATTN_SCALE = 0.125
ROPE_BASE = 10000.0
D_FF = 2816
N_EXPERTS = 8
E_FF = 3584
NEG_BIG = -1e30

VMEM_LIMIT = 56 * 1024 * 1024

NN = ((1,), (0,))
NT = ((1,), (1,))
TN = ((0,), (0,))


def _dg(a, b, dims=NN):
    return lax.dot_general(a, b, (dims, ((), ())), preferred_element_type=F32)


def _parts(x, n):
    out = []
    r = x
    for i in range(n):
        p = r.astype(BF16)
        out.append(p)
        if i + 1 < n:
            r = r - p.astype(F32)
    return out


def _mm(a, b, dims=NN, passes=1):
    if passes == 1:
        return _dg(a.astype(BF16), b.astype(BF16), dims)
    ah, al = _parts(a, 2)
    bh, bl = _parts(b, 2)
    return _dg(ah, bh, dims) + (_dg(ah, bl, dims) + _dg(al, bh, dims))


def _mm_exact_rhs(a, b_bf, dims=NN, n=2):
    acc = None
    for p in _parts(a, n):
        t = _dg(p, b_bf, dims)
        acc = t if acc is None else acc + t
    return acc


def _iota2(shape, dim):
    return lax.broadcasted_iota(jnp.int32, shape, dim)


def _head_mask(rows, cols=MIX_W):
    return (_iota2((rows, cols), 0) >> 6) == (_iota2((rows, cols), 1) >> 6)


def _seg_matrix(width, value):
    m = _head_mask(width, width)
    return jnp.where(m, value, 0.0).astype(BF16)


def _tile4(z):
    return jnp.concatenate([z, z, z, z], axis=0)


def _fold4(z):
    return (z[0:64] + z[64:128]) + (z[128:192] + z[192:256])


def _heads_to_block_diag(state_ref):
    rows = []
    for h in range(N_HEADS):
        pieces = [jnp.zeros((HEAD_DIM, HEAD_DIM), F32)] * N_HEADS
        pieces[h] = state_ref[h]
        rows.append(jnp.concatenate(pieces, axis=1))
    return jnp.concatenate(rows, axis=0)


def _block_diag_to_heads(m, state_ref):
    for h in range(N_HEADS):
        state_ref[h] = m[h * HEAD_DIM:(h + 1) * HEAD_DIM, h * HEAD_DIM:(h + 1) * HEAD_DIM]


def _sigmoid(x):
    return 1.0 / (1.0 + jnp.exp(-x))


def _cparams(sem):
    return pltpu.CompilerParams(dimension_semantics=sem, vmem_limit_bytes=VMEM_LIMIT)


IN_TM = 512


def _two_source_specs(xa, xb, n):
    na, nb = xa.shape[0] // IN_TM, xb.shape[0] // IN_TM
    spec_a = pl.BlockSpec((IN_TM, D_MODEL), lambda i: (jnp.minimum(i, na - 1), 0))
    spec_b = pl.BlockSpec((IN_TM, D_MODEL), lambda i: (jnp.clip(i - na, 0, nb - 1), 0))
    return na, n // IN_TM, spec_a, spec_b


def _inproj_kernel(xa_ref, xb_ref, g_ref, w_ref, aqg_ref, akg_ref, dqg_ref, dkg_ref,
                   aq_ref, ak_ref, av_ref, hb_ref, hc_ref, dq_ref, dk_ref, dv_ref, *, n_first):
    x = jnp.where(pl.program_id(0) < n_first, xa_ref[...], xb_ref[...])
    ms = jnp.mean(x * x, axis=-1, keepdims=True)
    xn = ((x * lax.rsqrt(ms + NORM_EPS)) * g_ref[...]).astype(BF16)
    seg = _seg_matrix(MIX_W, 1.0 / HEAD_DIM)

    def proj(lo, hi):
        return jnp.dot(xn, w_ref[:, lo:hi], preferred_element_type=F32)

    def head_rms(h, gain_ref):
        w = h.shape[-1]
        msq = _mm_exact_rhs(h * h, seg[:w, :w], n=1)
        return (h * lax.rsqrt(msq + NORM_EPS)) * gain_ref[...]

    aq_ref[...] = head_rms(proj(0, 256), aqg_ref)
    ak_ref[...] = head_rms(proj(256, 384), akg_ref)
    av_ref[...] = proj(384, 512)
    hb_ref[...] = proj(512, 1536)
    hc_ref[...] = proj(1536, 2560)
    dq_ref[...] = head_rms(proj(2560, 2816), dqg_ref)
    dk_ref[...] = head_rms(proj(2816, 3072), dkg_ref)
    dv_ref[...] = proj(3072, 3328)


def _inproj(xa, xb, n, g, w_bf, aqg, akg, dqg, dkg):
    na, nblk, spec_a, spec_b = _two_source_specs(xa, xb, n)
    widths = (256, 128, 128, B_PROJ, C_PROJ, 256, 256, 256)
    row = lambda w: pl.BlockSpec((IN_TM, w), lambda i: (i, 0))
    full = lambda a: pl.BlockSpec(a.shape, lambda i: (0,) * a.ndim)
    return pl.pallas_call(
        functools.partial(_inproj_kernel, n_first=na),
        grid=(nblk,),
        in_specs=[spec_a, spec_b, full(g), full(w_bf), full(aqg), full(akg), full(dqg), full(dkg)],
        out_specs=[row(w) for w in widths],
        out_shape=[jax.ShapeDtypeStruct((n, w), F32) for w in widths],
        compiler_params=_cparams(("parallel",)),
        name="inproj",
    )(xa, xb, g, w_bf, aqg, akg, dqg, dkg)


PACK_W = 256


def _pack_bf16_pairs(hi, lo):
    bits = lambda z: pltpu.bitcast(z.astype(BF16).astype(F32), jnp.int32)
    return bits(hi) | lax.shift_right_logical(bits(lo), jnp.full(lo.shape, 16, jnp.int32))


def _unpack_bf16_pairs(w):
    hi = pltpu.bitcast(w & jnp.int32(-65536), F32)
    lo = pltpu.bitcast(lax.shift_left(w, jnp.full(w.shape, 16, jnp.int32)), F32)
    return hi, lo


def _pack_rows(x):
    return (_pack_bf16_pairs(x[:, 0:PACK_W], x[:, PACK_W:2 * PACK_W]),
            _pack_bf16_pairs(x[:, 2 * PACK_W:3 * PACK_W], x[:, 3 * PACK_W:4 * PACK_W]))


def _unpack_rows(wa, wb):
    return jnp.concatenate(_unpack_bf16_pairs(wa) + _unpack_bf16_pairs(wb), axis=1)


ROUTER_LANES = 128


ROUTER_TERMS = 3


def _router_lanes(router):
    terms = _parts(router.astype(F32), ROUTER_TERMS)
    return jnp.pad(jnp.concatenate(terms, axis=1), ((0, 0), (0, ROUTER_LANES - ROUTER_TERMS * N_EXPERTS)))


def _route(xn_bf, router_bf, counts):
    split = _dg(xn_bf, router_bf)
    logits = split
    for k in range(1, ROUTER_TERMS):
        logits = logits + pltpu.roll(split, ROUTER_LANES - k * N_EXPERTS, 1)
    lane = _iota2(logits.shape, 1)
    logits = jnp.where(lane < N_EXPERTS, logits, NEG_BIG)
    m1 = jnp.max(logits, axis=-1, keepdims=True)
    i1 = jnp.min(jnp.where(logits == m1, lane, ROUTER_LANES), axis=-1, keepdims=True)
    rest = jnp.where(lane == i1, NEG_BIG, logits)
    m2 = jnp.max(rest, axis=-1, keepdims=True)
    i2 = jnp.min(jnp.where(rest == m2, lane, ROUTER_LANES), axis=-1, keepdims=True)
    e2 = jnp.exp(m2 - m1)
    den = 1.0 + e2
    gates = jnp.where(lane == i1, 1.0 / den, 0.0) + jnp.where(lane == i2, e2 / den, 0.0)
    sel = jnp.where(gates > 0.0, 1.0, 0.0)
    tm = sel.shape[0]
    before = jnp.where(_iota2((tm, tm), 1) < _iota2((tm, tm), 0), 1.0, 0.0).astype(BF16)
    rank = (_dg(before, sel.astype(BF16)) + counts).astype(jnp.int32)
    return gates, rank, counts + jnp.sum(sel, axis=0, keepdims=True)


def _mixed_residual(i, xa_ref, xb_ref, mix_refs, w_ref, g_ref, n_first_x, n_first_mix):
    x1 = jnp.where(i < n_first_x, xa_ref[...], xb_ref[...])
    for m in range(4):
        o = jnp.where(i < n_first_mix, mix_refs[2 * m][...], mix_refs[2 * m + 1][...])
        x1 = x1 + jnp.dot(o.astype(BF16), w_ref[m * MIX_W:(m + 1) * MIX_W, :], preferred_element_type=F32)
    ms = jnp.mean(x1 * x1, axis=-1, keepdims=True)
    return x1, (x1 * lax.rsqrt(ms + NORM_EPS)) * g_ref[...]


def _outproj_route_kernel(xa_ref, xb_ref, *refs, n_first_x, n_first_mix):
    mix_refs, (w_ref, g_ref, r_ref), outs = refs[:8], refs[8:11], refs[11:]
    x1_ref, pa_ref, pb_ref, gate_ref, rank_ref, cnt_ref, cnt_scr = outs
    i = pl.program_id(0)
    acc, xn = _mixed_residual(i, xa_ref, xb_ref, mix_refs, w_ref, g_ref, n_first_x, n_first_mix)

    @pl.when(i == 0)
    def _():
        cnt_scr[...] = jnp.zeros_like(cnt_scr)

    x1_ref[...] = acc
    pa_ref[...], pb_ref[...] = _pack_rows(xn)
    gates, rank, counts = _route(xn.astype(BF16), r_ref[...], cnt_scr[...])
    gate_ref[...] = gates
    rank_ref[...] = rank
    cnt_scr[...] = counts
    cnt_ref[...] = counts.astype(jnp.int32)


def _residual_specs(xa, xb, n, mix, index):
    na, nb = xa.shape[0] // IN_TM, xb.shape[0] // IN_TM
    nm_p, nm_s = mix[0][0].shape[0] // IN_TM, mix[0][1].shape[0] // IN_TM
    first = lambda w, cnt: pl.BlockSpec((IN_TM, w), index(lambda i: jnp.minimum(i, cnt - 1)))
    second = lambda w, skip, cnt: pl.BlockSpec((IN_TM, w), index(lambda i: jnp.clip(i - skip, 0, cnt - 1)))
    specs = [first(D_MODEL, na), second(D_MODEL, na, nb)] + [first(MIX_W, nm_p), second(MIX_W, nm_p, nm_s)] * 4
    return specs, [xa, xb, *[a for pair in mix for a in pair]], dict(n_first_x=na, n_first_mix=nm_p)


def _outproj_route(xa, xb, n, mix, w_bf, g2, router_bf):
    specs, operands, statics = _residual_specs(xa, xb, n, mix, lambda blk: (lambda i: (blk(i), 0)))
    row = lambda w: pl.BlockSpec((IN_TM, w), lambda i: (i, 0))
    full = lambda a: pl.BlockSpec(a.shape, lambda i: (0,) * a.ndim)
    return pl.pallas_call(
        functools.partial(_outproj_route_kernel, **statics),
        grid=(n // IN_TM,),
        in_specs=specs + [full(w_bf), full(g2), full(router_bf)],
        out_specs=[row(D_MODEL), row(PACK_W), row(PACK_W), row(ROUTER_LANES), row(ROUTER_LANES),
                   pl.BlockSpec((1, ROUTER_LANES), lambda i: (0, 0))],
        out_shape=[jax.ShapeDtypeStruct((n, D_MODEL), F32),
                   jax.ShapeDtypeStruct((n, PACK_W), jnp.int32), jax.ShapeDtypeStruct((n, PACK_W), jnp.int32),
                   jax.ShapeDtypeStruct((n, ROUTER_LANES), F32), jax.ShapeDtypeStruct((n, ROUTER_LANES), jnp.int32),
                   jax.ShapeDtypeStruct((1, ROUTER_LANES), jnp.int32)],
        scratch_shapes=[pltpu.VMEM((1, ROUTER_LANES), F32)],
        compiler_params=_cparams(("arbitrary",)),
        name="outproj_route",
    )(*operands, w_bf, g2, router_bf)


FFN_TF = 1408


def _outproj_ffn_kernel(xa_ref, xb_ref, *refs, n_first_x, n_first_mix):
    mix_refs, rest = refs[:8], refs[8:]
    w_ref, g_ref, w1_ref, w3_ref, w2_ref, o_ref, x1_scr, xn_scr = rest
    i, f = pl.program_id(0), pl.program_id(1)

    @pl.when(f == 0)
    def _():
        x1, xn = _mixed_residual(i, xa_ref, xb_ref, mix_refs, w_ref, g_ref, n_first_x, n_first_mix)
        x1_scr[...] = x1
        xn_scr[...] = xn.astype(BF16)

    xn = xn_scr[...]
    a = jnp.dot(xn, w1_ref[...], preferred_element_type=F32)
    b = jnp.dot(xn, w3_ref[...], preferred_element_type=F32)
    h = ((a * _sigmoid(a)) * b).astype(BF16)
    y = jnp.dot(h, w2_ref[...], preferred_element_type=F32)

    @pl.when(f == 0)
    def _():
        o_ref[...] = x1_scr[...] + y

    @pl.when(f != 0)
    def _():
        o_ref[...] += y


def _outproj_ffn(xa, xb, n, mix, w_bf, g2, w1_bf, w3_bf, w2_bf):
    specs, operands, statics = _residual_specs(xa, xb, n, mix, lambda blk: (lambda i, f: (blk(i), 0)))
    full = lambda a: pl.BlockSpec(a.shape, lambda i, f: (0,) * a.ndim)
    return pl.pallas_call(
        functools.partial(_outproj_ffn_kernel, **statics),
        grid=(n // IN_TM, D_FF // FFN_TF),
        in_specs=specs + [full(w_bf), full(g2),
                          pl.BlockSpec((D_MODEL, FFN_TF), lambda i, f: (0, f)),
                          pl.BlockSpec((D_MODEL, FFN_TF), lambda i, f: (0, f)),
                          pl.BlockSpec((FFN_TF, D_MODEL), lambda i, f: (f, 0))],
        out_specs=pl.BlockSpec((IN_TM, D_MODEL), lambda i, f: (i, 0)),
        out_shape=jax.ShapeDtypeStruct((n, D_MODEL), F32),
        scratch_shapes=[pltpu.VMEM((IN_TM, D_MODEL), F32), pltpu.VMEM((IN_TM, D_MODEL), BF16)],
        compiler_params=_cparams(("parallel", "arbitrary")),
        name="outproj_ffn",
    )(*operands, w_bf, g2, w1_bf, w3_bf, w2_bf)


MOE_R = 512
MOE_TF = 1792
SC_WINDOW = 128


def _moe_plan(gates, rank, counts, n):
    n_blocks = (2 * n) // MOE_R + N_EXPERTS + 1
    spare_row = (n_blocks - 1) * MOE_R
    sel = gates[:, :N_EXPERTS] > 0.0
    rank = rank[:, :N_EXPERTS]
    counts = counts[0, :N_EXPERTS]
    padded = ((counts + MOE_R - 1) // MOE_R) * MOE_R
    pad_end = jnp.cumsum(padded)
    pad_start = pad_end - padded
    pos = jnp.where(sel, pad_start[None, :] + rank, -1)
    order = jnp.cumsum(sel.astype(jnp.int32), axis=1)
    pick = lambda j: jnp.max(jnp.where(jnp.logical_and(sel, order == j), pos, -1), axis=1)
    to_row = lambda p: jnp.where(p >= 0, p, spare_row).astype(jnp.int32).reshape(1, n)
    block_expert = jnp.minimum(
        jnp.sum(pad_end[None, :] <= (jnp.arange(n_blocks) * MOE_R)[:, None], axis=1), N_EXPERTS - 1)
    return dict(n_blocks=n_blocks, pos0=to_row(pick(1)), pos1=to_row(pick(2)),
                block_expert=block_expert.astype(jnp.int32), n_used=(pad_end[-1:] // MOE_R).astype(jnp.int32))


def _sc_mesh():
    return plsc.VectorSubcoreMesh(core_axis_name="core", subcore_axis_name="subcore")


def _sc_scatter_rows(table, idx_lists, n_rows):
    n, cols = table.shape
    k = len(idx_lists)

    @functools.partial(pl.kernel, out_type=jax.ShapeDtypeStruct((n_rows, cols), table.dtype), mesh=_sc_mesh())
    def scatter(x_hbm, *rest):
        i_hbms, o_hbm = rest[:k], rest[k]

        def body(x_vmem, *i_vmems):
            for i_vmem in i_vmems:
                pltpu.sync_copy(x_vmem, o_hbm.at[i_vmem.at[0]])

        pltpu.emit_pipeline(
            body,
            grid=(n // SC_WINDOW,),
            in_specs=[pl.BlockSpec((SC_WINDOW, cols), lambda i: (i, 0))]
            + [pl.BlockSpec((1, SC_WINDOW), lambda i: (0, i))] * k,
            out_specs=[],
            core_axis_name=("core", "subcore"),
            dimension_semantics=(pltpu.PARALLEL,),
        )(x_hbm, *i_hbms)

    return scatter(table, *idx_lists)


def _sc_gather_rows(table, idx):
    n = idx.shape[1]
    cols = table.shape[1]

    @functools.partial(pl.kernel, out_type=jax.ShapeDtypeStruct((n, cols), table.dtype), mesh=_sc_mesh())
    def gather(x_hbm, i_hbm, o_hbm):
        def body(i_vmem, o_vmem):
            pltpu.sync_copy(x_hbm.at[i_vmem.at[0]], o_vmem)

        pltpu.emit_pipeline(
            body,
            grid=(n // SC_WINDOW,),
            in_specs=[pl.BlockSpec((1, SC_WINDOW), lambda i: (0, i))],
            out_specs=[pl.BlockSpec((SC_WINDOW, cols), lambda i: (i, 0))],
            core_axis_name=("core", "subcore"),
            dimension_semantics=(pltpu.PARALLEL,),
        )(i_hbm, o_hbm)

    return gather(table, idx)


def _moe_expert_kernel(be_ref, nu_ref, xa_ref, xb_ref, gs_ref, w1_ref, w3_ref, w2_ref, oa_ref, ob_ref, acc_ref):
    j, f = pl.program_id(0), pl.program_id(1)
    used = j < nu_ref[0]

    @pl.when(used)
    def _():
        x = _unpack_rows(xa_ref[...], xb_ref[...]).astype(BF16)
        a = jnp.dot(x, w1_ref[...], preferred_element_type=F32)
        b = jnp.dot(x, w3_ref[...], preferred_element_type=F32)
        h = ((a * _sigmoid(a)) * b).astype(BF16)
        y = jnp.dot(h, w2_ref[...], preferred_element_type=F32)

        @pl.when(f == 0)
        def _():
            acc_ref[...] = y

        @pl.when(f != 0)
        def _():
            acc_ref[...] += y

    @pl.when(f == pl.num_programs(1) - 1)
    def _():
        lane = _iota2(gs_ref.shape, 1)
        g = jnp.sum(jnp.where(lane == be_ref[j], gs_ref[...], 0.0), axis=1, keepdims=True)
        oa_ref[...], ob_ref[...] = _pack_rows(jnp.where(used, acc_ref[...] * g, 0.0))


def _moe_experts(plan, xs_a, xs_b, gs, w1_bf, w3_bf, w2_bf):
    n_blocks = plan["n_blocks"]
    half = pl.BlockSpec((MOE_R, PACK_W), lambda j, f, be, nu: (j, 0))
    grid_spec = pltpu.PrefetchScalarGridSpec(
        num_scalar_prefetch=2,
        grid=(n_blocks, E_FF // MOE_TF),
        in_specs=[half, half,
                  pl.BlockSpec((MOE_R, ROUTER_LANES), lambda j, f, be, nu: (j, 0)),
                  pl.BlockSpec((None, D_MODEL, MOE_TF), lambda j, f, be, nu: (be[j], 0, f)),
                  pl.BlockSpec((None, D_MODEL, MOE_TF), lambda j, f, be, nu: (be[j], 0, f)),
                  pl.BlockSpec((None, MOE_TF, D_MODEL), lambda j, f, be, nu: (be[j], f, 0))],
        out_specs=[half, half],
        scratch_shapes=[pltpu.VMEM((MOE_R, D_MODEL), F32)])
    return pl.pallas_call(
        _moe_expert_kernel,
        grid_spec=grid_spec,
        out_shape=[jax.ShapeDtypeStruct((n_blocks * MOE_R, PACK_W), jnp.int32)] * 2,
        compiler_params=_cparams(("arbitrary", "arbitrary")),
        name="moe_experts",
    )(plan["block_expert"], plan["n_used"], xs_a, xs_b, gs, w1_bf, w3_bf, w2_bf)


def _moe_combine_kernel(x1_ref, a0_ref, b0_ref, a1_ref, b1_ref, o_ref):
    o_ref[...] = (x1_ref[...] + _unpack_rows(a0_ref[...], b0_ref[...])) + _unpack_rows(a1_ref[...], b1_ref[...])


def _moe_combine(x1, picked, row0, n_rows):
    base = row0 // IN_TM
    row = lambda w: pl.BlockSpec((IN_TM, w), lambda i: (base + i, 0))
    return pl.pallas_call(
        _moe_combine_kernel,
        grid=(n_rows // IN_TM,),
        in_specs=[row(D_MODEL)] + [row(PACK_W)] * 4,
        out_specs=pl.BlockSpec((IN_TM, D_MODEL), lambda i: (i, 0)),
        out_shape=jax.ShapeDtypeStruct((n_rows, D_MODEL), F32),
        compiler_params=_cparams(("parallel",)),
        name="moe_combine",
    )(x1, *picked)


def _moe(xn_a, xn_b, x1, routing, w1_bf, w3_bf, w2_bf, row_groups):
    gates, rank, counts = routing
    plan = _moe_plan(gates, rank, counts, x1.shape[0])
    n_rows = plan["n_blocks"] * MOE_R
    idx = (plan["pos0"], plan["pos1"])
    xs_a = _sc_scatter_rows(xn_a, idx, n_rows)
    xs_b = _sc_scatter_rows(xn_b, idx, n_rows)
    gs = _sc_scatter_rows(gates, idx, n_rows)
    os_a, os_b = _moe_experts(plan, xs_a, xs_b, gs, w1_bf, w3_bf, w2_bf)
    picked = [_sc_gather_rows(t, p) for p in idx for t in (os_a, os_b)]
    return [_moe_combine(x1, picked, row0, rows) for row0, rows in row_groups]


def _relbias_kernel(rb_ref, o_ref, *, nk):
    h = pl.program_id(0)
    q = _iota2((CHUNK, nk), 0)
    r = _iota2((CHUNK, nk), 1)
    idx = jnp.clip(q - (r - (nk - CHUNK)), -D_REL_CLIP, D_REL_CLIP) + D_REL_CLIP

    def body(j, acc):
        return jnp.where(idx == j, rb_ref[h, j], acc)

    o_ref[...] = lax.fori_loop(0, 2 * D_REL_CLIP + 1, body, jnp.zeros((CHUNK, nk), F32))


def _relbias_table(rel_bias, nk):
    return pl.pallas_call(
        functools.partial(_relbias_kernel, nk=nk),
        grid=(N_HEADS,),
        in_specs=[pl.BlockSpec(memory_space=pltpu.SMEM)],
        out_specs=pl.BlockSpec((CHUNK, nk), lambda h: (h, 0)),
        out_shape=jax.ShapeDtypeStruct((N_HEADS * CHUNK, nk), F32),
        name="relbias",
    )(rel_bias)


ATTN_QB = 8
ATTN_GROUP = 8


def _attn_kernel(q_ref, kp_ref, kc_ref, vp_ref, vc_ref, x_ref, *rest, qb, n_prev, use_sink, mask_first):
    o_ref, kbuf, vbuf = rest[-3:]
    i = pl.program_id(1)
    p_rows = kp_ref.shape[0]
    nk = (n_prev + 1) * CHUNK
    wk = kp_ref.shape[1]

    kbuf[0:p_rows, :] = kp_ref[...].astype(BF16)
    kbuf[p_rows:, :] = kc_ref[...].astype(BF16)
    vbuf[0:p_rows, :] = vp_ref[...].astype(BF16)
    vbuf[p_rows:, :] = vc_ref[...].astype(BF16)

    hmask = _head_mask(N_HEADS * CHUNK)
    extra = x_ref[...]
    grouped = wk != MIX_W
    low = _iota2((CHUNK, A_KV_W), 1) < HEAD_DIM

    def stack_queries(qj):
        if not grouped:
            return jnp.where(hmask, _tile4(qj), 0.0)
        shifted = pltpu.roll(qj, MIX_W - HEAD_DIM, 1)[:, :A_KV_W]
        return jnp.concatenate([jnp.where(low, qj[:, :A_KV_W], 0.0), jnp.where(low, shifted, 0.0),
                                jnp.where(low, 0.0, shifted), jnp.where(low, 0.0, qj[:, A_KV_W:])], axis=0)

    def unstack_outputs(o_all):
        if not grouped:
            return _fold4(jnp.where(hmask, o_all, 0.0))
        b0, b1, b2, b3 = (o_all[h * CHUNK:(h + 1) * CHUNK] for h in range(N_HEADS))
        left = jnp.where(low, b0, 0.0) + pltpu.roll(jnp.where(low, b1, 0.0), HEAD_DIM, 1)
        right = pltpu.roll(jnp.where(low, 0.0, b2), HEAD_DIM, 1) + jnp.where(low, 0.0, b3)
        return jnp.concatenate([left, right], axis=1)

    def scores(j):
        base = p_rows + (j - n_prev) * CHUNK
        qs = stack_queries(q_ref[pl.ds(j * CHUNK, CHUNK), :] * ATTN_SCALE).astype(BF16)
        s = _dg(qs, kbuf[pl.ds(base, nk), :], NT)
        if not use_sink:
            s = s + extra
        if mask_first and base < p_rows:
            krow = base + _iota2(s.shape, 1)
            s = jnp.where(jnp.logical_and(i == 0, krow < p_rows), NEG_BIG, s)
        return s

    def weights(s):
        m = jnp.max(s, axis=-1, keepdims=True)
        if use_sink:
            m = jnp.maximum(m, extra)
        e = jnp.exp(s - m)
        den = jnp.sum(e, axis=-1, keepdims=True)
        if use_sink:
            den = den + jnp.exp(extra - m)
        return e.astype(BF16), 1.0 / den

    def output(j, e, inv_den):
        base = p_rows + (j - n_prev) * CHUNK
        o_all = _dg(e, vbuf[pl.ds(base, nk), :]) * inv_den
        o_ref[pl.ds(j * CHUNK, CHUNK), :] = unstack_outputs(o_all)

    for j0 in range(0, qb, ATTN_GROUP):
        group = range(j0, min(j0 + ATTN_GROUP, qb))
        ss = [scores(j) for j in group]
        ws = [weights(s) for s in ss]
        for j, (e, inv_den) in zip(group, ws):
            output(j, e, inv_den)


def _attention(q, k, v, prev, extra, *, n_prev, use_sink, n_streams, t, base_row):
    wk = k.shape[-1]
    if prev is None:
        qb = ATTN_QB
        rows = qb * CHUNK
        nblk = t // rows
        base = base_row // rows
        prev_spec = pl.BlockSpec((rows, wk), lambda s, i: (base + s * nblk + jnp.maximum(i - 1, 0), 0))
        k_prev, v_prev, p_rows, mask_first = k, v, rows, True
    else:
        qb, rows, nblk = t // CHUNK, t, 1
        base = base_row // rows
        k_prev, v_prev = prev
        p_rows = k_prev.shape[1]
        prev_spec = pl.BlockSpec((None, p_rows, wk), lambda s, i: (s, 0, 0))
        mask_first = False
    cur = lambda w: pl.BlockSpec((rows, w), lambda s, i: (base + s * nblk + i, 0))
    kern = functools.partial(_attn_kernel, qb=qb, n_prev=n_prev, use_sink=use_sink, mask_first=mask_first)
    return pl.pallas_call(
        kern,
        grid=(n_streams, nblk),
        in_specs=[cur(MIX_W), prev_spec, cur(wk), prev_spec, cur(wk),
                  pl.BlockSpec(extra.shape, lambda s, i: (0, 0))],
        out_specs=pl.BlockSpec((rows, MIX_W), lambda s, i: (s * nblk + i, 0)),
        out_shape=jax.ShapeDtypeStruct((n_streams * t, MIX_W), F32),
        scratch_shapes=[pltpu.VMEM((p_rows + rows, wk), BF16), pltpu.VMEM((p_rows + rows, wk), BF16)],
        compiler_params=_cparams(("parallel", "arbitrary")),
        name="attn_sink" if use_sink else "attn_bias",
    )(q, k_prev, k, v_prev, v, extra)


def _head_layer_norm(o, seg_mean_bf, w, b, eps):
    mu = _mm_exact_rhs(o, seg_mean_bf)
    d = o - mu
    var = _mm_exact_rhs(d * d, seg_mean_bf)
    return (d * lax.rsqrt(var + eps)) * w + b


def _ret_kernel(hc_ref, cos_ref, sin_ref, s0_ref, dstack_ref, qsc_ref, ksc_ref, gam_ref, lnw_ref, lnb_ref,
                *rest, qb):
    o_ref, sout_ref, s_scr = rest[-3:]
    i = pl.program_id(1)

    @pl.when(i == 0)
    def _():
        s_scr[...] = _heads_to_block_diag(s0_ref)

    hmask = _head_mask(N_HEADS * CHUNK)
    seg_mean = _seg_matrix(MIX_W, 1.0 / HEAD_DIM)
    rows = qb * CHUNK
    first_half = (_iota2((rows, MIX_W), 1) & (HEAD_DIM - 1)) < (HEAD_DIM // 2)
    cos = jnp.concatenate([cos_ref[...]] * (MIX_W // ROPE_W), axis=1)
    sin = jnp.concatenate([sin_ref[...]] * (MIX_W // ROPE_W), axis=1)

    def rope(x):
        partner = jnp.where(first_half, pltpu.roll(x, MIX_W - HEAD_DIM // 2, 1), pltpu.roll(x, HEAD_DIM // 2, 1))
        return x * cos + partner * sin

    q = rope(hc_ref[:, 0:256])
    k = rope(hc_ref[:, 256:512]) * ATTN_SCALE
    v_bf = hc_ref[:, 512:768].astype(BF16)
    state = s_scr[...]
    outs = []
    for j in range(qb):
        sl = slice(j * CHUNK, (j + 1) * CHUNK)
        qj, kj, vj = q[sl], k[sl], v_bf[sl]
        qs = jnp.where(hmask, _tile4(qj), 0.0).astype(BF16)
        sc = _dg(qs, kj.astype(BF16), NT) * dstack_ref[...]
        intra = _fold4(jnp.where(hmask, _dg(sc.astype(BF16), vj), 0.0))
        inter = _dg((qj * qsc_ref[...]).astype(BF16), state.astype(BF16))
        kv = _dg((kj * ksc_ref[...]).astype(BF16), vj, TN)
        state = gam_ref[...] * state + jnp.where(hmask, kv, 0.0)
        outs.append(intra + inter)
    s_scr[...] = state
    _block_diag_to_heads(state, sout_ref)
    y = _head_layer_norm(jnp.concatenate(outs, axis=0), seg_mean, lnw_ref[...], lnb_ref[...], C_GN_EPS)
    g = hc_ref[:, 768:1024]
    o_ref[...] = y * (g * _sigmoid(g))


def _retention(hc, cos, sin, s0, tabs, lnw, lnb, *, n_streams, t, base_row):
    qb = min(8, t // CHUNK)
    rows = qb * CHUNK
    nblk = t // rows
    base = base_row // rows
    dstack, qsc, ksc, gam = tabs
    full = lambda a: pl.BlockSpec(a.shape, lambda s, i: (0,) * a.ndim)
    cur = lambda w: pl.BlockSpec((rows, w), lambda s, i: (base + s * nblk + i, 0))
    state = pl.BlockSpec((None, N_HEADS, HEAD_DIM, HEAD_DIM), lambda s, i: (s, 0, 0, 0))
    return pl.pallas_call(
        functools.partial(_ret_kernel, qb=qb),
        grid=(n_streams, nblk),
        in_specs=[cur(C_PROJ),
                  pl.BlockSpec((rows, ROPE_W), lambda s, i: (i, 0)),
                  pl.BlockSpec((rows, ROPE_W), lambda s, i: (i, 0)),
                  state, full(dstack), full(qsc), full(ksc), full(gam), full(lnw), full(lnb)],
        out_specs=[pl.BlockSpec((rows, MIX_W), lambda s, i: (s * nblk + i, 0)), state],
        out_shape=[jax.ShapeDtypeStruct((n_streams * t, MIX_W), F32),
                   jax.ShapeDtypeStruct((n_streams, N_HEADS, HEAD_DIM, HEAD_DIM), F32)],
        scratch_shapes=[pltpu.VMEM((MIX_W, MIX_W), F32)],
        compiler_params=_cparams(("parallel", "arbitrary")),
        name="retention",
    )(hc, cos, sin, s0, dstack, qsc, ksc, gam, lnw, lnb)


def _retention_tables():
    gamma = 1.0 - 2.0 ** (-5.0 - np.arange(N_HEADS, dtype=np.float64))
    t = np.arange(CHUNK)
    diff = t[:, None] - t[None, :]
    dmat = np.where(diff >= 0, gamma[:, None, None] ** np.maximum(diff, 0), 0.0)
    dstack = dmat.reshape(N_HEADS * CHUNK, CHUNK)
    lanes = lambda per_head: np.repeat(per_head, HEAD_DIM, axis=-1)
    qsc = lanes(gamma[None, :] ** (t + 1)[:, None])
    ksc = lanes(gamma[None, :] ** (CHUNK - 1 - t)[:, None])
    gam = np.broadcast_to(lanes(gamma ** CHUNK)[:, None], (MIX_W, MIX_W))
    return tuple(jnp.asarray(a, F32) for a in (dstack, qsc, ksc, gam))


ROPE_W = 2 * HEAD_DIM


def _rope_tables(pos):
    half = HEAD_DIM // 2
    theta = np.float32(1.0) / (np.float32(ROPE_BASE) ** np.linspace(0.0, 1.0, half, dtype=np.float32))
    ang = np.asarray(pos, np.float32)[:, None] * theta[None, :]
    cos, sin = np.cos(ang), np.sin(ang)
    reps = ROPE_W // HEAD_DIM
    cos_t = np.tile(np.concatenate([cos, cos], axis=-1), (1, reps))
    sin_t = np.tile(np.concatenate([-sin, sin], axis=-1), (1, reps))
    return jnp.asarray(cos_t, F32), jnp.asarray(sin_t, F32)


DECAY_SCALE = 0.6065306597126334
RWKV_CB = 4
N_LEVELS = 6
MASK_HEAD, MASK_STRICT, MASK_INCL, MASK_LEVEL0 = 0, 1, 2, 3


def _rwkv_masks():
    n4 = N_HEADS * CHUNK
    ri = np.arange(n4)[:, None]
    ci = np.arange(n4)[None, :]
    head = (ri >> 6) == (ci >> 6)
    tabs = [head, head & ((ci & 63) < (ri & 63)), head & ((ci & 63) <= (ri & 63))]
    for log_m in range(N_LEVELS):
        same = (ri >> (log_m + 1)) == (ci >> (log_m + 1))
        tabs.append(same & (((ri >> log_m) & 1) == 1) & (((ci >> log_m) & 1) == 0))
    return jnp.asarray(np.stack(tabs), BF16)


def _rwkv_kernel(hb_ref, shift0_ref, h0_ref, masks_ref, mu_ref, w0_ref, w2_ref, a0_ref, a2_ref, g2_ref,
                 kk_ref, ka_ref, rk_ref, lnw_ref, lnb_ref, *rest, cb, independent):
    o_ref, hout_ref, h_scr, shift_scr = rest[-4:]
    c = pl.program_id(1)
    rows = cb * CHUNK
    xb = hb_ref[...]
    row = _iota2(xb.shape, 0)
    prev = pltpu.roll(xb, 1, 0)
    if independent:
        for j in range(cb):
            prev = jnp.where(row == j * CHUNK, shift0_ref[j], prev)
    else:
        @pl.when(c == 0)
        def _():
            h_scr[...] = _heads_to_block_diag(h0_ref.at[0])
            shift_scr[...] = shift0_ref[0]

        prev = jnp.where(row == 0, shift_scr[...], prev)
        shift_scr[...] = xb[rows - 1:rows, :]
    xs = xb + mu_ref[...] * (prev - xb)
    r = xs[:, 0:256]
    k = xs[:, 256:512]
    v = xs[:, 512:768]
    xw = xs[:, 768:832]
    xa = xs[:, 832:896]
    xg = xs[:, 896:1024]

    z = w0_ref[...] + _mm(jnp.tanh(xw), w2_ref[...], passes=3)
    lw = -DECAY_SCALE * _sigmoid(z)
    a_gate = _sigmoid(a0_ref[...] + _mm(xa, a2_ref[...], passes=3))
    gate = _mm(_sigmoid(xg), g2_ref[...], passes=1)

    seg_sum = _seg_matrix(MIX_W, 1.0)
    seg_mean = _seg_matrix(MIX_W, 1.0 / HEAD_DIM)
    kkn = k * kk_ref[...]
    norm = jnp.sqrt(_mm_exact_rhs(kkn * kkn, seg_sum))
    kk = kkn / jnp.maximum(norm, 1e-12)
    kf = k * (1.0 + (a_gate - 1.0) * ka_ref[...])

    tt = _iota2((rows, rows), 0)
    ss = _iota2((rows, rows), 1)
    tril = jnp.where(jnp.logical_and(ss <= tt, (ss >> 6) == (tt >> 6)), 1.0, 0.0).astype(BF16)
    lw_parts = _parts(lw, 3)
    cum = _dg(tril, lw_parts[0]) + (_dg(tril, lw_parts[1]) + _dg(tril, lw_parts[2]))
    w_inv = jnp.exp(-cum)
    rho = (r * jnp.exp(cum)).astype(BF16)
    alpha = (-kk * jnp.exp(cum - lw)).astype(BF16)
    beta = ((kk * a_gate) * w_inv).astype(BF16)
    kappa = (kf * w_inv).astype(BF16)
    v_bf = v.astype(BF16)

    hmask = masks_ref[MASK_HEAD]
    n4 = N_HEADS * CHUNK
    eye = jnp.where(_iota2((n4, n4), 0) == _iota2((n4, n4), 1), 1.0, 0.0)

    pre, a_bfs, t_invs = [], [], []
    for j in range(cb):
        sl = slice(j * CHUNK, (j + 1) * CHUNK)
        bd = lambda zz: _tile4(zz[sl]) * hmask
        al_bd, be_bd, ka_bd, rh_bd, v_bd = bd(alpha), bd(beta), bd(kappa), bd(rho), bd(v_bf)
        a_bf = _dg(al_bd, be_bd, NT).astype(BF16) * masks_ref[MASK_STRICT]
        a_ak = _dg(al_bd, ka_bd, NT).astype(BF16) * masks_ref[MASK_STRICT]
        b_rb = _dg(rh_bd, be_bd, NT).astype(BF16) * masks_ref[MASK_INCL]
        b_rk = _dg(rh_bd, ka_bd, NT).astype(BF16) * masks_ref[MASK_INCL]
        x0 = _dg(a_ak, v_bd)
        y0 = _dg(b_rk, v_bd)
        sn0 = _dg(v_bd, ka_bd, TN)
        w_chunk = jnp.exp(cum[(j + 1) * CHUNK - 1:(j + 1) * CHUNK, :])
        a_bfs.append(a_bf)
        t_invs.append(eye + (a_bf * masks_ref[MASK_LEVEL0]).astype(F32))
        pre.append([al_bd, be_bd, rh_bd, b_rb, None, x0, y0, sn0, w_chunk])
    for lvl in range(1, N_LEVELS):
        t_bfs = [t.astype(BF16) for t in t_invs]
        e_mats = [_dg(a_bfs[j] * masks_ref[MASK_LEVEL0 + lvl], t_bfs[j]) for j in range(cb)]
        t_invs = [t_invs[j] + _dg(t_bfs[j], e_mats[j].astype(BF16)) for j in range(cb)]
    for j in range(cb):
        pre[j][4] = t_invs[j].astype(BF16)

    ys = []
    h = None if independent else h_scr[...]
    for j in range(cb):
        al_bd, be_bd, rh_bd, b_rb, t_bf, x0, y0, sn0, w_chunk = pre[j]
        h0 = _heads_to_block_diag(h0_ref.at[j]) if independent else h
        h0_bf = h0.astype(BF16)
        x_mat = _dg(al_bd, h0_bf, NT) + x0
        u_bf = _dg(t_bf, x_mat.astype(BF16)).astype(BF16)
        y_bd = _dg(rh_bd, h0_bf, NT) + _dg(b_rb, u_bf) + y0
        h_new = (h0 + _dg(u_bf, be_bd, TN) + sn0) * w_chunk
        ys.append(_fold4(y_bd))
        if independent:
            _block_diag_to_heads(h_new, hout_ref.at[j])
        else:
            h = h_new
    if not independent:
        h_scr[...] = h
        _block_diag_to_heads(h, hout_ref.at[0])

    y = _head_layer_norm(jnp.concatenate(ys, axis=0), seg_mean, lnw_ref[...], lnb_ref[...], B_GN_EPS)
    bonus = _mm_exact_rhs(r * kf * rk_ref[...], seg_sum) * v
    o_ref[...] = (y + bonus) * gate


def _rwkv(hb, shift0, h0, params, *, independent, n_streams, t, base_row):
    cb = RWKV_CB
    rows = cb * CHUNK
    nblk = t // rows
    base = base_row // rows
    masks = _rwkv_masks()
    full = lambda a: pl.BlockSpec(a.shape, lambda s, c: (0,) * a.ndim)
    if independent:
        assert n_streams == 1
        st_idx = lambda s, c: c
        n_state, st_blk = t // CHUNK, cb
    else:
        st_idx = lambda s, c: s
        n_state, st_blk = n_streams, 1
    cur = lambda w: pl.BlockSpec((rows, w), lambda s, c: (base + s * nblk + c, 0))
    state = pl.BlockSpec((st_blk, N_HEADS, HEAD_DIM, HEAD_DIM), lambda s, c: (st_idx(s, c), 0, 0, 0))
    return pl.pallas_call(
        functools.partial(_rwkv_kernel, cb=cb, independent=independent),
        grid=(n_streams, nblk),
        in_specs=[cur(B_PROJ),
                  pl.BlockSpec((st_blk, 1, B_PROJ), lambda s, c: (st_idx(s, c), 0, 0)),
                  state,
                  full(masks)] + [full(p) for p in params],
        out_specs=[pl.BlockSpec((rows, MIX_W), lambda s, c: (s * nblk + c, 0)), state],
        out_shape=[jax.ShapeDtypeStruct((n_streams * t, MIX_W), F32),
                   jax.ShapeDtypeStruct((n_state, N_HEADS, HEAD_DIM, HEAD_DIM), F32)],
        scratch_shapes=[pltpu.VMEM((MIX_W, MIX_W), F32), pltpu.VMEM((1, B_PROJ), F32)],
        compiler_params=_cparams(("parallel", "arbitrary")),
        name="rwkv7",
    )(hb, shift0, h0, masks, *params)


def _row(p):
    return p.reshape(1, -1).astype(F32)


def _mixers(proj, caches, lp, tabs, geom):
    aq, ak, av, hb, hc, dq, dk, dv = proj
    bp, tp, bs, ts = geom
    n_p = bp * tp
    ca_k, ca_v, sb_shift, sb_wkv, sc, cd_k, cd_v = caches
    pr = dict(n_streams=bp, t=tp, base_row=0)
    sm = dict(n_streams=bs, t=ts, base_row=n_p)
    zeros_state = jnp.zeros((bp, N_HEADS, HEAD_DIM, HEAD_DIM), F32)

    oa_p = _attention(aq, ak, av, None, lp["sink_col"], n_prev=A_PREV_CHUNKS, use_sink=True, **pr)
    oa_s = _attention(aq, ak, av, (ca_k.reshape(bs, -1, A_KV_W), ca_v.reshape(bs, -1, A_KV_W)), lp["sink_col"],
                      n_prev=A_PREV_CHUNKS, use_sink=True, **sm)

    ob_p, h_p = _rwkv(hb, jnp.zeros((bp, 1, B_PROJ), F32), zeros_state, lp["rwkv"], independent=False, **pr)
    ob_s, h_s = _rwkv(hb, sb_shift.reshape(bs, 1, B_PROJ), sb_wkv.astype(F32), lp["rwkv"], independent=True,
                      n_streams=1, t=bs * ts, base_row=n_p)

    oc_p, s_p = _retention(hc, *tabs["rope_prompt"], zeros_state, tabs["ret"], lp["c_ln_w"], lp["c_ln_b"], **pr)
    oc_s, s_s = _retention(hc, *tabs["rope_sample"], sc.astype(F32), tabs["ret"], lp["c_ln_w"], lp["c_ln_b"], **sm)

    od_p = _attention(dq, dk, dv, None, lp["bias_table"], n_prev=D_PREV_CHUNKS, use_sink=False, **pr)
    od_s = _attention(dq, dk, dv, (cd_k.reshape(bs, -1, MIX_W), cd_v.reshape(bs, -1, MIX_W)), lp["bias_table"],
                      n_prev=D_PREV_CHUNKS, use_sink=False, **sm)

    mix = ((oa_p, oa_s), (ob_p, ob_s), (oc_p, oc_s), (od_p, od_s))
    return mix, (h_p, s_p), (h_s, s_s)


def kernel(x_prompt, x_sample, cache_a_k, cache_a_v, state_b_shift, state_b_wkv, state_c, cache_d_k, cache_d_v,
           norm1_g, norm2_g, w_in, w_out, a_q_norm, a_k_norm, a_sinks, b_mu, b_w0, b_w2, b_a0, b_a2, b_g2,
           b_k_k, b_k_a, b_r_k, b_ln_w, b_ln_b, c_ln_w, c_ln_b, d_q_norm, d_k_norm, d_rel_bias,
           ffn_w1, ffn_w3, ffn_w2, moe_router, moe_w1, moe_w3, moe_w2):
    bp, tp, _ = x_prompt.shape
    bs, ts, _ = x_sample.shape
    assert ts == CHUNK
    n_p, n_s = bp * tp, bs * ts
    geom = (bp, tp, bs, ts)
    xa = x_prompt.reshape(n_p, D_MODEL)
    xb = x_sample.reshape(n_s, D_MODEL)

    tabs = {
        "ret": _retention_tables(),
        "rope_prompt": _rope_tables(np.arange(tp)),
        "rope_sample": _rope_tables(PAST_LEN + np.arange(ts)),
    }
    tile = lambda g: _row(jnp.tile(g, MIX_W // HEAD_DIM))

    p_states, s_states = [], []
    for l in range(DEPTH):
        lp = {
            "sink_col": jnp.repeat(a_sinks[l].astype(F32), CHUNK).reshape(N_HEADS * CHUNK, 1),
            "bias_table": _relbias_table(d_rel_bias[l].astype(F32), (D_PREV_CHUNKS + 1) * CHUNK),
            "rwkv": (_row(b_mu[l]), _row(b_w0[l]), b_w2[l], _row(b_a0[l]), b_a2[l], b_g2[l], _row(b_k_k[l]),
                     _row(b_k_a[l]), _row(b_r_k[l]), _row(b_ln_w[l]), _row(b_ln_b[l])),
            "c_ln_w": _row(c_ln_w[l]), "c_ln_b": _row(c_ln_b[l]),
        }
        proj = _inproj(xa, xb, n_p + n_s, _row(norm1_g[l]), w_in[l].astype(BF16), tile(a_q_norm[l]),
                       _row(jnp.tile(a_k_norm[l], A_KV_W // HEAD_DIM)), tile(d_q_norm[l]), tile(d_k_norm[l]))
        _, ak, av, hb, _, _, dk, dv = proj
        caches = (cache_a_k[l], cache_a_v[l], state_b_shift[l], state_b_wkv[l], state_c[l], cache_d_k[l], cache_d_v[l])
        mix, (wkv_p, ret_p), (wkv_s, ret_s) = _mixers(proj, caches, lp, tabs, geom)
        j = l // 2
        if l % 2 == 0:
            xa = xb = _outproj_ffn(xa, xb, n_p + n_s, mix, w_out[l].astype(BF16), _row(norm2_g[l]),
                                   ffn_w1[j].astype(BF16), ffn_w3[j].astype(BF16), ffn_w2[j].astype(BF16))
        else:
            x1, xn_a, xn_b, *routing = _outproj_route(xa, xb, n_p + n_s, mix, w_out[l].astype(BF16),
                                                      _row(norm2_g[l]), _router_lanes(moe_router[j]))
            groups = ((0, n_p), (n_p, n_s)) if l == DEPTH - 1 else ((0, n_p + n_s),)
            outs = _moe(xn_a, xn_b, x1, routing,
                        moe_w1[j].astype(BF16), moe_w3[j].astype(BF16), moe_w2[j].astype(BF16), groups)
            xa, xb = (outs[0], outs[-1])

        wa = min(A_PREV_CHUNKS * CHUNK, tp)
        wd = min(D_PREV_CHUNKS * CHUNK, tp)
        tail = lambda a, w, heads: jnp.stack(
            [a[(b + 1) * tp - w:(b + 1) * tp] for b in range(bp)]).reshape(bp, w, heads, HEAD_DIM)
        last_rows = lambda a, t, first, count: jnp.concatenate(
            [a[first + (s + 1) * t - 1:first + (s + 1) * t] for s in range(count)], axis=0)
        p_states.append((tail(ak, wa, 2), tail(av, wa, 2), last_rows(hb, tp, 0, bp), wkv_p, ret_p,
                         tail(dk, wd, N_HEADS), tail(dv, wd, N_HEADS)))
        new_rows = lambda a, heads: a[n_p:].reshape(bs, ts, heads, HEAD_DIM)
        s_states.append((new_rows(ak, 2), new_rows(av, 2), last_rows(hb, ts, n_p, bs), wkv_s, ret_s,
                         new_rows(dk, N_HEADS), new_rows(dv, N_HEADS)))

    if xa is xb:
        xa, xb = xa[:n_p], xa[n_p:]
    yp = xa.reshape(bp, tp, D_MODEL)
    ys = xb.reshape(bs, ts, D_MODEL)
    st = lambda group, i: jnp.stack([g[i] for g in group], axis=0)
    roll_in = lambda cache, i: jnp.concatenate([cache.astype(F32), st(s_states, i)], axis=2)[:, :, -cache.shape[2]:]
    return (yp, ys,
            st(p_states, 0), st(p_states, 1), st(p_states, 2), st(p_states, 3), st(p_states, 4), st(p_states, 5), st(p_states, 6),
            roll_in(cache_a_k, 0), roll_in(cache_a_v, 1), st(s_states, 2), st(s_states, 3), st(s_states, 4),
            roll_in(cache_d_k, 5), roll_in(cache_d_v, 6))
```

```python
import functools

import jax
import jax.numpy as jnp
import numpy as np
from jax import lax
from jax.experimental import pallas as pl
from jax.experimental.pallas import tpu as pltpu
from jax.experimental.pallas import tpu_sc as plsc

F32 = jnp.float32
BF16 = jnp.bfloat16

D_MODEL = 1024
DEPTH = 2
PAST_LEN = 4096
CHUNK = 64
HEAD_DIM = 64
N_HEADS = 4
MIX_W = N_HEADS * HEAD_DIM
A_KV_W = 128
A_PREV_CHUNKS = 2
D_PREV_CHUNKS = 8
D_REL_CLIP = 128
B_PROJ = 1024
C_PROJ = 1024
IN_PROJ = 3328
B_GN_EPS = 64e-5
C_GN_EPS = 1e-6
NORM_EPS = 1e-6
ATTN_SCALE = 0.125
ROPE_BASE = 10000.0
D_FF = 2816
N_EXPERTS = 8
E_FF = 3584
NEG_BIG = -1e30

VMEM_LIMIT = 56 * 1024 * 1024

NN = ((1,), (0,))
NT = ((1,), (1,))
TN = ((0,), (0,))


def _dg(a, b, dims=NN):
    return lax.dot_general(a, b, (dims, ((), ())), preferred_element_type=F32)


def _parts(x, n):
    out = []
    r = x
    for i in range(n):
        p = r.astype(BF16)
        out.append(p)
        if i + 1 < n:
            r = r - p.astype(F32)
    return out


def _mm(a, b, dims=NN, passes=1):
    if passes == 1:
        return _dg(a.astype(BF16), b.astype(BF16), dims)
    ah, al = _parts(a, 2)
    bh, bl = _parts(b, 2)
    return _dg(ah, bh, dims) + (_dg(ah, bl, dims) + _dg(al, bh, dims))


def _mm_exact_rhs(a, b_bf, dims=NN, n=2):
    acc = None
    for p in _parts(a, n):
        t = _dg(p, b_bf, dims)
        acc = t if acc is None else acc + t
    return acc


def _iota2(shape, dim):
    return lax.broadcasted_iota(jnp.int32, shape, dim)


def _head_mask(rows, cols=MIX_W):
    return (_iota2((rows, cols), 0) >> 6) == (_iota2((rows, cols), 1) >> 6)


def _seg_matrix(width, value):
    m = _head_mask(width, width)
    return jnp.where(m, value, 0.0).astype(BF16)


def _tile4(z):
    return jnp.concatenate([z, z, z, z], axis=0)


def _fold4(z):
    return (z[0:64] + z[64:128]) + (z[128:192] + z[192:256])


def _heads_to_block_diag(state_ref):
    rows = []
    for h in range(N_HEADS):
        pieces = [jnp.zeros((HEAD_DIM, HEAD_DIM), F32)] * N_HEADS
        pieces[h] = state_ref[h]
        rows.append(jnp.concatenate(pieces, axis=1))
    return jnp.concatenate(rows, axis=0)


def _block_diag_to_heads(m, state_ref):
    for h in range(N_HEADS):
        state_ref[h] = m[h * HEAD_DIM:(h + 1) * HEAD_DIM, h * HEAD_DIM:(h + 1) * HEAD_DIM]


def _sigmoid(x):
    return 1.0 / (1.0 + jnp.exp(-x))


def _cparams(sem):
    return pltpu.CompilerParams(dimension_semantics=sem, vmem_limit_bytes=VMEM_LIMIT)


IN_TM = 512


def _two_source_specs(xa, xb, n):
    na, nb = xa.shape[0] // IN_TM, xb.shape[0] // IN_TM
    spec_a = pl.BlockSpec((IN_TM, D_MODEL), lambda i: (jnp.minimum(i, na - 1), 0))
    spec_b = pl.BlockSpec((IN_TM, D_MODEL), lambda i: (jnp.clip(i - na, 0, nb - 1), 0))
    return na, n // IN_TM, spec_a, spec_b


def _inproj_kernel(xa_ref, xb_ref, g_ref, w_ref, aqg_ref, akg_ref, dqg_ref, dkg_ref,
                   aq_ref, ak_ref, av_ref, hb_ref, hc_ref, dq_ref, dk_ref, dv_ref, *, n_first):
    x = jnp.where(pl.program_id(0) < n_first, xa_ref[...], xb_ref[...])
    ms = jnp.mean(x * x, axis=-1, keepdims=True)
    xn = ((x * lax.rsqrt(ms + NORM_EPS)) * g_ref[...]).astype(BF16)
    seg = _seg_matrix(MIX_W, 1.0 / HEAD_DIM)

    def proj(lo, hi):
        return jnp.dot(xn, w_ref[:, lo:hi], preferred_element_type=F32)

    def head_rms(h, gain_ref):
        w = h.shape[-1]
        msq = _mm_exact_rhs(h * h, seg[:w, :w], n=1)
        return (h * lax.rsqrt(msq + NORM_EPS)) * gain_ref[...]

    aq_ref[...] = head_rms(proj(0, 256), aqg_ref)
    ak_ref[...] = head_rms(proj(256, 384), akg_ref)
    av_ref[...] = proj(384, 512)
    hb_ref[...] = proj(512, 1536)
    hc_ref[...] = proj(1536, 2560)
    dq_ref[...] = head_rms(proj(2560, 2816), dqg_ref)
    dk_ref[...] = head_rms(proj(2816, 3072), dkg_ref)
    dv_ref[...] = proj(3072, 3328)


def _inproj(xa, xb, n, g, w_bf, aqg, akg, dqg, dkg):
    na, nblk, spec_a, spec_b = _two_source_specs(xa, xb, n)
    widths = (256, 128, 128, B_PROJ, C_PROJ, 256, 256, 256)
    row = lambda w: pl.BlockSpec((IN_TM, w), lambda i: (i, 0))
    full = lambda a: pl.BlockSpec(a.shape, lambda i: (0,) * a.ndim)
    return pl.pallas_call(
        functools.partial(_inproj_kernel, n_first=na),
        grid=(nblk,),
        in_specs=[spec_a, spec_b, full(g), full(w_bf), full(aqg), full(akg), full(dqg), full(dkg)],
        out_specs=[row(w) for w in widths],
        out_shape=[jax.ShapeDtypeStruct((n, w), F32) for w in widths],
        compiler_params=_cparams(("parallel",)),
        name="inproj",
    )(xa, xb, g, w_bf, aqg, akg, dqg, dkg)


PACK_W = 256


def _pack_bf16_pairs(hi, lo):
    bits = lambda z: pltpu.bitcast(z.astype(BF16).astype(F32), jnp.int32)
    return bits(hi) | lax.shift_right_logical(bits(lo), jnp.full(lo.shape, 16, jnp.int32))


def _unpack_bf16_pairs(w):
    hi = pltpu.bitcast(w & jnp.int32(-65536), F32)
    lo = pltpu.bitcast(lax.shift_left(w, jnp.full(w.shape, 16, jnp.int32)), F32)
    return hi, lo


def _pack_rows(x):
    return (_pack_bf16_pairs(x[:, 0:PACK_W], x[:, PACK_W:2 * PACK_W]),
            _pack_bf16_pairs(x[:, 2 * PACK_W:3 * PACK_W], x[:, 3 * PACK_W:4 * PACK_W]))


def _unpack_rows(wa, wb):
    return jnp.concatenate(_unpack_bf16_pairs(wa) + _unpack_bf16_pairs(wb), axis=1)


ROUTER_LANES = 128


ROUTER_TERMS = 3


def _router_lanes(router):
    terms = _parts(router.astype(F32), ROUTER_TERMS)
    return jnp.pad(jnp.concatenate(terms, axis=1), ((0, 0), (0, ROUTER_LANES - ROUTER_TERMS * N_EXPERTS)))


def _route(xn_bf, router_bf, counts):
    split = _dg(xn_bf, router_bf)
    logits = split
    for k in range(1, ROUTER_TERMS):
        logits = logits + pltpu.roll(split, ROUTER_LANES - k * N_EXPERTS, 1)
    lane = _iota2(logits.shape, 1)
    logits = jnp.where(lane < N_EXPERTS, logits, NEG_BIG)
    m1 = jnp.max(logits, axis=-1, keepdims=True)
    i1 = jnp.min(jnp.where(logits == m1, lane, ROUTER_LANES), axis=-1, keepdims=True)
    rest = jnp.where(lane == i1, NEG_BIG, logits)
    m2 = jnp.max(rest, axis=-1, keepdims=True)
    i2 = jnp.min(jnp.where(rest == m2, lane, ROUTER_LANES), axis=-1, keepdims=True)
    e2 = jnp.exp(m2 - m1)
    den = 1.0 + e2
    gates = jnp.where(lane == i1, 1.0 / den, 0.0) + jnp.where(lane == i2, e2 / den, 0.0)
    sel = jnp.where(gates > 0.0, 1.0, 0.0)
    tm = sel.shape[0]
    before = jnp.where(_iota2((tm, tm), 1) < _iota2((tm, tm), 0), 1.0, 0.0).astype(BF16)
    rank = (_dg(before, sel.astype(BF16)) + counts).astype(jnp.int32)
    return gates, rank, counts + jnp.sum(sel, axis=0, keepdims=True)


def _mixed_residual(i, xa_ref, xb_ref, mix_refs, w_ref, g_ref, n_first_x, n_first_mix):
    x1 = jnp.where(i < n_first_x, xa_ref[...], xb_ref[...])
    for m in range(4):
        o = jnp.where(i < n_first_mix, mix_refs[2 * m][...], mix_refs[2 * m + 1][...])
        x1 = x1 + jnp.dot(o.astype(BF16), w_ref[m * MIX_W:(m + 1) * MIX_W, :], preferred_element_type=F32)
    ms = jnp.mean(x1 * x1, axis=-1, keepdims=True)
    return x1, (x1 * lax.rsqrt(ms + NORM_EPS)) * g_ref[...]


def _outproj_route_kernel(xa_ref, xb_ref, *refs, n_first_x, n_first_mix):
    mix_refs, (w_ref, g_ref, r_ref), outs = refs[:8], refs[8:11], refs[11:]
    x1_ref, pa_ref, pb_ref, gate_ref, rank_ref, cnt_ref, cnt_scr = outs
    i = pl.program_id(0)
    acc, xn = _mixed_residual(i, xa_ref, xb_ref, mix_refs, w_ref, g_ref, n_first_x, n_first_mix)

    @pl.when(i == 0)
    def _():
        cnt_scr[...] = jnp.zeros_like(cnt_scr)

    x1_ref[...] = acc
    pa_ref[...], pb_ref[...] = _pack_rows(xn)
    gates, rank, counts = _route(xn.astype(BF16), r_ref[...], cnt_scr[...])
    gate_ref[...] = gates
    rank_ref[...] = rank
    cnt_scr[...] = counts
    cnt_ref[...] = counts.astype(jnp.int32)


def _residual_specs(xa, xb, n, mix, index):
    na, nb = xa.shape[0] // IN_TM, xb.shape[0] // IN_TM
    nm_p, nm_s = mix[0][0].shape[0] // IN_TM, mix[0][1].shape[0] // IN_TM
    first = lambda w, cnt: pl.BlockSpec((IN_TM, w), index(lambda i: jnp.minimum(i, cnt - 1)))
    second = lambda w, skip, cnt: pl.BlockSpec((IN_TM, w), index(lambda i: jnp.clip(i - skip, 0, cnt - 1)))
    specs = [first(D_MODEL, na), second(D_MODEL, na, nb)] + [first(MIX_W, nm_p), second(MIX_W, nm_p, nm_s)] * 4
    return specs, [xa, xb, *[a for pair in mix for a in pair]], dict(n_first_x=na, n_first_mix=nm_p)


def _outproj_route(xa, xb, n, mix, w_bf, g2, router_bf):
    specs, operands, statics = _residual_specs(xa, xb, n, mix, lambda blk: (lambda i: (blk(i), 0)))
    row = lambda w: pl.BlockSpec((IN_TM, w), lambda i: (i, 0))
    full = lambda a: pl.BlockSpec(a.shape, lambda i: (0,) * a.ndim)
    return pl.pallas_call(
        functools.partial(_outproj_route_kernel, **statics),
        grid=(n // IN_TM,),
        in_specs=specs + [full(w_bf), full(g2), full(router_bf)],
        out_specs=[row(D_MODEL), row(PACK_W), row(PACK_W), row(ROUTER_LANES), row(ROUTER_LANES),
                   pl.BlockSpec((1, ROUTER_LANES), lambda i: (0, 0))],
        out_shape=[jax.ShapeDtypeStruct((n, D_MODEL), F32),
                   jax.ShapeDtypeStruct((n, PACK_W), jnp.int32), jax.ShapeDtypeStruct((n, PACK_W), jnp.int32),
                   jax.ShapeDtypeStruct((n, ROUTER_LANES), F32), jax.ShapeDtypeStruct((n, ROUTER_LANES), jnp.int32),
                   jax.ShapeDtypeStruct((1, ROUTER_LANES), jnp.int32)],
        scratch_shapes=[pltpu.VMEM((1, ROUTER_LANES), F32)],
        compiler_params=_cparams(("arbitrary",)),
        name="outproj_route",
    )(*operands, w_bf, g2, router_bf)


FFN_TF = 1408


def _outproj_ffn_kernel(xa_ref, xb_ref, *refs, n_first_x, n_first_mix):
    mix_refs, rest = refs[:8], refs[8:]
    w_ref, g_ref, w1_ref, w3_ref, w2_ref, o_ref, x1_scr, xn_scr = rest
    i, f = pl.program_id(0), pl.program_id(1)

    @pl.when(f == 0)
    def _():
        x1, xn = _mixed_residual(i, xa_ref, xb_ref, mix_refs, w_ref, g_ref, n_first_x, n_first_mix)
        x1_scr[...] = x1
        xn_scr[...] = xn.astype(BF16)

    xn = xn_scr[...]
    a = jnp.dot(xn, w1_ref[...], preferred_element_type=F32)
    b = jnp.dot(xn, w3_ref[...], preferred_element_type=F32)
    h = ((a * _sigmoid(a)) * b).astype(BF16)
    y = jnp.dot(h, w2_ref[...], preferred_element_type=F32)

    @pl.when(f == 0)
    def _():
        o_ref[...] = x1_scr[...] + y

    @pl.when(f != 0)
    def _():
        o_ref[...] += y


def _outproj_ffn(xa, xb, n, mix, w_bf, g2, w1_bf, w3_bf, w2_bf):
    specs, operands, statics = _residual_specs(xa, xb, n, mix, lambda blk: (lambda i, f: (blk(i), 0)))
    full = lambda a: pl.BlockSpec(a.shape, lambda i, f: (0,) * a.ndim)
    return pl.pallas_call(
        functools.partial(_outproj_ffn_kernel, **statics),
        grid=(n // IN_TM, D_FF // FFN_TF),
        in_specs=specs + [full(w_bf), full(g2),
                          pl.BlockSpec((D_MODEL, FFN_TF), lambda i, f: (0, f)),
                          pl.BlockSpec((D_MODEL, FFN_TF), lambda i, f: (0, f)),
                          pl.BlockSpec((FFN_TF, D_MODEL), lambda i, f: (f, 0))],
        out_specs=pl.BlockSpec((IN_TM, D_MODEL), lambda i, f: (i, 0)),
        out_shape=jax.ShapeDtypeStruct((n, D_MODEL), F32),
        scratch_shapes=[pltpu.VMEM((IN_TM, D_MODEL), F32), pltpu.VMEM((IN_TM, D_MODEL), BF16)],
        compiler_params=_cparams(("parallel", "arbitrary")),
        name="outproj_ffn",
    )(*operands, w_bf, g2, w1_bf, w3_bf, w2_bf)


MOE_R = 512
MOE_TF = 1792
SC_WINDOW = 128


def _moe_plan(gates, rank, counts, n):
    n_blocks = (2 * n) // MOE_R + N_EXPERTS + 1
    spare_row = (n_blocks - 1) * MOE_R
    sel = gates[:, :N_EXPERTS] > 0.0
    rank = rank[:, :N_EXPERTS]
    counts = counts[0, :N_EXPERTS]
    padded = ((counts + MOE_R - 1) // MOE_R) * MOE_R
    pad_end = jnp.cumsum(padded)
    pad_start = pad_end - padded
    pos = jnp.where(sel, pad_start[None, :] + rank, -1)
    order = jnp.cumsum(sel.astype(jnp.int32), axis=1)
    pick = lambda j: jnp.max(jnp.where(jnp.logical_and(sel, order == j), pos, -1), axis=1)
    to_row = lambda p: jnp.where(p >= 0, p, spare_row).astype(jnp.int32).reshape(1, n)
    block_expert = jnp.minimum(
        jnp.sum(pad_end[None, :] <= (jnp.arange(n_blocks) * MOE_R)[:, None], axis=1), N_EXPERTS - 1)
    return dict(n_blocks=n_blocks, pos0=to_row(pick(1)), pos1=to_row(pick(2)),
                block_expert=block_expert.astype(jnp.int32), n_used=(pad_end[-1:] // MOE_R).astype(jnp.int32))


def _sc_mesh():
    return plsc.VectorSubcoreMesh(core_axis_name="core", subcore_axis_name="subcore")


def _sc_scatter_rows(table, idx_lists, n_rows):
    n, cols = table.shape
    k = len(idx_lists)

    @functools.partial(pl.kernel, out_type=jax.ShapeDtypeStruct((n_rows, cols), table.dtype), mesh=_sc_mesh())
    def scatter(x_hbm, *rest):
        i_hbms, o_hbm = rest[:k], rest[k]

        def body(x_vmem, *i_vmems):
            for i_vmem in i_vmems:
                pltpu.sync_copy(x_vmem, o_hbm.at[i_vmem.at[0]])

        pltpu.emit_pipeline(
            body,
            grid=(n // SC_WINDOW,),
            in_specs=[pl.BlockSpec((SC_WINDOW, cols), lambda i: (i, 0))]
            + [pl.BlockSpec((1, SC_WINDOW), lambda i: (0, i))] * k,
            out_specs=[],
            core_axis_name=("core", "subcore"),
            dimension_semantics=(pltpu.PARALLEL,),
        )(x_hbm, *i_hbms)

    return scatter(table, *idx_lists)


def _sc_gather_rows(table, idx):
    n = idx.shape[1]
    cols = table.shape[1]

    @functools.partial(pl.kernel, out_type=jax.ShapeDtypeStruct((n, cols), table.dtype), mesh=_sc_mesh())
    def gather(x_hbm, i_hbm, o_hbm):
        def body(i_vmem, o_vmem):
            pltpu.sync_copy(x_hbm.at[i_vmem.at[0]], o_vmem)

        pltpu.emit_pipeline(
            body,
            grid=(n // SC_WINDOW,),
            in_specs=[pl.BlockSpec((1, SC_WINDOW), lambda i: (0, i))],
            out_specs=[pl.BlockSpec((SC_WINDOW, cols), lambda i: (i, 0))],
            core_axis_name=("core", "subcore"),
            dimension_semantics=(pltpu.PARALLEL,),
        )(i_hbm, o_hbm)

    return gather(table, idx)


def _moe_expert_kernel(be_ref, nu_ref, xa_ref, xb_ref, gs_ref, w1_ref, w3_ref, w2_ref, oa_ref, ob_ref, acc_ref):
    j, f = pl.program_id(0), pl.program_id(1)
    used = j < nu_ref[0]

    @pl.when(used)
    def _():
        x = _unpack_rows(xa_ref[...], xb_ref[...]).astype(BF16)
        a = jnp.dot(x, w1_ref[...], preferred_element_type=F32)
        b = jnp.dot(x, w3_ref[...], preferred_element_type=F32)
        h = ((a * _sigmoid(a)) * b).astype(BF16)
        y = jnp.dot(h, w2_ref[...], preferred_element_type=F32)

        @pl.when(f == 0)
        def _():
            acc_ref[...] = y

        @pl.when(f != 0)
        def _():
            acc_ref[...] += y

    @pl.when(f == pl.num_programs(1) - 1)
    def _():
        lane = _iota2(gs_ref.shape, 1)
        g = jnp.sum(jnp.where(lane == be_ref[j], gs_ref[...], 0.0), axis=1, keepdims=True)
        oa_ref[...], ob_ref[...] = _pack_rows(jnp.where(used, acc_ref[...] * g, 0.0))


def _moe_experts(plan, xs_a, xs_b, gs, w1_bf, w3_bf, w2_bf):
    n_blocks = plan["n_blocks"]
    half = pl.BlockSpec((MOE_R, PACK_W), lambda j, f, be, nu: (j, 0))
    grid_spec = pltpu.PrefetchScalarGridSpec(
        num_scalar_prefetch=2,
        grid=(n_blocks, E_FF // MOE_TF),
        in_specs=[half, half,
                  pl.BlockSpec((MOE_R, ROUTER_LANES), lambda j, f, be, nu: (j, 0)),
                  pl.BlockSpec((None, D_MODEL, MOE_TF), lambda j, f, be, nu: (be[j], 0, f)),
                  pl.BlockSpec((None, D_MODEL, MOE_TF), lambda j, f, be, nu: (be[j], 0, f)),
                  pl.BlockSpec((None, MOE_TF, D_MODEL), lambda j, f, be, nu: (be[j], f, 0))],
        out_specs=[half, half],
        scratch_shapes=[pltpu.VMEM((MOE_R, D_MODEL), F32)])
    return pl.pallas_call(
        _moe_expert_kernel,
        grid_spec=grid_spec,
        out_shape=[jax.ShapeDtypeStruct((n_blocks * MOE_R, PACK_W), jnp.int32)] * 2,
        compiler_params=_cparams(("arbitrary", "arbitrary")),
        name="moe_experts",
    )(plan["block_expert"], plan["n_used"], xs_a, xs_b, gs, w1_bf, w3_bf, w2_bf)


def _moe_combine_kernel(x1_ref, a0_ref, b0_ref, a1_ref, b1_ref, o_ref):
    o_ref[...] = (x1_ref[...] + _unpack_rows(a0_ref[...], b0_ref[...])) + _unpack_rows(a1_ref[...], b1_ref[...])


def _moe_combine(x1, picked, row0, n_rows):
    base = row0 // IN_TM
    row = lambda w: pl.BlockSpec((IN_TM, w), lambda i: (base + i, 0))
    return pl.pallas_call(
        _moe_combine_kernel,
        grid=(n_rows // IN_TM,),
        in_specs=[row(D_MODEL)] + [row(PACK_W)] * 4,
        out_specs=pl.BlockSpec((IN_TM, D_MODEL), lambda i: (i, 0)),
        out_shape=jax.ShapeDtypeStruct((n_rows, D_MODEL), F32),
        compiler_params=_cparams(("parallel",)),
        name="moe_combine",
    )(x1, *picked)


def _moe(xn_a, xn_b, x1, routing, w1_bf, w3_bf, w2_bf, row_groups):
    gates, rank, counts = routing
    plan = _moe_plan(gates, rank, counts, x1.shape[0])
    n_rows = plan["n_blocks"] * MOE_R
    idx = (plan["pos0"], plan["pos1"])
    xs_a = _sc_scatter_rows(xn_a, idx, n_rows)
    xs_b = _sc_scatter_rows(xn_b, idx, n_rows)
    gs = _sc_scatter_rows(gates, idx, n_rows)
    os_a, os_b = _moe_experts(plan, xs_a, xs_b, gs, w1_bf, w3_bf, w2_bf)
    picked = [_sc_gather_rows(t, p) for p in idx for t in (os_a, os_b)]
    return [_moe_combine(x1, picked, row0, rows) for row0, rows in row_groups]


def _relbias_kernel(rb_ref, o_ref, *, nk):
    h = pl.program_id(0)
    q = _iota2((CHUNK, nk), 0)
    r = _iota2((CHUNK, nk), 1)
    idx = jnp.clip(q - (r - (nk - CHUNK)), -D_REL_CLIP, D_REL_CLIP) + D_REL_CLIP

    def body(j, acc):
        return jnp.where(idx == j, rb_ref[h, j], acc)

    o_ref[...] = lax.fori_loop(0, 2 * D_REL_CLIP + 1, body, jnp.zeros((CHUNK, nk), F32))


def _relbias_table(rel_bias, nk):
    return pl.pallas_call(
        functools.partial(_relbias_kernel, nk=nk),
        grid=(N_HEADS,),
        in_specs=[pl.BlockSpec(memory_space=pltpu.SMEM)],
        out_specs=pl.BlockSpec((CHUNK, nk), lambda h: (h, 0)),
        out_shape=jax.ShapeDtypeStruct((N_HEADS * CHUNK, nk), F32),
        name="relbias",
    )(rel_bias)


ATTN_QB = 8
ATTN_GROUP = 8


def _attn_kernel(q_ref, kp_ref, kc_ref, vp_ref, vc_ref, x_ref, *rest, qb, n_prev, use_sink, mask_first):
    o_ref, kbuf, vbuf = rest[-3:]
    i = pl.program_id(1)
    p_rows = kp_ref.shape[0]
    nk = (n_prev + 1) * CHUNK
    wk = kp_ref.shape[1]

    kbuf[0:p_rows, :] = kp_ref[...].astype(BF16)
    kbuf[p_rows:, :] = kc_ref[...].astype(BF16)
    vbuf[0:p_rows, :] = vp_ref[...].astype(BF16)
    vbuf[p_rows:, :] = vc_ref[...].astype(BF16)

    hmask = _head_mask(N_HEADS * CHUNK)
    extra = x_ref[...]
    grouped = wk != MIX_W
    low = _iota2((CHUNK, A_KV_W), 1) < HEAD_DIM

    def stack_queries(qj):
        if not grouped:
            return jnp.where(hmask, _tile4(qj), 0.0)
        shifted = pltpu.roll(qj, MIX_W - HEAD_DIM, 1)[:, :A_KV_W]
        return jnp.concatenate([jnp.where(low, qj[:, :A_KV_W], 0.0), jnp.where(low, shifted, 0.0),
                                jnp.where(low, 0.0, shifted), jnp.where(low, 0.0, qj[:, A_KV_W:])], axis=0)

    def unstack_outputs(o_all):
        if not grouped:
            return _fold4(jnp.where(hmask, o_all, 0.0))
        b0, b1, b2, b3 = (o_all[h * CHUNK:(h + 1) * CHUNK] for h in range(N_HEADS))
        left = jnp.where(low, b0, 0.0) + pltpu.roll(jnp.where(low, b1, 0.0), HEAD_DIM, 1)
        right = pltpu.roll(jnp.where(low, 0.0, b2), HEAD_DIM, 1) + jnp.where(low, 0.0, b3)
        return jnp.concatenate([left, right], axis=1)

    def scores(j):
        base = p_rows + (j - n_prev) * CHUNK
        qs = stack_queries(q_ref[pl.ds(j * CHUNK, CHUNK), :] * ATTN_SCALE).astype(BF16)
        s = _dg(qs, kbuf[pl.ds(base, nk), :], NT)
        if not use_sink:
            s = s + extra
        if mask_first and base < p_rows:
            krow = base + _iota2(s.shape, 1)
            s = jnp.where(jnp.logical_and(i == 0, krow < p_rows), NEG_BIG, s)
        return s

    def weights(s):
        m = jnp.max(s, axis=-1, keepdims=True)
        if use_sink:
            m = jnp.maximum(m, extra)
        e = jnp.exp(s - m)
        den = jnp.sum(e, axis=-1, keepdims=True)
        if use_sink:
            den = den + jnp.exp(extra - m)
        return e.astype(BF16), 1.0 / den

    def output(j, e, inv_den):
        base = p_rows + (j - n_prev) * CHUNK
        o_all = _dg(e, vbuf[pl.ds(base, nk), :]) * inv_den
        o_ref[pl.ds(j * CHUNK, CHUNK), :] = unstack_outputs(o_all)

    for j0 in range(0, qb, ATTN_GROUP):
        group = range(j0, min(j0 + ATTN_GROUP, qb))
        ss = [scores(j) for j in group]
        ws = [weights(s) for s in ss]
        for j, (e, inv_den) in zip(group, ws):
            output(j, e, inv_den)


def _attention(q, k, v, prev, extra, *, n_prev, use_sink, n_streams, t, base_row):
    wk = k.shape[-1]
    if prev is None:
        qb = ATTN_QB
        rows = qb * CHUNK
        nblk = t // rows
        base = base_row // rows
        prev_spec = pl.BlockSpec((rows, wk), lambda s, i: (base + s * nblk + jnp.maximum(i - 1, 0), 0))
        k_prev, v_prev, p_rows, mask_first = k, v, rows, True
    else:
        qb, rows, nblk = t // CHUNK, t, 1
        base = base_row // rows
        k_prev, v_prev = prev
        p_rows = k_prev.shape[1]
        prev_spec = pl.BlockSpec((None, p_rows, wk), lambda s, i: (s, 0, 0))
        mask_first = False
    cur = lambda w: pl.BlockSpec((rows, w), lambda s, i: (base + s * nblk + i, 0))
    kern = functools.partial(_attn_kernel, qb=qb, n_prev=n_prev, use_sink=use_sink, mask_first=mask_first)
    return pl.pallas_call(
        kern,
        grid=(n_streams, nblk),
        in_specs=[cur(MIX_W), prev_spec, cur(wk), prev_spec, cur(wk),
                  pl.BlockSpec(extra.shape, lambda s, i: (0, 0))],
        out_specs=pl.BlockSpec((rows, MIX_W), lambda s, i: (s * nblk + i, 0)),
        out_shape=jax.ShapeDtypeStruct((n_streams * t, MIX_W), F32),
        scratch_shapes=[pltpu.VMEM((p_rows + rows, wk), BF16), pltpu.VMEM((p_rows + rows, wk), BF16)],
        compiler_params=_cparams(("parallel", "arbitrary")),
        name="attn_sink" if use_sink else "attn_bias",
    )(q, k_prev, k, v_prev, v, extra)


def _head_layer_norm(o, seg_mean_bf, w, b, eps):
    mu = _mm_exact_rhs(o, seg_mean_bf)
    d = o - mu
    var = _mm_exact_rhs(d * d, seg_mean_bf)
    return (d * lax.rsqrt(var + eps)) * w + b


def _ret_kernel(hc_ref, cos_ref, sin_ref, s0_ref, dstack_ref, qsc_ref, ksc_ref, gam_ref, lnw_ref, lnb_ref,
                *rest, qb):
    o_ref, sout_ref, s_scr = rest[-3:]
    i = pl.program_id(1)

    @pl.when(i == 0)
    def _():
        s_scr[...] = _heads_to_block_diag(s0_ref)

    hmask = _head_mask(N_HEADS * CHUNK)
    seg_mean = _seg_matrix(MIX_W, 1.0 / HEAD_DIM)
    rows = qb * CHUNK
    first_half = (_iota2((rows, MIX_W), 1) & (HEAD_DIM - 1)) < (HEAD_DIM // 2)
    cos = jnp.concatenate([cos_ref[...]] * (MIX_W // ROPE_W), axis=1)
    sin = jnp.concatenate([sin_ref[...]] * (MIX_W // ROPE_W), axis=1)

    def rope(x):
        partner = jnp.where(first_half, pltpu.roll(x, MIX_W - HEAD_DIM // 2, 1), pltpu.roll(x, HEAD_DIM // 2, 1))
        return x * cos + partner * sin

    q = rope(hc_ref[:, 0:256])
    k = rope(hc_ref[:, 256:512]) * ATTN_SCALE
    v_bf = hc_ref[:, 512:768].astype(BF16)
    state = s_scr[...]
    outs = []
    for j in range(qb):
        sl = slice(j * CHUNK, (j + 1) * CHUNK)
        qj, kj, vj = q[sl], k[sl], v_bf[sl]
        qs = jnp.where(hmask, _tile4(qj), 0.0).astype(BF16)
        sc = _dg(qs, kj.astype(BF16), NT) * dstack_ref[...]
        intra = _fold4(jnp.where(hmask, _dg(sc.astype(BF16), vj), 0.0))
        inter = _dg((qj * qsc_ref[...]).astype(BF16), state.astype(BF16))
        kv = _dg((kj * ksc_ref[...]).astype(BF16), vj, TN)
        state = gam_ref[...] * state + jnp.where(hmask, kv, 0.0)
        outs.append(intra + inter)
    s_scr[...] = state
    _block_diag_to_heads(state, sout_ref)
    y = _head_layer_norm(jnp.concatenate(outs, axis=0), seg_mean, lnw_ref[...], lnb_ref[...], C_GN_EPS)
    g = hc_ref[:, 768:1024]
    o_ref[...] = y * (g * _sigmoid(g))


def _retention(hc, cos, sin, s0, tabs, lnw, lnb, *, n_streams, t, base_row):
    qb = min(8, t // CHUNK)
    rows = qb * CHUNK
    nblk = t // rows
    base = base_row // rows
    dstack, qsc, ksc, gam = tabs
    full = lambda a: pl.BlockSpec(a.shape, lambda s, i: (0,) * a.ndim)
    cur = lambda w: pl.BlockSpec((rows, w), lambda s, i: (base + s * nblk + i, 0))
    state = pl.BlockSpec((None, N_HEADS, HEAD_DIM, HEAD_DIM), lambda s, i: (s, 0, 0, 0))
    return pl.pallas_call(
        functools.partial(_ret_kernel, qb=qb),
        grid=(n_streams, nblk),
        in_specs=[cur(C_PROJ),
                  pl.BlockSpec((rows, ROPE_W), lambda s, i: (i, 0)),
                  pl.BlockSpec((rows, ROPE_W), lambda s, i: (i, 0)),
                  state, full(dstack), full(qsc), full(ksc), full(gam), full(lnw), full(lnb)],
        out_specs=[pl.BlockSpec((rows, MIX_W), lambda s, i: (s * nblk + i, 0)), state],
        out_shape=[jax.ShapeDtypeStruct((n_streams * t, MIX_W), F32),
                   jax.ShapeDtypeStruct((n_streams, N_HEADS, HEAD_DIM, HEAD_DIM), F32)],
        scratch_shapes=[pltpu.VMEM((MIX_W, MIX_W), F32)],
        compiler_params=_cparams(("parallel", "arbitrary")),
        name="retention",
    )(hc, cos, sin, s0, dstack, qsc, ksc, gam, lnw, lnb)


def _retention_tables():
    gamma = 1.0 - 2.0 ** (-5.0 - np.arange(N_HEADS, dtype=np.float64))
    t = np.arange(CHUNK)
    diff = t[:, None] - t[None, :]
    dmat = np.where(diff >= 0, gamma[:, None, None] ** np.maximum(diff, 0), 0.0)
    dstack = dmat.reshape(N_HEADS * CHUNK, CHUNK)
    lanes = lambda per_head: np.repeat(per_head, HEAD_DIM, axis=-1)
    qsc = lanes(gamma[None, :] ** (t + 1)[:, None])
    ksc = lanes(gamma[None, :] ** (CHUNK - 1 - t)[:, None])
    gam = np.broadcast_to(lanes(gamma ** CHUNK)[:, None], (MIX_W, MIX_W))
    return tuple(jnp.asarray(a, F32) for a in (dstack, qsc, ksc, gam))


ROPE_W = 2 * HEAD_DIM


def _rope_tables(pos):
    half = HEAD_DIM // 2
    theta = np.float32(1.0) / (np.float32(ROPE_BASE) ** np.linspace(0.0, 1.0, half, dtype=np.float32))
    ang = np.asarray(pos, np.float32)[:, None] * theta[None, :]
    cos, sin = np.cos(ang), np.sin(ang)
    reps = ROPE_W // HEAD_DIM
    cos_t = np.tile(np.concatenate([cos, cos], axis=-1), (1, reps))
    sin_t = np.tile(np.concatenate([-sin, sin], axis=-1), (1, reps))
    return jnp.asarray(cos_t, F32), jnp.asarray(sin_t, F32)


DECAY_SCALE = 0.6065306597126334
RWKV_CB = 4
N_LEVELS = 6
MASK_HEAD, MASK_STRICT, MASK_INCL, MASK_LEVEL0 = 0, 1, 2, 3


def _rwkv_masks():
    n4 = N_HEADS * CHUNK
    ri = np.arange(n4)[:, None]
    ci = np.arange(n4)[None, :]
    head = (ri >> 6) == (ci >> 6)
    tabs = [head, head & ((ci & 63) < (ri & 63)), head & ((ci & 63) <= (ri & 63))]
    for log_m in range(N_LEVELS):
        same = (ri >> (log_m + 1)) == (ci >> (log_m + 1))
        tabs.append(same & (((ri >> log_m) & 1) == 1) & (((ci >> log_m) & 1) == 0))
    return jnp.asarray(np.stack(tabs), BF16)


def _rwkv_kernel(*refs, n_hb, n_seg, seg_chunks, carry):
    hb_refs, rest = refs[:n_hb], refs[n_hb:]
    (shift0_ref, h0_ref, masks_ref, mu_ref, w0_ref, w2_ref, a0_ref, a2_ref, g2_ref,
     kk_ref, ka_ref, rk_ref, lnw_ref, lnb_ref, o_ref, hout_ref, h_scr, shift_scr) = rest
    c = pl.program_id(0)
    cb = n_seg * seg_chunks
    seg_rows = seg_chunks * CHUNK
    rows = cb * CHUNK
    xb = jnp.concatenate([ref[...] for ref in hb_refs], axis=0)
    row = _iota2(xb.shape, 0)
    prev = pltpu.roll(xb, 1, 0)
    if carry:
        @pl.when(c == 0)
        def _():
            for s in range(n_seg):
                h_scr[s] = _heads_to_block_diag(h0_ref.at[s])
                shift_scr[s] = shift0_ref[s]

    for s in range(n_seg):
        first = shift_scr[s] if carry else shift0_ref[s]
        prev = jnp.where(row == s * seg_rows, first, prev)
    if carry:
        for s in range(n_seg):
            shift_scr[s] = xb[(s + 1) * seg_rows - 1:(s + 1) * seg_rows, :]
    xs = xb + mu_ref[...] * (prev - xb)
    r = xs[:, 0:256]
    k = xs[:, 256:512]
    v = xs[:, 512:768]
    xw = xs[:, 768:832]
    xa = xs[:, 832:896]
    xg = xs[:, 896:1024]

    z = w0_ref[...] + _mm(jnp.tanh(xw), w2_ref[...], passes=3)
    lw = -DECAY_SCALE * _sigmoid(z)
    a_gate = _sigmoid(a0_ref[...] + _mm(xa, a2_ref[...], passes=3))
    gate = _mm(_sigmoid(xg), g2_ref[...], passes=1)

    seg_sum = _seg_matrix(MIX_W, 1.0)
    seg_mean = _seg_matrix(MIX_W, 1.0 / HEAD_DIM)
    kkn = k * kk_ref[...]
    norm = jnp.sqrt(_mm_exact_rhs(kkn * kkn, seg_sum))
    kk = kkn / jnp.maximum(norm, 1e-12)
    kf = k * (1.0 + (a_gate - 1.0) * ka_ref[...])

    tt = _iota2((rows, rows), 0)
    ss = _iota2((rows, rows), 1)
    tril = jnp.where(jnp.logical_and(ss <= tt, (ss >> 6) == (tt >> 6)), 1.0, 0.0).astype(BF16)
    lw_parts = _parts(lw, 3)
    cum = _dg(tril, lw_parts[0]) + (_dg(tril, lw_parts[1]) + _dg(tril, lw_parts[2]))
    w_inv = jnp.exp(-cum)
    rho = (r * jnp.exp(cum)).astype(BF16)
    alpha = (-kk * jnp.exp(cum - lw)).astype(BF16)
    beta = ((kk * a_gate) * w_inv).astype(BF16)
    kappa = (kf * w_inv).astype(BF16)
    v_bf = v.astype(BF16)

    hmask = masks_ref[MASK_HEAD]
    n4 = N_HEADS * CHUNK
    eye = jnp.where(_iota2((n4, n4), 0) == _iota2((n4, n4), 1), 1.0, 0.0)

    pre, a_bfs, t_invs = [], [], []
    for j in range(cb):
        sl = slice(j * CHUNK, (j + 1) * CHUNK)
        bd = lambda zz: _tile4(zz[sl]) * hmask
        al_bd, be_bd, ka_bd, rh_bd, v_bd = bd(alpha), bd(beta), bd(kappa), bd(rho), bd(v_bf)
        a_bf = _dg(al_bd, be_bd, NT).astype(BF16) * masks_ref[MASK_STRICT]
        a_ak = _dg(al_bd, ka_bd, NT).astype(BF16) * masks_ref[MASK_STRICT]
        b_rb = _dg(rh_bd, be_bd, NT).astype(BF16) * masks_ref[MASK_INCL]
        b_rk = _dg(rh_bd, ka_bd, NT).astype(BF16) * masks_ref[MASK_INCL]
        x0 = _dg(a_ak, v_bd)
        y0 = _dg(b_rk, v_bd)
        sn0 = _dg(v_bd, ka_bd, TN)
        w_chunk = jnp.exp(cum[(j + 1) * CHUNK - 1:(j + 1) * CHUNK, :])
        a_bfs.append(a_bf)
        t_invs.append(eye + (a_bf * masks_ref[MASK_LEVEL0]).astype(F32))
        pre.append([al_bd, be_bd, rh_bd, b_rb, None, x0, y0, sn0, w_chunk])
    for lvl in range(1, N_LEVELS):
        t_bfs = [t.astype(BF16) for t in t_invs]
        e_mats = [_dg(a_bfs[j] * masks_ref[MASK_LEVEL0 + lvl], t_bfs[j]) for j in range(cb)]
        t_invs = [t_invs[j] + _dg(t_bfs[j], e_mats[j].astype(BF16)) for j in range(cb)]
    for j in range(cb):
        pre[j][4] = t_invs[j].astype(BF16)

    h = [h_scr[s] if carry else _heads_to_block_diag(h0_ref.at[s]) for s in range(n_seg)]
    ys = [None] * cb
    for j in range(seg_chunks):
        for s in range(n_seg):
            q = s * seg_chunks + j
            al_bd, be_bd, rh_bd, b_rb, t_bf, x0, y0, sn0, w_chunk = pre[q]
            h0_bf = h[s].astype(BF16)
            x_mat = _dg(al_bd, h0_bf, NT) + x0
            u_bf = _dg(t_bf, x_mat.astype(BF16)).astype(BF16)
            y_bd = _dg(rh_bd, h0_bf, NT) + _dg(b_rb, u_bf) + y0
            h[s] = (h[s] + _dg(u_bf, be_bd, TN) + sn0) * w_chunk
            ys[q] = _fold4(y_bd)
    for s in range(n_seg):
        if carry:
            h_scr[s] = h[s]
        _block_diag_to_heads(h[s], hout_ref.at[s])

    y = _head_layer_norm(jnp.concatenate(ys, axis=0), seg_mean, lnw_ref[...], lnb_ref[...], B_GN_EPS)
    bonus = _mm_exact_rhs(r * kf * rk_ref[...], seg_sum) * v
    out = (y + bonus) * gate
    for s in range(n_seg):
        o_ref[s] = out[s * seg_rows:(s + 1) * seg_rows]


def _rwkv(hb, shift0, h0, params, *, independent, n_streams, t, base_row):
    masks = _rwkv_masks()
    full = lambda a: pl.BlockSpec(a.shape, lambda c: (0,) * a.ndim)
    if independent:
        assert n_streams == 1
        n_state, n_seg, seg_chunks = t // CHUNK, RWKV_CB, 1
        rows = n_seg * CHUNK
        nblk, base = t // rows, base_row // rows
        hb_specs = [pl.BlockSpec((rows, B_PROJ), lambda c: (base + c, 0))]
        st_idx = lambda c: c
    else:
        n_state, n_seg, seg_chunks = n_streams, n_streams, RWKV_CB
        rows = seg_chunks * CHUNK
        nblk, base = t // rows, base_row // rows
        stream_spec = lambda s: pl.BlockSpec((rows, B_PROJ), lambda c: (base + s * nblk + c, 0))
        hb_specs = [stream_spec(s) for s in range(n_streams)]
        st_idx = lambda c: 0
    seg_rows = seg_chunks * CHUNK
    state = pl.BlockSpec((n_seg, N_HEADS, HEAD_DIM, HEAD_DIM), lambda c: (st_idx(c), 0, 0, 0))
    out_idx = (lambda c: (c, 0, 0)) if independent else (lambda c: (0, c, 0))
    o, h_out = pl.pallas_call(
        functools.partial(_rwkv_kernel, n_hb=len(hb_specs), n_seg=n_seg, seg_chunks=seg_chunks,
                          carry=not independent),
        grid=(nblk,),
        in_specs=hb_specs + [pl.BlockSpec((n_seg, 1, B_PROJ), lambda c: (st_idx(c), 0, 0)), state,
                             full(masks)] + [full(p) for p in params],
        out_specs=[pl.BlockSpec((n_seg, seg_rows, MIX_W), out_idx), state],
        out_shape=[jax.ShapeDtypeStruct((n_state, CHUNK if independent else t, MIX_W), F32),
                   jax.ShapeDtypeStruct((n_state, N_HEADS, HEAD_DIM, HEAD_DIM), F32)],
        scratch_shapes=[pltpu.VMEM((n_seg, MIX_W, MIX_W), F32), pltpu.VMEM((n_seg, 1, B_PROJ), F32)],
        compiler_params=_cparams(("arbitrary",)),
        name="rwkv7",
    )(*([hb] * len(hb_specs)), shift0, h0, masks, *params)
    return o.reshape(-1, MIX_W), h_out


def _row(p):
    return p.reshape(1, -1).astype(F32)


def _mixers(proj, caches, lp, tabs, geom):
    aq, ak, av, hb, hc, dq, dk, dv = proj
    bp, tp, bs, ts = geom
    n_p = bp * tp
    ca_k, ca_v, sb_shift, sb_wkv, sc, cd_k, cd_v = caches
    pr = dict(n_streams=bp, t=tp, base_row=0)
    sm = dict(n_streams=bs, t=ts, base_row=n_p)
    zeros_state = jnp.zeros((bp, N_HEADS, HEAD_DIM, HEAD_DIM), F32)

    oa_p = _attention(aq, ak, av, None, lp["sink_col"], n_prev=A_PREV_CHUNKS, use_sink=True, **pr)
    oa_s = _attention(aq, ak, av, (ca_k.reshape(bs, -1, A_KV_W), ca_v.reshape(bs, -1, A_KV_W)), lp["sink_col"],
                      n_prev=A_PREV_CHUNKS, use_sink=True, **sm)

    ob_p, h_p = _rwkv(hb, jnp.zeros((bp, 1, B_PROJ), F32), zeros_state, lp["rwkv"], independent=False, **pr)
    ob_s, h_s = _rwkv(hb, sb_shift.reshape(bs, 1, B_PROJ), sb_wkv.astype(F32), lp["rwkv"], independent=True,
                      n_streams=1, t=bs * ts, base_row=n_p)

    oc_p, s_p = _retention(hc, *tabs["rope_prompt"], zeros_state, tabs["ret"], lp["c_ln_w"], lp["c_ln_b"], **pr)
    oc_s, s_s = _retention(hc, *tabs["rope_sample"], sc.astype(F32), tabs["ret"], lp["c_ln_w"], lp["c_ln_b"], **sm)

    od_p = _attention(dq, dk, dv, None, lp["bias_table"], n_prev=D_PREV_CHUNKS, use_sink=False, **pr)
    od_s = _attention(dq, dk, dv, (cd_k.reshape(bs, -1, MIX_W), cd_v.reshape(bs, -1, MIX_W)), lp["bias_table"],
                      n_prev=D_PREV_CHUNKS, use_sink=False, **sm)

    mix = ((oa_p, oa_s), (ob_p, ob_s), (oc_p, oc_s), (od_p, od_s))
    return mix, (h_p, s_p), (h_s, s_s)


def kernel(x_prompt, x_sample, cache_a_k, cache_a_v, state_b_shift, state_b_wkv, state_c, cache_d_k, cache_d_v,
           norm1_g, norm2_g, w_in, w_out, a_q_norm, a_k_norm, a_sinks, b_mu, b_w0, b_w2, b_a0, b_a2, b_g2,
           b_k_k, b_k_a, b_r_k, b_ln_w, b_ln_b, c_ln_w, c_ln_b, d_q_norm, d_k_norm, d_rel_bias,
           ffn_w1, ffn_w3, ffn_w2, moe_router, moe_w1, moe_w3, moe_w2):
    bp, tp, _ = x_prompt.shape
    bs, ts, _ = x_sample.shape
    assert ts == CHUNK
    n_p, n_s = bp * tp, bs * ts
    geom = (bp, tp, bs, ts)
    xa = x_prompt.reshape(n_p, D_MODEL)
    xb = x_sample.reshape(n_s, D_MODEL)

    tabs = {
        "ret": _retention_tables(),
        "rope_prompt": _rope_tables(np.arange(tp)),
        "rope_sample": _rope_tables(PAST_LEN + np.arange(ts)),
    }
    tile = lambda g: _row(jnp.tile(g, MIX_W // HEAD_DIM))

    p_states, s_states = [], []
    for l in range(DEPTH):
        lp = {
            "sink_col": jnp.repeat(a_sinks[l].astype(F32), CHUNK).reshape(N_HEADS * CHUNK, 1),
            "bias_table": _relbias_table(d_rel_bias[l].astype(F32), (D_PREV_CHUNKS + 1) * CHUNK),
            "rwkv": (_row(b_mu[l]), _row(b_w0[l]), b_w2[l], _row(b_a0[l]), b_a2[l], b_g2[l], _row(b_k_k[l]),
                     _row(b_k_a[l]), _row(b_r_k[l]), _row(b_ln_w[l]), _row(b_ln_b[l])),
            "c_ln_w": _row(c_ln_w[l]), "c_ln_b": _row(c_ln_b[l]),
        }
        proj = _inproj(xa, xb, n_p + n_s, _row(norm1_g[l]), w_in[l].astype(BF16), tile(a_q_norm[l]),
                       _row(jnp.tile(a_k_norm[l], A_KV_W // HEAD_DIM)), tile(d_q_norm[l]), tile(d_k_norm[l]))
        _, ak, av, hb, _, _, dk, dv = proj
        caches = (cache_a_k[l], cache_a_v[l], state_b_shift[l], state_b_wkv[l], state_c[l], cache_d_k[l], cache_d_v[l])
        mix, (wkv_p, ret_p), (wkv_s, ret_s) = _mixers(proj, caches, lp, tabs, geom)
        j = l // 2
        if l % 2 == 0:
            xa = xb = _outproj_ffn(xa, xb, n_p + n_s, mix, w_out[l].astype(BF16), _row(norm2_g[l]),
                                   ffn_w1[j].astype(BF16), ffn_w3[j].astype(BF16), ffn_w2[j].astype(BF16))
        else:
            x1, xn_a, xn_b, *routing = _outproj_route(xa, xb, n_p + n_s, mix, w_out[l].astype(BF16),
                                                      _row(norm2_g[l]), _router_lanes(moe_router[j]))
            groups = ((0, n_p), (n_p, n_s)) if l == DEPTH - 1 else ((0, n_p + n_s),)
            outs = _moe(xn_a, xn_b, x1, routing,
                        moe_w1[j].astype(BF16), moe_w3[j].astype(BF16), moe_w2[j].astype(BF16), groups)
            xa, xb = (outs[0], outs[-1])

        wa = min(A_PREV_CHUNKS * CHUNK, tp)
        wd = min(D_PREV_CHUNKS * CHUNK, tp)
        tail = lambda a, w, heads: jnp.stack(
            [a[(b + 1) * tp - w:(b + 1) * tp] for b in range(bp)]).reshape(bp, w, heads, HEAD_DIM)
        last_rows = lambda a, t, first, count: jnp.concatenate(
            [a[first + (s + 1) * t - 1:first + (s + 1) * t] for s in range(count)], axis=0)
        p_states.append((tail(ak, wa, 2), tail(av, wa, 2), last_rows(hb, tp, 0, bp), wkv_p, ret_p,
                         tail(dk, wd, N_HEADS), tail(dv, wd, N_HEADS)))
        new_rows = lambda a, heads: a[n_p:].reshape(bs, ts, heads, HEAD_DIM)
        s_states.append((new_rows(ak, 2), new_rows(av, 2), last_rows(hb, ts, n_p, bs), wkv_s, ret_s,
                         new_rows(dk, N_HEADS), new_rows(dv, N_HEADS)))

    if xa is xb:
        xa, xb = xa[:n_p], xa[n_p:]
    yp = xa.reshape(bp, tp, D_MODEL)
    ys = xb.reshape(bs, ts, D_MODEL)
    st = lambda group, i: jnp.stack([g[i] for g in group], axis=0)
    roll_in = lambda cache, i: jnp.concatenate([cache.astype(F32), st(s_states, i)], axis=2)[:, :, -cache.shape[2]:]
    return (yp, ys,
            st(p_states, 0), st(p_states, 1), st(p_states, 2), st(p_states, 3), st(p_states, 4), st(p_states, 5), st(p_states, 6),
            roll_in(cache_a_k, 0), roll_in(cache_a_v, 1), st(s_states, 2), st(s_states, 3), st(s_states, 4),
            roll_in(cache_d_k, 5), roll_in(cache_d_v, 6))
```

```python
import functools

import jax
import jax.numpy as jnp
import numpy as np
from jax import lax
from jax.experimental import pallas as pl
from jax.experimental.pallas import tpu as pltpu
from jax.experimental.pallas import tpu_sc as plsc

F32 = jnp.float32
BF16 = jnp.bfloat16

D_MODEL = 1024
DEPTH = 2
PAST_LEN = 4096
CHUNK = 64
HEAD_DIM = 64
N_HEADS = 4
MIX_W = N_HEADS * HEAD_DIM
A_KV_W = 128
A_PREV_CHUNKS = 2
D_PREV_CHUNKS = 8
D_REL_CLIP = 128
B_PROJ = 1024
C_PROJ = 1024
IN_PROJ = 3328
B_GN_EPS = 64e-5
C_GN_EPS = 1e-6
NORM_EPS = 1e-6
ATTN_SCALE = 0.125
ROPE_BASE = 10000.0
D_FF = 2816
N_EXPERTS = 8
E_FF = 3584
NEG_BIG = -1e30

VMEM_LIMIT = 56 * 1024 * 1024

NN = ((1,), (0,))
NT = ((1,), (1,))
TN = ((0,), (0,))


def _dg(a, b, dims=NN):
    return lax.dot_general(a, b, (dims, ((), ())), preferred_element_type=F32)


def _parts(x, n):
    out = []
    r = x
    for i in range(n):
        p = r.astype(BF16)
        out.append(p)
        if i + 1 < n:
            r = r - p.astype(F32)
    return out


def _mm(a, b, dims=NN, passes=1):
    if passes == 1:
        return _dg(a.astype(BF16), b.astype(BF16), dims)
    ah, al = _parts(a, 2)
    bh, bl = _parts(b, 2)
    return _dg(ah, bh, dims) + (_dg(ah, bl, dims) + _dg(al, bh, dims))


def _mm_exact_rhs(a, b_bf, dims=NN, n=2):
    acc = None
    for p in _parts(a, n):
        t = _dg(p, b_bf, dims)
        acc = t if acc is None else acc + t
    return acc


def _iota2(shape, dim):
    return lax.broadcasted_iota(jnp.int32, shape, dim)


def _head_mask(rows, cols=MIX_W):
    return (_iota2((rows, cols), 0) >> 6) == (_iota2((rows, cols), 1) >> 6)


def _seg_matrix(width, value):
    m = _head_mask(width, width)
    return jnp.where(m, value, 0.0).astype(BF16)


def _tile4(z):
    return jnp.concatenate([z, z, z, z], axis=0)


def _fold4(z):
    return (z[0:64] + z[64:128]) + (z[128:192] + z[192:256])


def _heads_to_block_diag(state_ref):
    rows = []
    for h in range(N_HEADS):
        pieces = [jnp.zeros((HEAD_DIM, HEAD_DIM), F32)] * N_HEADS
        pieces[h] = state_ref[h]
        rows.append(jnp.concatenate(pieces, axis=1))
    return jnp.concatenate(rows, axis=0)


def _block_diag_to_heads(m, state_ref):
    for h in range(N_HEADS):
        state_ref[h] = m[h * HEAD_DIM:(h + 1) * HEAD_DIM, h * HEAD_DIM:(h + 1) * HEAD_DIM]


def _sigmoid(x):
    return 1.0 / (1.0 + jnp.exp(-x))


def _cparams(sem):
    return pltpu.CompilerParams(dimension_semantics=sem, vmem_limit_bytes=VMEM_LIMIT)


IN_TM = 512


def _two_source_specs(xa, xb, n):
    na, nb = xa.shape[0] // IN_TM, xb.shape[0] // IN_TM
    spec_a = pl.BlockSpec((IN_TM, D_MODEL), lambda i: (jnp.minimum(i, na - 1), 0))
    spec_b = pl.BlockSpec((IN_TM, D_MODEL), lambda i: (jnp.clip(i - na, 0, nb - 1), 0))
    return na, n // IN_TM, spec_a, spec_b


def _inproj_kernel(xa_ref, xb_ref, g_ref, w_ref, aqg_ref, akg_ref, dqg_ref, dkg_ref,
                   aq_ref, ak_ref, av_ref, hb_ref, hc_ref, dq_ref, dk_ref, dv_ref, *, n_first):
    x = jnp.where(pl.program_id(0) < n_first, xa_ref[...], xb_ref[...])
    ms = jnp.mean(x * x, axis=-1, keepdims=True)
    xn = ((x * lax.rsqrt(ms + NORM_EPS)) * g_ref[...]).astype(BF16)
    seg = _seg_matrix(MIX_W, 1.0 / HEAD_DIM)

    def proj(lo, hi):
        return jnp.dot(xn, w_ref[:, lo:hi], preferred_element_type=F32)

    def head_rms(h, gain_ref):
        w = h.shape[-1]
        msq = _mm_exact_rhs(h * h, seg[:w, :w], n=1)
        return (h * lax.rsqrt(msq + NORM_EPS)) * gain_ref[...]

    aq_ref[...] = head_rms(proj(0, 256), aqg_ref)
    ak_ref[...] = head_rms(proj(256, 384), akg_ref)
    av_ref[...] = proj(384, 512)
    hb_ref[...] = proj(512, 1536)
    hc_ref[...] = proj(1536, 2560)
    dq_ref[...] = head_rms(proj(2560, 2816), dqg_ref)
    dk_ref[...] = head_rms(proj(2816, 3072), dkg_ref)
    dv_ref[...] = proj(3072, 3328)


def _inproj(xa, xb, n, g, w_bf, aqg, akg, dqg, dkg):
    na, nblk, spec_a, spec_b = _two_source_specs(xa, xb, n)
    widths = (256, 128, 128, B_PROJ, C_PROJ, 256, 256, 256)
    row = lambda w: pl.BlockSpec((IN_TM, w), lambda i: (i, 0))
    full = lambda a: pl.BlockSpec(a.shape, lambda i: (0,) * a.ndim)
    return pl.pallas_call(
        functools.partial(_inproj_kernel, n_first=na),
        grid=(nblk,),
        in_specs=[spec_a, spec_b, full(g), full(w_bf), full(aqg), full(akg), full(dqg), full(dkg)],
        out_specs=[row(w) for w in widths],
        out_shape=[jax.ShapeDtypeStruct((n, w), F32) for w in widths],
        compiler_params=_cparams(("parallel",)),
        name="inproj",
    )(xa, xb, g, w_bf, aqg, akg, dqg, dkg)


PACK_W = 256


def _pack_bf16_pairs(hi, lo):
    bits = lambda z: pltpu.bitcast(z.astype(BF16).astype(F32), jnp.int32)
    return bits(hi) | lax.shift_right_logical(bits(lo), jnp.full(lo.shape, 16, jnp.int32))


def _unpack_bf16_pairs(w):
    hi = pltpu.bitcast(w & jnp.int32(-65536), F32)
    lo = pltpu.bitcast(lax.shift_left(w, jnp.full(w.shape, 16, jnp.int32)), F32)
    return hi, lo


def _pack_rows(x):
    return (_pack_bf16_pairs(x[:, 0:PACK_W], x[:, PACK_W:2 * PACK_W]),
            _pack_bf16_pairs(x[:, 2 * PACK_W:3 * PACK_W], x[:, 3 * PACK_W:4 * PACK_W]))


def _unpack_rows(wa, wb):
    return jnp.concatenate(_unpack_bf16_pairs(wa) + _unpack_bf16_pairs(wb), axis=1)


ROUTER_LANES = 128


ROUTER_TERMS = 3


def _router_lanes(router):
    terms = _parts(router.astype(F32), ROUTER_TERMS)
    return jnp.pad(jnp.concatenate(terms, axis=1), ((0, 0), (0, ROUTER_LANES - ROUTER_TERMS * N_EXPERTS)))


def _route(xn_bf, router_bf, counts):
    split = _dg(xn_bf, router_bf)
    logits = split
    for k in range(1, ROUTER_TERMS):
        logits = logits + pltpu.roll(split, ROUTER_LANES - k * N_EXPERTS, 1)
    lane = _iota2(logits.shape, 1)
    logits = jnp.where(lane < N_EXPERTS, logits, NEG_BIG)
    m1 = jnp.max(logits, axis=-1, keepdims=True)
    i1 = jnp.min(jnp.where(logits == m1, lane, ROUTER_LANES), axis=-1, keepdims=True)
    rest = jnp.where(lane == i1, NEG_BIG, logits)
    m2 = jnp.max(rest, axis=-1, keepdims=True)
    i2 = jnp.min(jnp.where(rest == m2, lane, ROUTER_LANES), axis=-1, keepdims=True)
    e2 = jnp.exp(m2 - m1)
    den = 1.0 + e2
    gates = jnp.where(lane == i1, 1.0 / den, 0.0) + jnp.where(lane == i2, e2 / den, 0.0)
    sel = jnp.where(gates > 0.0, 1.0, 0.0)
    tm = sel.shape[0]
    before = jnp.where(_iota2((tm, tm), 1) < _iota2((tm, tm), 0), 1.0, 0.0).astype(BF16)
    rank = (_dg(before, sel.astype(BF16)) + counts).astype(jnp.int32)
    return gates, rank, counts + jnp.sum(sel, axis=0, keepdims=True)


def _mixed_residual(i, xa_ref, xb_ref, mix_refs, w_ref, g_ref, n_first_x, n_first_mix):
    x1 = jnp.where(i < n_first_x, xa_ref[...], xb_ref[...])
    for m in range(4):
        o = jnp.where(i < n_first_mix, mix_refs[2 * m][...], mix_refs[2 * m + 1][...])
        x1 = x1 + jnp.dot(o.astype(BF16), w_ref[m * MIX_W:(m + 1) * MIX_W, :], preferred_element_type=F32)
    ms = jnp.mean(x1 * x1, axis=-1, keepdims=True)
    return x1, (x1 * lax.rsqrt(ms + NORM_EPS)) * g_ref[...]


def _outproj_route_kernel(xa_ref, xb_ref, *refs, n_first_x, n_first_mix):
    mix_refs, (w_ref, g_ref, r_ref), outs = refs[:8], refs[8:11], refs[11:]
    x1_ref, pa_ref, pb_ref, gate_ref, rank_ref, cnt_ref, cnt_scr = outs
    i = pl.program_id(0)
    acc, xn = _mixed_residual(i, xa_ref, xb_ref, mix_refs, w_ref, g_ref, n_first_x, n_first_mix)

    @pl.when(i == 0)
    def _():
        cnt_scr[...] = jnp.zeros_like(cnt_scr)

    x1_ref[...] = acc
    pa_ref[...], pb_ref[...] = _pack_rows(xn)
    gates, rank, counts = _route(xn.astype(BF16), r_ref[...], cnt_scr[...])
    gate_ref[...] = gates
    rank_ref[...] = rank
    cnt_scr[...] = counts
    cnt_ref[...] = counts.astype(jnp.int32)


def _residual_specs(xa, xb, n, mix, index):
    na, nb = xa.shape[0] // IN_TM, xb.shape[0] // IN_TM
    nm_p, nm_s = mix[0][0].shape[0] // IN_TM, mix[0][1].shape[0] // IN_TM
    first = lambda w, cnt: pl.BlockSpec((IN_TM, w), index(lambda i: jnp.minimum(i, cnt - 1)))
    second = lambda w, skip, cnt: pl.BlockSpec((IN_TM, w), index(lambda i: jnp.clip(i - skip, 0, cnt - 1)))
    specs = [first(D_MODEL, na), second(D_MODEL, na, nb)] + [first(MIX_W, nm_p), second(MIX_W, nm_p, nm_s)] * 4
    return specs, [xa, xb, *[a for pair in mix for a in pair]], dict(n_first_x=na, n_first_mix=nm_p)


def _outproj_route(xa, xb, n, mix, w_bf, g2, router_bf):
    specs, operands, statics = _residual_specs(xa, xb, n, mix, lambda blk: (lambda i: (blk(i), 0)))
    row = lambda w: pl.BlockSpec((IN_TM, w), lambda i: (i, 0))
    full = lambda a: pl.BlockSpec(a.shape, lambda i: (0,) * a.ndim)
    return pl.pallas_call(
        functools.partial(_outproj_route_kernel, **statics),
        grid=(n // IN_TM,),
        in_specs=specs + [full(w_bf), full(g2), full(router_bf)],
        out_specs=[row(D_MODEL), row(PACK_W), row(PACK_W), row(ROUTER_LANES), row(ROUTER_LANES),
                   pl.BlockSpec((1, ROUTER_LANES), lambda i: (0, 0))],
        out_shape=[jax.ShapeDtypeStruct((n, D_MODEL), F32),
                   jax.ShapeDtypeStruct((n, PACK_W), jnp.int32), jax.ShapeDtypeStruct((n, PACK_W), jnp.int32),
                   jax.ShapeDtypeStruct((n, ROUTER_LANES), F32), jax.ShapeDtypeStruct((n, ROUTER_LANES), jnp.int32),
                   jax.ShapeDtypeStruct((1, ROUTER_LANES), jnp.int32)],
        scratch_shapes=[pltpu.VMEM((1, ROUTER_LANES), F32)],
        compiler_params=_cparams(("arbitrary",)),
        name="outproj_route",
    )(*operands, w_bf, g2, router_bf)


FFN_TF = 1408


def _outproj_ffn_kernel(xa_ref, xb_ref, *refs, n_first_x, n_first_mix):
    mix_refs, rest = refs[:8], refs[8:]
    w_ref, g_ref, w1_ref, w3_ref, w2_ref, o_ref, x1_scr, xn_scr = rest
    i, f = pl.program_id(0), pl.program_id(1)

    @pl.when(f == 0)
    def _():
        x1, xn = _mixed_residual(i, xa_ref, xb_ref, mix_refs, w_ref, g_ref, n_first_x, n_first_mix)
        x1_scr[...] = x1
        xn_scr[...] = xn.astype(BF16)

    xn = xn_scr[...]
    a = jnp.dot(xn, w1_ref[...], preferred_element_type=F32)
    b = jnp.dot(xn, w3_ref[...], preferred_element_type=F32)
    h = ((a * _sigmoid(a)) * b).astype(BF16)
    y = jnp.dot(h, w2_ref[...], preferred_element_type=F32)

    @pl.when(f == 0)
    def _():
        o_ref[...] = x1_scr[...] + y

    @pl.when(f != 0)
    def _():
        o_ref[...] += y


def _outproj_ffn(xa, xb, n, mix, w_bf, g2, w1_bf, w3_bf, w2_bf):
    specs, operands, statics = _residual_specs(xa, xb, n, mix, lambda blk: (lambda i, f: (blk(i), 0)))
    full = lambda a: pl.BlockSpec(a.shape, lambda i, f: (0,) * a.ndim)
    return pl.pallas_call(
        functools.partial(_outproj_ffn_kernel, **statics),
        grid=(n // IN_TM, D_FF // FFN_TF),
        in_specs=specs + [full(w_bf), full(g2),
                          pl.BlockSpec((D_MODEL, FFN_TF), lambda i, f: (0, f)),
                          pl.BlockSpec((D_MODEL, FFN_TF), lambda i, f: (0, f)),
                          pl.BlockSpec((FFN_TF, D_MODEL), lambda i, f: (f, 0))],
        out_specs=pl.BlockSpec((IN_TM, D_MODEL), lambda i, f: (i, 0)),
        out_shape=jax.ShapeDtypeStruct((n, D_MODEL), F32),
        scratch_shapes=[pltpu.VMEM((IN_TM, D_MODEL), F32), pltpu.VMEM((IN_TM, D_MODEL), BF16)],
        compiler_params=_cparams(("parallel", "arbitrary")),
        name="outproj_ffn",
    )(*operands, w_bf, g2, w1_bf, w3_bf, w2_bf)


MOE_R = 512
MOE_TF = 1792
SC_WINDOW = 128


def _moe_plan(gates, rank, counts, n):
    n_blocks = (2 * n) // MOE_R + N_EXPERTS + 1
    spare_row = (n_blocks - 1) * MOE_R
    sel = gates[:, :N_EXPERTS] > 0.0
    rank = rank[:, :N_EXPERTS]
    counts = counts[0, :N_EXPERTS]
    padded = ((counts + MOE_R - 1) // MOE_R) * MOE_R
    pad_end = jnp.cumsum(padded)
    pad_start = pad_end - padded
    pos = jnp.where(sel, pad_start[None, :] + rank, -1)
    order = jnp.cumsum(sel.astype(jnp.int32), axis=1)
    pick = lambda j: jnp.max(jnp.where(jnp.logical_and(sel, order == j), pos, -1), axis=1)
    to_row = lambda p: jnp.where(p >= 0, p, spare_row).astype(jnp.int32).reshape(1, n)
    block_expert = jnp.minimum(
        jnp.sum(pad_end[None, :] <= (jnp.arange(n_blocks) * MOE_R)[:, None], axis=1), N_EXPERTS - 1)
    return dict(n_blocks=n_blocks, pos0=to_row(pick(1)), pos1=to_row(pick(2)),
                block_expert=block_expert.astype(jnp.int32), n_used=(pad_end[-1:] // MOE_R).astype(jnp.int32))


def _sc_mesh():
    return plsc.VectorSubcoreMesh(core_axis_name="core", subcore_axis_name="subcore")


def _sc_scatter_rows(table, idx_lists, n_rows):
    n, cols = table.shape
    k = len(idx_lists)

    @functools.partial(pl.kernel, out_type=jax.ShapeDtypeStruct((n_rows, cols), table.dtype), mesh=_sc_mesh())
    def scatter(x_hbm, *rest):
        i_hbms, o_hbm = rest[:k], rest[k]

        def body(x_vmem, *i_vmems):
            for i_vmem in i_vmems:
                pltpu.sync_copy(x_vmem, o_hbm.at[i_vmem.at[0]])

        pltpu.emit_pipeline(
            body,
            grid=(n // SC_WINDOW,),
            in_specs=[pl.BlockSpec((SC_WINDOW, cols), lambda i: (i, 0))]
            + [pl.BlockSpec((1, SC_WINDOW), lambda i: (0, i))] * k,
            out_specs=[],
            core_axis_name=("core", "subcore"),
            dimension_semantics=(pltpu.PARALLEL,),
        )(x_hbm, *i_hbms)

    return scatter(table, *idx_lists)


def _sc_gather_rows(table, idx):
    n = idx.shape[1]
    cols = table.shape[1]

    @functools.partial(pl.kernel, out_type=jax.ShapeDtypeStruct((n, cols), table.dtype), mesh=_sc_mesh())
    def gather(x_hbm, i_hbm, o_hbm):
        def body(i_vmem, o_vmem):
            pltpu.sync_copy(x_hbm.at[i_vmem.at[0]], o_vmem)

        pltpu.emit_pipeline(
            body,
            grid=(n // SC_WINDOW,),
            in_specs=[pl.BlockSpec((1, SC_WINDOW), lambda i: (0, i))],
            out_specs=[pl.BlockSpec((SC_WINDOW, cols), lambda i: (i, 0))],
            core_axis_name=("core", "subcore"),
            dimension_semantics=(pltpu.PARALLEL,),
        )(i_hbm, o_hbm)

    return gather(table, idx)


def _moe_expert_kernel(be_ref, nu_ref, xa_ref, xb_ref, gs_ref, w1_ref, w3_ref, w2_ref, oa_ref, ob_ref, acc_ref):
    j, f = pl.program_id(0), pl.program_id(1)
    used = j < nu_ref[0]

    @pl.when(used)
    def _():
        x = _unpack_rows(xa_ref[...], xb_ref[...]).astype(BF16)
        a = jnp.dot(x, w1_ref[...], preferred_element_type=F32)
        b = jnp.dot(x, w3_ref[...], preferred_element_type=F32)
        h = ((a * _sigmoid(a)) * b).astype(BF16)
        y = jnp.dot(h, w2_ref[...], preferred_element_type=F32)

        @pl.when(f == 0)
        def _():
            acc_ref[...] = y

        @pl.when(f != 0)
        def _():
            acc_ref[...] += y

    @pl.when(f == pl.num_programs(1) - 1)
    def _():
        lane = _iota2(gs_ref.shape, 1)
        g = jnp.sum(jnp.where(lane == be_ref[j], gs_ref[...], 0.0), axis=1, keepdims=True)
        oa_ref[...], ob_ref[...] = _pack_rows(jnp.where(used, acc_ref[...] * g, 0.0))


def _moe_experts(plan, xs_a, xs_b, gs, w1_bf, w3_bf, w2_bf):
    n_blocks = plan["n_blocks"]
    half = pl.BlockSpec((MOE_R, PACK_W), lambda j, f, be, nu: (j, 0))
    grid_spec = pltpu.PrefetchScalarGridSpec(
        num_scalar_prefetch=2,
        grid=(n_blocks, E_FF // MOE_TF),
        in_specs=[half, half,
                  pl.BlockSpec((MOE_R, ROUTER_LANES), lambda j, f, be, nu: (j, 0)),
                  pl.BlockSpec((None, D_MODEL, MOE_TF), lambda j, f, be, nu: (be[j], 0, f)),
                  pl.BlockSpec((None, D_MODEL, MOE_TF), lambda j, f, be, nu: (be[j], 0, f)),
                  pl.BlockSpec((None, MOE_TF, D_MODEL), lambda j, f, be, nu: (be[j], f, 0))],
        out_specs=[half, half],
        scratch_shapes=[pltpu.VMEM((MOE_R, D_MODEL), F32)])
    return pl.pallas_call(
        _moe_expert_kernel,
        grid_spec=grid_spec,
        out_shape=[jax.ShapeDtypeStruct((n_blocks * MOE_R, PACK_W), jnp.int32)] * 2,
        compiler_params=_cparams(("arbitrary", "arbitrary")),
        name="moe_experts",
    )(plan["block_expert"], plan["n_used"], xs_a, xs_b, gs, w1_bf, w3_bf, w2_bf)


def _moe_combine_kernel(x1_ref, a0_ref, b0_ref, a1_ref, b1_ref, o_ref):
    o_ref[...] = (x1_ref[...] + _unpack_rows(a0_ref[...], b0_ref[...])) + _unpack_rows(a1_ref[...], b1_ref[...])


def _moe_combine(x1, picked, row0, n_rows):
    base = row0 // IN_TM
    row = lambda w: pl.BlockSpec((IN_TM, w), lambda i: (base + i, 0))
    return pl.pallas_call(
        _moe_combine_kernel,
        grid=(n_rows // IN_TM,),
        in_specs=[row(D_MODEL)] + [row(PACK_W)] * 4,
        out_specs=pl.BlockSpec((IN_TM, D_MODEL), lambda i: (i, 0)),
        out_shape=jax.ShapeDtypeStruct((n_rows, D_MODEL), F32),
        compiler_params=_cparams(("parallel",)),
        name="moe_combine",
    )(x1, *picked)


def _moe(xn_a, xn_b, x1, routing, w1_bf, w3_bf, w2_bf, row_groups):
    gates, rank, counts = routing
    plan = _moe_plan(gates, rank, counts, x1.shape[0])
    n_rows = plan["n_blocks"] * MOE_R
    idx = (plan["pos0"], plan["pos1"])
    xs_a = _sc_scatter_rows(xn_a, idx, n_rows)
    xs_b = _sc_scatter_rows(xn_b, idx, n_rows)
    gs = _sc_scatter_rows(gates, idx, n_rows)
    os_a, os_b = _moe_experts(plan, xs_a, xs_b, gs, w1_bf, w3_bf, w2_bf)
    picked = [_sc_gather_rows(t, p) for p in idx for t in (os_a, os_b)]
    return [_moe_combine(x1, picked, row0, rows) for row0, rows in row_groups]


def _relbias_kernel(rb_ref, o_ref, *, nk):
    h = pl.program_id(0)
    q = _iota2((CHUNK, nk), 0)
    r = _iota2((CHUNK, nk), 1)
    idx = jnp.clip(q - (r - (nk - CHUNK)), -D_REL_CLIP, D_REL_CLIP) + D_REL_CLIP

    def body(j, acc):
        return jnp.where(idx == j, rb_ref[h, j], acc)

    o_ref[...] = lax.fori_loop(0, 2 * D_REL_CLIP + 1, body, jnp.zeros((CHUNK, nk), F32))


def _relbias_table(rel_bias, nk):
    return pl.pallas_call(
        functools.partial(_relbias_kernel, nk=nk),
        grid=(N_HEADS,),
        in_specs=[pl.BlockSpec(memory_space=pltpu.SMEM)],
        out_specs=pl.BlockSpec((CHUNK, nk), lambda h: (h, 0)),
        out_shape=jax.ShapeDtypeStruct((N_HEADS * CHUNK, nk), F32),
        name="relbias",
    )(rel_bias)


ATTN_QB = 8
ATTN_GROUP = 8


def _attn_kernel(q_ref, kp_ref, kc_ref, vp_ref, vc_ref, x_ref, *rest, qb, n_grp, n_prev, use_sink, mask_first):
    o_ref, kbuf, vbuf = rest[-3:]
    i = pl.program_id(1)
    p_rows = kp_ref.shape[-2]
    nk = (n_prev + 1) * CHUNK
    wk = kp_ref.shape[-1]
    grp_rows = qb * CHUNK

    for g in range(n_grp):
        cur = pl.ds(g * grp_rows, grp_rows)
        kbuf[g, 0:p_rows, :] = (kp_ref[g] if len(kp_ref.shape) == 3 else kp_ref[...]).astype(BF16)
        kbuf[g, p_rows:, :] = kc_ref[cur, :].astype(BF16)
        vbuf[g, 0:p_rows, :] = (vp_ref[g] if len(vp_ref.shape) == 3 else vp_ref[...]).astype(BF16)
        vbuf[g, p_rows:, :] = vc_ref[cur, :].astype(BF16)

    hmask = _head_mask(N_HEADS * CHUNK)
    extra = x_ref[...]
    grouped = wk != MIX_W
    low = _iota2((CHUNK, A_KV_W), 1) < HEAD_DIM

    def stack_queries(qj):
        if not grouped:
            return jnp.where(hmask, _tile4(qj), 0.0)
        shifted = pltpu.roll(qj, MIX_W - HEAD_DIM, 1)[:, :A_KV_W]
        return jnp.concatenate([jnp.where(low, qj[:, :A_KV_W], 0.0), jnp.where(low, shifted, 0.0),
                                jnp.where(low, 0.0, shifted), jnp.where(low, 0.0, qj[:, A_KV_W:])], axis=0)

    def unstack_outputs(o_all):
        if not grouped:
            return _fold4(jnp.where(hmask, o_all, 0.0))
        b0, b1, b2, b3 = (o_all[h * CHUNK:(h + 1) * CHUNK] for h in range(N_HEADS))
        left = jnp.where(low, b0, 0.0) + pltpu.roll(jnp.where(low, b1, 0.0), HEAD_DIM, 1)
        right = pltpu.roll(jnp.where(low, 0.0, b2), HEAD_DIM, 1) + jnp.where(low, 0.0, b3)
        return jnp.concatenate([left, right], axis=1)

    def scores(g, j):
        base = p_rows + (j - n_prev) * CHUNK
        qs = stack_queries(q_ref[pl.ds(g * grp_rows + j * CHUNK, CHUNK), :] * ATTN_SCALE).astype(BF16)
        s = _dg(qs, kbuf[g, pl.ds(base, nk), :], NT)
        if not use_sink:
            s = s + extra
        if mask_first and base < p_rows:
            krow = base + _iota2(s.shape, 1)
            s = jnp.where(jnp.logical_and(i == 0, krow < p_rows), NEG_BIG, s)
        return s

    def weights(s):
        m = jnp.max(s, axis=-1, keepdims=True)
        if use_sink:
            m = jnp.maximum(m, extra)
        e = jnp.exp(s - m)
        den = jnp.sum(e, axis=-1, keepdims=True)
        if use_sink:
            den = den + jnp.exp(extra - m)
        return e.astype(BF16), 1.0 / den

    def output(g, j, e, inv_den):
        base = p_rows + (j - n_prev) * CHUNK
        o_all = _dg(e, vbuf[g, pl.ds(base, nk), :]) * inv_den
        o_ref[pl.ds(g * grp_rows + j * CHUNK, CHUNK), :] = unstack_outputs(o_all)

    chunks = [(g, j) for g in range(n_grp) for j in range(qb)]
    for c0 in range(0, len(chunks), ATTN_GROUP):
        group = chunks[c0:c0 + ATTN_GROUP]
        ss = [scores(g, j) for g, j in group]
        ws = [weights(s) for s in ss]
        for (g, j), (e, inv_den) in zip(group, ws):
            output(g, j, e, inv_den)


def _attention(q, k, v, prev, extra, *, n_prev, use_sink, n_streams, t, base_row):
    wk = k.shape[-1]
    if prev is None:
        qb = ATTN_QB
        rows = qb * CHUNK
        nblk = t // rows
        base = base_row // rows
        prev_spec = pl.BlockSpec((rows, wk), lambda s, i: (base + s * nblk + jnp.maximum(i - 1, 0), 0))
        k_prev, v_prev, p_rows, mask_first = k, v, rows, True
        n_grp, n_steps = 1, n_streams
    else:
        n_grp = ATTN_QB // (t // CHUNK)
        assert n_grp >= 1 and n_streams % n_grp == 0
        qb, rows, nblk, n_steps = t // CHUNK, n_grp * t, 1, n_streams // n_grp
        base = base_row // rows
        k_prev, v_prev = prev
        p_rows = k_prev.shape[1]
        prev_spec = pl.BlockSpec((n_grp, p_rows, wk), lambda s, i: (s, 0, 0))
        mask_first = False
    cur = lambda w: pl.BlockSpec((rows, w), lambda s, i: (base + s * nblk + i, 0))
    kern = functools.partial(_attn_kernel, qb=qb, n_grp=n_grp, n_prev=n_prev, use_sink=use_sink,
                             mask_first=mask_first)
    buf = pltpu.VMEM((n_grp, p_rows + rows // n_grp, wk), BF16)
    return pl.pallas_call(
        kern,
        grid=(n_steps, nblk),
        in_specs=[cur(MIX_W), prev_spec, cur(wk), prev_spec, cur(wk),
                  pl.BlockSpec(extra.shape, lambda s, i: (0, 0))],
        out_specs=pl.BlockSpec((rows, MIX_W), lambda s, i: (s * nblk + i, 0)),
        out_shape=jax.ShapeDtypeStruct((n_streams * t, MIX_W), F32),
        scratch_shapes=[buf, buf],
        compiler_params=_cparams(("parallel", "arbitrary")),
        name="attn_sink" if use_sink else "attn_bias",
    )(q, k_prev, k, v_prev, v, extra)


def _head_layer_norm(o, seg_mean_bf, w, b, eps):
    mu = _mm_exact_rhs(o, seg_mean_bf)
    d = o - mu
    var = _mm_exact_rhs(d * d, seg_mean_bf)
    return (d * lax.rsqrt(var + eps)) * w + b


def _ret_kernel(hc_ref, cos_ref, sin_ref, s0_ref, dstack_ref, qsc_ref, ksc_ref, gam_ref, lnw_ref, lnb_ref,
                *rest, qb):
    o_ref, sout_ref, s_scr = rest[-3:]
    i = pl.program_id(1)

    @pl.when(i == 0)
    def _():
        s_scr[...] = _heads_to_block_diag(s0_ref)

    hmask = _head_mask(N_HEADS * CHUNK)
    seg_mean = _seg_matrix(MIX_W, 1.0 / HEAD_DIM)
    rows = qb * CHUNK
    first_half = (_iota2((rows, MIX_W), 1) & (HEAD_DIM - 1)) < (HEAD_DIM // 2)
    cos = jnp.concatenate([cos_ref[...]] * (MIX_W // ROPE_W), axis=1)
    sin = jnp.concatenate([sin_ref[...]] * (MIX_W // ROPE_W), axis=1)

    def rope(x):
        partner = jnp.where(first_half, pltpu.roll(x, MIX_W - HEAD_DIM // 2, 1), pltpu.roll(x, HEAD_DIM // 2, 1))
        return x * cos + partner * sin

    q = rope(hc_ref[:, 0:256])
    k = rope(hc_ref[:, 256:512]) * ATTN_SCALE
    v_bf = hc_ref[:, 512:768].astype(BF16)
    state = s_scr[...]
    outs = []
    for j in range(qb):
        sl = slice(j * CHUNK, (j + 1) * CHUNK)
        qj, kj, vj = q[sl], k[sl], v_bf[sl]
        qs = jnp.where(hmask, _tile4(qj), 0.0).astype(BF16)
        sc = _dg(qs, kj.astype(BF16), NT) * dstack_ref[...]
        intra = _fold4(jnp.where(hmask, _dg(sc.astype(BF16), vj), 0.0))
        inter = _dg((qj * qsc_ref[...]).astype(BF16), state.astype(BF16))
        kv = _dg((kj * ksc_ref[...]).astype(BF16), vj, TN)
        state = gam_ref[...] * state + jnp.where(hmask, kv, 0.0)
        outs.append(intra + inter)
    s_scr[...] = state
    _block_diag_to_heads(state, sout_ref)
    y = _head_layer_norm(jnp.concatenate(outs, axis=0), seg_mean, lnw_ref[...], lnb_ref[...], C_GN_EPS)
    g = hc_ref[:, 768:1024]
    o_ref[...] = y * (g * _sigmoid(g))


def _retention(hc, cos, sin, s0, tabs, lnw, lnb, *, n_streams, t, base_row):
    qb = min(8, t // CHUNK)
    rows = qb * CHUNK
    nblk = t // rows
    base = base_row // rows
    dstack, qsc, ksc, gam = tabs
    full = lambda a: pl.BlockSpec(a.shape, lambda s, i: (0,) * a.ndim)
    cur = lambda w: pl.BlockSpec((rows, w), lambda s, i: (base + s * nblk + i, 0))
    state = pl.BlockSpec((None, N_HEADS, HEAD_DIM, HEAD_DIM), lambda s, i: (s, 0, 0, 0))
    return pl.pallas_call(
        functools.partial(_ret_kernel, qb=qb),
        grid=(n_streams, nblk),
        in_specs=[cur(C_PROJ),
                  pl.BlockSpec((rows, ROPE_W), lambda s, i: (i, 0)),
                  pl.BlockSpec((rows, ROPE_W), lambda s, i: (i, 0)),
                  state, full(dstack), full(qsc), full(ksc), full(gam), full(lnw), full(lnb)],
        out_specs=[pl.BlockSpec((rows, MIX_W), lambda s, i: (s * nblk + i, 0)), state],
        out_shape=[jax.ShapeDtypeStruct((n_streams * t, MIX_W), F32),
                   jax.ShapeDtypeStruct((n_streams, N_HEADS, HEAD_DIM, HEAD_DIM), F32)],
        scratch_shapes=[pltpu.VMEM((MIX_W, MIX_W), F32)],
        compiler_params=_cparams(("parallel", "arbitrary")),
        name="retention",
    )(hc, cos, sin, s0, dstack, qsc, ksc, gam, lnw, lnb)


def _retention_tables():
    gamma = 1.0 - 2.0 ** (-5.0 - np.arange(N_HEADS, dtype=np.float64))
    t = np.arange(CHUNK)
    diff = t[:, None] - t[None, :]
    dmat = np.where(diff >= 0, gamma[:, None, None] ** np.maximum(diff, 0), 0.0)
    dstack = dmat.reshape(N_HEADS * CHUNK, CHUNK)
    lanes = lambda per_head: np.repeat(per_head, HEAD_DIM, axis=-1)
    qsc = lanes(gamma[None, :] ** (t + 1)[:, None])
    ksc = lanes(gamma[None, :] ** (CHUNK - 1 - t)[:, None])
    gam = np.broadcast_to(lanes(gamma ** CHUNK)[:, None], (MIX_W, MIX_W))
    return tuple(jnp.asarray(a, F32) for a in (dstack, qsc, ksc, gam))


ROPE_W = 2 * HEAD_DIM


def _rope_tables(pos):
    half = HEAD_DIM // 2
    theta = np.float32(1.0) / (np.float32(ROPE_BASE) ** np.linspace(0.0, 1.0, half, dtype=np.float32))
    ang = np.asarray(pos, np.float32)[:, None] * theta[None, :]
    cos, sin = np.cos(ang), np.sin(ang)
    reps = ROPE_W // HEAD_DIM
    cos_t = np.tile(np.concatenate([cos, cos], axis=-1), (1, reps))
    sin_t = np.tile(np.concatenate([-sin, sin], axis=-1), (1, reps))
    return jnp.asarray(cos_t, F32), jnp.asarray(sin_t, F32)


DECAY_SCALE = 0.6065306597126334
RWKV_CB = 4
N_LEVELS = 6
MASK_HEAD, MASK_STRICT, MASK_INCL, MASK_LEVEL0 = 0, 1, 2, 3


def _rwkv_masks():
    n4 = N_HEADS * CHUNK
    ri = np.arange(n4)[:, None]
    ci = np.arange(n4)[None, :]
    head = (ri >> 6) == (ci >> 6)
    tabs = [head, head & ((ci & 63) < (ri & 63)), head & ((ci & 63) <= (ri & 63))]
    for log_m in range(N_LEVELS):
        same = (ri >> (log_m + 1)) == (ci >> (log_m + 1))
        tabs.append(same & (((ri >> log_m) & 1) == 1) & (((ci >> log_m) & 1) == 0))
    return jnp.asarray(np.stack(tabs), BF16)


def _rwkv_kernel(*refs, n_hb, n_seg, seg_chunks, carry):
    hb_refs, rest = refs[:n_hb], refs[n_hb:]
    (shift0_ref, h0_ref, masks_ref, mu_ref, w0_ref, w2_ref, a0_ref, a2_ref, g2_ref,
     kk_ref, ka_ref, rk_ref, lnw_ref, lnb_ref, o_ref, hout_ref, h_scr, shift_scr) = rest
    c = pl.program_id(0)
    cb = n_seg * seg_chunks
    seg_rows = seg_chunks * CHUNK
    rows = cb * CHUNK
    xb = jnp.concatenate([ref[...] for ref in hb_refs], axis=0)
    row = _iota2(xb.shape, 0)
    prev = pltpu.roll(xb, 1, 0)
    if carry:
        @pl.when(c == 0)
        def _():
            for s in range(n_seg):
                h_scr[s] = _heads_to_block_diag(h0_ref.at[s])
                shift_scr[s] = shift0_ref[s]

    for s in range(n_seg):
        first = shift_scr[s] if carry else shift0_ref[s]
        prev = jnp.where(row == s * seg_rows, first, prev)
    if carry:
        for s in range(n_seg):
            shift_scr[s] = xb[(s + 1) * seg_rows - 1:(s + 1) * seg_rows, :]
    xs = xb + mu_ref[...] * (prev - xb)
    r = xs[:, 0:256]
    k = xs[:, 256:512]
    v = xs[:, 512:768]
    xw = xs[:, 768:832]
    xa = xs[:, 832:896]
    xg = xs[:, 896:1024]

    z = w0_ref[...] + _mm(jnp.tanh(xw), w2_ref[...], passes=3)
    lw = -DECAY_SCALE * _sigmoid(z)
    a_gate = _sigmoid(a0_ref[...] + _mm(xa, a2_ref[...], passes=3))
    gate = _mm(_sigmoid(xg), g2_ref[...], passes=1)

    seg_sum = _seg_matrix(MIX_W, 1.0)
    seg_mean = _seg_matrix(MIX_W, 1.0 / HEAD_DIM)
    kkn = k * kk_ref[...]
    norm = jnp.sqrt(_mm_exact_rhs(kkn * kkn, seg_sum))
    kk = kkn / jnp.maximum(norm, 1e-12)
    kf = k * (1.0 + (a_gate - 1.0) * ka_ref[...])

    tt = _iota2((rows, rows), 0)
    ss = _iota2((rows, rows), 1)
    tril = jnp.where(jnp.logical_and(ss <= tt, (ss >> 6) == (tt >> 6)), 1.0, 0.0).astype(BF16)
    lw_parts = _parts(lw, 3)
    cum = _dg(tril, lw_parts[0]) + (_dg(tril, lw_parts[1]) + _dg(tril, lw_parts[2]))
    w_inv = jnp.exp(-cum)
    rho = (r * jnp.exp(cum)).astype(BF16)
    alpha = (-kk * jnp.exp(cum - lw)).astype(BF16)
    beta = ((kk * a_gate) * w_inv).astype(BF16)
    kappa = (kf * w_inv).astype(BF16)
    v_bf = v.astype(BF16)

    hmask = masks_ref[MASK_HEAD]
    n4 = N_HEADS * CHUNK
    eye = jnp.where(_iota2((n4, n4), 0) == _iota2((n4, n4), 1), 1.0, 0.0)

    pre, a_bfs, t_invs = [], [], []
    for j in range(cb):
        sl = slice(j * CHUNK, (j + 1) * CHUNK)
        bd = lambda zz: _tile4(zz[sl]) * hmask
        al_bd, be_bd, ka_bd, rh_bd, v_bd = bd(alpha), bd(beta), bd(kappa), bd(rho), bd(v_bf)
        a_bf = _dg(al_bd, be_bd, NT).astype(BF16) * masks_ref[MASK_STRICT]
        a_ak = _dg(al_bd, ka_bd, NT).astype(BF16) * masks_ref[MASK_STRICT]
        b_rb = _dg(rh_bd, be_bd, NT).astype(BF16) * masks_ref[MASK_INCL]
        b_rk = _dg(rh_bd, ka_bd, NT).astype(BF16) * masks_ref[MASK_INCL]
        x0 = _dg(a_ak, v_bd)
        y0 = _dg(b_rk, v_bd)
        sn0 = _dg(v_bd, ka_bd, TN)
        w_chunk = jnp.exp(cum[(j + 1) * CHUNK - 1:(j + 1) * CHUNK, :])
        a_bfs.append(a_bf)
        t_invs.append(eye + (a_bf * masks_ref[MASK_LEVEL0]).astype(F32))
        pre.append([al_bd, be_bd, rh_bd, b_rb, None, x0, y0, sn0, w_chunk])
    for lvl in range(1, N_LEVELS):
        t_bfs = [t.astype(BF16) for t in t_invs]
        e_mats = [_dg(a_bfs[j] * masks_ref[MASK_LEVEL0 + lvl], t_bfs[j]) for j in range(cb)]
        t_invs = [t_invs[j] + _dg(t_bfs[j], e_mats[j].astype(BF16)) for j in range(cb)]
    for j in range(cb):
        pre[j][4] = t_invs[j].astype(BF16)

    h = [h_scr[s] if carry else _heads_to_block_diag(h0_ref.at[s]) for s in range(n_seg)]
    ys = [None] * cb
    for j in range(seg_chunks):
        for s in range(n_seg):
            q = s * seg_chunks + j
            al_bd, be_bd, rh_bd, b_rb, t_bf, x0, y0, sn0, w_chunk = pre[q]
            h0_bf = h[s].astype(BF16)
            x_mat = _dg(al_bd, h0_bf, NT) + x0
            u_bf = _dg(t_bf, x_mat.astype(BF16)).astype(BF16)
            y_bd = _dg(rh_bd, h0_bf, NT) + _dg(b_rb, u_bf) + y0
            h[s] = (h[s] + _dg(u_bf, be_bd, TN) + sn0) * w_chunk
            ys[q] = _fold4(y_bd)
    for s in range(n_seg):
        if carry:
            h_scr[s] = h[s]
        _block_diag_to_heads(h[s], hout_ref.at[s])

    y = _head_layer_norm(jnp.concatenate(ys, axis=0), seg_mean, lnw_ref[...], lnb_ref[...], B_GN_EPS)
    bonus = _mm_exact_rhs(r * kf * rk_ref[...], seg_sum) * v
    out = (y + bonus) * gate
    for s in range(n_seg):
        o_ref[s] = out[s * seg_rows:(s + 1) * seg_rows]


def _rwkv(hb, shift0, h0, params, *, independent, n_streams, t, base_row):
    masks = _rwkv_masks()
    full = lambda a: pl.BlockSpec(a.shape, lambda c: (0,) * a.ndim)
    if independent:
        assert n_streams == 1
        n_state, n_seg, seg_chunks = t // CHUNK, RWKV_CB, 1
        rows = n_seg * CHUNK
        nblk, base = t // rows, base_row // rows
        hb_specs = [pl.BlockSpec((rows, B_PROJ), lambda c: (base + c, 0))]
        st_idx = lambda c: c
    else:
        n_state, n_seg, seg_chunks = n_streams, n_streams, RWKV_CB
        rows = seg_chunks * CHUNK
        nblk, base = t // rows, base_row // rows
        stream_spec = lambda s: pl.BlockSpec((rows, B_PROJ), lambda c: (base + s * nblk + c, 0))
        hb_specs = [stream_spec(s) for s in range(n_streams)]
        st_idx = lambda c: 0
    seg_rows = seg_chunks * CHUNK
    state = pl.BlockSpec((n_seg, N_HEADS, HEAD_DIM, HEAD_DIM), lambda c: (st_idx(c), 0, 0, 0))
    out_idx = (lambda c: (c, 0, 0)) if independent else (lambda c: (0, c, 0))
    o, h_out = pl.pallas_call(
        functools.partial(_rwkv_kernel, n_hb=len(hb_specs), n_seg=n_seg, seg_chunks=seg_chunks,
                          carry=not independent),
        grid=(nblk,),
        in_specs=hb_specs + [pl.BlockSpec((n_seg, 1, B_PROJ), lambda c: (st_idx(c), 0, 0)), state,
                             full(masks)] + [full(p) for p in params],
        out_specs=[pl.BlockSpec((n_seg, seg_rows, MIX_W), out_idx), state],
        out_shape=[jax.ShapeDtypeStruct((n_state, CHUNK if independent else t, MIX_W), F32),
                   jax.ShapeDtypeStruct((n_state, N_HEADS, HEAD_DIM, HEAD_DIM), F32)],
        scratch_shapes=[pltpu.VMEM((n_seg, MIX_W, MIX_W), F32), pltpu.VMEM((n_seg, 1, B_PROJ), F32)],
        compiler_params=_cparams(("arbitrary",)),
        name="rwkv7",
    )(*([hb] * len(hb_specs)), shift0, h0, masks, *params)
    return o.reshape(-1, MIX_W), h_out


def _row(p):
    return p.reshape(1, -1).astype(F32)


def _mixers(proj, caches, lp, tabs, geom):
    aq, ak, av, hb, hc, dq, dk, dv = proj
    bp, tp, bs, ts = geom
    n_p = bp * tp
    ca_k, ca_v, sb_shift, sb_wkv, sc, cd_k, cd_v = caches
    pr = dict(n_streams=bp, t=tp, base_row=0)
    sm = dict(n_streams=bs, t=ts, base_row=n_p)
    zeros_state = jnp.zeros((bp, N_HEADS, HEAD_DIM, HEAD_DIM), F32)

    oa_p = _attention(aq, ak, av, None, lp["sink_col"], n_prev=A_PREV_CHUNKS, use_sink=True, **pr)
    oa_s = _attention(aq, ak, av, (ca_k.reshape(bs, -1, A_KV_W), ca_v.reshape(bs, -1, A_KV_W)), lp["sink_col"],
                      n_prev=A_PREV_CHUNKS, use_sink=True, **sm)

    ob_p, h_p = _rwkv(hb, jnp.zeros((bp, 1, B_PROJ), F32), zeros_state, lp["rwkv"], independent=False, **pr)
    ob_s, h_s = _rwkv(hb, sb_shift.reshape(bs, 1, B_PROJ), sb_wkv.astype(F32), lp["rwkv"], independent=True,
                      n_streams=1, t=bs * ts, base_row=n_p)

    oc_p, s_p = _retention(hc, *tabs["rope_prompt"], zeros_state, tabs["ret"], lp["c_ln_w"], lp["c_ln_b"], **pr)
    oc_s, s_s = _retention(hc, *tabs["rope_sample"], sc.astype(F32), tabs["ret"], lp["c_ln_w"], lp["c_ln_b"], **sm)

    od_p = _attention(dq, dk, dv, None, lp["bias_table"], n_prev=D_PREV_CHUNKS, use_sink=False, **pr)
    od_s = _attention(dq, dk, dv, (cd_k.reshape(bs, -1, MIX_W), cd_v.reshape(bs, -1, MIX_W)), lp["bias_table"],
                      n_prev=D_PREV_CHUNKS, use_sink=False, **sm)

    mix = ((oa_p, oa_s), (ob_p, ob_s), (oc_p, oc_s), (od_p, od_s))
    return mix, (h_p, s_p), (h_s, s_s)


def kernel(x_prompt, x_sample, cache_a_k, cache_a_v, state_b_shift, state_b_wkv, state_c, cache_d_k, cache_d_v,
           norm1_g, norm2_g, w_in, w_out, a_q_norm, a_k_norm, a_sinks, b_mu, b_w0, b_w2, b_a0, b_a2, b_g2,
           b_k_k, b_k_a, b_r_k, b_ln_w, b_ln_b, c_ln_w, c_ln_b, d_q_norm, d_k_norm, d_rel_bias,
           ffn_w1, ffn_w3, ffn_w2, moe_router, moe_w1, moe_w3, moe_w2):
    bp, tp, _ = x_prompt.shape
    bs, ts, _ = x_sample.shape
    assert ts == CHUNK
    n_p, n_s = bp * tp, bs * ts
    geom = (bp, tp, bs, ts)
    xa = x_prompt.reshape(n_p, D_MODEL)
    xb = x_sample.reshape(n_s, D_MODEL)

    tabs = {
        "ret": _retention_tables(),
        "rope_prompt": _rope_tables(np.arange(tp)),
        "rope_sample": _rope_tables(PAST_LEN + np.arange(ts)),
    }
    tile = lambda g: _row(jnp.tile(g, MIX_W // HEAD_DIM))

    p_states, s_states = [], []
    for l in range(DEPTH):
        lp = {
            "sink_col": jnp.repeat(a_sinks[l].astype(F32), CHUNK).reshape(N_HEADS * CHUNK, 1),
            "bias_table": _relbias_table(d_rel_bias[l].astype(F32), (D_PREV_CHUNKS + 1) * CHUNK),
            "rwkv": (_row(b_mu[l]), _row(b_w0[l]), b_w2[l], _row(b_a0[l]), b_a2[l], b_g2[l], _row(b_k_k[l]),
                     _row(b_k_a[l]), _row(b_r_k[l]), _row(b_ln_w[l]), _row(b_ln_b[l])),
            "c_ln_w": _row(c_ln_w[l]), "c_ln_b": _row(c_ln_b[l]),
        }
        proj = _inproj(xa, xb, n_p + n_s, _row(norm1_g[l]), w_in[l].astype(BF16), tile(a_q_norm[l]),
                       _row(jnp.tile(a_k_norm[l], A_KV_W // HEAD_DIM)), tile(d_q_norm[l]), tile(d_k_norm[l]))
        _, ak, av, hb, _, _, dk, dv = proj
        caches = (cache_a_k[l], cache_a_v[l], state_b_shift[l], state_b_wkv[l], state_c[l], cache_d_k[l], cache_d_v[l])
        mix, (wkv_p, ret_p), (wkv_s, ret_s) = _mixers(proj, caches, lp, tabs, geom)
        j = l // 2
        if l % 2 == 0:
            xa = xb = _outproj_ffn(xa, xb, n_p + n_s, mix, w_out[l].astype(BF16), _row(norm2_g[l]),
                                   ffn_w1[j].astype(BF16), ffn_w3[j].astype(BF16), ffn_w2[j].astype(BF16))
        else:
            x1, xn_a, xn_b, *routing = _outproj_route(xa, xb, n_p + n_s, mix, w_out[l].astype(BF16),
                                                      _row(norm2_g[l]), _router_lanes(moe_router[j]))
            groups = ((0, n_p), (n_p, n_s)) if l == DEPTH - 1 else ((0, n_p + n_s),)
            outs = _moe(xn_a, xn_b, x1, routing,
                        moe_w1[j].astype(BF16), moe_w3[j].astype(BF16), moe_w2[j].astype(BF16), groups)
            xa, xb = (outs[0], outs[-1])

        wa = min(A_PREV_CHUNKS * CHUNK, tp)
        wd = min(D_PREV_CHUNKS * CHUNK, tp)
        tail = lambda a, w, heads: jnp.stack(
            [a[(b + 1) * tp - w:(b + 1) * tp] for b in range(bp)]).reshape(bp, w, heads, HEAD_DIM)
        last_rows = lambda a, t, first, count: jnp.concatenate(
            [a[first + (s + 1) * t - 1:first + (s + 1) * t] for s in range(count)], axis=0)
        p_states.append((tail(ak, wa, 2), tail(av, wa, 2), last_rows(hb, tp, 0, bp), wkv_p, ret_p,
                         tail(dk, wd, N_HEADS), tail(dv, wd, N_HEADS)))
        new_rows = lambda a, heads: a[n_p:].reshape(bs, ts, heads, HEAD_DIM)
        s_states.append((new_rows(ak, 2), new_rows(av, 2), last_rows(hb, ts, n_p, bs), wkv_s, ret_s,
                         new_rows(dk, N_HEADS), new_rows(dv, N_HEADS)))

    if xa is xb:
        xa, xb = xa[:n_p], xa[n_p:]
    yp = xa.reshape(bp, tp, D_MODEL)
    ys = xb.reshape(bs, ts, D_MODEL)
    st = lambda group, i: jnp.stack([g[i] for g in group], axis=0)
    roll_in = lambda cache, i: jnp.concatenate([cache.astype(F32), st(s_states, i)], axis=2)[:, :, -cache.shape[2]:]
    return (yp, ys,
            st(p_states, 0), st(p_states, 1), st(p_states, 2), st(p_states, 3), st(p_states, 4), st(p_states, 5), st(p_states, 6),
            roll_in(cache_a_k, 0), roll_in(cache_a_v, 1), st(s_states, 2), st(s_states, 3), st(s_states, 4),
            roll_in(cache_d_k, 5), roll_in(cache_d_v, 6))
```

```python
import functools

import jax
import jax.numpy as jnp
import numpy as np
from jax import lax
from jax.experimental import pallas as pl
from jax.experimental.pallas import tpu as pltpu
from jax.experimental.pallas import tpu_sc as plsc

F32 = jnp.float32
BF16 = jnp.bfloat16

D_MODEL = 1024
DEPTH = 2
PAST_LEN = 4096
CHUNK = 64
HEAD_DIM = 64
N_HEADS = 4
MIX_W = N_HEADS * HEAD_DIM
A_KV_W = 128
A_PREV_CHUNKS = 2
D_PREV_CHUNKS = 8
D_REL_CLIP = 128
B_PROJ = 1024
C_PROJ = 1024
IN_PROJ = 3328
B_GN_EPS = 64e-5
C_GN_EPS = 1e-6
NORM_EPS = 1e-6
ATTN_SCALE = 0.125
ROPE_BASE = 10000.0
D_FF = 2816
N_EXPERTS = 8
E_FF = 3584
NEG_BIG = -1e30

VMEM_LIMIT = 56 * 1024 * 1024

NN = ((1,), (0,))
NT = ((1,), (1,))
TN = ((0,), (0,))


def _dg(a, b, dims=NN):
    return lax.dot_general(a, b, (dims, ((), ())), preferred_element_type=F32)


def _parts(x, n):
    out = []
    r = x
    for i in range(n):
        p = r.astype(BF16)
        out.append(p)
        if i + 1 < n:
            r = r - p.astype(F32)
    return out


def _mm(a, b, dims=NN, passes=1):
    if passes == 1:
        return _dg(a.astype(BF16), b.astype(BF16), dims)
    ah, al = _parts(a, 2)
    bh, bl = _parts(b, 2)
    return _dg(ah, bh, dims) + (_dg(ah, bl, dims) + _dg(al, bh, dims))


def _mm_exact_rhs(a, b_bf, dims=NN, n=2):
    acc = None
    for p in _parts(a, n):
        t = _dg(p, b_bf, dims)
        acc = t if acc is None else acc + t
    return acc


def _iota2(shape, dim):
    return lax.broadcasted_iota(jnp.int32, shape, dim)


def _head_mask(rows, cols=MIX_W):
    return (_iota2((rows, cols), 0) >> 6) == (_iota2((rows, cols), 1) >> 6)


def _seg_matrix(width, value):
    m = _head_mask(width, width)
    return jnp.where(m, value, 0.0).astype(BF16)


def _tile4(z):
    return jnp.concatenate([z, z, z, z], axis=0)


def _fold4(z):
    return (z[0:64] + z[64:128]) + (z[128:192] + z[192:256])


def _heads_to_block_diag(state_ref):
    rows = []
    for h in range(N_HEADS):
        pieces = [jnp.zeros((HEAD_DIM, HEAD_DIM), F32)] * N_HEADS
        pieces[h] = state_ref[h]
        rows.append(jnp.concatenate(pieces, axis=1))
    return jnp.concatenate(rows, axis=0)


def _block_diag_to_heads(m, state_ref):
    for h in range(N_HEADS):
        state_ref[h] = m[h * HEAD_DIM:(h + 1) * HEAD_DIM, h * HEAD_DIM:(h + 1) * HEAD_DIM]


def _sigmoid(x):
    return 1.0 / (1.0 + jnp.exp(-x))


def _cparams(sem):
    return pltpu.CompilerParams(dimension_semantics=sem, vmem_limit_bytes=VMEM_LIMIT)


IN_TM = 512


def _two_source_specs(xa, xb, n):
    na, nb = xa.shape[0] // IN_TM, xb.shape[0] // IN_TM
    spec_a = pl.BlockSpec((IN_TM, D_MODEL), lambda i: (jnp.minimum(i, na - 1), 0))
    spec_b = pl.BlockSpec((IN_TM, D_MODEL), lambda i: (jnp.clip(i - na, 0, nb - 1), 0))
    return na, n // IN_TM, spec_a, spec_b


def _inproj_kernel(xa_ref, xb_ref, g_ref, w_ref, aqg_ref, akg_ref, dqg_ref, dkg_ref,
                   aq_ref, ak_ref, av_ref, hb_ref, hc_ref, dq_ref, dk_ref, dv_ref, *, n_first):
    x = jnp.where(pl.program_id(0) < n_first, xa_ref[...], xb_ref[...])
    ms = jnp.mean(x * x, axis=-1, keepdims=True)
    xn = ((x * lax.rsqrt(ms + NORM_EPS)) * g_ref[...]).astype(BF16)
    seg = _seg_matrix(MIX_W, 1.0 / HEAD_DIM)

    def proj(lo, hi):
        return jnp.dot(xn, w_ref[:, lo:hi], preferred_element_type=F32)

    def head_rms(h, gain_ref):
        w = h.shape[-1]
        msq = _mm_exact_rhs(h * h, seg[:w, :w], n=1)
        return (h * lax.rsqrt(msq + NORM_EPS)) * gain_ref[...]

    aq_ref[...] = head_rms(proj(0, 256), aqg_ref)
    ak_ref[...] = head_rms(proj(256, 384), akg_ref)
    av_ref[...] = proj(384, 512)
    hb_ref[...] = proj(512, 1536)
    hc_ref[...] = proj(1536, 2560)
    dq_ref[...] = head_rms(proj(2560, 2816), dqg_ref)
    dk_ref[...] = head_rms(proj(2816, 3072), dkg_ref)
    dv_ref[...] = proj(3072, 3328)


def _inproj(xa, xb, n, g, w_bf, aqg, akg, dqg, dkg):
    na, nblk, spec_a, spec_b = _two_source_specs(xa, xb, n)
    widths = (256, 128, 128, B_PROJ, C_PROJ, 256, 256, 256)
    row = lambda w: pl.BlockSpec((IN_TM, w), lambda i: (i, 0))
    full = lambda a: pl.BlockSpec(a.shape, lambda i: (0,) * a.ndim)
    return pl.pallas_call(
        functools.partial(_inproj_kernel, n_first=na),
        grid=(nblk,),
        in_specs=[spec_a, spec_b, full(g), full(w_bf), full(aqg), full(akg), full(dqg), full(dkg)],
        out_specs=[row(w) for w in widths],
        out_shape=[jax.ShapeDtypeStruct((n, w), F32) for w in widths],
        compiler_params=_cparams(("parallel",)),
        name="inproj",
    )(xa, xb, g, w_bf, aqg, akg, dqg, dkg)


PACK_W = 256


def _pack_bf16_pairs(hi, lo):
    bits = lambda z: pltpu.bitcast(z.astype(BF16).astype(F32), jnp.int32)
    return bits(hi) | lax.shift_right_logical(bits(lo), jnp.full(lo.shape, 16, jnp.int32))


def _unpack_bf16_pairs(w):
    hi = pltpu.bitcast(w & jnp.int32(-65536), F32)
    lo = pltpu.bitcast(lax.shift_left(w, jnp.full(w.shape, 16, jnp.int32)), F32)
    return hi, lo


def _pack_rows(x):
    return (_pack_bf16_pairs(x[:, 0:PACK_W], x[:, PACK_W:2 * PACK_W]),
            _pack_bf16_pairs(x[:, 2 * PACK_W:3 * PACK_W], x[:, 3 * PACK_W:4 * PACK_W]))


def _unpack_rows(wa, wb):
    return jnp.concatenate(_unpack_bf16_pairs(wa) + _unpack_bf16_pairs(wb), axis=1)


ROUTER_LANES = 128


ROUTER_TERMS = 3


def _router_lanes(router):
    terms = _parts(router.astype(F32), ROUTER_TERMS)
    return jnp.pad(jnp.concatenate(terms, axis=1), ((0, 0), (0, ROUTER_LANES - ROUTER_TERMS * N_EXPERTS)))


def _route(xn_bf, router_bf, counts):
    split = _dg(xn_bf, router_bf)
    logits = split
    for k in range(1, ROUTER_TERMS):
        logits = logits + pltpu.roll(split, ROUTER_LANES - k * N_EXPERTS, 1)
    lane = _iota2(logits.shape, 1)
    logits = jnp.where(lane < N_EXPERTS, logits, NEG_BIG)
    m1 = jnp.max(logits, axis=-1, keepdims=True)
    i1 = jnp.min(jnp.where(logits == m1, lane, ROUTER_LANES), axis=-1, keepdims=True)
    rest = jnp.where(lane == i1, NEG_BIG, logits)
    m2 = jnp.max(rest, axis=-1, keepdims=True)
    i2 = jnp.min(jnp.where(rest == m2, lane, ROUTER_LANES), axis=-1, keepdims=True)
    e2 = jnp.exp(m2 - m1)
    den = 1.0 + e2
    gates = jnp.where(lane == i1, 1.0 / den, 0.0) + jnp.where(lane == i2, e2 / den, 0.0)
    sel = jnp.where(gates > 0.0, 1.0, 0.0)
    tm = sel.shape[0]
    before = jnp.where(_iota2((tm, tm), 1) < _iota2((tm, tm), 0), 1.0, 0.0).astype(BF16)
    rank = (_dg(before, sel.astype(BF16)) + counts).astype(jnp.int32)
    return gates, rank, counts + jnp.sum(sel, axis=0, keepdims=True)


def _mixed_residual(i, xa_ref, xb_ref, mix_refs, w_ref, g_ref, n_first_x, n_first_mix):
    x1 = jnp.where(i < n_first_x, xa_ref[...], xb_ref[...])
    for m in range(4):
        o = jnp.where(i < n_first_mix, mix_refs[2 * m][...], mix_refs[2 * m + 1][...])
        x1 = x1 + jnp.dot(o.astype(BF16), w_ref[m * MIX_W:(m + 1) * MIX_W, :], preferred_element_type=F32)
    ms = jnp.mean(x1 * x1, axis=-1, keepdims=True)
    return x1, (x1 * lax.rsqrt(ms + NORM_EPS)) * g_ref[...]


def _outproj_route_kernel(xa_ref, xb_ref, *refs, n_first_x, n_first_mix):
    mix_refs, (w_ref, g_ref, r_ref), outs = refs[:8], refs[8:11], refs[11:]
    x1_ref, pa_ref, pb_ref, gate_ref, rank_ref, cnt_ref, cnt_scr = outs
    i = pl.program_id(0)
    acc, xn = _mixed_residual(i, xa_ref, xb_ref, mix_refs, w_ref, g_ref, n_first_x, n_first_mix)

    @pl.when(i == 0)
    def _():
        cnt_scr[...] = jnp.zeros_like(cnt_scr)

    x1_ref[...] = acc
    pa_ref[...], pb_ref[...] = _pack_rows(xn)
    gates, rank, counts = _route(xn.astype(BF16), r_ref[...], cnt_scr[...])
    gate_ref[...] = gates
    rank_ref[...] = rank
    cnt_scr[...] = counts
    cnt_ref[...] = counts.astype(jnp.int32)


def _residual_specs(xa, xb, n, mix, index):
    na, nb = xa.shape[0] // IN_TM, xb.shape[0] // IN_TM
    nm_p, nm_s = mix[0][0].shape[0] // IN_TM, mix[0][1].shape[0] // IN_TM
    first = lambda w, cnt: pl.BlockSpec((IN_TM, w), index(lambda i: jnp.minimum(i, cnt - 1)))
    second = lambda w, skip, cnt: pl.BlockSpec((IN_TM, w), index(lambda i: jnp.clip(i - skip, 0, cnt - 1)))
    specs = [first(D_MODEL, na), second(D_MODEL, na, nb)] + [first(MIX_W, nm_p), second(MIX_W, nm_p, nm_s)] * 4
    return specs, [xa, xb, *[a for pair in mix for a in pair]], dict(n_first_x=na, n_first_mix=nm_p)


def _outproj_route(xa, xb, n, mix, w_bf, g2, router_bf):
    specs, operands, statics = _residual_specs(xa, xb, n, mix, lambda blk: (lambda i: (blk(i), 0)))
    row = lambda w: pl.BlockSpec((IN_TM, w), lambda i: (i, 0))
    full = lambda a: pl.BlockSpec(a.shape, lambda i: (0,) * a.ndim)
    return pl.pallas_call(
        functools.partial(_outproj_route_kernel, **statics),
        grid=(n // IN_TM,),
        in_specs=specs + [full(w_bf), full(g2), full(router_bf)],
        out_specs=[row(D_MODEL), row(PACK_W), row(PACK_W), row(ROUTER_LANES), row(ROUTER_LANES),
                   pl.BlockSpec((1, ROUTER_LANES), lambda i: (0, 0))],
        out_shape=[jax.ShapeDtypeStruct((n, D_MODEL), F32),
                   jax.ShapeDtypeStruct((n, PACK_W), jnp.int32), jax.ShapeDtypeStruct((n, PACK_W), jnp.int32),
                   jax.ShapeDtypeStruct((n, ROUTER_LANES), F32), jax.ShapeDtypeStruct((n, ROUTER_LANES), jnp.int32),
                   jax.ShapeDtypeStruct((1, ROUTER_LANES), jnp.int32)],
        scratch_shapes=[pltpu.VMEM((1, ROUTER_LANES), F32)],
        compiler_params=_cparams(("arbitrary",)),
        name="outproj_route",
    )(*operands, w_bf, g2, router_bf)


FFN_TF = 1408


def _outproj_ffn_kernel(xa_ref, xb_ref, *refs, n_first_x, n_first_mix):
    mix_refs, rest = refs[:8], refs[8:]
    w_ref, g_ref, w1_ref, w3_ref, w2_ref, o_ref, x1_scr, xn_scr = rest
    i, f = pl.program_id(0), pl.program_id(1)

    @pl.when(f == 0)
    def _():
        x1, xn = _mixed_residual(i, xa_ref, xb_ref, mix_refs, w_ref, g_ref, n_first_x, n_first_mix)
        x1_scr[...] = x1
        xn_scr[...] = xn.astype(BF16)

    xn = xn_scr[...]
    a = jnp.dot(xn, w1_ref[...], preferred_element_type=F32)
    b = jnp.dot(xn, w3_ref[...], preferred_element_type=F32)
    h = ((a * _sigmoid(a)) * b).astype(BF16)
    y = jnp.dot(h, w2_ref[...], preferred_element_type=F32)

    @pl.when(f == 0)
    def _():
        o_ref[...] = x1_scr[...] + y

    @pl.when(f != 0)
    def _():
        o_ref[...] += y


def _outproj_ffn(xa, xb, n, mix, w_bf, g2, w1_bf, w3_bf, w2_bf):
    specs, operands, statics = _residual_specs(xa, xb, n, mix, lambda blk: (lambda i, f: (blk(i), 0)))
    full = lambda a: pl.BlockSpec(a.shape, lambda i, f: (0,) * a.ndim)
    return pl.pallas_call(
        functools.partial(_outproj_ffn_kernel, **statics),
        grid=(n // IN_TM, D_FF // FFN_TF),
        in_specs=specs + [full(w_bf), full(g2),
                          pl.BlockSpec((D_MODEL, FFN_TF), lambda i, f: (0, f)),
                          pl.BlockSpec((D_MODEL, FFN_TF), lambda i, f: (0, f)),
                          pl.BlockSpec((FFN_TF, D_MODEL), lambda i, f: (f, 0))],
        out_specs=pl.BlockSpec((IN_TM, D_MODEL), lambda i, f: (i, 0)),
        out_shape=jax.ShapeDtypeStruct((n, D_MODEL), F32),
        scratch_shapes=[pltpu.VMEM((IN_TM, D_MODEL), F32), pltpu.VMEM((IN_TM, D_MODEL), BF16)],
        compiler_params=_cparams(("parallel", "arbitrary")),
        name="outproj_ffn",
    )(*operands, w_bf, g2, w1_bf, w3_bf, w2_bf)


MOE_R = 512
MOE_TF = 1792
SC_WINDOW = 128


def _moe_plan(gates, rank, counts, n):
    n_blocks = (2 * n) // MOE_R + N_EXPERTS + 1
    spare_row = (n_blocks - 1) * MOE_R
    sel = gates[:, :N_EXPERTS] > 0.0
    rank = rank[:, :N_EXPERTS]
    counts = counts[0, :N_EXPERTS]
    padded = ((counts + MOE_R - 1) // MOE_R) * MOE_R
    pad_end = jnp.cumsum(padded)
    pad_start = pad_end - padded
    pos = jnp.where(sel, pad_start[None, :] + rank, -1)
    order = jnp.cumsum(sel.astype(jnp.int32), axis=1)
    pick = lambda j: jnp.max(jnp.where(jnp.logical_and(sel, order == j), pos, -1), axis=1)
    to_row = lambda p: jnp.where(p >= 0, p, spare_row).astype(jnp.int32).reshape(1, n)
    block_expert = jnp.minimum(
        jnp.sum(pad_end[None, :] <= (jnp.arange(n_blocks) * MOE_R)[:, None], axis=1), N_EXPERTS - 1)
    return dict(n_blocks=n_blocks, pos0=to_row(pick(1)), pos1=to_row(pick(2)),
                block_expert=block_expert.astype(jnp.int32), n_used=(pad_end[-1:] // MOE_R).astype(jnp.int32))


def _sc_mesh():
    return plsc.VectorSubcoreMesh(core_axis_name="core", subcore_axis_name="subcore")


def _sc_scatter_rows(table, idx_lists, n_rows):
    n, cols = table.shape
    k = len(idx_lists)

    @functools.partial(pl.kernel, out_type=jax.ShapeDtypeStruct((n_rows, cols), table.dtype), mesh=_sc_mesh())
    def scatter(x_hbm, *rest):
        i_hbms, o_hbm = rest[:k], rest[k]

        def body(x_vmem, *i_vmems):
            for i_vmem in i_vmems:
                pltpu.sync_copy(x_vmem, o_hbm.at[i_vmem.at[0]])

        pltpu.emit_pipeline(
            body,
            grid=(n // SC_WINDOW,),
            in_specs=[pl.BlockSpec((SC_WINDOW, cols), lambda i: (i, 0))]
            + [pl.BlockSpec((1, SC_WINDOW), lambda i: (0, i))] * k,
            out_specs=[],
            core_axis_name=("core", "subcore"),
            dimension_semantics=(pltpu.PARALLEL,),
        )(x_hbm, *i_hbms)

    return scatter(table, *idx_lists)


def _sc_gather_rows(table, idx):
    n = idx.shape[1]
    cols = table.shape[1]

    @functools.partial(pl.kernel, out_type=jax.ShapeDtypeStruct((n, cols), table.dtype), mesh=_sc_mesh())
    def gather(x_hbm, i_hbm, o_hbm):
        def body(i_vmem, o_vmem):
            pltpu.sync_copy(x_hbm.at[i_vmem.at[0]], o_vmem)

        pltpu.emit_pipeline(
            body,
            grid=(n // SC_WINDOW,),
            in_specs=[pl.BlockSpec((1, SC_WINDOW), lambda i: (0, i))],
            out_specs=[pl.BlockSpec((SC_WINDOW, cols), lambda i: (i, 0))],
            core_axis_name=("core", "subcore"),
            dimension_semantics=(pltpu.PARALLEL,),
        )(i_hbm, o_hbm)

    return gather(table, idx)


def _moe_expert_kernel(be_ref, nu_ref, xa_ref, xb_ref, gs_ref, w1_ref, w3_ref, w2_ref, oa_ref, ob_ref, acc_ref):
    j, f = pl.program_id(0), pl.program_id(1)
    used = j < nu_ref[0]

    @pl.when(used)
    def _():
        x = _unpack_rows(xa_ref[...], xb_ref[...]).astype(BF16)
        a = jnp.dot(x, w1_ref[...], preferred_element_type=F32)
        b = jnp.dot(x, w3_ref[...], preferred_element_type=F32)
        h = ((a * _sigmoid(a)) * b).astype(BF16)
        y = jnp.dot(h, w2_ref[...], preferred_element_type=F32)

        @pl.when(f == 0)
        def _():
            acc_ref[...] = y

        @pl.when(f != 0)
        def _():
            acc_ref[...] += y

    @pl.when(f == pl.num_programs(1) - 1)
    def _():
        lane = _iota2(gs_ref.shape, 1)
        g = jnp.sum(jnp.where(lane == be_ref[j], gs_ref[...], 0.0), axis=1, keepdims=True)
        oa_ref[...], ob_ref[...] = _pack_rows(jnp.where(used, acc_ref[...] * g, 0.0))


def _moe_experts(plan, xs_a, xs_b, gs, w1_bf, w3_bf, w2_bf):
    n_blocks = plan["n_blocks"]
    half = pl.BlockSpec((MOE_R, PACK_W), lambda j, f, be, nu: (j, 0))
    grid_spec = pltpu.PrefetchScalarGridSpec(
        num_scalar_prefetch=2,
        grid=(n_blocks, E_FF // MOE_TF),
        in_specs=[half, half,
                  pl.BlockSpec((MOE_R, ROUTER_LANES), lambda j, f, be, nu: (j, 0)),
                  pl.BlockSpec((None, D_MODEL, MOE_TF), lambda j, f, be, nu: (be[j], 0, f)),
                  pl.BlockSpec((None, D_MODEL, MOE_TF), lambda j, f, be, nu: (be[j], 0, f)),
                  pl.BlockSpec((None, MOE_TF, D_MODEL), lambda j, f, be, nu: (be[j], f, 0))],
        out_specs=[half, half],
        scratch_shapes=[pltpu.VMEM((MOE_R, D_MODEL), F32)])
    return pl.pallas_call(
        _moe_expert_kernel,
        grid_spec=grid_spec,
        out_shape=[jax.ShapeDtypeStruct((n_blocks * MOE_R, PACK_W), jnp.int32)] * 2,
        compiler_params=_cparams(("arbitrary", "arbitrary")),
        name="moe_experts",
    )(plan["block_expert"], plan["n_used"], xs_a, xs_b, gs, w1_bf, w3_bf, w2_bf)


def _moe_combine_kernel(x1_ref, a0_ref, b0_ref, a1_ref, b1_ref, o_ref):
    o_ref[...] = (x1_ref[...] + _unpack_rows(a0_ref[...], b0_ref[...])) + _unpack_rows(a1_ref[...], b1_ref[...])


def _moe_combine(x1, picked, row0, n_rows):
    base = row0 // IN_TM
    row = lambda w: pl.BlockSpec((IN_TM, w), lambda i: (base + i, 0))
    return pl.pallas_call(
        _moe_combine_kernel,
        grid=(n_rows // IN_TM,),
        in_specs=[row(D_MODEL)] + [row(PACK_W)] * 4,
        out_specs=pl.BlockSpec((IN_TM, D_MODEL), lambda i: (i, 0)),
        out_shape=jax.ShapeDtypeStruct((n_rows, D_MODEL), F32),
        compiler_params=_cparams(("parallel",)),
        name="moe_combine",
    )(x1, *picked)


def _moe(xn_a, xn_b, x1, routing, w1_bf, w3_bf, w2_bf, row_groups):
    gates, rank, counts = routing
    plan = _moe_plan(gates, rank, counts, x1.shape[0])
    n_rows = plan["n_blocks"] * MOE_R
    idx = (plan["pos0"], plan["pos1"])
    xs_a = _sc_scatter_rows(xn_a, idx, n_rows)
    xs_b = _sc_scatter_rows(xn_b, idx, n_rows)
    gs = _sc_scatter_rows(gates, idx, n_rows)
    os_a, os_b = _moe_experts(plan, xs_a, xs_b, gs, w1_bf, w3_bf, w2_bf)
    picked = [_sc_gather_rows(t, p) for p in idx for t in (os_a, os_b)]
    return [_moe_combine(x1, picked, row0, rows) for row0, rows in row_groups]


def _relbias_kernel(rb_ref, o_ref, *, nk):
    h = pl.program_id(0)
    q = _iota2((CHUNK, nk), 0)
    r = _iota2((CHUNK, nk), 1)
    idx = jnp.clip(q - (r - (nk - CHUNK)), -D_REL_CLIP, D_REL_CLIP) + D_REL_CLIP

    def body(j, acc):
        return jnp.where(idx == j, rb_ref[h, j], acc)

    o_ref[...] = lax.fori_loop(0, 2 * D_REL_CLIP + 1, body, jnp.zeros((CHUNK, nk), F32))


def _relbias_table(rel_bias, nk):
    return pl.pallas_call(
        functools.partial(_relbias_kernel, nk=nk),
        grid=(N_HEADS,),
        in_specs=[pl.BlockSpec(memory_space=pltpu.SMEM)],
        out_specs=pl.BlockSpec((CHUNK, nk), lambda h: (h, 0)),
        out_shape=jax.ShapeDtypeStruct((N_HEADS * CHUNK, nk), F32),
        name="relbias",
    )(rel_bias)


ATTN_QB = 16
ATTN_GROUP = 8


def _attn_kernel(q_ref, kp_ref, kc_ref, vp_ref, vc_ref, x_ref, *rest, qb, n_grp, n_prev, use_sink, mask_first):
    o_ref, kbuf, vbuf = rest[-3:]
    i = pl.program_id(1)
    p_rows = kp_ref.shape[-2]
    nk = (n_prev + 1) * CHUNK
    wk = kp_ref.shape[-1]
    grp_rows = qb * CHUNK

    for g in range(n_grp):
        cur = pl.ds(g * grp_rows, grp_rows)
        kbuf[g, 0:p_rows, :] = (kp_ref[g] if len(kp_ref.shape) == 3 else kp_ref[...]).astype(BF16)
        kbuf[g, p_rows:, :] = kc_ref[cur, :].astype(BF16)
        vbuf[g, 0:p_rows, :] = (vp_ref[g] if len(vp_ref.shape) == 3 else vp_ref[...]).astype(BF16)
        vbuf[g, p_rows:, :] = vc_ref[cur, :].astype(BF16)

    hmask = _head_mask(N_HEADS * CHUNK)
    extra = x_ref[...]
    grouped = wk != MIX_W
    low = _iota2((CHUNK, A_KV_W), 1) < HEAD_DIM

    def stack_queries(qj):
        if not grouped:
            return jnp.where(hmask, _tile4(qj), 0.0)
        shifted = pltpu.roll(qj, MIX_W - HEAD_DIM, 1)[:, :A_KV_W]
        return jnp.concatenate([jnp.where(low, qj[:, :A_KV_W], 0.0), jnp.where(low, shifted, 0.0),
                                jnp.where(low, 0.0, shifted), jnp.where(low, 0.0, qj[:, A_KV_W:])], axis=0)

    def unstack_outputs(o_all):
        if not grouped:
            return _fold4(jnp.where(hmask, o_all, 0.0))
        b0, b1, b2, b3 = (o_all[h * CHUNK:(h + 1) * CHUNK] for h in range(N_HEADS))
        left = jnp.where(low, b0, 0.0) + pltpu.roll(jnp.where(low, b1, 0.0), HEAD_DIM, 1)
        right = pltpu.roll(jnp.where(low, 0.0, b2), HEAD_DIM, 1) + jnp.where(low, 0.0, b3)
        return jnp.concatenate([left, right], axis=1)

    def scores(g, j):
        base = p_rows + (j - n_prev) * CHUNK
        qs = stack_queries(q_ref[pl.ds(g * grp_rows + j * CHUNK, CHUNK), :] * ATTN_SCALE).astype(BF16)
        s = _dg(qs, kbuf[g, pl.ds(base, nk), :], NT)
        if not use_sink:
            s = s + extra
        if mask_first and base < p_rows:
            krow = base + _iota2(s.shape, 1)
            s = jnp.where(jnp.logical_and(i == 0, krow < p_rows), NEG_BIG, s)
        return s

    def weights(s):
        m = jnp.max(s, axis=-1, keepdims=True)
        if use_sink:
            m = jnp.maximum(m, extra)
        e = jnp.exp(s - m)
        den = jnp.sum(e, axis=-1, keepdims=True)
        if use_sink:
            den = den + jnp.exp(extra - m)
        return e.astype(BF16), 1.0 / den

    def output(g, j, e, inv_den):
        base = p_rows + (j - n_prev) * CHUNK
        o_all = _dg(e, vbuf[g, pl.ds(base, nk), :]) * inv_den
        o_ref[pl.ds(g * grp_rows + j * CHUNK, CHUNK), :] = unstack_outputs(o_all)

    chunks = [(g, j) for g in range(n_grp) for j in range(qb)]
    for c0 in range(0, len(chunks), ATTN_GROUP):
        group = chunks[c0:c0 + ATTN_GROUP]
        ss = [scores(g, j) for g, j in group]
        ws = [weights(s) for s in ss]
        for (g, j), (e, inv_den) in zip(group, ws):
            output(g, j, e, inv_den)


def _attention(q, k, v, prev, extra, *, n_prev, use_sink, n_streams, t, base_row):
    wk = k.shape[-1]
    if prev is None:
        qb = ATTN_QB
        rows = qb * CHUNK
        nblk = t // rows
        base = base_row // rows
        prev_spec = pl.BlockSpec((rows, wk), lambda s, i: (base + s * nblk + jnp.maximum(i - 1, 0), 0))
        k_prev, v_prev, p_rows, mask_first = k, v, rows, True
        n_grp, n_steps = 1, n_streams
    else:
        n_grp = ATTN_QB // (t // CHUNK)
        assert n_grp >= 1 and n_streams % n_grp == 0
        qb, rows, nblk, n_steps = t // CHUNK, n_grp * t, 1, n_streams // n_grp
        base = base_row // rows
        k_prev, v_prev = prev
        p_rows = k_prev.shape[1]
        prev_spec = pl.BlockSpec((n_grp, p_rows, wk), lambda s, i: (s, 0, 0))
        mask_first = False
    cur = lambda w: pl.BlockSpec((rows, w), lambda s, i: (base + s * nblk + i, 0))
    kern = functools.partial(_attn_kernel, qb=qb, n_grp=n_grp, n_prev=n_prev, use_sink=use_sink,
                             mask_first=mask_first)
    buf = pltpu.VMEM((n_grp, p_rows + rows // n_grp, wk), BF16)
    return pl.pallas_call(
        kern,
        grid=(n_steps, nblk),
        in_specs=[cur(MIX_W), prev_spec, cur(wk), prev_spec, cur(wk),
                  pl.BlockSpec(extra.shape, lambda s, i: (0, 0))],
        out_specs=pl.BlockSpec((rows, MIX_W), lambda s, i: (s * nblk + i, 0)),
        out_shape=jax.ShapeDtypeStruct((n_streams * t, MIX_W), F32),
        scratch_shapes=[buf, buf],
        compiler_params=_cparams(("parallel", "arbitrary")),
        name="attn_sink" if use_sink else "attn_bias",
    )(q, k_prev, k, v_prev, v, extra)


def _head_layer_norm(o, seg_mean_bf, w, b, eps):
    mu = _mm_exact_rhs(o, seg_mean_bf)
    d = o - mu
    var = _mm_exact_rhs(d * d, seg_mean_bf)
    return (d * lax.rsqrt(var + eps)) * w + b


def _ret_kernel(hc_ref, cos_ref, sin_ref, s0_ref, dstack_ref, qsc_ref, ksc_ref, gam_ref, lnw_ref, lnb_ref,
                *rest, qb):
    o_ref, sout_ref, s_scr = rest[-3:]
    i = pl.program_id(1)

    @pl.when(i == 0)
    def _():
        s_scr[...] = _heads_to_block_diag(s0_ref)

    hmask = _head_mask(N_HEADS * CHUNK)
    seg_mean = _seg_matrix(MIX_W, 1.0 / HEAD_DIM)
    rows = qb * CHUNK
    first_half = (_iota2((rows, MIX_W), 1) & (HEAD_DIM - 1)) < (HEAD_DIM // 2)
    cos = jnp.concatenate([cos_ref[...]] * (MIX_W // ROPE_W), axis=1)
    sin = jnp.concatenate([sin_ref[...]] * (MIX_W // ROPE_W), axis=1)

    def rope(x):
        partner = jnp.where(first_half, pltpu.roll(x, MIX_W - HEAD_DIM // 2, 1), pltpu.roll(x, HEAD_DIM // 2, 1))
        return x * cos + partner * sin

    q = rope(hc_ref[:, 0:256])
    k = rope(hc_ref[:, 256:512]) * ATTN_SCALE
    v_bf = hc_ref[:, 512:768].astype(BF16)
    state = s_scr[...]
    outs = []
    for j in range(qb):
        sl = slice(j * CHUNK, (j + 1) * CHUNK)
        qj, kj, vj = q[sl], k[sl], v_bf[sl]
        qs = jnp.where(hmask, _tile4(qj), 0.0).astype(BF16)
        sc = _dg(qs, kj.astype(BF16), NT) * dstack_ref[...]
        intra = _fold4(jnp.where(hmask, _dg(sc.astype(BF16), vj), 0.0))
        inter = _dg((qj * qsc_ref[...]).astype(BF16), state.astype(BF16))
        kv = _dg((kj * ksc_ref[...]).astype(BF16), vj, TN)
        state = gam_ref[...] * state + jnp.where(hmask, kv, 0.0)
        outs.append(intra + inter)
    s_scr[...] = state
    _block_diag_to_heads(state, sout_ref)
    y = _head_layer_norm(jnp.concatenate(outs, axis=0), seg_mean, lnw_ref[...], lnb_ref[...], C_GN_EPS)
    g = hc_ref[:, 768:1024]
    o_ref[...] = y * (g * _sigmoid(g))


def _retention(hc, cos, sin, s0, tabs, lnw, lnb, *, n_streams, t, base_row):
    qb = min(8, t // CHUNK)
    rows = qb * CHUNK
    nblk = t // rows
    base = base_row // rows
    dstack, qsc, ksc, gam = tabs
    full = lambda a: pl.BlockSpec(a.shape, lambda s, i: (0,) * a.ndim)
    cur = lambda w: pl.BlockSpec((rows, w), lambda s, i: (base + s * nblk + i, 0))
    state = pl.BlockSpec((None, N_HEADS, HEAD_DIM, HEAD_DIM), lambda s, i: (s, 0, 0, 0))
    return pl.pallas_call(
        functools.partial(_ret_kernel, qb=qb),
        grid=(n_streams, nblk),
        in_specs=[cur(C_PROJ),
                  pl.BlockSpec((rows, ROPE_W), lambda s, i: (i, 0)),
                  pl.BlockSpec((rows, ROPE_W), lambda s, i: (i, 0)),
                  state, full(dstack), full(qsc), full(ksc), full(gam), full(lnw), full(lnb)],
        out_specs=[pl.BlockSpec((rows, MIX_W), lambda s, i: (s * nblk + i, 0)), state],
        out_shape=[jax.ShapeDtypeStruct((n_streams * t, MIX_W), F32),
                   jax.ShapeDtypeStruct((n_streams, N_HEADS, HEAD_DIM, HEAD_DIM), F32)],
        scratch_shapes=[pltpu.VMEM((MIX_W, MIX_W), F32)],
        compiler_params=_cparams(("parallel", "arbitrary")),
        name="retention",
    )(hc, cos, sin, s0, dstack, qsc, ksc, gam, lnw, lnb)


def _retention_tables():
    gamma = 1.0 - 2.0 ** (-5.0 - np.arange(N_HEADS, dtype=np.float64))
    t = np.arange(CHUNK)
    diff = t[:, None] - t[None, :]
    dmat = np.where(diff >= 0, gamma[:, None, None] ** np.maximum(diff, 0), 0.0)
    dstack = dmat.reshape(N_HEADS * CHUNK, CHUNK)
    lanes = lambda per_head: np.repeat(per_head, HEAD_DIM, axis=-1)
    qsc = lanes(gamma[None, :] ** (t + 1)[:, None])
    ksc = lanes(gamma[None, :] ** (CHUNK - 1 - t)[:, None])
    gam = np.broadcast_to(lanes(gamma ** CHUNK)[:, None], (MIX_W, MIX_W))
    return tuple(jnp.asarray(a, F32) for a in (dstack, qsc, ksc, gam))


ROPE_W = 2 * HEAD_DIM


def _rope_tables(pos):
    half = HEAD_DIM // 2
    theta = np.float32(1.0) / (np.float32(ROPE_BASE) ** np.linspace(0.0, 1.0, half, dtype=np.float32))
    ang = np.asarray(pos, np.float32)[:, None] * theta[None, :]
    cos, sin = np.cos(ang), np.sin(ang)
    reps = ROPE_W // HEAD_DIM
    cos_t = np.tile(np.concatenate([cos, cos], axis=-1), (1, reps))
    sin_t = np.tile(np.concatenate([-sin, sin], axis=-1), (1, reps))
    return jnp.asarray(cos_t, F32), jnp.asarray(sin_t, F32)


DECAY_SCALE = 0.6065306597126334
RWKV_CB = 4
N_LEVELS = 6
MASK_HEAD, MASK_STRICT, MASK_INCL, MASK_LEVEL0 = 0, 1, 2, 3


def _rwkv_masks():
    n4 = N_HEADS * CHUNK
    ri = np.arange(n4)[:, None]
    ci = np.arange(n4)[None, :]
    head = (ri >> 6) == (ci >> 6)
    tabs = [head, head & ((ci & 63) < (ri & 63)), head & ((ci & 63) <= (ri & 63))]
    for log_m in range(N_LEVELS):
        same = (ri >> (log_m + 1)) == (ci >> (log_m + 1))
        tabs.append(same & (((ri >> log_m) & 1) == 1) & (((ci >> log_m) & 1) == 0))
    return jnp.asarray(np.stack(tabs), BF16)


def _rwkv_kernel(*refs, n_hb, n_seg, seg_chunks, carry):
    hb_refs, rest = refs[:n_hb], refs[n_hb:]
    (shift0_ref, h0_ref, masks_ref, mu_ref, w0_ref, w2_ref, a0_ref, a2_ref, g2_ref,
     kk_ref, ka_ref, rk_ref, lnw_ref, lnb_ref, o_ref, hout_ref, h_scr, shift_scr) = rest
    c = pl.program_id(0)
    cb = n_seg * seg_chunks
    seg_rows = seg_chunks * CHUNK
    rows = cb * CHUNK
    xb = jnp.concatenate([ref[...] for ref in hb_refs], axis=0)
    row = _iota2(xb.shape, 0)
    prev = pltpu.roll(xb, 1, 0)
    if carry:
        @pl.when(c == 0)
        def _():
            for s in range(n_seg):
                h_scr[s] = _heads_to_block_diag(h0_ref.at[s])
                shift_scr[s] = shift0_ref[s]

    for s in range(n_seg):
        first = shift_scr[s] if carry else shift0_ref[s]
        prev = jnp.where(row == s * seg_rows, first, prev)
    if carry:
        for s in range(n_seg):
            shift_scr[s] = xb[(s + 1) * seg_rows - 1:(s + 1) * seg_rows, :]
    xs = xb + mu_ref[...] * (prev - xb)
    r = xs[:, 0:256]
    k = xs[:, 256:512]
    v = xs[:, 512:768]
    xw = xs[:, 768:832]
    xa = xs[:, 832:896]
    xg = xs[:, 896:1024]

    z = w0_ref[...] + _mm(jnp.tanh(xw), w2_ref[...], passes=3)
    lw = -DECAY_SCALE * _sigmoid(z)
    a_gate = _sigmoid(a0_ref[...] + _mm(xa, a2_ref[...], passes=3))
    gate = _mm(_sigmoid(xg), g2_ref[...], passes=1)

    seg_sum = _seg_matrix(MIX_W, 1.0)
    seg_mean = _seg_matrix(MIX_W, 1.0 / HEAD_DIM)
    kkn = k * kk_ref[...]
    norm = jnp.sqrt(_mm_exact_rhs(kkn * kkn, seg_sum))
    kk = kkn / jnp.maximum(norm, 1e-12)
    kf = k * (1.0 + (a_gate - 1.0) * ka_ref[...])

    tt = _iota2((rows, rows), 0)
    ss = _iota2((rows, rows), 1)
    tril = jnp.where(jnp.logical_and(ss <= tt, (ss >> 6) == (tt >> 6)), 1.0, 0.0).astype(BF16)
    lw_parts = _parts(lw, 3)
    cum = _dg(tril, lw_parts[0]) + (_dg(tril, lw_parts[1]) + _dg(tril, lw_parts[2]))
    w_inv = jnp.exp(-cum)
    rho = (r * jnp.exp(cum)).astype(BF16)
    alpha = (-kk * jnp.exp(cum - lw)).astype(BF16)
    beta = ((kk * a_gate) * w_inv).astype(BF16)
    kappa = (kf * w_inv).astype(BF16)
    v_bf = v.astype(BF16)

    hmask = masks_ref[MASK_HEAD]
    n4 = N_HEADS * CHUNK
    eye = jnp.where(_iota2((n4, n4), 0) == _iota2((n4, n4), 1), 1.0, 0.0)

    pre, a_bfs, t_invs = [], [], []
    for j in range(cb):
        sl = slice(j * CHUNK, (j + 1) * CHUNK)
        bd = lambda zz: _tile4(zz[sl]) * hmask
        al_bd, be_bd, ka_bd, rh_bd, v_bd = bd(alpha), bd(beta), bd(kappa), bd(rho), bd(v_bf)
        a_bf = _dg(al_bd, be_bd, NT).astype(BF16) * masks_ref[MASK_STRICT]
        a_ak = _dg(al_bd, ka_bd, NT).astype(BF16) * masks_ref[MASK_STRICT]
        b_rb = _dg(rh_bd, be_bd, NT).astype(BF16) * masks_ref[MASK_INCL]
        b_rk = _dg(rh_bd, ka_bd, NT).astype(BF16) * masks_ref[MASK_INCL]
        x0 = _dg(a_ak, v_bd)
        y0 = _dg(b_rk, v_bd)
        sn0 = _dg(v_bd, ka_bd, TN)
        w_chunk = jnp.exp(cum[(j + 1) * CHUNK - 1:(j + 1) * CHUNK, :])
        a_bfs.append(a_bf)
        t_invs.append(eye + (a_bf * masks_ref[MASK_LEVEL0]).astype(F32))
        pre.append([al_bd, be_bd, rh_bd, b_rb, None, x0, y0, sn0, w_chunk])
    for lvl in range(1, N_LEVELS):
        t_bfs = [t.astype(BF16) for t in t_invs]
        e_mats = [_dg(a_bfs[j] * masks_ref[MASK_LEVEL0 + lvl], t_bfs[j]) for j in range(cb)]
        t_invs = [t_invs[j] + _dg(t_bfs[j], e_mats[j].astype(BF16)) for j in range(cb)]
    for j in range(cb):
        pre[j][4] = t_invs[j].astype(BF16)

    h = [h_scr[s] if carry else _heads_to_block_diag(h0_ref.at[s]) for s in range(n_seg)]
    ys = [None] * cb
    for j in range(seg_chunks):
        for s in range(n_seg):
            q = s * seg_chunks + j
            al_bd, be_bd, rh_bd, b_rb, t_bf, x0, y0, sn0, w_chunk = pre[q]
            h0_bf = h[s].astype(BF16)
            x_mat = _dg(al_bd, h0_bf, NT) + x0
            u_bf = _dg(t_bf, x_mat.astype(BF16)).astype(BF16)
            y_bd = _dg(rh_bd, h0_bf, NT) + _dg(b_rb, u_bf) + y0
            h[s] = (h[s] + _dg(u_bf, be_bd, TN) + sn0) * w_chunk
            ys[q] = _fold4(y_bd)
    for s in range(n_seg):
        if carry:
            h_scr[s] = h[s]
        _block_diag_to_heads(h[s], hout_ref.at[s])

    y = _head_layer_norm(jnp.concatenate(ys, axis=0), seg_mean, lnw_ref[...], lnb_ref[...], B_GN_EPS)
    bonus = _mm_exact_rhs(r * kf * rk_ref[...], seg_sum) * v
    out = (y + bonus) * gate
    for s in range(n_seg):
        o_ref[s] = out[s * seg_rows:(s + 1) * seg_rows]


def _rwkv(hb, shift0, h0, params, *, independent, n_streams, t, base_row):
    masks = _rwkv_masks()
    full = lambda a: pl.BlockSpec(a.shape, lambda c: (0,) * a.ndim)
    if independent:
        assert n_streams == 1
        n_state, n_seg, seg_chunks = t // CHUNK, RWKV_CB, 1
        rows = n_seg * CHUNK
        nblk, base = t // rows, base_row // rows
        hb_specs = [pl.BlockSpec((rows, B_PROJ), lambda c: (base + c, 0))]
        st_idx = lambda c: c
    else:
        n_state, n_seg, seg_chunks = n_streams, n_streams, RWKV_CB
        rows = seg_chunks * CHUNK
        nblk, base = t // rows, base_row // rows
        stream_spec = lambda s: pl.BlockSpec((rows, B_PROJ), lambda c: (base + s * nblk + c, 0))
        hb_specs = [stream_spec(s) for s in range(n_streams)]
        st_idx = lambda c: 0
    seg_rows = seg_chunks * CHUNK
    state = pl.BlockSpec((n_seg, N_HEADS, HEAD_DIM, HEAD_DIM), lambda c: (st_idx(c), 0, 0, 0))
    out_idx = (lambda c: (c, 0, 0)) if independent else (lambda c: (0, c, 0))
    o, h_out = pl.pallas_call(
        functools.partial(_rwkv_kernel, n_hb=len(hb_specs), n_seg=n_seg, seg_chunks=seg_chunks,
                          carry=not independent),
        grid=(nblk,),
        in_specs=hb_specs + [pl.BlockSpec((n_seg, 1, B_PROJ), lambda c: (st_idx(c), 0, 0)), state,
                             full(masks)] + [full(p) for p in params],
        out_specs=[pl.BlockSpec((n_seg, seg_rows, MIX_W), out_idx), state],
        out_shape=[jax.ShapeDtypeStruct((n_state, CHUNK if independent else t, MIX_W), F32),
                   jax.ShapeDtypeStruct((n_state, N_HEADS, HEAD_DIM, HEAD_DIM), F32)],
        scratch_shapes=[pltpu.VMEM((n_seg, MIX_W, MIX_W), F32), pltpu.VMEM((n_seg, 1, B_PROJ), F32)],
        compiler_params=_cparams(("arbitrary",)),
        name="rwkv7",
    )(*([hb] * len(hb_specs)), shift0, h0, masks, *params)
    return o.reshape(-1, MIX_W), h_out


def _row(p):
    return p.reshape(1, -1).astype(F32)


def _mixers(proj, caches, lp, tabs, geom):
    aq, ak, av, hb, hc, dq, dk, dv = proj
    bp, tp, bs, ts = geom
    n_p = bp * tp
    ca_k, ca_v, sb_shift, sb_wkv, sc, cd_k, cd_v = caches
    pr = dict(n_streams=bp, t=tp, base_row=0)
    sm = dict(n_streams=bs, t=ts, base_row=n_p)
    zeros_state = jnp.zeros((bp, N_HEADS, HEAD_DIM, HEAD_DIM), F32)

    oa_p = _attention(aq, ak, av, None, lp["sink_col"], n_prev=A_PREV_CHUNKS, use_sink=True, **pr)
    oa_s = _attention(aq, ak, av, (ca_k.reshape(bs, -1, A_KV_W), ca_v.reshape(bs, -1, A_KV_W)), lp["sink_col"],
                      n_prev=A_PREV_CHUNKS, use_sink=True, **sm)

    ob_p, h_p = _rwkv(hb, jnp.zeros((bp, 1, B_PROJ), F32), zeros_state, lp["rwkv"], independent=False, **pr)
    ob_s, h_s = _rwkv(hb, sb_shift.reshape(bs, 1, B_PROJ), sb_wkv.astype(F32), lp["rwkv"], independent=True,
                      n_streams=1, t=bs * ts, base_row=n_p)

    oc_p, s_p = _retention(hc, *tabs["rope_prompt"], zeros_state, tabs["ret"], lp["c_ln_w"], lp["c_ln_b"], **pr)
    oc_s, s_s = _retention(hc, *tabs["rope_sample"], sc.astype(F32), tabs["ret"], lp["c_ln_w"], lp["c_ln_b"], **sm)

    od_p = _attention(dq, dk, dv, None, lp["bias_table"], n_prev=D_PREV_CHUNKS, use_sink=False, **pr)
    od_s = _attention(dq, dk, dv, (cd_k.reshape(bs, -1, MIX_W), cd_v.reshape(bs, -1, MIX_W)), lp["bias_table"],
                      n_prev=D_PREV_CHUNKS, use_sink=False, **sm)

    mix = ((oa_p, oa_s), (ob_p, ob_s), (oc_p, oc_s), (od_p, od_s))
    return mix, (h_p, s_p), (h_s, s_s)


def kernel(x_prompt, x_sample, cache_a_k, cache_a_v, state_b_shift, state_b_wkv, state_c, cache_d_k, cache_d_v,
           norm1_g, norm2_g, w_in, w_out, a_q_norm, a_k_norm, a_sinks, b_mu, b_w0, b_w2, b_a0, b_a2, b_g2,
           b_k_k, b_k_a, b_r_k, b_ln_w, b_ln_b, c_ln_w, c_ln_b, d_q_norm, d_k_norm, d_rel_bias,
           ffn_w1, ffn_w3, ffn_w2, moe_router, moe_w1, moe_w3, moe_w2):
    bp, tp, _ = x_prompt.shape
    bs, ts, _ = x_sample.shape
    assert ts == CHUNK
    n_p, n_s = bp * tp, bs * ts
    geom = (bp, tp, bs, ts)
    xa = x_prompt.reshape(n_p, D_MODEL)
    xb = x_sample.reshape(n_s, D_MODEL)

    tabs = {
        "ret": _retention_tables(),
        "rope_prompt": _rope_tables(np.arange(tp)),
        "rope_sample": _rope_tables(PAST_LEN + np.arange(ts)),
    }
    tile = lambda g: _row(jnp.tile(g, MIX_W // HEAD_DIM))

    p_states, s_states = [], []
    for l in range(DEPTH):
        lp = {
            "sink_col": jnp.repeat(a_sinks[l].astype(F32), CHUNK).reshape(N_HEADS * CHUNK, 1),
            "bias_table": _relbias_table(d_rel_bias[l].astype(F32), (D_PREV_CHUNKS + 1) * CHUNK),
            "rwkv": (_row(b_mu[l]), _row(b_w0[l]), b_w2[l], _row(b_a0[l]), b_a2[l], b_g2[l], _row(b_k_k[l]),
                     _row(b_k_a[l]), _row(b_r_k[l]), _row(b_ln_w[l]), _row(b_ln_b[l])),
            "c_ln_w": _row(c_ln_w[l]), "c_ln_b": _row(c_ln_b[l]),
        }
        proj = _inproj(xa, xb, n_p + n_s, _row(norm1_g[l]), w_in[l].astype(BF16), tile(a_q_norm[l]),
                       _row(jnp.tile(a_k_norm[l], A_KV_W // HEAD_DIM)), tile(d_q_norm[l]), tile(d_k_norm[l]))
        _, ak, av, hb, _, _, dk, dv = proj
        caches = (cache_a_k[l], cache_a_v[l], state_b_shift[l], state_b_wkv[l], state_c[l], cache_d_k[l], cache_d_v[l])
        mix, (wkv_p, ret_p), (wkv_s, ret_s) = _mixers(proj, caches, lp, tabs, geom)
        j = l // 2
        if l % 2 == 0:
            xa = xb = _outproj_ffn(xa, xb, n_p + n_s, mix, w_out[l].astype(BF16), _row(norm2_g[l]),
                                   ffn_w1[j].astype(BF16), ffn_w3[j].astype(BF16), ffn_w2[j].astype(BF16))
        else:
            x1, xn_a, xn_b, *routing = _outproj_route(xa, xb, n_p + n_s, mix, w_out[l].astype(BF16),
                                                      _row(norm2_g[l]), _router_lanes(moe_router[j]))
            groups = ((0, n_p), (n_p, n_s)) if l == DEPTH - 1 else ((0, n_p + n_s),)
            outs = _moe(xn_a, xn_b, x1, routing,
                        moe_w1[j].astype(BF16), moe_w3[j].astype(BF16), moe_w2[j].astype(BF16), groups)
            xa, xb = (outs[0], outs[-1])

        wa = min(A_PREV_CHUNKS * CHUNK, tp)
        wd = min(D_PREV_CHUNKS * CHUNK, tp)
        tail = lambda a, w, heads: jnp.stack(
            [a[(b + 1) * tp - w:(b + 1) * tp] for b in range(bp)]).reshape(bp, w, heads, HEAD_DIM)
        last_rows = lambda a, t, first, count: jnp.concatenate(
            [a[first + (s + 1) * t - 1:first + (s + 1) * t] for s in range(count)], axis=0)
        p_states.append((tail(ak, wa, 2), tail(av, wa, 2), last_rows(hb, tp, 0, bp), wkv_p, ret_p,
                         tail(dk, wd, N_HEADS), tail(dv, wd, N_HEADS)))
        new_rows = lambda a, heads: a[n_p:].reshape(bs, ts, heads, HEAD_DIM)
        s_states.append((new_rows(ak, 2), new_rows(av, 2), last_rows(hb, ts, n_p, bs), wkv_s, ret_s,
                         new_rows(dk, N_HEADS), new_rows(dv, N_HEADS)))

    if xa is xb:
        xa, xb = xa[:n_p], xa[n_p:]
    yp = xa.reshape(bp, tp, D_MODEL)
    ys = xb.reshape(bs, ts, D_MODEL)
    st = lambda group, i: jnp.stack([g[i] for g in group], axis=0)
    roll_in = lambda cache, i: jnp.concatenate([cache.astype(F32), st(s_states, i)], axis=2)[:, :, -cache.shape[2]:]
    return (yp, ys,
            st(p_states, 0), st(p_states, 1), st(p_states, 2), st(p_states, 3), st(p_states, 4), st(p_states, 5), st(p_states, 6),
            roll_in(cache_a_k, 0), roll_in(cache_a_v, 1), st(s_states, 2), st(s_states, 3), st(s_states, 4),
            roll_in(cache_d_k, 5), roll_in(cache_d_v, 6))
```

```python
import functools

import jax
import jax.numpy as jnp
import numpy as np
from jax import lax
from jax.experimental import pallas as pl
from jax.experimental.pallas import tpu as pltpu
from jax.experimental.pallas import tpu_sc as plsc

F32 = jnp.float32
BF16 = jnp.bfloat16

D_MODEL = 1024
DEPTH = 2
PAST_LEN = 4096
CHUNK = 64
HEAD_DIM = 64
N_HEADS = 4
MIX_W = N_HEADS * HEAD_DIM
A_KV_W = 128
A_PREV_CHUNKS = 2
D_PREV_CHUNKS = 8
D_REL_CLIP = 128
B_PROJ = 1024
C_PROJ = 1024
IN_PROJ = 3328
B_GN_EPS = 64e-5
C_GN_EPS = 1e-6
NORM_EPS = 1e-6
ATTN_SCALE = 0.125
ROPE_BASE = 10000.0
D_FF = 2816
N_EXPERTS = 8
E_FF = 3584
NEG_BIG = -1e30

VMEM_LIMIT = 56 * 1024 * 1024

NN = ((1,), (0,))
NT = ((1,), (1,))
TN = ((0,), (0,))


def _dg(a, b, dims=NN):
    return lax.dot_general(a, b, (dims, ((), ())), preferred_element_type=F32)


def _parts(x, n):
    out = []
    r = x
    for i in range(n):
        p = r.astype(BF16)
        out.append(p)
        if i + 1 < n:
            r = r - p.astype(F32)
    return out


def _mm(a, b, dims=NN, passes=1):
    if passes == 1:
        return _dg(a.astype(BF16), b.astype(BF16), dims)
    ah, al = _parts(a, 2)
    bh, bl = _parts(b, 2)
    return _dg(ah, bh, dims) + (_dg(ah, bl, dims) + _dg(al, bh, dims))


def _mm_exact_rhs(a, b_bf, dims=NN, n=2):
    acc = None
    for p in _parts(a, n):
        t = _dg(p, b_bf, dims)
        acc = t if acc is None else acc + t
    return acc


def _iota2(shape, dim):
    return lax.broadcasted_iota(jnp.int32, shape, dim)


def _head_mask(rows, cols=MIX_W):
    return (_iota2((rows, cols), 0) >> 6) == (_iota2((rows, cols), 1) >> 6)


def _seg_matrix(width, value):
    m = _head_mask(width, width)
    return jnp.where(m, value, 0.0).astype(BF16)


def _tile4(z):
    return jnp.concatenate([z, z, z, z], axis=0)


def _fold4(z):
    return (z[0:64] + z[64:128]) + (z[128:192] + z[192:256])


def _heads_to_block_diag(state_ref):
    rows = []
    for h in range(N_HEADS):
        pieces = [jnp.zeros((HEAD_DIM, HEAD_DIM), F32)] * N_HEADS
        pieces[h] = state_ref[h]
        rows.append(jnp.concatenate(pieces, axis=1))
    return jnp.concatenate(rows, axis=0)


def _block_diag_to_heads(m, state_ref):
    for h in range(N_HEADS):
        state_ref[h] = m[h * HEAD_DIM:(h + 1) * HEAD_DIM, h * HEAD_DIM:(h + 1) * HEAD_DIM]


def _sigmoid(x):
    return 1.0 / (1.0 + jnp.exp(-x))


def _cparams(sem):
    return pltpu.CompilerParams(dimension_semantics=sem, vmem_limit_bytes=VMEM_LIMIT)


IN_TM = 512


def _two_source_specs(xa, xb, n):
    na, nb = xa.shape[0] // IN_TM, xb.shape[0] // IN_TM
    spec_a = pl.BlockSpec((IN_TM, D_MODEL), lambda i: (jnp.minimum(i, na - 1), 0))
    spec_b = pl.BlockSpec((IN_TM, D_MODEL), lambda i: (jnp.clip(i - na, 0, nb - 1), 0))
    return na, n // IN_TM, spec_a, spec_b


def _inproj_kernel(xa_ref, xb_ref, g_ref, w_ref, aqg_ref, akg_ref, dqg_ref, dkg_ref,
                   aq_ref, ak_ref, av_ref, hb_ref, hc_ref, dq_ref, dk_ref, dv_ref, *, n_first):
    x = jnp.where(pl.program_id(0) < n_first, xa_ref[...], xb_ref[...])
    ms = jnp.mean(x * x, axis=-1, keepdims=True)
    xn = ((x * lax.rsqrt(ms + NORM_EPS)) * g_ref[...]).astype(BF16)
    seg = _seg_matrix(MIX_W, 1.0 / HEAD_DIM)

    def proj(lo, hi):
        return jnp.dot(xn, w_ref[:, lo:hi], preferred_element_type=F32)

    def head_rms(h, gain_ref):
        w = h.shape[-1]
        msq = _mm_exact_rhs(h * h, seg[:w, :w], n=1)
        return (h * lax.rsqrt(msq + NORM_EPS)) * gain_ref[...]

    aq_ref[...] = head_rms(proj(0, 256), aqg_ref)
    ak_ref[...] = head_rms(proj(256, 384), akg_ref)
    av_ref[...] = proj(384, 512)
    hb_ref[...] = proj(512, 1536)
    hc_ref[...] = proj(1536, 2560)
    dq_ref[...] = head_rms(proj(2560, 2816), dqg_ref)
    dk_ref[...] = head_rms(proj(2816, 3072), dkg_ref)
    dv_ref[...] = proj(3072, 3328)


def _inproj(xa, xb, n, g, w_bf, aqg, akg, dqg, dkg):
    na, nblk, spec_a, spec_b = _two_source_specs(xa, xb, n)
    widths = (256, 128, 128, B_PROJ, C_PROJ, 256, 256, 256)
    row = lambda w: pl.BlockSpec((IN_TM, w), lambda i: (i, 0))
    full = lambda a: pl.BlockSpec(a.shape, lambda i: (0,) * a.ndim)
    return pl.pallas_call(
        functools.partial(_inproj_kernel, n_first=na),
        grid=(nblk,),
        in_specs=[spec_a, spec_b, full(g), full(w_bf), full(aqg), full(akg), full(dqg), full(dkg)],
        out_specs=[row(w) for w in widths],
        out_shape=[jax.ShapeDtypeStruct((n, w), F32) for w in widths],
        compiler_params=_cparams(("parallel",)),
        name="inproj",
    )(xa, xb, g, w_bf, aqg, akg, dqg, dkg)


PACK_W = 256


def _pack_bf16_pairs(hi, lo):
    bits = lambda z: pltpu.bitcast(z.astype(BF16).astype(F32), jnp.int32)
    return bits(hi) | lax.shift_right_logical(bits(lo), jnp.full(lo.shape, 16, jnp.int32))


def _unpack_bf16_pairs(w):
    hi = pltpu.bitcast(w & jnp.int32(-65536), F32)
    lo = pltpu.bitcast(lax.shift_left(w, jnp.full(w.shape, 16, jnp.int32)), F32)
    return hi, lo


def _pack_rows(x):
    return (_pack_bf16_pairs(x[:, 0:PACK_W], x[:, PACK_W:2 * PACK_W]),
            _pack_bf16_pairs(x[:, 2 * PACK_W:3 * PACK_W], x[:, 3 * PACK_W:4 * PACK_W]))


def _unpack_rows(wa, wb):
    return jnp.concatenate(_unpack_bf16_pairs(wa) + _unpack_bf16_pairs(wb), axis=1)


ROUTER_LANES = 128


ROUTER_TERMS = 3


def _router_lanes(router):
    terms = _parts(router.astype(F32), ROUTER_TERMS)
    return jnp.pad(jnp.concatenate(terms, axis=1), ((0, 0), (0, ROUTER_LANES - ROUTER_TERMS * N_EXPERTS)))


def _route(xn_bf, router_bf, counts):
    split = _dg(xn_bf, router_bf)
    logits = split
    for k in range(1, ROUTER_TERMS):
        logits = logits + pltpu.roll(split, ROUTER_LANES - k * N_EXPERTS, 1)
    lane = _iota2(logits.shape, 1)
    logits = jnp.where(lane < N_EXPERTS, logits, NEG_BIG)
    m1 = jnp.max(logits, axis=-1, keepdims=True)
    i1 = jnp.min(jnp.where(logits == m1, lane, ROUTER_LANES), axis=-1, keepdims=True)
    rest = jnp.where(lane == i1, NEG_BIG, logits)
    m2 = jnp.max(rest, axis=-1, keepdims=True)
    i2 = jnp.min(jnp.where(rest == m2, lane, ROUTER_LANES), axis=-1, keepdims=True)
    e2 = jnp.exp(m2 - m1)
    den = 1.0 + e2
    gates = jnp.where(lane == i1, 1.0 / den, 0.0) + jnp.where(lane == i2, e2 / den, 0.0)
    sel = jnp.where(gates > 0.0, 1.0, 0.0)
    tm = sel.shape[0]
    before = jnp.where(_iota2((tm, tm), 1) < _iota2((tm, tm), 0), 1.0, 0.0).astype(BF16)
    rank = (_dg(before, sel.astype(BF16)) + counts).astype(jnp.int32)
    return gates, rank, counts + jnp.sum(sel, axis=0, keepdims=True)


def _mixed_residual(i, xa_ref, xb_ref, mix_refs, w_ref, g_ref, n_first_x, n_first_mix):
    x1 = jnp.where(i < n_first_x, xa_ref[...], xb_ref[...])
    for m in range(4):
        o = jnp.where(i < n_first_mix, mix_refs[2 * m][...], mix_refs[2 * m + 1][...])
        x1 = x1 + jnp.dot(o.astype(BF16), w_ref[m * MIX_W:(m + 1) * MIX_W, :], preferred_element_type=F32)
    ms = jnp.mean(x1 * x1, axis=-1, keepdims=True)
    return x1, (x1 * lax.rsqrt(ms + NORM_EPS)) * g_ref[...]


def _outproj_route_kernel(xa_ref, xb_ref, *refs, n_first_x, n_first_mix):
    mix_refs, (w_ref, g_ref, r_ref), outs = refs[:8], refs[8:11], refs[11:]
    x1_ref, pa_ref, pb_ref, gate_ref, rank_ref, cnt_ref, cnt_scr = outs
    i = pl.program_id(0)
    acc, xn = _mixed_residual(i, xa_ref, xb_ref, mix_refs, w_ref, g_ref, n_first_x, n_first_mix)

    @pl.when(i == 0)
    def _():
        cnt_scr[...] = jnp.zeros_like(cnt_scr)

    x1_ref[...] = acc
    pa_ref[...], pb_ref[...] = _pack_rows(xn)
    gates, rank, counts = _route(xn.astype(BF16), r_ref[...], cnt_scr[...])
    gate_ref[...] = gates
    rank_ref[...] = rank
    cnt_scr[...] = counts
    cnt_ref[...] = counts.astype(jnp.int32)


def _residual_specs(xa, xb, n, mix, index):
    na, nb = xa.shape[0] // IN_TM, xb.shape[0] // IN_TM
    nm_p, nm_s = mix[0][0].shape[0] // IN_TM, mix[0][1].shape[0] // IN_TM
    first = lambda w, cnt: pl.BlockSpec((IN_TM, w), index(lambda i: jnp.minimum(i, cnt - 1)))
    second = lambda w, skip, cnt: pl.BlockSpec((IN_TM, w), index(lambda i: jnp.clip(i - skip, 0, cnt - 1)))
    specs = [first(D_MODEL, na), second(D_MODEL, na, nb)] + [first(MIX_W, nm_p), second(MIX_W, nm_p, nm_s)] * 4
    return specs, [xa, xb, *[a for pair in mix for a in pair]], dict(n_first_x=na, n_first_mix=nm_p)


def _outproj_route(xa, xb, n, mix, w_bf, g2, router_bf):
    specs, operands, statics = _residual_specs(xa, xb, n, mix, lambda blk: (lambda i: (blk(i), 0)))
    row = lambda w: pl.BlockSpec((IN_TM, w), lambda i: (i, 0))
    full = lambda a: pl.BlockSpec(a.shape, lambda i: (0,) * a.ndim)
    return pl.pallas_call(
        functools.partial(_outproj_route_kernel, **statics),
        grid=(n // IN_TM,),
        in_specs=specs + [full(w_bf), full(g2), full(router_bf)],
        out_specs=[row(D_MODEL), row(PACK_W), row(PACK_W), row(ROUTER_LANES), row(ROUTER_LANES),
                   pl.BlockSpec((1, ROUTER_LANES), lambda i: (0, 0))],
        out_shape=[jax.ShapeDtypeStruct((n, D_MODEL), F32),
                   jax.ShapeDtypeStruct((n, PACK_W), jnp.int32), jax.ShapeDtypeStruct((n, PACK_W), jnp.int32),
                   jax.ShapeDtypeStruct((n, ROUTER_LANES), F32), jax.ShapeDtypeStruct((n, ROUTER_LANES), jnp.int32),
                   jax.ShapeDtypeStruct((1, ROUTER_LANES), jnp.int32)],
        scratch_shapes=[pltpu.VMEM((1, ROUTER_LANES), F32)],
        compiler_params=_cparams(("arbitrary",)),
        name="outproj_route",
    )(*operands, w_bf, g2, router_bf)


FFN_TF = 1408


def _outproj_ffn_kernel(xa_ref, xb_ref, *refs, n_first_x, n_first_mix):
    mix_refs, rest = refs[:8], refs[8:]
    w_ref, g_ref, w1_ref, w3_ref, w2_ref, o_ref, x1_scr, xn_scr = rest
    i, f = pl.program_id(0), pl.program_id(1)

    @pl.when(f == 0)
    def _():
        x1, xn = _mixed_residual(i, xa_ref, xb_ref, mix_refs, w_ref, g_ref, n_first_x, n_first_mix)
        x1_scr[...] = x1
        xn_scr[...] = xn.astype(BF16)

    xn = xn_scr[...]
    a = jnp.dot(xn, w1_ref[...], preferred_element_type=F32)
    b = jnp.dot(xn, w3_ref[...], preferred_element_type=F32)
    h = ((a * _sigmoid(a)) * b).astype(BF16)
    y = jnp.dot(h, w2_ref[...], preferred_element_type=F32)

    @pl.when(f == 0)
    def _():
        o_ref[...] = x1_scr[...] + y

    @pl.when(f != 0)
    def _():
        o_ref[...] += y


def _outproj_ffn(xa, xb, n, mix, w_bf, g2, w1_bf, w3_bf, w2_bf):
    specs, operands, statics = _residual_specs(xa, xb, n, mix, lambda blk: (lambda i, f: (blk(i), 0)))
    full = lambda a: pl.BlockSpec(a.shape, lambda i, f: (0,) * a.ndim)
    return pl.pallas_call(
        functools.partial(_outproj_ffn_kernel, **statics),
        grid=(n // IN_TM, D_FF // FFN_TF),
        in_specs=specs + [full(w_bf), full(g2),
                          pl.BlockSpec((D_MODEL, FFN_TF), lambda i, f: (0, f)),
                          pl.BlockSpec((D_MODEL, FFN_TF), lambda i, f: (0, f)),
                          pl.BlockSpec((FFN_TF, D_MODEL), lambda i, f: (f, 0))],
        out_specs=pl.BlockSpec((IN_TM, D_MODEL), lambda i, f: (i, 0)),
        out_shape=jax.ShapeDtypeStruct((n, D_MODEL), F32),
        scratch_shapes=[pltpu.VMEM((IN_TM, D_MODEL), F32), pltpu.VMEM((IN_TM, D_MODEL), BF16)],
        compiler_params=_cparams(("parallel", "arbitrary")),
        name="outproj_ffn",
    )(*operands, w_bf, g2, w1_bf, w3_bf, w2_bf)


MOE_R = 512
MOE_TF = 1792
SC_WINDOW = 128


def _moe_plan(gates, rank, counts, n):
    n_blocks = (2 * n) // MOE_R + N_EXPERTS + 1
    spare_row = (n_blocks - 1) * MOE_R
    sel = gates[:, :N_EXPERTS] > 0.0
    rank = rank[:, :N_EXPERTS]
    counts = counts[0, :N_EXPERTS]
    padded = ((counts + MOE_R - 1) // MOE_R) * MOE_R
    pad_end = jnp.cumsum(padded)
    pad_start = pad_end - padded
    pos = jnp.where(sel, pad_start[None, :] + rank, -1)
    order = jnp.cumsum(sel.astype(jnp.int32), axis=1)
    pick = lambda j: jnp.max(jnp.where(jnp.logical_and(sel, order == j), pos, -1), axis=1)
    to_row = lambda p: jnp.where(p >= 0, p, spare_row).astype(jnp.int32).reshape(1, n)
    block_expert = jnp.minimum(
        jnp.sum(pad_end[None, :] <= (jnp.arange(n_blocks) * MOE_R)[:, None], axis=1), N_EXPERTS - 1)
    return dict(n_blocks=n_blocks, pos0=to_row(pick(1)), pos1=to_row(pick(2)),
                block_expert=block_expert.astype(jnp.int32), n_used=(pad_end[-1:] // MOE_R).astype(jnp.int32))


def _sc_mesh():
    return plsc.VectorSubcoreMesh(core_axis_name="core", subcore_axis_name="subcore")


def _sc_scatter_rows(table, idx_lists, n_rows):
    n, cols = table.shape
    k = len(idx_lists)

    @functools.partial(pl.kernel, out_type=jax.ShapeDtypeStruct((n_rows, cols), table.dtype), mesh=_sc_mesh())
    def scatter(x_hbm, *rest):
        i_hbms, o_hbm = rest[:k], rest[k]

        def body(x_vmem, *i_vmems):
            for i_vmem in i_vmems:
                pltpu.sync_copy(x_vmem, o_hbm.at[i_vmem.at[0]])

        pltpu.emit_pipeline(
            body,
            grid=(n // SC_WINDOW,),
            in_specs=[pl.BlockSpec((SC_WINDOW, cols), lambda i: (i, 0))]
            + [pl.BlockSpec((1, SC_WINDOW), lambda i: (0, i))] * k,
            out_specs=[],
            core_axis_name=("core", "subcore"),
            dimension_semantics=(pltpu.PARALLEL,),
        )(x_hbm, *i_hbms)

    return scatter(table, *idx_lists)


def _sc_gather_rows(table, idx):
    n = idx.shape[1]
    cols = table.shape[1]

    @functools.partial(pl.kernel, out_type=jax.ShapeDtypeStruct((n, cols), table.dtype), mesh=_sc_mesh())
    def gather(x_hbm, i_hbm, o_hbm):
        def body(i_vmem, o_vmem):
            pltpu.sync_copy(x_hbm.at[i_vmem.at[0]], o_vmem)

        pltpu.emit_pipeline(
            body,
            grid=(n // SC_WINDOW,),
            in_specs=[pl.BlockSpec((1, SC_WINDOW), lambda i: (0, i))],
            out_specs=[pl.BlockSpec((SC_WINDOW, cols), lambda i: (i, 0))],
            core_axis_name=("core", "subcore"),
            dimension_semantics=(pltpu.PARALLEL,),
        )(i_hbm, o_hbm)

    return gather(table, idx)


def _moe_expert_kernel(be_ref, nu_ref, xa_ref, xb_ref, gs_ref, w1_ref, w3_ref, w2_ref, oa_ref, ob_ref, acc_ref):
    j, f = pl.program_id(0), pl.program_id(1)
    used = j < nu_ref[0]

    @pl.when(used)
    def _():
        x = _unpack_rows(xa_ref[...], xb_ref[...]).astype(BF16)
        a = jnp.dot(x, w1_ref[...], preferred_element_type=F32)
        b = jnp.dot(x, w3_ref[...], preferred_element_type=F32)
        h = ((a * _sigmoid(a)) * b).astype(BF16)
        y = jnp.dot(h, w2_ref[...], preferred_element_type=F32)

        @pl.when(f == 0)
        def _():
            acc_ref[...] = y

        @pl.when(f != 0)
        def _():
            acc_ref[...] += y

    @pl.when(f == pl.num_programs(1) - 1)
    def _():
        lane = _iota2(gs_ref.shape, 1)
        g = jnp.sum(jnp.where(lane == be_ref[j], gs_ref[...], 0.0), axis=1, keepdims=True)
        oa_ref[...], ob_ref[...] = _pack_rows(jnp.where(used, acc_ref[...] * g, 0.0))


def _moe_experts(plan, xs_a, xs_b, gs, w1_bf, w3_bf, w2_bf):
    n_blocks = plan["n_blocks"]
    half = pl.BlockSpec((MOE_R, PACK_W), lambda j, f, be, nu: (j, 0))
    grid_spec = pltpu.PrefetchScalarGridSpec(
        num_scalar_prefetch=2,
        grid=(n_blocks, E_FF // MOE_TF),
        in_specs=[half, half,
                  pl.BlockSpec((MOE_R, ROUTER_LANES), lambda j, f, be, nu: (j, 0)),
                  pl.BlockSpec((None, D_MODEL, MOE_TF), lambda j, f, be, nu: (be[j], 0, f)),
                  pl.BlockSpec((None, D_MODEL, MOE_TF), lambda j, f, be, nu: (be[j], 0, f)),
                  pl.BlockSpec((None, MOE_TF, D_MODEL), lambda j, f, be, nu: (be[j], f, 0))],
        out_specs=[half, half],
        scratch_shapes=[pltpu.VMEM((MOE_R, D_MODEL), F32)])
    return pl.pallas_call(
        _moe_expert_kernel,
        grid_spec=grid_spec,
        out_shape=[jax.ShapeDtypeStruct((n_blocks * MOE_R, PACK_W), jnp.int32)] * 2,
        compiler_params=_cparams(("arbitrary", "arbitrary")),
        name="moe_experts",
    )(plan["block_expert"], plan["n_used"], xs_a, xs_b, gs, w1_bf, w3_bf, w2_bf)


def _moe_combine_kernel(x1_ref, a0_ref, b0_ref, a1_ref, b1_ref, o_ref):
    o_ref[...] = (x1_ref[...] + _unpack_rows(a0_ref[...], b0_ref[...])) + _unpack_rows(a1_ref[...], b1_ref[...])


def _moe_combine(x1, picked, row0, n_rows):
    base = row0 // IN_TM
    row = lambda w: pl.BlockSpec((IN_TM, w), lambda i: (base + i, 0))
    return pl.pallas_call(
        _moe_combine_kernel,
        grid=(n_rows // IN_TM,),
        in_specs=[row(D_MODEL)] + [row(PACK_W)] * 4,
        out_specs=pl.BlockSpec((IN_TM, D_MODEL), lambda i: (i, 0)),
        out_shape=jax.ShapeDtypeStruct((n_rows, D_MODEL), F32),
        compiler_params=_cparams(("parallel",)),
        name="moe_combine",
    )(x1, *picked)


def _moe(xn_a, xn_b, x1, routing, w1_bf, w3_bf, w2_bf, row_groups):
    gates, rank, counts = routing
    plan = _moe_plan(gates, rank, counts, x1.shape[0])
    n_rows = plan["n_blocks"] * MOE_R
    idx = (plan["pos0"], plan["pos1"])
    xs_a = _sc_scatter_rows(xn_a, idx, n_rows)
    xs_b = _sc_scatter_rows(xn_b, idx, n_rows)
    gs = _sc_scatter_rows(gates, idx, n_rows)
    os_a, os_b = _moe_experts(plan, xs_a, xs_b, gs, w1_bf, w3_bf, w2_bf)
    picked = [_sc_gather_rows(t, p) for p in idx for t in (os_a, os_b)]
    return [_moe_combine(x1, picked, row0, rows) for row0, rows in row_groups]


def _relbias_kernel(rb_ref, o_ref, *, nk):
    h = pl.program_id(0)
    q = _iota2((CHUNK, nk), 0)
    r = _iota2((CHUNK, nk), 1)
    idx = jnp.clip(q - (r - (nk - CHUNK)), -D_REL_CLIP, D_REL_CLIP) + D_REL_CLIP

    def body(j, acc):
        return jnp.where(idx == j, rb_ref[h, j], acc)

    o_ref[...] = lax.fori_loop(0, 2 * D_REL_CLIP + 1, body, jnp.zeros((CHUNK, nk), F32))


def _relbias_table(rel_bias, nk):
    return pl.pallas_call(
        functools.partial(_relbias_kernel, nk=nk),
        grid=(N_HEADS,),
        in_specs=[pl.BlockSpec(memory_space=pltpu.SMEM)],
        out_specs=pl.BlockSpec((CHUNK, nk), lambda h: (h, 0)),
        out_shape=jax.ShapeDtypeStruct((N_HEADS * CHUNK, nk), F32),
        name="relbias",
    )(rel_bias)


ATTN_QB = 16
ATTN_GROUP = 8


def _attn_kernel(q_ref, kp_ref, kc_ref, vp_ref, vc_ref, x_ref, *rest, qb, n_grp, n_prev, use_sink, mask_first):
    o_ref, kbuf, vbuf = rest[-3:]
    i = pl.program_id(1)
    p_rows = kp_ref.shape[-2]
    nk = (n_prev + 1) * CHUNK
    wk = kp_ref.shape[-1]
    grp_rows = qb * CHUNK

    for g in range(n_grp):
        cur = pl.ds(g * grp_rows, grp_rows)
        kbuf[g, 0:p_rows, :] = (kp_ref[g] if len(kp_ref.shape) == 3 else kp_ref[...]).astype(BF16)
        kbuf[g, p_rows:, :] = kc_ref[cur, :].astype(BF16)
        vbuf[g, 0:p_rows, :] = (vp_ref[g] if len(vp_ref.shape) == 3 else vp_ref[...]).astype(BF16)
        vbuf[g, p_rows:, :] = vc_ref[cur, :].astype(BF16)

    hmask = _head_mask(N_HEADS * CHUNK)
    extra = x_ref[...]
    grouped = wk != MIX_W
    low = _iota2((CHUNK, A_KV_W), 1) < HEAD_DIM

    def stack_queries(qj):
        if not grouped:
            return jnp.where(hmask, _tile4(qj), 0.0)
        shifted = pltpu.roll(qj, MIX_W - HEAD_DIM, 1)[:, :A_KV_W]
        return jnp.concatenate([jnp.where(low, qj[:, :A_KV_W], 0.0), jnp.where(low, shifted, 0.0),
                                jnp.where(low, 0.0, shifted), jnp.where(low, 0.0, qj[:, A_KV_W:])], axis=0)

    def unstack_outputs(o_all):
        if not grouped:
            return _fold4(jnp.where(hmask, o_all, 0.0))
        b0, b1, b2, b3 = (o_all[h * CHUNK:(h + 1) * CHUNK] for h in range(N_HEADS))
        left = jnp.where(low, b0, 0.0) + pltpu.roll(jnp.where(low, b1, 0.0), HEAD_DIM, 1)
        right = pltpu.roll(jnp.where(low, 0.0, b2), HEAD_DIM, 1) + jnp.where(low, 0.0, b3)
        return jnp.concatenate([left, right], axis=1)

    def scores(g, j):
        base = p_rows + (j - n_prev) * CHUNK
        qs = stack_queries(q_ref[pl.ds(g * grp_rows + j * CHUNK, CHUNK), :] * ATTN_SCALE).astype(BF16)
        s = _dg(qs, kbuf[g, pl.ds(base, nk), :], NT)
        if not use_sink:
            s = s + extra
        if mask_first and base < p_rows:
            krow = base + _iota2(s.shape, 1)
            s = jnp.where(jnp.logical_and(i == 0, krow < p_rows), NEG_BIG, s)
        return s

    def weights(s):
        m = jnp.max(s, axis=-1, keepdims=True)
        if use_sink:
            m = jnp.maximum(m, extra)
        e = jnp.exp(s - m)
        den = jnp.sum(e, axis=-1, keepdims=True)
        if use_sink:
            den = den + jnp.exp(extra - m)
        return e.astype(BF16), 1.0 / den

    def output(g, j, e, inv_den):
        base = p_rows + (j - n_prev) * CHUNK
        o_all = _dg(e, vbuf[g, pl.ds(base, nk), :]) * inv_den
        o_ref[pl.ds(g * grp_rows + j * CHUNK, CHUNK), :] = unstack_outputs(o_all)

    chunks = [(g, j) for g in range(n_grp) for j in range(qb)]
    for c0 in range(0, len(chunks), ATTN_GROUP):
        group = chunks[c0:c0 + ATTN_GROUP]
        ss = [scores(g, j) for g, j in group]
        ws = [weights(s) for s in ss]
        for (g, j), (e, inv_den) in zip(group, ws):
            output(g, j, e, inv_den)


def _attention(q, k, v, prev, extra, *, n_prev, use_sink, n_streams, t, base_row):
    wk = k.shape[-1]
    if prev is None:
        qb = ATTN_QB
        rows = qb * CHUNK
        nblk = t // rows
        base = base_row // rows
        prev_spec = pl.BlockSpec((rows, wk), lambda s, i: (base + s * nblk + jnp.maximum(i - 1, 0), 0))
        k_prev, v_prev, p_rows, mask_first = k, v, rows, True
        n_grp, n_steps = 1, n_streams
    else:
        n_grp = ATTN_QB // (t // CHUNK)
        assert n_grp >= 1 and n_streams % n_grp == 0
        qb, rows, nblk, n_steps = t // CHUNK, n_grp * t, 1, n_streams // n_grp
        base = base_row // rows
        k_prev, v_prev = prev
        p_rows = k_prev.shape[1]
        prev_spec = pl.BlockSpec((n_grp, p_rows, wk), lambda s, i: (s, 0, 0))
        mask_first = False
    cur = lambda w: pl.BlockSpec((rows, w), lambda s, i: (base + s * nblk + i, 0))
    kern = functools.partial(_attn_kernel, qb=qb, n_grp=n_grp, n_prev=n_prev, use_sink=use_sink,
                             mask_first=mask_first)
    buf = pltpu.VMEM((n_grp, p_rows + rows // n_grp, wk), BF16)
    return pl.pallas_call(
        kern,
        grid=(n_steps, nblk),
        in_specs=[cur(MIX_W), prev_spec, cur(wk), prev_spec, cur(wk),
                  pl.BlockSpec(extra.shape, lambda s, i: (0, 0))],
        out_specs=pl.BlockSpec((rows, MIX_W), lambda s, i: (s * nblk + i, 0)),
        out_shape=jax.ShapeDtypeStruct((n_streams * t, MIX_W), F32),
        scratch_shapes=[buf, buf],
        compiler_params=_cparams(("parallel", "arbitrary")),
        name="attn_sink" if use_sink else "attn_bias",
    )(q, k_prev, k, v_prev, v, extra)


def _head_layer_norm(o, seg_mean_bf, w, b, eps):
    mu = _mm_exact_rhs(o, seg_mean_bf)
    d = o - mu
    var = _mm_exact_rhs(d * d, seg_mean_bf)
    return (d * lax.rsqrt(var + eps)) * w + b


def _ret_kernel(hc_ref, cos_ref, sin_ref, s0_ref, dstack_ref, qsc_ref, ksc_ref, gam_ref, lnw_ref, lnb_ref,
                *rest, qb):
    o_ref, sout_ref, s_scr = rest[-3:]
    i = pl.program_id(1)

    @pl.when(i == 0)
    def _():
        s_scr[...] = _heads_to_block_diag(s0_ref)

    hmask = _head_mask(N_HEADS * CHUNK)
    seg_mean = _seg_matrix(MIX_W, 1.0 / HEAD_DIM)
    rows = qb * CHUNK
    first_half = (_iota2((rows, MIX_W), 1) & (HEAD_DIM - 1)) < (HEAD_DIM // 2)
    cos = jnp.concatenate([cos_ref[...]] * (MIX_W // ROPE_W), axis=1)
    sin = jnp.concatenate([sin_ref[...]] * (MIX_W // ROPE_W), axis=1)

    def rope(x):
        partner = jnp.where(first_half, pltpu.roll(x, MIX_W - HEAD_DIM // 2, 1), pltpu.roll(x, HEAD_DIM // 2, 1))
        return x * cos + partner * sin

    q = rope(hc_ref[:, 0:256])
    k = rope(hc_ref[:, 256:512]) * ATTN_SCALE
    v_bf = hc_ref[:, 512:768].astype(BF16)
    state = s_scr[...]
    outs = []
    for j in range(qb):
        sl = slice(j * CHUNK, (j + 1) * CHUNK)
        qj, kj, vj = q[sl], k[sl], v_bf[sl]
        qs = jnp.where(hmask, _tile4(qj), 0.0).astype(BF16)
        sc = _dg(qs, kj.astype(BF16), NT) * dstack_ref[...]
        intra = _fold4(jnp.where(hmask, _dg(sc.astype(BF16), vj), 0.0))
        inter = _dg((qj * qsc_ref[...]).astype(BF16), state.astype(BF16))
        kv = _dg((kj * ksc_ref[...]).astype(BF16), vj, TN)
        state = gam_ref[...] * state + jnp.where(hmask, kv, 0.0)
        outs.append(intra + inter)
    s_scr[...] = state
    _block_diag_to_heads(state, sout_ref)
    y = _head_layer_norm(jnp.concatenate(outs, axis=0), seg_mean, lnw_ref[...], lnb_ref[...], C_GN_EPS)
    g = hc_ref[:, 768:1024]
    o_ref[...] = y * (g * _sigmoid(g))


def _retention(hc, cos, sin, s0, tabs, lnw, lnb, *, n_streams, t, base_row):
    qb = min(16, t // CHUNK)
    rows = qb * CHUNK
    nblk = t // rows
    base = base_row // rows
    dstack, qsc, ksc, gam = tabs
    full = lambda a: pl.BlockSpec(a.shape, lambda s, i: (0,) * a.ndim)
    cur = lambda w: pl.BlockSpec((rows, w), lambda s, i: (base + s * nblk + i, 0))
    state = pl.BlockSpec((None, N_HEADS, HEAD_DIM, HEAD_DIM), lambda s, i: (s, 0, 0, 0))
    return pl.pallas_call(
        functools.partial(_ret_kernel, qb=qb),
        grid=(n_streams, nblk),
        in_specs=[cur(C_PROJ),
                  pl.BlockSpec((rows, ROPE_W), lambda s, i: (i, 0)),
                  pl.BlockSpec((rows, ROPE_W), lambda s, i: (i, 0)),
                  state, full(dstack), full(qsc), full(ksc), full(gam), full(lnw), full(lnb)],
        out_specs=[pl.BlockSpec((rows, MIX_W), lambda s, i: (s * nblk + i, 0)), state],
        out_shape=[jax.ShapeDtypeStruct((n_streams * t, MIX_W), F32),
                   jax.ShapeDtypeStruct((n_streams, N_HEADS, HEAD_DIM, HEAD_DIM), F32)],
        scratch_shapes=[pltpu.VMEM((MIX_W, MIX_W), F32)],
        compiler_params=_cparams(("parallel", "arbitrary")),
        name="retention",
    )(hc, cos, sin, s0, dstack, qsc, ksc, gam, lnw, lnb)


def _retention_tables():
    gamma = 1.0 - 2.0 ** (-5.0 - np.arange(N_HEADS, dtype=np.float64))
    t = np.arange(CHUNK)
    diff = t[:, None] - t[None, :]
    dmat = np.where(diff >= 0, gamma[:, None, None] ** np.maximum(diff, 0), 0.0)
    dstack = dmat.reshape(N_HEADS * CHUNK, CHUNK)
    lanes = lambda per_head: np.repeat(per_head, HEAD_DIM, axis=-1)
    qsc = lanes(gamma[None, :] ** (t + 1)[:, None])
    ksc = lanes(gamma[None, :] ** (CHUNK - 1 - t)[:, None])
    gam = np.broadcast_to(lanes(gamma ** CHUNK)[:, None], (MIX_W, MIX_W))
    return tuple(jnp.asarray(a, F32) for a in (dstack, qsc, ksc, gam))


ROPE_W = 2 * HEAD_DIM


def _rope_tables(pos):
    half = HEAD_DIM // 2
    theta = np.float32(1.0) / (np.float32(ROPE_BASE) ** np.linspace(0.0, 1.0, half, dtype=np.float32))
    ang = np.asarray(pos, np.float32)[:, None] * theta[None, :]
    cos, sin = np.cos(ang), np.sin(ang)
    reps = ROPE_W // HEAD_DIM
    cos_t = np.tile(np.concatenate([cos, cos], axis=-1), (1, reps))
    sin_t = np.tile(np.concatenate([-sin, sin], axis=-1), (1, reps))
    return jnp.asarray(cos_t, F32), jnp.asarray(sin_t, F32)


DECAY_SCALE = 0.6065306597126334
RWKV_CB = 4
N_LEVELS = 6
MASK_HEAD, MASK_STRICT, MASK_INCL, MASK_LEVEL0 = 0, 1, 2, 3


def _rwkv_masks():
    n4 = N_HEADS * CHUNK
    ri = np.arange(n4)[:, None]
    ci = np.arange(n4)[None, :]
    head = (ri >> 6) == (ci >> 6)
    tabs = [head, head & ((ci & 63) < (ri & 63)), head & ((ci & 63) <= (ri & 63))]
    for log_m in range(N_LEVELS):
        same = (ri >> (log_m + 1)) == (ci >> (log_m + 1))
        tabs.append(same & (((ri >> log_m) & 1) == 1) & (((ci >> log_m) & 1) == 0))
    return jnp.asarray(np.stack(tabs), BF16)


def _rwkv_kernel(*refs, n_hb, n_seg, seg_chunks, carry):
    hb_refs, rest = refs[:n_hb], refs[n_hb:]
    (shift0_ref, h0_ref, masks_ref, mu_ref, w0_ref, w2_ref, a0_ref, a2_ref, g2_ref,
     kk_ref, ka_ref, rk_ref, lnw_ref, lnb_ref, o_ref, hout_ref, h_scr, shift_scr) = rest
    c = pl.program_id(0)
    cb = n_seg * seg_chunks
    seg_rows = seg_chunks * CHUNK
    rows = cb * CHUNK
    xb = jnp.concatenate([ref[...] for ref in hb_refs], axis=0)
    row = _iota2(xb.shape, 0)
    prev = pltpu.roll(xb, 1, 0)
    if carry:
        @pl.when(c == 0)
        def _():
            for s in range(n_seg):
                h_scr[s] = _heads_to_block_diag(h0_ref.at[s])
                shift_scr[s] = shift0_ref[s]

    for s in range(n_seg):
        first = shift_scr[s] if carry else shift0_ref[s]
        prev = jnp.where(row == s * seg_rows, first, prev)
    if carry:
        for s in range(n_seg):
            shift_scr[s] = xb[(s + 1) * seg_rows - 1:(s + 1) * seg_rows, :]
    xs = xb + mu_ref[...] * (prev - xb)
    r = xs[:, 0:256]
    k = xs[:, 256:512]
    v = xs[:, 512:768]
    xw = xs[:, 768:832]
    xa = xs[:, 832:896]
    xg = xs[:, 896:1024]

    z = w0_ref[...] + _mm(jnp.tanh(xw), w2_ref[...], passes=3)
    lw = -DECAY_SCALE * _sigmoid(z)
    a_gate = _sigmoid(a0_ref[...] + _mm(xa, a2_ref[...], passes=3))
    gate = _mm(_sigmoid(xg), g2_ref[...], passes=1)

    seg_sum = _seg_matrix(MIX_W, 1.0)
    seg_mean = _seg_matrix(MIX_W, 1.0 / HEAD_DIM)
    kkn = k * kk_ref[...]
    norm = jnp.sqrt(_mm_exact_rhs(kkn * kkn, seg_sum))
    kk = kkn / jnp.maximum(norm, 1e-12)
    kf = k * (1.0 + (a_gate - 1.0) * ka_ref[...])

    tt = _iota2((rows, rows), 0)
    ss = _iota2((rows, rows), 1)
    tril = jnp.where(jnp.logical_and(ss <= tt, (ss >> 6) == (tt >> 6)), 1.0, 0.0).astype(BF16)
    lw_parts = _parts(lw, 3)
    cum = _dg(tril, lw_parts[0]) + (_dg(tril, lw_parts[1]) + _dg(tril, lw_parts[2]))
    w_inv = jnp.exp(-cum)
    rho = (r * jnp.exp(cum)).astype(BF16)
    alpha = (-kk * jnp.exp(cum - lw)).astype(BF16)
    beta = ((kk * a_gate) * w_inv).astype(BF16)
    kappa = (kf * w_inv).astype(BF16)
    v_bf = v.astype(BF16)

    hmask = masks_ref[MASK_HEAD]
    n4 = N_HEADS * CHUNK
    eye = jnp.where(_iota2((n4, n4), 0) == _iota2((n4, n4), 1), 1.0, 0.0)

    pre, a_bfs, t_invs = [], [], []
    for j in range(cb):
        sl = slice(j * CHUNK, (j + 1) * CHUNK)
        bd = lambda zz: _tile4(zz[sl]) * hmask
        al_bd, be_bd, ka_bd, rh_bd, v_bd = bd(alpha), bd(beta), bd(kappa), bd(rho), bd(v_bf)
        a_bf = _dg(al_bd, be_bd, NT).astype(BF16) * masks_ref[MASK_STRICT]
        a_ak = _dg(al_bd, ka_bd, NT).astype(BF16) * masks_ref[MASK_STRICT]
        b_rb = _dg(rh_bd, be_bd, NT).astype(BF16) * masks_ref[MASK_INCL]
        b_rk = _dg(rh_bd, ka_bd, NT).astype(BF16) * masks_ref[MASK_INCL]
        x0 = _dg(a_ak, v_bd)
        y0 = _dg(b_rk, v_bd)
        sn0 = _dg(v_bd, ka_bd, TN)
        w_chunk = jnp.exp(cum[(j + 1) * CHUNK - 1:(j + 1) * CHUNK, :])
        a_bfs.append(a_bf)
        t_invs.append(eye + (a_bf * masks_ref[MASK_LEVEL0]).astype(F32))
        pre.append([al_bd, be_bd, rh_bd, b_rb, None, x0, y0, sn0, w_chunk])
    for lvl in range(1, N_LEVELS):
        t_bfs = [t.astype(BF16) for t in t_invs]
        e_mats = [_dg(a_bfs[j] * masks_ref[MASK_LEVEL0 + lvl], t_bfs[j]) for j in range(cb)]
        t_invs = [t_invs[j] + _dg(t_bfs[j], e_mats[j].astype(BF16)) for j in range(cb)]
    for j in range(cb):
        pre[j][4] = t_invs[j].astype(BF16)

    h = [h_scr[s] if carry else _heads_to_block_diag(h0_ref.at[s]) for s in range(n_seg)]
    ys = [None] * cb
    for j in range(seg_chunks):
        for s in range(n_seg):
            q = s * seg_chunks + j
            al_bd, be_bd, rh_bd, b_rb, t_bf, x0, y0, sn0, w_chunk = pre[q]
            h0_bf = h[s].astype(BF16)
            x_mat = _dg(al_bd, h0_bf, NT) + x0
            u_bf = _dg(t_bf, x_mat.astype(BF16)).astype(BF16)
            y_bd = _dg(rh_bd, h0_bf, NT) + _dg(b_rb, u_bf) + y0
            h[s] = (h[s] + _dg(u_bf, be_bd, TN) + sn0) * w_chunk
            ys[q] = _fold4(y_bd)
    for s in range(n_seg):
        if carry:
            h_scr[s] = h[s]
        _block_diag_to_heads(h[s], hout_ref.at[s])

    y = _head_layer_norm(jnp.concatenate(ys, axis=0), seg_mean, lnw_ref[...], lnb_ref[...], B_GN_EPS)
    bonus = _mm_exact_rhs(r * kf * rk_ref[...], seg_sum) * v
    out = (y + bonus) * gate
    for s in range(n_seg):
        o_ref[s] = out[s * seg_rows:(s + 1) * seg_rows]


def _rwkv(hb, shift0, h0, params, *, independent, n_streams, t, base_row):
    masks = _rwkv_masks()
    full = lambda a: pl.BlockSpec(a.shape, lambda c: (0,) * a.ndim)
    if independent:
        assert n_streams == 1
        n_state, n_seg, seg_chunks = t // CHUNK, RWKV_CB, 1
        rows = n_seg * CHUNK
        nblk, base = t // rows, base_row // rows
        hb_specs = [pl.BlockSpec((rows, B_PROJ), lambda c: (base + c, 0))]
        st_idx = lambda c: c
    else:
        n_state, n_seg, seg_chunks = n_streams, n_streams, RWKV_CB
        rows = seg_chunks * CHUNK
        nblk, base = t // rows, base_row // rows
        stream_spec = lambda s: pl.BlockSpec((rows, B_PROJ), lambda c: (base + s * nblk + c, 0))
        hb_specs = [stream_spec(s) for s in range(n_streams)]
        st_idx = lambda c: 0
    seg_rows = seg_chunks * CHUNK
    state = pl.BlockSpec((n_seg, N_HEADS, HEAD_DIM, HEAD_DIM), lambda c: (st_idx(c), 0, 0, 0))
    out_idx = (lambda c: (c, 0, 0)) if independent else (lambda c: (0, c, 0))
    o, h_out = pl.pallas_call(
        functools.partial(_rwkv_kernel, n_hb=len(hb_specs), n_seg=n_seg, seg_chunks=seg_chunks,
                          carry=not independent),
        grid=(nblk,),
        in_specs=hb_specs + [pl.BlockSpec((n_seg, 1, B_PROJ), lambda c: (st_idx(c), 0, 0)), state,
                             full(masks)] + [full(p) for p in params],
        out_specs=[pl.BlockSpec((n_seg, seg_rows, MIX_W), out_idx), state],
        out_shape=[jax.ShapeDtypeStruct((n_state, CHUNK if independent else t, MIX_W), F32),
                   jax.ShapeDtypeStruct((n_state, N_HEADS, HEAD_DIM, HEAD_DIM), F32)],
        scratch_shapes=[pltpu.VMEM((n_seg, MIX_W, MIX_W), F32), pltpu.VMEM((n_seg, 1, B_PROJ), F32)],
        compiler_params=_cparams(("arbitrary",)),
        name="rwkv7",
    )(*([hb] * len(hb_specs)), shift0, h0, masks, *params)
    return o.reshape(-1, MIX_W), h_out


def _row(p):
    return p.reshape(1, -1).astype(F32)


def _mixers(proj, caches, lp, tabs, geom):
    aq, ak, av, hb, hc, dq, dk, dv = proj
    bp, tp, bs, ts = geom
    n_p = bp * tp
    ca_k, ca_v, sb_shift, sb_wkv, sc, cd_k, cd_v = caches
    pr = dict(n_streams=bp, t=tp, base_row=0)
    sm = dict(n_streams=bs, t=ts, base_row=n_p)
    zeros_state = jnp.zeros((bp, N_HEADS, HEAD_DIM, HEAD_DIM), F32)

    oa_p = _attention(aq, ak, av, None, lp["sink_col"], n_prev=A_PREV_CHUNKS, use_sink=True, **pr)
    oa_s = _attention(aq, ak, av, (ca_k.reshape(bs, -1, A_KV_W), ca_v.reshape(bs, -1, A_KV_W)), lp["sink_col"],
                      n_prev=A_PREV_CHUNKS, use_sink=True, **sm)

    ob_p, h_p = _rwkv(hb, jnp.zeros((bp, 1, B_PROJ), F32), zeros_state, lp["rwkv"], independent=False, **pr)
    ob_s, h_s = _rwkv(hb, sb_shift.reshape(bs, 1, B_PROJ), sb_wkv.astype(F32), lp["rwkv"], independent=True,
                      n_streams=1, t=bs * ts, base_row=n_p)

    oc_p, s_p = _retention(hc, *tabs["rope_prompt"], zeros_state, tabs["ret"], lp["c_ln_w"], lp["c_ln_b"], **pr)
    oc_s, s_s = _retention(hc, *tabs["rope_sample"], sc.astype(F32), tabs["ret"], lp["c_ln_w"], lp["c_ln_b"], **sm)

    od_p = _attention(dq, dk, dv, None, lp["bias_table"], n_prev=D_PREV_CHUNKS, use_sink=False, **pr)
    od_s = _attention(dq, dk, dv, (cd_k.reshape(bs, -1, MIX_W), cd_v.reshape(bs, -1, MIX_W)), lp["bias_table"],
                      n_prev=D_PREV_CHUNKS, use_sink=False, **sm)

    mix = ((oa_p, oa_s), (ob_p, ob_s), (oc_p, oc_s), (od_p, od_s))
    return mix, (h_p, s_p), (h_s, s_s)


def kernel(x_prompt, x_sample, cache_a_k, cache_a_v, state_b_shift, state_b_wkv, state_c, cache_d_k, cache_d_v,
           norm1_g, norm2_g, w_in, w_out, a_q_norm, a_k_norm, a_sinks, b_mu, b_w0, b_w2, b_a0, b_a2, b_g2,
           b_k_k, b_k_a, b_r_k, b_ln_w, b_ln_b, c_ln_w, c_ln_b, d_q_norm, d_k_norm, d_rel_bias,
           ffn_w1, ffn_w3, ffn_w2, moe_router, moe_w1, moe_w3, moe_w2):
    bp, tp, _ = x_prompt.shape
    bs, ts, _ = x_sample.shape
    assert ts == CHUNK
    n_p, n_s = bp * tp, bs * ts
    geom = (bp, tp, bs, ts)
    xa = x_prompt.reshape(n_p, D_MODEL)
    xb = x_sample.reshape(n_s, D_MODEL)

    tabs = {
        "ret": _retention_tables(),
        "rope_prompt": _rope_tables(np.arange(tp)),
        "rope_sample": _rope_tables(PAST_LEN + np.arange(ts)),
    }
    tile = lambda g: _row(jnp.tile(g, MIX_W // HEAD_DIM))

    p_states, s_states = [], []
    for l in range(DEPTH):
        lp = {
            "sink_col": jnp.repeat(a_sinks[l].astype(F32), CHUNK).reshape(N_HEADS * CHUNK, 1),
            "bias_table": _relbias_table(d_rel_bias[l].astype(F32), (D_PREV_CHUNKS + 1) * CHUNK),
            "rwkv": (_row(b_mu[l]), _row(b_w0[l]), b_w2[l], _row(b_a0[l]), b_a2[l], b_g2[l], _row(b_k_k[l]),
                     _row(b_k_a[l]), _row(b_r_k[l]), _row(b_ln_w[l]), _row(b_ln_b[l])),
            "c_ln_w": _row(c_ln_w[l]), "c_ln_b": _row(c_ln_b[l]),
        }
        proj = _inproj(xa, xb, n_p + n_s, _row(norm1_g[l]), w_in[l].astype(BF16), tile(a_q_norm[l]),
                       _row(jnp.tile(a_k_norm[l], A_KV_W // HEAD_DIM)), tile(d_q_norm[l]), tile(d_k_norm[l]))
        _, ak, av, hb, _, _, dk, dv = proj
        caches = (cache_a_k[l], cache_a_v[l], state_b_shift[l], state_b_wkv[l], state_c[l], cache_d_k[l], cache_d_v[l])
        mix, (wkv_p, ret_p), (wkv_s, ret_s) = _mixers(proj, caches, lp, tabs, geom)
        j = l // 2
        if l % 2 == 0:
            xa = xb = _outproj_ffn(xa, xb, n_p + n_s, mix, w_out[l].astype(BF16), _row(norm2_g[l]),
                                   ffn_w1[j].astype(BF16), ffn_w3[j].astype(BF16), ffn_w2[j].astype(BF16))
        else:
            x1, xn_a, xn_b, *routing = _outproj_route(xa, xb, n_p + n_s, mix, w_out[l].astype(BF16),
                                                      _row(norm2_g[l]), _router_lanes(moe_router[j]))
            groups = ((0, n_p), (n_p, n_s)) if l == DEPTH - 1 else ((0, n_p + n_s),)
            outs = _moe(xn_a, xn_b, x1, routing,
                        moe_w1[j].astype(BF16), moe_w3[j].astype(BF16), moe_w2[j].astype(BF16), groups)
            xa, xb = (outs[0], outs[-1])

        wa = min(A_PREV_CHUNKS * CHUNK, tp)
        wd = min(D_PREV_CHUNKS * CHUNK, tp)
        tail = lambda a, w, heads: jnp.stack(
            [a[(b + 1) * tp - w:(b + 1) * tp] for b in range(bp)]).reshape(bp, w, heads, HEAD_DIM)
        last_rows = lambda a, t, first, count: jnp.concatenate(
            [a[first + (s + 1) * t - 1:first + (s + 1) * t] for s in range(count)], axis=0)
        p_states.append((tail(ak, wa, 2), tail(av, wa, 2), last_rows(hb, tp, 0, bp), wkv_p, ret_p,
                         tail(dk, wd, N_HEADS), tail(dv, wd, N_HEADS)))
        new_rows = lambda a, heads: a[n_p:].reshape(bs, ts, heads, HEAD_DIM)
        s_states.append((new_rows(ak, 2), new_rows(av, 2), last_rows(hb, ts, n_p, bs), wkv_s, ret_s,
                         new_rows(dk, N_HEADS), new_rows(dv, N_HEADS)))

    if xa is xb:
        xa, xb = xa[:n_p], xa[n_p:]
    yp = xa.reshape(bp, tp, D_MODEL)
    ys = xb.reshape(bs, ts, D_MODEL)
    st = lambda group, i: jnp.stack([g[i] for g in group], axis=0)
    roll_in = lambda cache, i: jnp.concatenate([cache.astype(F32), st(s_states, i)], axis=2)[:, :, -cache.shape[2]:]
    return (yp, ys,
            st(p_states, 0), st(p_states, 1), st(p_states, 2), st(p_states, 3), st(p_states, 4), st(p_states, 5), st(p_states, 6),
            roll_in(cache_a_k, 0), roll_in(cache_a_v, 1), st(s_states, 2), st(s_states, 3), st(s_states, 4),
            roll_in(cache_d_k, 5), roll_in(cache_d_v, 6))
```
